```python
import math
import functools
import jax
import jax.numpy as jnp
from jax import lax
import numpy as np

D_MODEL = 1024
BATCH = 16
SEQ = 256
DEPTH = 4
DEC_BATCH = 4
DEC_SEQ = 1024
PAST_LEN = 256

GRID_W = 64
N_MIXERS = 4
N_MLSTM = (DEPTH + 3) // 4
N_CONV = (DEPTH + 2) // 4
N_POOL = (DEPTH + 1) // 4
N_FOURIER = DEPTH // 4
ML_HEADS = 8
ML_DQK = D_MODEL // 16
ML_DHV = D_MODEL // 8
ML_CHUNK = 64
ML_QK = ML_HEADS * ML_DQK
ML_V = ML_HEADS * ML_DHV
ML_IN = 2 * ML_QK + 2 * ML_V
N_GROUPS = 4
GROUP_W = D_MODEL // N_GROUPS
POOL_WINDOWS = (2, 4, 8, 16)
CONV_W = 3
D_FF = 4 * D_MODEL
ALPHA = (2.0 * DEPTH) ** 0.25
BETA = (8.0 * DEPTH) ** -0.25
LN_EPS = 1e-5
F32 = jnp.float32

kernel_name = "hybrid_mlstm_conv_pool_fourier_flow_step"


def layer_norm(x, g, b):
    xf = x.astype(F32)
    mu = jnp.mean(xf, axis=-1, keepdims=True)
    var = jnp.mean(jnp.square(xf - mu), axis=-1, keepdims=True)
    return ((xf - mu) * lax.rsqrt(var + LN_EPS) * g.astype(F32) + b.astype(F32)).astype(x.dtype)


def grid_pos_embed(rows, dtype):
    rr, cc = jnp.meshgrid(jnp.arange(rows, dtype=F32), jnp.arange(GRID_W, dtype=F32), indexing="ij")
    quarter = D_MODEL // 4
    omega = 1.0 / (10000.0 ** (jnp.arange(quarter, dtype=F32) / quarter))
    er = rr.reshape(-1, 1) * omega
    ec = cc.reshape(-1, 1) * omega
    return jnp.concatenate([jnp.sin(er), jnp.cos(er), jnp.sin(ec), jnp.cos(ec)], axis=-1).astype(dtype)


def mlstm_scan(q, k, v, ig, lf, C0, n0, m0):
    B, S, H, _ = q.shape
    L = math.gcd(S, ML_CHUNK)
    NC = S // L
    qc = q.reshape(B, NC, L, H, ML_DQK).transpose(1, 0, 3, 2, 4)
    kc = k.reshape(B, NC, L, H, ML_DQK).transpose(1, 0, 3, 2, 4)
    vc = v.reshape(B, NC, L, H, ML_DHV).transpose(1, 0, 3, 2, 4)
    ic = ig.reshape(B, NC, L, H).transpose(1, 0, 3, 2)
    fc = lf.reshape(B, NC, L, H).transpose(1, 0, 3, 2)
    mask = jnp.tril(jnp.ones((L, L), dtype=bool))

    def step(carry, inp):
        C, n, m = carry
        qq, kk, vv, ii, ff = inp
        b = jnp.cumsum(ff, axis=-1)
        dmat = jnp.where(mask, b[..., :, None] - b[..., None, :] + ii[..., None, :], -jnp.inf)
        inter = b + m[..., None]
        mt = jnp.maximum(inter, jnp.max(dmat, axis=-1))
        s = jnp.einsum("bhtd,bhsd->bhts", qq, kk) * jnp.exp(dmat - mt[..., None])
        a = jnp.exp(inter - mt)
        num = jnp.einsum("bhts,bhsv->bhtv", s, vv) + a[..., None] * jnp.einsum("bhtd,bhdv->bhtv", qq, C)
        den = jnp.sum(s, axis=-1) + a * jnp.einsum("bhtd,bhd->bht", qq, n)
        h = num / jnp.maximum(jnp.abs(den), jnp.exp(-mt))[..., None]
        m_new = mt[..., -1]
        b_last = b[..., -1]
        w = jnp.exp(b_last[..., None] - b + ii - m_new[..., None])
        dec = jnp.exp(b_last + m - m_new)
        C_new = dec[..., None, None] * C + jnp.einsum("bhs,bhsd,bhsv->bhdv", w, kk, vv)
        n_new = dec[..., None] * n + jnp.einsum("bhs,bhsd->bhd", w, kk)
        return (C_new, n_new, m_new), h

    (C, n, m), h = lax.scan(step, (C0, n0, m0), (qc, kc, vc, ic, fc))
    h = h.transpose(1, 0, 3, 2, 4).reshape(B, S, H, ML_DHV)
    return h, C, n, m


def mlstm_mixer(h, w_in, w_gate, b_gate, norm_g, w_out, C0, n0, m0):
    B, S, _ = h.shape
    q, k, v, o = jnp.split(h @ w_in, [ML_QK, 2 * ML_QK, 2 * ML_QK + ML_V], axis=-1)
    q = q.astype(F32).reshape(B, S, ML_HEADS, ML_DQK)
    k = k.astype(F32).reshape(B, S, ML_HEADS, ML_DQK) * (ML_DQK ** -0.5)
    v = v.astype(F32).reshape(B, S, ML_HEADS, ML_DHV)
    g = (h @ w_gate + b_gate).astype(F32).reshape(B, S, 2, 2, ML_HEADS)
    hs, Cs, ns, ms = [], [], [], []
    for d in range(2):
        seqs = (q, k, v, g[:, :, d, 0], jax.nn.log_sigmoid(g[:, :, d, 1]))
        if d == 1:
            seqs = tuple(jnp.flip(a, axis=1) for a in seqs)
        hd, Cd, nd, md = mlstm_scan(*seqs, C0[:, d].astype(F32), n0[:, d].astype(F32), m0[:, d].astype(F32))
        hs.append(jnp.flip(hd, axis=1) if d == 1 else hd)
        Cs.append(Cd)
        ns.append(nd)
        ms.append(md)
    hsum = hs[0] + hs[1]
    mu = jnp.mean(hsum, axis=-1, keepdims=True)
    var = jnp.mean(jnp.square(hsum - mu), axis=-1, keepdims=True)
    hn = (hsum - mu) * lax.rsqrt(var + LN_EPS) * norm_g.astype(F32).reshape(ML_HEADS, ML_DHV)
    y = (hn.reshape(B, S, ML_V) * jax.nn.sigmoid(o.astype(F32))).astype(h.dtype) @ w_out
    return y, jnp.stack(Cs, axis=1), jnp.stack(ns, axis=1), jnp.stack(ms, axis=1)


def shortconv_mixer(h, w_in, conv_w, w_out):
    bg, cg, u = jnp.split(h @ w_in, 3, axis=-1)
    cu = cg * u
    conv = lax.conv_general_dilated(cu, conv_w[:, None, :].astype(cu.dtype), window_strides=(1,),
                                    padding=((1, 1),), dimension_numbers=("NWC", "WIO", "NWC"),
                                    feature_group_count=D_MODEL)
    return (bg * conv) @ w_out


def pool_mixer(h, w, b, scale):
    B, S, _ = h.shape
    hg = h.astype(F32).reshape(B, S, N_GROUPS, GROUP_W)
    cs = jnp.pad(jnp.cumsum(hg, axis=1), ((0, 0), (1, 0), (0, 0), (0, 0)))
    t = jnp.arange(S)
    outs = []
    for gi, win in enumerate(POOL_WINDOWS):
        lo = jnp.clip(t - win // 2, 0, S)
        hi = jnp.clip(t + win - win // 2, 0, S)
        cs_g = cs[:, :, gi]
        mean = (cs_g[:, hi] - cs_g[:, lo]) / (hi - lo).astype(F32)[None, :, None]
        outs.append(mean - hg[:, :, gi])
    p = jnp.stack(outs, axis=2).astype(h.dtype)
    y = jnp.einsum("bsgc,gcd->bsgd", p, w) + b
    return y.reshape(B, S, D_MODEL) * scale


def fourier_mixer(h, w, b):
    B, S, _ = h.shape
    hg = h.astype(F32).reshape(B, S, N_GROUPS, GROUP_W)
    f = jnp.real(jnp.fft.fft2(hg, axes=(1, 3), norm="ortho"))
    return f.reshape(B, S, D_MODEL).astype(h.dtype) @ w + b


def sq_relu_mlp(h, w1, w2):
    return jnp.square(jax.nn.relu(h @ w1)) @ w2


def trunk(x, cond, st_C, st_n, st_m, w_mod, b_mod, ln_g, ln_b, mlp_w1, mlp_w2,
          ml_w_in, ml_w_gate, ml_b_gate, ml_norm_g, ml_w_out,
          sc_w_in, sc_conv_w, sc_w_out, pl_w, pl_b, pl_scale, ft_w_out, ft_b_out):
    s = jax.nn.silu(cond)
    fin_C, fin_n, fin_m = [], [], []
    for i in range(DEPTH):
        kind, j = i % N_MIXERS, i // N_MIXERS
        mod = (s @ w_mod[i] + b_mod[i])[:, None, :]
        sh1, sc1, g1, sh2, sc2, g2 = jnp.split(mod, 6, axis=-1)
        h = x * (1.0 + sc1) + sh1
        if kind == 0:
            y, Cf, nf, mf = mlstm_mixer(h, ml_w_in[j], ml_w_gate[j], ml_b_gate[j], ml_norm_g[j], ml_w_out[j],
                                        st_C[:, j], st_n[:, j], st_m[:, j])
            fin_C.append(Cf)
            fin_n.append(nf)
            fin_m.append(mf)
        elif kind == 1:
            y = shortconv_mixer(h, sc_w_in[j], sc_conv_w[j], sc_w_out[j])
        elif kind == 2:
            y = pool_mixer(h, pl_w[j], pl_b[j], pl_scale[j])
        else:
            y = fourier_mixer(h, ft_w_out[j], ft_b_out[j])
        x = layer_norm(ALPHA * x + g1 * y, ln_g[i, 0], ln_b[i, 0])
        h = x * (1.0 + sc2) + sh2
        x = layer_norm(ALPHA * x + g2 * sq_relu_mlp(h, mlp_w1[i], mlp_w2[i]), ln_g[i, 1], ln_b[i, 1])
    return x, jnp.stack(fin_C, axis=1), jnp.stack(fin_n, axis=1), jnp.stack(fin_m, axis=1)


def setup_inputs(seed: int = 0) -> dict:
    key = jax.random.key(seed)
    ks = jax.random.split(key, 26)
    D = D_MODEL

    def nrm(k, shape, scale):
        return jax.random.normal(k, shape, F32) * scale

    gate_noise = nrm(ks[15], (N_MLSTM, 2, 2, ML_HEADS), 0.1)
    gate_offset = jnp.array([0.0, 3.0], F32)[None, None, :, None]
    return {
        "x_prompt": nrm(ks[0], (BATCH, SEQ, D), 1.0),
        "x_sample": nrm(ks[1], (DEC_BATCH, DEC_SEQ, D), 1.0),
        "state_C": nrm(ks[2], (DEC_BATCH, N_MLSTM, 2, ML_HEADS, ML_DQK, ML_DHV), 0.1),
        "state_n": nrm(ks[3], (DEC_BATCH, N_MLSTM, 2, ML_HEADS, ML_DQK), 0.1),
        "state_m": nrm(ks[4], (DEC_BATCH, N_MLSTM, 2, ML_HEADS), 1.0),
        "c": nrm(ks[5], (DEC_BATCH, D), 1.0),
        "c_ctx": nrm(ks[6], (D,), 1.0),
        "w_mod": nrm(ks[7], (DEPTH, D, 6 * D), 0.5 * D ** -0.5),
        "b_mod": nrm(ks[8], (DEPTH, 6 * D), 0.02),
        "ln_g": 1.0 + nrm(ks[9], (DEPTH, 2, D), 0.02),
        "ln_b": nrm(ks[10], (DEPTH, 2, D), 0.02),
        "mlp_w1": nrm(ks[11], (DEPTH, D, D_FF), D ** -0.5),
        "mlp_w2": nrm(ks[12], (DEPTH, D_FF, D), BETA * D_FF ** -0.5),
        "ml_w_in": nrm(ks[13], (N_MLSTM, D, ML_IN), D ** -0.5),
        "ml_w_gate": nrm(ks[14], (N_MLSTM, D, 4 * ML_HEADS), D ** -0.5),
        "ml_b_gate": (gate_noise + gate_offset).reshape(N_MLSTM, 4 * ML_HEADS),
        "ml_norm_g": 1.0 + nrm(ks[16], (N_MLSTM, ML_V), 0.02),
        "ml_w_out": nrm(ks[17], (N_MLSTM, ML_V, D), BETA * ML_V ** -0.5),
        "sc_w_in": nrm(ks[18], (N_CONV, D, 3 * D), D ** -0.5),
        "sc_conv_w": nrm(ks[19], (N_CONV, CONV_W, D), CONV_W ** -0.5),
        "sc_w_out": nrm(ks[20], (N_CONV, D, D), BETA * D ** -0.5),
        "pl_w": nrm(ks[21], (N_POOL, N_GROUPS, GROUP_W, GROUP_W), BETA * GROUP_W ** -0.5),
        "pl_b": nrm(ks[22], (N_POOL, N_GROUPS, GROUP_W), 0.02),
        "pl_scale": 1.0 + nrm(ks[23], (N_POOL, D), 0.1),
        "ft_w_out": nrm(ks[24], (N_FOURIER, D, D), BETA * D ** -0.5),
        "ft_b_out": nrm(ks[25], (N_FOURIER, D), 0.02),
    }


def reference(x_prompt, x_sample, state_C, state_n, state_m, c, c_ctx,
              w_mod, b_mod, ln_g, ln_b, mlp_w1, mlp_w2,
              ml_w_in, ml_w_gate, ml_b_gate, ml_norm_g, ml_w_out,
              sc_w_in, sc_conv_w, sc_w_out, pl_w, pl_b, pl_scale, ft_w_out, ft_b_out):
    run = functools.partial(trunk, w_mod=w_mod, b_mod=b_mod, ln_g=ln_g, ln_b=ln_b,
                            mlp_w1=mlp_w1, mlp_w2=mlp_w2, ml_w_in=ml_w_in, ml_w_gate=ml_w_gate,
                            ml_b_gate=ml_b_gate, ml_norm_g=ml_norm_g, ml_w_out=ml_w_out,
                            sc_w_in=sc_w_in, sc_conv_w=sc_conv_w, sc_w_out=sc_w_out,
                            pl_w=pl_w, pl_b=pl_b, pl_scale=pl_scale, ft_w_out=ft_w_out, ft_b_out=ft_b_out)
    bp = x_prompt.shape[0]
    zC = jnp.zeros((bp, N_MLSTM, 2, ML_HEADS, ML_DQK, ML_DHV), F32)
    zn = jnp.zeros((bp, N_MLSTM, 2, ML_HEADS, ML_DQK), F32)
    zm = jnp.zeros((bp, N_MLSTM, 2, ML_HEADS), F32)
    y_prompt, new_C, new_n, new_m = run(x_prompt, c_ctx[None, :], zC, zn, zm)
    rows = x_sample.shape[1] // GRID_W
    xs = x_sample + grid_pos_embed(rows, x_sample.dtype)[None]
    y_sample, _, _, _ = run(xs, c, state_C, state_n, state_m)
    return (y_prompt, y_sample, new_C.astype(x_prompt.dtype), new_n.astype(x_prompt.dtype), new_m.astype(x_prompt.dtype))
```

```python
import functools
import math

import numpy as np
import jax
import jax.numpy as jnp
from jax import lax
from jax.experimental import pallas as pl
from jax.experimental.pallas import tpu as pltpu

D = 1024
BATCH, SEQ = 16, 256
DEC_BATCH, DEC_SEQ = 4, 1024
DEPTH = 4
GRID_W = 64
H = 8
DQK = 64
DHV = 128
QK = H * DQK
V = H * DHV
N_GROUPS = 4
GROUP_W = D // N_GROUPS
POOL_WINDOWS = (2, 4, 8, 16)
D_FF = 4 * D
ALPHA = (2.0 * DEPTH) ** 0.25
LN_EPS = 1e-5
F32 = jnp.float32
BF = jnp.bfloat16

NCTX = BATCH * SEQ
NLAT = DEC_BATCH * DEC_SEQ
NTOK = NCTX + NLAT
NCOND = 8
CHUNK = 256
VMEM_LIMIT = 56 * 1024 * 1024


def _cparams(n_axes):
    return pltpu.CompilerParams(dimension_semantics=("arbitrary",) * n_axes,
                                vmem_limit_bytes=VMEM_LIMIT)


def _mod_row(row0):
    return jnp.where(row0 < NCTX, 0, 1 + (row0 - NCTX) // DEC_SEQ)


def _seq_len(row0):
    return jnp.where(row0 < NCTX, SEQ, DEC_SEQ)


def _ln(z, g, b):
    mu = jnp.mean(z, axis=-1, keepdims=True)
    zc = z - mu
    var = jnp.mean(zc * zc, axis=-1, keepdims=True)
    return zc * lax.rsqrt(var + LN_EPS) * g + b


def _dot(a, b):
    return jnp.dot(a, b, preferred_element_type=F32)


def _dot_nt(a, b):
    return lax.dot_general(a, b, (((1,), (1,)), ((), ())), preferred_element_type=F32)


def _split2(x):
    hi = x.astype(BF)
    lo = (x - hi.astype(F32)).astype(BF)
    return hi, lo


def _split3(x):
    hi = x.astype(BF)
    r = x - hi.astype(F32)
    mid = r.astype(BF)
    lo = (r - mid.astype(F32)).astype(BF)
    return hi, mid, lo


def _full(shape):
    n = len(shape)
    return pl.BlockSpec(shape, lambda *_: (0,) * n)


def _mod_spec(layer, tm):
    return pl.BlockSpec((None, None, 6, D), lambda t: (layer, _mod_row(t * tm), 0, 0))


def _mod_kernel(c_ref, w_ref, b_ref, o_ref):
    c = c_ref[...]
    s = c * jax.nn.sigmoid(c)
    o_ref[...] = _dot(s.astype(BF), w_ref[...].astype(BF)) + b_ref[...]


def _modulation(cond, w_mod, b_mod):
    tn = 1536
    out = pl.pallas_call(
        _mod_kernel,
        grid=(DEPTH, 6 * D // tn),
        in_specs=[pl.BlockSpec((NCOND, D), lambda i, j: (0, 0)),
                  pl.BlockSpec((None, D, tn), lambda i, j: (i, 0, j)),
                  pl.BlockSpec((None, 1, tn), lambda i, j: (i, 0, j))],
        out_specs=pl.BlockSpec((None, NCOND, tn), lambda i, j: (i, 0, j)),
        out_shape=jax.ShapeDtypeStruct((DEPTH, NCOND, 6 * D), F32),
        compiler_params=_cparams(2),
        name="modulation",
    )(cond, w_mod, b_mod.reshape(DEPTH, 1, 6 * D))
    return out.reshape(DEPTH, NCOND, 6, D)


MLP_TM = 512
MLP_FCHUNK = 1024


def _mlp_kernel(x_ref, mod_ref, w1_ref, w2_ref, g_ref, b_ref, o_ref):
    x = x_ref[...]
    m = mod_ref[...]
    h = (x * (1.0 + m[4:5]) + m[3:4]).astype(BF)
    acc = jnp.zeros((MLP_TM, D), F32)
    for f in range(D_FF // MLP_FCHUNK):
        lo, hi = f * MLP_FCHUNK, (f + 1) * MLP_FCHUNK
        a = _dot(h, w1_ref[:, lo:hi])
        a = jnp.square(jnp.maximum(a, 0.0)).astype(BF)
        acc = acc + _dot(a, w2_ref[lo:hi, :])
    o_ref[...] = _ln(ALPHA * x + m[5:6] * acc, g_ref[...], b_ref[...])


def _mlp(x, mod, layer, w1, w2, ln_g, ln_b):
    tm = MLP_TM
    return pl.pallas_call(
        _mlp_kernel,
        grid=(NTOK // tm,),
        in_specs=[pl.BlockSpec((tm, D), lambda t: (t, 0)),
                  _mod_spec(layer, tm),
                  pl.BlockSpec((D, D_FF), lambda t: (0, 0), pipeline_mode=pl.Buffered(1)),
                  pl.BlockSpec((D_FF, D), lambda t: (0, 0), pipeline_mode=pl.Buffered(1)),
                  _full((1, D)), _full((1, D))],
        out_specs=pl.BlockSpec((tm, D), lambda t: (t, 0)),
        out_shape=jax.ShapeDtypeStruct((NTOK, D), F32),
        compiler_params=_cparams(1),
        name="mlp",
    )(x, mod, w1, w2, ln_g.reshape(1, D), ln_b.reshape(1, D))


OUT_TM = 512


def _out_kernel(uc_ref, ul_ref, w_ref, bias_ref, x_ref, mod_ref, g_ref, b_ref, o_ref):
    is_ctx = pl.program_id(0) * OUT_TM < NCTX
    u = jnp.where(is_ctx, uc_ref[...], ul_ref[...])
    y = _dot(u, w_ref[...]) + bias_ref[...]
    m = mod_ref[...]
    o_ref[...] = _ln(ALPHA * x_ref[...] + m[2:3] * y, g_ref[...], b_ref[...])


def _out_proj(uc, ul, w, bias, x, mod, layer, ln_g, ln_b):
    tm = OUT_TM
    nc = NCTX // tm
    k = w.shape[0]
    return pl.pallas_call(
        _out_kernel,
        grid=(NTOK // tm,),
        in_specs=[pl.BlockSpec((tm, k), lambda t: (jnp.minimum(t, nc - 1), 0)),
                  pl.BlockSpec((tm, k), lambda t: (jnp.maximum(t - nc, 0), 0)),
                  pl.BlockSpec((k, D), lambda t: (0, 0), pipeline_mode=pl.Buffered(1)),
                  _full((1, D)),
                  pl.BlockSpec((tm, D), lambda t: (t, 0)),
                  _mod_spec(layer, tm),
                  _full((1, D)), _full((1, D))],
        out_specs=pl.BlockSpec((tm, D), lambda t: (t, 0)),
        out_shape=jax.ShapeDtypeStruct((NTOK, D), F32),
        compiler_params=_cparams(1),
        name="out_proj",
    )(uc, ul, w, bias.reshape(1, D), x, mod, ln_g.reshape(1, D), ln_b.reshape(1, D))


ML_TM = 512
N_HP = H // 2
NG = 8


def _log_sigmoid(g):
    return jnp.minimum(g, 0.0) - jnp.log1p(jnp.exp(-jnp.abs(g)))


def _ml_proj_kernel(x_ref, mod_ref, wq_ref, wkt_ref, wv_ref, wo_ref, wg_ref, wgt_ref, bg_ref, bgt_ref,
                    q_ref, kt_ref, v_ref, o_ref, g_ref, gt_ref):
    m = mod_ref[...]
    h = (x_ref[...] * (1.0 + m[1:2]) + m[0:1]).astype(BF)
    q_ref[...] = _dot(h, wq_ref[...]).astype(BF)
    kt_ref[...] = (_dot_nt(wkt_ref[...], h) * (DQK ** -0.5)).astype(BF)
    v_ref[...] = _dot(h, wv_ref[...]).astype(BF)
    o_ref[...] = _dot(h, wo_ref[...])
    g = _dot(h, wg_ref[...]) + bg_ref[...]
    col = lax.broadcasted_iota(jnp.int32, g.shape, 1)
    g = jnp.where((col & 2) != 0, _log_sigmoid(g), g)
    gt = _dot_nt(wgt_ref[...], h) + bgt_ref[...]
    row = lax.broadcasted_iota(jnp.int32, gt.shape, 0)
    gt = jnp.where((row & 2) != 0, _log_sigmoid(gt), gt)
    for hp in range(N_HP):
        g_ref[hp] = g[:, hp * NG:(hp + 1) * NG]
        gt_ref[hp] = gt[hp * NG:(hp + 1) * NG, :]


def _ml_proj(x, mod, layer, wq, wkt, wv, wo, wg, wgt, bg, bgt):
    tm = ML_TM
    res = lambda shape: pl.BlockSpec(shape, lambda t: (0,) * len(shape), pipeline_mode=pl.Buffered(1))
    return pl.pallas_call(
        _ml_proj_kernel,
        grid=(NTOK // tm,),
        in_specs=[pl.BlockSpec((tm, D), lambda t: (t, 0)), _mod_spec(layer, tm),
                  res((D, QK)), res((QK, D)), res((D, V)), res((D, V)),
                  res((D, 4 * NG)), res((4 * NG, D)), res((1, 4 * NG)), res((4 * NG, 1))],
        out_specs=[pl.BlockSpec((tm, QK), lambda t: (t, 0)),
                   pl.BlockSpec((QK, tm), lambda t: (0, t)),
                   pl.BlockSpec((tm, V), lambda t: (t, 0)),
                   pl.BlockSpec((tm, V), lambda t: (t, 0)),
                   pl.BlockSpec((N_HP, tm, NG), lambda t: (0, t, 0)),
                   pl.BlockSpec((N_HP, NG, tm), lambda t: (0, 0, t))],
        out_shape=[jax.ShapeDtypeStruct((NTOK, QK), BF),
                   jax.ShapeDtypeStruct((QK, NTOK), BF),
                   jax.ShapeDtypeStruct((NTOK, V), BF),
                   jax.ShapeDtypeStruct((NTOK, V), F32),
                   jax.ShapeDtypeStruct((N_HP, NTOK, NG), F32),
                   jax.ShapeDtypeStruct((N_HP, NG, NTOK), F32)],
        compiler_params=_cparams(1),
        name="mlstm_proj",
    )(x, mod, wq, wkt, wv, wo, wg, wgt, bg, bgt)


def _scan_chunk(q_c, kt_c, vaug, bc, br, igr, m, caug, d, need_state):
    L = q_c.shape[0]
    rr = igr - br
    ti = lax.broadcasted_iota(jnp.int32, (L, L), 0)
    si = lax.broadcasted_iota(jnp.int32, (L, L), 1)
    mask = (si <= ti) if d == 0 else (si >= ti)
    dm = jnp.where(mask, bc + rr, -jnp.inf)
    inter = bc if m is None else bc + m
    mt = jnp.maximum(inter, jnp.max(dm, axis=1, keepdims=True))
    s = _dot(q_c, kt_c) * jnp.exp(dm - mt)
    num = _dot(s.astype(BF), vaug)[:, :DHV]
    den = jnp.sum(s, axis=1, keepdims=True)
    if caug is not None:
        a = jnp.exp(inter - mt)
        qc = _dot(q_c, caug.astype(BF))
        num = num + a * qc[:, :DHV]
        den = den + a * qc[:, DHV:DHV + 1]
    h = num / jnp.maximum(jnp.abs(den), jnp.exp(-mt))
    if not need_state:
        return h, None, None
    last = L - 1 if d == 0 else 0
    m_new = mt[last:last + 1, :]
    b_last = bc[last:last + 1, :]
    w = jnp.exp(b_last + rr - m_new)
    c_new = _dot((kt_c.astype(F32) * w).astype(BF), vaug)
    if caug is not None:
        c_new = jnp.exp(b_last + m - m_new) * caug + c_new
    return h, c_new, m_new


def _scan_kernel(*refs, is_ctx):
    if is_ctx:
        q_ref, kt_ref, v_ref, o_ref, g_ref, gt_ref, ng_ref, u_ref, cst_ref, mst_ref, hbuf = refs
    else:
        q_ref, kt_ref, v_ref, o_ref, g_ref, gt_ref, ng_ref, c0_ref, m0_ref, u_ref, hbuf = refs
    L = CHUNK
    nch = DEC_SEQ // L
    ti = lax.broadcasted_iota(jnp.int32, (L, L), 0)
    si = lax.broadcasted_iota(jnp.int32, (L, L), 1)
    lower = jnp.where(si <= ti, 1.0, 0.0).astype(BF)
    upper = jnp.where(si >= ti, 1.0, 0.0).astype(BF)
    lane = lax.broadcasted_iota(jnp.int32, (L, DHV), 1)
    ones_col = jnp.where(lane == 0, 1.0, 0.0).astype(BF)

    cols, rows, gts = [], [], []
    for c in range(nch):
        gc = g_ref[c * L:(c + 1) * L, :]
        gtc = gt_ref[:, c * L:(c + 1) * L]
        gparts = _split3(gc)
        tparts = _split3(gtc)
        cols.append((sum(_dot(lower, p) for p in gparts), sum(_dot(upper, p) for p in gparts)))
        rows.append((sum(_dot(p, upper) for p in tparts), sum(_dot(p, lower) for p in tparts)))
        gts.append(gtc)

    for hh in range(2):
        for d in range(2):
            ig_i, lf_i = d * 4 + hh, d * 4 + 2 + hh
            if is_ctx:
                m, caug = None, None
            else:
                m = m0_ref[d, hh][:, 0:1]
                caug = c0_ref[d, hh]
            order = range(nch) if d == 0 else range(nch - 1, -1, -1)
            for n, c in enumerate(order):
                sl = slice(c * L, (c + 1) * L)
                q_c = q_ref[sl, hh * DQK:(hh + 1) * DQK]
                kt_c = kt_ref[hh * DQK:(hh + 1) * DQK, sl]
                vaug = jnp.concatenate([v_ref[sl, hh * DHV:(hh + 1) * DHV], ones_col], axis=1)
                bc = cols[c][d][:, lf_i:lf_i + 1]
                br = rows[c][d][lf_i:lf_i + 1, :]
                igr = gts[c][ig_i:ig_i + 1, :]
                need_state = is_ctx or n < nch - 1
                hcur, c_new, m_new = _scan_chunk(q_c, kt_c, vaug, bc, br, igr, m, caug, d, need_state)
                if d == 0:
                    hbuf[sl, hh * DHV:(hh + 1) * DHV] = hcur
                else:
                    hbuf[sl, hh * DHV:(hh + 1) * DHV] += hcur
                if is_ctx:
                    cst_ref[c, d, hh] = c_new
                    mst_ref[c, d, hh] = jnp.broadcast_to(m_new, (8, 128))
                else:
                    m, caug = m_new, c_new

    for hh in range(2):
        hs = hbuf[:, hh * DHV:(hh + 1) * DHV]
        mu = jnp.mean(hs, axis=-1, keepdims=True)
        hc = hs - mu
        var = jnp.mean(hc * hc, axis=-1, keepdims=True)
        hn = hc * lax.rsqrt(var + LN_EPS) * ng_ref[:, hh * DHV:(hh + 1) * DHV]
        og = jax.nn.sigmoid(o_ref[:, hh * DHV:(hh + 1) * DHV])
        u_ref[:, hh * DHV:(hh + 1) * DHV] = (hn * og).astype(BF)


def _ml_scan(q, kt, v, o, g, gt, norm_g, c0aug=None, m0b=None):
    is_ctx = c0aug is None
    tt = DEC_SEQ
    toff = 0 if is_ctx else NCTX // tt
    nt = (NCTX if is_ctx else NLAT) // tt
    in_specs = [pl.BlockSpec((tt, 2 * DQK), lambda t, p: (toff + t, p)),
                pl.BlockSpec((2 * DQK, tt), lambda t, p: (p, toff + t)),
                pl.BlockSpec((tt, 2 * DHV), lambda t, p: (toff + t, p)),
                pl.BlockSpec((tt, 2 * DHV), lambda t, p: (toff + t, p)),
                pl.BlockSpec((None, tt, NG), lambda t, p: (p, toff + t, 0)),
                pl.BlockSpec((None, NG, tt), lambda t, p: (p, 0, toff + t)),
                pl.BlockSpec((1, 2 * DHV), lambda t, p: (0, p))]
    args = [q, kt, v, o, g, gt, norm_g.reshape(1, V)]
    u_spec = pl.BlockSpec((tt, 2 * DHV), lambda t, p: (t, p))
    u_shape = jax.ShapeDtypeStruct((nt * tt, V), BF)
    if is_ctx:
        nseq = tt // SEQ
        out_specs = [u_spec,
                     pl.BlockSpec((nseq, 2, 2, DQK, 2 * DHV), lambda t, p: (t, 0, p, 0, 0)),
                     pl.BlockSpec((nseq, 2, 2, 8, 128), lambda t, p: (t, 0, p, 0, 0))]
        out_shape = [u_shape,
                     jax.ShapeDtypeStruct((BATCH, 2, H, DQK, 2 * DHV), F32),
                     jax.ShapeDtypeStruct((BATCH, 2, H, 8, 128), F32)]
    else:
        in_specs += [pl.BlockSpec((None, 2, 2, DQK, 2 * DHV), lambda t, p: (t, 0, p, 0, 0)),
                     pl.BlockSpec((None, 2, 2, 1, 128), lambda t, p: (t, 0, p, 0, 0))]
        args += [c0aug, m0b]
        out_specs = [u_spec]
        out_shape = [u_shape]
    return pl.pallas_call(
        functools.partial(_scan_kernel, is_ctx=is_ctx),
        grid=(nt, N_HP),
        in_specs=in_specs,
        out_specs=out_specs,
        out_shape=out_shape,
        scratch_shapes=[pltpu.VMEM((tt, 2 * DHV), F32)],
        compiler_params=_cparams(2),
        name="mlstm_scan_ctx" if is_ctx else "mlstm_scan_lat",
    )(*args)


SC_TM = 512


def _sc_proj_kernel(x_ref, mod_ref, w_ref, bg_ref, cu_ref):
    m = mod_ref[...]
    h = (x_ref[...] * (1.0 + m[1:2]) + m[0:1]).astype(BF)
    bg_ref[...] = _dot(h, w_ref[:, 0:D])
    cu_ref[...] = _dot(h, w_ref[:, D:2 * D]) * _dot(h, w_ref[:, 2 * D:3 * D])


def _sc_proj(x, mod, layer, w_in):
    tm = SC_TM
    return pl.pallas_call(
        _sc_proj_kernel,
        grid=(NTOK // tm,),
        in_specs=[pl.BlockSpec((tm, D), lambda t: (t, 0)), _mod_spec(layer, tm),
                  pl.BlockSpec((D, 3 * D), lambda t: (0, 0), pipeline_mode=pl.Buffered(1))],
        out_specs=[pl.BlockSpec((tm, D), lambda t: (t, 0))] * 2,
        out_shape=[jax.ShapeDtypeStruct((NTOK, D), F32)] * 2,
        compiler_params=_cparams(1),
        name="sconv_proj",
    )(x, mod, w_in)


def _sc_out_kernel(cu_ref, prev_ref, next_ref, bg_ref, cw_ref, w_ref, x_ref, mod_ref, g_ref, b_ref, o_ref):
    tm = SC_TM
    row0 = pl.program_id(0) * tm
    slen = _seq_len(row0)
    loc = lax.broadcasted_iota(jnp.int32, (tm, 1), 0)
    pos = (row0 + loc) & (slen - 1)
    cu = cu_ref[...]
    prev = jnp.where(loc == 0, prev_ref[7:8, :], pltpu.roll(cu, 1, 0))
    prev = jnp.where(pos == 0, 0.0, prev)
    nxt = jnp.where(loc == tm - 1, next_ref[0:1, :], pltpu.roll(cu, tm - 1, 0))
    nxt = jnp.where(pos == slen - 1, 0.0, nxt)
    conv = cw_ref[0:1, :] * prev + cw_ref[1:2, :] * cu + cw_ref[2:3, :] * nxt
    y = _dot((bg_ref[...] * conv).astype(BF), w_ref[...])
    m = mod_ref[...]
    o_ref[...] = _ln(ALPHA * x_ref[...] + m[2:3] * y, g_ref[...], b_ref[...])


def _sc_out(cu, bg, conv_w, w_out, x, mod, layer, ln_g, ln_b):
    tm = SC_TM
    hb = tm // 8
    nhb = NTOK // 8
    return pl.pallas_call(
        _sc_out_kernel,
        grid=(NTOK // tm,),
        in_specs=[pl.BlockSpec((tm, D), lambda t: (t, 0)),
                  pl.BlockSpec((8, D), lambda t: (jnp.maximum(t * hb - 1, 0), 0)),
                  pl.BlockSpec((8, D), lambda t: (jnp.minimum((t + 1) * hb, nhb - 1), 0)),
                  pl.BlockSpec((tm, D), lambda t: (t, 0)),
                  _full((3, D)),
                  pl.BlockSpec((D, D), lambda t: (0, 0), pipeline_mode=pl.Buffered(1)),
                  pl.BlockSpec((tm, D), lambda t: (t, 0)),
                  _mod_spec(layer, tm),
                  _full((1, D)), _full((1, D))],
        out_specs=pl.BlockSpec((tm, D), lambda t: (t, 0)),
        out_shape=jax.ShapeDtypeStruct((NTOK, D), F32),
        compiler_params=_cparams(1),
        name="sconv_out",
    )(cu, cu, cu, bg, conv_w, w_out, x, mod, ln_g.reshape(1, D), ln_b.reshape(1, D))


PL_TM = 256
PL_HALO = 8


def _pool_kernel(x_ref, prev_ref, next_ref, mod_ref, w_ref, pb_ref, ps_ref, g_ref, b_ref, o_ref, e_ref):
    tm = PL_TM
    row0 = pl.program_id(0) * tm
    slen = _seq_len(row0)
    pos0 = row0 & (slen - 1)
    m = mod_ref[...]
    x = x_ref[...]
    h = x * (1.0 + m[1:2]) + m[0:1]
    hp = prev_ref[...] * (1.0 + m[1:2]) + m[0:1]
    hn = next_ref[...] * (1.0 + m[1:2]) + m[0:1]
    e_ref[0:PL_HALO, :] = jnp.where(pos0 != 0, hp, 0.0)
    e_ref[PL_HALO:PL_HALO + tm, :] = h
    e_ref[PL_HALO + tm:, :] = jnp.where(pos0 + tm != slen, hn, 0.0)
    pos = pos0 + lax.broadcasted_iota(jnp.int32, (tm, 1), 0)
    ys = []
    for gi, win in enumerate(POOL_WINDOWS):
        cs = slice(gi * GROUP_W, (gi + 1) * GROUP_W)
        back = win // 2
        tot = e_ref[PL_HALO - back:PL_HALO - back + tm, cs]
        for k in range(1 - back, win - back):
            tot = tot + e_ref[PL_HALO + k:PL_HALO + k + tm, cs]
        cnt = jnp.minimum(pos + (win - back), slen) - jnp.maximum(pos - back, 0)
        p = tot / cnt.astype(F32) - h[:, cs]
        ys.append(_dot(p.astype(BF), w_ref[gi]))
    y = (jnp.concatenate(ys, axis=1) + pb_ref[...]) * ps_ref[...]
    o_ref[...] = _ln(ALPHA * x + m[2:3] * y, g_ref[...], b_ref[...])


def _pool(x, mod, layer, w, pb, ps, ln_g, ln_b):
    tm = PL_TM
    hb = tm // PL_HALO
    nhb = NTOK // PL_HALO
    return pl.pallas_call(
        _pool_kernel,
        grid=(NTOK // tm,),
        in_specs=[pl.BlockSpec((tm, D), lambda t: (t, 0)),
                  pl.BlockSpec((PL_HALO, D), lambda t: (jnp.maximum(t * hb - 1, 0), 0)),
                  pl.BlockSpec((PL_HALO, D), lambda t: (jnp.minimum((t + 1) * hb, nhb - 1), 0)),
                  _mod_spec(layer, tm),
                  _full((N_GROUPS, GROUP_W, GROUP_W)),
                  _full((1, D)), _full((1, D)), _full((1, D)), _full((1, D))],
        out_specs=pl.BlockSpec((tm, D), lambda t: (t, 0)),
        out_shape=jax.ShapeDtypeStruct((NTOK, D), F32),
        scratch_shapes=[pltpu.VMEM((tm + 2 * PL_HALO, D), F32)],
        compiler_params=_cparams(1),
        name="pool",
    )(x, x, x, mod, w, pb.reshape(1, D), ps.reshape(1, D), ln_g.reshape(1, D), ln_b.reshape(1, D))


FT_TK = 256


def _dft_mats(n, scale):
    k = np.arange(n, dtype=np.int64)
    ang = 2.0 * np.pi * ((k[:, None] * k[None, :]) % n).astype(np.float64) / n
    return np.stack([np.cos(ang) * scale, np.sin(ang) * scale]).astype(np.float32)


def _dot_x3(a_parts, b_parts):
    (ah, al), (bh, bl) = a_parts, b_parts
    return _dot(ah, bh) + (_dot(ah, bl) + _dot(al, bh))


def _fourier_kernel(x_ref, mod_ref, cs_ref, cc_ref, o_ref):
    m = mod_ref[...]
    hparts = _split2(x_ref[...] * (1.0 + m[1:2]) + m[0:1])
    ac = _dot_x3(_split2(cs_ref[0]), hparts)
    as_ = _dot_x3(_split2(cs_ref[1]), hparts)
    ccp = _split2(cc_ref[0])
    scp = _split2(cc_ref[1])
    outs = []
    for gi in range(N_GROUPS):
        cs = slice(gi * GROUP_W, (gi + 1) * GROUP_W)
        outs.append(_dot_x3(_split2(ac[:, cs]), ccp) - _dot_x3(_split2(as_[:, cs]), scp))
    o_ref[...] = jnp.concatenate(outs, axis=1).astype(BF)


def _fourier(x, mod, layer, s, nseq, row_off, mod_row_fn):
    tk = FT_TK
    seq_off = row_off // s
    cs = jnp.asarray(_dft_mats(s, s ** -0.5))
    cc = jnp.asarray(_dft_mats(GROUP_W, GROUP_W ** -0.5))
    return pl.pallas_call(
        _fourier_kernel,
        grid=(nseq, s // tk),
        in_specs=[pl.BlockSpec((s, D), lambda b, j: (seq_off + b, 0)),
                  pl.BlockSpec((None, None, 6, D), lambda b, j: (layer, mod_row_fn(b), 0, 0)),
                  pl.BlockSpec((2, tk, s), lambda b, j: (0, j, 0)),
                  _full((2, GROUP_W, GROUP_W))],
        out_specs=pl.BlockSpec((tk, D), lambda b, j: (b * (s // tk) + j, 0)),
        out_shape=jax.ShapeDtypeStruct((nseq * s, D), BF),
        compiler_params=_cparams(2),
        name="fourier_%d" % s,
    )(x, mod, cs, cc)


def _pos_embed():
    rows = DEC_SEQ // GRID_W
    quarter = D // 4
    omega = 1.0 / (10000.0 ** (np.arange(quarter, dtype=np.float64) / quarter))
    rr, cc = np.meshgrid(np.arange(rows, dtype=np.float64), np.arange(GRID_W, dtype=np.float64), indexing="ij")
    er = rr.reshape(-1, 1) * omega
    ec = cc.reshape(-1, 1) * omega
    return np.concatenate([np.sin(er), np.cos(er), np.sin(ec), np.cos(ec)], axis=-1).astype(np.float32)


def _gate_perm():
    perm = np.zeros(4 * H, dtype=np.int32)
    for hp in range(N_HP):
        for d in range(2):
            for j in range(2):
                for hh in range(2):
                    perm[hp * NG + d * 4 + j * 2 + hh] = d * 2 * H + j * H + 2 * hp + hh
    return perm


def kernel(x_prompt, x_sample, state_C, state_n, state_m, c, c_ctx, w_mod, b_mod, ln_g, ln_b, mlp_w1, mlp_w2,
           ml_w_in, ml_w_gate, ml_b_gate, ml_norm_g, ml_w_out, sc_w_in, sc_conv_w, sc_w_out,
           pl_w, pl_b, pl_scale, ft_w_out, ft_b_out):
    cond = jnp.concatenate([c_ctx[None, :], c, jnp.zeros((NCOND - 1 - DEC_BATCH, D), F32)], axis=0)
    mod = _modulation(cond, w_mod, b_mod)

    xs = x_sample + jnp.asarray(_pos_embed())[None]
    x = jnp.concatenate([x_prompt.reshape(NCTX, D), xs.reshape(NLAT, D)], axis=0)

    new_c = new_n = new_m = None
    for i in range(DEPTH):
        kind, j = i % 4, i // 4
        if kind == 0:
            w_in = ml_w_in[j].astype(BF)
            perm = _gate_perm()
            wg = ml_w_gate[j][:, perm].astype(BF)
            bg = ml_b_gate[j][perm]
            q, kt, v, o, g, gt = _ml_proj(
                x, mod, i, w_in[:, :QK], w_in[:, QK:2 * QK].T, w_in[:, 2 * QK:2 * QK + V], w_in[:, 2 * QK + V:],
                wg, wg.T, bg.reshape(1, -1), bg.reshape(-1, 1))
            uc, cst, mst = _ml_scan(q, kt, v, o, g, gt, ml_norm_g[j])
            c0 = jnp.concatenate([state_C[:, j], state_n[:, j][..., None],
                                  jnp.zeros((DEC_BATCH, 2, H, DQK, DHV - 1), F32)], axis=-1)
            m0 = jnp.broadcast_to(state_m[:, j][..., None, None], (DEC_BATCH, 2, H, 1, 128))
            (ul,) = _ml_scan(q, kt, v, o, g, gt, ml_norm_g[j], c0, m0)
            x = _out_proj(uc, ul, ml_w_out[j].astype(BF), jnp.zeros((D,), F32), x, mod, i, ln_g[i, 0], ln_b[i, 0])
            new_c = cst[..., :DHV][:, None]
            new_n = cst[..., DHV][:, None]
            new_m = mst[..., 0, 0][:, None]
        elif kind == 1:
            bgate, cu = _sc_proj(x, mod, i, sc_w_in[j].astype(BF))
            x = _sc_out(cu, bgate, sc_conv_w[j], sc_w_out[j].astype(BF), x, mod, i, ln_g[i, 0], ln_b[i, 0])
        elif kind == 2:
            x = _pool(x, mod, i, pl_w[j].astype(BF), pl_b[j], pl_scale[j], ln_g[i, 0], ln_b[i, 0])
        else:
            fc = _fourier(x, mod, i, SEQ, BATCH, 0, lambda b: 0)
            fl = _fourier(x, mod, i, DEC_SEQ, DEC_BATCH, NCTX, lambda b: 1 + b)
            x = _out_proj(fc, fl, ft_w_out[j].astype(BF), ft_b_out[j], x, mod, i, ln_g[i, 0], ln_b[i, 0])
        x = _mlp(x, mod, i, mlp_w1[i].astype(BF), mlp_w2[i].astype(BF), ln_g[i, 1], ln_b[i, 1])

    y_prompt = x[:NCTX].reshape(BATCH, SEQ, D)
    y_sample = x[NCTX:].reshape(DEC_BATCH, DEC_SEQ, D)
    return y_prompt, y_sample, new_c, new_n, new_m
```

```python
import functools
import math

import numpy as np
import jax
import jax.numpy as jnp
from jax import lax
from jax.experimental import pallas as pl
from jax.experimental.pallas import tpu as pltpu

D = 1024
BATCH, SEQ = 16, 256
DEC_BATCH, DEC_SEQ = 4, 1024
DEPTH = 4
GRID_W = 64
H = 8
DQK = 64
DHV = 128
QK = H * DQK
V = H * DHV
N_GROUPS = 4
GROUP_W = D // N_GROUPS
POOL_WINDOWS = (2, 4, 8, 16)
D_FF = 4 * D
ALPHA = (2.0 * DEPTH) ** 0.25
LN_EPS = 1e-5
F32 = jnp.float32
BF = jnp.bfloat16

NCTX = BATCH * SEQ
NLAT = DEC_BATCH * DEC_SEQ
NTOK = NCTX + NLAT
NCOND = 8
CHUNK = 256
VMEM_LIMIT = 56 * 1024 * 1024


def _cparams(n_axes):
    return pltpu.CompilerParams(dimension_semantics=("arbitrary",) * n_axes,
                                vmem_limit_bytes=VMEM_LIMIT)


def _mod_row(row0):
    return jnp.where(row0 < NCTX, 0, 1 + (row0 - NCTX) // DEC_SEQ)


def _seq_len(row0):
    return jnp.where(row0 < NCTX, SEQ, DEC_SEQ)


def _ln(z, g, b):
    mu = jnp.mean(z, axis=-1, keepdims=True)
    zc = z - mu
    var = jnp.mean(zc * zc, axis=-1, keepdims=True)
    return zc * lax.rsqrt(var + LN_EPS) * g + b


def _dot(a, b):
    return jnp.dot(a, b, preferred_element_type=F32)


def _dot_nt(a, b):
    return lax.dot_general(a, b, (((1,), (1,)), ((), ())), preferred_element_type=F32)


def _split2(x):
    hi = x.astype(BF)
    lo = (x - hi.astype(F32)).astype(BF)
    return hi, lo


def _split3(x):
    hi = x.astype(BF)
    r = x - hi.astype(F32)
    mid = r.astype(BF)
    lo = (r - mid.astype(F32)).astype(BF)
    return hi, mid, lo


def _full(shape):
    n = len(shape)
    return pl.BlockSpec(shape, lambda *_: (0,) * n)


def _x_specs(x, tm):
    if not isinstance(x, tuple):
        return [pl.BlockSpec((tm, D), lambda t: (t, 0))], [x]
    nc = NCTX // tm
    return ([pl.BlockSpec((tm, D), lambda t: (jnp.minimum(t, nc - 1), 0)),
             pl.BlockSpec((tm, D), lambda t: (jnp.maximum(t - nc, 0), 0)),
             pl.BlockSpec((tm, D), lambda t: (t % (DEC_SEQ // tm), 0))], list(x))


def _read_x(x_refs, tm):
    if len(x_refs) == 1:
        return x_refs[0][...]
    xc_ref, xl_ref, pe_ref = x_refs
    return jnp.where(pl.program_id(0) * tm < NCTX, xc_ref[...], xl_ref[...] + pe_ref[...])


def _mod_spec(layer, tm):
    return pl.BlockSpec((None, None, 6, D), lambda t: (layer, _mod_row(t * tm), 0, 0))


def _mod_kernel(c_ref, w_ref, b_ref, o_ref):
    c = c_ref[...]
    s = c * jax.nn.sigmoid(c)
    o_ref[...] = _dot(s.astype(BF), w_ref[...].astype(BF)) + b_ref[...]


def _modulation(cond, w_mod, b_mod):
    tn = 1536
    out = pl.pallas_call(
        _mod_kernel,
        grid=(DEPTH, 6 * D // tn),
        in_specs=[pl.BlockSpec((NCOND, D), lambda i, j: (0, 0)),
                  pl.BlockSpec((None, D, tn), lambda i, j: (i, 0, j)),
                  pl.BlockSpec((None, 1, tn), lambda i, j: (i, 0, j))],
        out_specs=pl.BlockSpec((None, NCOND, tn), lambda i, j: (i, 0, j)),
        out_shape=jax.ShapeDtypeStruct((DEPTH, NCOND, 6 * D), F32),
        compiler_params=_cparams(2),
        name="modulation",
    )(cond, w_mod, b_mod.reshape(DEPTH, 1, 6 * D))
    return out.reshape(DEPTH, NCOND, 6, D)


MLP_TM = 512
MLP_FCHUNK = 1024


def _mlp_kernel(x_ref, mod_ref, w1_ref, w2_ref, g_ref, b_ref, *o_refs):
    x = x_ref[...]
    m = mod_ref[...]
    h = (x * (1.0 + m[4:5]) + m[3:4]).astype(BF)
    acc = jnp.zeros((MLP_TM, D), F32)
    for f in range(D_FF // MLP_FCHUNK):
        lo, hi = f * MLP_FCHUNK, (f + 1) * MLP_FCHUNK
        a = _dot(h, w1_ref[:, lo:hi])
        a = jnp.square(jnp.maximum(a, 0.0)).astype(BF)
        acc = acc + _dot(a, w2_ref[lo:hi, :])
    res = _ln(ALPHA * x + m[5:6] * acc, g_ref[...], b_ref[...])
    if len(o_refs) == 1:
        o_refs[0][...] = res
    else:
        is_ctx = pl.program_id(0) * MLP_TM < NCTX

        @pl.when(is_ctx)
        def _():
            o_refs[0][...] = res

        @pl.when(jnp.logical_not(is_ctx))
        def _():
            o_refs[1][...] = res


def _mlp(x, mod, layer, w1, w2, ln_g, ln_b, split_out=False):
    tm = MLP_TM
    nc = NCTX // tm
    if split_out:
        out_specs = [pl.BlockSpec((tm, D), lambda t: (jnp.minimum(t, nc - 1), 0)),
                     pl.BlockSpec((tm, D), lambda t: (jnp.maximum(t - nc, 0), 0))]
        out_shape = [jax.ShapeDtypeStruct((NCTX, D), F32), jax.ShapeDtypeStruct((NLAT, D), F32)]
    else:
        out_specs = pl.BlockSpec((tm, D), lambda t: (t, 0))
        out_shape = jax.ShapeDtypeStruct((NTOK, D), F32)
    return pl.pallas_call(
        _mlp_kernel,
        grid=(NTOK // tm,),
        in_specs=[pl.BlockSpec((tm, D), lambda t: (t, 0)),
                  _mod_spec(layer, tm),
                  pl.BlockSpec((D, D_FF), lambda t: (0, 0), pipeline_mode=pl.Buffered(1)),
                  pl.BlockSpec((D_FF, D), lambda t: (0, 0), pipeline_mode=pl.Buffered(1)),
                  _full((1, D)), _full((1, D))],
        out_specs=out_specs,
        out_shape=out_shape,
        compiler_params=_cparams(1),
        name="mlp",
    )(x, mod, w1, w2, ln_g.reshape(1, D), ln_b.reshape(1, D))


OUT_TM = 512


def _out_kernel(uc_ref, ul_ref, w_ref, bias_ref, mod_ref, g_ref, b_ref, *refs):
    x_refs, o_ref = refs[:-1], refs[-1]
    is_ctx = pl.program_id(0) * OUT_TM < NCTX
    u = jnp.where(is_ctx, uc_ref[...], ul_ref[...])
    y = _dot(u, w_ref[...]) + bias_ref[...]
    m = mod_ref[...]
    o_ref[...] = _ln(ALPHA * _read_x(x_refs, OUT_TM) + m[2:3] * y, g_ref[...], b_ref[...])


def _out_proj(uc, ul, w, bias, x, mod, layer, ln_g, ln_b):
    tm = OUT_TM
    nc = NCTX // tm
    k = w.shape[0]
    x_specs, x_args = _x_specs(x, tm)
    return pl.pallas_call(
        _out_kernel,
        grid=(NTOK // tm,),
        in_specs=[pl.BlockSpec((tm, k), lambda t: (jnp.minimum(t, nc - 1), 0)),
                  pl.BlockSpec((tm, k), lambda t: (jnp.maximum(t - nc, 0), 0)),
                  pl.BlockSpec((k, D), lambda t: (0, 0), pipeline_mode=pl.Buffered(1)),
                  _full((1, D)),
                  _mod_spec(layer, tm),
                  _full((1, D)), _full((1, D))] + x_specs,
        out_specs=pl.BlockSpec((tm, D), lambda t: (t, 0)),
        out_shape=jax.ShapeDtypeStruct((NTOK, D), F32),
        compiler_params=_cparams(1),
        name="out_proj",
    )(uc, ul, w, bias.reshape(1, D), mod, ln_g.reshape(1, D), ln_b.reshape(1, D), *x_args)


ML_TM = 512
N_HP = H // 2
NG = 8


def _log_sigmoid(g):
    return jnp.minimum(g, 0.0) - jnp.log1p(jnp.exp(-jnp.abs(g)))


def _ml_proj_kernel(mod_ref, wq_ref, wkt_ref, wv_ref, wo_ref, wg_ref, wgt_ref, bg_ref, bgt_ref, *refs):
    x_refs = refs[:-6]
    q_ref, kt_ref, v_ref, o_ref, g_ref, gt_ref = refs[-6:]
    m = mod_ref[...]
    h = (_read_x(x_refs, ML_TM) * (1.0 + m[1:2]) + m[0:1]).astype(BF)
    q_ref[...] = _dot(h, wq_ref[...]).astype(BF)
    kt_ref[...] = (_dot_nt(wkt_ref[...], h) * (DQK ** -0.5)).astype(BF)
    v_ref[...] = _dot(h, wv_ref[...]).astype(BF)
    o_ref[...] = _dot(h, wo_ref[...])
    g = _dot(h, wg_ref[...]) + bg_ref[...]
    col = lax.broadcasted_iota(jnp.int32, g.shape, 1)
    g = jnp.where((col & 2) != 0, _log_sigmoid(g), g)
    gt = _dot_nt(wgt_ref[...], h) + bgt_ref[...]
    row = lax.broadcasted_iota(jnp.int32, gt.shape, 0)
    gt = jnp.where((row & 2) != 0, _log_sigmoid(gt), gt)
    for hp in range(N_HP):
        g_ref[hp] = g[:, hp * NG:(hp + 1) * NG]
        gt_ref[hp] = gt[hp * NG:(hp + 1) * NG, :]


def _ml_proj(x, mod, layer, wq, wkt, wv, wo, wg, wgt, bg, bgt):
    tm = ML_TM
    res = lambda shape: pl.BlockSpec(shape, lambda t: (0,) * len(shape), pipeline_mode=pl.Buffered(1))
    x_specs, x_args = _x_specs(x, tm)
    return pl.pallas_call(
        _ml_proj_kernel,
        grid=(NTOK // tm,),
        in_specs=[_mod_spec(layer, tm),
                  res((D, QK)), res((QK, D)), res((D, V)), res((D, V)),
                  res((D, 4 * NG)), res((4 * NG, D)), res((1, 4 * NG)), res((4 * NG, 1))] + x_specs,
        out_specs=[pl.BlockSpec((tm, QK), lambda t: (t, 0)),
                   pl.BlockSpec((QK, tm), lambda t: (0, t)),
                   pl.BlockSpec((tm, V), lambda t: (t, 0)),
                   pl.BlockSpec((tm, V), lambda t: (t, 0)),
                   pl.BlockSpec((N_HP, tm, NG), lambda t: (0, t, 0)),
                   pl.BlockSpec((N_HP, NG, tm), lambda t: (0, 0, t))],
        out_shape=[jax.ShapeDtypeStruct((NTOK, QK), BF),
                   jax.ShapeDtypeStruct((QK, NTOK), BF),
                   jax.ShapeDtypeStruct((NTOK, V), BF),
                   jax.ShapeDtypeStruct((NTOK, V), F32),
                   jax.ShapeDtypeStruct((N_HP, NTOK, NG), F32),
                   jax.ShapeDtypeStruct((N_HP, NG, NTOK), F32)],
        compiler_params=_cparams(1),
        name="mlstm_proj",
    )(mod, wq, wkt, wv, wo, wg, wgt, bg, bgt, *x_args)


def _scan_chunk(q_c, kt_c, vaug, bc, br, igr, m, caug, d, need_state):
    L = q_c.shape[0]
    rr = igr - br
    ti = lax.broadcasted_iota(jnp.int32, (L, L), 0)
    si = lax.broadcasted_iota(jnp.int32, (L, L), 1)
    mask = (si <= ti) if d == 0 else (si >= ti)
    dm = jnp.where(mask, bc + rr, -jnp.inf)
    inter = bc if m is None else bc + m
    mt = jnp.maximum(inter, jnp.max(dm, axis=1, keepdims=True))
    s = _dot(q_c, kt_c) * jnp.exp(dm - mt)
    num = _dot(s.astype(BF), vaug)[:, :DHV]
    den = jnp.sum(s, axis=1, keepdims=True)
    if caug is not None:
        a = jnp.exp(inter - mt)
        qc = _dot(q_c, caug.astype(BF))
        num = num + a * qc[:, :DHV]
        den = den + a * qc[:, DHV:DHV + 1]
    h = num / jnp.maximum(jnp.abs(den), jnp.exp(-mt))
    if not need_state:
        return h, None, None
    last = L - 1 if d == 0 else 0
    m_new = mt[last:last + 1, :]
    b_last = bc[last:last + 1, :]
    w = jnp.exp(b_last + rr - m_new)
    c_new = _dot((kt_c.astype(F32) * w).astype(BF), vaug)
    if caug is not None:
        c_new = jnp.exp(b_last + m - m_new) * caug + c_new
    return h, c_new, m_new


def _scan_kernel(*refs, is_ctx):
    if is_ctx:
        q_ref, kt_ref, v_ref, o_ref, g_ref, gt_ref, ng_ref, u_ref, cst_ref, mst_ref, hbuf = refs
    else:
        q_ref, kt_ref, v_ref, o_ref, g_ref, gt_ref, ng_ref, c0_ref, m0_ref, u_ref, hbuf = refs
    L = CHUNK
    nch = DEC_SEQ // L
    ti = lax.broadcasted_iota(jnp.int32, (L, L), 0)
    si = lax.broadcasted_iota(jnp.int32, (L, L), 1)
    lower = jnp.where(si <= ti, 1.0, 0.0).astype(BF)
    upper = jnp.where(si >= ti, 1.0, 0.0).astype(BF)
    lane = lax.broadcasted_iota(jnp.int32, (L, DHV), 1)
    ones_col = jnp.where(lane == 0, 1.0, 0.0).astype(BF)

    cols, rows, gts = [], [], []
    for c in range(nch):
        gc = g_ref[c * L:(c + 1) * L, :]
        gtc = gt_ref[:, c * L:(c + 1) * L]
        gparts = _split3(gc)
        tparts = _split3(gtc)
        cols.append((sum(_dot(lower, p) for p in gparts), sum(_dot(upper, p) for p in gparts)))
        rows.append((sum(_dot(p, upper) for p in tparts), sum(_dot(p, lower) for p in tparts)))
        gts.append(gtc)

    for hh in range(2):
        for d in range(2):
            ig_i, lf_i = d * 4 + hh, d * 4 + 2 + hh
            if is_ctx:
                m, caug = None, None
            else:
                m = m0_ref[d, hh][:, 0:1]
                caug = c0_ref[d, hh]
            order = range(nch) if d == 0 else range(nch - 1, -1, -1)
            for n, c in enumerate(order):
                sl = slice(c * L, (c + 1) * L)
                q_c = q_ref[sl, hh * DQK:(hh + 1) * DQK]
                kt_c = kt_ref[hh * DQK:(hh + 1) * DQK, sl]
                vaug = jnp.concatenate([v_ref[sl, hh * DHV:(hh + 1) * DHV], ones_col], axis=1)
                bc = cols[c][d][:, lf_i:lf_i + 1]
                br = rows[c][d][lf_i:lf_i + 1, :]
                igr = gts[c][ig_i:ig_i + 1, :]
                need_state = is_ctx or n < nch - 1
                hcur, c_new, m_new = _scan_chunk(q_c, kt_c, vaug, bc, br, igr, m, caug, d, need_state)
                if d == 0:
                    hbuf[sl, hh * DHV:(hh + 1) * DHV] = hcur
                else:
                    hbuf[sl, hh * DHV:(hh + 1) * DHV] += hcur
                if is_ctx:
                    cst_ref[c, d, hh] = c_new
                    mst_ref[c, d, hh] = jnp.broadcast_to(m_new, (8, 128))
                else:
                    m, caug = m_new, c_new

    for hh in range(2):
        hs = hbuf[:, hh * DHV:(hh + 1) * DHV]
        mu = jnp.mean(hs, axis=-1, keepdims=True)
        hc = hs - mu
        var = jnp.mean(hc * hc, axis=-1, keepdims=True)
        hn = hc * lax.rsqrt(var + LN_EPS) * ng_ref[:, hh * DHV:(hh + 1) * DHV]
        og = jax.nn.sigmoid(o_ref[:, hh * DHV:(hh + 1) * DHV])
        u_ref[:, hh * DHV:(hh + 1) * DHV] = (hn * og).astype(BF)


def _ml_scan(q, kt, v, o, g, gt, norm_g, c0aug=None, m0b=None):
    is_ctx = c0aug is None
    tt = DEC_SEQ
    toff = 0 if is_ctx else NCTX // tt
    nt = (NCTX if is_ctx else NLAT) // tt
    in_specs = [pl.BlockSpec((tt, 2 * DQK), lambda t, p: (toff + t, p)),
                pl.BlockSpec((2 * DQK, tt), lambda t, p: (p, toff + t)),
                pl.BlockSpec((tt, 2 * DHV), lambda t, p: (toff + t, p)),
                pl.BlockSpec((tt, 2 * DHV), lambda t, p: (toff + t, p)),
                pl.BlockSpec((None, tt, NG), lambda t, p: (p, toff + t, 0)),
                pl.BlockSpec((None, NG, tt), lambda t, p: (p, 0, toff + t)),
                pl.BlockSpec((1, 2 * DHV), lambda t, p: (0, p))]
    args = [q, kt, v, o, g, gt, norm_g.reshape(1, V)]
    u_spec = pl.BlockSpec((tt, 2 * DHV), lambda t, p: (t, p))
    u_shape = jax.ShapeDtypeStruct((nt * tt, V), BF)
    if is_ctx:
        nseq = tt // SEQ
        out_specs = [u_spec,
                     pl.BlockSpec((nseq, 2, 2, DQK, 2 * DHV), lambda t, p: (t, 0, p, 0, 0)),
                     pl.BlockSpec((nseq, 2, 2, 8, 128), lambda t, p: (t, 0, p, 0, 0))]
        out_shape = [u_shape,
                     jax.ShapeDtypeStruct((BATCH, 2, H, DQK, 2 * DHV), F32),
                     jax.ShapeDtypeStruct((BATCH, 2, H, 8, 128), F32)]
    else:
        in_specs += [pl.BlockSpec((None, 2, 2, DQK, 2 * DHV), lambda t, p: (t, 0, p, 0, 0)),
                     pl.BlockSpec((None, 2, 2, 1, 128), lambda t, p: (t, 0, p, 0, 0))]
        args += [c0aug, m0b]
        out_specs = [u_spec]
        out_shape = [u_shape]
    return pl.pallas_call(
        functools.partial(_scan_kernel, is_ctx=is_ctx),
        grid=(nt, N_HP),
        in_specs=in_specs,
        out_specs=out_specs,
        out_shape=out_shape,
        scratch_shapes=[pltpu.VMEM((tt, 2 * DHV), F32)],
        compiler_params=_cparams(2),
        name="mlstm_scan_ctx" if is_ctx else "mlstm_scan_lat",
    )(*args)


SC_TM = 512


def _sc_proj_kernel(x_ref, mod_ref, w_ref, bg_ref, cu_ref):
    m = mod_ref[...]
    h = (x_ref[...] * (1.0 + m[1:2]) + m[0:1]).astype(BF)
    bg_ref[...] = _dot(h, w_ref[:, 0:D])
    cu_ref[...] = _dot(h, w_ref[:, D:2 * D]) * _dot(h, w_ref[:, 2 * D:3 * D])


def _sc_proj(x, mod, layer, w_in):
    tm = SC_TM
    return pl.pallas_call(
        _sc_proj_kernel,
        grid=(NTOK // tm,),
        in_specs=[pl.BlockSpec((tm, D), lambda t: (t, 0)), _mod_spec(layer, tm),
                  pl.BlockSpec((D, 3 * D), lambda t: (0, 0), pipeline_mode=pl.Buffered(1))],
        out_specs=[pl.BlockSpec((tm, D), lambda t: (t, 0))] * 2,
        out_shape=[jax.ShapeDtypeStruct((NTOK, D), F32)] * 2,
        compiler_params=_cparams(1),
        name="sconv_proj",
    )(x, mod, w_in)


def _sc_out_kernel(cu_ref, prev_ref, next_ref, bg_ref, cw_ref, w_ref, x_ref, mod_ref, g_ref, b_ref, o_ref):
    tm = SC_TM
    row0 = pl.program_id(0) * tm
    slen = _seq_len(row0)
    loc = lax.broadcasted_iota(jnp.int32, (tm, 1), 0)
    pos = (row0 + loc) & (slen - 1)
    cu = cu_ref[...]
    prev = jnp.where(loc == 0, prev_ref[7:8, :], pltpu.roll(cu, 1, 0))
    prev = jnp.where(pos == 0, 0.0, prev)
    nxt = jnp.where(loc == tm - 1, next_ref[0:1, :], pltpu.roll(cu, tm - 1, 0))
    nxt = jnp.where(pos == slen - 1, 0.0, nxt)
    conv = cw_ref[0:1, :] * prev + cw_ref[1:2, :] * cu + cw_ref[2:3, :] * nxt
    y = _dot((bg_ref[...] * conv).astype(BF), w_ref[...])
    m = mod_ref[...]
    o_ref[...] = _ln(ALPHA * x_ref[...] + m[2:3] * y, g_ref[...], b_ref[...])


def _sc_out(cu, bg, conv_w, w_out, x, mod, layer, ln_g, ln_b):
    tm = SC_TM
    hb = tm // 8
    nhb = NTOK // 8
    return pl.pallas_call(
        _sc_out_kernel,
        grid=(NTOK // tm,),
        in_specs=[pl.BlockSpec((tm, D), lambda t: (t, 0)),
                  pl.BlockSpec((8, D), lambda t: (jnp.maximum(t * hb - 1, 0), 0)),
                  pl.BlockSpec((8, D), lambda t: (jnp.minimum((t + 1) * hb, nhb - 1), 0)),
                  pl.BlockSpec((tm, D), lambda t: (t, 0)),
                  _full((3, D)),
                  pl.BlockSpec((D, D), lambda t: (0, 0), pipeline_mode=pl.Buffered(1)),
                  pl.BlockSpec((tm, D), lambda t: (t, 0)),
                  _mod_spec(layer, tm),
                  _full((1, D)), _full((1, D))],
        out_specs=pl.BlockSpec((tm, D), lambda t: (t, 0)),
        out_shape=jax.ShapeDtypeStruct((NTOK, D), F32),
        compiler_params=_cparams(1),
        name="sconv_out",
    )(cu, cu, cu, bg, conv_w, w_out, x, mod, ln_g.reshape(1, D), ln_b.reshape(1, D))


PL_TM = 256
PL_HALO = 8


def _pool_kernel(x_ref, prev_ref, next_ref, mod_ref, w_ref, pb_ref, ps_ref, g_ref, b_ref, o_ref, e_ref):
    tm = PL_TM
    row0 = pl.program_id(0) * tm
    slen = _seq_len(row0)
    pos0 = row0 & (slen - 1)
    m = mod_ref[...]
    x = x_ref[...]
    h = x * (1.0 + m[1:2]) + m[0:1]
    hp = prev_ref[...] * (1.0 + m[1:2]) + m[0:1]
    hn = next_ref[...] * (1.0 + m[1:2]) + m[0:1]
    e_ref[0:PL_HALO, :] = jnp.where(pos0 != 0, hp, 0.0)
    e_ref[PL_HALO:PL_HALO + tm, :] = h
    e_ref[PL_HALO + tm:, :] = jnp.where(pos0 + tm != slen, hn, 0.0)
    pos = pos0 + lax.broadcasted_iota(jnp.int32, (tm, 1), 0)
    ys = []
    for gi, win in enumerate(POOL_WINDOWS):
        cs = slice(gi * GROUP_W, (gi + 1) * GROUP_W)
        back = win // 2
        tot = e_ref[PL_HALO - back:PL_HALO - back + tm, cs]
        for k in range(1 - back, win - back):
            tot = tot + e_ref[PL_HALO + k:PL_HALO + k + tm, cs]
        cnt = jnp.minimum(pos + (win - back), slen) - jnp.maximum(pos - back, 0)
        p = tot / cnt.astype(F32) - h[:, cs]
        ys.append(_dot(p.astype(BF), w_ref[gi]))
    y = (jnp.concatenate(ys, axis=1) + pb_ref[...]) * ps_ref[...]
    o_ref[...] = _ln(ALPHA * x + m[2:3] * y, g_ref[...], b_ref[...])


def _pool(x, mod, layer, w, pb, ps, ln_g, ln_b):
    tm = PL_TM
    hb = tm // PL_HALO
    nhb = NTOK // PL_HALO
    return pl.pallas_call(
        _pool_kernel,
        grid=(NTOK // tm,),
        in_specs=[pl.BlockSpec((tm, D), lambda t: (t, 0)),
                  pl.BlockSpec((PL_HALO, D), lambda t: (jnp.maximum(t * hb - 1, 0), 0)),
                  pl.BlockSpec((PL_HALO, D), lambda t: (jnp.minimum((t + 1) * hb, nhb - 1), 0)),
                  _mod_spec(layer, tm),
                  _full((N_GROUPS, GROUP_W, GROUP_W)),
                  _full((1, D)), _full((1, D)), _full((1, D)), _full((1, D))],
        out_specs=pl.BlockSpec((tm, D), lambda t: (t, 0)),
        out_shape=jax.ShapeDtypeStruct((NTOK, D), F32),
        scratch_shapes=[pltpu.VMEM((tm + 2 * PL_HALO, D), F32)],
        compiler_params=_cparams(1),
        name="pool",
    )(x, x, x, mod, w, pb.reshape(1, D), ps.reshape(1, D), ln_g.reshape(1, D), ln_b.reshape(1, D))


FT_TK = 256


def _dft_mats(n, scale):
    k = np.arange(n, dtype=np.int64)
    ang = 2.0 * np.pi * ((k[:, None] * k[None, :]) % n).astype(np.float64) / n
    return np.stack([np.cos(ang) * scale, np.sin(ang) * scale]).astype(np.float32)


def _dot_x3(a_parts, b_parts):
    (ah, al), (bh, bl) = a_parts, b_parts
    return _dot(ah, bh) + (_dot(ah, bl) + _dot(al, bh))


def _fourier_kernel(x_ref, mod_ref, cs_ref, cc_ref, o_ref):
    m = mod_ref[...]
    hparts = _split2(x_ref[...] * (1.0 + m[1:2]) + m[0:1])
    ac = _dot_x3(_split2(cs_ref[0]), hparts)
    as_ = _dot_x3(_split2(cs_ref[1]), hparts)
    ccp = _split2(cc_ref[0])
    scp = _split2(cc_ref[1])
    outs = []
    for gi in range(N_GROUPS):
        cs = slice(gi * GROUP_W, (gi + 1) * GROUP_W)
        outs.append(_dot_x3(_split2(ac[:, cs]), ccp) - _dot_x3(_split2(as_[:, cs]), scp))
    o_ref[...] = jnp.concatenate(outs, axis=1).astype(BF)


def _fourier(x, mod, layer, s, nseq, row_off, mod_row_fn):
    tk = FT_TK
    seq_off = row_off // s
    cs = jnp.asarray(_dft_mats(s, s ** -0.5))
    cc = jnp.asarray(_dft_mats(GROUP_W, GROUP_W ** -0.5))
    return pl.pallas_call(
        _fourier_kernel,
        grid=(nseq, s // tk),
        in_specs=[pl.BlockSpec((s, D), lambda b, j: (seq_off + b, 0)),
                  pl.BlockSpec((None, None, 6, D), lambda b, j: (layer, mod_row_fn(b), 0, 0)),
                  pl.BlockSpec((2, tk, s), lambda b, j: (0, j, 0)),
                  _full((2, GROUP_W, GROUP_W))],
        out_specs=pl.BlockSpec((tk, D), lambda b, j: (b * (s // tk) + j, 0)),
        out_shape=jax.ShapeDtypeStruct((nseq * s, D), BF),
        compiler_params=_cparams(2),
        name="fourier_%d" % s,
    )(x, mod, cs, cc)


def _pos_embed():
    rows = DEC_SEQ // GRID_W
    quarter = D // 4
    omega = 1.0 / (10000.0 ** (np.arange(quarter, dtype=np.float64) / quarter))
    rr, cc = np.meshgrid(np.arange(rows, dtype=np.float64), np.arange(GRID_W, dtype=np.float64), indexing="ij")
    er = rr.reshape(-1, 1) * omega
    ec = cc.reshape(-1, 1) * omega
    return np.concatenate([np.sin(er), np.cos(er), np.sin(ec), np.cos(ec)], axis=-1).astype(np.float32)


def _gate_perm():
    perm = np.zeros(4 * H, dtype=np.int32)
    for hp in range(N_HP):
        for d in range(2):
            for j in range(2):
                for hh in range(2):
                    perm[hp * NG + d * 4 + j * 2 + hh] = d * 2 * H + j * H + 2 * hp + hh
    return perm


def kernel(x_prompt, x_sample, state_C, state_n, state_m, c, c_ctx, w_mod, b_mod, ln_g, ln_b, mlp_w1, mlp_w2,
           ml_w_in, ml_w_gate, ml_b_gate, ml_norm_g, ml_w_out, sc_w_in, sc_conv_w, sc_w_out,
           pl_w, pl_b, pl_scale, ft_w_out, ft_b_out):
    cond = jnp.concatenate([c_ctx[None, :], c, jnp.zeros((NCOND - 1 - DEC_BATCH, D), F32)], axis=0)
    mod = _modulation(cond, w_mod, b_mod)

    x = (x_prompt.reshape(NCTX, D), x_sample.reshape(NLAT, D), jnp.asarray(_pos_embed()))

    new_c = new_n = new_m = None
    for i in range(DEPTH):
        kind, j = i % 4, i // 4
        if kind == 0:
            w_in = ml_w_in[j].astype(BF)
            perm = _gate_perm()
            wg = ml_w_gate[j][:, perm].astype(BF)
            bg = ml_b_gate[j][perm]
            q, kt, v, o, g, gt = _ml_proj(
                x, mod, i, w_in[:, :QK], w_in[:, QK:2 * QK].T, w_in[:, 2 * QK:2 * QK + V], w_in[:, 2 * QK + V:],
                wg, wg.T, bg.reshape(1, -1), bg.reshape(-1, 1))
            uc, cst, mst = _ml_scan(q, kt, v, o, g, gt, ml_norm_g[j])
            c0 = jnp.concatenate([state_C[:, j], state_n[:, j][..., None],
                                  jnp.zeros((DEC_BATCH, 2, H, DQK, DHV - 1), F32)], axis=-1)
            m0 = jnp.broadcast_to(state_m[:, j][..., None, None], (DEC_BATCH, 2, H, 1, 128))
            (ul,) = _ml_scan(q, kt, v, o, g, gt, ml_norm_g[j], c0, m0)
            x = _out_proj(uc, ul, ml_w_out[j].astype(BF), jnp.zeros((D,), F32), x, mod, i, ln_g[i, 0], ln_b[i, 0])
            new_c = cst[..., :DHV][:, None]
            new_n = cst[..., DHV][:, None]
            new_m = mst[..., 0, 0][:, None]
        elif kind == 1:
            bgate, cu = _sc_proj(x, mod, i, sc_w_in[j].astype(BF))
            x = _sc_out(cu, bgate, sc_conv_w[j], sc_w_out[j].astype(BF), x, mod, i, ln_g[i, 0], ln_b[i, 0])
        elif kind == 2:
            x = _pool(x, mod, i, pl_w[j].astype(BF), pl_b[j], pl_scale[j], ln_g[i, 0], ln_b[i, 0])
        else:
            fc = _fourier(x, mod, i, SEQ, BATCH, 0, lambda b: 0)
            fl = _fourier(x, mod, i, DEC_SEQ, DEC_BATCH, NCTX, lambda b: 1 + b)
            x = _out_proj(fc, fl, ft_w_out[j].astype(BF), ft_b_out[j], x, mod, i, ln_g[i, 0], ln_b[i, 0])
        x = _mlp(x, mod, i, mlp_w1[i].astype(BF), mlp_w2[i].astype(BF), ln_g[i, 1], ln_b[i, 1],
                 split_out=(i == DEPTH - 1))

    y_prompt = x[0].reshape(BATCH, SEQ, D)
    y_sample = x[1].reshape(DEC_BATCH, DEC_SEQ, D)
    return y_prompt, y_sample, new_c, new_n, new_m
```

```python
import functools
import math

import numpy as np
import jax
import jax.numpy as jnp
from jax import lax
from jax.experimental import pallas as pl
from jax.experimental.pallas import tpu as pltpu

D = 1024
BATCH, SEQ = 16, 256
DEC_BATCH, DEC_SEQ = 4, 1024
DEPTH = 4
GRID_W = 64
H = 8
DQK = 64
DHV = 128
QK = H * DQK
V = H * DHV
N_GROUPS = 4
GROUP_W = D // N_GROUPS
POOL_WINDOWS = (2, 4, 8, 16)
D_FF = 4 * D
ALPHA = (2.0 * DEPTH) ** 0.25
LN_EPS = 1e-5
F32 = jnp.float32
BF = jnp.bfloat16

NCTX = BATCH * SEQ
NLAT = DEC_BATCH * DEC_SEQ
NTOK = NCTX + NLAT
NCOND = 8
CHUNK = 256
VMEM_LIMIT = 56 * 1024 * 1024


def _cparams(n_axes):
    return pltpu.CompilerParams(dimension_semantics=("arbitrary",) * n_axes,
                                vmem_limit_bytes=VMEM_LIMIT)


def _mod_row(row0):
    return jnp.where(row0 < NCTX, 0, 1 + (row0 - NCTX) // DEC_SEQ)


def _seq_len(row0):
    return jnp.where(row0 < NCTX, SEQ, DEC_SEQ)


def _ln(z, g, b):
    mu = jnp.mean(z, axis=-1, keepdims=True)
    zc = z - mu
    var = jnp.mean(zc * zc, axis=-1, keepdims=True)
    return zc * lax.rsqrt(var + LN_EPS) * g + b


def _dot(a, b):
    return jnp.dot(a, b, preferred_element_type=F32)


def _dot_nt(a, b):
    return lax.dot_general(a, b, (((1,), (1,)), ((), ())), preferred_element_type=F32)


def _split2(x):
    hi = x.astype(BF)
    lo = (x - hi.astype(F32)).astype(BF)
    return hi, lo


def _split3(x):
    hi = x.astype(BF)
    r = x - hi.astype(F32)
    mid = r.astype(BF)
    lo = (r - mid.astype(F32)).astype(BF)
    return hi, mid, lo


def _full(shape):
    n = len(shape)
    return pl.BlockSpec(shape, lambda *_: (0,) * n)


def _x_specs(x, tm):
    if not isinstance(x, tuple):
        return [pl.BlockSpec((tm, D), lambda t: (t, 0))], [x]
    nc = NCTX // tm
    return ([pl.BlockSpec((tm, D), lambda t: (jnp.minimum(t, nc - 1), 0)),
             pl.BlockSpec((tm, D), lambda t: (jnp.maximum(t - nc, 0), 0)),
             pl.BlockSpec((tm, D), lambda t: (t % (DEC_SEQ // tm), 0))], list(x))


def _read_x(x_refs, tm):
    if len(x_refs) == 1:
        return x_refs[0][...]
    xc_ref, xl_ref, pe_ref = x_refs
    return jnp.where(pl.program_id(0) * tm < NCTX, xc_ref[...], xl_ref[...] + pe_ref[...])


def _mod_spec(layer, tm):
    return pl.BlockSpec((None, None, 6, D), lambda t: (layer, _mod_row(t * tm), 0, 0))


def _mod_kernel(c_ref, w_ref, b_ref, o_ref):
    c = c_ref[...]
    s = c * jax.nn.sigmoid(c)
    o_ref[...] = _dot(s.astype(BF), w_ref[...].astype(BF)) + b_ref[...]


def _modulation(cond, w_mod, b_mod):
    tn = 1536
    out = pl.pallas_call(
        _mod_kernel,
        grid=(DEPTH, 6 * D // tn),
        in_specs=[pl.BlockSpec((NCOND, D), lambda i, j: (0, 0)),
                  pl.BlockSpec((None, D, tn), lambda i, j: (i, 0, j)),
                  pl.BlockSpec((None, 1, tn), lambda i, j: (i, 0, j))],
        out_specs=pl.BlockSpec((None, NCOND, tn), lambda i, j: (i, 0, j)),
        out_shape=jax.ShapeDtypeStruct((DEPTH, NCOND, 6 * D), F32),
        compiler_params=_cparams(2),
        name="modulation",
    )(cond, w_mod, b_mod.reshape(DEPTH, 1, 6 * D))
    return out.reshape(DEPTH, NCOND, 6, D)


MLP_TM = 512
MLP_FC = 512
MLP_NF = D_FF // MLP_FC
MLP_NT = NTOK // MLP_TM


def _mlp_tile(s):
    return jnp.maximum(s - (MLP_NF - 1), 0)


def _mlp_kernel(x_ref, mod_ref, w1_ref, w2_ref, g_ref, b_ref, *refs):
    o_refs, (w1s_ref, w2s_ref, acc_ref) = refs[:-3], refs[-3:]
    s = pl.program_id(0)
    x = x_ref[...]
    m = mod_ref[...]
    h = (x * (1.0 + m[4:5]) + m[3:4]).astype(BF)

    def hidden(w1c):
        return jnp.square(jnp.maximum(_dot(h, w1c), 0.0)).astype(BF)

    def result(acc):
        return _ln(ALPHA * x + m[5:6] * acc, g_ref[...], b_ref[...])

    @pl.when(s < MLP_NF)
    def _():
        w1c = w1_ref[...].astype(BF)
        w2c = w2_ref[...].astype(BF)
        w1s_ref[s] = w1c
        w2s_ref[s] = w2c
        part = _dot(hidden(w1c), w2c)

        @pl.when(s == 0)
        def _():
            acc_ref[...] = part

        @pl.when(s > 0)
        def _():
            acc_ref[...] += part

        @pl.when(s == MLP_NF - 1)
        def _():
            o_refs[0][...] = result(acc_ref[...])

    @pl.when(s >= MLP_NF)
    def _():
        acc = jnp.zeros((MLP_TM, D), F32)
        for f in range(MLP_NF):
            acc = acc + _dot(hidden(w1s_ref[f]), w2s_ref[f])
        res = result(acc)
        if len(o_refs) == 1:
            o_refs[0][...] = res
        else:
            is_ctx = _mlp_tile(s) * MLP_TM < NCTX

            @pl.when(is_ctx)
            def _():
                o_refs[0][...] = res

            @pl.when(jnp.logical_not(is_ctx))
            def _():
                o_refs[1][...] = res


def _mlp(x, mod, layer, w1, w2, ln_g, ln_b, split_out=False):
    tm = MLP_TM
    nc = NCTX // tm
    if split_out:
        out_specs = [pl.BlockSpec((tm, D), lambda s: (jnp.minimum(_mlp_tile(s), nc - 1), 0)),
                     pl.BlockSpec((tm, D), lambda s: (jnp.maximum(_mlp_tile(s) - nc, 0), 0))]
        out_shape = [jax.ShapeDtypeStruct((NCTX, D), F32), jax.ShapeDtypeStruct((NLAT, D), F32)]
    else:
        out_specs = pl.BlockSpec((tm, D), lambda s: (_mlp_tile(s), 0))
        out_shape = jax.ShapeDtypeStruct((NTOK, D), F32)
    return pl.pallas_call(
        _mlp_kernel,
        grid=(MLP_NF - 1 + MLP_NT,),
        in_specs=[pl.BlockSpec((tm, D), lambda s: (_mlp_tile(s), 0)),
                  pl.BlockSpec((None, None, 6, D), lambda s: (layer, _mod_row(_mlp_tile(s) * tm), 0, 0)),
                  pl.BlockSpec((None, D, MLP_FC), lambda s: (layer, 0, jnp.minimum(s, MLP_NF - 1))),
                  pl.BlockSpec((None, MLP_FC, D), lambda s: (layer, jnp.minimum(s, MLP_NF - 1), 0)),
                  _full((1, D)), _full((1, D))],
        out_specs=out_specs,
        out_shape=out_shape,
        scratch_shapes=[pltpu.VMEM((MLP_NF, D, MLP_FC), BF),
                        pltpu.VMEM((MLP_NF, MLP_FC, D), BF),
                        pltpu.VMEM((tm, D), F32)],
        compiler_params=_cparams(1),
        name="mlp",
    )(x, mod, w1, w2, ln_g.reshape(1, D), ln_b.reshape(1, D))


OUT_TM = 512


def _out_kernel(uc_ref, ul_ref, w_ref, bias_ref, mod_ref, g_ref, b_ref, *refs):
    x_refs, o_ref = refs[:-1], refs[-1]
    is_ctx = pl.program_id(0) * OUT_TM < NCTX
    u = jnp.where(is_ctx, uc_ref[...], ul_ref[...])
    y = _dot(u, w_ref[...]) + bias_ref[...]
    m = mod_ref[...]
    o_ref[...] = _ln(ALPHA * _read_x(x_refs, OUT_TM) + m[2:3] * y, g_ref[...], b_ref[...])


def _out_proj(uc, ul, w, bias, x, mod, layer, ln_g, ln_b):
    tm = OUT_TM
    nc = NCTX // tm
    k = w.shape[0]
    x_specs, x_args = _x_specs(x, tm)
    return pl.pallas_call(
        _out_kernel,
        grid=(NTOK // tm,),
        in_specs=[pl.BlockSpec((tm, k), lambda t: (jnp.minimum(t, nc - 1), 0)),
                  pl.BlockSpec((tm, k), lambda t: (jnp.maximum(t - nc, 0), 0)),
                  pl.BlockSpec((k, D), lambda t: (0, 0), pipeline_mode=pl.Buffered(1)),
                  _full((1, D)),
                  _mod_spec(layer, tm),
                  _full((1, D)), _full((1, D))] + x_specs,
        out_specs=pl.BlockSpec((tm, D), lambda t: (t, 0)),
        out_shape=jax.ShapeDtypeStruct((NTOK, D), F32),
        compiler_params=_cparams(1),
        name="out_proj",
    )(uc, ul, w, bias.reshape(1, D), mod, ln_g.reshape(1, D), ln_b.reshape(1, D), *x_args)


ML_TM = 512
N_HP = H // 2
NG = 8


def _log_sigmoid(g):
    return jnp.minimum(g, 0.0) - jnp.log1p(jnp.exp(-jnp.abs(g)))


def _ml_proj_kernel(mod_ref, wq_ref, wkt_ref, wv_ref, wo_ref, wg_ref, wgt_ref, bg_ref, bgt_ref, *refs):
    x_refs = refs[:-6]
    q_ref, kt_ref, v_ref, o_ref, g_ref, gt_ref = refs[-6:]
    m = mod_ref[...]
    h = (_read_x(x_refs, ML_TM) * (1.0 + m[1:2]) + m[0:1]).astype(BF)
    q_ref[...] = _dot(h, wq_ref[...]).astype(BF)
    kt_ref[...] = (_dot_nt(wkt_ref[...], h) * (DQK ** -0.5)).astype(BF)
    v_ref[...] = _dot(h, wv_ref[...]).astype(BF)
    o_ref[...] = _dot(h, wo_ref[...])
    g = _dot(h, wg_ref[...]) + bg_ref[...]
    col = lax.broadcasted_iota(jnp.int32, g.shape, 1)
    g = jnp.where((col & 2) != 0, _log_sigmoid(g), g)
    gt = _dot_nt(wgt_ref[...], h) + bgt_ref[...]
    row = lax.broadcasted_iota(jnp.int32, gt.shape, 0)
    gt = jnp.where((row & 2) != 0, _log_sigmoid(gt), gt)
    for hp in range(N_HP):
        g_ref[hp] = g[:, hp * NG:(hp + 1) * NG]
        gt_ref[hp] = gt[hp * NG:(hp + 1) * NG, :]


def _ml_proj(x, mod, layer, wq, wkt, wv, wo, wg, wgt, bg, bgt):
    tm = ML_TM
    res = lambda shape: pl.BlockSpec(shape, lambda t: (0,) * len(shape), pipeline_mode=pl.Buffered(1))
    x_specs, x_args = _x_specs(x, tm)
    return pl.pallas_call(
        _ml_proj_kernel,
        grid=(NTOK // tm,),
        in_specs=[_mod_spec(layer, tm),
                  res((D, QK)), res((QK, D)), res((D, V)), res((D, V)),
                  res((D, 4 * NG)), res((4 * NG, D)), res((1, 4 * NG)), res((4 * NG, 1))] + x_specs,
        out_specs=[pl.BlockSpec((tm, QK), lambda t: (t, 0)),
                   pl.BlockSpec((QK, tm), lambda t: (0, t)),
                   pl.BlockSpec((tm, V), lambda t: (t, 0)),
                   pl.BlockSpec((tm, V), lambda t: (t, 0)),
                   pl.BlockSpec((N_HP, tm, NG), lambda t: (0, t, 0)),
                   pl.BlockSpec((N_HP, NG, tm), lambda t: (0, 0, t))],
        out_shape=[jax.ShapeDtypeStruct((NTOK, QK), BF),
                   jax.ShapeDtypeStruct((QK, NTOK), BF),
                   jax.ShapeDtypeStruct((NTOK, V), BF),
                   jax.ShapeDtypeStruct((NTOK, V), F32),
                   jax.ShapeDtypeStruct((N_HP, NTOK, NG), F32),
                   jax.ShapeDtypeStruct((N_HP, NG, NTOK), F32)],
        compiler_params=_cparams(1),
        name="mlstm_proj",
    )(mod, wq, wkt, wv, wo, wg, wgt, bg, bgt, *x_args)


def _scan_chunk(q_c, kt_c, vaug, bc, br, igr, m, caug, d, need_state):
    L = q_c.shape[0]
    rr = igr - br
    ti = lax.broadcasted_iota(jnp.int32, (L, L), 0)
    si = lax.broadcasted_iota(jnp.int32, (L, L), 1)
    mask = (si <= ti) if d == 0 else (si >= ti)
    dm = jnp.where(mask, bc + rr, -jnp.inf)
    inter = bc if m is None else bc + m
    mt = jnp.maximum(inter, jnp.max(dm, axis=1, keepdims=True))
    s = _dot(q_c, kt_c) * jnp.exp(dm - mt)
    num = _dot(s.astype(BF), vaug)[:, :DHV]
    den = jnp.sum(s, axis=1, keepdims=True)
    if caug is not None:
        a = jnp.exp(inter - mt)
        qc = _dot(q_c, caug.astype(BF))
        num = num + a * qc[:, :DHV]
        den = den + a * qc[:, DHV:DHV + 1]
    h = num / jnp.maximum(jnp.abs(den), jnp.exp(-mt))
    if not need_state:
        return h, None, None
    last = L - 1 if d == 0 else 0
    m_new = mt[last:last + 1, :]
    b_last = bc[last:last + 1, :]
    w = jnp.exp(b_last + rr - m_new)
    c_new = _dot((kt_c.astype(F32) * w).astype(BF), vaug)
    if caug is not None:
        c_new = jnp.exp(b_last + m - m_new) * caug + c_new
    return h, c_new, m_new


def _scan_kernel(*refs, is_ctx):
    if is_ctx:
        q_ref, kt_ref, v_ref, o_ref, g_ref, gt_ref, ng_ref, u_ref, cst_ref, mst_ref, hbuf = refs
    else:
        q_ref, kt_ref, v_ref, o_ref, g_ref, gt_ref, ng_ref, c0_ref, m0_ref, u_ref, hbuf = refs
    L = CHUNK
    nch = DEC_SEQ // L
    ti = lax.broadcasted_iota(jnp.int32, (L, L), 0)
    si = lax.broadcasted_iota(jnp.int32, (L, L), 1)
    lower = jnp.where(si <= ti, 1.0, 0.0).astype(BF)
    upper = jnp.where(si >= ti, 1.0, 0.0).astype(BF)
    lane = lax.broadcasted_iota(jnp.int32, (L, DHV), 1)
    ones_col = jnp.where(lane == 0, 1.0, 0.0).astype(BF)

    cols, rows, gts = [], [], []
    for c in range(nch):
        gc = g_ref[c * L:(c + 1) * L, :]
        gtc = gt_ref[:, c * L:(c + 1) * L]
        gparts = _split3(gc)
        tparts = _split3(gtc)
        cols.append((sum(_dot(lower, p) for p in gparts), sum(_dot(upper, p) for p in gparts)))
        rows.append((sum(_dot(p, upper) for p in tparts), sum(_dot(p, lower) for p in tparts)))
        gts.append(gtc)

    for hh in range(2):
        for d in range(2):
            ig_i, lf_i = d * 4 + hh, d * 4 + 2 + hh
            if is_ctx:
                m, caug = None, None
            else:
                m = m0_ref[d, hh][:, 0:1]
                caug = c0_ref[d, hh]
            order = range(nch) if d == 0 else range(nch - 1, -1, -1)
            for n, c in enumerate(order):
                sl = slice(c * L, (c + 1) * L)
                q_c = q_ref[sl, hh * DQK:(hh + 1) * DQK]
                kt_c = kt_ref[hh * DQK:(hh + 1) * DQK, sl]
                vaug = jnp.concatenate([v_ref[sl, hh * DHV:(hh + 1) * DHV], ones_col], axis=1)
                bc = cols[c][d][:, lf_i:lf_i + 1]
                br = rows[c][d][lf_i:lf_i + 1, :]
                igr = gts[c][ig_i:ig_i + 1, :]
                need_state = is_ctx or n < nch - 1
                hcur, c_new, m_new = _scan_chunk(q_c, kt_c, vaug, bc, br, igr, m, caug, d, need_state)
                if d == 0:
                    hbuf[sl, hh * DHV:(hh + 1) * DHV] = hcur
                else:
                    hbuf[sl, hh * DHV:(hh + 1) * DHV] += hcur
                if is_ctx:
                    cst_ref[c, d, hh] = c_new
                    mst_ref[c, d, hh] = jnp.broadcast_to(m_new, (8, 128))
                else:
                    m, caug = m_new, c_new

    for hh in range(2):
        hs = hbuf[:, hh * DHV:(hh + 1) * DHV]
        mu = jnp.mean(hs, axis=-1, keepdims=True)
        hc = hs - mu
        var = jnp.mean(hc * hc, axis=-1, keepdims=True)
        hn = hc * lax.rsqrt(var + LN_EPS) * ng_ref[:, hh * DHV:(hh + 1) * DHV]
        og = jax.nn.sigmoid(o_ref[:, hh * DHV:(hh + 1) * DHV])
        u_ref[:, hh * DHV:(hh + 1) * DHV] = (hn * og).astype(BF)


def _ml_scan(q, kt, v, o, g, gt, norm_g, c0aug=None, m0b=None):
    is_ctx = c0aug is None
    tt = DEC_SEQ
    toff = 0 if is_ctx else NCTX // tt
    nt = (NCTX if is_ctx else NLAT) // tt
    in_specs = [pl.BlockSpec((tt, 2 * DQK), lambda t, p: (toff + t, p)),
                pl.BlockSpec((2 * DQK, tt), lambda t, p: (p, toff + t)),
                pl.BlockSpec((tt, 2 * DHV), lambda t, p: (toff + t, p)),
                pl.BlockSpec((tt, 2 * DHV), lambda t, p: (toff + t, p)),
                pl.BlockSpec((None, tt, NG), lambda t, p: (p, toff + t, 0)),
                pl.BlockSpec((None, NG, tt), lambda t, p: (p, 0, toff + t)),
                pl.BlockSpec((1, 2 * DHV), lambda t, p: (0, p))]
    args = [q, kt, v, o, g, gt, norm_g.reshape(1, V)]
    u_spec = pl.BlockSpec((tt, 2 * DHV), lambda t, p: (t, p))
    u_shape = jax.ShapeDtypeStruct((nt * tt, V), BF)
    if is_ctx:
        nseq = tt // SEQ
        out_specs = [u_spec,
                     pl.BlockSpec((nseq, 2, 2, DQK, 2 * DHV), lambda t, p: (t, 0, p, 0, 0)),
                     pl.BlockSpec((nseq, 2, 2, 8, 128), lambda t, p: (t, 0, p, 0, 0))]
        out_shape = [u_shape,
                     jax.ShapeDtypeStruct((BATCH, 2, H, DQK, 2 * DHV), F32),
                     jax.ShapeDtypeStruct((BATCH, 2, H, 8, 128), F32)]
    else:
        in_specs += [pl.BlockSpec((None, 2, 2, DQK, 2 * DHV), lambda t, p: (t, 0, p, 0, 0)),
                     pl.BlockSpec((None, 2, 2, 1, 128), lambda t, p: (t, 0, p, 0, 0))]
        args += [c0aug, m0b]
        out_specs = [u_spec]
        out_shape = [u_shape]
    return pl.pallas_call(
        functools.partial(_scan_kernel, is_ctx=is_ctx),
        grid=(nt, N_HP),
        in_specs=in_specs,
        out_specs=out_specs,
        out_shape=out_shape,
        scratch_shapes=[pltpu.VMEM((tt, 2 * DHV), F32)],
        compiler_params=_cparams(2),
        name="mlstm_scan_ctx" if is_ctx else "mlstm_scan_lat",
    )(*args)


SC_TM = 512


def _sc_proj_kernel(x_ref, mod_ref, w_ref, bg_ref, cu_ref):
    m = mod_ref[...]
    h = (x_ref[...] * (1.0 + m[1:2]) + m[0:1]).astype(BF)
    bg_ref[...] = _dot(h, w_ref[:, 0:D])
    cu_ref[...] = _dot(h, w_ref[:, D:2 * D]) * _dot(h, w_ref[:, 2 * D:3 * D])


def _sc_proj(x, mod, layer, w_in):
    tm = SC_TM
    return pl.pallas_call(
        _sc_proj_kernel,
        grid=(NTOK // tm,),
        in_specs=[pl.BlockSpec((tm, D), lambda t: (t, 0)), _mod_spec(layer, tm),
                  pl.BlockSpec((D, 3 * D), lambda t: (0, 0), pipeline_mode=pl.Buffered(1))],
        out_specs=[pl.BlockSpec((tm, D), lambda t: (t, 0))] * 2,
        out_shape=[jax.ShapeDtypeStruct((NTOK, D), F32)] * 2,
        compiler_params=_cparams(1),
        name="sconv_proj",
    )(x, mod, w_in)


def _sc_out_kernel(cu_ref, prev_ref, next_ref, bg_ref, cw_ref, w_ref, x_ref, mod_ref, g_ref, b_ref, o_ref):
    tm = SC_TM
    row0 = pl.program_id(0) * tm
    slen = _seq_len(row0)
    loc = lax.broadcasted_iota(jnp.int32, (tm, 1), 0)
    pos = (row0 + loc) & (slen - 1)
    cu = cu_ref[...]
    prev = jnp.where(loc == 0, prev_ref[7:8, :], pltpu.roll(cu, 1, 0))
    prev = jnp.where(pos == 0, 0.0, prev)
    nxt = jnp.where(loc == tm - 1, next_ref[0:1, :], pltpu.roll(cu, tm - 1, 0))
    nxt = jnp.where(pos == slen - 1, 0.0, nxt)
    conv = cw_ref[0:1, :] * prev + cw_ref[1:2, :] * cu + cw_ref[2:3, :] * nxt
    y = _dot((bg_ref[...] * conv).astype(BF), w_ref[...])
    m = mod_ref[...]
    o_ref[...] = _ln(ALPHA * x_ref[...] + m[2:3] * y, g_ref[...], b_ref[...])


def _sc_out(cu, bg, conv_w, w_out, x, mod, layer, ln_g, ln_b):
    tm = SC_TM
    hb = tm // 8
    nhb = NTOK // 8
    return pl.pallas_call(
        _sc_out_kernel,
        grid=(NTOK // tm,),
        in_specs=[pl.BlockSpec((tm, D), lambda t: (t, 0)),
                  pl.BlockSpec((8, D), lambda t: (jnp.maximum(t * hb - 1, 0), 0)),
                  pl.BlockSpec((8, D), lambda t: (jnp.minimum((t + 1) * hb, nhb - 1), 0)),
                  pl.BlockSpec((tm, D), lambda t: (t, 0)),
                  _full((3, D)),
                  pl.BlockSpec((D, D), lambda t: (0, 0), pipeline_mode=pl.Buffered(1)),
                  pl.BlockSpec((tm, D), lambda t: (t, 0)),
                  _mod_spec(layer, tm),
                  _full((1, D)), _full((1, D))],
        out_specs=pl.BlockSpec((tm, D), lambda t: (t, 0)),
        out_shape=jax.ShapeDtypeStruct((NTOK, D), F32),
        compiler_params=_cparams(1),
        name="sconv_out",
    )(cu, cu, cu, bg, conv_w, w_out, x, mod, ln_g.reshape(1, D), ln_b.reshape(1, D))


PL_TM = 256
PL_HALO = 8


def _pool_kernel(x_ref, prev_ref, next_ref, mod_ref, w_ref, pb_ref, ps_ref, g_ref, b_ref, o_ref, e_ref):
    tm = PL_TM
    row0 = pl.program_id(0) * tm
    slen = _seq_len(row0)
    pos0 = row0 & (slen - 1)
    m = mod_ref[...]
    x = x_ref[...]
    h = x * (1.0 + m[1:2]) + m[0:1]
    hp = prev_ref[...] * (1.0 + m[1:2]) + m[0:1]
    hn = next_ref[...] * (1.0 + m[1:2]) + m[0:1]
    e_ref[0:PL_HALO, :] = jnp.where(pos0 != 0, hp, 0.0)
    e_ref[PL_HALO:PL_HALO + tm, :] = h
    e_ref[PL_HALO + tm:, :] = jnp.where(pos0 + tm != slen, hn, 0.0)
    pos = pos0 + lax.broadcasted_iota(jnp.int32, (tm, 1), 0)
    ys = []
    for gi, win in enumerate(POOL_WINDOWS):
        cs = slice(gi * GROUP_W, (gi + 1) * GROUP_W)
        back = win // 2
        tot = e_ref[PL_HALO - back:PL_HALO - back + tm, cs]
        for k in range(1 - back, win - back):
            tot = tot + e_ref[PL_HALO + k:PL_HALO + k + tm, cs]
        cnt = jnp.minimum(pos + (win - back), slen) - jnp.maximum(pos - back, 0)
        p = tot / cnt.astype(F32) - h[:, cs]
        ys.append(_dot(p.astype(BF), w_ref[gi]))
    y = (jnp.concatenate(ys, axis=1) + pb_ref[...]) * ps_ref[...]
    o_ref[...] = _ln(ALPHA * x + m[2:3] * y, g_ref[...], b_ref[...])


def _pool(x, mod, layer, w, pb, ps, ln_g, ln_b):
    tm = PL_TM
    hb = tm // PL_HALO
    nhb = NTOK // PL_HALO
    return pl.pallas_call(
        _pool_kernel,
        grid=(NTOK // tm,),
        in_specs=[pl.BlockSpec((tm, D), lambda t: (t, 0)),
                  pl.BlockSpec((PL_HALO, D), lambda t: (jnp.maximum(t * hb - 1, 0), 0)),
                  pl.BlockSpec((PL_HALO, D), lambda t: (jnp.minimum((t + 1) * hb, nhb - 1), 0)),
                  _mod_spec(layer, tm),
                  _full((N_GROUPS, GROUP_W, GROUP_W)),
                  _full((1, D)), _full((1, D)), _full((1, D)), _full((1, D))],
        out_specs=pl.BlockSpec((tm, D), lambda t: (t, 0)),
        out_shape=jax.ShapeDtypeStruct((NTOK, D), F32),
        scratch_shapes=[pltpu.VMEM((tm + 2 * PL_HALO, D), F32)],
        compiler_params=_cparams(1),
        name="pool",
    )(x, x, x, mod, w, pb.reshape(1, D), ps.reshape(1, D), ln_g.reshape(1, D), ln_b.reshape(1, D))


FT_TK = 256


def _dft_mats(n, scale):
    k = np.arange(n, dtype=np.int64)
    ang = 2.0 * np.pi * ((k[:, None] * k[None, :]) % n).astype(np.float64) / n
    return np.stack([np.cos(ang) * scale, np.sin(ang) * scale]).astype(np.float32)


def _dot_x3(a_parts, b_parts):
    (ah, al), (bh, bl) = a_parts, b_parts
    return _dot(ah, bh) + (_dot(ah, bl) + _dot(al, bh))


def _fourier_kernel(x_ref, mod_ref, cs_ref, cc_ref, o_ref):
    m = mod_ref[...]
    hparts = _split2(x_ref[...] * (1.0 + m[1:2]) + m[0:1])
    ac = _dot_x3(_split2(cs_ref[0]), hparts)
    as_ = _dot_x3(_split2(cs_ref[1]), hparts)
    ccp = _split2(cc_ref[0])
    scp = _split2(cc_ref[1])
    outs = []
    for gi in range(N_GROUPS):
        cs = slice(gi * GROUP_W, (gi + 1) * GROUP_W)
        outs.append(_dot_x3(_split2(ac[:, cs]), ccp) - _dot_x3(_split2(as_[:, cs]), scp))
    o_ref[...] = jnp.concatenate(outs, axis=1).astype(BF)


def _fourier(x, mod, layer, s, nseq, row_off, mod_row_fn):
    tk = FT_TK
    seq_off = row_off // s
    cs = jnp.asarray(_dft_mats(s, s ** -0.5))
    cc = jnp.asarray(_dft_mats(GROUP_W, GROUP_W ** -0.5))
    return pl.pallas_call(
        _fourier_kernel,
        grid=(nseq, s // tk),
        in_specs=[pl.BlockSpec((s, D), lambda b, j: (seq_off + b, 0)),
                  pl.BlockSpec((None, None, 6, D), lambda b, j: (layer, mod_row_fn(b), 0, 0)),
                  pl.BlockSpec((2, tk, s), lambda b, j: (0, j, 0)),
                  _full((2, GROUP_W, GROUP_W))],
        out_specs=pl.BlockSpec((tk, D), lambda b, j: (b * (s // tk) + j, 0)),
        out_shape=jax.ShapeDtypeStruct((nseq * s, D), BF),
        compiler_params=_cparams(2),
        name="fourier_%d" % s,
    )(x, mod, cs, cc)


def _pos_embed():
    rows = DEC_SEQ // GRID_W
    quarter = D // 4
    omega = 1.0 / (10000.0 ** (np.arange(quarter, dtype=np.float64) / quarter))
    rr, cc = np.meshgrid(np.arange(rows, dtype=np.float64), np.arange(GRID_W, dtype=np.float64), indexing="ij")
    er = rr.reshape(-1, 1) * omega
    ec = cc.reshape(-1, 1) * omega
    return np.concatenate([np.sin(er), np.cos(er), np.sin(ec), np.cos(ec)], axis=-1).astype(np.float32)


def _gate_perm():
    perm = np.zeros(4 * H, dtype=np.int32)
    for hp in range(N_HP):
        for d in range(2):
            for j in range(2):
                for hh in range(2):
                    perm[hp * NG + d * 4 + j * 2 + hh] = d * 2 * H + j * H + 2 * hp + hh
    return perm


def kernel(x_prompt, x_sample, state_C, state_n, state_m, c, c_ctx, w_mod, b_mod, ln_g, ln_b, mlp_w1, mlp_w2,
           ml_w_in, ml_w_gate, ml_b_gate, ml_norm_g, ml_w_out, sc_w_in, sc_conv_w, sc_w_out,
           pl_w, pl_b, pl_scale, ft_w_out, ft_b_out):
    cond = jnp.concatenate([c_ctx[None, :], c, jnp.zeros((NCOND - 1 - DEC_BATCH, D), F32)], axis=0)
    mod = _modulation(cond, w_mod, b_mod)

    x = (x_prompt.reshape(NCTX, D), x_sample.reshape(NLAT, D), jnp.asarray(_pos_embed()))

    new_c = new_n = new_m = None
    for i in range(DEPTH):
        kind, j = i % 4, i // 4
        if kind == 0:
            w_in = ml_w_in[j].astype(BF)
            perm = _gate_perm()
            wg = ml_w_gate[j][:, perm].astype(BF)
            bg = ml_b_gate[j][perm]
            q, kt, v, o, g, gt = _ml_proj(
                x, mod, i, w_in[:, :QK], w_in[:, QK:2 * QK].T, w_in[:, 2 * QK:2 * QK + V], w_in[:, 2 * QK + V:],
                wg, wg.T, bg.reshape(1, -1), bg.reshape(-1, 1))
            uc, cst, mst = _ml_scan(q, kt, v, o, g, gt, ml_norm_g[j])
            c0 = jnp.concatenate([state_C[:, j], state_n[:, j][..., None],
                                  jnp.zeros((DEC_BATCH, 2, H, DQK, DHV - 1), F32)], axis=-1)
            m0 = jnp.broadcast_to(state_m[:, j][..., None, None], (DEC_BATCH, 2, H, 1, 128))
            (ul,) = _ml_scan(q, kt, v, o, g, gt, ml_norm_g[j], c0, m0)
            x = _out_proj(uc, ul, ml_w_out[j].astype(BF), jnp.zeros((D,), F32), x, mod, i, ln_g[i, 0], ln_b[i, 0])
            new_c = cst[..., :DHV][:, None]
            new_n = cst[..., DHV][:, None]
            new_m = mst[..., 0, 0][:, None]
        elif kind == 1:
            bgate, cu = _sc_proj(x, mod, i, sc_w_in[j].astype(BF))
            x = _sc_out(cu, bgate, sc_conv_w[j], sc_w_out[j].astype(BF), x, mod, i, ln_g[i, 0], ln_b[i, 0])
        elif kind == 2:
            x = _pool(x, mod, i, pl_w[j].astype(BF), pl_b[j], pl_scale[j], ln_g[i, 0], ln_b[i, 0])
        else:
            fc = _fourier(x, mod, i, SEQ, BATCH, 0, lambda b: 0)
            fl = _fourier(x, mod, i, DEC_SEQ, DEC_BATCH, NCTX, lambda b: 1 + b)
            x = _out_proj(fc, fl, ft_w_out[j].astype(BF), ft_b_out[j], x, mod, i, ln_g[i, 0], ln_b[i, 0])
        x = _mlp(x, mod, i, mlp_w1, mlp_w2, ln_g[i, 1], ln_b[i, 1],
                 split_out=(i == DEPTH - 1))

    y_prompt = x[0].reshape(BATCH, SEQ, D)
    y_sample = x[1].reshape(DEC_BATCH, DEC_SEQ, D)
    return y_prompt, y_sample, new_c, new_n, new_m
```

```python
import functools
import math

import numpy as np
import jax
import jax.numpy as jnp
from jax import lax
from jax.experimental import pallas as pl
from jax.experimental.pallas import tpu as pltpu

D = 1024
BATCH, SEQ = 16, 256
DEC_BATCH, DEC_SEQ = 4, 1024
DEPTH = 4
GRID_W = 64
H = 8
DQK = 64
DHV = 128
QK = H * DQK
V = H * DHV
N_GROUPS = 4
GROUP_W = D // N_GROUPS
POOL_WINDOWS = (2, 4, 8, 16)
D_FF = 4 * D
ALPHA = (2.0 * DEPTH) ** 0.25
LN_EPS = 1e-5
F32 = jnp.float32
BF = jnp.bfloat16

NCTX = BATCH * SEQ
NLAT = DEC_BATCH * DEC_SEQ
NTOK = NCTX + NLAT
NCOND = 8
CHUNK = 256
VMEM_LIMIT = 56 * 1024 * 1024


def _cparams(n_axes):
    return pltpu.CompilerParams(dimension_semantics=("arbitrary",) * n_axes,
                                vmem_limit_bytes=VMEM_LIMIT)


def _mod_row(row0):
    return jnp.where(row0 < NCTX, 0, 1 + (row0 - NCTX) // DEC_SEQ)


def _seq_len(row0):
    return jnp.where(row0 < NCTX, SEQ, DEC_SEQ)


def _ln(z, g, b):
    mu = jnp.mean(z, axis=-1, keepdims=True)
    zc = z - mu
    var = jnp.mean(zc * zc, axis=-1, keepdims=True)
    return zc * lax.rsqrt(var + LN_EPS) * g + b


def _dot(a, b):
    return jnp.dot(a, b, preferred_element_type=F32)


def _dot_nt(a, b):
    return lax.dot_general(a, b, (((1,), (1,)), ((), ())), preferred_element_type=F32)


def _split2(x):
    hi = x.astype(BF)
    lo = (x - hi.astype(F32)).astype(BF)
    return hi, lo


def _split3(x):
    hi = x.astype(BF)
    r = x - hi.astype(F32)
    mid = r.astype(BF)
    lo = (r - mid.astype(F32)).astype(BF)
    return hi, mid, lo


def _full(shape):
    n = len(shape)
    return pl.BlockSpec(shape, lambda *_: (0,) * n)


def _x_specs(x, tm):
    if not isinstance(x, tuple):
        return [pl.BlockSpec((tm, D), lambda t: (t, 0))], [x]
    nc = NCTX // tm
    return ([pl.BlockSpec((tm, D), lambda t: (jnp.minimum(t, nc - 1), 0)),
             pl.BlockSpec((tm, D), lambda t: (jnp.maximum(t - nc, 0), 0)),
             pl.BlockSpec((tm, D), lambda t: (t % (DEC_SEQ // tm), 0))], list(x))


def _read_x(x_refs, tm):
    if len(x_refs) == 1:
        return x_refs[0][...]
    xc_ref, xl_ref, pe_ref = x_refs
    return jnp.where(pl.program_id(0) * tm < NCTX, xc_ref[...], xl_ref[...] + pe_ref[...])


def _mod_spec(layer, tm):
    return pl.BlockSpec((None, None, 6, D), lambda t: (layer, _mod_row(t * tm), 0, 0))


def _mod_kernel(c_ref, w_ref, b_ref, o_ref):
    c = c_ref[...]
    s = c * jax.nn.sigmoid(c)
    o_ref[...] = _dot(s.astype(BF), w_ref[...].astype(BF)) + b_ref[...]


def _modulation(cond, w_mod, b_mod):
    tn = 1536
    out = pl.pallas_call(
        _mod_kernel,
        grid=(DEPTH, 6 * D // tn),
        in_specs=[pl.BlockSpec((NCOND, D), lambda i, j: (0, 0)),
                  pl.BlockSpec((None, D, tn), lambda i, j: (i, 0, j)),
                  pl.BlockSpec((None, 1, tn), lambda i, j: (i, 0, j))],
        out_specs=pl.BlockSpec((None, NCOND, tn), lambda i, j: (i, 0, j)),
        out_shape=jax.ShapeDtypeStruct((DEPTH, NCOND, 6 * D), F32),
        compiler_params=_cparams(2),
        name="modulation",
    )(cond, w_mod, b_mod.reshape(DEPTH, 1, 6 * D))
    return out.reshape(DEPTH, NCOND, 6, D)


MLP_TM = 512
MLP_FC = 512
MLP_NF = D_FF // MLP_FC
MLP_NT = NTOK // MLP_TM


def _mlp_tile(s):
    return jnp.maximum(s - (MLP_NF - 1), 0)


def _mlp_kernel(x_ref, mod_ref, w1_ref, w2_ref, g_ref, b_ref, *refs):
    o_refs, (w1s_ref, w2s_ref, acc_ref) = refs[:-3], refs[-3:]
    s = pl.program_id(0)
    x = x_ref[...]
    m = mod_ref[...]
    h = (x * (1.0 + m[4:5]) + m[3:4]).astype(BF)

    def hidden(w1c):
        return jnp.square(jnp.maximum(_dot(h, w1c), 0.0)).astype(BF)

    def result(acc):
        return _ln(ALPHA * x + m[5:6] * acc, g_ref[...], b_ref[...])

    @pl.when(s < MLP_NF)
    def _():
        w1c = w1_ref[...].astype(BF)
        w2c = w2_ref[...].astype(BF)
        w1s_ref[s] = w1c
        w2s_ref[s] = w2c
        part = _dot(hidden(w1c), w2c)

        @pl.when(s == 0)
        def _():
            acc_ref[...] = part

        @pl.when(s > 0)
        def _():
            acc_ref[...] += part

        @pl.when(s == MLP_NF - 1)
        def _():
            o_refs[0][...] = result(acc_ref[...])

    @pl.when(s >= MLP_NF)
    def _():
        acc = jnp.zeros((MLP_TM, D), F32)
        for f in range(MLP_NF):
            acc = acc + _dot(hidden(w1s_ref[f]), w2s_ref[f])
        res = result(acc)
        if len(o_refs) == 1:
            o_refs[0][...] = res
        else:
            is_ctx = _mlp_tile(s) * MLP_TM < NCTX

            @pl.when(is_ctx)
            def _():
                o_refs[0][...] = res

            @pl.when(jnp.logical_not(is_ctx))
            def _():
                o_refs[1][...] = res


def _mlp(x, mod, layer, w1, w2, ln_g, ln_b, split_out=False):
    tm = MLP_TM
    nc = NCTX // tm
    if split_out:
        out_specs = [pl.BlockSpec((tm, D), lambda s: (jnp.minimum(_mlp_tile(s), nc - 1), 0)),
                     pl.BlockSpec((tm, D), lambda s: (jnp.maximum(_mlp_tile(s) - nc, 0), 0))]
        out_shape = [jax.ShapeDtypeStruct((NCTX, D), F32), jax.ShapeDtypeStruct((NLAT, D), F32)]
    else:
        out_specs = pl.BlockSpec((tm, D), lambda s: (_mlp_tile(s), 0))
        out_shape = jax.ShapeDtypeStruct((NTOK, D), F32)
    return pl.pallas_call(
        _mlp_kernel,
        grid=(MLP_NF - 1 + MLP_NT,),
        in_specs=[pl.BlockSpec((tm, D), lambda s: (_mlp_tile(s), 0)),
                  pl.BlockSpec((None, None, 6, D), lambda s: (layer, _mod_row(_mlp_tile(s) * tm), 0, 0)),
                  pl.BlockSpec((None, D, MLP_FC), lambda s: (layer, 0, jnp.minimum(s, MLP_NF - 1))),
                  pl.BlockSpec((None, MLP_FC, D), lambda s: (layer, jnp.minimum(s, MLP_NF - 1), 0)),
                  _full((1, D)), _full((1, D))],
        out_specs=out_specs,
        out_shape=out_shape,
        scratch_shapes=[pltpu.VMEM((MLP_NF, D, MLP_FC), BF),
                        pltpu.VMEM((MLP_NF, MLP_FC, D), BF),
                        pltpu.VMEM((tm, D), F32)],
        compiler_params=_cparams(1),
        name="mlp",
    )(x, mod, w1, w2, ln_g.reshape(1, D), ln_b.reshape(1, D))


OUT_TM = 512


def _out_kernel(uc_ref, ul_ref, w_ref, bias_ref, mod_ref, g_ref, b_ref, *refs):
    x_refs, o_ref = refs[:-1], refs[-1]
    is_ctx = pl.program_id(0) * OUT_TM < NCTX
    u = jnp.where(is_ctx, uc_ref[...], ul_ref[...])
    y = _dot(u, w_ref[...]) + bias_ref[...]
    m = mod_ref[...]
    o_ref[...] = _ln(ALPHA * _read_x(x_refs, OUT_TM) + m[2:3] * y, g_ref[...], b_ref[...])


def _out_proj(uc, ul, w, bias, x, mod, layer, ln_g, ln_b):
    tm = OUT_TM
    nc = NCTX // tm
    k = w.shape[0]
    x_specs, x_args = _x_specs(x, tm)
    return pl.pallas_call(
        _out_kernel,
        grid=(NTOK // tm,),
        in_specs=[pl.BlockSpec((tm, k), lambda t: (jnp.minimum(t, nc - 1), 0)),
                  pl.BlockSpec((tm, k), lambda t: (jnp.maximum(t - nc, 0), 0)),
                  pl.BlockSpec((k, D), lambda t: (0, 0), pipeline_mode=pl.Buffered(1)),
                  _full((1, D)),
                  _mod_spec(layer, tm),
                  _full((1, D)), _full((1, D))] + x_specs,
        out_specs=pl.BlockSpec((tm, D), lambda t: (t, 0)),
        out_shape=jax.ShapeDtypeStruct((NTOK, D), F32),
        compiler_params=_cparams(1),
        name="out_proj",
    )(uc, ul, w, bias.reshape(1, D), mod, ln_g.reshape(1, D), ln_b.reshape(1, D), *x_args)


ML_TM = 512
N_HP = H // 2
NG = 8


def _log_sigmoid(g):
    return jnp.minimum(g, 0.0) - jnp.log1p(jnp.exp(-jnp.abs(g)))


def _ml_proj_kernel(mod_ref, wqt_ref, wk_ref, wkt_ref, wvt_ref, wo_ref, wg_ref, wgt_ref, bg_ref, bgt_ref, *refs):
    x_refs = refs[:-7]
    qt_ref, k_ref, kt_ref, vt_ref, o_ref, g_ref, gt_ref = refs[-7:]
    m = mod_ref[...]
    h = (_read_x(x_refs, ML_TM) * (1.0 + m[1:2]) + m[0:1]).astype(BF)
    qt_ref[...] = _dot_nt(wqt_ref[...], h).astype(BF)
    k_ref[...] = (_dot(h, wk_ref[...]) * (DQK ** -0.5)).astype(BF)
    kt_ref[...] = (_dot_nt(wkt_ref[...], h) * (DQK ** -0.5)).astype(BF)
    vt_ref[...] = _dot_nt(wvt_ref[...], h).astype(BF)
    o_ref[...] = _dot(h, wo_ref[...])
    g = _dot(h, wg_ref[...]) + bg_ref[...]
    col = lax.broadcasted_iota(jnp.int32, g.shape, 1)
    g = jnp.where((col & 2) != 0, _log_sigmoid(g), g)
    gt = _dot_nt(wgt_ref[...], h) + bgt_ref[...]
    row = lax.broadcasted_iota(jnp.int32, gt.shape, 0)
    gt = jnp.where((row & 2) != 0, _log_sigmoid(gt), gt)
    for hp in range(N_HP):
        g_ref[hp] = g[:, hp * NG:(hp + 1) * NG]
        gt_ref[hp] = gt[hp * NG:(hp + 1) * NG, :]


def _ml_proj(x, mod, layer, wqt, wk, wkt, wvt, wo, wg, wgt, bg, bgt):
    tm = ML_TM
    res = lambda shape: pl.BlockSpec(shape, lambda t: (0,) * len(shape), pipeline_mode=pl.Buffered(1))
    x_specs, x_args = _x_specs(x, tm)
    return pl.pallas_call(
        _ml_proj_kernel,
        grid=(NTOK // tm,),
        in_specs=[_mod_spec(layer, tm),
                  res((QK, D)), res((D, QK)), res((QK, D)), res((V, D)), res((D, V)),
                  res((D, 4 * NG)), res((4 * NG, D)), res((1, 4 * NG)), res((4 * NG, 1))] + x_specs,
        out_specs=[pl.BlockSpec((QK, tm), lambda t: (0, t)),
                   pl.BlockSpec((tm, QK), lambda t: (t, 0)),
                   pl.BlockSpec((QK, tm), lambda t: (0, t)),
                   pl.BlockSpec((V, tm), lambda t: (0, t)),
                   pl.BlockSpec((tm, V), lambda t: (t, 0)),
                   pl.BlockSpec((N_HP, tm, NG), lambda t: (0, t, 0)),
                   pl.BlockSpec((N_HP, NG, tm), lambda t: (0, 0, t))],
        out_shape=[jax.ShapeDtypeStruct((QK, NTOK), BF),
                   jax.ShapeDtypeStruct((NTOK, QK), BF),
                   jax.ShapeDtypeStruct((QK, NTOK), BF),
                   jax.ShapeDtypeStruct((V, NTOK), BF),
                   jax.ShapeDtypeStruct((NTOK, V), F32),
                   jax.ShapeDtypeStruct((N_HP, NTOK, NG), F32),
                   jax.ShapeDtypeStruct((N_HP, NG, NTOK), F32)],
        compiler_params=_cparams(1),
        name="mlstm_proj",
    )(mod, wqt, wk, wkt, wvt, wo, wg, wgt, bg, bgt, *x_args)


DHA = DHV + 8


def _scan_chunk(k_c, kt_c, qt_c, vta, rr_col, rr_row, b_row, m, cta, d, need_state):
    L = k_c.shape[0]
    si = lax.broadcasted_iota(jnp.int32, (L, L), 0)
    ti = lax.broadcasted_iota(jnp.int32, (L, L), 1)
    mask = (si <= ti) if d == 0 else (si >= ti)
    rrm = jnp.where(mask, rr_col, -jnp.inf)
    g = jnp.maximum(jnp.max(rrm, axis=0, keepdims=True), m)
    st = _dot(k_c, qt_c) * jnp.exp(rrm - g)
    num = _dot(vta[:DHV], st.astype(BF))
    den = jnp.sum(st, axis=0, keepdims=True)
    if cta is not None:
        a = jnp.exp(m - g)
        qc = _dot(cta.astype(BF), qt_c)
        num = num + a * qc[:DHV]
        den = den + a * qc[DHV:DHV + 1]
    mt = b_row + g
    ht = num * (1.0 / jnp.maximum(jnp.abs(den), jnp.exp(-mt)))
    if not need_state:
        return ht, None, None
    last = L - 1 if d == 0 else 0
    m_new = mt[:, last:last + 1]
    b_last = b_row[:, last:last + 1]
    w = jnp.exp(b_last + rr_row - m_new)
    c_new = _dot_nt(vta, (kt_c.astype(F32) * w).astype(BF))
    if cta is not None:
        c_new = jnp.exp(b_last + m - m_new) * cta + c_new
    return ht, c_new, m_new


def _scan_kernel(*refs, is_ctx):
    if is_ctx:
        qt_ref, k_ref, kt_ref, vt_ref, o_ref, g_ref, gt_ref, ng_ref, u_ref, cst_ref, mst_ref, hbuf = refs
    else:
        qt_ref, k_ref, kt_ref, vt_ref, o_ref, g_ref, gt_ref, ng_ref, c0_ref, m0_ref, u_ref, hbuf = refs
    L = CHUNK
    nch = DEC_SEQ // L
    ri = lax.broadcasted_iota(jnp.int32, (L, L), 0)
    ci = lax.broadcasted_iota(jnp.int32, (L, L), 1)
    lower = jnp.where(ci <= ri, 1.0, 0.0).astype(BF)
    upper = jnp.where(ci >= ri, 1.0, 0.0).astype(BF)
    sub = lax.broadcasted_iota(jnp.int32, (DHA - DHV, L), 0)
    ones_rows = jnp.where(sub == 0, 1.0, 0.0).astype(BF)

    rr_cols, rr_rows, b_rows = [], [], []
    for c in range(nch):
        gc = g_ref[c * L:(c + 1) * L, :]
        gtc = gt_ref[:, c * L:(c + 1) * L]
        gparts = _split3(gc)
        tparts = _split3(gtc)
        b_col = (sum(_dot(lower, p) for p in gparts), sum(_dot(upper, p) for p in gparts))
        b_row = (sum(_dot(p, upper) for p in tparts), sum(_dot(p, lower) for p in tparts))
        rr_cols.append([gc[:, 0:NG - 2] - b[:, 2:NG] for b in b_col])
        rr_rows.append([gtc[0:NG - 2, :] - b[2:NG, :] for b in b_row])
        b_rows.append(b_row)

    for hh in range(2):
        for d in range(2):
            ig_i, lf_i = d * 4 + hh, d * 4 + 2 + hh
            if is_ctx:
                m, cta = jnp.zeros((1, 1), F32), None
            else:
                m = m0_ref[d, hh][:, 0:1]
                cta = c0_ref[d, hh]
            order = range(nch) if d == 0 else range(nch - 1, -1, -1)
            for n, c in enumerate(order):
                sl = slice(c * L, (c + 1) * L)
                hs = slice(hh * DQK, (hh + 1) * DQK)
                vta = jnp.concatenate([vt_ref[hh * DHV:(hh + 1) * DHV, sl], ones_rows], axis=0)
                need_state = is_ctx or n < nch - 1
                hcur, c_new, m_new = _scan_chunk(
                    k_ref[sl, hs], kt_ref[hs, sl], qt_ref[hs, sl], vta,
                    rr_cols[c][d][:, ig_i:ig_i + 1], rr_rows[c][d][ig_i:ig_i + 1, :],
                    b_rows[c][d][lf_i:lf_i + 1, :], m, cta, d, need_state)
                if d == 0:
                    hbuf[hh * DHV:(hh + 1) * DHV, sl] = hcur
                else:
                    hbuf[hh * DHV:(hh + 1) * DHV, sl] += hcur
                if is_ctx:
                    cst_ref[c, d, hh] = c_new
                    mst_ref[c, d, hh] = jnp.broadcast_to(m_new, (8, 128))
                else:
                    m, cta = m_new, c_new

    for hh in range(2):
        hsum = hbuf[hh * DHV:(hh + 1) * DHV, :]
        mu = jnp.mean(hsum, axis=0, keepdims=True)
        hc = hsum - mu
        var = jnp.mean(hc * hc, axis=0, keepdims=True)
        hn = hc * lax.rsqrt(var + LN_EPS) * ng_ref[hh * DHV:(hh + 1) * DHV, :]
        og = jax.nn.sigmoid(o_ref[:, hh * DHV:(hh + 1) * DHV])
        u_ref[:, hh * DHV:(hh + 1) * DHV] = (hn.T * og).astype(BF)


def _ml_scan(qt, k, kt, vt, o, g, gt, norm_g, c0aug=None, m0b=None):
    is_ctx = c0aug is None
    tt = DEC_SEQ
    toff = 0 if is_ctx else NCTX // tt
    nt = (NCTX if is_ctx else NLAT) // tt
    in_specs = [pl.BlockSpec((2 * DQK, tt), lambda t, p: (p, toff + t)),
                pl.BlockSpec((tt, 2 * DQK), lambda t, p: (toff + t, p)),
                pl.BlockSpec((2 * DQK, tt), lambda t, p: (p, toff + t)),
                pl.BlockSpec((2 * DHV, tt), lambda t, p: (p, toff + t)),
                pl.BlockSpec((tt, 2 * DHV), lambda t, p: (toff + t, p)),
                pl.BlockSpec((None, tt, NG), lambda t, p: (p, toff + t, 0)),
                pl.BlockSpec((None, NG, tt), lambda t, p: (p, 0, toff + t)),
                pl.BlockSpec((2 * DHV, 1), lambda t, p: (p, 0))]
    args = [qt, k, kt, vt, o, g, gt, norm_g.reshape(V, 1)]
    u_spec = pl.BlockSpec((tt, 2 * DHV), lambda t, p: (t, p))
    u_shape = jax.ShapeDtypeStruct((nt * tt, V), BF)
    if is_ctx:
        nseq = tt // SEQ
        out_specs = [u_spec,
                     pl.BlockSpec((nseq, 2, 2, DHA, DQK), lambda t, p: (t, 0, p, 0, 0)),
                     pl.BlockSpec((nseq, 2, 2, 8, 128), lambda t, p: (t, 0, p, 0, 0))]
        out_shape = [u_shape,
                     jax.ShapeDtypeStruct((BATCH, 2, H, DHA, DQK), F32),
                     jax.ShapeDtypeStruct((BATCH, 2, H, 8, 128), F32)]
    else:
        in_specs += [pl.BlockSpec((None, 2, 2, DHA, DQK), lambda t, p: (t, 0, p, 0, 0)),
                     pl.BlockSpec((None, 2, 2, 1, 128), lambda t, p: (t, 0, p, 0, 0))]
        args += [c0aug, m0b]
        out_specs = [u_spec]
        out_shape = [u_shape]
    return pl.pallas_call(
        functools.partial(_scan_kernel, is_ctx=is_ctx),
        grid=(nt, N_HP),
        in_specs=in_specs,
        out_specs=out_specs,
        out_shape=out_shape,
        scratch_shapes=[pltpu.VMEM((2 * DHV, tt), F32)],
        compiler_params=_cparams(2),
        name="mlstm_scan_ctx" if is_ctx else "mlstm_scan_lat",
    )(*args)


SC_TM = 512


def _sc_proj_kernel(x_ref, mod_ref, w_ref, bg_ref, cu_ref):
    m = mod_ref[...]
    h = (x_ref[...] * (1.0 + m[1:2]) + m[0:1]).astype(BF)
    bg_ref[...] = _dot(h, w_ref[:, 0:D])
    cu_ref[...] = _dot(h, w_ref[:, D:2 * D]) * _dot(h, w_ref[:, 2 * D:3 * D])


def _sc_proj(x, mod, layer, w_in):
    tm = SC_TM
    return pl.pallas_call(
        _sc_proj_kernel,
        grid=(NTOK // tm,),
        in_specs=[pl.BlockSpec((tm, D), lambda t: (t, 0)), _mod_spec(layer, tm),
                  pl.BlockSpec((D, 3 * D), lambda t: (0, 0), pipeline_mode=pl.Buffered(1))],
        out_specs=[pl.BlockSpec((tm, D), lambda t: (t, 0))] * 2,
        out_shape=[jax.ShapeDtypeStruct((NTOK, D), F32)] * 2,
        compiler_params=_cparams(1),
        name="sconv_proj",
    )(x, mod, w_in)


def _sc_out_kernel(cu_ref, prev_ref, next_ref, bg_ref, cw_ref, w_ref, x_ref, mod_ref, g_ref, b_ref, o_ref):
    tm = SC_TM
    row0 = pl.program_id(0) * tm
    slen = _seq_len(row0)
    loc = lax.broadcasted_iota(jnp.int32, (tm, 1), 0)
    pos = (row0 + loc) & (slen - 1)
    cu = cu_ref[...]
    prev = jnp.where(loc == 0, prev_ref[7:8, :], pltpu.roll(cu, 1, 0))
    prev = jnp.where(pos == 0, 0.0, prev)
    nxt = jnp.where(loc == tm - 1, next_ref[0:1, :], pltpu.roll(cu, tm - 1, 0))
    nxt = jnp.where(pos == slen - 1, 0.0, nxt)
    conv = cw_ref[0:1, :] * prev + cw_ref[1:2, :] * cu + cw_ref[2:3, :] * nxt
    y = _dot((bg_ref[...] * conv).astype(BF), w_ref[...])
    m = mod_ref[...]
    o_ref[...] = _ln(ALPHA * x_ref[...] + m[2:3] * y, g_ref[...], b_ref[...])


def _sc_out(cu, bg, conv_w, w_out, x, mod, layer, ln_g, ln_b):
    tm = SC_TM
    hb = tm // 8
    nhb = NTOK // 8
    return pl.pallas_call(
        _sc_out_kernel,
        grid=(NTOK // tm,),
        in_specs=[pl.BlockSpec((tm, D), lambda t: (t, 0)),
                  pl.BlockSpec((8, D), lambda t: (jnp.maximum(t * hb - 1, 0), 0)),
                  pl.BlockSpec((8, D), lambda t: (jnp.minimum((t + 1) * hb, nhb - 1), 0)),
                  pl.BlockSpec((tm, D), lambda t: (t, 0)),
                  _full((3, D)),
                  pl.BlockSpec((D, D), lambda t: (0, 0), pipeline_mode=pl.Buffered(1)),
                  pl.BlockSpec((tm, D), lambda t: (t, 0)),
                  _mod_spec(layer, tm),
                  _full((1, D)), _full((1, D))],
        out_specs=pl.BlockSpec((tm, D), lambda t: (t, 0)),
        out_shape=jax.ShapeDtypeStruct((NTOK, D), F32),
        compiler_params=_cparams(1),
        name="sconv_out",
    )(cu, cu, cu, bg, conv_w, w_out, x, mod, ln_g.reshape(1, D), ln_b.reshape(1, D))


PL_TM = 256
PL_HALO = 8


def _pool_kernel(x_ref, prev_ref, next_ref, mod_ref, w_ref, pb_ref, ps_ref, g_ref, b_ref, o_ref, e_ref):
    tm = PL_TM
    row0 = pl.program_id(0) * tm
    slen = _seq_len(row0)
    pos0 = row0 & (slen - 1)
    m = mod_ref[...]
    x = x_ref[...]
    h = x * (1.0 + m[1:2]) + m[0:1]
    hp = prev_ref[...] * (1.0 + m[1:2]) + m[0:1]
    hn = next_ref[...] * (1.0 + m[1:2]) + m[0:1]
    e_ref[0:PL_HALO, :] = jnp.where(pos0 != 0, hp, 0.0)
    e_ref[PL_HALO:PL_HALO + tm, :] = h
    e_ref[PL_HALO + tm:, :] = jnp.where(pos0 + tm != slen, hn, 0.0)
    pos = pos0 + lax.broadcasted_iota(jnp.int32, (tm, 1), 0)
    ys = []
    for gi, win in enumerate(POOL_WINDOWS):
        cs = slice(gi * GROUP_W, (gi + 1) * GROUP_W)
        back = win // 2
        tot = e_ref[PL_HALO - back:PL_HALO - back + tm, cs]
        for k in range(1 - back, win - back):
            tot = tot + e_ref[PL_HALO + k:PL_HALO + k + tm, cs]
        cnt = jnp.minimum(pos + (win - back), slen) - jnp.maximum(pos - back, 0)
        p = tot / cnt.astype(F32) - h[:, cs]
        ys.append(_dot(p.astype(BF), w_ref[gi]))
    y = (jnp.concatenate(ys, axis=1) + pb_ref[...]) * ps_ref[...]
    o_ref[...] = _ln(ALPHA * x + m[2:3] * y, g_ref[...], b_ref[...])


def _pool(x, mod, layer, w, pb, ps, ln_g, ln_b):
    tm = PL_TM
    hb = tm // PL_HALO
    nhb = NTOK // PL_HALO
    return pl.pallas_call(
        _pool_kernel,
        grid=(NTOK // tm,),
        in_specs=[pl.BlockSpec((tm, D), lambda t: (t, 0)),
                  pl.BlockSpec((PL_HALO, D), lambda t: (jnp.maximum(t * hb - 1, 0), 0)),
                  pl.BlockSpec((PL_HALO, D), lambda t: (jnp.minimum((t + 1) * hb, nhb - 1), 0)),
                  _mod_spec(layer, tm),
                  _full((N_GROUPS, GROUP_W, GROUP_W)),
                  _full((1, D)), _full((1, D)), _full((1, D)), _full((1, D))],
        out_specs=pl.BlockSpec((tm, D), lambda t: (t, 0)),
        out_shape=jax.ShapeDtypeStruct((NTOK, D), F32),
        scratch_shapes=[pltpu.VMEM((tm + 2 * PL_HALO, D), F32)],
        compiler_params=_cparams(1),
        name="pool",
    )(x, x, x, mod, w, pb.reshape(1, D), ps.reshape(1, D), ln_g.reshape(1, D), ln_b.reshape(1, D))


FT_TK = 256


def _dft_mats(n, scale):
    k = np.arange(n, dtype=np.int64)
    ang = 2.0 * np.pi * ((k[:, None] * k[None, :]) % n).astype(np.float64) / n
    return np.stack([np.cos(ang) * scale, np.sin(ang) * scale]).astype(np.float32)


def _dot_x3(a_parts, b_parts):
    (ah, al), (bh, bl) = a_parts, b_parts
    return _dot(ah, bh) + (_dot(ah, bl) + _dot(al, bh))


def _fourier_kernel(x_ref, mod_ref, cs_ref, cc_ref, o_ref):
    m = mod_ref[...]
    hparts = _split2(x_ref[...] * (1.0 + m[1:2]) + m[0:1])
    ac = _dot_x3(_split2(cs_ref[0]), hparts)
    as_ = _dot_x3(_split2(cs_ref[1]), hparts)
    ccp = _split2(cc_ref[0])
    scp = _split2(cc_ref[1])
    outs = []
    for gi in range(N_GROUPS):
        cs = slice(gi * GROUP_W, (gi + 1) * GROUP_W)
        outs.append(_dot_x3(_split2(ac[:, cs]), ccp) - _dot_x3(_split2(as_[:, cs]), scp))
    o_ref[...] = jnp.concatenate(outs, axis=1).astype(BF)


def _fourier(x, mod, layer, s, nseq, row_off, mod_row_fn):
    tk = FT_TK
    seq_off = row_off // s
    cs = jnp.asarray(_dft_mats(s, s ** -0.5))
    cc = jnp.asarray(_dft_mats(GROUP_W, GROUP_W ** -0.5))
    return pl.pallas_call(
        _fourier_kernel,
        grid=(nseq, s // tk),
        in_specs=[pl.BlockSpec((s, D), lambda b, j: (seq_off + b, 0)),
                  pl.BlockSpec((None, None, 6, D), lambda b, j: (layer, mod_row_fn(b), 0, 0)),
                  pl.BlockSpec((2, tk, s), lambda b, j: (0, j, 0)),
                  _full((2, GROUP_W, GROUP_W))],
        out_specs=pl.BlockSpec((tk, D), lambda b, j: (b * (s // tk) + j, 0)),
        out_shape=jax.ShapeDtypeStruct((nseq * s, D), BF),
        compiler_params=_cparams(2),
        name="fourier_%d" % s,
    )(x, mod, cs, cc)


def _pos_embed():
    rows = DEC_SEQ // GRID_W
    quarter = D // 4
    omega = 1.0 / (10000.0 ** (np.arange(quarter, dtype=np.float64) / quarter))
    rr, cc = np.meshgrid(np.arange(rows, dtype=np.float64), np.arange(GRID_W, dtype=np.float64), indexing="ij")
    er = rr.reshape(-1, 1) * omega
    ec = cc.reshape(-1, 1) * omega
    return np.concatenate([np.sin(er), np.cos(er), np.sin(ec), np.cos(ec)], axis=-1).astype(np.float32)


def _gate_perm():
    perm = np.zeros(4 * H, dtype=np.int32)
    for hp in range(N_HP):
        for d in range(2):
            for j in range(2):
                for hh in range(2):
                    perm[hp * NG + d * 4 + j * 2 + hh] = d * 2 * H + j * H + 2 * hp + hh
    return perm


def kernel(x_prompt, x_sample, state_C, state_n, state_m, c, c_ctx, w_mod, b_mod, ln_g, ln_b, mlp_w1, mlp_w2,
           ml_w_in, ml_w_gate, ml_b_gate, ml_norm_g, ml_w_out, sc_w_in, sc_conv_w, sc_w_out,
           pl_w, pl_b, pl_scale, ft_w_out, ft_b_out):
    cond = jnp.concatenate([c_ctx[None, :], c, jnp.zeros((NCOND - 1 - DEC_BATCH, D), F32)], axis=0)
    mod = _modulation(cond, w_mod, b_mod)

    x = (x_prompt.reshape(NCTX, D), x_sample.reshape(NLAT, D), jnp.asarray(_pos_embed()))

    new_c = new_n = new_m = None
    for i in range(DEPTH):
        kind, j = i % 4, i // 4
        if kind == 0:
            w_in = ml_w_in[j].astype(BF)
            perm = _gate_perm()
            wg = ml_w_gate[j][:, perm].astype(BF)
            bg = ml_b_gate[j][perm]
            wk = w_in[:, QK:2 * QK]
            qt, k, kt, vt, o, g, gt = _ml_proj(
                x, mod, i, w_in[:, :QK].T, wk, wk.T, w_in[:, 2 * QK:2 * QK + V].T, w_in[:, 2 * QK + V:],
                wg, wg.T, bg.reshape(1, -1), bg.reshape(-1, 1))
            uc, cst, mst = _ml_scan(qt, k, kt, vt, o, g, gt, ml_norm_g[j])
            c0 = jnp.concatenate([jnp.swapaxes(state_C[:, j], -1, -2), state_n[:, j][..., None, :],
                                  jnp.zeros((DEC_BATCH, 2, H, DHA - DHV - 1, DQK), F32)], axis=-2)
            m0 = jnp.broadcast_to(state_m[:, j][..., None, None], (DEC_BATCH, 2, H, 1, 128))
            (ul,) = _ml_scan(qt, k, kt, vt, o, g, gt, ml_norm_g[j], c0, m0)
            x = _out_proj(uc, ul, ml_w_out[j].astype(BF), jnp.zeros((D,), F32), x, mod, i, ln_g[i, 0], ln_b[i, 0])
            new_c = jnp.swapaxes(cst[..., :DHV, :], -1, -2)[:, None]
            new_n = cst[..., DHV, :][:, None]
            new_m = mst[..., 0, 0][:, None]
        elif kind == 1:
            bgate, cu = _sc_proj(x, mod, i, sc_w_in[j].astype(BF))
            x = _sc_out(cu, bgate, sc_conv_w[j], sc_w_out[j].astype(BF), x, mod, i, ln_g[i, 0], ln_b[i, 0])
        elif kind == 2:
            x = _pool(x, mod, i, pl_w[j].astype(BF), pl_b[j], pl_scale[j], ln_g[i, 0], ln_b[i, 0])
        else:
            fc = _fourier(x, mod, i, SEQ, BATCH, 0, lambda b: 0)
            fl = _fourier(x, mod, i, DEC_SEQ, DEC_BATCH, NCTX, lambda b: 1 + b)
            x = _out_proj(fc, fl, ft_w_out[j].astype(BF), ft_b_out[j], x, mod, i, ln_g[i, 0], ln_b[i, 0])
        x = _mlp(x, mod, i, mlp_w1, mlp_w2, ln_g[i, 1], ln_b[i, 1],
                 split_out=(i == DEPTH - 1))

    y_prompt = x[0].reshape(BATCH, SEQ, D)
    y_sample = x[1].reshape(DEC_BATCH, DEC_SEQ, D)
    return y_prompt, y_sample, new_c, new_n, new_m
```

```python
import functools
import math

import numpy as np
import jax
import jax.numpy as jnp
from jax import lax
from jax.experimental import pallas as pl
from jax.experimental.pallas import tpu as pltpu

D = 1024
BATCH, SEQ = 16, 256
DEC_BATCH, DEC_SEQ = 4, 1024
DEPTH = 4
GRID_W = 64
H = 8
DQK = 64
DHV = 128
QK = H * DQK
V = H * DHV
N_GROUPS = 4
GROUP_W = D // N_GROUPS
POOL_WINDOWS = (2, 4, 8, 16)
D_FF = 4 * D
ALPHA = (2.0 * DEPTH) ** 0.25
LN_EPS = 1e-5
F32 = jnp.float32
BF = jnp.bfloat16

NCTX = BATCH * SEQ
NLAT = DEC_BATCH * DEC_SEQ
NTOK = NCTX + NLAT
NCOND = 8
CHUNK = 256
VMEM_LIMIT = 56 * 1024 * 1024


def _cparams(n_axes):
    return pltpu.CompilerParams(dimension_semantics=("arbitrary",) * n_axes,
                                vmem_limit_bytes=VMEM_LIMIT)


def _mod_row(row0):
    return jnp.where(row0 < NCTX, 0, 1 + (row0 - NCTX) // DEC_SEQ)


def _seq_len(row0):
    return jnp.where(row0 < NCTX, SEQ, DEC_SEQ)


def _ln(z, g, b):
    mu = jnp.mean(z, axis=-1, keepdims=True)
    zc = z - mu
    var = jnp.mean(zc * zc, axis=-1, keepdims=True)
    return zc * lax.rsqrt(var + LN_EPS) * g + b


def _dot(a, b):
    return jnp.dot(a, b, preferred_element_type=F32)


def _dot_nt(a, b):
    return lax.dot_general(a, b, (((1,), (1,)), ((), ())), preferred_element_type=F32)


def _split2(x):
    hi = x.astype(BF)
    lo = (x - hi.astype(F32)).astype(BF)
    return hi, lo


def _split3(x):
    hi = x.astype(BF)
    r = x - hi.astype(F32)
    mid = r.astype(BF)
    lo = (r - mid.astype(F32)).astype(BF)
    return hi, mid, lo


def _full(shape):
    n = len(shape)
    return pl.BlockSpec(shape, lambda *_: (0,) * n)


def _x_specs(x, tm):
    if not isinstance(x, tuple):
        return [pl.BlockSpec((tm, D), lambda t: (t, 0))], [x]
    nc = NCTX // tm
    return ([pl.BlockSpec((tm, D), lambda t: (jnp.minimum(t, nc - 1), 0)),
             pl.BlockSpec((tm, D), lambda t: (jnp.maximum(t - nc, 0), 0)),
             pl.BlockSpec((tm, D), lambda t: (t % (DEC_SEQ // tm), 0))], list(x))


def _read_x(x_refs, tm):
    if len(x_refs) == 1:
        return x_refs[0][...]
    xc_ref, xl_ref, pe_ref = x_refs
    return jnp.where(pl.program_id(0) * tm < NCTX, xc_ref[...], xl_ref[...] + pe_ref[...])


def _mod_spec(layer, tm):
    return pl.BlockSpec((None, None, 6, D), lambda t: (layer, _mod_row(t * tm), 0, 0))


def _mod_kernel(c_ref, w_ref, b_ref, o_ref):
    c = c_ref[...]
    s = c * jax.nn.sigmoid(c)
    o_ref[...] = _dot(s.astype(BF), w_ref[...].astype(BF)) + b_ref[...]


def _modulation(cond, w_mod, b_mod):
    tn = 1536
    out = pl.pallas_call(
        _mod_kernel,
        grid=(DEPTH, 6 * D // tn),
        in_specs=[pl.BlockSpec((NCOND, D), lambda i, j: (0, 0)),
                  pl.BlockSpec((None, D, tn), lambda i, j: (i, 0, j)),
                  pl.BlockSpec((None, 1, tn), lambda i, j: (i, 0, j))],
        out_specs=pl.BlockSpec((None, NCOND, tn), lambda i, j: (i, 0, j)),
        out_shape=jax.ShapeDtypeStruct((DEPTH, NCOND, 6 * D), F32),
        compiler_params=_cparams(2),
        name="modulation",
    )(cond, w_mod, b_mod.reshape(DEPTH, 1, 6 * D))
    return out.reshape(DEPTH, NCOND, 6, D)


MLP_TM = 512
MLP_FC = 512
MLP_NF = D_FF // MLP_FC
MLP_NT = NTOK // MLP_TM


def _mlp_tile(s):
    return jnp.maximum(s - (MLP_NF - 1), 0)


def _mlp_kernel(x_ref, mod_ref, w1_ref, w2_ref, g_ref, b_ref, *refs):
    o_refs, (w1s_ref, w2s_ref, acc_ref) = refs[:-3], refs[-3:]
    s = pl.program_id(0)
    x = x_ref[...]
    m = mod_ref[...]
    h = (x * (1.0 + m[4:5]) + m[3:4]).astype(BF)

    def hidden(w1c):
        return jnp.square(jnp.maximum(_dot(h, w1c), 0.0)).astype(BF)

    def result(acc):
        return _ln(ALPHA * x + m[5:6] * acc, g_ref[...], b_ref[...])

    @pl.when(s < MLP_NF)
    def _():
        w1c = w1_ref[...].astype(BF)
        w2c = w2_ref[...].astype(BF)
        w1s_ref[s] = w1c
        w2s_ref[s] = w2c
        part = _dot(hidden(w1c), w2c)

        @pl.when(s == 0)
        def _():
            acc_ref[...] = part

        @pl.when(s > 0)
        def _():
            acc_ref[...] += part

        @pl.when(s == MLP_NF - 1)
        def _():
            o_refs[0][...] = result(acc_ref[...])

    @pl.when(s >= MLP_NF)
    def _():
        acc = jnp.zeros((MLP_TM, D), F32)
        for f in range(MLP_NF):
            acc = acc + _dot(hidden(w1s_ref[f]), w2s_ref[f])
        res = result(acc)
        if len(o_refs) == 1:
            o_refs[0][...] = res
        else:
            is_ctx = _mlp_tile(s) * MLP_TM < NCTX

            @pl.when(is_ctx)
            def _():
                o_refs[0][...] = res

            @pl.when(jnp.logical_not(is_ctx))
            def _():
                o_refs[1][...] = res


def _mlp(x, mod, layer, w1, w2, ln_g, ln_b, split_out=False):
    tm = MLP_TM
    nc = NCTX // tm
    if split_out:
        out_specs = [pl.BlockSpec((tm, D), lambda s: (jnp.minimum(_mlp_tile(s), nc - 1), 0)),
                     pl.BlockSpec((tm, D), lambda s: (jnp.maximum(_mlp_tile(s) - nc, 0), 0))]
        out_shape = [jax.ShapeDtypeStruct((NCTX, D), F32), jax.ShapeDtypeStruct((NLAT, D), F32)]
    else:
        out_specs = pl.BlockSpec((tm, D), lambda s: (_mlp_tile(s), 0))
        out_shape = jax.ShapeDtypeStruct((NTOK, D), F32)
    return pl.pallas_call(
        _mlp_kernel,
        grid=(MLP_NF - 1 + MLP_NT,),
        in_specs=[pl.BlockSpec((tm, D), lambda s: (_mlp_tile(s), 0)),
                  pl.BlockSpec((None, None, 6, D), lambda s: (layer, _mod_row(_mlp_tile(s) * tm), 0, 0)),
                  pl.BlockSpec((None, D, MLP_FC), lambda s: (layer, 0, jnp.minimum(s, MLP_NF - 1))),
                  pl.BlockSpec((None, MLP_FC, D), lambda s: (layer, jnp.minimum(s, MLP_NF - 1), 0)),
                  _full((1, D)), _full((1, D))],
        out_specs=out_specs,
        out_shape=out_shape,
        scratch_shapes=[pltpu.VMEM((MLP_NF, D, MLP_FC), BF),
                        pltpu.VMEM((MLP_NF, MLP_FC, D), BF),
                        pltpu.VMEM((tm, D), F32)],
        compiler_params=_cparams(1),
        name="mlp",
    )(x, mod, w1, w2, ln_g.reshape(1, D), ln_b.reshape(1, D))


OUT_TM = 512


def _out_kernel(uc_ref, ul_ref, w_ref, bias_ref, mod_ref, g_ref, b_ref, *refs):
    x_refs, o_ref = refs[:-1], refs[-1]
    is_ctx = pl.program_id(0) * OUT_TM < NCTX
    u = jnp.where(is_ctx, uc_ref[...], ul_ref[...])
    y = _dot(u, w_ref[...]) + bias_ref[...]
    m = mod_ref[...]
    o_ref[...] = _ln(ALPHA * _read_x(x_refs, OUT_TM) + m[2:3] * y, g_ref[...], b_ref[...])


def _out_proj(uc, ul, w, bias, x, mod, layer, ln_g, ln_b):
    tm = OUT_TM
    nc = NCTX // tm
    k = w.shape[0]
    x_specs, x_args = _x_specs(x, tm)
    return pl.pallas_call(
        _out_kernel,
        grid=(NTOK // tm,),
        in_specs=[pl.BlockSpec((tm, k), lambda t: (jnp.minimum(t, nc - 1), 0)),
                  pl.BlockSpec((tm, k), lambda t: (jnp.maximum(t - nc, 0), 0)),
                  pl.BlockSpec((k, D), lambda t: (0, 0), pipeline_mode=pl.Buffered(1)),
                  _full((1, D)),
                  _mod_spec(layer, tm),
                  _full((1, D)), _full((1, D))] + x_specs,
        out_specs=pl.BlockSpec((tm, D), lambda t: (t, 0)),
        out_shape=jax.ShapeDtypeStruct((NTOK, D), F32),
        compiler_params=_cparams(1),
        name="out_proj",
    )(uc, ul, w, bias.reshape(1, D), mod, ln_g.reshape(1, D), ln_b.reshape(1, D), *x_args)


ML_TM = 512
N_HP = H // 2
NG = 8


def _log_sigmoid(g):
    return jnp.minimum(g, 0.0) - jnp.log1p(jnp.exp(-jnp.abs(g)))


def _ml_proj_kernel(mod_ref, wqt_ref, wk_ref, wkt_ref, wvt_ref, wo_ref, wg_ref, wgt_ref, bg_ref, bgt_ref, *refs):
    x_refs = refs[:-7]
    qt_ref, k_ref, kt_ref, vt_ref, o_ref, g_ref, gt_ref = refs[-7:]
    m = mod_ref[...]
    h = (_read_x(x_refs, ML_TM) * (1.0 + m[1:2]) + m[0:1]).astype(BF)
    qt_ref[...] = _dot_nt(wqt_ref[...], h).astype(BF)
    k_ref[...] = (_dot(h, wk_ref[...]) * (DQK ** -0.5)).astype(BF)
    kt_ref[...] = (_dot_nt(wkt_ref[...], h) * (DQK ** -0.5)).astype(BF)
    vt_ref[...] = _dot_nt(wvt_ref[...], h).astype(BF)
    o_ref[...] = _dot(h, wo_ref[...])
    g = _dot(h, wg_ref[...]) + bg_ref[...]
    col = lax.broadcasted_iota(jnp.int32, g.shape, 1)
    g = jnp.where((col & 2) != 0, _log_sigmoid(g), g)
    gt = _dot_nt(wgt_ref[...], h) + bgt_ref[...]
    row = lax.broadcasted_iota(jnp.int32, gt.shape, 0)
    gt = jnp.where((row & 2) != 0, _log_sigmoid(gt), gt)
    for hp in range(N_HP):
        g_ref[hp] = g[:, hp * NG:(hp + 1) * NG]
        gt_ref[hp] = gt[hp * NG:(hp + 1) * NG, :]


def _ml_proj(x, mod, layer, wqt, wk, wkt, wvt, wo, wg, wgt, bg, bgt):
    tm = ML_TM
    res = lambda shape: pl.BlockSpec(shape, lambda t: (0,) * len(shape), pipeline_mode=pl.Buffered(1))
    x_specs, x_args = _x_specs(x, tm)
    return pl.pallas_call(
        _ml_proj_kernel,
        grid=(NTOK // tm,),
        in_specs=[_mod_spec(layer, tm),
                  res((QK, D)), res((D, QK)), res((QK, D)), res((V, D)), res((D, V)),
                  res((D, 4 * NG)), res((4 * NG, D)), res((1, 4 * NG)), res((4 * NG, 1))] + x_specs,
        out_specs=[pl.BlockSpec((QK, tm), lambda t: (0, t)),
                   pl.BlockSpec((tm, QK), lambda t: (t, 0)),
                   pl.BlockSpec((QK, tm), lambda t: (0, t)),
                   pl.BlockSpec((V, tm), lambda t: (0, t)),
                   pl.BlockSpec((tm, V), lambda t: (t, 0)),
                   pl.BlockSpec((N_HP, tm, NG), lambda t: (0, t, 0)),
                   pl.BlockSpec((N_HP, NG, tm), lambda t: (0, 0, t))],
        out_shape=[jax.ShapeDtypeStruct((QK, NTOK), BF),
                   jax.ShapeDtypeStruct((NTOK, QK), BF),
                   jax.ShapeDtypeStruct((QK, NTOK), BF),
                   jax.ShapeDtypeStruct((V, NTOK), BF),
                   jax.ShapeDtypeStruct((NTOK, V), F32),
                   jax.ShapeDtypeStruct((N_HP, NTOK, NG), F32),
                   jax.ShapeDtypeStruct((N_HP, NG, NTOK), F32)],
        compiler_params=_cparams(1),
        name="mlstm_proj",
    )(mod, wqt, wk, wkt, wvt, wo, wg, wgt, bg, bgt, *x_args)


DHA = DHV + 8


def _scan_chunk(k_c, kt_c, qt_c, vta, rr_col, rr_row, b_row, m, cta, d, need_state):
    L = k_c.shape[0]
    si = lax.broadcasted_iota(jnp.int32, (L, L), 0)
    ti = lax.broadcasted_iota(jnp.int32, (L, L), 1)
    mask = (si <= ti) if d == 0 else (si >= ti)
    rrm = jnp.where(mask, rr_col, -jnp.inf)
    g = jnp.maximum(jnp.max(rrm, axis=0, keepdims=True), m)
    st = _dot(k_c, qt_c) * jnp.exp(rrm - g)
    num = _dot(vta[:DHV], st.astype(BF))
    den = jnp.sum(st, axis=0, keepdims=True)
    if cta is not None:
        a = jnp.exp(m - g)
        qc = _dot(cta.astype(BF), qt_c)
        num = num + a * qc[:DHV]
        den = den + a * qc[DHV:DHV + 1]
    mt = b_row + g
    ht = num * (1.0 / jnp.maximum(jnp.abs(den), jnp.exp(-mt)))
    if not need_state:
        return ht, None, None
    last = L - 1 if d == 0 else 0
    m_new = mt[:, last:last + 1]
    b_last = b_row[:, last:last + 1]
    w = jnp.exp(b_last + rr_row - m_new)
    kw = (kt_c.astype(F32) * w).astype(BF)
    if cta is None:
        return ht, _dot_nt(kw, vta), m_new
    c_new = jnp.exp(b_last + m - m_new) * cta + _dot_nt(vta, kw)
    return ht, c_new, m_new


def _scan_kernel(*refs, is_ctx):
    if is_ctx:
        qt_ref, k_ref, kt_ref, vt_ref, o_ref, g_ref, gt_ref, ng_ref, u_ref, cst_ref, nst_ref, mst_ref, hbuf = refs
    else:
        qt_ref, k_ref, kt_ref, vt_ref, o_ref, g_ref, gt_ref, ng_ref, c0_ref, m0_ref, u_ref, hbuf = refs
    L = CHUNK
    nch = DEC_SEQ // L
    ri = lax.broadcasted_iota(jnp.int32, (L, L), 0)
    ci = lax.broadcasted_iota(jnp.int32, (L, L), 1)
    lower = jnp.where(ci <= ri, 1.0, 0.0).astype(BF)
    upper = jnp.where(ci >= ri, 1.0, 0.0).astype(BF)
    sub = lax.broadcasted_iota(jnp.int32, (DHA - DHV, L), 0)
    ones_rows = jnp.where(sub == 0, 1.0, 0.0).astype(BF)

    rr_cols, rr_rows, b_rows = [], [], []
    tparts = _split3(jnp.concatenate([gt_ref[:, c * L:(c + 1) * L] for c in range(nch)], axis=0))
    b_row_all = (sum(_dot(p, upper) for p in tparts), sum(_dot(p, lower) for p in tparts))
    for c in range(nch):
        gc = g_ref[c * L:(c + 1) * L, :]
        gtc = gt_ref[:, c * L:(c + 1) * L]
        gparts = _split3(gc)
        b_col = (sum(_dot(lower, p) for p in gparts), sum(_dot(upper, p) for p in gparts))
        b_row = tuple(b[c * NG:(c + 1) * NG, :] for b in b_row_all)
        rr_cols.append([gc[:, 0:NG - 2] - b[:, 2:NG] for b in b_col])
        rr_rows.append([gtc[0:NG - 2, :] - b[2:NG, :] for b in b_row])
        b_rows.append(b_row)

    for hh in range(2):
        for d in range(2):
            ig_i, lf_i = d * 4 + hh, d * 4 + 2 + hh
            if is_ctx:
                m, cta = jnp.zeros((1, 1), F32), None
            else:
                m = m0_ref[d, hh][:, 0:1]
                cta = c0_ref[d, hh]
            order = range(nch) if d == 0 else range(nch - 1, -1, -1)
            for n, c in enumerate(order):
                sl = slice(c * L, (c + 1) * L)
                hs = slice(hh * DQK, (hh + 1) * DQK)
                vta = jnp.concatenate([vt_ref[hh * DHV:(hh + 1) * DHV, sl], ones_rows], axis=0)
                need_state = is_ctx or n < nch - 1
                hcur, c_new, m_new = _scan_chunk(
                    k_ref[sl, hs], kt_ref[hs, sl], qt_ref[hs, sl], vta,
                    rr_cols[c][d][:, ig_i:ig_i + 1], rr_rows[c][d][ig_i:ig_i + 1, :],
                    b_rows[c][d][lf_i:lf_i + 1, :], m, cta, d, need_state)
                if d == 0:
                    hbuf[hh * DHV:(hh + 1) * DHV, sl] = hcur
                else:
                    hbuf[hh * DHV:(hh + 1) * DHV, sl] += hcur
                if is_ctx:
                    cst_ref[c, d, hh] = c_new[:, :DHV]
                    nst_ref[c, d, hh] = c_new[:, DHV:]
                    mst_ref[c, d, hh] = jnp.broadcast_to(m_new, (8, 128))
                else:
                    m, cta = m_new, c_new

    for hh in range(2):
        hsum = hbuf[hh * DHV:(hh + 1) * DHV, :]
        mu = jnp.mean(hsum, axis=0, keepdims=True)
        hc = hsum - mu
        var = jnp.mean(hc * hc, axis=0, keepdims=True)
        hn = hc * lax.rsqrt(var + LN_EPS) * ng_ref[hh * DHV:(hh + 1) * DHV, :]
        og = jax.nn.sigmoid(o_ref[:, hh * DHV:(hh + 1) * DHV])
        u_ref[:, hh * DHV:(hh + 1) * DHV] = (hn.T * og).astype(BF)


def _ml_scan(qt, k, kt, vt, o, g, gt, norm_g, c0aug=None, m0b=None):
    is_ctx = c0aug is None
    tt = DEC_SEQ
    toff = 0 if is_ctx else NCTX // tt
    nt = (NCTX if is_ctx else NLAT) // tt
    in_specs = [pl.BlockSpec((2 * DQK, tt), lambda t, p: (p, toff + t)),
                pl.BlockSpec((tt, 2 * DQK), lambda t, p: (toff + t, p)),
                pl.BlockSpec((2 * DQK, tt), lambda t, p: (p, toff + t)),
                pl.BlockSpec((2 * DHV, tt), lambda t, p: (p, toff + t)),
                pl.BlockSpec((tt, 2 * DHV), lambda t, p: (toff + t, p)),
                pl.BlockSpec((None, tt, NG), lambda t, p: (p, toff + t, 0)),
                pl.BlockSpec((None, NG, tt), lambda t, p: (p, 0, toff + t)),
                pl.BlockSpec((2 * DHV, 1), lambda t, p: (p, 0))]
    args = [qt, k, kt, vt, o, g, gt, norm_g.reshape(V, 1)]
    u_spec = pl.BlockSpec((tt, 2 * DHV), lambda t, p: (t, p))
    u_shape = jax.ShapeDtypeStruct((nt * tt, V), BF)
    if is_ctx:
        nseq = tt // SEQ
        out_specs = [u_spec,
                     pl.BlockSpec((nseq, 2, 2, DQK, DHV), lambda t, p: (t, 0, p, 0, 0)),
                     pl.BlockSpec((nseq, 2, 2, DQK, DHA - DHV), lambda t, p: (t, 0, p, 0, 0)),
                     pl.BlockSpec((nseq, 2, 2, 8, 128), lambda t, p: (t, 0, p, 0, 0))]
        out_shape = [u_shape,
                     jax.ShapeDtypeStruct((BATCH, 2, H, DQK, DHV), F32),
                     jax.ShapeDtypeStruct((BATCH, 2, H, DQK, DHA - DHV), F32),
                     jax.ShapeDtypeStruct((BATCH, 2, H, 8, 128), F32)]
    else:
        in_specs += [pl.BlockSpec((None, 2, 2, DHA, DQK), lambda t, p: (t, 0, p, 0, 0)),
                     pl.BlockSpec((None, 2, 2, 1, 128), lambda t, p: (t, 0, p, 0, 0))]
        args += [c0aug, m0b]
        out_specs = [u_spec]
        out_shape = [u_shape]
    return pl.pallas_call(
        functools.partial(_scan_kernel, is_ctx=is_ctx),
        grid=(nt, N_HP),
        in_specs=in_specs,
        out_specs=out_specs,
        out_shape=out_shape,
        scratch_shapes=[pltpu.VMEM((2 * DHV, tt), F32)],
        compiler_params=_cparams(2),
        name="mlstm_scan_ctx" if is_ctx else "mlstm_scan_lat",
    )(*args)


SC_TM = 512


SC_HALO = 8


def _sconv_kernel(x_ref, prev_ref, next_ref, mod_ref, w_in_ref, cw_ref, w_out_ref, g_ref, b_ref, o_ref, e_ref):
    tm = SC_TM
    row0 = pl.program_id(0) * tm
    slen = _seq_len(row0)
    pos = (row0 + lax.broadcasted_iota(jnp.int32, (tm, 1), 0)) & (slen - 1)
    m = mod_ref[...]
    x = x_ref[...]
    xe = jnp.concatenate([prev_ref[...], x, next_ref[...]], axis=0)
    he = (xe * (1.0 + m[1:2]) + m[0:1]).astype(BF)
    e_ref[...] = _dot(he, w_in_ref[:, D:2 * D]) * _dot(he, w_in_ref[:, 2 * D:3 * D])
    bg = _dot(he[SC_HALO:SC_HALO + tm], w_in_ref[:, 0:D])
    prev = jnp.where(pos == 0, 0.0, e_ref[SC_HALO - 1:SC_HALO - 1 + tm, :])
    nxt = jnp.where(pos == slen - 1, 0.0, e_ref[SC_HALO + 1:SC_HALO + 1 + tm, :])
    conv = cw_ref[0:1, :] * prev + cw_ref[1:2, :] * e_ref[SC_HALO:SC_HALO + tm, :] + cw_ref[2:3, :] * nxt
    y = _dot((bg * conv).astype(BF), w_out_ref[...])
    o_ref[...] = _ln(ALPHA * x + m[2:3] * y, g_ref[...], b_ref[...])


def _sconv(x, mod, layer, w_in, conv_w, w_out, ln_g, ln_b):
    tm = SC_TM
    hb = tm // SC_HALO
    nhb = NTOK // SC_HALO
    return pl.pallas_call(
        _sconv_kernel,
        grid=(NTOK // tm,),
        in_specs=[pl.BlockSpec((tm, D), lambda t: (t, 0)),
                  pl.BlockSpec((SC_HALO, D), lambda t: (jnp.maximum(t * hb - 1, 0), 0)),
                  pl.BlockSpec((SC_HALO, D), lambda t: (jnp.minimum((t + 1) * hb, nhb - 1), 0)),
                  _mod_spec(layer, tm),
                  pl.BlockSpec((D, 3 * D), lambda t: (0, 0), pipeline_mode=pl.Buffered(1)),
                  _full((3, D)),
                  pl.BlockSpec((D, D), lambda t: (0, 0), pipeline_mode=pl.Buffered(1)),
                  _full((1, D)), _full((1, D))],
        out_specs=pl.BlockSpec((tm, D), lambda t: (t, 0)),
        out_shape=jax.ShapeDtypeStruct((NTOK, D), F32),
        scratch_shapes=[pltpu.VMEM((tm + 2 * SC_HALO, D), F32)],
        compiler_params=_cparams(1),
        name="sconv",
    )(x, x, x, mod, w_in, conv_w, w_out, ln_g.reshape(1, D), ln_b.reshape(1, D))


PL_TM = 256
PL_HALO = 16


def _pool_kernel(x_ref, prev_ref, next_ref, mod_ref, w_ref, pb_ref, ps_ref, g_ref, b_ref, o_ref,
                 e_ref, s2_ref, s4_ref, s8_ref):
    tm, hl, gw = PL_TM, PL_HALO, GROUP_W
    row0 = pl.program_id(0) * tm
    slen = _seq_len(row0)
    pos0 = row0 & (slen - 1)
    m = mod_ref[...]
    x = x_ref[...]
    h = x * (1.0 + m[1:2]) + m[0:1]
    hp = prev_ref[...] * (1.0 + m[1:2]) + m[0:1]
    hn = next_ref[...] * (1.0 + m[1:2]) + m[0:1]
    n0 = tm + 2 * hl
    e_ref[0:hl, :] = jnp.where(pos0 != 0, hp, 0.0)
    e_ref[hl:hl + tm, :] = h
    e_ref[hl + tm:n0, :] = jnp.where(pos0 + tm != slen, hn, 0.0)
    e_ref[n0:, :] = jnp.zeros((8, D), F32)
    n2, n4, n8 = n0, n0 - 8, n0 - 16
    s2_ref[...] = e_ref[0:n2, :] + e_ref[1:n2 + 1, :]
    s4_ref[...] = s2_ref[0:n4, gw:] + s2_ref[2:n4 + 2, gw:]
    s8_ref[...] = s4_ref[0:n8, gw:] + s4_ref[4:n8 + 4, gw:]
    tots = [s2_ref[hl - 1:hl - 1 + tm, 0:gw],
            s4_ref[hl - 2:hl - 2 + tm, 0:gw],
            s8_ref[hl - 4:hl - 4 + tm, 0:gw],
            s8_ref[hl - 8:hl - 8 + tm, gw:] + s8_ref[hl:hl + tm, gw:]]
    pos = pos0 + lax.broadcasted_iota(jnp.int32, (tm, 1), 0)
    ys = []
    for gi, win in enumerate(POOL_WINDOWS):
        cs = slice(gi * gw, (gi + 1) * gw)
        back = win // 2
        cnt = jnp.minimum(pos + (win - back), slen) - jnp.maximum(pos - back, 0)
        p = tots[gi] / cnt.astype(F32) - h[:, cs]
        ys.append(_dot(p.astype(BF), w_ref[gi]))
    y = (jnp.concatenate(ys, axis=1) + pb_ref[...]) * ps_ref[...]
    o_ref[...] = _ln(ALPHA * x + m[2:3] * y, g_ref[...], b_ref[...])


def _pool(x, mod, layer, w, pb, ps, ln_g, ln_b):
    assert POOL_WINDOWS == (2, 4, 8, 16) and PL_HALO >= POOL_WINDOWS[-1] // 2
    tm = PL_TM
    hb = tm // PL_HALO
    nhb = NTOK // PL_HALO
    n0 = tm + 2 * PL_HALO
    return pl.pallas_call(
        _pool_kernel,
        grid=(NTOK // tm,),
        in_specs=[pl.BlockSpec((tm, D), lambda t: (t, 0)),
                  pl.BlockSpec((PL_HALO, D), lambda t: (jnp.maximum(t * hb - 1, 0), 0)),
                  pl.BlockSpec((PL_HALO, D), lambda t: (jnp.minimum((t + 1) * hb, nhb - 1), 0)),
                  _mod_spec(layer, tm),
                  _full((N_GROUPS, GROUP_W, GROUP_W)),
                  _full((1, D)), _full((1, D)), _full((1, D)), _full((1, D))],
        out_specs=pl.BlockSpec((tm, D), lambda t: (t, 0)),
        out_shape=jax.ShapeDtypeStruct((NTOK, D), F32),
        scratch_shapes=[pltpu.VMEM((n0 + 8, D), F32),
                        pltpu.VMEM((n0, D), F32),
                        pltpu.VMEM((n0 - 8, D - GROUP_W), F32),
                        pltpu.VMEM((n0 - 16, D - 2 * GROUP_W), F32)],
        compiler_params=_cparams(1),
        name="pool",
    )(x, x, x, mod, w, pb.reshape(1, D), ps.reshape(1, D), ln_g.reshape(1, D), ln_b.reshape(1, D))


FT_TK = 256


def _dft_mats(n, scale):
    k = np.arange(n, dtype=np.int64)
    ang = 2.0 * np.pi * ((k[:, None] * k[None, :]) % n).astype(np.float64) / n
    return np.stack([np.cos(ang) * scale, np.sin(ang) * scale]).astype(np.float32)


def _dot_x3(a_parts, b_parts):
    (ah, al), (bh, bl) = a_parts, b_parts
    return _dot(ah, bh) + (_dot(ah, bl) + _dot(al, bh))


def _fourier_kernel(x_ref, mod_ref, cs_ref, cc_ref, o_ref):
    m = mod_ref[...]
    hparts = _split2(x_ref[...] * (1.0 + m[1:2]) + m[0:1])
    ac = _dot_x3(_split2(cs_ref[0]), hparts)
    as_ = _dot_x3(_split2(cs_ref[1]), hparts)
    ccp = _split2(cc_ref[0])
    scp = _split2(cc_ref[1])
    outs = []
    for gi in range(N_GROUPS):
        cs = slice(gi * GROUP_W, (gi + 1) * GROUP_W)
        outs.append(_dot_x3(_split2(ac[:, cs]), ccp) - _dot_x3(_split2(as_[:, cs]), scp))
    o_ref[...] = jnp.concatenate(outs, axis=1).astype(BF)


def _fourier(x, mod, layer, s, nseq, row_off, mod_row_fn):
    tk = FT_TK
    seq_off = row_off // s
    cs = jnp.asarray(_dft_mats(s, s ** -0.5))
    cc = jnp.asarray(_dft_mats(GROUP_W, GROUP_W ** -0.5))
    return pl.pallas_call(
        _fourier_kernel,
        grid=(nseq, s // tk),
        in_specs=[pl.BlockSpec((s, D), lambda b, j: (seq_off + b, 0)),
                  pl.BlockSpec((None, None, 6, D), lambda b, j: (layer, mod_row_fn(b), 0, 0)),
                  pl.BlockSpec((2, tk, s), lambda b, j: (0, j, 0)),
                  _full((2, GROUP_W, GROUP_W))],
        out_specs=pl.BlockSpec((tk, D), lambda b, j: (b * (s // tk) + j, 0)),
        out_shape=jax.ShapeDtypeStruct((nseq * s, D), BF),
        compiler_params=_cparams(2),
        name="fourier_%d" % s,
    )(x, mod, cs, cc)


def _pos_embed():
    rows = DEC_SEQ // GRID_W
    quarter = D // 4
    omega = 1.0 / (10000.0 ** (np.arange(quarter, dtype=np.float64) / quarter))
    rr, cc = np.meshgrid(np.arange(rows, dtype=np.float64), np.arange(GRID_W, dtype=np.float64), indexing="ij")
    er = rr.reshape(-1, 1) * omega
    ec = cc.reshape(-1, 1) * omega
    return np.concatenate([np.sin(er), np.cos(er), np.sin(ec), np.cos(ec)], axis=-1).astype(np.float32)


def _gate_perm():
    perm = np.zeros(4 * H, dtype=np.int32)
    for hp in range(N_HP):
        for d in range(2):
            for j in range(2):
                for hh in range(2):
                    perm[hp * NG + d * 4 + j * 2 + hh] = d * 2 * H + j * H + 2 * hp + hh
    return perm


def kernel(x_prompt, x_sample, state_C, state_n, state_m, c, c_ctx, w_mod, b_mod, ln_g, ln_b, mlp_w1, mlp_w2,
           ml_w_in, ml_w_gate, ml_b_gate, ml_norm_g, ml_w_out, sc_w_in, sc_conv_w, sc_w_out,
           pl_w, pl_b, pl_scale, ft_w_out, ft_b_out):
    cond = jnp.concatenate([c_ctx[None, :], c, jnp.zeros((NCOND - 1 - DEC_BATCH, D), F32)], axis=0)
    mod = _modulation(cond, w_mod, b_mod)

    x = (x_prompt.reshape(NCTX, D), x_sample.reshape(NLAT, D), jnp.asarray(_pos_embed()))

    new_c = new_n = new_m = None
    for i in range(DEPTH):
        kind, j = i % 4, i // 4
        if kind == 0:
            w_in = ml_w_in[j].astype(BF)
            perm = _gate_perm()
            wg = ml_w_gate[j][:, perm].astype(BF)
            bg = ml_b_gate[j][perm]
            wk = w_in[:, QK:2 * QK]
            qt, k, kt, vt, o, g, gt = _ml_proj(
                x, mod, i, w_in[:, :QK].T, wk, wk.T, w_in[:, 2 * QK:2 * QK + V].T, w_in[:, 2 * QK + V:],
                wg, wg.T, bg.reshape(1, -1), bg.reshape(-1, 1))
            uc, cst, nst, mst = _ml_scan(qt, k, kt, vt, o, g, gt, ml_norm_g[j])
            c0 = jnp.concatenate([jnp.swapaxes(state_C[:, j], -1, -2), state_n[:, j][..., None, :],
                                  jnp.zeros((DEC_BATCH, 2, H, DHA - DHV - 1, DQK), F32)], axis=-2)
            m0 = jnp.broadcast_to(state_m[:, j][..., None, None], (DEC_BATCH, 2, H, 1, 128))
            (ul,) = _ml_scan(qt, k, kt, vt, o, g, gt, ml_norm_g[j], c0, m0)
            x = _out_proj(uc, ul, ml_w_out[j].astype(BF), jnp.zeros((D,), F32), x, mod, i, ln_g[i, 0], ln_b[i, 0])
            new_c = cst[:, None]
            new_n = nst[..., 0][:, None]
            new_m = mst[..., 0, 0][:, None]
        elif kind == 1:
            x = _sconv(x, mod, i, sc_w_in[j].astype(BF), sc_conv_w[j], sc_w_out[j].astype(BF),
                       ln_g[i, 0], ln_b[i, 0])
        elif kind == 2:
            x = _pool(x, mod, i, pl_w[j].astype(BF), pl_b[j], pl_scale[j], ln_g[i, 0], ln_b[i, 0])
        else:
            fc = _fourier(x, mod, i, SEQ, BATCH, 0, lambda b: 0)
            fl = _fourier(x, mod, i, DEC_SEQ, DEC_BATCH, NCTX, lambda b: 1 + b)
            x = _out_proj(fc, fl, ft_w_out[j].astype(BF), ft_b_out[j], x, mod, i, ln_g[i, 0], ln_b[i, 0])
        x = _mlp(x, mod, i, mlp_w1, mlp_w2, ln_g[i, 1], ln_b[i, 1],
                 split_out=(i == DEPTH - 1))

    y_prompt = x[0].reshape(BATCH, SEQ, D)
    y_sample = x[1].reshape(DEC_BATCH, DEC_SEQ, D)
    return y_prompt, y_sample, new_c, new_n, new_m
```

```python
import functools
import math

import numpy as np
import jax
import jax.numpy as jnp
from jax import lax
from jax.experimental import pallas as pl
from jax.experimental.pallas import tpu as pltpu

D = 1024
BATCH, SEQ = 16, 256
DEC_BATCH, DEC_SEQ = 4, 1024
DEPTH = 4
GRID_W = 64
H = 8
DQK = 64
DHV = 128
QK = H * DQK
V = H * DHV
N_GROUPS = 4
GROUP_W = D // N_GROUPS
POOL_WINDOWS = (2, 4, 8, 16)
D_FF = 4 * D
ALPHA = (2.0 * DEPTH) ** 0.25
LN_EPS = 1e-5
F32 = jnp.float32
BF = jnp.bfloat16

NCTX = BATCH * SEQ
NLAT = DEC_BATCH * DEC_SEQ
NTOK = NCTX + NLAT
NCOND = 8
CHUNK = 256
VMEM_LIMIT = 56 * 1024 * 1024


def _cparams(n_axes):
    return pltpu.CompilerParams(dimension_semantics=("arbitrary",) * n_axes,
                                vmem_limit_bytes=VMEM_LIMIT)


def _mod_row(row0):
    return jnp.where(row0 < NCTX, 0, 1 + (row0 - NCTX) // DEC_SEQ)


def _seq_len(row0):
    return jnp.where(row0 < NCTX, SEQ, DEC_SEQ)


def _ln(z, g, b):
    mu = jnp.mean(z, axis=-1, keepdims=True)
    zc = z - mu
    var = jnp.mean(zc * zc, axis=-1, keepdims=True)
    return zc * lax.rsqrt(var + LN_EPS) * g + b


def _dot(a, b):
    return jnp.dot(a, b, preferred_element_type=F32)


def _dot_nt(a, b):
    return lax.dot_general(a, b, (((1,), (1,)), ((), ())), preferred_element_type=F32)


def _split2(x):
    hi = x.astype(BF)
    lo = (x - hi.astype(F32)).astype(BF)
    return hi, lo


def _split3(x):
    hi = x.astype(BF)
    r = x - hi.astype(F32)
    mid = r.astype(BF)
    lo = (r - mid.astype(F32)).astype(BF)
    return hi, mid, lo


def _full(shape):
    n = len(shape)
    return pl.BlockSpec(shape, lambda *_: (0,) * n)


def _x_specs(x, tm):
    if not isinstance(x, tuple):
        return [pl.BlockSpec((tm, D), lambda t: (t, 0))], [x]
    nc = NCTX // tm
    return ([pl.BlockSpec((tm, D), lambda t: (jnp.minimum(t, nc - 1), 0)),
             pl.BlockSpec((tm, D), lambda t: (jnp.maximum(t - nc, 0), 0)),
             pl.BlockSpec((tm, D), lambda t: (t % (DEC_SEQ // tm), 0))], list(x))


def _read_x(x_refs, tm):
    if len(x_refs) == 1:
        return x_refs[0][...]
    xc_ref, xl_ref, pe_ref = x_refs
    return jnp.where(pl.program_id(0) * tm < NCTX, xc_ref[...], xl_ref[...] + pe_ref[...])


def _mod_spec(layer, tm):
    return pl.BlockSpec((None, None, 6, D), lambda t: (layer, _mod_row(t * tm), 0, 0))


def _mod_kernel(c_ref, w_ref, b_ref, o_ref):
    c = c_ref[...]
    s = c * jax.nn.sigmoid(c)
    o_ref[...] = _dot(s.astype(BF), w_ref[...].astype(BF)) + b_ref[...]


def _modulation(cond, w_mod, b_mod):
    tn = 1536
    out = pl.pallas_call(
        _mod_kernel,
        grid=(DEPTH, 6 * D // tn),
        in_specs=[pl.BlockSpec((NCOND, D), lambda i, j: (0, 0)),
                  pl.BlockSpec((None, D, tn), lambda i, j: (i, 0, j)),
                  pl.BlockSpec((None, 1, tn), lambda i, j: (i, 0, j))],
        out_specs=pl.BlockSpec((None, NCOND, tn), lambda i, j: (i, 0, j)),
        out_shape=jax.ShapeDtypeStruct((DEPTH, NCOND, 6 * D), F32),
        compiler_params=_cparams(2),
        name="modulation",
    )(cond, w_mod, b_mod.reshape(DEPTH, 1, 6 * D))
    return out.reshape(DEPTH, NCOND, 6, D)


MLP_TM = 512
MLP_FC = 512
MLP_NF = D_FF // MLP_FC
MLP_NT = NTOK // MLP_TM


def _mlp_tile(s):
    return jnp.maximum(s - (MLP_NF - 1), 0)


def _mlp_kernel(mod_ref, w1_ref, w2_ref, g_ref, b_ref, x1_ref, *refs, nx):
    x_refs, o_refs, (w1s_ref, w2s_ref, acc_ref) = refs[:nx], refs[nx:-3], refs[-3:]
    s = pl.program_id(0)
    is_ctx = _mlp_tile(s) * MLP_TM < NCTX
    m = mod_ref[...]

    def pre(x):
        return (x * (1.0 + m[4:5]) + m[3:4]).astype(BF)

    def hidden(h, w1c):
        return jnp.square(jnp.maximum(_dot(h, w1c), 0.0)).astype(BF)

    def result(x, acc):
        return _ln(ALPHA * x + m[5:6] * acc, g_ref[...], b_ref[...])

    def read_x():
        if nx == 1:
            return x_refs[0][...]
        return jnp.where(is_ctx, x_refs[0][...], x_refs[1][...])

    def accumulate(i, first, part):
        @pl.when(first)
        def _():
            acc_ref[i] = part

        @pl.when(jnp.logical_not(first))
        def _():
            acc_ref[i] += part

    @pl.when(s < MLP_NF)
    def _():
        x = read_x()
        w1c = w1_ref[...].astype(BF)
        w2c = w2_ref[...].astype(BF)
        w1s_ref[s] = w1c
        w2s_ref[s] = w2c
        accumulate(0, s == 0, _dot(hidden(pre(x), w1c), w2c))

        @pl.when(s == MLP_NF - 1)
        def _():
            o_refs[0][...] = result(x, acc_ref[0])

    @pl.when(jnp.logical_and(s >= 1, s <= MLP_NF))
    def _():
        x = x1_ref[...]
        accumulate(1, s == 1, _dot(hidden(pre(x), w1s_ref[s - 1]), w2s_ref[s - 1]))

        @pl.when(s == MLP_NF)
        def _():
            o_refs[0][...] = result(x, acc_ref[1])

    @pl.when(s > MLP_NF)
    def _():
        x = read_x()
        h = pre(x)
        acc = jnp.zeros((MLP_TM, D), F32)
        for f in range(MLP_NF):
            acc = acc + _dot(hidden(h, w1s_ref[f]), w2s_ref[f])
        res = result(x, acc)
        if len(o_refs) == 1:
            o_refs[0][...] = res
        else:
            @pl.when(is_ctx)
            def _():
                o_refs[0][...] = res

            @pl.when(jnp.logical_not(is_ctx))
            def _():
                o_refs[1][...] = res


def _mlp(x, mod, layer, w1, w2, ln_g, ln_b, split_out=False):
    tm = MLP_TM
    nc = NCTX // tm
    assert nc >= 2 and MLP_NT >= 2
    half_specs = [pl.BlockSpec((tm, D), lambda s: (jnp.minimum(_mlp_tile(s), nc - 1), 0)),
                  pl.BlockSpec((tm, D), lambda s: (jnp.maximum(_mlp_tile(s) - nc, 0), 0))]
    if isinstance(x, tuple):
        x_specs, x_args = half_specs, list(x)
    else:
        x_specs, x_args = [pl.BlockSpec((tm, D), lambda s: (_mlp_tile(s), 0))], [x]
    if split_out:
        out_specs = half_specs
        out_shape = [jax.ShapeDtypeStruct((NCTX, D), F32), jax.ShapeDtypeStruct((NLAT, D), F32)]
    else:
        out_specs = pl.BlockSpec((tm, D), lambda s: (_mlp_tile(s), 0))
        out_shape = jax.ShapeDtypeStruct((NTOK, D), F32)
    return pl.pallas_call(
        functools.partial(_mlp_kernel, nx=len(x_args)),
        grid=(MLP_NF - 1 + MLP_NT,),
        in_specs=[pl.BlockSpec((None, None, 6, D), lambda s: (layer, _mod_row(_mlp_tile(s) * tm), 0, 0)),
                  pl.BlockSpec((None, D, MLP_FC), lambda s: (layer, 0, jnp.minimum(s, MLP_NF - 1))),
                  pl.BlockSpec((None, MLP_FC, D), lambda s: (layer, jnp.minimum(s, MLP_NF - 1), 0)),
                  _full((1, D)), _full((1, D)),
                  pl.BlockSpec((tm, D), lambda s: (1, 0))] + x_specs,
        out_specs=out_specs,
        out_shape=out_shape,
        scratch_shapes=[pltpu.VMEM((MLP_NF, D, MLP_FC), BF),
                        pltpu.VMEM((MLP_NF, MLP_FC, D), BF),
                        pltpu.VMEM((2, tm, D), F32)],
        compiler_params=_cparams(1),
        name="mlp",
    )(mod, w1, w2, ln_g.reshape(1, D), ln_b.reshape(1, D), x_args[0], *x_args)


OUT_TM = 512


def _out_kernel(uc_ref, ul_ref, w_ref, bias_ref, mod_ref, g_ref, b_ref, *refs):
    x_refs, o_ref = refs[:-1], refs[-1]
    is_ctx = pl.program_id(0) * OUT_TM < NCTX
    u = jnp.where(is_ctx, uc_ref[...], ul_ref[...])
    y = _dot(u, w_ref[...]) + bias_ref[...]
    m = mod_ref[...]
    o_ref[...] = _ln(ALPHA * _read_x(x_refs, OUT_TM) + m[2:3] * y, g_ref[...], b_ref[...])


def _out_proj(uc, ul, w, bias, x, mod, layer, ln_g, ln_b):
    tm = OUT_TM
    nc = NCTX // tm
    k = w.shape[0]
    x_specs, x_args = _x_specs(x, tm)
    return pl.pallas_call(
        _out_kernel,
        grid=(NTOK // tm,),
        in_specs=[pl.BlockSpec((tm, k), lambda t: (jnp.minimum(t, nc - 1), 0)),
                  pl.BlockSpec((tm, k), lambda t: (jnp.maximum(t - nc, 0), 0)),
                  pl.BlockSpec((k, D), lambda t: (0, 0), pipeline_mode=pl.Buffered(1)),
                  _full((1, D)),
                  _mod_spec(layer, tm),
                  _full((1, D)), _full((1, D))] + x_specs,
        out_specs=pl.BlockSpec((tm, D), lambda t: (t, 0)),
        out_shape=jax.ShapeDtypeStruct((NTOK, D), F32),
        compiler_params=_cparams(1),
        name="out_proj",
    )(uc, ul, w, bias.reshape(1, D), mod, ln_g.reshape(1, D), ln_b.reshape(1, D), *x_args)


ML_TM = 512
N_HP = H // 2
NG = 8


def _log_sigmoid(g):
    return jnp.minimum(g, 0.0) - jnp.log1p(jnp.exp(-jnp.abs(g)))


def _ml_proj_kernel(mod_ref, wqt_ref, wk_ref, wkt_ref, wvt_ref, wo_ref, wg_ref, wgt_ref, bg_ref, bgt_ref, *refs):
    x_refs = refs[:-7]
    qt_ref, k_ref, kt_ref, vt_ref, o_ref, g_ref, gt_ref = refs[-7:]
    m = mod_ref[...]
    h = (_read_x(x_refs, ML_TM) * (1.0 + m[1:2]) + m[0:1]).astype(BF)
    qt_ref[...] = _dot_nt(wqt_ref[...], h).astype(BF)
    k_ref[...] = (_dot(h, wk_ref[...]) * (DQK ** -0.5)).astype(BF)
    kt_ref[...] = (_dot_nt(wkt_ref[...], h) * (DQK ** -0.5)).astype(BF)
    vt_ref[...] = _dot_nt(wvt_ref[...], h).astype(BF)
    o_ref[...] = _dot(h, wo_ref[...])
    g = _dot(h, wg_ref[...]) + bg_ref[...]
    col = lax.broadcasted_iota(jnp.int32, g.shape, 1)
    g = jnp.where((col & 2) != 0, _log_sigmoid(g), g)
    gt = _dot_nt(wgt_ref[...], h) + bgt_ref[...]
    row = lax.broadcasted_iota(jnp.int32, gt.shape, 0)
    gt = jnp.where((row & 2) != 0, _log_sigmoid(gt), gt)
    for hp in range(N_HP):
        g_ref[hp] = g[:, hp * NG:(hp + 1) * NG]
        gt_ref[hp] = gt[hp * NG:(hp + 1) * NG, :]


def _ml_proj(x, mod, layer, wqt, wk, wkt, wvt, wo, wg, wgt, bg, bgt):
    tm = ML_TM
    res = lambda shape: pl.BlockSpec(shape, lambda t: (0,) * len(shape), pipeline_mode=pl.Buffered(1))
    x_specs, x_args = _x_specs(x, tm)
    return pl.pallas_call(
        _ml_proj_kernel,
        grid=(NTOK // tm,),
        in_specs=[_mod_spec(layer, tm),
                  res((QK, D)), res((D, QK)), res((QK, D)), res((V, D)), res((D, V)),
                  res((D, 4 * NG)), res((4 * NG, D)), res((1, 4 * NG)), res((4 * NG, 1))] + x_specs,
        out_specs=[pl.BlockSpec((QK, tm), lambda t: (0, t)),
                   pl.BlockSpec((tm, QK), lambda t: (t, 0)),
                   pl.BlockSpec((QK, tm), lambda t: (0, t)),
                   pl.BlockSpec((V, tm), lambda t: (0, t)),
                   pl.BlockSpec((tm, V), lambda t: (t, 0)),
                   pl.BlockSpec((N_HP, tm, NG), lambda t: (0, t, 0)),
                   pl.BlockSpec((N_HP, NG, tm), lambda t: (0, 0, t))],
        out_shape=[jax.ShapeDtypeStruct((QK, NTOK), BF),
                   jax.ShapeDtypeStruct((NTOK, QK), BF),
                   jax.ShapeDtypeStruct((QK, NTOK), BF),
                   jax.ShapeDtypeStruct((V, NTOK), BF),
                   jax.ShapeDtypeStruct((NTOK, V), F32),
                   jax.ShapeDtypeStruct((N_HP, NTOK, NG), F32),
                   jax.ShapeDtypeStruct((N_HP, NG, NTOK), F32)],
        compiler_params=_cparams(1),
        name="mlstm_proj",
    )(mod, wqt, wk, wkt, wvt, wo, wg, wgt, bg, bgt, *x_args)


DHA = DHV + 8


def _scan_chunk(k_c, kt_c, qt_c, vta, rr_col, rr_row, b_row, m, cta, d, need_state):
    L = k_c.shape[0]
    si = lax.broadcasted_iota(jnp.int32, (L, L), 0)
    ti = lax.broadcasted_iota(jnp.int32, (L, L), 1)
    mask = (si <= ti) if d == 0 else (si >= ti)
    rrm = jnp.where(mask, rr_col, -jnp.inf)
    g = jnp.maximum(jnp.max(rrm, axis=0, keepdims=True), m)
    st = _dot(k_c, qt_c) * jnp.exp(rrm - g)
    num = _dot(vta[:DHV], st.astype(BF))
    den = jnp.sum(st, axis=0, keepdims=True)
    if cta is not None:
        a = jnp.exp(m - g)
        qc = _dot(cta.astype(BF), qt_c)
        num = num + a * qc[:DHV]
        den = den + a * qc[DHV:DHV + 1]
    mt = b_row + g
    ht = num * (1.0 / jnp.maximum(jnp.abs(den), jnp.exp(-mt)))
    if not need_state:
        return ht, None, None
    last = L - 1 if d == 0 else 0
    m_new = mt[:, last:last + 1]
    b_last = b_row[:, last:last + 1]
    w = jnp.exp(b_last + rr_row - m_new)
    kw = (kt_c.astype(F32) * w).astype(BF)
    if cta is None:
        return ht, _dot_nt(kw, vta), m_new
    c_new = jnp.exp(b_last + m - m_new) * cta + _dot_nt(vta, kw)
    return ht, c_new, m_new


def _scan_kernel(qt_ref, k_ref, kt_ref, vt_ref, o_ref, g_ref, gt_ref, ng_ref,
                 wo_ref, mod_ref, lg_ref, lb_ref, x_ref, *refs, is_ctx, has_pe):
    refs = list(refs)
    pe_ref = refs.pop(0) if has_pe else None
    if is_ctx:
        xo_ref, cst_ref, nst_ref, mst_ref, hbuf, acc_ref = refs
    else:
        c0_ref, m0_ref, xo_ref, hbuf, acc_ref = refs
    L = CHUNK
    nch = DEC_SEQ // L
    ri = lax.broadcasted_iota(jnp.int32, (L, L), 0)
    ci = lax.broadcasted_iota(jnp.int32, (L, L), 1)
    lower = jnp.where(ci <= ri, 1.0, 0.0).astype(BF)
    upper = jnp.where(ci >= ri, 1.0, 0.0).astype(BF)
    sub = lax.broadcasted_iota(jnp.int32, (DHA - DHV, L), 0)
    ones_rows = jnp.where(sub == 0, 1.0, 0.0).astype(BF)

    rr_cols, rr_rows, b_rows = [], [], []
    tparts = _split3(jnp.concatenate([gt_ref[:, c * L:(c + 1) * L] for c in range(nch)], axis=0))
    b_row_all = (sum(_dot(p, upper) for p in tparts), sum(_dot(p, lower) for p in tparts))
    for c in range(nch):
        gc = g_ref[c * L:(c + 1) * L, :]
        gtc = gt_ref[:, c * L:(c + 1) * L]
        gparts = _split3(gc)
        b_col = (sum(_dot(lower, p) for p in gparts), sum(_dot(upper, p) for p in gparts))
        b_row = tuple(b[c * NG:(c + 1) * NG, :] for b in b_row_all)
        rr_cols.append([gc[:, 0:NG - 2] - b[:, 2:NG] for b in b_col])
        rr_rows.append([gtc[0:NG - 2, :] - b[2:NG, :] for b in b_row])
        b_rows.append(b_row)

    for hh in range(2):
        for d in range(2):
            ig_i, lf_i = d * 4 + hh, d * 4 + 2 + hh
            if is_ctx:
                m, cta = jnp.zeros((1, 1), F32), None
            else:
                m = m0_ref[d, hh][:, 0:1]
                cta = c0_ref[d, hh]
            order = range(nch) if d == 0 else range(nch - 1, -1, -1)
            for n, c in enumerate(order):
                sl = slice(c * L, (c + 1) * L)
                hs = slice(hh * DQK, (hh + 1) * DQK)
                vta = jnp.concatenate([vt_ref[hh * DHV:(hh + 1) * DHV, sl], ones_rows], axis=0)
                need_state = is_ctx or n < nch - 1
                hcur, c_new, m_new = _scan_chunk(
                    k_ref[sl, hs], kt_ref[hs, sl], qt_ref[hs, sl], vta,
                    rr_cols[c][d][:, ig_i:ig_i + 1], rr_rows[c][d][ig_i:ig_i + 1, :],
                    b_rows[c][d][lf_i:lf_i + 1, :], m, cta, d, need_state)
                if d == 0:
                    hbuf[hh * DHV:(hh + 1) * DHV, sl] = hcur
                else:
                    hbuf[hh * DHV:(hh + 1) * DHV, sl] += hcur
                if is_ctx:
                    cst_ref[c, d, hh] = c_new[:, :DHV]
                    nst_ref[c, d, hh] = c_new[:, DHV:]
                    mst_ref[c, d, hh] = jnp.broadcast_to(m_new, (8, 128))
                else:
                    m, cta = m_new, c_new

    us = []
    for hh in range(2):
        hsum = hbuf[hh * DHV:(hh + 1) * DHV, :]
        mu = jnp.mean(hsum, axis=0, keepdims=True)
        hc = hsum - mu
        var = jnp.mean(hc * hc, axis=0, keepdims=True)
        hn = hc * lax.rsqrt(var + LN_EPS) * ng_ref[hh * DHV:(hh + 1) * DHV, :]
        og = jax.nn.sigmoid(o_ref[:, hh * DHV:(hh + 1) * DHV])
        us.append((hn.T * og).astype(BF))
    part = _dot(jnp.concatenate(us, axis=1), wo_ref[...])
    p = pl.program_id(1)

    @pl.when(p == 0)
    def _():
        acc_ref[...] = part

    @pl.when(jnp.logical_and(p > 0, p < N_HP - 1))
    def _():
        acc_ref[...] += part

    @pl.when(p == N_HP - 1)
    def _():
        x = x_ref[...] if pe_ref is None else x_ref[...] + pe_ref[...]
        y = acc_ref[...] + part
        xo_ref[...] = _ln(ALPHA * x + mod_ref[2:3, :] * y, lg_ref[...], lb_ref[...])


def _ml_scan(qt, k, kt, vt, o, g, gt, norm_g, w_out, mod, layer, ln_g, ln_b, x, c0aug=None, m0b=None):
    is_ctx = c0aug is None
    tt = DEC_SEQ
    toff = 0 if is_ctx else NCTX // tt
    nt = (NCTX if is_ctx else NLAT) // tt
    if isinstance(x, tuple):
        x_arr, x_off, pe = (x[0], 0, None) if is_ctx else (x[1], 0, x[2])
    else:
        x_arr, x_off, pe = x, toff, None
    in_specs = [pl.BlockSpec((2 * DQK, tt), lambda t, p: (p, toff + t)),
                pl.BlockSpec((tt, 2 * DQK), lambda t, p: (toff + t, p)),
                pl.BlockSpec((2 * DQK, tt), lambda t, p: (p, toff + t)),
                pl.BlockSpec((2 * DHV, tt), lambda t, p: (p, toff + t)),
                pl.BlockSpec((tt, 2 * DHV), lambda t, p: (toff + t, p)),
                pl.BlockSpec((None, tt, NG), lambda t, p: (p, toff + t, 0)),
                pl.BlockSpec((None, NG, tt), lambda t, p: (p, 0, toff + t)),
                pl.BlockSpec((2 * DHV, 1), lambda t, p: (p, 0)),
                pl.BlockSpec((2 * DHV, D), lambda t, p: (p, 0)),
                pl.BlockSpec((None, None, 6, D), lambda t, p: (layer, _mod_row((toff + t) * tt), 0, 0)),
                _full((1, D)), _full((1, D)),
                pl.BlockSpec((tt, D), lambda t, p: (x_off + t, 0))]
    args = [qt, k, kt, vt, o, g, gt, norm_g.reshape(V, 1), w_out, mod,
            ln_g.reshape(1, D), ln_b.reshape(1, D), x_arr]
    if pe is not None:
        in_specs.append(pl.BlockSpec((tt, D), lambda t, p: (0, 0), pipeline_mode=pl.Buffered(1)))
        args.append(pe)
    u_spec = pl.BlockSpec((tt, D), lambda t, p: (t, 0))
    u_shape = jax.ShapeDtypeStruct((nt * tt, D), F32)
    if is_ctx:
        nseq = tt // SEQ
        out_specs = [u_spec,
                     pl.BlockSpec((nseq, 2, 2, DQK, DHV), lambda t, p: (t, 0, p, 0, 0)),
                     pl.BlockSpec((nseq, 2, 2, DQK, DHA - DHV), lambda t, p: (t, 0, p, 0, 0)),
                     pl.BlockSpec((nseq, 2, 2, 8, 128), lambda t, p: (t, 0, p, 0, 0))]
        out_shape = [u_shape,
                     jax.ShapeDtypeStruct((BATCH, 2, H, DQK, DHV), F32),
                     jax.ShapeDtypeStruct((BATCH, 2, H, DQK, DHA - DHV), F32),
                     jax.ShapeDtypeStruct((BATCH, 2, H, 8, 128), F32)]
    else:
        in_specs += [pl.BlockSpec((None, 2, 2, DHA, DQK), lambda t, p: (t, 0, p, 0, 0)),
                     pl.BlockSpec((None, 2, 2, 1, 128), lambda t, p: (t, 0, p, 0, 0))]
        args += [c0aug, m0b]
        out_specs = [u_spec]
        out_shape = [u_shape]
    return pl.pallas_call(
        functools.partial(_scan_kernel, is_ctx=is_ctx, has_pe=pe is not None),
        grid=(nt, N_HP),
        in_specs=in_specs,
        out_specs=out_specs,
        out_shape=out_shape,
        scratch_shapes=[pltpu.VMEM((2 * DHV, tt), F32), pltpu.VMEM((tt, D), F32)],
        compiler_params=_cparams(2),
        name="mlstm_scan_ctx" if is_ctx else "mlstm_scan_lat",
    )(*args)


SC_TM = 512


SC_HALO = 8


def _sconv_kernel(x_ref, prev_ref, next_ref, mod_ref, w_in_ref, cw_ref, w_out_ref, g_ref, b_ref, o_ref, e_ref):
    tm = SC_TM
    row0 = pl.program_id(0) * tm
    slen = _seq_len(row0)
    pos = (row0 + lax.broadcasted_iota(jnp.int32, (tm, 1), 0)) & (slen - 1)
    m = mod_ref[...]
    x = x_ref[...]
    xe = jnp.concatenate([prev_ref[...], x, next_ref[...]], axis=0)
    he = (xe * (1.0 + m[1:2]) + m[0:1]).astype(BF)
    e_ref[...] = _dot(he, w_in_ref[:, D:2 * D]) * _dot(he, w_in_ref[:, 2 * D:3 * D])
    bg = _dot(he[SC_HALO:SC_HALO + tm], w_in_ref[:, 0:D])
    prev = jnp.where(pos == 0, 0.0, e_ref[SC_HALO - 1:SC_HALO - 1 + tm, :])
    nxt = jnp.where(pos == slen - 1, 0.0, e_ref[SC_HALO + 1:SC_HALO + 1 + tm, :])
    conv = cw_ref[0:1, :] * prev + cw_ref[1:2, :] * e_ref[SC_HALO:SC_HALO + tm, :] + cw_ref[2:3, :] * nxt
    y = _dot((bg * conv).astype(BF), w_out_ref[...])
    o_ref[...] = _ln(ALPHA * x + m[2:3] * y, g_ref[...], b_ref[...])


def _sconv(x, mod, layer, w_in, conv_w, w_out, ln_g, ln_b):
    tm = SC_TM
    hb = tm // SC_HALO
    nhb = NTOK // SC_HALO
    return pl.pallas_call(
        _sconv_kernel,
        grid=(NTOK // tm,),
        in_specs=[pl.BlockSpec((tm, D), lambda t: (t, 0)),
                  pl.BlockSpec((SC_HALO, D), lambda t: (jnp.maximum(t * hb - 1, 0), 0)),
                  pl.BlockSpec((SC_HALO, D), lambda t: (jnp.minimum((t + 1) * hb, nhb - 1), 0)),
                  _mod_spec(layer, tm),
                  pl.BlockSpec((D, 3 * D), lambda t: (0, 0), pipeline_mode=pl.Buffered(1)),
                  _full((3, D)),
                  pl.BlockSpec((D, D), lambda t: (0, 0), pipeline_mode=pl.Buffered(1)),
                  _full((1, D)), _full((1, D))],
        out_specs=pl.BlockSpec((tm, D), lambda t: (t, 0)),
        out_shape=jax.ShapeDtypeStruct((NTOK, D), F32),
        scratch_shapes=[pltpu.VMEM((tm + 2 * SC_HALO, D), F32)],
        compiler_params=_cparams(1),
        name="sconv",
    )(x, x, x, mod, w_in, conv_w, w_out, ln_g.reshape(1, D), ln_b.reshape(1, D))


PL_TM = 256
PL_HALO = 16


def _pool_kernel(x_ref, prev_ref, next_ref, mod_ref, w_ref, pb_ref, ps_ref, g_ref, b_ref, o_ref,
                 e_ref, s2_ref, s4_ref, s8_ref):
    tm, hl, gw = PL_TM, PL_HALO, GROUP_W
    row0 = pl.program_id(0) * tm
    slen = _seq_len(row0)
    pos0 = row0 & (slen - 1)
    m = mod_ref[...]
    x = x_ref[...]
    h = x * (1.0 + m[1:2]) + m[0:1]
    hp = prev_ref[...] * (1.0 + m[1:2]) + m[0:1]
    hn = next_ref[...] * (1.0 + m[1:2]) + m[0:1]
    n0 = tm + 2 * hl
    e_ref[0:hl, :] = jnp.where(pos0 != 0, hp, 0.0)
    e_ref[hl:hl + tm, :] = h
    e_ref[hl + tm:n0, :] = jnp.where(pos0 + tm != slen, hn, 0.0)
    e_ref[n0:, :] = jnp.zeros((8, D), F32)
    n2, n4, n8 = n0, n0 - 8, n0 - 16
    s2_ref[...] = e_ref[0:n2, :] + e_ref[1:n2 + 1, :]
    s4_ref[...] = s2_ref[0:n4, gw:] + s2_ref[2:n4 + 2, gw:]
    s8_ref[...] = s4_ref[0:n8, gw:] + s4_ref[4:n8 + 4, gw:]
    tots = [s2_ref[hl - 1:hl - 1 + tm, 0:gw],
            s4_ref[hl - 2:hl - 2 + tm, 0:gw],
            s8_ref[hl - 4:hl - 4 + tm, 0:gw],
            s8_ref[hl - 8:hl - 8 + tm, gw:] + s8_ref[hl:hl + tm, gw:]]
    pos = pos0 + lax.broadcasted_iota(jnp.int32, (tm, 1), 0)
    ys = []
    for gi, win in enumerate(POOL_WINDOWS):
        cs = slice(gi * gw, (gi + 1) * gw)
        back = win // 2
        cnt = jnp.minimum(pos + (win - back), slen) - jnp.maximum(pos - back, 0)
        p = tots[gi] / cnt.astype(F32) - h[:, cs]
        ys.append(_dot(p.astype(BF), w_ref[gi]))
    y = (jnp.concatenate(ys, axis=1) + pb_ref[...]) * ps_ref[...]
    o_ref[...] = _ln(ALPHA * x + m[2:3] * y, g_ref[...], b_ref[...])


def _pool(x, mod, layer, w, pb, ps, ln_g, ln_b):
    assert POOL_WINDOWS == (2, 4, 8, 16) and PL_HALO >= POOL_WINDOWS[-1] // 2
    tm = PL_TM
    hb = tm // PL_HALO
    nhb = NTOK // PL_HALO
    n0 = tm + 2 * PL_HALO
    return pl.pallas_call(
        _pool_kernel,
        grid=(NTOK // tm,),
        in_specs=[pl.BlockSpec((tm, D), lambda t: (t, 0)),
                  pl.BlockSpec((PL_HALO, D), lambda t: (jnp.maximum(t * hb - 1, 0), 0)),
                  pl.BlockSpec((PL_HALO, D), lambda t: (jnp.minimum((t + 1) * hb, nhb - 1), 0)),
                  _mod_spec(layer, tm),
                  _full((N_GROUPS, GROUP_W, GROUP_W)),
                  _full((1, D)), _full((1, D)), _full((1, D)), _full((1, D))],
        out_specs=pl.BlockSpec((tm, D), lambda t: (t, 0)),
        out_shape=jax.ShapeDtypeStruct((NTOK, D), F32),
        scratch_shapes=[pltpu.VMEM((n0 + 8, D), F32),
                        pltpu.VMEM((n0, D), F32),
                        pltpu.VMEM((n0 - 8, D - GROUP_W), F32),
                        pltpu.VMEM((n0 - 16, D - 2 * GROUP_W), F32)],
        compiler_params=_cparams(1),
        name="pool",
    )(x, x, x, mod, w, pb.reshape(1, D), ps.reshape(1, D), ln_g.reshape(1, D), ln_b.reshape(1, D))


FT_TK = 256


def _dft_mats(n, scale):
    k = np.arange(n, dtype=np.int64)
    ang = 2.0 * np.pi * ((k[:, None] * k[None, :]) % n).astype(np.float64) / n
    return np.stack([np.cos(ang) * scale, np.sin(ang) * scale]).astype(np.float32)


def _dot_x3(a_parts, b_parts):
    (ah, al), (bh, bl) = a_parts, b_parts
    return _dot(ah, bh) + (_dot(ah, bl) + _dot(al, bh))


def _fourier_kernel(x_ref, mod_ref, cs_ref, cc_ref, o_ref):
    m = mod_ref[...]
    hparts = _split2(x_ref[...] * (1.0 + m[1:2]) + m[0:1])
    ac = _dot_x3(_split2(cs_ref[0]), hparts)
    as_ = _dot_x3(_split2(cs_ref[1]), hparts)
    ccp = _split2(cc_ref[0])
    scp = _split2(cc_ref[1])
    outs = []
    for gi in range(N_GROUPS):
        cs = slice(gi * GROUP_W, (gi + 1) * GROUP_W)
        outs.append(_dot_x3(_split2(ac[:, cs]), ccp) - _dot_x3(_split2(as_[:, cs]), scp))
    o_ref[...] = jnp.concatenate(outs, axis=1).astype(BF)


def _fourier(x, mod, layer, s, nseq, row_off, mod_row_fn):
    tk = FT_TK
    seq_off = row_off // s
    cs = jnp.asarray(_dft_mats(s, s ** -0.5))
    cc = jnp.asarray(_dft_mats(GROUP_W, GROUP_W ** -0.5))
    return pl.pallas_call(
        _fourier_kernel,
        grid=(nseq, s // tk),
        in_specs=[pl.BlockSpec((s, D), lambda b, j: (seq_off + b, 0)),
                  pl.BlockSpec((None, None, 6, D), lambda b, j: (layer, mod_row_fn(b), 0, 0)),
                  pl.BlockSpec((2, tk, s), lambda b, j: (0, j, 0)),
                  _full((2, GROUP_W, GROUP_W))],
        out_specs=pl.BlockSpec((tk, D), lambda b, j: (b * (s // tk) + j, 0)),
        out_shape=jax.ShapeDtypeStruct((nseq * s, D), BF),
        compiler_params=_cparams(2),
        name="fourier_%d" % s,
    )(x, mod, cs, cc)


def _pos_embed():
    rows = DEC_SEQ // GRID_W
    quarter = D // 4
    omega = 1.0 / (10000.0 ** (np.arange(quarter, dtype=np.float64) / quarter))
    rr, cc = np.meshgrid(np.arange(rows, dtype=np.float64), np.arange(GRID_W, dtype=np.float64), indexing="ij")
    er = rr.reshape(-1, 1) * omega
    ec = cc.reshape(-1, 1) * omega
    return np.concatenate([np.sin(er), np.cos(er), np.sin(ec), np.cos(ec)], axis=-1).astype(np.float32)


def _gate_perm():
    perm = np.zeros(4 * H, dtype=np.int32)
    for hp in range(N_HP):
        for d in range(2):
            for j in range(2):
                for hh in range(2):
                    perm[hp * NG + d * 4 + j * 2 + hh] = d * 2 * H + j * H + 2 * hp + hh
    return perm


def kernel(x_prompt, x_sample, state_C, state_n, state_m, c, c_ctx, w_mod, b_mod, ln_g, ln_b, mlp_w1, mlp_w2,
           ml_w_in, ml_w_gate, ml_b_gate, ml_norm_g, ml_w_out, sc_w_in, sc_conv_w, sc_w_out,
           pl_w, pl_b, pl_scale, ft_w_out, ft_b_out):
    cond = jnp.concatenate([c_ctx[None, :], c, jnp.zeros((NCOND - 1 - DEC_BATCH, D), F32)], axis=0)
    mod = _modulation(cond, w_mod, b_mod)

    x = (x_prompt.reshape(NCTX, D), x_sample.reshape(NLAT, D), jnp.asarray(_pos_embed()))

    new_c = new_n = new_m = None
    for i in range(DEPTH):
        kind, j = i % 4, i // 4
        if kind == 0:
            w_in = ml_w_in[j].astype(BF)
            perm = _gate_perm()
            wg = ml_w_gate[j][:, perm].astype(BF)
            bg = ml_b_gate[j][perm]
            wk = w_in[:, QK:2 * QK]
            qt, k, kt, vt, o, g, gt = _ml_proj(
                x, mod, i, w_in[:, :QK].T, wk, wk.T, w_in[:, 2 * QK:2 * QK + V].T, w_in[:, 2 * QK + V:],
                wg, wg.T, bg.reshape(1, -1), bg.reshape(-1, 1))
            scan_args = (qt, k, kt, vt, o, g, gt, ml_norm_g[j], ml_w_out[j].astype(BF), mod, i,
                         ln_g[i, 0], ln_b[i, 0], x)
            xc, cst, nst, mst = _ml_scan(*scan_args)
            c0 = jnp.concatenate([jnp.swapaxes(state_C[:, j], -1, -2), state_n[:, j][..., None, :],
                                  jnp.zeros((DEC_BATCH, 2, H, DHA - DHV - 1, DQK), F32)], axis=-2)
            m0 = jnp.broadcast_to(state_m[:, j][..., None, None], (DEC_BATCH, 2, H, 1, 128))
            (xl,) = _ml_scan(*scan_args, c0, m0)
            x = (xc, xl)
            new_c = cst[:, None]
            new_n = nst[..., 0][:, None]
            new_m = mst[..., 0, 0][:, None]
        elif kind == 1:
            x = _sconv(x, mod, i, sc_w_in[j].astype(BF), sc_conv_w[j], sc_w_out[j].astype(BF),
                       ln_g[i, 0], ln_b[i, 0])
        elif kind == 2:
            x = _pool(x, mod, i, pl_w[j].astype(BF), pl_b[j], pl_scale[j], ln_g[i, 0], ln_b[i, 0])
        else:
            fc = _fourier(x, mod, i, SEQ, BATCH, 0, lambda b: 0)
            fl = _fourier(x, mod, i, DEC_SEQ, DEC_BATCH, NCTX, lambda b: 1 + b)
            x = _out_proj(fc, fl, ft_w_out[j].astype(BF), ft_b_out[j], x, mod, i, ln_g[i, 0], ln_b[i, 0])
        x = _mlp(x, mod, i, mlp_w1, mlp_w2, ln_g[i, 1], ln_b[i, 1],
                 split_out=(i == DEPTH - 1))

    y_prompt = x[0].reshape(BATCH, SEQ, D)
    y_sample = x[1].reshape(DEC_BATCH, DEC_SEQ, D)
    return y_prompt, y_sample, new_c, new_n, new_m
```

```python
import functools
import math

import numpy as np
import jax
import jax.numpy as jnp
from jax import lax
from jax.experimental import pallas as pl
from jax.experimental.pallas import tpu as pltpu

D = 1024
BATCH, SEQ = 16, 256
DEC_BATCH, DEC_SEQ = 4, 1024
DEPTH = 4
GRID_W = 64
H = 8
DQK = 64
DHV = 128
QK = H * DQK
V = H * DHV
N_GROUPS = 4
GROUP_W = D // N_GROUPS
POOL_WINDOWS = (2, 4, 8, 16)
D_FF = 4 * D
ALPHA = (2.0 * DEPTH) ** 0.25
LN_EPS = 1e-5
F32 = jnp.float32
BF = jnp.bfloat16

NCTX = BATCH * SEQ
NLAT = DEC_BATCH * DEC_SEQ
NTOK = NCTX + NLAT
NCOND = 8
CHUNK = 256
VMEM_LIMIT = 56 * 1024 * 1024


def _cparams(n_axes):
    return pltpu.CompilerParams(dimension_semantics=("arbitrary",) * n_axes,
                                vmem_limit_bytes=VMEM_LIMIT)


def _mod_row(row0):
    return jnp.where(row0 < NCTX, 0, 1 + (row0 - NCTX) // DEC_SEQ)


def _seq_len(row0):
    return jnp.where(row0 < NCTX, SEQ, DEC_SEQ)


def _ln(z, g, b):
    mu = jnp.mean(z, axis=-1, keepdims=True)
    zc = z - mu
    var = jnp.mean(zc * zc, axis=-1, keepdims=True)
    return zc * lax.rsqrt(var + LN_EPS) * g + b


def _dot(a, b):
    return jnp.dot(a, b, preferred_element_type=F32)


def _dot_nt(a, b):
    return lax.dot_general(a, b, (((1,), (1,)), ((), ())), preferred_element_type=F32)


def _split2(x):
    hi = x.astype(BF)
    lo = (x - hi.astype(F32)).astype(BF)
    return hi, lo


def _split3(x):
    hi = x.astype(BF)
    r = x - hi.astype(F32)
    mid = r.astype(BF)
    lo = (r - mid.astype(F32)).astype(BF)
    return hi, mid, lo


def _full(shape):
    n = len(shape)
    return pl.BlockSpec(shape, lambda *_: (0,) * n)


def _x_specs(x, tm):
    if not isinstance(x, tuple):
        return [pl.BlockSpec((tm, D), lambda t: (t, 0))], [x]
    nc = NCTX // tm
    return ([pl.BlockSpec((tm, D), lambda t: (jnp.minimum(t, nc - 1), 0)),
             pl.BlockSpec((tm, D), lambda t: (jnp.maximum(t - nc, 0), 0)),
             pl.BlockSpec((tm, D), lambda t: (t % (DEC_SEQ // tm), 0))], list(x))


def _read_x(x_refs, tm):
    if len(x_refs) == 1:
        return x_refs[0][...]
    xc_ref, xl_ref, pe_ref = x_refs
    return jnp.where(pl.program_id(0) * tm < NCTX, xc_ref[...], xl_ref[...] + pe_ref[...])


def _mod_spec(layer, tm):
    return pl.BlockSpec((None, None, 6, D), lambda t: (layer, _mod_row(t * tm), 0, 0))


def _mod_kernel(c_ref, w_ref, b_ref, o_ref):
    c = c_ref[...]
    s = c * jax.nn.sigmoid(c)
    o_ref[...] = _dot(s.astype(BF), w_ref[...].astype(BF)) + b_ref[...]


def _modulation(cond, w_mod, b_mod):
    tn = 1536
    out = pl.pallas_call(
        _mod_kernel,
        grid=(DEPTH, 6 * D // tn),
        in_specs=[pl.BlockSpec((NCOND, D), lambda i, j: (0, 0)),
                  pl.BlockSpec((None, D, tn), lambda i, j: (i, 0, j)),
                  pl.BlockSpec((None, 1, tn), lambda i, j: (i, 0, j))],
        out_specs=pl.BlockSpec((None, NCOND, tn), lambda i, j: (i, 0, j)),
        out_shape=jax.ShapeDtypeStruct((DEPTH, NCOND, 6 * D), F32),
        compiler_params=_cparams(2),
        name="modulation",
    )(cond, w_mod, b_mod.reshape(DEPTH, 1, 6 * D))
    return out.reshape(DEPTH, NCOND, 6, D)


MLP_TM = 512
MLP_FC = 512
MLP_NF = D_FF // MLP_FC
MLP_NT = NTOK // MLP_TM


def _mlp_tile(s):
    return jnp.maximum(s - (MLP_NF - 1), 0)


def _mlp_kernel(mod_ref, w1_ref, w2_ref, g_ref, b_ref, x1_ref, *refs, nx):
    x_refs, o_refs, (w1s_ref, w2s_ref, acc_ref) = refs[:nx], refs[nx:-3], refs[-3:]
    s = pl.program_id(0)
    is_ctx = _mlp_tile(s) * MLP_TM < NCTX
    m = mod_ref[...]

    def pre(x):
        return (x * (1.0 + m[4:5]) + m[3:4]).astype(BF)

    def hidden(h, w1c):
        return jnp.square(jnp.maximum(_dot(h, w1c), 0.0)).astype(BF)

    def result(x, acc):
        return _ln(ALPHA * x + m[5:6] * acc, g_ref[...], b_ref[...])

    def read_x():
        if nx == 1:
            return x_refs[0][...]
        return jnp.where(is_ctx, x_refs[0][...], x_refs[1][...])

    def accumulate(i, first, part):
        @pl.when(first)
        def _():
            acc_ref[i] = part

        @pl.when(jnp.logical_not(first))
        def _():
            acc_ref[i] += part

    @pl.when(s < MLP_NF)
    def _():
        x = read_x()
        w1c = w1_ref[...].astype(BF)
        w2c = w2_ref[...].astype(BF)
        w1s_ref[s] = w1c
        w2s_ref[s] = w2c
        accumulate(0, s == 0, _dot(hidden(pre(x), w1c), w2c))

        @pl.when(s == MLP_NF - 1)
        def _():
            o_refs[0][...] = result(x, acc_ref[0])

    @pl.when(jnp.logical_and(s >= 1, s <= MLP_NF))
    def _():
        x = x1_ref[...]
        accumulate(1, s == 1, _dot(hidden(pre(x), w1s_ref[s - 1]), w2s_ref[s - 1]))

        @pl.when(s == MLP_NF)
        def _():
            o_refs[0][...] = result(x, acc_ref[1])

    @pl.when(s > MLP_NF)
    def _():
        x = read_x()
        h = pre(x)
        acc = jnp.zeros((MLP_TM, D), F32)
        for f in range(MLP_NF):
            acc = acc + _dot(hidden(h, w1s_ref[f]), w2s_ref[f])
        res = result(x, acc)
        if len(o_refs) == 1:
            o_refs[0][...] = res
        else:
            @pl.when(is_ctx)
            def _():
                o_refs[0][...] = res

            @pl.when(jnp.logical_not(is_ctx))
            def _():
                o_refs[1][...] = res


def _mlp(x, mod, layer, w1, w2, ln_g, ln_b, split_out=False):
    tm = MLP_TM
    nc = NCTX // tm
    assert nc >= 2 and MLP_NT >= 2
    half_specs = [pl.BlockSpec((tm, D), lambda s: (jnp.minimum(_mlp_tile(s), nc - 1), 0)),
                  pl.BlockSpec((tm, D), lambda s: (jnp.maximum(_mlp_tile(s) - nc, 0), 0))]
    if isinstance(x, tuple):
        x_specs, x_args = half_specs, list(x)
    else:
        x_specs, x_args = [pl.BlockSpec((tm, D), lambda s: (_mlp_tile(s), 0))], [x]
    if split_out:
        out_specs = half_specs
        out_shape = [jax.ShapeDtypeStruct((NCTX, D), F32), jax.ShapeDtypeStruct((NLAT, D), F32)]
    else:
        out_specs = pl.BlockSpec((tm, D), lambda s: (_mlp_tile(s), 0))
        out_shape = jax.ShapeDtypeStruct((NTOK, D), F32)
    return pl.pallas_call(
        functools.partial(_mlp_kernel, nx=len(x_args)),
        grid=(MLP_NF - 1 + MLP_NT,),
        in_specs=[pl.BlockSpec((None, None, 6, D), lambda s: (layer, _mod_row(_mlp_tile(s) * tm), 0, 0)),
                  pl.BlockSpec((None, D, MLP_FC), lambda s: (layer, 0, jnp.minimum(s, MLP_NF - 1))),
                  pl.BlockSpec((None, MLP_FC, D), lambda s: (layer, jnp.minimum(s, MLP_NF - 1), 0)),
                  _full((1, D)), _full((1, D)),
                  pl.BlockSpec((tm, D), lambda s: (1, 0))] + x_specs,
        out_specs=out_specs,
        out_shape=out_shape,
        scratch_shapes=[pltpu.VMEM((MLP_NF, D, MLP_FC), BF),
                        pltpu.VMEM((MLP_NF, MLP_FC, D), BF),
                        pltpu.VMEM((2, tm, D), F32)],
        compiler_params=_cparams(1),
        name="mlp",
    )(mod, w1, w2, ln_g.reshape(1, D), ln_b.reshape(1, D), x_args[0], *x_args)


OUT_TM = 512


def _out_kernel(uc_ref, ul_ref, w_ref, bias_ref, mod_ref, g_ref, b_ref, *refs):
    x_refs, o_ref = refs[:-1], refs[-1]
    is_ctx = pl.program_id(0) * OUT_TM < NCTX
    u = jnp.where(is_ctx, uc_ref[...], ul_ref[...])
    y = _dot(u, w_ref[...]) + bias_ref[...]
    m = mod_ref[...]
    o_ref[...] = _ln(ALPHA * _read_x(x_refs, OUT_TM) + m[2:3] * y, g_ref[...], b_ref[...])


def _out_proj(uc, ul, w, bias, x, mod, layer, ln_g, ln_b):
    tm = OUT_TM
    nc = NCTX // tm
    k = w.shape[0]
    x_specs, x_args = _x_specs(x, tm)
    return pl.pallas_call(
        _out_kernel,
        grid=(NTOK // tm,),
        in_specs=[pl.BlockSpec((tm, k), lambda t: (jnp.minimum(t, nc - 1), 0)),
                  pl.BlockSpec((tm, k), lambda t: (jnp.maximum(t - nc, 0), 0)),
                  pl.BlockSpec((k, D), lambda t: (0, 0), pipeline_mode=pl.Buffered(1)),
                  _full((1, D)),
                  _mod_spec(layer, tm),
                  _full((1, D)), _full((1, D))] + x_specs,
        out_specs=pl.BlockSpec((tm, D), lambda t: (t, 0)),
        out_shape=jax.ShapeDtypeStruct((NTOK, D), F32),
        compiler_params=_cparams(1),
        name="out_proj",
    )(uc, ul, w, bias.reshape(1, D), mod, ln_g.reshape(1, D), ln_b.reshape(1, D), *x_args)


ML_TM = 512
N_HP = H // 2
NG = 8


def _log_sigmoid(g):
    return jnp.minimum(g, 0.0) - jnp.log1p(jnp.exp(-jnp.abs(g)))


def _ml_proj_kernel(mod_ref, wqt_ref, wk_ref, wkt_ref, wvt_ref, wo_ref, wg_ref, wgt_ref, bg_ref, bgt_ref, *refs):
    x_refs = refs[:-7]
    qt_ref, k_ref, kt_ref, vt_ref, o_ref, g_ref, gt_ref = refs[-7:]
    m = mod_ref[...]
    h = (_read_x(x_refs, ML_TM) * (1.0 + m[1:2]) + m[0:1]).astype(BF)
    qt_ref[...] = _dot_nt(wqt_ref[...], h).astype(BF)
    k_ref[...] = (_dot(h, wk_ref[...]) * (DQK ** -0.5)).astype(BF)
    kt_ref[...] = (_dot_nt(wkt_ref[...], h) * (DQK ** -0.5)).astype(BF)
    vt_ref[...] = _dot_nt(wvt_ref[...], h).astype(BF)
    o_ref[...] = _dot(h, wo_ref[...])
    g = _dot(h, wg_ref[...]) + bg_ref[...]
    col = lax.broadcasted_iota(jnp.int32, g.shape, 1)
    g = jnp.where((col & 2) != 0, _log_sigmoid(g), g)
    gt = _dot_nt(wgt_ref[...], h) + bgt_ref[...]
    row = lax.broadcasted_iota(jnp.int32, gt.shape, 0)
    gt = jnp.where((row & 2) != 0, _log_sigmoid(gt), gt)
    for hp in range(N_HP):
        g_ref[hp] = g[:, hp * NG:(hp + 1) * NG]
        gt_ref[hp] = gt[hp * NG:(hp + 1) * NG, :]


def _ml_proj(x, mod, layer, wqt, wk, wkt, wvt, wo, wg, wgt, bg, bgt):
    tm = ML_TM
    res = lambda shape: pl.BlockSpec(shape, lambda t: (0,) * len(shape), pipeline_mode=pl.Buffered(1))
    x_specs, x_args = _x_specs(x, tm)
    return pl.pallas_call(
        _ml_proj_kernel,
        grid=(NTOK // tm,),
        in_specs=[_mod_spec(layer, tm),
                  res((QK, D)), res((D, QK)), res((QK, D)), res((V, D)), res((D, V)),
                  res((D, 4 * NG)), res((4 * NG, D)), res((1, 4 * NG)), res((4 * NG, 1))] + x_specs,
        out_specs=[pl.BlockSpec((QK, tm), lambda t: (0, t)),
                   pl.BlockSpec((tm, QK), lambda t: (t, 0)),
                   pl.BlockSpec((QK, tm), lambda t: (0, t)),
                   pl.BlockSpec((V, tm), lambda t: (0, t)),
                   pl.BlockSpec((tm, V), lambda t: (t, 0)),
                   pl.BlockSpec((N_HP, tm, NG), lambda t: (0, t, 0)),
                   pl.BlockSpec((N_HP, NG, tm), lambda t: (0, 0, t))],
        out_shape=[jax.ShapeDtypeStruct((QK, NTOK), BF),
                   jax.ShapeDtypeStruct((NTOK, QK), BF),
                   jax.ShapeDtypeStruct((QK, NTOK), BF),
                   jax.ShapeDtypeStruct((V, NTOK), BF),
                   jax.ShapeDtypeStruct((NTOK, V), F32),
                   jax.ShapeDtypeStruct((N_HP, NTOK, NG), F32),
                   jax.ShapeDtypeStruct((N_HP, NG, NTOK), F32)],
        compiler_params=_cparams(1),
        name="mlstm_proj",
    )(mod, wqt, wk, wkt, wvt, wo, wg, wgt, bg, bgt, *x_args)


DHA = DHV + 8


def _scan_chunk(k_c, kt_c, qt_c, vta, rr_col, rr_row, b_row, m, cta, d, need_state):
    L = k_c.shape[0]
    si = lax.broadcasted_iota(jnp.int32, (L, L), 0)
    ti = lax.broadcasted_iota(jnp.int32, (L, L), 1)
    mask = (si <= ti) if d == 0 else (si >= ti)
    rrm = jnp.where(mask, rr_col, -jnp.inf)
    g = jnp.maximum(jnp.max(rrm, axis=0, keepdims=True), m)
    st = _dot(k_c, qt_c) * jnp.exp(rrm - g)
    num = _dot(vta[:DHV], st.astype(BF))
    den = jnp.sum(st, axis=0, keepdims=True)
    if cta is not None:
        a = jnp.exp(m - g)
        qc = _dot(cta.astype(BF), qt_c)
        num = num + a * qc[:DHV]
        den = den + a * qc[DHV:DHV + 1]
    mt = b_row + g
    ht = num * (1.0 / jnp.maximum(jnp.abs(den), jnp.exp(-mt)))
    if not need_state:
        return ht, None, None
    last = L - 1 if d == 0 else 0
    m_new = mt[:, last:last + 1]
    b_last = b_row[:, last:last + 1]
    w = jnp.exp(b_last + rr_row - m_new)
    kw = (kt_c.astype(F32) * w).astype(BF)
    if cta is None:
        return ht, _dot_nt(kw, vta), m_new
    c_new = jnp.exp(b_last + m - m_new) * cta + _dot_nt(vta, kw)
    return ht, c_new, m_new


def _scan_kernel(*refs, is_ctx):
    if is_ctx:
        qt_ref, k_ref, kt_ref, vt_ref, o_ref, g_ref, gt_ref, ng_ref, u_ref, cst_ref, nst_ref, mst_ref, hbuf = refs
    else:
        qt_ref, k_ref, kt_ref, vt_ref, o_ref, g_ref, gt_ref, ng_ref, c0_ref, m0_ref, u_ref, hbuf = refs
    L = CHUNK
    nch = DEC_SEQ // L
    ri = lax.broadcasted_iota(jnp.int32, (L, L), 0)
    ci = lax.broadcasted_iota(jnp.int32, (L, L), 1)
    lower = jnp.where(ci <= ri, 1.0, 0.0).astype(BF)
    upper = jnp.where(ci >= ri, 1.0, 0.0).astype(BF)
    sub = lax.broadcasted_iota(jnp.int32, (DHA - DHV, L), 0)
    ones_rows = jnp.where(sub == 0, 1.0, 0.0).astype(BF)

    rr_cols, rr_rows, b_rows = [], [], []
    tparts = _split3(jnp.concatenate([gt_ref[:, c * L:(c + 1) * L] for c in range(nch)], axis=0))
    b_row_all = (sum(_dot(p, upper) for p in tparts), sum(_dot(p, lower) for p in tparts))
    for c in range(nch):
        gc = g_ref[c * L:(c + 1) * L, :]
        gtc = gt_ref[:, c * L:(c + 1) * L]
        gparts = _split3(gc)
        b_col = (sum(_dot(lower, p) for p in gparts), sum(_dot(upper, p) for p in gparts))
        b_row = tuple(b[c * NG:(c + 1) * NG, :] for b in b_row_all)
        rr_cols.append([gc[:, 0:NG - 2] - b[:, 2:NG] for b in b_col])
        rr_rows.append([gtc[0:NG - 2, :] - b[2:NG, :] for b in b_row])
        b_rows.append(b_row)

    for hh in range(2):
        for d in range(2):
            ig_i, lf_i = d * 4 + hh, d * 4 + 2 + hh
            if is_ctx:
                m, cta = jnp.zeros((1, 1), F32), None
            else:
                m = m0_ref[d, hh][:, 0:1]
                cta = c0_ref[d, hh]
            order = range(nch) if d == 0 else range(nch - 1, -1, -1)
            for n, c in enumerate(order):
                sl = slice(c * L, (c + 1) * L)
                hs = slice(hh * DQK, (hh + 1) * DQK)
                vta = jnp.concatenate([vt_ref[hh * DHV:(hh + 1) * DHV, sl], ones_rows], axis=0)
                need_state = is_ctx or n < nch - 1
                hcur, c_new, m_new = _scan_chunk(
                    k_ref[sl, hs], kt_ref[hs, sl], qt_ref[hs, sl], vta,
                    rr_cols[c][d][:, ig_i:ig_i + 1], rr_rows[c][d][ig_i:ig_i + 1, :],
                    b_rows[c][d][lf_i:lf_i + 1, :], m, cta, d, need_state)
                if d == 0:
                    hbuf[hh * DHV:(hh + 1) * DHV, sl] = hcur
                else:
                    hbuf[hh * DHV:(hh + 1) * DHV, sl] += hcur
                if is_ctx:
                    cst_ref[c, d, hh] = c_new[:, :DHV]
                    nst_ref[c, d, hh] = c_new[:, DHV:]
                    mst_ref[c, d, hh] = jnp.broadcast_to(m_new, (8, 128))
                else:
                    m, cta = m_new, c_new

    for hh in range(2):
        hsum = hbuf[hh * DHV:(hh + 1) * DHV, :]
        mu = jnp.mean(hsum, axis=0, keepdims=True)
        hc = hsum - mu
        var = jnp.mean(hc * hc, axis=0, keepdims=True)
        hn = hc * lax.rsqrt(var + LN_EPS) * ng_ref[hh * DHV:(hh + 1) * DHV, :]
        og = jax.nn.sigmoid(o_ref[:, hh * DHV:(hh + 1) * DHV])
        u_ref[:, hh * DHV:(hh + 1) * DHV] = (hn.T * og).astype(BF)


def _ml_scan(qt, k, kt, vt, o, g, gt, norm_g, c0aug=None, m0b=None):
    is_ctx = c0aug is None
    tt = DEC_SEQ
    toff = 0 if is_ctx else NCTX // tt
    nt = (NCTX if is_ctx else NLAT) // tt
    in_specs = [pl.BlockSpec((2 * DQK, tt), lambda t, p: (p, toff + t)),
                pl.BlockSpec((tt, 2 * DQK), lambda t, p: (toff + t, p)),
                pl.BlockSpec((2 * DQK, tt), lambda t, p: (p, toff + t)),
                pl.BlockSpec((2 * DHV, tt), lambda t, p: (p, toff + t)),
                pl.BlockSpec((tt, 2 * DHV), lambda t, p: (toff + t, p)),
                pl.BlockSpec((None, tt, NG), lambda t, p: (p, toff + t, 0)),
                pl.BlockSpec((None, NG, tt), lambda t, p: (p, 0, toff + t)),
                pl.BlockSpec((2 * DHV, 1), lambda t, p: (p, 0))]
    args = [qt, k, kt, vt, o, g, gt, norm_g.reshape(V, 1)]
    u_spec = pl.BlockSpec((tt, 2 * DHV), lambda t, p: (t, p))
    u_shape = jax.ShapeDtypeStruct((nt * tt, V), BF)
    if is_ctx:
        nseq = tt // SEQ
        out_specs = [u_spec,
                     pl.BlockSpec((nseq, 2, 2, DQK, DHV), lambda t, p: (t, 0, p, 0, 0)),
                     pl.BlockSpec((nseq, 2, 2, DQK, DHA - DHV), lambda t, p: (t, 0, p, 0, 0)),
                     pl.BlockSpec((nseq, 2, 2, 8, 128), lambda t, p: (t, 0, p, 0, 0))]
        out_shape = [u_shape,
                     jax.ShapeDtypeStruct((BATCH, 2, H, DQK, DHV), F32),
                     jax.ShapeDtypeStruct((BATCH, 2, H, DQK, DHA - DHV), F32),
                     jax.ShapeDtypeStruct((BATCH, 2, H, 8, 128), F32)]
    else:
        in_specs += [pl.BlockSpec((None, 2, 2, DHA, DQK), lambda t, p: (t, 0, p, 0, 0)),
                     pl.BlockSpec((None, 2, 2, 1, 128), lambda t, p: (t, 0, p, 0, 0))]
        args += [c0aug, m0b]
        out_specs = [u_spec]
        out_shape = [u_shape]
    return pl.pallas_call(
        functools.partial(_scan_kernel, is_ctx=is_ctx),
        grid=(nt, N_HP),
        in_specs=in_specs,
        out_specs=out_specs,
        out_shape=out_shape,
        scratch_shapes=[pltpu.VMEM((2 * DHV, tt), F32)],
        compiler_params=_cparams(2),
        name="mlstm_scan_ctx" if is_ctx else "mlstm_scan_lat",
    )(*args)


SC_TM = 512


SC_HALO = 8


def _sconv_kernel(x_ref, prev_ref, next_ref, mod_ref, w_in_ref, cw_ref, w_out_ref, g_ref, b_ref, o_ref, e_ref):
    tm = SC_TM
    row0 = pl.program_id(0) * tm
    slen = _seq_len(row0)
    pos = (row0 + lax.broadcasted_iota(jnp.int32, (tm, 1), 0)) & (slen - 1)
    m = mod_ref[...]
    x = x_ref[...]
    xe = jnp.concatenate([prev_ref[...], x, next_ref[...]], axis=0)
    he = (xe * (1.0 + m[1:2]) + m[0:1]).astype(BF)
    e_ref[...] = _dot(he, w_in_ref[:, D:2 * D]) * _dot(he, w_in_ref[:, 2 * D:3 * D])
    bg = _dot(he[SC_HALO:SC_HALO + tm], w_in_ref[:, 0:D])
    prev = jnp.where(pos == 0, 0.0, e_ref[SC_HALO - 1:SC_HALO - 1 + tm, :])
    nxt = jnp.where(pos == slen - 1, 0.0, e_ref[SC_HALO + 1:SC_HALO + 1 + tm, :])
    conv = cw_ref[0:1, :] * prev + cw_ref[1:2, :] * e_ref[SC_HALO:SC_HALO + tm, :] + cw_ref[2:3, :] * nxt
    y = _dot((bg * conv).astype(BF), w_out_ref[...])
    o_ref[...] = _ln(ALPHA * x + m[2:3] * y, g_ref[...], b_ref[...])


def _sconv(x, mod, layer, w_in, conv_w, w_out, ln_g, ln_b):
    tm = SC_TM
    hb = tm // SC_HALO
    nhb = NTOK // SC_HALO
    return pl.pallas_call(
        _sconv_kernel,
        grid=(NTOK // tm,),
        in_specs=[pl.BlockSpec((tm, D), lambda t: (t, 0)),
                  pl.BlockSpec((SC_HALO, D), lambda t: (jnp.maximum(t * hb - 1, 0), 0)),
                  pl.BlockSpec((SC_HALO, D), lambda t: (jnp.minimum((t + 1) * hb, nhb - 1), 0)),
                  _mod_spec(layer, tm),
                  pl.BlockSpec((D, 3 * D), lambda t: (0, 0), pipeline_mode=pl.Buffered(1)),
                  _full((3, D)),
                  pl.BlockSpec((D, D), lambda t: (0, 0), pipeline_mode=pl.Buffered(1)),
                  _full((1, D)), _full((1, D))],
        out_specs=pl.BlockSpec((tm, D), lambda t: (t, 0)),
        out_shape=jax.ShapeDtypeStruct((NTOK, D), F32),
        scratch_shapes=[pltpu.VMEM((tm + 2 * SC_HALO, D), F32)],
        compiler_params=_cparams(1),
        name="sconv",
    )(x, x, x, mod, w_in, conv_w, w_out, ln_g.reshape(1, D), ln_b.reshape(1, D))


PL_TM = 256
PL_HALO = 16


def _pool_kernel(x_ref, prev_ref, next_ref, mod_ref, w_ref, pb_ref, ps_ref, g_ref, b_ref, o_ref,
                 e_ref, s2_ref, s4_ref, s8_ref):
    tm, hl, gw = PL_TM, PL_HALO, GROUP_W
    row0 = pl.program_id(0) * tm
    slen = _seq_len(row0)
    pos0 = row0 & (slen - 1)
    m = mod_ref[...]
    x = x_ref[...]
    h = x * (1.0 + m[1:2]) + m[0:1]
    hp = prev_ref[...] * (1.0 + m[1:2]) + m[0:1]
    hn = next_ref[...] * (1.0 + m[1:2]) + m[0:1]
    n0 = tm + 2 * hl
    e_ref[0:hl, :] = jnp.where(pos0 != 0, hp, 0.0)
    e_ref[hl:hl + tm, :] = h
    e_ref[hl + tm:n0, :] = jnp.where(pos0 + tm != slen, hn, 0.0)
    e_ref[n0:, :] = jnp.zeros((8, D), F32)
    n2, n4, n8 = n0, n0 - 8, n0 - 16
    s2_ref[...] = e_ref[0:n2, :] + e_ref[1:n2 + 1, :]
    s4_ref[...] = s2_ref[0:n4, gw:] + s2_ref[2:n4 + 2, gw:]
    s8_ref[...] = s4_ref[0:n8, gw:] + s4_ref[4:n8 + 4, gw:]
    tots = [s2_ref[hl - 1:hl - 1 + tm, 0:gw],
            s4_ref[hl - 2:hl - 2 + tm, 0:gw],
            s8_ref[hl - 4:hl - 4 + tm, 0:gw],
            s8_ref[hl - 8:hl - 8 + tm, gw:] + s8_ref[hl:hl + tm, gw:]]
    pos = pos0 + lax.broadcasted_iota(jnp.int32, (tm, 1), 0)
    ys = []
    for gi, win in enumerate(POOL_WINDOWS):
        cs = slice(gi * gw, (gi + 1) * gw)
        back = win // 2
        cnt = jnp.minimum(pos + (win - back), slen) - jnp.maximum(pos - back, 0)
        p = tots[gi] / cnt.astype(F32) - h[:, cs]
        ys.append(_dot(p.astype(BF), w_ref[gi]))
    y = (jnp.concatenate(ys, axis=1) + pb_ref[...]) * ps_ref[...]
    o_ref[...] = _ln(ALPHA * x + m[2:3] * y, g_ref[...], b_ref[...])


def _pool(x, mod, layer, w, pb, ps, ln_g, ln_b):
    assert POOL_WINDOWS == (2, 4, 8, 16) and PL_HALO >= POOL_WINDOWS[-1] // 2
    tm = PL_TM
    hb = tm // PL_HALO
    nhb = NTOK // PL_HALO
    n0 = tm + 2 * PL_HALO
    return pl.pallas_call(
        _pool_kernel,
        grid=(NTOK // tm,),
        in_specs=[pl.BlockSpec((tm, D), lambda t: (t, 0)),
                  pl.BlockSpec((PL_HALO, D), lambda t: (jnp.maximum(t * hb - 1, 0), 0)),
                  pl.BlockSpec((PL_HALO, D), lambda t: (jnp.minimum((t + 1) * hb, nhb - 1), 0)),
                  _mod_spec(layer, tm),
                  _full((N_GROUPS, GROUP_W, GROUP_W)),
                  _full((1, D)), _full((1, D)), _full((1, D)), _full((1, D))],
        out_specs=pl.BlockSpec((tm, D), lambda t: (t, 0)),
        out_shape=jax.ShapeDtypeStruct((NTOK, D), F32),
        scratch_shapes=[pltpu.VMEM((n0 + 8, D), F32),
                        pltpu.VMEM((n0, D), F32),
                        pltpu.VMEM((n0 - 8, D - GROUP_W), F32),
                        pltpu.VMEM((n0 - 16, D - 2 * GROUP_W), F32)],
        compiler_params=_cparams(1),
        name="pool",
    )(x, x, x, mod, w, pb.reshape(1, D), ps.reshape(1, D), ln_g.reshape(1, D), ln_b.reshape(1, D))


FT_TK = 256


def _dft_mats(n, scale):
    k = np.arange(n, dtype=np.int64)
    ang = 2.0 * np.pi * ((k[:, None] * k[None, :]) % n).astype(np.float64) / n
    return np.stack([np.cos(ang) * scale, np.sin(ang) * scale]).astype(np.float32)


def _dot_x3(a_parts, b_parts):
    (ah, al), (bh, bl) = a_parts, b_parts
    return _dot(ah, bh) + (_dot(ah, bl) + _dot(al, bh))


FT_PAD = 8


def _fourier_kernel(x_ref, mod_ref, cs_ref, cc_ref, o_ref, *, s, tk):
    m = mod_ref[...]
    hparts = _split2(x_ref[...] * (1.0 + m[1:2]) + m[0:1])
    ccp = _split2(cc_ref[0])
    scp = _split2(cc_ref[1])
    ri = lax.broadcasted_iota(jnp.int32, (tk, tk), 0)
    ci = lax.broadcasted_iota(jnp.int32, (tk, tk), 1)
    anti = jnp.where(ri + ci == tk - 1, 1.0, 0.0).astype(BF)
    for j in range(s // (2 * tk)):
        ac = _dot_x3(_split2(cs_ref[0, j]), hparts)
        as_ = _dot_x3(_split2(cs_ref[1, j]), hparts)
        direct, mirror = [], []
        for gi in range(N_GROUPS):
            cs = slice(gi * GROUP_W, (gi + 1) * GROUP_W)
            p = _dot_x3(_split2(ac[:, cs]), ccp)
            q = _dot_x3(_split2(as_[:, cs]), scp)
            direct.append((p - q)[0:tk])
            mirror.append((p + q)[1:tk + 1])
        o_ref[j * tk:(j + 1) * tk, :] = jnp.concatenate(direct, axis=1).astype(BF)
        mir = jnp.concatenate(mirror, axis=1).astype(BF)
        o_ref[s - (j + 1) * tk:s - j * tk, :] = _dot(anti, mir).astype(BF)


def _half_dft_mats(s, tk):
    full = _dft_mats(s, s ** -0.5)
    rows = (np.arange(s // (2 * tk))[:, None] * tk + np.arange(tk + FT_PAD)[None, :]) % s
    return full[:, rows, :]


def _fourier(x, mod, layer, s, nseq, row_off, mod_row_fn):
    tk = min(FT_TK, s // 2)
    seq_off = row_off // s
    cs = jnp.asarray(_half_dft_mats(s, tk))
    cc = jnp.asarray(_dft_mats(GROUP_W, GROUP_W ** -0.5))
    return pl.pallas_call(
        functools.partial(_fourier_kernel, s=s, tk=tk),
        grid=(nseq,),
        in_specs=[pl.BlockSpec((s, D), lambda b: (seq_off + b, 0)),
                  pl.BlockSpec((None, None, 6, D), lambda b: (layer, mod_row_fn(b), 0, 0)),
                  pl.BlockSpec(cs.shape, lambda b: (0, 0, 0, 0), pipeline_mode=pl.Buffered(1)),
                  _full((2, GROUP_W, GROUP_W))],
        out_specs=pl.BlockSpec((s, D), lambda b: (b, 0)),
        out_shape=jax.ShapeDtypeStruct((nseq * s, D), BF),
        compiler_params=_cparams(1),
        name="fourier_%d" % s,
    )(x, mod, cs, cc)


def _pos_embed():
    rows = DEC_SEQ // GRID_W
    quarter = D // 4
    omega = 1.0 / (10000.0 ** (np.arange(quarter, dtype=np.float64) / quarter))
    rr, cc = np.meshgrid(np.arange(rows, dtype=np.float64), np.arange(GRID_W, dtype=np.float64), indexing="ij")
    er = rr.reshape(-1, 1) * omega
    ec = cc.reshape(-1, 1) * omega
    return np.concatenate([np.sin(er), np.cos(er), np.sin(ec), np.cos(ec)], axis=-1).astype(np.float32)


def _gate_perm():
    perm = np.zeros(4 * H, dtype=np.int32)
    for hp in range(N_HP):
        for d in range(2):
            for j in range(2):
                for hh in range(2):
                    perm[hp * NG + d * 4 + j * 2 + hh] = d * 2 * H + j * H + 2 * hp + hh
    return perm


def kernel(x_prompt, x_sample, state_C, state_n, state_m, c, c_ctx, w_mod, b_mod, ln_g, ln_b, mlp_w1, mlp_w2,
           ml_w_in, ml_w_gate, ml_b_gate, ml_norm_g, ml_w_out, sc_w_in, sc_conv_w, sc_w_out,
           pl_w, pl_b, pl_scale, ft_w_out, ft_b_out):
    cond = jnp.concatenate([c_ctx[None, :], c, jnp.zeros((NCOND - 1 - DEC_BATCH, D), F32)], axis=0)
    mod = _modulation(cond, w_mod, b_mod)

    x = (x_prompt.reshape(NCTX, D), x_sample.reshape(NLAT, D), jnp.asarray(_pos_embed()))

    new_c = new_n = new_m = None
    for i in range(DEPTH):
        kind, j = i % 4, i // 4
        if kind == 0:
            w_in = ml_w_in[j].astype(BF)
            perm = _gate_perm()
            wg = ml_w_gate[j][:, perm].astype(BF)
            bg = ml_b_gate[j][perm]
            wk = w_in[:, QK:2 * QK]
            qt, k, kt, vt, o, g, gt = _ml_proj(
                x, mod, i, w_in[:, :QK].T, wk, wk.T, w_in[:, 2 * QK:2 * QK + V].T, w_in[:, 2 * QK + V:],
                wg, wg.T, bg.reshape(1, -1), bg.reshape(-1, 1))
            scan_args = (qt, k, kt, vt, o, g, gt, ml_norm_g[j])
            uc, cst, nst, mst = _ml_scan(*scan_args)
            c0 = jnp.concatenate([jnp.swapaxes(state_C[:, j], -1, -2), state_n[:, j][..., None, :],
                                  jnp.zeros((DEC_BATCH, 2, H, DHA - DHV - 1, DQK), F32)], axis=-2)
            m0 = jnp.broadcast_to(state_m[:, j][..., None, None], (DEC_BATCH, 2, H, 1, 128))
            (ul,) = _ml_scan(*scan_args, c0, m0)
            x = _out_proj(uc, ul, ml_w_out[j].astype(BF), jnp.zeros((D,), F32), x, mod, i, ln_g[i, 0], ln_b[i, 0])
            new_c = cst[:, None]
            new_n = nst[..., 0][:, None]
            new_m = mst[..., 0, 0][:, None]
        elif kind == 1:
            x = _sconv(x, mod, i, sc_w_in[j].astype(BF), sc_conv_w[j], sc_w_out[j].astype(BF),
                       ln_g[i, 0], ln_b[i, 0])
        elif kind == 2:
            x = _pool(x, mod, i, pl_w[j].astype(BF), pl_b[j], pl_scale[j], ln_g[i, 0], ln_b[i, 0])
        else:
            fc = _fourier(x, mod, i, SEQ, BATCH, 0, lambda b: 0)
            fl = _fourier(x, mod, i, DEC_SEQ, DEC_BATCH, NCTX, lambda b: 1 + b)
            x = _out_proj(fc, fl, ft_w_out[j].astype(BF), ft_b_out[j], x, mod, i, ln_g[i, 0], ln_b[i, 0])
        x = _mlp(x, mod, i, mlp_w1, mlp_w2, ln_g[i, 1], ln_b[i, 1],
                 split_out=(i == DEPTH - 1))

    y_prompt = x[0].reshape(BATCH, SEQ, D)
    y_sample = x[1].reshape(DEC_BATCH, DEC_SEQ, D)
    return y_prompt, y_sample, new_c, new_n, new_m
```

```python
import functools
import math

import numpy as np
import jax
import jax.numpy as jnp
from jax import lax
from jax.experimental import pallas as pl
from jax.experimental.pallas import tpu as pltpu

D = 1024
BATCH, SEQ = 16, 256
DEC_BATCH, DEC_SEQ = 4, 1024
DEPTH = 4
GRID_W = 64
H = 8
DQK = 64
DHV = 128
QK = H * DQK
V = H * DHV
N_GROUPS = 4
GROUP_W = D // N_GROUPS
POOL_WINDOWS = (2, 4, 8, 16)
D_FF = 4 * D
ALPHA = (2.0 * DEPTH) ** 0.25
LN_EPS = 1e-5
F32 = jnp.float32
BF = jnp.bfloat16

NCTX = BATCH * SEQ
NLAT = DEC_BATCH * DEC_SEQ
NTOK = NCTX + NLAT
NCOND = 8
CHUNK = 256
VMEM_LIMIT = 56 * 1024 * 1024


def _cparams(n_axes):
    return pltpu.CompilerParams(dimension_semantics=("arbitrary",) * n_axes,
                                vmem_limit_bytes=VMEM_LIMIT)


def _mod_row(row0):
    return jnp.where(row0 < NCTX, 0, 1 + (row0 - NCTX) // DEC_SEQ)


def _seq_len(row0):
    return jnp.where(row0 < NCTX, SEQ, DEC_SEQ)


def _ln(z, g, b):
    mu = jnp.mean(z, axis=-1, keepdims=True)
    zc = z - mu
    var = jnp.mean(zc * zc, axis=-1, keepdims=True)
    return zc * lax.rsqrt(var + LN_EPS) * g + b


def _dot(a, b):
    return jnp.dot(a, b, preferred_element_type=F32)


def _dot_nt(a, b):
    return lax.dot_general(a, b, (((1,), (1,)), ((), ())), preferred_element_type=F32)


def _split2(x):
    hi = x.astype(BF)
    lo = (x - hi.astype(F32)).astype(BF)
    return hi, lo


def _split3(x):
    hi = x.astype(BF)
    r = x - hi.astype(F32)
    mid = r.astype(BF)
    lo = (r - mid.astype(F32)).astype(BF)
    return hi, mid, lo


def _full(shape):
    n = len(shape)
    return pl.BlockSpec(shape, lambda *_: (0,) * n)


def _x_specs(x, tm):
    if not isinstance(x, tuple):
        return [pl.BlockSpec((tm, D), lambda t: (t, 0))], [x]
    nc = NCTX // tm
    return ([pl.BlockSpec((tm, D), lambda t: (jnp.minimum(t, nc - 1), 0)),
             pl.BlockSpec((tm, D), lambda t: (jnp.maximum(t - nc, 0), 0)),
             pl.BlockSpec((tm, D), lambda t: (t % (DEC_SEQ // tm), 0))], list(x))


def _read_x(x_refs, tm):
    if len(x_refs) == 1:
        return x_refs[0][...]
    xc_ref, xl_ref, pe_ref = x_refs
    return jnp.where(pl.program_id(0) * tm < NCTX, xc_ref[...], xl_ref[...] + pe_ref[...])


def _mod_spec(layer, tm):
    return pl.BlockSpec((None, None, 6, D), lambda t: (layer, _mod_row(t * tm), 0, 0))


def _mod_kernel(c_ref, w_ref, b_ref, o_ref):
    c = c_ref[...]
    s = c * jax.nn.sigmoid(c)
    o_ref[...] = _dot(s.astype(BF), w_ref[...].astype(BF)) + b_ref[...]


def _modulation(cond, w_mod, b_mod):
    tn = 1536
    out = pl.pallas_call(
        _mod_kernel,
        grid=(DEPTH, 6 * D // tn),
        in_specs=[pl.BlockSpec((NCOND, D), lambda i, j: (0, 0)),
                  pl.BlockSpec((None, D, tn), lambda i, j: (i, 0, j)),
                  pl.BlockSpec((None, 1, tn), lambda i, j: (i, 0, j))],
        out_specs=pl.BlockSpec((None, NCOND, tn), lambda i, j: (i, 0, j)),
        out_shape=jax.ShapeDtypeStruct((DEPTH, NCOND, 6 * D), F32),
        compiler_params=_cparams(2),
        name="modulation",
    )(cond, w_mod, b_mod.reshape(DEPTH, 1, 6 * D))
    return out.reshape(DEPTH, NCOND, 6, D)


MLP_TM = 512
MLP_FC = 512
MLP_NF = D_FF // MLP_FC
MLP_NT = NTOK // MLP_TM


def _mlp_tile(s):
    return jnp.maximum(s - (MLP_NF - 1), 0)


def _mlp_kernel(mod_ref, w1_ref, w2_ref, g_ref, b_ref, x1_ref, x_ref, *refs):
    o_refs, (w1s_ref, w2s_ref, acc_ref, h_ref) = refs[:-4], refs[-4:]
    s = pl.program_id(0)
    m = mod_ref[...]

    def pre(x):
        return (x * (1.0 + m[4:5]) + m[3:4]).astype(BF)

    def hidden(h, w1c):
        return jnp.square(jnp.maximum(_dot(h, w1c), 0.0)).astype(BF)

    def result(x, acc):
        return _ln(ALPHA * x + m[5:6] * acc, g_ref[...], b_ref[...])

    def accumulate(i, first, part):
        @pl.when(first)
        def _():
            acc_ref[i] = part

        @pl.when(jnp.logical_not(first))
        def _():
            acc_ref[i] += part

    @pl.when(s < MLP_NF)
    def _():
        @pl.when(s == 0)
        def _():
            h_ref[0] = pre(x_ref[...])

        w1c = w1_ref[...].astype(BF)
        w2c = w2_ref[...].astype(BF)
        w1s_ref[s] = w1c
        w2s_ref[s] = w2c
        accumulate(0, s == 0, _dot(hidden(h_ref[0], w1c), w2c))

        @pl.when(s == MLP_NF - 1)
        def _():
            o_refs[0][...] = result(x_ref[...], acc_ref[0])

    @pl.when(jnp.logical_and(s >= 1, s <= MLP_NF))
    def _():
        @pl.when(s == 1)
        def _():
            h_ref[1] = pre(x1_ref[...])

        accumulate(1, s == 1, _dot(hidden(h_ref[1], w1s_ref[s - 1]), w2s_ref[s - 1]))

        @pl.when(s == MLP_NF)
        def _():
            o_refs[0][...] = result(x1_ref[...], acc_ref[1])

    @pl.when(s > MLP_NF)
    def _():
        x = x_ref[...]
        h = pre(x)
        acc = jnp.zeros((MLP_TM, D), F32)
        for f in range(MLP_NF):
            acc = acc + _dot(hidden(h, w1s_ref[f]), w2s_ref[f])
        res = result(x, acc)
        if len(o_refs) == 1:
            o_refs[0][...] = res
        else:
            is_ctx = _mlp_tile(s) * MLP_TM < NCTX

            @pl.when(is_ctx)
            def _():
                o_refs[0][...] = res

            @pl.when(jnp.logical_not(is_ctx))
            def _():
                o_refs[1][...] = res


def _mlp(x, mod, layer, w1, w2, ln_g, ln_b, split_out=False):
    tm = MLP_TM
    nc = NCTX // tm
    assert nc >= 2 and MLP_NT >= 2
    if split_out:
        out_specs = [pl.BlockSpec((tm, D), lambda s: (jnp.minimum(_mlp_tile(s), nc - 1), 0)),
                     pl.BlockSpec((tm, D), lambda s: (jnp.maximum(_mlp_tile(s) - nc, 0), 0))]
        out_shape = [jax.ShapeDtypeStruct((NCTX, D), F32), jax.ShapeDtypeStruct((NLAT, D), F32)]
    else:
        out_specs = pl.BlockSpec((tm, D), lambda s: (_mlp_tile(s), 0))
        out_shape = jax.ShapeDtypeStruct((NTOK, D), F32)
    return pl.pallas_call(
        _mlp_kernel,
        grid=(MLP_NF - 1 + MLP_NT,),
        in_specs=[pl.BlockSpec((None, None, 6, D), lambda s: (layer, _mod_row(_mlp_tile(s) * tm), 0, 0)),
                  pl.BlockSpec((None, D, MLP_FC), lambda s: (layer, 0, jnp.minimum(s, MLP_NF - 1))),
                  pl.BlockSpec((None, MLP_FC, D), lambda s: (layer, jnp.minimum(s, MLP_NF - 1), 0)),
                  _full((1, D)), _full((1, D)),
                  pl.BlockSpec((tm, D), lambda s: (1, 0)),
                  pl.BlockSpec((tm, D), lambda s: (_mlp_tile(s), 0))],
        out_specs=out_specs,
        out_shape=out_shape,
        scratch_shapes=[pltpu.VMEM((MLP_NF, D, MLP_FC), BF),
                        pltpu.VMEM((MLP_NF, MLP_FC, D), BF),
                        pltpu.VMEM((2, tm, D), F32),
                        pltpu.VMEM((2, tm, D), BF)],
        compiler_params=_cparams(1),
        name="mlp",
    )(mod, w1, w2, ln_g.reshape(1, D), ln_b.reshape(1, D), x, x)


OUT_TM = 1024


def _out_kernel(uc_ref, ul_ref, w_ref, bias_ref, mod_ref, g_ref, b_ref, *refs):
    x_refs, o_ref = refs[:-1], refs[-1]
    is_ctx = pl.program_id(0) * OUT_TM < NCTX
    u = jnp.where(is_ctx, uc_ref[...], ul_ref[...])
    y = _dot(u, w_ref[...]) + bias_ref[...]
    m = mod_ref[...]
    o_ref[...] = _ln(ALPHA * _read_x(x_refs, OUT_TM) + m[2:3] * y, g_ref[...], b_ref[...])


def _out_proj(uc, ul, w, bias, x, mod, layer, ln_g, ln_b):
    tm = OUT_TM
    nc = NCTX // tm
    k = w.shape[0]
    x_specs, x_args = _x_specs(x, tm)
    return pl.pallas_call(
        _out_kernel,
        grid=(NTOK // tm,),
        in_specs=[pl.BlockSpec((tm, k), lambda t: (jnp.minimum(t, nc - 1), 0)),
                  pl.BlockSpec((tm, k), lambda t: (jnp.maximum(t - nc, 0), 0)),
                  pl.BlockSpec((k, D), lambda t: (0, 0), pipeline_mode=pl.Buffered(1)),
                  _full((1, D)),
                  _mod_spec(layer, tm),
                  _full((1, D)), _full((1, D))] + x_specs,
        out_specs=pl.BlockSpec((tm, D), lambda t: (t, 0)),
        out_shape=jax.ShapeDtypeStruct((NTOK, D), F32),
        compiler_params=_cparams(1),
        name="out_proj",
    )(uc, ul, w, bias.reshape(1, D), mod, ln_g.reshape(1, D), ln_b.reshape(1, D), *x_args)


ML_TM = 512
N_HP = H // 2
NG = 8


def _log_sigmoid(g):
    return jnp.minimum(g, 0.0) - jnp.log1p(jnp.exp(-jnp.abs(g)))


def _ml_proj_kernel(mod_ref, wqt_ref, wk_ref, wkt_ref, wvt_ref, wo_ref, wg_ref, wgt_ref, bg_ref, bgt_ref, *refs):
    x_refs = refs[:-7]
    qt_ref, k_ref, kt_ref, vt_ref, o_ref, g_ref, gt_ref = refs[-7:]
    m = mod_ref[...]
    h = (_read_x(x_refs, ML_TM) * (1.0 + m[1:2]) + m[0:1]).astype(BF)
    qt_ref[...] = _dot_nt(wqt_ref[...], h).astype(BF)
    k_ref[...] = (_dot(h, wk_ref[...]) * (DQK ** -0.5)).astype(BF)
    kt_ref[...] = (_dot_nt(wkt_ref[...], h) * (DQK ** -0.5)).astype(BF)
    vt_ref[...] = _dot_nt(wvt_ref[...], h).astype(BF)
    o_ref[...] = _dot(h, wo_ref[...])
    g = _dot(h, wg_ref[...]) + bg_ref[...]
    col = lax.broadcasted_iota(jnp.int32, g.shape, 1)
    g = jnp.where((col & 2) != 0, _log_sigmoid(g), g)
    gt = _dot_nt(wgt_ref[...], h) + bgt_ref[...]
    row = lax.broadcasted_iota(jnp.int32, gt.shape, 0)
    gt = jnp.where((row & 2) != 0, _log_sigmoid(gt), gt)
    for hp in range(N_HP):
        g_ref[hp] = g[:, hp * NG:(hp + 1) * NG]
        gt_ref[hp] = gt[hp * NG:(hp + 1) * NG, :]


def _ml_proj(x, mod, layer, wqt, wk, wkt, wvt, wo, wg, wgt, bg, bgt):
    tm = ML_TM
    res = lambda shape: pl.BlockSpec(shape, lambda t: (0,) * len(shape), pipeline_mode=pl.Buffered(1))
    x_specs, x_args = _x_specs(x, tm)
    return pl.pallas_call(
        _ml_proj_kernel,
        grid=(NTOK // tm,),
        in_specs=[_mod_spec(layer, tm),
                  res((QK, D)), res((D, QK)), res((QK, D)), res((V, D)), res((D, V)),
                  res((D, 4 * NG)), res((4 * NG, D)), res((1, 4 * NG)), res((4 * NG, 1))] + x_specs,
        out_specs=[pl.BlockSpec((QK, tm), lambda t: (0, t)),
                   pl.BlockSpec((tm, QK), lambda t: (t, 0)),
                   pl.BlockSpec((QK, tm), lambda t: (0, t)),
                   pl.BlockSpec((V, tm), lambda t: (0, t)),
                   pl.BlockSpec((tm, V), lambda t: (t, 0)),
                   pl.BlockSpec((N_HP, tm, NG), lambda t: (0, t, 0)),
                   pl.BlockSpec((N_HP, NG, tm), lambda t: (0, 0, t))],
        out_shape=[jax.ShapeDtypeStruct((QK, NTOK), BF),
                   jax.ShapeDtypeStruct((NTOK, QK), BF),
                   jax.ShapeDtypeStruct((QK, NTOK), BF),
                   jax.ShapeDtypeStruct((V, NTOK), BF),
                   jax.ShapeDtypeStruct((NTOK, V), F32),
                   jax.ShapeDtypeStruct((N_HP, NTOK, NG), F32),
                   jax.ShapeDtypeStruct((N_HP, NG, NTOK), F32)],
        compiler_params=_cparams(1),
        name="mlstm_proj",
    )(mod, wqt, wk, wkt, wvt, wo, wg, wgt, bg, bgt, *x_args)


DHA = DHV + 8


def _scan_chunk(k_c, kt_c, qt_c, vta, rr_col, rr_row, b_row, m, cta, d, need_state):
    L = k_c.shape[0]
    si = lax.broadcasted_iota(jnp.int32, (L, L), 0)
    ti = lax.broadcasted_iota(jnp.int32, (L, L), 1)
    mask = (si <= ti) if d == 0 else (si >= ti)
    rrm = jnp.where(mask, rr_col, -jnp.inf)
    g = jnp.maximum(jnp.max(rrm, axis=0, keepdims=True), m)
    st = _dot(k_c, qt_c) * jnp.exp(rrm - g)
    num = _dot(vta[:DHV], st.astype(BF))
    den = jnp.sum(st, axis=0, keepdims=True)
    if cta is not None:
        a = jnp.exp(m - g)
        qc = _dot(cta.astype(BF), qt_c)
        num = num + a * qc[:DHV]
        den = den + a * qc[DHV:DHV + 1]
    mt = b_row + g
    ht = num * (1.0 / jnp.maximum(jnp.abs(den), jnp.exp(-mt)))
    if not need_state:
        return ht, None, None
    last = L - 1 if d == 0 else 0
    m_new = mt[:, last:last + 1]
    b_last = b_row[:, last:last + 1]
    w = jnp.exp(b_last + rr_row - m_new)
    kw = (kt_c.astype(F32) * w).astype(BF)
    if cta is None:
        return ht, (_dot_nt(kw, vta[:DHV]), _dot_nt(vta[DHV:], kw)), m_new
    c_new = jnp.exp(b_last + m - m_new) * cta + _dot_nt(vta, kw)
    return ht, c_new, m_new


def _scan_kernel(*refs, is_ctx):
    if is_ctx:
        qt_ref, k_ref, kt_ref, vt_ref, o_ref, g_ref, gt_ref, ng_ref, u_ref, cst_ref, nmst_ref, hbuf = refs
    else:
        qt_ref, k_ref, kt_ref, vt_ref, o_ref, g_ref, gt_ref, ng_ref, c0_ref, m0_ref, u_ref, hbuf = refs
    L = CHUNK
    nch = DEC_SEQ // L
    ri = lax.broadcasted_iota(jnp.int32, (L, L), 0)
    ci = lax.broadcasted_iota(jnp.int32, (L, L), 1)
    lower = jnp.where(ci <= ri, 1.0, 0.0).astype(BF)
    upper = jnp.where(ci >= ri, 1.0, 0.0).astype(BF)
    sub = lax.broadcasted_iota(jnp.int32, (DHA - DHV, L), 0)
    ones_rows = jnp.where(sub == 0, 1.0, 0.0).astype(BF)

    rr_cols, rr_rows, b_rows = [], [], []
    tparts = _split3(jnp.concatenate([gt_ref[:, c * L:(c + 1) * L] for c in range(nch)], axis=0))
    b_row_all = (sum(_dot(p, upper) for p in tparts), sum(_dot(p, lower) for p in tparts))
    for c in range(nch):
        gc = g_ref[c * L:(c + 1) * L, :]
        gtc = gt_ref[:, c * L:(c + 1) * L]
        gparts = _split3(gc)
        b_col = (sum(_dot(lower, p) for p in gparts), sum(_dot(upper, p) for p in gparts))
        b_row = tuple(b[c * NG:(c + 1) * NG, :] for b in b_row_all)
        rr_cols.append([gc[:, 0:NG - 2] - b[:, 2:NG] for b in b_col])
        rr_rows.append([gtc[0:NG - 2, :] - b[2:NG, :] for b in b_row])
        b_rows.append(b_row)

    for hh in range(2):
        for d in range(2):
            ig_i, lf_i = d * 4 + hh, d * 4 + 2 + hh
            if is_ctx:
                m, cta = jnp.zeros((1, 1), F32), None
            else:
                m = m0_ref[d, hh][:, 0:1]
                cta = c0_ref[d, hh]
            order = range(nch) if d == 0 else range(nch - 1, -1, -1)
            for n, c in enumerate(order):
                sl = slice(c * L, (c + 1) * L)
                hs = slice(hh * DQK, (hh + 1) * DQK)
                vta = jnp.concatenate([vt_ref[hh * DHV:(hh + 1) * DHV, sl], ones_rows], axis=0)
                need_state = is_ctx or n < nch - 1
                hcur, c_new, m_new = _scan_chunk(
                    k_ref[sl, hs], kt_ref[hs, sl], qt_ref[hs, sl], vta,
                    rr_cols[c][d][:, ig_i:ig_i + 1], rr_rows[c][d][ig_i:ig_i + 1, :],
                    b_rows[c][d][lf_i:lf_i + 1, :], m, cta, d, need_state)
                if d == 0:
                    hbuf[hh * DHV:(hh + 1) * DHV, sl] = hcur
                else:
                    hbuf[hh * DHV:(hh + 1) * DHV, sl] += hcur
                if is_ctx:
                    cst_ref[c, d, hh] = c_new[0]
                    nmst_ref[c, d, hh] = jnp.concatenate(
                        [c_new[1], jnp.broadcast_to(m_new, (8, 128 - DQK))], axis=1)
                else:
                    m, cta = m_new, c_new

    for hh in range(2):
        hsum = hbuf[hh * DHV:(hh + 1) * DHV, :]
        mu = jnp.mean(hsum, axis=0, keepdims=True)
        hc = hsum - mu
        var = jnp.mean(hc * hc, axis=0, keepdims=True)
        hn = hc * lax.rsqrt(var + LN_EPS) * ng_ref[hh * DHV:(hh + 1) * DHV, :]
        og = jax.nn.sigmoid(o_ref[:, hh * DHV:(hh + 1) * DHV])
        u_ref[:, hh * DHV:(hh + 1) * DHV] = (hn.T * og).astype(BF)


def _ml_scan(qt, k, kt, vt, o, g, gt, norm_g, c0aug=None, m0b=None):
    is_ctx = c0aug is None
    tt = DEC_SEQ
    toff = 0 if is_ctx else NCTX // tt
    nt = (NCTX if is_ctx else NLAT) // tt
    in_specs = [pl.BlockSpec((2 * DQK, tt), lambda t, p: (p, toff + t)),
                pl.BlockSpec((tt, 2 * DQK), lambda t, p: (toff + t, p)),
                pl.BlockSpec((2 * DQK, tt), lambda t, p: (p, toff + t)),
                pl.BlockSpec((2 * DHV, tt), lambda t, p: (p, toff + t)),
                pl.BlockSpec((tt, 2 * DHV), lambda t, p: (toff + t, p)),
                pl.BlockSpec((None, tt, NG), lambda t, p: (p, toff + t, 0)),
                pl.BlockSpec((None, NG, tt), lambda t, p: (p, 0, toff + t)),
                pl.BlockSpec((2 * DHV, 1), lambda t, p: (p, 0))]
    args = [qt, k, kt, vt, o, g, gt, norm_g.reshape(V, 1)]
    u_spec = pl.BlockSpec((tt, 2 * DHV), lambda t, p: (t, p))
    u_shape = jax.ShapeDtypeStruct((nt * tt, V), BF)
    if is_ctx:
        nseq = tt // SEQ
        out_specs = [u_spec,
                     pl.BlockSpec((nseq, 2, 2, DQK, DHV), lambda t, p: (t, 0, p, 0, 0)),
                     pl.BlockSpec((nseq, 2, 2, 8, 128), lambda t, p: (t, 0, p, 0, 0))]
        out_shape = [u_shape,
                     jax.ShapeDtypeStruct((BATCH, 2, H, DQK, DHV), F32),
                     jax.ShapeDtypeStruct((BATCH, 2, H, 8, 128), F32)]
    else:
        in_specs += [pl.BlockSpec((None, 2, 2, DHA, DQK), lambda t, p: (t, 0, p, 0, 0)),
                     pl.BlockSpec((None, 2, 2, 1, 128), lambda t, p: (t, 0, p, 0, 0))]
        args += [c0aug, m0b]
        out_specs = [u_spec]
        out_shape = [u_shape]
    return pl.pallas_call(
        functools.partial(_scan_kernel, is_ctx=is_ctx),
        grid=(nt, N_HP),
        in_specs=in_specs,
        out_specs=out_specs,
        out_shape=out_shape,
        scratch_shapes=[pltpu.VMEM((2 * DHV, tt), F32)],
        compiler_params=_cparams(2),
        name="mlstm_scan_ctx" if is_ctx else "mlstm_scan_lat",
    )(*args)


SC_TM = 512


SC_HALO = 8


def _sconv_kernel(x_ref, prev_ref, next_ref, mod_ref, w_in_ref, cw_ref, w_out_ref, g_ref, b_ref, o_ref, e_ref):
    tm = SC_TM
    row0 = pl.program_id(0) * tm
    slen = _seq_len(row0)
    pos = (row0 + lax.broadcasted_iota(jnp.int32, (tm, 1), 0)) & (slen - 1)
    m = mod_ref[...]
    x = x_ref[...]
    xe = jnp.concatenate([prev_ref[...], x, next_ref[...]], axis=0)
    he = (xe * (1.0 + m[1:2]) + m[0:1]).astype(BF)
    e_ref[...] = _dot(he, w_in_ref[:, D:2 * D]) * _dot(he, w_in_ref[:, 2 * D:3 * D])
    bg = _dot(he[SC_HALO:SC_HALO + tm], w_in_ref[:, 0:D])
    prev = jnp.where(pos == 0, 0.0, e_ref[SC_HALO - 1:SC_HALO - 1 + tm, :])
    nxt = jnp.where(pos == slen - 1, 0.0, e_ref[SC_HALO + 1:SC_HALO + 1 + tm, :])
    conv = cw_ref[0:1, :] * prev + cw_ref[1:2, :] * e_ref[SC_HALO:SC_HALO + tm, :] + cw_ref[2:3, :] * nxt
    y = _dot((bg * conv).astype(BF), w_out_ref[...])
    o_ref[...] = _ln(ALPHA * x + m[2:3] * y, g_ref[...], b_ref[...])


def _sconv(x, mod, layer, w_in, conv_w, w_out, ln_g, ln_b):
    tm = SC_TM
    hb = tm // SC_HALO
    nhb = NTOK // SC_HALO
    return pl.pallas_call(
        _sconv_kernel,
        grid=(NTOK // tm,),
        in_specs=[pl.BlockSpec((tm, D), lambda t: (t, 0)),
                  pl.BlockSpec((SC_HALO, D), lambda t: (jnp.maximum(t * hb - 1, 0), 0)),
                  pl.BlockSpec((SC_HALO, D), lambda t: (jnp.minimum((t + 1) * hb, nhb - 1), 0)),
                  _mod_spec(layer, tm),
                  pl.BlockSpec((D, 3 * D), lambda t: (0, 0), pipeline_mode=pl.Buffered(1)),
                  _full((3, D)),
                  pl.BlockSpec((D, D), lambda t: (0, 0), pipeline_mode=pl.Buffered(1)),
                  _full((1, D)), _full((1, D))],
        out_specs=pl.BlockSpec((tm, D), lambda t: (t, 0)),
        out_shape=jax.ShapeDtypeStruct((NTOK, D), F32),
        scratch_shapes=[pltpu.VMEM((tm + 2 * SC_HALO, D), F32)],
        compiler_params=_cparams(1),
        name="sconv",
    )(x, x, x, mod, w_in, conv_w, w_out, ln_g.reshape(1, D), ln_b.reshape(1, D))


PL_SEG = 256
PL_NSEG = 2
PL_TM = PL_SEG * PL_NSEG
PL_HALO = 16
PL_STRIDE = PL_SEG + 2 * PL_HALO


def _pool_kernel(x_ref, prev_ref, next_ref, mod_ref, w_ref, pb_ref, ps_ref, g_ref, b_ref, o_ref,
                 e_ref, s2_ref, s4_ref, s8_ref):
    tm, seg, hl, gw = PL_TM, PL_SEG, PL_HALO, GROUP_W
    row0 = pl.program_id(0) * tm
    slen = _seq_len(row0)
    m = mod_ref[...]
    x = x_ref[...]
    h = x * (1.0 + m[1:2]) + m[0:1]
    hp = prev_ref[...] * (1.0 + m[1:2]) + m[0:1]
    hn = next_ref[...] * (1.0 + m[1:2]) + m[0:1]
    n0 = PL_NSEG * PL_STRIDE
    for i in range(PL_NSEG):
        base = i * PL_STRIDE
        pos0 = (row0 + i * seg) & (slen - 1)
        before = hp if i == 0 else h[i * seg - hl:i * seg]
        after = hn if i == PL_NSEG - 1 else h[(i + 1) * seg:(i + 1) * seg + hl]
        e_ref[base:base + hl, :] = jnp.where(pos0 != 0, before, 0.0)
        e_ref[base + hl:base + hl + seg, :] = h[i * seg:(i + 1) * seg]
        e_ref[base + hl + seg:base + PL_STRIDE, :] = jnp.where(pos0 + seg != slen, after, 0.0)
    e_ref[n0:, :] = jnp.zeros((8, D), F32)
    n2, n4, n8 = n0, n0 - 8, n0 - 16
    s2_ref[...] = e_ref[0:n2, :] + e_ref[1:n2 + 1, :]
    s4_ref[...] = s2_ref[0:n4, gw:] + s2_ref[2:n4 + 2, gw:]
    s8_ref[...] = s4_ref[0:n8, gw:] + s4_ref[4:n8 + 4, gw:]

    def rows(ref, off, lanes):
        return jnp.concatenate([ref[i * PL_STRIDE + off:i * PL_STRIDE + off + seg, lanes]
                                for i in range(PL_NSEG)], axis=0)

    tots = [rows(s2_ref, hl - 1, slice(0, gw)),
            rows(s4_ref, hl - 2, slice(0, gw)),
            rows(s8_ref, hl - 4, slice(0, gw)),
            rows(s8_ref, hl - 8, slice(gw, 2 * gw)) + rows(s8_ref, hl, slice(gw, 2 * gw))]
    pos = (row0 + lax.broadcasted_iota(jnp.int32, (tm, 1), 0)) & (slen - 1)
    ys = []
    for gi, win in enumerate(POOL_WINDOWS):
        cs = slice(gi * gw, (gi + 1) * gw)
        back = win // 2
        cnt = jnp.minimum(pos + (win - back), slen) - jnp.maximum(pos - back, 0)
        p = tots[gi] / cnt.astype(F32) - h[:, cs]
        ys.append(_dot(p.astype(BF), w_ref[gi]))
    y = (jnp.concatenate(ys, axis=1) + pb_ref[...]) * ps_ref[...]
    o_ref[...] = _ln(ALPHA * x + m[2:3] * y, g_ref[...], b_ref[...])


def _pool(x, mod, layer, w, pb, ps, ln_g, ln_b):
    assert POOL_WINDOWS == (2, 4, 8, 16) and PL_HALO >= POOL_WINDOWS[-1] // 2
    assert SEQ % PL_SEG == 0 and DEC_SEQ % PL_SEG == 0
    tm = PL_TM
    hb = tm // PL_HALO
    nhb = NTOK // PL_HALO
    n0 = PL_NSEG * PL_STRIDE
    return pl.pallas_call(
        _pool_kernel,
        grid=(NTOK // tm,),
        in_specs=[pl.BlockSpec((tm, D), lambda t: (t, 0)),
                  pl.BlockSpec((PL_HALO, D), lambda t: (jnp.maximum(t * hb - 1, 0), 0)),
                  pl.BlockSpec((PL_HALO, D), lambda t: (jnp.minimum((t + 1) * hb, nhb - 1), 0)),
                  _mod_spec(layer, tm),
                  _full((N_GROUPS, GROUP_W, GROUP_W)),
                  _full((1, D)), _full((1, D)), _full((1, D)), _full((1, D))],
        out_specs=pl.BlockSpec((tm, D), lambda t: (t, 0)),
        out_shape=jax.ShapeDtypeStruct((NTOK, D), F32),
        scratch_shapes=[pltpu.VMEM((n0 + 8, D), F32),
                        pltpu.VMEM((n0, D), F32),
                        pltpu.VMEM((n0 - 8, D - GROUP_W), F32),
                        pltpu.VMEM((n0 - 16, D - 2 * GROUP_W), F32)],
        compiler_params=_cparams(1),
        name="pool",
    )(x, x, x, mod, w, pb.reshape(1, D), ps.reshape(1, D), ln_g.reshape(1, D), ln_b.reshape(1, D))


FT_TK = 256


def _dft_mats(n, scale):
    k = np.arange(n, dtype=np.int64)
    ang = 2.0 * np.pi * ((k[:, None] * k[None, :]) % n).astype(np.float64) / n
    return np.stack([np.cos(ang) * scale, np.sin(ang) * scale]).astype(np.float32)


def _dot_x3(a_parts, b_parts):
    (ah, al), (bh, bl) = a_parts, b_parts
    return _dot(ah, bh) + (_dot(ah, bl) + _dot(al, bh))


FT_PAD = 8


def _fourier_kernel(x_ref, mod_ref, cs_ref, cc_ref, o_ref, *, s, tk):
    m = mod_ref[...]
    hparts = _split2(x_ref[...] * (1.0 + m[1:2]) + m[0:1])
    ccp = _split2(cc_ref[0])
    scp = _split2(cc_ref[1])
    ri = lax.broadcasted_iota(jnp.int32, (tk, tk), 0)
    ci = lax.broadcasted_iota(jnp.int32, (tk, tk), 1)
    anti = jnp.where(ri + ci == tk - 1, 1.0, 0.0).astype(BF)
    for j in range(s // (2 * tk)):
        ac = _dot_x3(_split2(cs_ref[0, j]), hparts)
        as_ = _dot_x3(_split2(cs_ref[1, j]), hparts)
        direct, mirror = [], []
        for gi in range(N_GROUPS):
            cs = slice(gi * GROUP_W, (gi + 1) * GROUP_W)
            p = _dot_x3(_split2(ac[:, cs]), ccp)
            q = _dot_x3(_split2(as_[:, cs]), scp)
            direct.append((p - q)[0:tk])
            mirror.append((p + q)[1:tk + 1])
        o_ref[j * tk:(j + 1) * tk, :] = jnp.concatenate(direct, axis=1).astype(BF)
        mir = jnp.concatenate(mirror, axis=1).astype(BF)
        o_ref[s - (j + 1) * tk:s - j * tk, :] = _dot(anti, mir).astype(BF)


def _half_dft_mats(s, tk):
    full = _dft_mats(s, s ** -0.5)
    rows = (np.arange(s // (2 * tk))[:, None] * tk + np.arange(tk + FT_PAD)[None, :]) % s
    return full[:, rows, :]


def _fourier(x, mod, layer, s, nseq, row_off, mod_row_fn):
    tk = min(FT_TK, s // 2)
    seq_off = row_off // s
    cs = jnp.asarray(_half_dft_mats(s, tk))
    cc = jnp.asarray(_dft_mats(GROUP_W, GROUP_W ** -0.5))
    return pl.pallas_call(
        functools.partial(_fourier_kernel, s=s, tk=tk),
        grid=(nseq,),
        in_specs=[pl.BlockSpec((s, D), lambda b: (seq_off + b, 0)),
                  pl.BlockSpec((None, None, 6, D), lambda b: (layer, mod_row_fn(b), 0, 0)),
                  pl.BlockSpec(cs.shape, lambda b: (0, 0, 0, 0), pipeline_mode=pl.Buffered(1)),
                  _full((2, GROUP_W, GROUP_W))],
        out_specs=pl.BlockSpec((s, D), lambda b: (b, 0)),
        out_shape=jax.ShapeDtypeStruct((nseq * s, D), BF),
        compiler_params=_cparams(1),
        name="fourier_%d" % s,
    )(x, mod, cs, cc)


def _pos_embed():
    rows = DEC_SEQ // GRID_W
    quarter = D // 4
    omega = 1.0 / (10000.0 ** (np.arange(quarter, dtype=np.float64) / quarter))
    rr, cc = np.meshgrid(np.arange(rows, dtype=np.float64), np.arange(GRID_W, dtype=np.float64), indexing="ij")
    er = rr.reshape(-1, 1) * omega
    ec = cc.reshape(-1, 1) * omega
    return np.concatenate([np.sin(er), np.cos(er), np.sin(ec), np.cos(ec)], axis=-1).astype(np.float32)


def _gate_perm():
    perm = np.zeros(4 * H, dtype=np.int32)
    for hp in range(N_HP):
        for d in range(2):
            for j in range(2):
                for hh in range(2):
                    perm[hp * NG + d * 4 + j * 2 + hh] = d * 2 * H + j * H + 2 * hp + hh
    return perm


def kernel(x_prompt, x_sample, state_C, state_n, state_m, c, c_ctx, w_mod, b_mod, ln_g, ln_b, mlp_w1, mlp_w2,
           ml_w_in, ml_w_gate, ml_b_gate, ml_norm_g, ml_w_out, sc_w_in, sc_conv_w, sc_w_out,
           pl_w, pl_b, pl_scale, ft_w_out, ft_b_out):
    cond = jnp.concatenate([c_ctx[None, :], c, jnp.zeros((NCOND - 1 - DEC_BATCH, D), F32)], axis=0)
    mod = _modulation(cond, w_mod, b_mod)

    x = (x_prompt.reshape(NCTX, D), x_sample.reshape(NLAT, D), jnp.asarray(_pos_embed()))

    new_c = new_n = new_m = None
    for i in range(DEPTH):
        kind, j = i % 4, i // 4
        if kind == 0:
            w_in = ml_w_in[j].astype(BF)
            perm = _gate_perm()
            wg = ml_w_gate[j][:, perm].astype(BF)
            bg = ml_b_gate[j][perm]
            wk = w_in[:, QK:2 * QK]
            qt, k, kt, vt, o, g, gt = _ml_proj(
                x, mod, i, w_in[:, :QK].T, wk, wk.T, w_in[:, 2 * QK:2 * QK + V].T, w_in[:, 2 * QK + V:],
                wg, wg.T, bg.reshape(1, -1), bg.reshape(-1, 1))
            scan_args = (qt, k, kt, vt, o, g, gt, ml_norm_g[j])
            uc, cst, nmst = _ml_scan(*scan_args)
            c0 = jnp.concatenate([jnp.swapaxes(state_C[:, j], -1, -2), state_n[:, j][..., None, :],
                                  jnp.zeros((DEC_BATCH, 2, H, DHA - DHV - 1, DQK), F32)], axis=-2)
            m0 = jnp.broadcast_to(state_m[:, j][..., None, None], (DEC_BATCH, 2, H, 1, 128))
            (ul,) = _ml_scan(*scan_args, c0, m0)
            x = _out_proj(uc, ul, ml_w_out[j].astype(BF), jnp.zeros((D,), F32), x, mod, i, ln_g[i, 0], ln_b[i, 0])
            new_c = cst[:, None]
            new_n = nmst[..., 0, :DQK][:, None]
            new_m = nmst[..., 0, DQK][:, None]
        elif kind == 1:
            x = _sconv(x, mod, i, sc_w_in[j].astype(BF), sc_conv_w[j], sc_w_out[j].astype(BF),
                       ln_g[i, 0], ln_b[i, 0])
        elif kind == 2:
            x = _pool(x, mod, i, pl_w[j].astype(BF), pl_b[j], pl_scale[j], ln_g[i, 0], ln_b[i, 0])
        else:
            fc = _fourier(x, mod, i, SEQ, BATCH, 0, lambda b: 0)
            fl = _fourier(x, mod, i, DEC_SEQ, DEC_BATCH, NCTX, lambda b: 1 + b)
            x = _out_proj(fc, fl, ft_w_out[j].astype(BF), ft_b_out[j], x, mod, i, ln_g[i, 0], ln_b[i, 0])
        x = _mlp(x, mod, i, mlp_w1, mlp_w2, ln_g[i, 1], ln_b[i, 1],
                 split_out=(i == DEPTH - 1))

    y_prompt = x[0].reshape(BATCH, SEQ, D)
    y_sample = x[1].reshape(DEC_BATCH, DEC_SEQ, D)
    return y_prompt, y_sample, new_c, new_n, new_m
```

```python
import functools
import math

import numpy as np
import jax
import jax.numpy as jnp
from jax import lax
from jax.experimental import pallas as pl
from jax.experimental.pallas import tpu as pltpu

D = 1024
BATCH, SEQ = 16, 256
DEC_BATCH, DEC_SEQ = 4, 1024
DEPTH = 4
GRID_W = 64
H = 8
DQK = 64
DHV = 128
QK = H * DQK
V = H * DHV
N_GROUPS = 4
GROUP_W = D // N_GROUPS
POOL_WINDOWS = (2, 4, 8, 16)
D_FF = 4 * D
ALPHA = (2.0 * DEPTH) ** 0.25
LN_EPS = 1e-5
F32 = jnp.float32
BF = jnp.bfloat16

NCTX = BATCH * SEQ
NLAT = DEC_BATCH * DEC_SEQ
NTOK = NCTX + NLAT
NCOND = 8
CHUNK = 256
VMEM_LIMIT = 56 * 1024 * 1024


def _cparams(n_axes):
    return pltpu.CompilerParams(dimension_semantics=("arbitrary",) * n_axes,
                                vmem_limit_bytes=VMEM_LIMIT)


def _mod_row(row0):
    return jnp.where(row0 < NCTX, 0, 1 + (row0 - NCTX) // DEC_SEQ)


def _seq_len(row0):
    return jnp.where(row0 < NCTX, SEQ, DEC_SEQ)


def _ln(z, g, b):
    mu = jnp.mean(z, axis=-1, keepdims=True)
    zc = z - mu
    var = jnp.mean(zc * zc, axis=-1, keepdims=True)
    return zc * lax.rsqrt(var + LN_EPS) * g + b


def _dot(a, b):
    return jnp.dot(a, b, preferred_element_type=F32)


def _dot_nt(a, b):
    return lax.dot_general(a, b, (((1,), (1,)), ((), ())), preferred_element_type=F32)


def _split2(x):
    hi = x.astype(BF)
    lo = (x - hi.astype(F32)).astype(BF)
    return hi, lo


def _split3(x):
    hi = x.astype(BF)
    r = x - hi.astype(F32)
    mid = r.astype(BF)
    lo = (r - mid.astype(F32)).astype(BF)
    return hi, mid, lo


def _full(shape):
    n = len(shape)
    return pl.BlockSpec(shape, lambda *_: (0,) * n)


def _x_specs(x, tm):
    if not isinstance(x, tuple):
        return [pl.BlockSpec((tm, D), lambda t: (t, 0))], [x]
    nc = NCTX // tm
    return ([pl.BlockSpec((tm, D), lambda t: (jnp.minimum(t, nc - 1), 0)),
             pl.BlockSpec((tm, D), lambda t: (jnp.maximum(t - nc, 0), 0)),
             pl.BlockSpec((tm, D), lambda t: (t % (DEC_SEQ // tm), 0))], list(x))


def _read_x(x_refs, tm):
    if len(x_refs) == 1:
        return x_refs[0][...]
    xc_ref, xl_ref, pe_ref = x_refs
    return jnp.where(pl.program_id(0) * tm < NCTX, xc_ref[...], xl_ref[...] + pe_ref[...])


def _mod_spec(layer, tm):
    return pl.BlockSpec((None, None, 6, D), lambda t: (layer, _mod_row(t * tm), 0, 0))


def _mod_kernel(c_ref, w_ref, b_ref, o_ref):
    c = c_ref[...]
    s = c * jax.nn.sigmoid(c)
    o_ref[...] = _dot(s.astype(BF), w_ref[...].astype(BF)) + b_ref[...]


def _modulation(cond, w_mod, b_mod):
    tn = 1536
    out = pl.pallas_call(
        _mod_kernel,
        grid=(DEPTH, 6 * D // tn),
        in_specs=[pl.BlockSpec((NCOND, D), lambda i, j: (0, 0)),
                  pl.BlockSpec((None, D, tn), lambda i, j: (i, 0, j)),
                  pl.BlockSpec((None, 1, tn), lambda i, j: (i, 0, j))],
        out_specs=pl.BlockSpec((None, NCOND, tn), lambda i, j: (i, 0, j)),
        out_shape=jax.ShapeDtypeStruct((DEPTH, NCOND, 6 * D), F32),
        compiler_params=_cparams(2),
        name="modulation",
    )(cond, w_mod, b_mod.reshape(DEPTH, 1, 6 * D))
    return out.reshape(DEPTH, NCOND, 6, D)


MLP_TM = 512
MLP_FC = 512
MLP_NF = D_FF // MLP_FC
MLP_NT = NTOK // MLP_TM


def _mlp_tile(s):
    return jnp.maximum(s - (MLP_NF - 1), 0)


def _mlp_kernel(mod_ref, w1_ref, w2_ref, g_ref, b_ref, x1_ref, x_ref, *refs):
    o_refs, (w1s_ref, w2s_ref, acc_ref, h_ref) = refs[:-4], refs[-4:]
    s = pl.program_id(0)
    m = mod_ref[...]

    def pre(x):
        return (x * (1.0 + m[4:5]) + m[3:4]).astype(BF)

    def hidden(h, w1c):
        return jnp.square(jnp.maximum(_dot(h, w1c), 0.0)).astype(BF)

    def result(x, acc):
        return _ln(ALPHA * x + m[5:6] * acc, g_ref[...], b_ref[...])

    def accumulate(i, first, part):
        @pl.when(first)
        def _():
            acc_ref[i] = part

        @pl.when(jnp.logical_not(first))
        def _():
            acc_ref[i] += part

    @pl.when(s < MLP_NF)
    def _():
        @pl.when(s == 0)
        def _():
            h_ref[0] = pre(x_ref[...])

        w1c = w1_ref[...].astype(BF)
        w2c = w2_ref[...].astype(BF)
        part = _dot(hidden(h_ref[0], w1c), w2c)
        w1s_ref[s] = w1c
        w2s_ref[s] = w2c
        accumulate(0, s == 0, part)

        @pl.when(s == MLP_NF - 1)
        def _():
            o_refs[0][...] = result(x_ref[...], acc_ref[0])

    @pl.when(jnp.logical_and(s >= 1, s <= MLP_NF))
    def _():
        @pl.when(s == 1)
        def _():
            h_ref[1] = pre(x1_ref[...])

        accumulate(1, s == 1, _dot(hidden(h_ref[1], w1s_ref[s - 1]), w2s_ref[s - 1]))

        @pl.when(s == MLP_NF)
        def _():
            o_refs[0][...] = result(x1_ref[...], acc_ref[1])

    @pl.when(s > MLP_NF)
    def _():
        x = x_ref[...]
        h = pre(x)
        acc = jnp.zeros((MLP_TM, D), F32)
        for f in range(MLP_NF):
            acc = acc + _dot(hidden(h, w1s_ref[f]), w2s_ref[f])
        res = result(x, acc)
        if len(o_refs) == 1:
            o_refs[0][...] = res
        else:
            is_ctx = _mlp_tile(s) * MLP_TM < NCTX

            @pl.when(is_ctx)
            def _():
                o_refs[0][...] = res

            @pl.when(jnp.logical_not(is_ctx))
            def _():
                o_refs[1][...] = res


def _mlp(x, mod, layer, w1, w2, ln_g, ln_b, split_out=False):
    tm = MLP_TM
    nc = NCTX // tm
    assert nc >= 2 and MLP_NT >= 2
    if split_out:
        out_specs = [pl.BlockSpec((tm, D), lambda s: (jnp.minimum(_mlp_tile(s), nc - 1), 0)),
                     pl.BlockSpec((tm, D), lambda s: (jnp.maximum(_mlp_tile(s) - nc, 0), 0))]
        out_shape = [jax.ShapeDtypeStruct((NCTX, D), F32), jax.ShapeDtypeStruct((NLAT, D), F32)]
    else:
        out_specs = pl.BlockSpec((tm, D), lambda s: (_mlp_tile(s), 0))
        out_shape = jax.ShapeDtypeStruct((NTOK, D), F32)
    return pl.pallas_call(
        _mlp_kernel,
        grid=(MLP_NF - 1 + MLP_NT,),
        in_specs=[pl.BlockSpec((None, None, 6, D), lambda s: (layer, _mod_row(_mlp_tile(s) * tm), 0, 0)),
                  pl.BlockSpec((None, D, MLP_FC), lambda s: (layer, 0, jnp.minimum(s, MLP_NF - 1))),
                  pl.BlockSpec((None, MLP_FC, D), lambda s: (layer, jnp.minimum(s, MLP_NF - 1), 0)),
                  _full((1, D)), _full((1, D)),
                  pl.BlockSpec((tm, D), lambda s: (1, 0)),
                  pl.BlockSpec((tm, D), lambda s: (_mlp_tile(s), 0))],
        out_specs=out_specs,
        out_shape=out_shape,
        scratch_shapes=[pltpu.VMEM((MLP_NF, D, MLP_FC), BF),
                        pltpu.VMEM((MLP_NF, MLP_FC, D), BF),
                        pltpu.VMEM((2, tm, D), F32),
                        pltpu.VMEM((2, tm, D), BF)],
        compiler_params=_cparams(1),
        name="mlp",
    )(mod, w1, w2, ln_g.reshape(1, D), ln_b.reshape(1, D), x, x)


OUT_TM = 1024


def _out_kernel(uc_ref, ul_ref, w_ref, bias_ref, mod_ref, g_ref, b_ref, *refs):
    x_refs, o_ref = refs[:-1], refs[-1]
    is_ctx = pl.program_id(0) * OUT_TM < NCTX
    u = jnp.where(is_ctx, uc_ref[...], ul_ref[...])
    y = _dot(u, w_ref[...]) + bias_ref[...]
    m = mod_ref[...]
    o_ref[...] = _ln(ALPHA * _read_x(x_refs, OUT_TM) + m[2:3] * y, g_ref[...], b_ref[...])


def _out_proj(uc, ul, w, bias, x, mod, layer, ln_g, ln_b):
    tm = OUT_TM
    nc = NCTX // tm
    k = w.shape[0]
    x_specs, x_args = _x_specs(x, tm)
    return pl.pallas_call(
        _out_kernel,
        grid=(NTOK // tm,),
        in_specs=[pl.BlockSpec((tm, k), lambda t: (jnp.minimum(t, nc - 1), 0)),
                  pl.BlockSpec((tm, k), lambda t: (jnp.maximum(t - nc, 0), 0)),
                  pl.BlockSpec((k, D), lambda t: (0, 0), pipeline_mode=pl.Buffered(1)),
                  _full((1, D)),
                  _mod_spec(layer, tm),
                  _full((1, D)), _full((1, D))] + x_specs,
        out_specs=pl.BlockSpec((tm, D), lambda t: (t, 0)),
        out_shape=jax.ShapeDtypeStruct((NTOK, D), F32),
        compiler_params=_cparams(1),
        name="out_proj",
    )(uc, ul, w, bias.reshape(1, D), mod, ln_g.reshape(1, D), ln_b.reshape(1, D), *x_args)


ML_TM = 512
N_HP = H // 2
NG = 8


def _log_sigmoid(g):
    return jnp.minimum(g, 0.0) - jnp.log1p(jnp.exp(-jnp.abs(g)))


def _ml_proj_kernel(mod_ref, wqt_ref, wk_ref, wkt_ref, wvt_ref, wo_ref, wg_ref, wgt_ref, bg_ref, bgt_ref, *refs):
    x_refs = refs[:-7]
    qt_ref, k_ref, kt_ref, vt_ref, o_ref, g_ref, gt_ref = refs[-7:]
    m = mod_ref[...]
    h = (_read_x(x_refs, ML_TM) * (1.0 + m[1:2]) + m[0:1]).astype(BF)
    qt = _dot_nt(wqt_ref[...], h).astype(BF)
    k = (_dot(h, wk_ref[...]) * (DQK ** -0.5)).astype(BF)
    kt = (_dot_nt(wkt_ref[...], h) * (DQK ** -0.5)).astype(BF)
    vt = _dot_nt(wvt_ref[...], h).astype(BF)
    o = _dot(h, wo_ref[...])
    g = _dot(h, wg_ref[...]) + bg_ref[...]
    col = lax.broadcasted_iota(jnp.int32, g.shape, 1)
    g = jnp.where((col & 2) != 0, _log_sigmoid(g), g)
    gt = _dot_nt(wgt_ref[...], h) + bgt_ref[...]
    row = lax.broadcasted_iota(jnp.int32, gt.shape, 0)
    gt = jnp.where((row & 2) != 0, _log_sigmoid(gt), gt)
    qt_ref[...] = qt
    k_ref[...] = k
    kt_ref[...] = kt
    vt_ref[...] = vt
    o_ref[...] = o
    for hp in range(N_HP):
        g_ref[hp] = g[:, hp * NG:(hp + 1) * NG]
        gt_ref[hp] = gt[hp * NG:(hp + 1) * NG, :]


def _ml_proj(x, mod, layer, wqt, wk, wkt, wvt, wo, wg, wgt, bg, bgt):
    tm = ML_TM
    res = lambda shape: pl.BlockSpec(shape, lambda t: (0,) * len(shape), pipeline_mode=pl.Buffered(1))
    x_specs, x_args = _x_specs(x, tm)
    return pl.pallas_call(
        _ml_proj_kernel,
        grid=(NTOK // tm,),
        in_specs=[_mod_spec(layer, tm),
                  res((QK, D)), res((D, QK)), res((QK, D)), res((V, D)), res((D, V)),
                  res((D, 4 * NG)), res((4 * NG, D)), res((1, 4 * NG)), res((4 * NG, 1))] + x_specs,
        out_specs=[pl.BlockSpec((QK, tm), lambda t: (0, t)),
                   pl.BlockSpec((tm, QK), lambda t: (t, 0)),
                   pl.BlockSpec((QK, tm), lambda t: (0, t)),
                   pl.BlockSpec((V, tm), lambda t: (0, t)),
                   pl.BlockSpec((tm, V), lambda t: (t, 0)),
                   pl.BlockSpec((N_HP, tm, NG), lambda t: (0, t, 0)),
                   pl.BlockSpec((N_HP, NG, tm), lambda t: (0, 0, t))],
        out_shape=[jax.ShapeDtypeStruct((QK, NTOK), BF),
                   jax.ShapeDtypeStruct((NTOK, QK), BF),
                   jax.ShapeDtypeStruct((QK, NTOK), BF),
                   jax.ShapeDtypeStruct((V, NTOK), BF),
                   jax.ShapeDtypeStruct((NTOK, V), F32),
                   jax.ShapeDtypeStruct((N_HP, NTOK, NG), F32),
                   jax.ShapeDtypeStruct((N_HP, NG, NTOK), F32)],
        compiler_params=_cparams(1),
        name="mlstm_proj",
    )(mod, wqt, wk, wkt, wvt, wo, wg, wgt, bg, bgt, *x_args)


DHA = DHV + 8


def _scan_chunk(k_c, kt_c, qt_c, vta, rr_col, rr_row, b_row, m, cta, d, need_state):
    L = k_c.shape[0]
    si = lax.broadcasted_iota(jnp.int32, (L, L), 0)
    ti = lax.broadcasted_iota(jnp.int32, (L, L), 1)
    mask = (si <= ti) if d == 0 else (si >= ti)
    rrm = jnp.where(mask, rr_col, -jnp.inf)
    g = jnp.maximum(jnp.max(rrm, axis=0, keepdims=True), m)
    st = _dot(k_c, qt_c) * jnp.exp(rrm - g)
    num = _dot(vta[:DHV], st.astype(BF))
    den = jnp.sum(st, axis=0, keepdims=True)
    if cta is not None:
        a = jnp.exp(m - g)
        qc = _dot(cta.astype(BF), qt_c)
        num = num + a * qc[:DHV]
        den = den + a * qc[DHV:DHV + 1]
    mt = b_row + g
    ht = num * (1.0 / jnp.maximum(jnp.abs(den), jnp.exp(-mt)))
    if not need_state:
        return ht, None, None
    last = L - 1 if d == 0 else 0
    m_new = mt[:, last:last + 1]
    b_last = b_row[:, last:last + 1]
    w = jnp.exp(b_last + rr_row - m_new)
    kw = (kt_c.astype(F32) * w).astype(BF)
    if cta is None:
        return ht, (_dot_nt(kw, vta[:DHV]), _dot_nt(vta[DHV:], kw)), m_new
    c_new = jnp.exp(b_last + m - m_new) * cta + _dot_nt(vta, kw)
    return ht, c_new, m_new


def _scan_kernel(*refs, is_ctx):
    if is_ctx:
        qt_ref, k_ref, kt_ref, vt_ref, o_ref, g_ref, gt_ref, ng_ref, u_ref, cst_ref, nmst_ref = refs
    else:
        qt_ref, k_ref, kt_ref, vt_ref, o_ref, g_ref, gt_ref, ng_ref, c0_ref, m0_ref, u_ref = refs
    L = CHUNK
    nch = DEC_SEQ // L
    ri = lax.broadcasted_iota(jnp.int32, (L, L), 0)
    ci = lax.broadcasted_iota(jnp.int32, (L, L), 1)
    lower = jnp.where(ci <= ri, 1.0, 0.0).astype(BF)
    upper = jnp.where(ci >= ri, 1.0, 0.0).astype(BF)
    sub = lax.broadcasted_iota(jnp.int32, (DHA - DHV, L), 0)
    ones_rows = jnp.where(sub == 0, 1.0, 0.0).astype(BF)

    rr_cols, rr_rows, b_rows = [], [], []
    tparts = _split3(jnp.concatenate([gt_ref[:, c * L:(c + 1) * L] for c in range(nch)], axis=0))
    b_row_all = (sum(_dot(p, upper) for p in tparts), sum(_dot(p, lower) for p in tparts))
    for c in range(nch):
        gc = g_ref[c * L:(c + 1) * L, :]
        gtc = gt_ref[:, c * L:(c + 1) * L]
        gparts = _split3(gc)
        b_col = (sum(_dot(lower, p) for p in gparts), sum(_dot(upper, p) for p in gparts))
        b_row = tuple(b[c * NG:(c + 1) * NG, :] for b in b_row_all)
        rr_cols.append([gc[:, 0:NG - 2] - b[:, 2:NG] for b in b_col])
        rr_rows.append([gtc[0:NG - 2, :] - b[2:NG, :] for b in b_row])
        b_rows.append(b_row)

    states, us = [], []
    for hh in range(2):
        hsum = [None] * nch
        for d in range(2):
            ig_i, lf_i = d * 4 + hh, d * 4 + 2 + hh
            if is_ctx:
                m, cta = jnp.zeros((1, 1), F32), None
            else:
                m = m0_ref[d, hh][:, 0:1]
                cta = c0_ref[d, hh]
            order = range(nch) if d == 0 else range(nch - 1, -1, -1)
            for n, c in enumerate(order):
                sl = slice(c * L, (c + 1) * L)
                hs = slice(hh * DQK, (hh + 1) * DQK)
                vta = jnp.concatenate([vt_ref[hh * DHV:(hh + 1) * DHV, sl], ones_rows], axis=0)
                need_state = is_ctx or n < nch - 1
                hcur, c_new, m_new = _scan_chunk(
                    k_ref[sl, hs], kt_ref[hs, sl], qt_ref[hs, sl], vta,
                    rr_cols[c][d][:, ig_i:ig_i + 1], rr_rows[c][d][ig_i:ig_i + 1, :],
                    b_rows[c][d][lf_i:lf_i + 1, :], m, cta, d, need_state)
                hsum[c] = hcur if d == 0 else hsum[c] + hcur
                if is_ctx:
                    states.append(((c, d, hh), c_new, m_new))
                else:
                    m, cta = m_new, c_new

        for c in range(nch):
            sl = slice(c * L, (c + 1) * L)
            mu = jnp.mean(hsum[c], axis=0, keepdims=True)
            hc = hsum[c] - mu
            var = jnp.mean(hc * hc, axis=0, keepdims=True)
            hn = hc * lax.rsqrt(var + LN_EPS) * ng_ref[hh * DHV:(hh + 1) * DHV, :]
            og = jax.nn.sigmoid(o_ref[sl, hh * DHV:(hh + 1) * DHV])
            us.append(((sl, slice(hh * DHV, (hh + 1) * DHV)), (hn.T * og).astype(BF)))

    for idx, u in us:
        u_ref[idx] = u
    for idx, (c_fin, n_fin), m_fin in states:
        cst_ref[idx] = c_fin
        nmst_ref[idx] = jnp.concatenate([n_fin, jnp.broadcast_to(m_fin, (8, 128 - DQK))], axis=1)


def _ml_scan(qt, k, kt, vt, o, g, gt, norm_g, c0aug=None, m0b=None):
    is_ctx = c0aug is None
    tt = DEC_SEQ
    toff = 0 if is_ctx else NCTX // tt
    nt = (NCTX if is_ctx else NLAT) // tt
    in_specs = [pl.BlockSpec((2 * DQK, tt), lambda t, p: (p, toff + t)),
                pl.BlockSpec((tt, 2 * DQK), lambda t, p: (toff + t, p)),
                pl.BlockSpec((2 * DQK, tt), lambda t, p: (p, toff + t)),
                pl.BlockSpec((2 * DHV, tt), lambda t, p: (p, toff + t)),
                pl.BlockSpec((tt, 2 * DHV), lambda t, p: (toff + t, p)),
                pl.BlockSpec((None, tt, NG), lambda t, p: (p, toff + t, 0)),
                pl.BlockSpec((None, NG, tt), lambda t, p: (p, 0, toff + t)),
                pl.BlockSpec((2 * DHV, 1), lambda t, p: (p, 0))]
    args = [qt, k, kt, vt, o, g, gt, norm_g.reshape(V, 1)]
    u_spec = pl.BlockSpec((tt, 2 * DHV), lambda t, p: (t, p))
    u_shape = jax.ShapeDtypeStruct((nt * tt, V), BF)
    if is_ctx:
        nseq = tt // SEQ
        out_specs = [u_spec,
                     pl.BlockSpec((nseq, 2, 2, DQK, DHV), lambda t, p: (t, 0, p, 0, 0)),
                     pl.BlockSpec((nseq, 2, 2, 8, 128), lambda t, p: (t, 0, p, 0, 0))]
        out_shape = [u_shape,
                     jax.ShapeDtypeStruct((BATCH, 2, H, DQK, DHV), F32),
                     jax.ShapeDtypeStruct((BATCH, 2, H, 8, 128), F32)]
    else:
        in_specs += [pl.BlockSpec((None, 2, 2, DHA, DQK), lambda t, p: (t, 0, p, 0, 0)),
                     pl.BlockSpec((None, 2, 2, 1, 128), lambda t, p: (t, 0, p, 0, 0))]
        args += [c0aug, m0b]
        out_specs = [u_spec]
        out_shape = [u_shape]
    return pl.pallas_call(
        functools.partial(_scan_kernel, is_ctx=is_ctx),
        grid=(nt, N_HP),
        in_specs=in_specs,
        out_specs=out_specs,
        out_shape=out_shape,
        compiler_params=_cparams(2),
        name="mlstm_scan_ctx" if is_ctx else "mlstm_scan_lat",
    )(*args)


SC_TM = 512


SC_HALO = 8


def _sconv_kernel(x_ref, prev_ref, next_ref, mod_ref, w_in_ref, cw_ref, w_out_ref, g_ref, b_ref, o_ref, e_ref):
    tm = SC_TM
    row0 = pl.program_id(0) * tm
    slen = _seq_len(row0)
    pos = (row0 + lax.broadcasted_iota(jnp.int32, (tm, 1), 0)) & (slen - 1)
    m = mod_ref[...]
    x = x_ref[...]
    xe = jnp.concatenate([prev_ref[...], x, next_ref[...]], axis=0)
    he = (xe * (1.0 + m[1:2]) + m[0:1]).astype(BF)
    cu = _dot(he, w_in_ref[:, D:2 * D]) * _dot(he, w_in_ref[:, 2 * D:3 * D])
    bg = _dot(he[SC_HALO:SC_HALO + tm], w_in_ref[:, 0:D])
    e_ref[...] = cu
    prev = jnp.where(pos == 0, 0.0, e_ref[SC_HALO - 1:SC_HALO - 1 + tm, :])
    nxt = jnp.where(pos == slen - 1, 0.0, e_ref[SC_HALO + 1:SC_HALO + 1 + tm, :])
    conv = cw_ref[0:1, :] * prev + cw_ref[1:2, :] * e_ref[SC_HALO:SC_HALO + tm, :] + cw_ref[2:3, :] * nxt
    y = _dot((bg * conv).astype(BF), w_out_ref[...])
    o_ref[...] = _ln(ALPHA * x + m[2:3] * y, g_ref[...], b_ref[...])


def _sconv(x, mod, layer, w_in, conv_w, w_out, ln_g, ln_b):
    tm = SC_TM
    hb = tm // SC_HALO
    nhb = NTOK // SC_HALO
    return pl.pallas_call(
        _sconv_kernel,
        grid=(NTOK // tm,),
        in_specs=[pl.BlockSpec((tm, D), lambda t: (t, 0)),
                  pl.BlockSpec((SC_HALO, D), lambda t: (jnp.maximum(t * hb - 1, 0), 0)),
                  pl.BlockSpec((SC_HALO, D), lambda t: (jnp.minimum((t + 1) * hb, nhb - 1), 0)),
                  _mod_spec(layer, tm),
                  pl.BlockSpec((D, 3 * D), lambda t: (0, 0), pipeline_mode=pl.Buffered(1)),
                  _full((3, D)),
                  pl.BlockSpec((D, D), lambda t: (0, 0), pipeline_mode=pl.Buffered(1)),
                  _full((1, D)), _full((1, D))],
        out_specs=pl.BlockSpec((tm, D), lambda t: (t, 0)),
        out_shape=jax.ShapeDtypeStruct((NTOK, D), F32),
        scratch_shapes=[pltpu.VMEM((tm + 2 * SC_HALO, D), F32)],
        compiler_params=_cparams(1),
        name="sconv",
    )(x, x, x, mod, w_in, conv_w, w_out, ln_g.reshape(1, D), ln_b.reshape(1, D))


PL_SEG = 256
PL_NSEG = 2
PL_TM = PL_SEG * PL_NSEG
PL_HALO = 16
PL_STRIDE = PL_SEG + 2 * PL_HALO


def _pool_kernel(x_ref, prev_ref, next_ref, mod_ref, w_ref, pb_ref, ps_ref, g_ref, b_ref, o_ref,
                 e_ref, s2_ref, s4_ref, s8_ref):
    tm, seg, hl, gw = PL_TM, PL_SEG, PL_HALO, GROUP_W
    row0 = pl.program_id(0) * tm
    slen = _seq_len(row0)
    m = mod_ref[...]
    x = x_ref[...]
    h = x * (1.0 + m[1:2]) + m[0:1]
    hp = prev_ref[...] * (1.0 + m[1:2]) + m[0:1]
    hn = next_ref[...] * (1.0 + m[1:2]) + m[0:1]
    n0 = PL_NSEG * PL_STRIDE
    for i in range(PL_NSEG):
        base = i * PL_STRIDE
        pos0 = (row0 + i * seg) & (slen - 1)
        before = hp if i == 0 else h[i * seg - hl:i * seg]
        after = hn if i == PL_NSEG - 1 else h[(i + 1) * seg:(i + 1) * seg + hl]
        e_ref[base:base + hl, :] = jnp.where(pos0 != 0, before, 0.0)
        e_ref[base + hl:base + hl + seg, :] = h[i * seg:(i + 1) * seg]
        e_ref[base + hl + seg:base + PL_STRIDE, :] = jnp.where(pos0 + seg != slen, after, 0.0)
    e_ref[n0:, :] = jnp.zeros((8, D), F32)
    n2, n4, n8 = n0, n0 - 8, n0 - 16
    s2_ref[...] = e_ref[0:n2, :] + e_ref[1:n2 + 1, :]
    s4_ref[...] = s2_ref[0:n4, gw:] + s2_ref[2:n4 + 2, gw:]
    s8_ref[...] = s4_ref[0:n8, gw:] + s4_ref[4:n8 + 4, gw:]

    def rows(ref, off, lanes):
        return jnp.concatenate([ref[i * PL_STRIDE + off:i * PL_STRIDE + off + seg, lanes]
                                for i in range(PL_NSEG)], axis=0)

    tots = [rows(s2_ref, hl - 1, slice(0, gw)),
            rows(s4_ref, hl - 2, slice(0, gw)),
            rows(s8_ref, hl - 4, slice(0, gw)),
            rows(s8_ref, hl - 8, slice(gw, 2 * gw)) + rows(s8_ref, hl, slice(gw, 2 * gw))]
    pos = (row0 + lax.broadcasted_iota(jnp.int32, (tm, 1), 0)) & (slen - 1)
    ys = []
    for gi, win in enumerate(POOL_WINDOWS):
        cs = slice(gi * gw, (gi + 1) * gw)
        back = win // 2
        cnt = jnp.minimum(pos + (win - back), slen) - jnp.maximum(pos - back, 0)
        p = tots[gi] / cnt.astype(F32) - h[:, cs]
        ys.append(_dot(p.astype(BF), w_ref[gi]))
    y = (jnp.concatenate(ys, axis=1) + pb_ref[...]) * ps_ref[...]
    o_ref[...] = _ln(ALPHA * x + m[2:3] * y, g_ref[...], b_ref[...])


def _pool(x, mod, layer, w, pb, ps, ln_g, ln_b):
    assert POOL_WINDOWS == (2, 4, 8, 16) and PL_HALO >= POOL_WINDOWS[-1] // 2
    assert SEQ % PL_SEG == 0 and DEC_SEQ % PL_SEG == 0
    tm = PL_TM
    hb = tm // PL_HALO
    nhb = NTOK // PL_HALO
    n0 = PL_NSEG * PL_STRIDE
    return pl.pallas_call(
        _pool_kernel,
        grid=(NTOK // tm,),
        in_specs=[pl.BlockSpec((tm, D), lambda t: (t, 0)),
                  pl.BlockSpec((PL_HALO, D), lambda t: (jnp.maximum(t * hb - 1, 0), 0)),
                  pl.BlockSpec((PL_HALO, D), lambda t: (jnp.minimum((t + 1) * hb, nhb - 1), 0)),
                  _mod_spec(layer, tm),
                  _full((N_GROUPS, GROUP_W, GROUP_W)),
                  _full((1, D)), _full((1, D)), _full((1, D)), _full((1, D))],
        out_specs=pl.BlockSpec((tm, D), lambda t: (t, 0)),
        out_shape=jax.ShapeDtypeStruct((NTOK, D), F32),
        scratch_shapes=[pltpu.VMEM((n0 + 8, D), F32),
                        pltpu.VMEM((n0, D), F32),
                        pltpu.VMEM((n0 - 8, D - GROUP_W), F32),
                        pltpu.VMEM((n0 - 16, D - 2 * GROUP_W), F32)],
        compiler_params=_cparams(1),
        name="pool",
    )(x, x, x, mod, w, pb.reshape(1, D), ps.reshape(1, D), ln_g.reshape(1, D), ln_b.reshape(1, D))


FT_TK = 256


def _dft_mats(n, scale):
    k = np.arange(n, dtype=np.int64)
    ang = 2.0 * np.pi * ((k[:, None] * k[None, :]) % n).astype(np.float64) / n
    return np.stack([np.cos(ang) * scale, np.sin(ang) * scale]).astype(np.float32)


def _dot_x3(a_parts, b_parts):
    (ah, al), (bh, bl) = a_parts, b_parts
    return _dot(ah, bh) + (_dot(ah, bl) + _dot(al, bh))


FT_PAD = 8


def _fourier_kernel(x_ref, mod_ref, cs_ref, cc_ref, o_ref, *, s, tk):
    m = mod_ref[...]
    hparts = _split2(x_ref[...] * (1.0 + m[1:2]) + m[0:1])
    ccp = _split2(cc_ref[0])
    scp = _split2(cc_ref[1])
    ri = lax.broadcasted_iota(jnp.int32, (tk, tk), 0)
    ci = lax.broadcasted_iota(jnp.int32, (tk, tk), 1)
    anti = jnp.where(ri + ci == tk - 1, 1.0, 0.0).astype(BF)
    blocks = []
    for j in range(s // (2 * tk)):
        ac = _dot_x3(_split2(cs_ref[0, j]), hparts)
        as_ = _dot_x3(_split2(cs_ref[1, j]), hparts)
        direct, mirror = [], []
        for gi in range(N_GROUPS):
            cs = slice(gi * GROUP_W, (gi + 1) * GROUP_W)
            p = _dot_x3(_split2(ac[:, cs]), ccp)
            q = _dot_x3(_split2(as_[:, cs]), scp)
            direct.append((p - q)[0:tk])
            mirror.append((p + q)[1:tk + 1])
        blocks.append((j * tk, jnp.concatenate(direct, axis=1).astype(BF)))
        mir = jnp.concatenate(mirror, axis=1).astype(BF)
        blocks.append((s - (j + 1) * tk, _dot(anti, mir).astype(BF)))
    for r0, blk in blocks:
        o_ref[r0:r0 + tk, :] = blk


def _half_dft_mats(s, tk):
    full = _dft_mats(s, s ** -0.5)
    rows = (np.arange(s // (2 * tk))[:, None] * tk + np.arange(tk + FT_PAD)[None, :]) % s
    return full[:, rows, :]


def _fourier(x, mod, layer, s, nseq, row_off, mod_row_fn):
    tk = min(FT_TK, s // 2)
    seq_off = row_off // s
    cs = jnp.asarray(_half_dft_mats(s, tk))
    cc = jnp.asarray(_dft_mats(GROUP_W, GROUP_W ** -0.5))
    return pl.pallas_call(
        functools.partial(_fourier_kernel, s=s, tk=tk),
        grid=(nseq,),
        in_specs=[pl.BlockSpec((s, D), lambda b: (seq_off + b, 0)),
                  pl.BlockSpec((None, None, 6, D), lambda b: (layer, mod_row_fn(b), 0, 0)),
                  pl.BlockSpec(cs.shape, lambda b: (0, 0, 0, 0), pipeline_mode=pl.Buffered(1)),
                  _full((2, GROUP_W, GROUP_W))],
        out_specs=pl.BlockSpec((s, D), lambda b: (b, 0)),
        out_shape=jax.ShapeDtypeStruct((nseq * s, D), BF),
        compiler_params=_cparams(1),
        name="fourier_%d" % s,
    )(x, mod, cs, cc)


def _pos_embed():
    rows = DEC_SEQ // GRID_W
    quarter = D // 4
    omega = 1.0 / (10000.0 ** (np.arange(quarter, dtype=np.float64) / quarter))
    rr, cc = np.meshgrid(np.arange(rows, dtype=np.float64), np.arange(GRID_W, dtype=np.float64), indexing="ij")
    er = rr.reshape(-1, 1) * omega
    ec = cc.reshape(-1, 1) * omega
    return np.concatenate([np.sin(er), np.cos(er), np.sin(ec), np.cos(ec)], axis=-1).astype(np.float32)


def _gate_perm():
    perm = np.zeros(4 * H, dtype=np.int32)
    for hp in range(N_HP):
        for d in range(2):
            for j in range(2):
                for hh in range(2):
                    perm[hp * NG + d * 4 + j * 2 + hh] = d * 2 * H + j * H + 2 * hp + hh
    return perm


def kernel(x_prompt, x_sample, state_C, state_n, state_m, c, c_ctx, w_mod, b_mod, ln_g, ln_b, mlp_w1, mlp_w2,
           ml_w_in, ml_w_gate, ml_b_gate, ml_norm_g, ml_w_out, sc_w_in, sc_conv_w, sc_w_out,
           pl_w, pl_b, pl_scale, ft_w_out, ft_b_out):
    cond = jnp.concatenate([c_ctx[None, :], c, jnp.zeros((NCOND - 1 - DEC_BATCH, D), F32)], axis=0)
    mod = _modulation(cond, w_mod, b_mod)

    x = (x_prompt.reshape(NCTX, D), x_sample.reshape(NLAT, D), jnp.asarray(_pos_embed()))

    new_c = new_n = new_m = None
    for i in range(DEPTH):
        kind, j = i % 4, i // 4
        if kind == 0:
            w_in = ml_w_in[j].astype(BF)
            perm = _gate_perm()
            wg = ml_w_gate[j][:, perm].astype(BF)
            bg = ml_b_gate[j][perm]
            wk = w_in[:, QK:2 * QK]
            qt, k, kt, vt, o, g, gt = _ml_proj(
                x, mod, i, w_in[:, :QK].T, wk, wk.T, w_in[:, 2 * QK:2 * QK + V].T, w_in[:, 2 * QK + V:],
                wg, wg.T, bg.reshape(1, -1), bg.reshape(-1, 1))
            scan_args = (qt, k, kt, vt, o, g, gt, ml_norm_g[j])
            uc, cst, nmst = _ml_scan(*scan_args)
            c0 = jnp.concatenate([jnp.swapaxes(state_C[:, j], -1, -2), state_n[:, j][..., None, :],
                                  jnp.zeros((DEC_BATCH, 2, H, DHA - DHV - 1, DQK), F32)], axis=-2)
            m0 = jnp.broadcast_to(state_m[:, j][..., None, None], (DEC_BATCH, 2, H, 1, 128))
            (ul,) = _ml_scan(*scan_args, c0, m0)
            x = _out_proj(uc, ul, ml_w_out[j].astype(BF), jnp.zeros((D,), F32), x, mod, i, ln_g[i, 0], ln_b[i, 0])
            new_c = cst[:, None]
            new_n = nmst[..., 0, :DQK][:, None]
            new_m = nmst[..., 0, DQK][:, None]
        elif kind == 1:
            x = _sconv(x, mod, i, sc_w_in[j].astype(BF), sc_conv_w[j], sc_w_out[j].astype(BF),
                       ln_g[i, 0], ln_b[i, 0])
        elif kind == 2:
            x = _pool(x, mod, i, pl_w[j].astype(BF), pl_b[j], pl_scale[j], ln_g[i, 0], ln_b[i, 0])
        else:
            fc = _fourier(x, mod, i, SEQ, BATCH, 0, lambda b: 0)
            fl = _fourier(x, mod, i, DEC_SEQ, DEC_BATCH, NCTX, lambda b: 1 + b)
            x = _out_proj(fc, fl, ft_w_out[j].astype(BF), ft_b_out[j], x, mod, i, ln_g[i, 0], ln_b[i, 0])
        x = _mlp(x, mod, i, mlp_w1, mlp_w2, ln_g[i, 1], ln_b[i, 1],
                 split_out=(i == DEPTH - 1))

    y_prompt = x[0].reshape(BATCH, SEQ, D)
    y_sample = x[1].reshape(DEC_BATCH, DEC_SEQ, D)
    return y_prompt, y_sample, new_c, new_n, new_m
```

```python
import functools
import math

import numpy as np
import jax
import jax.numpy as jnp
from jax import lax
from jax.experimental import pallas as pl
from jax.experimental.pallas import tpu as pltpu

D = 1024
BATCH, SEQ = 16, 256
DEC_BATCH, DEC_SEQ = 4, 1024
DEPTH = 4
GRID_W = 64
H = 8
DQK = 64
DHV = 128
QK = H * DQK
V = H * DHV
N_GROUPS = 4
GROUP_W = D // N_GROUPS
POOL_WINDOWS = (2, 4, 8, 16)
D_FF = 4 * D
ALPHA = (2.0 * DEPTH) ** 0.25
LN_EPS = 1e-5
F32 = jnp.float32
BF = jnp.bfloat16

NCTX = BATCH * SEQ
NLAT = DEC_BATCH * DEC_SEQ
NTOK = NCTX + NLAT
NCOND = 8
CHUNK = 256
VMEM_LIMIT = 56 * 1024 * 1024


def _cparams(n_axes):
    return pltpu.CompilerParams(dimension_semantics=("arbitrary",) * n_axes,
                                vmem_limit_bytes=VMEM_LIMIT)


def _mod_row(row0):
    return jnp.where(row0 < NCTX, 0, 1 + (row0 - NCTX) // DEC_SEQ)


def _seq_len(row0):
    return jnp.where(row0 < NCTX, SEQ, DEC_SEQ)


def _ln(z, g, b):
    mu = jnp.mean(z, axis=-1, keepdims=True)
    zc = z - mu
    var = jnp.mean(zc * zc, axis=-1, keepdims=True)
    return zc * lax.rsqrt(var + LN_EPS) * g + b


def _dot(a, b):
    return jnp.dot(a, b, preferred_element_type=F32)


def _dot_nt(a, b):
    return lax.dot_general(a, b, (((1,), (1,)), ((), ())), preferred_element_type=F32)


def _split2(x):
    hi = x.astype(BF)
    lo = (x - hi.astype(F32)).astype(BF)
    return hi, lo


def _split3(x):
    hi = x.astype(BF)
    r = x - hi.astype(F32)
    mid = r.astype(BF)
    lo = (r - mid.astype(F32)).astype(BF)
    return hi, mid, lo


def _full(shape):
    n = len(shape)
    return pl.BlockSpec(shape, lambda *_: (0,) * n)


def _x_specs(x, tm):
    if not isinstance(x, tuple):
        return [pl.BlockSpec((tm, D), lambda t: (t, 0))], [x]
    nc = NCTX // tm
    return ([pl.BlockSpec((tm, D), lambda t: (jnp.minimum(t, nc - 1), 0)),
             pl.BlockSpec((tm, D), lambda t: (jnp.maximum(t - nc, 0), 0)),
             pl.BlockSpec((tm, D), lambda t: (t % (DEC_SEQ // tm), 0))], list(x))


def _read_x(x_refs, tm):
    if len(x_refs) == 1:
        return x_refs[0][...]
    xc_ref, xl_ref, pe_ref = x_refs
    return jnp.where(pl.program_id(0) * tm < NCTX, xc_ref[...], xl_ref[...] + pe_ref[...])


def _mod_spec(layer, tm):
    return pl.BlockSpec((None, None, 6, D), lambda t: (layer, _mod_row(t * tm), 0, 0))


def _mod_kernel(c_ref, w_ref, b_ref, o_ref):
    c = c_ref[...]
    s = c * jax.nn.sigmoid(c)
    o_ref[...] = _dot(s.astype(BF), w_ref[...].astype(BF)) + b_ref[...]


def _modulation(cond, w_mod, b_mod):
    tn = 1536
    out = pl.pallas_call(
        _mod_kernel,
        grid=(DEPTH, 6 * D // tn),
        in_specs=[pl.BlockSpec((NCOND, D), lambda i, j: (0, 0)),
                  pl.BlockSpec((None, D, tn), lambda i, j: (i, 0, j)),
                  pl.BlockSpec((None, 1, tn), lambda i, j: (i, 0, j))],
        out_specs=pl.BlockSpec((None, NCOND, tn), lambda i, j: (i, 0, j)),
        out_shape=jax.ShapeDtypeStruct((DEPTH, NCOND, 6 * D), F32),
        compiler_params=_cparams(2),
        name="modulation",
    )(cond, w_mod, b_mod.reshape(DEPTH, 1, 6 * D))
    return out.reshape(DEPTH, NCOND, 6, D)


MLP_TM = 512
MLP_FC = 512
MLP_NF = D_FF // MLP_FC
MLP_NT = NTOK // MLP_TM


def _mlp_tile(s):
    return jnp.maximum(s - (MLP_NF - 1), 0)


def _mlp_kernel(mod_ref, w1_ref, w2_ref, g_ref, b_ref, x1_ref, x_ref, *refs):
    o_refs, (w1s_ref, w2s_ref, acc_ref, h_ref) = refs[:-4], refs[-4:]
    s = pl.program_id(0)
    m = mod_ref[...]

    def pre(x):
        return (x * (1.0 + m[4:5]) + m[3:4]).astype(BF)

    def hidden(h, w1c):
        return jnp.square(jnp.maximum(_dot(h, w1c), 0.0)).astype(BF)

    def result(x, acc):
        return _ln(ALPHA * x + m[5:6] * acc, g_ref[...], b_ref[...])

    def accumulate(i, first, part):
        @pl.when(first)
        def _():
            acc_ref[i] = part

        @pl.when(jnp.logical_not(first))
        def _():
            acc_ref[i] += part

    @pl.when(s < MLP_NF)
    def _():
        @pl.when(s == 0)
        def _():
            h_ref[0] = pre(x_ref[...])

        w1c = w1_ref[...].astype(BF)
        w2c = w2_ref[...].astype(BF)
        part = _dot(hidden(h_ref[0], w1c), w2c)
        w1s_ref[s] = w1c
        w2s_ref[s] = w2c
        accumulate(0, s == 0, part)

        @pl.when(s == MLP_NF - 1)
        def _():
            o_refs[0][...] = result(x_ref[...], acc_ref[0])

    @pl.when(jnp.logical_and(s >= 1, s <= MLP_NF))
    def _():
        @pl.when(s == 1)
        def _():
            h_ref[1] = pre(x1_ref[...])

        accumulate(1, s == 1, _dot(hidden(h_ref[1], w1s_ref[s - 1]), w2s_ref[s - 1]))

        @pl.when(s == MLP_NF)
        def _():
            o_refs[0][...] = result(x1_ref[...], acc_ref[1])

    @pl.when(s > MLP_NF)
    def _():
        x = x_ref[...]
        h = pre(x)
        acc = jnp.zeros((MLP_TM, D), F32)
        for f in range(MLP_NF):
            acc = acc + _dot(hidden(h, w1s_ref[f]), w2s_ref[f])
        res = result(x, acc)
        if len(o_refs) == 1:
            o_refs[0][...] = res
        else:
            is_ctx = _mlp_tile(s) * MLP_TM < NCTX

            @pl.when(is_ctx)
            def _():
                o_refs[0][...] = res

            @pl.when(jnp.logical_not(is_ctx))
            def _():
                o_refs[1][...] = res


def _mlp(x, mod, layer, w1, w2, ln_g, ln_b, split_out=False):
    tm = MLP_TM
    nc = NCTX // tm
    assert nc >= 2 and MLP_NT >= 2
    if split_out:
        out_specs = [pl.BlockSpec((tm, D), lambda s: (jnp.minimum(_mlp_tile(s), nc - 1), 0)),
                     pl.BlockSpec((tm, D), lambda s: (jnp.maximum(_mlp_tile(s) - nc, 0), 0))]
        out_shape = [jax.ShapeDtypeStruct((NCTX, D), F32), jax.ShapeDtypeStruct((NLAT, D), F32)]
    else:
        out_specs = pl.BlockSpec((tm, D), lambda s: (_mlp_tile(s), 0))
        out_shape = jax.ShapeDtypeStruct((NTOK, D), F32)
    return pl.pallas_call(
        _mlp_kernel,
        grid=(MLP_NF - 1 + MLP_NT,),
        in_specs=[pl.BlockSpec((None, None, 6, D), lambda s: (layer, _mod_row(_mlp_tile(s) * tm), 0, 0)),
                  pl.BlockSpec((None, D, MLP_FC), lambda s: (layer, 0, jnp.minimum(s, MLP_NF - 1))),
                  pl.BlockSpec((None, MLP_FC, D), lambda s: (layer, jnp.minimum(s, MLP_NF - 1), 0)),
                  _full((1, D)), _full((1, D)),
                  pl.BlockSpec((tm, D), lambda s: (1, 0)),
                  pl.BlockSpec((tm, D), lambda s: (_mlp_tile(s), 0))],
        out_specs=out_specs,
        out_shape=out_shape,
        scratch_shapes=[pltpu.VMEM((MLP_NF, D, MLP_FC), BF),
                        pltpu.VMEM((MLP_NF, MLP_FC, D), BF),
                        pltpu.VMEM((2, tm, D), F32),
                        pltpu.VMEM((2, tm, D), BF)],
        compiler_params=_cparams(1),
        name="mlp",
    )(mod, w1, w2, ln_g.reshape(1, D), ln_b.reshape(1, D), x, x)


OUT_TM = 1024
OUT_SUB = 256


def _out_kernel(uc_ref, ul_ref, w_ref, bias_ref, mod_ref, g_ref, b_ref, *refs):
    x_refs, o_ref = refs[:-1], refs[-1]
    is_ctx = pl.program_id(0) * OUT_TM < NCTX
    m = mod_ref[...]
    outs = []
    for r in range(0, OUT_TM, OUT_SUB):
        rs = slice(r, r + OUT_SUB)
        u = jnp.where(is_ctx, uc_ref[rs, :], ul_ref[rs, :])
        y = _dot(u, w_ref[...]) + bias_ref[...]
        if len(x_refs) == 1:
            x = x_refs[0][rs, :]
        else:
            x = jnp.where(is_ctx, x_refs[0][rs, :], x_refs[1][rs, :] + x_refs[2][rs, :])
        outs.append(_ln(ALPHA * x + m[2:3] * y, g_ref[...], b_ref[...]))
    for r, out in zip(range(0, OUT_TM, OUT_SUB), outs):
        o_ref[r:r + OUT_SUB, :] = out


def _out_proj(uc, ul, w, bias, x, mod, layer, ln_g, ln_b):
    tm = OUT_TM
    nc = NCTX // tm
    k = w.shape[0]
    x_specs, x_args = _x_specs(x, tm)
    return pl.pallas_call(
        _out_kernel,
        grid=(NTOK // tm,),
        in_specs=[pl.BlockSpec((tm, k), lambda t: (jnp.minimum(t, nc - 1), 0)),
                  pl.BlockSpec((tm, k), lambda t: (jnp.maximum(t - nc, 0), 0)),
                  pl.BlockSpec((k, D), lambda t: (0, 0), pipeline_mode=pl.Buffered(1)),
                  _full((1, D)),
                  _mod_spec(layer, tm),
                  _full((1, D)), _full((1, D))] + x_specs,
        out_specs=pl.BlockSpec((tm, D), lambda t: (t, 0)),
        out_shape=jax.ShapeDtypeStruct((NTOK, D), F32),
        compiler_params=_cparams(1),
        name="out_proj",
    )(uc, ul, w, bias.reshape(1, D), mod, ln_g.reshape(1, D), ln_b.reshape(1, D), *x_args)


ML_TM = 512
N_HP = H // 2
NG = 8


def _log_sigmoid(g):
    return jnp.minimum(g, 0.0) - jnp.log1p(jnp.exp(-jnp.abs(g)))


def _ml_proj_kernel(mod_ref, wt_ref, wk_ref, wo_ref, wg_ref, bg_ref, bgt_ref, *refs):
    x_refs = refs[:-7]
    qt_ref, k_ref, kt_ref, vt_ref, o_ref, g_ref, gt_ref = refs[-7:]
    m = mod_ref[...]
    h = (_read_x(x_refs, ML_TM) * (1.0 + m[1:2]) + m[0:1]).astype(BF)
    tr = _dot_nt(wt_ref[...], h)
    r0, r1, r2 = QK, QK + 4 * NG, 2 * QK + 4 * NG
    qt = tr[:r0].astype(BF)
    kt = (tr[r1:r2] * (DQK ** -0.5)).astype(BF)
    vt = tr[r2:].astype(BF)
    k = (_dot(h, wk_ref[...]) * (DQK ** -0.5)).astype(BF)
    o = _dot(h, wo_ref[...])
    g = _dot(h, wg_ref[...]) + bg_ref[...]
    col = lax.broadcasted_iota(jnp.int32, g.shape, 1)
    g = jnp.where((col & 2) != 0, _log_sigmoid(g), g)
    gt = tr[r0:r1] + bgt_ref[...]
    row = lax.broadcasted_iota(jnp.int32, gt.shape, 0)
    gt = jnp.where((row & 2) != 0, _log_sigmoid(gt), gt)
    qt_ref[...] = qt
    k_ref[...] = k
    kt_ref[...] = kt
    vt_ref[...] = vt
    o_ref[...] = o
    for hp in range(N_HP):
        g_ref[hp] = g[:, hp * NG:(hp + 1) * NG]
        gt_ref[hp] = gt[hp * NG:(hp + 1) * NG, :]


def _ml_proj(x, mod, layer, wt, wk, wo, wg, bg, bgt):
    tm = ML_TM
    res = lambda shape: pl.BlockSpec(shape, lambda t: (0,) * len(shape), pipeline_mode=pl.Buffered(1))
    x_specs, x_args = _x_specs(x, tm)
    return pl.pallas_call(
        _ml_proj_kernel,
        grid=(NTOK // tm,),
        in_specs=[_mod_spec(layer, tm),
                  res((2 * QK + 4 * NG + V, D)), res((D, QK)), res((D, V)),
                  res((D, 4 * NG)), res((1, 4 * NG)), res((4 * NG, 1))] + x_specs,
        out_specs=[pl.BlockSpec((QK, tm), lambda t: (0, t)),
                   pl.BlockSpec((tm, QK), lambda t: (t, 0)),
                   pl.BlockSpec((QK, tm), lambda t: (0, t)),
                   pl.BlockSpec((V, tm), lambda t: (0, t)),
                   pl.BlockSpec((tm, V), lambda t: (t, 0)),
                   pl.BlockSpec((N_HP, tm, NG), lambda t: (0, t, 0)),
                   pl.BlockSpec((N_HP, NG, tm), lambda t: (0, 0, t))],
        out_shape=[jax.ShapeDtypeStruct((QK, NTOK), BF),
                   jax.ShapeDtypeStruct((NTOK, QK), BF),
                   jax.ShapeDtypeStruct((QK, NTOK), BF),
                   jax.ShapeDtypeStruct((V, NTOK), BF),
                   jax.ShapeDtypeStruct((NTOK, V), F32),
                   jax.ShapeDtypeStruct((N_HP, NTOK, NG), F32),
                   jax.ShapeDtypeStruct((N_HP, NG, NTOK), F32)],
        compiler_params=_cparams(1),
        name="mlstm_proj",
    )(mod, wt, wk, wo, wg, bg, bgt, *x_args)


DHA = DHV + 8


def _scan_chunk(k_c, kt_c, qt_c, vta, rr_col, rr_row, b_row, m, cta, d, need_state):
    L = k_c.shape[0]
    si = lax.broadcasted_iota(jnp.int32, (L, L), 0)
    ti = lax.broadcasted_iota(jnp.int32, (L, L), 1)
    mask = (si <= ti) if d == 0 else (si >= ti)
    rrm = jnp.where(mask, rr_col, -jnp.inf)
    g = jnp.maximum(jnp.max(rrm, axis=0, keepdims=True), m)
    st = _dot(k_c, qt_c) * jnp.exp(rrm - g)
    num = _dot(vta[:DHV], st.astype(BF))
    den = jnp.sum(st, axis=0, keepdims=True)
    if cta is not None:
        a = jnp.exp(m - g)
        qc = _dot(cta.astype(BF), qt_c)
        num = num + a * qc[:DHV]
        den = den + a * qc[DHV:DHV + 1]
    mt = b_row + g
    ht = num * (1.0 / jnp.maximum(jnp.abs(den), jnp.exp(-mt)))
    if not need_state:
        return ht, None, None
    last = L - 1 if d == 0 else 0
    m_new = mt[:, last:last + 1]
    b_last = b_row[:, last:last + 1]
    w = jnp.exp(b_last + rr_row - m_new)
    kw = (kt_c.astype(F32) * w).astype(BF)
    if cta is None:
        return ht, (_dot_nt(kw, vta[:DHV]), _dot_nt(vta[DHV:], kw)), m_new
    c_new = jnp.exp(b_last + m - m_new) * cta + _dot_nt(vta, kw)
    return ht, c_new, m_new


def _scan_kernel(*refs, is_ctx):
    if is_ctx:
        qt_ref, k_ref, kt_ref, vt_ref, o_ref, g_ref, gt_ref, ng_ref, u_ref, cst_ref, nmst_ref = refs
    else:
        qt_ref, k_ref, kt_ref, vt_ref, o_ref, g_ref, gt_ref, ng_ref, c0_ref, m0_ref, u_ref = refs
    L = CHUNK
    nch = DEC_SEQ // L
    ri = lax.broadcasted_iota(jnp.int32, (L, L), 0)
    ci = lax.broadcasted_iota(jnp.int32, (L, L), 1)
    lower = jnp.where(ci <= ri, 1.0, 0.0).astype(BF)
    upper = jnp.where(ci >= ri, 1.0, 0.0).astype(BF)
    sub = lax.broadcasted_iota(jnp.int32, (DHA - DHV, L), 0)
    ones_rows = jnp.where(sub == 0, 1.0, 0.0).astype(BF)

    rr_cols, rr_rows, b_rows = [], [], []
    tparts = _split3(jnp.concatenate([gt_ref[:, c * L:(c + 1) * L] for c in range(nch)], axis=0))
    b_row_all = (sum(_dot(p, upper) for p in tparts), sum(_dot(p, lower) for p in tparts))
    for c in range(nch):
        gc = g_ref[c * L:(c + 1) * L, :]
        gtc = gt_ref[:, c * L:(c + 1) * L]
        gparts = _split3(gc)
        b_col = (sum(_dot(lower, p) for p in gparts), sum(_dot(upper, p) for p in gparts))
        b_row = tuple(b[c * NG:(c + 1) * NG, :] for b in b_row_all)
        rr_cols.append([gc[:, 0:NG - 2] - b[:, 2:NG] for b in b_col])
        rr_rows.append([gtc[0:NG - 2, :] - b[2:NG, :] for b in b_row])
        b_rows.append(b_row)

    states, us = [], []
    for hh in range(2):
        hsum = [None] * nch
        for d in range(2):
            ig_i, lf_i = d * 4 + hh, d * 4 + 2 + hh
            if is_ctx:
                m, cta = jnp.zeros((1, 1), F32), None
            else:
                m = m0_ref[d, hh][:, 0:1]
                cta = c0_ref[d, hh]
            order = range(nch) if d == 0 else range(nch - 1, -1, -1)
            for n, c in enumerate(order):
                sl = slice(c * L, (c + 1) * L)
                hs = slice(hh * DQK, (hh + 1) * DQK)
                vta = jnp.concatenate([vt_ref[hh * DHV:(hh + 1) * DHV, sl], ones_rows], axis=0)
                need_state = is_ctx or n < nch - 1
                hcur, c_new, m_new = _scan_chunk(
                    k_ref[sl, hs], kt_ref[hs, sl], qt_ref[hs, sl], vta,
                    rr_cols[c][d][:, ig_i:ig_i + 1], rr_rows[c][d][ig_i:ig_i + 1, :],
                    b_rows[c][d][lf_i:lf_i + 1, :], m, cta, d, need_state)
                hsum[c] = hcur if d == 0 else hsum[c] + hcur
                if is_ctx:
                    states.append(((c, d, hh), c_new, m_new))
                else:
                    m, cta = m_new, c_new

        for c in range(nch):
            sl = slice(c * L, (c + 1) * L)
            mu = jnp.mean(hsum[c], axis=0, keepdims=True)
            hc = hsum[c] - mu
            var = jnp.mean(hc * hc, axis=0, keepdims=True)
            hn = hc * lax.rsqrt(var + LN_EPS) * ng_ref[hh * DHV:(hh + 1) * DHV, :]
            og = jax.nn.sigmoid(o_ref[sl, hh * DHV:(hh + 1) * DHV])
            us.append(((sl, slice(hh * DHV, (hh + 1) * DHV)), (hn.T * og).astype(BF)))

    for idx, u in us:
        u_ref[idx] = u
    for idx, (c_fin, n_fin), m_fin in states:
        cst_ref[idx] = c_fin
        nmst_ref[idx] = jnp.concatenate([n_fin, jnp.broadcast_to(m_fin, (8, 128 - DQK))], axis=1)


def _ml_scan(qt, k, kt, vt, o, g, gt, norm_g, c0aug=None, m0b=None):
    is_ctx = c0aug is None
    tt = DEC_SEQ
    toff = 0 if is_ctx else NCTX // tt
    nt = (NCTX if is_ctx else NLAT) // tt
    in_specs = [pl.BlockSpec((2 * DQK, tt), lambda t, p: (p, toff + t)),
                pl.BlockSpec((tt, 2 * DQK), lambda t, p: (toff + t, p)),
                pl.BlockSpec((2 * DQK, tt), lambda t, p: (p, toff + t)),
                pl.BlockSpec((2 * DHV, tt), lambda t, p: (p, toff + t)),
                pl.BlockSpec((tt, 2 * DHV), lambda t, p: (toff + t, p)),
                pl.BlockSpec((None, tt, NG), lambda t, p: (p, toff + t, 0)),
                pl.BlockSpec((None, NG, tt), lambda t, p: (p, 0, toff + t)),
                pl.BlockSpec((2 * DHV, 1), lambda t, p: (p, 0))]
    args = [qt, k, kt, vt, o, g, gt, norm_g.reshape(V, 1)]
    u_spec = pl.BlockSpec((tt, 2 * DHV), lambda t, p: (t, p))
    u_shape = jax.ShapeDtypeStruct((nt * tt, V), BF)
    if is_ctx:
        nseq = tt // SEQ
        out_specs = [u_spec,
                     pl.BlockSpec((nseq, 2, 2, DQK, DHV), lambda t, p: (t, 0, p, 0, 0)),
                     pl.BlockSpec((nseq, 2, 2, 8, 128), lambda t, p: (t, 0, p, 0, 0))]
        out_shape = [u_shape,
                     jax.ShapeDtypeStruct((BATCH, 2, H, DQK, DHV), F32),
                     jax.ShapeDtypeStruct((BATCH, 2, H, 8, 128), F32)]
    else:
        in_specs += [pl.BlockSpec((None, 2, 2, DHA, DQK), lambda t, p: (t, 0, p, 0, 0)),
                     pl.BlockSpec((None, 2, 2, 1, 128), lambda t, p: (t, 0, p, 0, 0))]
        args += [c0aug, m0b]
        out_specs = [u_spec]
        out_shape = [u_shape]
    return pl.pallas_call(
        functools.partial(_scan_kernel, is_ctx=is_ctx),
        grid=(nt, N_HP),
        in_specs=in_specs,
        out_specs=out_specs,
        out_shape=out_shape,
        compiler_params=_cparams(2),
        name="mlstm_scan_ctx" if is_ctx else "mlstm_scan_lat",
    )(*args)


SC_TM = 512


SC_HALO = 8


def _sconv_kernel(x_ref, prev_ref, next_ref, mod_ref, w_in_ref, cw_ref, w_out_ref, g_ref, b_ref, o_ref, e_ref):
    tm = SC_TM
    row0 = pl.program_id(0) * tm
    slen = _seq_len(row0)
    pos = (row0 + lax.broadcasted_iota(jnp.int32, (tm, 1), 0)) & (slen - 1)
    m = mod_ref[...]
    x = x_ref[...]
    xe = jnp.concatenate([prev_ref[...], x, next_ref[...]], axis=0)
    he = (xe * (1.0 + m[1:2]) + m[0:1]).astype(BF)
    cu = _dot(he, w_in_ref[:, D:2 * D]) * _dot(he, w_in_ref[:, 2 * D:3 * D])
    bg = _dot(he[SC_HALO:SC_HALO + tm], w_in_ref[:, 0:D])
    e_ref[...] = cu
    prev = jnp.where(pos == 0, 0.0, e_ref[SC_HALO - 1:SC_HALO - 1 + tm, :])
    nxt = jnp.where(pos == slen - 1, 0.0, e_ref[SC_HALO + 1:SC_HALO + 1 + tm, :])
    conv = cw_ref[0:1, :] * prev + cw_ref[1:2, :] * e_ref[SC_HALO:SC_HALO + tm, :] + cw_ref[2:3, :] * nxt
    y = _dot((bg * conv).astype(BF), w_out_ref[...])
    o_ref[...] = _ln(ALPHA * x + m[2:3] * y, g_ref[...], b_ref[...])


def _sconv(x, mod, layer, w_in, conv_w, w_out, ln_g, ln_b):
    tm = SC_TM
    hb = tm // SC_HALO
    nhb = NTOK // SC_HALO
    return pl.pallas_call(
        _sconv_kernel,
        grid=(NTOK // tm,),
        in_specs=[pl.BlockSpec((tm, D), lambda t: (t, 0)),
                  pl.BlockSpec((SC_HALO, D), lambda t: (jnp.maximum(t * hb - 1, 0), 0)),
                  pl.BlockSpec((SC_HALO, D), lambda t: (jnp.minimum((t + 1) * hb, nhb - 1), 0)),
                  _mod_spec(layer, tm),
                  pl.BlockSpec((D, 3 * D), lambda t: (0, 0), pipeline_mode=pl.Buffered(1)),
                  _full((3, D)),
                  pl.BlockSpec((D, D), lambda t: (0, 0), pipeline_mode=pl.Buffered(1)),
                  _full((1, D)), _full((1, D))],
        out_specs=pl.BlockSpec((tm, D), lambda t: (t, 0)),
        out_shape=jax.ShapeDtypeStruct((NTOK, D), F32),
        scratch_shapes=[pltpu.VMEM((tm + 2 * SC_HALO, D), F32)],
        compiler_params=_cparams(1),
        name="sconv",
    )(x, x, x, mod, w_in, conv_w, w_out, ln_g.reshape(1, D), ln_b.reshape(1, D))


PL_SEG = 256
PL_NSEG = 2
PL_TM = PL_SEG * PL_NSEG
PL_HALO = 16
PL_STRIDE = PL_SEG + 2 * PL_HALO


def _pool_kernel(x_ref, prev_ref, next_ref, mod_ref, w_ref, pb_ref, ps_ref, g_ref, b_ref, o_ref,
                 e_ref, s2_ref, s4_ref, s8_ref):
    tm, seg, hl, gw = PL_TM, PL_SEG, PL_HALO, GROUP_W
    row0 = pl.program_id(0) * tm
    slen = _seq_len(row0)
    m = mod_ref[...]
    x = x_ref[...]
    h = x * (1.0 + m[1:2]) + m[0:1]
    hp = prev_ref[...] * (1.0 + m[1:2]) + m[0:1]
    hn = next_ref[...] * (1.0 + m[1:2]) + m[0:1]
    n0 = PL_NSEG * PL_STRIDE
    for i in range(PL_NSEG):
        base = i * PL_STRIDE
        pos0 = (row0 + i * seg) & (slen - 1)
        before = hp if i == 0 else h[i * seg - hl:i * seg]
        after = hn if i == PL_NSEG - 1 else h[(i + 1) * seg:(i + 1) * seg + hl]
        e_ref[base:base + hl, :] = jnp.where(pos0 != 0, before, 0.0)
        e_ref[base + hl:base + hl + seg, :] = h[i * seg:(i + 1) * seg]
        e_ref[base + hl + seg:base + PL_STRIDE, :] = jnp.where(pos0 + seg != slen, after, 0.0)
    e_ref[n0:, :] = jnp.zeros((8, D), F32)
    n2, n4, n8 = n0, n0 - 8, n0 - 16
    s2_ref[...] = e_ref[0:n2, :] + e_ref[1:n2 + 1, :]
    s4_ref[...] = s2_ref[0:n4, gw:] + s2_ref[2:n4 + 2, gw:]
    s8_ref[...] = s4_ref[0:n8, gw:] + s4_ref[4:n8 + 4, gw:]

    def rows(ref, off, lanes):
        return jnp.concatenate([ref[i * PL_STRIDE + off:i * PL_STRIDE + off + seg, lanes]
                                for i in range(PL_NSEG)], axis=0)

    tots = [rows(s2_ref, hl - 1, slice(0, gw)),
            rows(s4_ref, hl - 2, slice(0, gw)),
            rows(s8_ref, hl - 4, slice(0, gw)),
            rows(s8_ref, hl - 8, slice(gw, 2 * gw)) + rows(s8_ref, hl, slice(gw, 2 * gw))]
    pos = (row0 + lax.broadcasted_iota(jnp.int32, (tm, 1), 0)) & (slen - 1)
    ys = []
    for gi, win in enumerate(POOL_WINDOWS):
        cs = slice(gi * gw, (gi + 1) * gw)
        back = win // 2
        cnt = jnp.minimum(pos + (win - back), slen) - jnp.maximum(pos - back, 0)
        p = tots[gi] / cnt.astype(F32) - h[:, cs]
        ys.append(_dot(p.astype(BF), w_ref[gi]))
    y = (jnp.concatenate(ys, axis=1) + pb_ref[...]) * ps_ref[...]
    o_ref[...] = _ln(ALPHA * x + m[2:3] * y, g_ref[...], b_ref[...])


def _pool(x, mod, layer, w, pb, ps, ln_g, ln_b):
    assert POOL_WINDOWS == (2, 4, 8, 16) and PL_HALO >= POOL_WINDOWS[-1] // 2
    assert SEQ % PL_SEG == 0 and DEC_SEQ % PL_SEG == 0
    tm = PL_TM
    hb = tm // PL_HALO
    nhb = NTOK // PL_HALO
    n0 = PL_NSEG * PL_STRIDE
    return pl.pallas_call(
        _pool_kernel,
        grid=(NTOK // tm,),
        in_specs=[pl.BlockSpec((tm, D), lambda t: (t, 0)),
                  pl.BlockSpec((PL_HALO, D), lambda t: (jnp.maximum(t * hb - 1, 0), 0)),
                  pl.BlockSpec((PL_HALO, D), lambda t: (jnp.minimum((t + 1) * hb, nhb - 1), 0)),
                  _mod_spec(layer, tm),
                  _full((N_GROUPS, GROUP_W, GROUP_W)),
                  _full((1, D)), _full((1, D)), _full((1, D)), _full((1, D))],
        out_specs=pl.BlockSpec((tm, D), lambda t: (t, 0)),
        out_shape=jax.ShapeDtypeStruct((NTOK, D), F32),
        scratch_shapes=[pltpu.VMEM((n0 + 8, D), F32),
                        pltpu.VMEM((n0, D), F32),
                        pltpu.VMEM((n0 - 8, D - GROUP_W), F32),
                        pltpu.VMEM((n0 - 16, D - 2 * GROUP_W), F32)],
        compiler_params=_cparams(1),
        name="pool",
    )(x, x, x, mod, w, pb.reshape(1, D), ps.reshape(1, D), ln_g.reshape(1, D), ln_b.reshape(1, D))


FT_TK = 256


def _dft_mats(n, scale):
    k = np.arange(n, dtype=np.int64)
    ang = 2.0 * np.pi * ((k[:, None] * k[None, :]) % n).astype(np.float64) / n
    return np.stack([np.cos(ang) * scale, np.sin(ang) * scale]).astype(np.float32)


def _dot_x3(a_parts, b_parts):
    (ah, al), (bh, bl) = a_parts, b_parts
    return _dot(ah, bh) + (_dot(ah, bl) + _dot(al, bh))


FT_PAD = 8


def _fourier_kernel(x_ref, mod_ref, cs_ref, cc_ref, o_ref, *, s, tk):
    m = mod_ref[...]
    hparts = _split2(x_ref[...] * (1.0 + m[1:2]) + m[0:1])
    ccp = _split2(cc_ref[0])
    scp = _split2(cc_ref[1])
    ri = lax.broadcasted_iota(jnp.int32, (tk, tk), 0)
    ci = lax.broadcasted_iota(jnp.int32, (tk, tk), 1)
    anti = jnp.where(ri + ci == tk - 1, 1.0, 0.0).astype(BF)
    blocks = []
    for j in range(s // (2 * tk)):
        ac = _dot_x3(_split2(cs_ref[0, j]), hparts)
        as_ = _dot_x3(_split2(cs_ref[1, j]), hparts)
        direct, mirror = [], []
        for gi in range(N_GROUPS):
            cs = slice(gi * GROUP_W, (gi + 1) * GROUP_W)
            p = _dot_x3(_split2(ac[:, cs]), ccp)
            q = _dot_x3(_split2(as_[:, cs]), scp)
            direct.append((p - q)[0:tk])
            mirror.append((p + q)[1:tk + 1])
        blocks.append((j * tk, jnp.concatenate(direct, axis=1).astype(BF)))
        mir = jnp.concatenate(mirror, axis=1).astype(BF)
        blocks.append((s - (j + 1) * tk, _dot(anti, mir).astype(BF)))
    for r0, blk in blocks:
        o_ref[r0:r0 + tk, :] = blk


def _half_dft_mats(s, tk):
    full = _dft_mats(s, s ** -0.5)
    rows = (np.arange(s // (2 * tk))[:, None] * tk + np.arange(tk + FT_PAD)[None, :]) % s
    return full[:, rows, :]


def _fourier(x, mod, layer, s, nseq, row_off, mod_row_fn):
    tk = min(FT_TK, s // 2)
    seq_off = row_off // s
    cs = jnp.asarray(_half_dft_mats(s, tk))
    cc = jnp.asarray(_dft_mats(GROUP_W, GROUP_W ** -0.5))
    return pl.pallas_call(
        functools.partial(_fourier_kernel, s=s, tk=tk),
        grid=(nseq,),
        in_specs=[pl.BlockSpec((s, D), lambda b: (seq_off + b, 0)),
                  pl.BlockSpec((None, None, 6, D), lambda b: (layer, mod_row_fn(b), 0, 0)),
                  pl.BlockSpec(cs.shape, lambda b: (0, 0, 0, 0), pipeline_mode=pl.Buffered(1)),
                  _full((2, GROUP_W, GROUP_W))],
        out_specs=pl.BlockSpec((s, D), lambda b: (b, 0)),
        out_shape=jax.ShapeDtypeStruct((nseq * s, D), BF),
        compiler_params=_cparams(1),
        name="fourier_%d" % s,
    )(x, mod, cs, cc)


def _pos_embed():
    rows = DEC_SEQ // GRID_W
    quarter = D // 4
    omega = 1.0 / (10000.0 ** (np.arange(quarter, dtype=np.float64) / quarter))
    rr, cc = np.meshgrid(np.arange(rows, dtype=np.float64), np.arange(GRID_W, dtype=np.float64), indexing="ij")
    er = rr.reshape(-1, 1) * omega
    ec = cc.reshape(-1, 1) * omega
    return np.concatenate([np.sin(er), np.cos(er), np.sin(ec), np.cos(ec)], axis=-1).astype(np.float32)


def _gate_perm():
    perm = np.zeros(4 * H, dtype=np.int32)
    for hp in range(N_HP):
        for d in range(2):
            for j in range(2):
                for hh in range(2):
                    perm[hp * NG + d * 4 + j * 2 + hh] = d * 2 * H + j * H + 2 * hp + hh
    return perm


def kernel(x_prompt, x_sample, state_C, state_n, state_m, c, c_ctx, w_mod, b_mod, ln_g, ln_b, mlp_w1, mlp_w2,
           ml_w_in, ml_w_gate, ml_b_gate, ml_norm_g, ml_w_out, sc_w_in, sc_conv_w, sc_w_out,
           pl_w, pl_b, pl_scale, ft_w_out, ft_b_out):
    cond = jnp.concatenate([c_ctx[None, :], c, jnp.zeros((NCOND - 1 - DEC_BATCH, D), F32)], axis=0)
    mod = _modulation(cond, w_mod, b_mod)

    x = (x_prompt.reshape(NCTX, D), x_sample.reshape(NLAT, D), jnp.asarray(_pos_embed()))

    new_c = new_n = new_m = None
    for i in range(DEPTH):
        kind, j = i % 4, i // 4
        if kind == 0:
            w_in = ml_w_in[j].astype(BF)
            perm = _gate_perm()
            wg = ml_w_gate[j][:, perm].astype(BF)
            bg = ml_b_gate[j][perm]
            wk = w_in[:, QK:2 * QK]
            qt, k, kt, vt, o, g, gt = _ml_proj(
                x, mod, i, jnp.concatenate([w_in[:, :QK], wg, w_in[:, QK:2 * QK + V]], axis=1).T, wk,
                w_in[:, 2 * QK + V:], wg, bg.reshape(1, -1), bg.reshape(-1, 1))
            scan_args = (qt, k, kt, vt, o, g, gt, ml_norm_g[j])
            uc, cst, nmst = _ml_scan(*scan_args)
            c0 = jnp.concatenate([jnp.swapaxes(state_C[:, j], -1, -2), state_n[:, j][..., None, :],
                                  jnp.zeros((DEC_BATCH, 2, H, DHA - DHV - 1, DQK), F32)], axis=-2)
            m0 = jnp.broadcast_to(state_m[:, j][..., None, None], (DEC_BATCH, 2, H, 1, 128))
            (ul,) = _ml_scan(*scan_args, c0, m0)
            x = _out_proj(uc, ul, ml_w_out[j].astype(BF), jnp.zeros((D,), F32), x, mod, i, ln_g[i, 0], ln_b[i, 0])
            new_c = cst[:, None]
            new_n = nmst[..., 0, :DQK][:, None]
            new_m = nmst[..., 0, DQK][:, None]
        elif kind == 1:
            x = _sconv(x, mod, i, sc_w_in[j].astype(BF), sc_conv_w[j], sc_w_out[j].astype(BF),
                       ln_g[i, 0], ln_b[i, 0])
        elif kind == 2:
            x = _pool(x, mod, i, pl_w[j].astype(BF), pl_b[j], pl_scale[j], ln_g[i, 0], ln_b[i, 0])
        else:
            fc = _fourier(x, mod, i, SEQ, BATCH, 0, lambda b: 0)
            fl = _fourier(x, mod, i, DEC_SEQ, DEC_BATCH, NCTX, lambda b: 1 + b)
            x = _out_proj(fc, fl, ft_w_out[j].astype(BF), ft_b_out[j], x, mod, i, ln_g[i, 0], ln_b[i, 0])
        x = _mlp(x, mod, i, mlp_w1, mlp_w2, ln_g[i, 1], ln_b[i, 1],
                 split_out=(i == DEPTH - 1))

    y_prompt = x[0].reshape(BATCH, SEQ, D)
    y_sample = x[1].reshape(DEC_BATCH, DEC_SEQ, D)
    return y_prompt, y_sample, new_c, new_n, new_m
```

```python
import functools
import math

import numpy as np
import jax
import jax.numpy as jnp
from jax import lax
from jax.experimental import pallas as pl
from jax.experimental.pallas import tpu as pltpu

D = 1024
BATCH, SEQ = 16, 256
DEC_BATCH, DEC_SEQ = 4, 1024
DEPTH = 4
GRID_W = 64
H = 8
DQK = 64
DHV = 128
QK = H * DQK
V = H * DHV
N_GROUPS = 4
GROUP_W = D // N_GROUPS
POOL_WINDOWS = (2, 4, 8, 16)
D_FF = 4 * D
ALPHA = (2.0 * DEPTH) ** 0.25
LN_EPS = 1e-5
F32 = jnp.float32
BF = jnp.bfloat16

NCTX = BATCH * SEQ
NLAT = DEC_BATCH * DEC_SEQ
NTOK = NCTX + NLAT
NCOND = 8
CHUNK = 256
VMEM_LIMIT = 56 * 1024 * 1024


def _cparams(n_axes):
    return pltpu.CompilerParams(dimension_semantics=("arbitrary",) * n_axes,
                                vmem_limit_bytes=VMEM_LIMIT)


def _mod_row(row0):
    return jnp.where(row0 < NCTX, 0, 1 + (row0 - NCTX) // DEC_SEQ)


def _seq_len(row0):
    return jnp.where(row0 < NCTX, SEQ, DEC_SEQ)


def _ln(z, g, b):
    mu = jnp.mean(z, axis=-1, keepdims=True)
    zc = z - mu
    var = jnp.mean(zc * zc, axis=-1, keepdims=True)
    return zc * lax.rsqrt(var + LN_EPS) * g + b


def _dot(a, b):
    return jnp.dot(a, b, preferred_element_type=F32)


def _dot_nt(a, b):
    return lax.dot_general(a, b, (((1,), (1,)), ((), ())), preferred_element_type=F32)


def _split2(x):
    hi = x.astype(BF)
    lo = (x - hi.astype(F32)).astype(BF)
    return hi, lo


def _split3(x):
    hi = x.astype(BF)
    r = x - hi.astype(F32)
    mid = r.astype(BF)
    lo = (r - mid.astype(F32)).astype(BF)
    return hi, mid, lo


def _full(shape):
    n = len(shape)
    return pl.BlockSpec(shape, lambda *_: (0,) * n)


def _x_specs(x, tm):
    if not isinstance(x, tuple):
        return [pl.BlockSpec((tm, D), lambda t: (t, 0))], [x]
    nc = NCTX // tm
    return ([pl.BlockSpec((tm, D), lambda t: (jnp.minimum(t, nc - 1), 0)),
             pl.BlockSpec((tm, D), lambda t: (jnp.maximum(t - nc, 0), 0)),
             pl.BlockSpec((tm, D), lambda t: (t % (DEC_SEQ // tm), 0))], list(x))


def _read_x(x_refs, tm):
    if len(x_refs) == 1:
        return x_refs[0][...]
    xc_ref, xl_ref, pe_ref = x_refs
    return jnp.where(pl.program_id(0) * tm < NCTX, xc_ref[...], xl_ref[...] + pe_ref[...])


def _mod_spec(layer, tm):
    return pl.BlockSpec((None, None, 6, D), lambda t: (layer, _mod_row(t * tm), 0, 0))


def _mod_kernel(c_ref, w_ref, b_ref, o_ref):
    c = c_ref[...]
    s = c * jax.nn.sigmoid(c)
    o_ref[...] = _dot(s.astype(BF), w_ref[...].astype(BF)) + b_ref[...]


def _modulation(cond, w_mod, b_mod):
    tn = 1536
    out = pl.pallas_call(
        _mod_kernel,
        grid=(DEPTH, 6 * D // tn),
        in_specs=[pl.BlockSpec((NCOND, D), lambda i, j: (0, 0)),
                  pl.BlockSpec((None, D, tn), lambda i, j: (i, 0, j)),
                  pl.BlockSpec((None, 1, tn), lambda i, j: (i, 0, j))],
        out_specs=pl.BlockSpec((None, NCOND, tn), lambda i, j: (i, 0, j)),
        out_shape=jax.ShapeDtypeStruct((DEPTH, NCOND, 6 * D), F32),
        compiler_params=_cparams(2),
        name="modulation",
    )(cond, w_mod, b_mod.reshape(DEPTH, 1, 6 * D))
    return out.reshape(DEPTH, NCOND, 6, D)


MLP_TM = 512
MLP_FC = 512
MLP_NF = D_FF // MLP_FC
MLP_NT = NTOK // MLP_TM


def _mlp_tile(s):
    return jnp.maximum(s - (MLP_NF - 1), 0)


def _mlp_kernel(mod_ref, w1_ref, w2_ref, g_ref, b_ref, x1_ref, x_ref, *refs):
    o_refs, (w1s_ref, w2s_ref, acc_ref, h_ref) = refs[:-4], refs[-4:]
    s = pl.program_id(0)
    m = mod_ref[...]

    def pre(x):
        return (x * (1.0 + m[4:5]) + m[3:4]).astype(BF)

    def hidden(h, w1c):
        return jnp.square(jnp.maximum(_dot(h, w1c), 0.0)).astype(BF)

    def result(x, acc):
        return _ln(ALPHA * x + m[5:6] * acc, g_ref[...], b_ref[...])

    @pl.when(s < MLP_NF)
    def _():
        @pl.when(s == 0)
        def _():
            h_ref[0:MLP_TM, :] = pre(x_ref[...])
            h_ref[MLP_TM:, :] = pre(x1_ref[...])

        w1c = w1_ref[...].astype(BF)
        w2c = w2_ref[...].astype(BF)
        part = _dot(hidden(h_ref[...], w1c), w2c)
        w1s_ref[s] = w1c
        w2s_ref[s] = w2c

        @pl.when(s == 0)
        def _():
            acc_ref[...] = part

        @pl.when(s > 0)
        def _():
            acc_ref[...] += part

        @pl.when(s == MLP_NF - 1)
        def _():
            o_refs[0][...] = result(x_ref[...], acc_ref[0:MLP_TM, :])

    @pl.when(s == MLP_NF)
    def _():
        o_refs[0][...] = result(x1_ref[...], acc_ref[MLP_TM:, :])

    @pl.when(s > MLP_NF)
    def _():
        x = x_ref[...]
        h = pre(x)
        acc = jnp.zeros((MLP_TM, D), F32)
        for f in range(MLP_NF):
            acc = acc + _dot(hidden(h, w1s_ref[f]), w2s_ref[f])
        res = result(x, acc)
        if len(o_refs) == 1:
            o_refs[0][...] = res
        else:
            is_ctx = _mlp_tile(s) * MLP_TM < NCTX

            @pl.when(is_ctx)
            def _():
                o_refs[0][...] = res

            @pl.when(jnp.logical_not(is_ctx))
            def _():
                o_refs[1][...] = res


def _mlp(x, mod, layer, w1, w2, ln_g, ln_b, split_out=False):
    tm = MLP_TM
    nc = NCTX // tm
    assert nc >= 2 and MLP_NT >= 2
    if split_out:
        out_specs = [pl.BlockSpec((tm, D), lambda s: (jnp.minimum(_mlp_tile(s), nc - 1), 0)),
                     pl.BlockSpec((tm, D), lambda s: (jnp.maximum(_mlp_tile(s) - nc, 0), 0))]
        out_shape = [jax.ShapeDtypeStruct((NCTX, D), F32), jax.ShapeDtypeStruct((NLAT, D), F32)]
    else:
        out_specs = pl.BlockSpec((tm, D), lambda s: (_mlp_tile(s), 0))
        out_shape = jax.ShapeDtypeStruct((NTOK, D), F32)
    return pl.pallas_call(
        _mlp_kernel,
        grid=(MLP_NF - 1 + MLP_NT,),
        in_specs=[pl.BlockSpec((None, None, 6, D), lambda s: (layer, _mod_row(_mlp_tile(s) * tm), 0, 0)),
                  pl.BlockSpec((None, D, MLP_FC), lambda s: (layer, 0, jnp.minimum(s, MLP_NF - 1))),
                  pl.BlockSpec((None, MLP_FC, D), lambda s: (layer, jnp.minimum(s, MLP_NF - 1), 0)),
                  _full((1, D)), _full((1, D)),
                  pl.BlockSpec((tm, D), lambda s: (1, 0)),
                  pl.BlockSpec((tm, D), lambda s: (_mlp_tile(s), 0))],
        out_specs=out_specs,
        out_shape=out_shape,
        scratch_shapes=[pltpu.VMEM((MLP_NF, D, MLP_FC), BF),
                        pltpu.VMEM((MLP_NF, MLP_FC, D), BF),
                        pltpu.VMEM((2 * tm, D), F32),
                        pltpu.VMEM((2 * tm, D), BF)],
        compiler_params=_cparams(1),
        name="mlp",
    )(mod, w1, w2, ln_g.reshape(1, D), ln_b.reshape(1, D), x, x)


OUT_TM = 1024
OUT_SUB = 256


def _out_kernel(uc_ref, ul_ref, w_ref, bias_ref, mod_ref, g_ref, b_ref, *refs):
    x_refs, o_ref = refs[:-1], refs[-1]
    is_ctx = pl.program_id(0) * OUT_TM < NCTX
    m = mod_ref[...]
    outs = []
    for r in range(0, OUT_TM, OUT_SUB):
        rs = slice(r, r + OUT_SUB)
        u = jnp.where(is_ctx, uc_ref[rs, :], ul_ref[rs, :])
        y = _dot(u, w_ref[...]) + bias_ref[...]
        if len(x_refs) == 1:
            x = x_refs[0][rs, :]
        else:
            x = jnp.where(is_ctx, x_refs[0][rs, :], x_refs[1][rs, :] + x_refs[2][rs, :])
        outs.append(_ln(ALPHA * x + m[2:3] * y, g_ref[...], b_ref[...]))
    for r, out in zip(range(0, OUT_TM, OUT_SUB), outs):
        o_ref[r:r + OUT_SUB, :] = out


def _out_proj(uc, ul, w, bias, x, mod, layer, ln_g, ln_b):
    tm = OUT_TM
    nc = NCTX // tm
    k = w.shape[0]
    x_specs, x_args = _x_specs(x, tm)
    return pl.pallas_call(
        _out_kernel,
        grid=(NTOK // tm,),
        in_specs=[pl.BlockSpec((tm, k), lambda t: (jnp.minimum(t, nc - 1), 0)),
                  pl.BlockSpec((tm, k), lambda t: (jnp.maximum(t - nc, 0), 0)),
                  pl.BlockSpec((k, D), lambda t: (0, 0), pipeline_mode=pl.Buffered(1)),
                  _full((1, D)),
                  _mod_spec(layer, tm),
                  _full((1, D)), _full((1, D))] + x_specs,
        out_specs=pl.BlockSpec((tm, D), lambda t: (t, 0)),
        out_shape=jax.ShapeDtypeStruct((NTOK, D), F32),
        compiler_params=_cparams(1),
        name="out_proj",
    )(uc, ul, w, bias.reshape(1, D), mod, ln_g.reshape(1, D), ln_b.reshape(1, D), *x_args)


ML_TM = 512
N_HP = H // 2
SCAN_PAIRS = 2
NG = 8


def _log_sigmoid(g):
    return jnp.minimum(g, 0.0) - jnp.log1p(jnp.exp(-jnp.abs(g)))


def _ml_proj_kernel(mod_ref, wt_ref, wk_ref, wo_ref, wg_ref, bg_ref, bgt_ref, *refs):
    x_refs = refs[:-7]
    qt_ref, k_ref, kt_ref, vt_ref, o_ref, g_ref, gt_ref = refs[-7:]
    m = mod_ref[...]
    h = (_read_x(x_refs, ML_TM) * (1.0 + m[1:2]) + m[0:1]).astype(BF)
    tr = _dot_nt(wt_ref[...], h)
    r0, r1, r2 = QK, QK + 4 * NG, 2 * QK + 4 * NG
    qt = tr[:r0].astype(BF)
    kt = (tr[r1:r2] * (DQK ** -0.5)).astype(BF)
    vt = tr[r2:].astype(BF)
    k = (_dot(h, wk_ref[...]) * (DQK ** -0.5)).astype(BF)
    o = _dot(h, wo_ref[...])
    g = _dot(h, wg_ref[...]) + bg_ref[...]
    col = lax.broadcasted_iota(jnp.int32, g.shape, 1)
    g = jnp.where((col & 2) != 0, _log_sigmoid(g), g)
    gt = tr[r0:r1] + bgt_ref[...]
    row = lax.broadcasted_iota(jnp.int32, gt.shape, 0)
    gt = jnp.where((row & 2) != 0, _log_sigmoid(gt), gt)
    qt_ref[...] = qt
    k_ref[...] = k
    kt_ref[...] = kt
    vt_ref[...] = vt
    o_ref[...] = o
    for hp in range(N_HP):
        g_ref[hp] = g[:, hp * NG:(hp + 1) * NG]
        gt_ref[hp] = gt[hp * NG:(hp + 1) * NG, :]


def _ml_proj(x, mod, layer, wt, wk, wo, wg, bg, bgt):
    tm = ML_TM
    res = lambda shape: pl.BlockSpec(shape, lambda t: (0,) * len(shape), pipeline_mode=pl.Buffered(1))
    x_specs, x_args = _x_specs(x, tm)
    return pl.pallas_call(
        _ml_proj_kernel,
        grid=(NTOK // tm,),
        in_specs=[_mod_spec(layer, tm),
                  res((2 * QK + 4 * NG + V, D)), res((D, QK)), res((D, V)),
                  res((D, 4 * NG)), res((1, 4 * NG)), res((4 * NG, 1))] + x_specs,
        out_specs=[pl.BlockSpec((QK, tm), lambda t: (0, t)),
                   pl.BlockSpec((tm, QK), lambda t: (t, 0)),
                   pl.BlockSpec((QK, tm), lambda t: (0, t)),
                   pl.BlockSpec((V, tm), lambda t: (0, t)),
                   pl.BlockSpec((tm, V), lambda t: (t, 0)),
                   pl.BlockSpec((N_HP, tm, NG), lambda t: (0, t, 0)),
                   pl.BlockSpec((N_HP, NG, tm), lambda t: (0, 0, t))],
        out_shape=[jax.ShapeDtypeStruct((QK, NTOK), BF),
                   jax.ShapeDtypeStruct((NTOK, QK), BF),
                   jax.ShapeDtypeStruct((QK, NTOK), BF),
                   jax.ShapeDtypeStruct((V, NTOK), BF),
                   jax.ShapeDtypeStruct((NTOK, V), F32),
                   jax.ShapeDtypeStruct((N_HP, NTOK, NG), F32),
                   jax.ShapeDtypeStruct((N_HP, NG, NTOK), F32)],
        compiler_params=_cparams(1),
        name="mlstm_proj",
    )(mod, wt, wk, wo, wg, bg, bgt, *x_args)


DHA = DHV + 8


def _scan_chunk(k_c, kt_c, qt_c, vta, rr_col, rr_row, b_row, m, cta, d, need_state):
    L = k_c.shape[0]
    si = lax.broadcasted_iota(jnp.int32, (L, L), 0)
    ti = lax.broadcasted_iota(jnp.int32, (L, L), 1)
    mask = (si <= ti) if d == 0 else (si >= ti)
    rrm = jnp.where(mask, rr_col, -jnp.inf)
    g = jnp.maximum(jnp.max(rrm, axis=0, keepdims=True), m)
    st = _dot(k_c, qt_c) * jnp.exp(rrm - g)
    num = _dot(vta[:DHV], st.astype(BF))
    den = jnp.sum(st, axis=0, keepdims=True)
    if cta is not None:
        a = jnp.exp(m - g)
        qc = _dot(cta.astype(BF), qt_c)
        num = num + a * qc[:DHV]
        den = den + a * qc[DHV:DHV + 1]
    mt = b_row + g
    ht = num * (1.0 / jnp.maximum(jnp.abs(den), jnp.exp(-mt)))
    if not need_state:
        return ht, None, None
    last = L - 1 if d == 0 else 0
    m_new = mt[:, last:last + 1]
    b_last = b_row[:, last:last + 1]
    w = jnp.exp(b_last + rr_row - m_new)
    kw = (kt_c.astype(F32) * w).astype(BF)
    if cta is None:
        return ht, (_dot_nt(kw, vta[:DHV]), _dot_nt(vta[DHV:], kw)), m_new
    c_new = jnp.exp(b_last + m - m_new) * cta + _dot_nt(vta, kw)
    return ht, c_new, m_new


def _scan_kernel(*refs, is_ctx):
    if is_ctx:
        qt_ref, k_ref, kt_ref, vt_ref, o_ref, g_ref, gt_ref, ng_ref, u_ref, cst_ref, nmst_ref = refs
    else:
        qt_ref, k_ref, kt_ref, vt_ref, o_ref, g_ref, gt_ref, ng_ref, c0_ref, m0_ref, u_ref = refs
    L = CHUNK
    nch = DEC_SEQ // L
    ri = lax.broadcasted_iota(jnp.int32, (L, L), 0)
    ci = lax.broadcasted_iota(jnp.int32, (L, L), 1)
    lower = jnp.where(ci <= ri, 1.0, 0.0).astype(BF)
    upper = jnp.where(ci >= ri, 1.0, 0.0).astype(BF)
    sub = lax.broadcasted_iota(jnp.int32, (DHA - DHV, L), 0)
    ones_rows = jnp.where(sub == 0, 1.0, 0.0).astype(BF)

    rr_cols, rr_rows, b_rows = [], [], []
    for pp in range(SCAN_PAIRS):
        tparts = _split3(jnp.concatenate([gt_ref[pp, :, c * L:(c + 1) * L] for c in range(nch)], axis=0))
        b_row_all = (sum(_dot(p, upper) for p in tparts), sum(_dot(p, lower) for p in tparts))
        rr_cols.append([])
        rr_rows.append([])
        b_rows.append([])
        for c in range(nch):
            gc = g_ref[pp, c * L:(c + 1) * L, :]
            gtc = gt_ref[pp, :, c * L:(c + 1) * L]
            gparts = _split3(gc)
            b_col = (sum(_dot(lower, p) for p in gparts), sum(_dot(upper, p) for p in gparts))
            b_row = tuple(b[c * NG:(c + 1) * NG, :] for b in b_row_all)
            rr_cols[pp].append([gc[:, 0:NG - 2] - b[:, 2:NG] for b in b_col])
            rr_rows[pp].append([gtc[0:NG - 2, :] - b[2:NG, :] for b in b_row])
            b_rows[pp].append(b_row)

    states, us = [], []
    for hl in range(2 * SCAN_PAIRS):
        pp, hh = divmod(hl, 2)
        hsum = [None] * nch
        for d in range(2):
            ig_i, lf_i = d * 4 + hh, d * 4 + 2 + hh
            if is_ctx:
                m, cta = jnp.zeros((1, 1), F32), None
            else:
                m = m0_ref[d, hl][:, 0:1]
                cta = c0_ref[d, hl]
            order = range(nch) if d == 0 else range(nch - 1, -1, -1)
            for n, c in enumerate(order):
                sl = slice(c * L, (c + 1) * L)
                hs = slice(hl * DQK, (hl + 1) * DQK)
                vta = jnp.concatenate([vt_ref[hl * DHV:(hl + 1) * DHV, sl], ones_rows], axis=0)
                need_state = is_ctx or n < nch - 1
                hcur, c_new, m_new = _scan_chunk(
                    k_ref[sl, hs], kt_ref[hs, sl], qt_ref[hs, sl], vta,
                    rr_cols[pp][c][d][:, ig_i:ig_i + 1], rr_rows[pp][c][d][ig_i:ig_i + 1, :],
                    b_rows[pp][c][d][lf_i:lf_i + 1, :], m, cta, d, need_state)
                hsum[c] = hcur if d == 0 else hsum[c] + hcur
                if is_ctx:
                    states.append(((c, d, hl), c_new, m_new))
                else:
                    m, cta = m_new, c_new

        for c in range(nch):
            sl = slice(c * L, (c + 1) * L)
            mu = jnp.mean(hsum[c], axis=0, keepdims=True)
            hc = hsum[c] - mu
            var = jnp.mean(hc * hc, axis=0, keepdims=True)
            hn = hc * lax.rsqrt(var + LN_EPS) * ng_ref[hl * DHV:(hl + 1) * DHV, :]
            og = jax.nn.sigmoid(o_ref[sl, hl * DHV:(hl + 1) * DHV])
            us.append(((sl, slice(hl * DHV, (hl + 1) * DHV)), (hn.T * og).astype(BF)))

    for idx, u in us:
        u_ref[idx] = u
    for idx, (c_fin, n_fin), m_fin in states:
        cst_ref[idx] = c_fin
        nmst_ref[idx] = jnp.concatenate([n_fin, jnp.broadcast_to(m_fin, (8, 128 - DQK))], axis=1)


def _ml_scan(qt, k, kt, vt, o, g, gt, norm_g, c0aug=None, m0b=None):
    is_ctx = c0aug is None
    tt = DEC_SEQ
    toff = 0 if is_ctx else NCTX // tt
    nt = (NCTX if is_ctx else NLAT) // tt
    nh = 2 * SCAN_PAIRS
    in_specs = [pl.BlockSpec((nh * DQK, tt), lambda t, p: (p, toff + t)),
                pl.BlockSpec((tt, nh * DQK), lambda t, p: (toff + t, p)),
                pl.BlockSpec((nh * DQK, tt), lambda t, p: (p, toff + t)),
                pl.BlockSpec((nh * DHV, tt), lambda t, p: (p, toff + t)),
                pl.BlockSpec((tt, nh * DHV), lambda t, p: (toff + t, p)),
                pl.BlockSpec((SCAN_PAIRS, tt, NG), lambda t, p: (p, toff + t, 0)),
                pl.BlockSpec((SCAN_PAIRS, NG, tt), lambda t, p: (p, 0, toff + t)),
                pl.BlockSpec((nh * DHV, 1), lambda t, p: (p, 0))]
    args = [qt, k, kt, vt, o, g, gt, norm_g.reshape(V, 1)]
    u_spec = pl.BlockSpec((tt, nh * DHV), lambda t, p: (t, p))
    u_shape = jax.ShapeDtypeStruct((nt * tt, V), BF)
    if is_ctx:
        nseq = tt // SEQ
        out_specs = [u_spec,
                     pl.BlockSpec((nseq, 2, nh, DQK, DHV), lambda t, p: (t, 0, p, 0, 0)),
                     pl.BlockSpec((nseq, 2, nh, 8, 128), lambda t, p: (t, 0, p, 0, 0))]
        out_shape = [u_shape,
                     jax.ShapeDtypeStruct((BATCH, 2, H, DQK, DHV), F32),
                     jax.ShapeDtypeStruct((BATCH, 2, H, 8, 128), F32)]
    else:
        in_specs += [pl.BlockSpec((None, 2, nh, DHA, DQK), lambda t, p: (t, 0, p, 0, 0)),
                     pl.BlockSpec((None, 2, nh, 1, 128), lambda t, p: (t, 0, p, 0, 0))]
        args += [c0aug, m0b]
        out_specs = [u_spec]
        out_shape = [u_shape]
    return pl.pallas_call(
        functools.partial(_scan_kernel, is_ctx=is_ctx),
        grid=(nt, N_HP // SCAN_PAIRS),
        in_specs=in_specs,
        out_specs=out_specs,
        out_shape=out_shape,
        compiler_params=_cparams(2),
        name="mlstm_scan_ctx" if is_ctx else "mlstm_scan_lat",
    )(*args)


SC_TM = 512


SC_HALO = 8


def _sconv_kernel(x_ref, prev_ref, next_ref, mod_ref, w_in_ref, cw_ref, w_out_ref, g_ref, b_ref, o_ref, e_ref):
    tm = SC_TM
    row0 = pl.program_id(0) * tm
    slen = _seq_len(row0)
    pos = (row0 + lax.broadcasted_iota(jnp.int32, (tm, 1), 0)) & (slen - 1)
    m = mod_ref[...]
    x = x_ref[...]
    xe = jnp.concatenate([prev_ref[...], x, next_ref[...]], axis=0)
    he = (xe * (1.0 + m[1:2]) + m[0:1]).astype(BF)
    cu = _dot(he, w_in_ref[:, D:2 * D]) * _dot(he, w_in_ref[:, 2 * D:3 * D])
    bg = _dot(he[SC_HALO:SC_HALO + tm], w_in_ref[:, 0:D])
    e_ref[...] = cu
    prev = jnp.where(pos == 0, 0.0, e_ref[SC_HALO - 1:SC_HALO - 1 + tm, :])
    nxt = jnp.where(pos == slen - 1, 0.0, e_ref[SC_HALO + 1:SC_HALO + 1 + tm, :])
    conv = cw_ref[0:1, :] * prev + cw_ref[1:2, :] * e_ref[SC_HALO:SC_HALO + tm, :] + cw_ref[2:3, :] * nxt
    y = _dot((bg * conv).astype(BF), w_out_ref[...])
    o_ref[...] = _ln(ALPHA * x + m[2:3] * y, g_ref[...], b_ref[...])


def _sconv(x, mod, layer, w_in, conv_w, w_out, ln_g, ln_b):
    tm = SC_TM
    hb = tm // SC_HALO
    nhb = NTOK // SC_HALO
    return pl.pallas_call(
        _sconv_kernel,
        grid=(NTOK // tm,),
        in_specs=[pl.BlockSpec((tm, D), lambda t: (t, 0)),
                  pl.BlockSpec((SC_HALO, D), lambda t: (jnp.maximum(t * hb - 1, 0), 0)),
                  pl.BlockSpec((SC_HALO, D), lambda t: (jnp.minimum((t + 1) * hb, nhb - 1), 0)),
                  _mod_spec(layer, tm),
                  pl.BlockSpec((D, 3 * D), lambda t: (0, 0), pipeline_mode=pl.Buffered(1)),
                  _full((3, D)),
                  pl.BlockSpec((D, D), lambda t: (0, 0), pipeline_mode=pl.Buffered(1)),
                  _full((1, D)), _full((1, D))],
        out_specs=pl.BlockSpec((tm, D), lambda t: (t, 0)),
        out_shape=jax.ShapeDtypeStruct((NTOK, D), F32),
        scratch_shapes=[pltpu.VMEM((tm + 2 * SC_HALO, D), F32)],
        compiler_params=_cparams(1),
        name="sconv",
    )(x, x, x, mod, w_in, conv_w, w_out, ln_g.reshape(1, D), ln_b.reshape(1, D))


PL_SEG = 256
PL_NSEG = 2
PL_TM = PL_SEG * PL_NSEG
PL_HALO = 16
PL_STRIDE = PL_SEG + 2 * PL_HALO


def _pool_kernel(x_ref, prev_ref, next_ref, mod_ref, w_ref, pb_ref, ps_ref, g_ref, b_ref, o_ref,
                 e_ref, s2_ref, s4_ref, s8_ref):
    tm, seg, hl, gw = PL_TM, PL_SEG, PL_HALO, GROUP_W
    row0 = pl.program_id(0) * tm
    slen = _seq_len(row0)
    m = mod_ref[...]
    x = x_ref[...]
    h = x * (1.0 + m[1:2]) + m[0:1]
    hp = prev_ref[...] * (1.0 + m[1:2]) + m[0:1]
    hn = next_ref[...] * (1.0 + m[1:2]) + m[0:1]
    n0 = PL_NSEG * PL_STRIDE
    for i in range(PL_NSEG):
        base = i * PL_STRIDE
        pos0 = (row0 + i * seg) & (slen - 1)
        before = hp if i == 0 else h[i * seg - hl:i * seg]
        after = hn if i == PL_NSEG - 1 else h[(i + 1) * seg:(i + 1) * seg + hl]
        e_ref[base:base + hl, :] = jnp.where(pos0 != 0, before, 0.0)
        e_ref[base + hl:base + hl + seg, :] = h[i * seg:(i + 1) * seg]
        e_ref[base + hl + seg:base + PL_STRIDE, :] = jnp.where(pos0 + seg != slen, after, 0.0)
    e_ref[n0:, :] = jnp.zeros((8, D), F32)
    n2, n4, n8 = n0, n0 - 8, n0 - 16
    s2_ref[...] = e_ref[0:n2, :] + e_ref[1:n2 + 1, :]
    s4_ref[...] = s2_ref[0:n4, gw:] + s2_ref[2:n4 + 2, gw:]
    s8_ref[...] = s4_ref[0:n8, gw:] + s4_ref[4:n8 + 4, gw:]

    def rows(ref, off, lanes):
        return jnp.concatenate([ref[i * PL_STRIDE + off:i * PL_STRIDE + off + seg, lanes]
                                for i in range(PL_NSEG)], axis=0)

    tots = [rows(s2_ref, hl - 1, slice(0, gw)),
            rows(s4_ref, hl - 2, slice(0, gw)),
            rows(s8_ref, hl - 4, slice(0, gw)),
            rows(s8_ref, hl - 8, slice(gw, 2 * gw)) + rows(s8_ref, hl, slice(gw, 2 * gw))]
    pos = (row0 + lax.broadcasted_iota(jnp.int32, (tm, 1), 0)) & (slen - 1)
    ys = []
    for gi, win in enumerate(POOL_WINDOWS):
        cs = slice(gi * gw, (gi + 1) * gw)
        back = win // 2
        cnt = jnp.minimum(pos + (win - back), slen) - jnp.maximum(pos - back, 0)
        p = tots[gi] / cnt.astype(F32) - h[:, cs]
        ys.append(_dot(p.astype(BF), w_ref[gi]))
    y = (jnp.concatenate(ys, axis=1) + pb_ref[...]) * ps_ref[...]
    o_ref[...] = _ln(ALPHA * x + m[2:3] * y, g_ref[...], b_ref[...])


def _pool(x, mod, layer, w, pb, ps, ln_g, ln_b):
    assert POOL_WINDOWS == (2, 4, 8, 16) and PL_HALO >= POOL_WINDOWS[-1] // 2
    assert SEQ % PL_SEG == 0 and DEC_SEQ % PL_SEG == 0
    tm = PL_TM
    hb = tm // PL_HALO
    nhb = NTOK // PL_HALO
    n0 = PL_NSEG * PL_STRIDE
    return pl.pallas_call(
        _pool_kernel,
        grid=(NTOK // tm,),
        in_specs=[pl.BlockSpec((tm, D), lambda t: (t, 0)),
                  pl.BlockSpec((PL_HALO, D), lambda t: (jnp.maximum(t * hb - 1, 0), 0)),
                  pl.BlockSpec((PL_HALO, D), lambda t: (jnp.minimum((t + 1) * hb, nhb - 1), 0)),
                  _mod_spec(layer, tm),
                  _full((N_GROUPS, GROUP_W, GROUP_W)),
                  _full((1, D)), _full((1, D)), _full((1, D)), _full((1, D))],
        out_specs=pl.BlockSpec((tm, D), lambda t: (t, 0)),
        out_shape=jax.ShapeDtypeStruct((NTOK, D), F32),
        scratch_shapes=[pltpu.VMEM((n0 + 8, D), F32),
                        pltpu.VMEM((n0, D), F32),
                        pltpu.VMEM((n0 - 8, D - GROUP_W), F32),
                        pltpu.VMEM((n0 - 16, D - 2 * GROUP_W), F32)],
        compiler_params=_cparams(1),
        name="pool",
    )(x, x, x, mod, w, pb.reshape(1, D), ps.reshape(1, D), ln_g.reshape(1, D), ln_b.reshape(1, D))


FT_TK = 256


def _dft_mats(n, scale):
    k = np.arange(n, dtype=np.int64)
    ang = 2.0 * np.pi * ((k[:, None] * k[None, :]) % n).astype(np.float64) / n
    return np.stack([np.cos(ang) * scale, np.sin(ang) * scale]).astype(np.float32)


def _dot_x3(a_parts, b_parts):
    (ah, al), (bh, bl) = a_parts, b_parts
    return _dot(ah, bh) + (_dot(ah, bl) + _dot(al, bh))


FT_PAD = 8


def _fourier_kernel(x_ref, mod_ref, cs_ref, cc_ref, o_ref, *, s, tk):
    m = mod_ref[...]
    hparts = _split2(x_ref[...] * (1.0 + m[1:2]) + m[0:1])
    ccp = _split2(cc_ref[0])
    scp = _split2(cc_ref[1])
    ri = lax.broadcasted_iota(jnp.int32, (tk, tk), 0)
    ci = lax.broadcasted_iota(jnp.int32, (tk, tk), 1)
    anti = jnp.where(ri + ci == tk - 1, 1.0, 0.0).astype(BF)
    blocks = []
    for j in range(s // (2 * tk)):
        ac = _dot_x3(_split2(cs_ref[0, j]), hparts)
        as_ = _dot_x3(_split2(cs_ref[1, j]), hparts)
        direct, mirror = [], []
        for gi in range(N_GROUPS):
            cs = slice(gi * GROUP_W, (gi + 1) * GROUP_W)
            p = _dot_x3(_split2(ac[:, cs]), ccp)
            q = _dot_x3(_split2(as_[:, cs]), scp)
            direct.append((p - q)[0:tk])
            mirror.append((p + q)[1:tk + 1])
        blocks.append((j * tk, jnp.concatenate(direct, axis=1).astype(BF)))
        mir = jnp.concatenate(mirror, axis=1).astype(BF)
        blocks.append((s - (j + 1) * tk, _dot(anti, mir).astype(BF)))
    for r0, blk in blocks:
        o_ref[r0:r0 + tk, :] = blk


def _half_dft_mats(s, tk):
    full = _dft_mats(s, s ** -0.5)
    rows = (np.arange(s // (2 * tk))[:, None] * tk + np.arange(tk + FT_PAD)[None, :]) % s
    return full[:, rows, :]


def _fourier(x, mod, layer, s, nseq, row_off, mod_row_fn):
    tk = min(FT_TK, s // 2)
    seq_off = row_off // s
    cs = jnp.asarray(_half_dft_mats(s, tk))
    cc = jnp.asarray(_dft_mats(GROUP_W, GROUP_W ** -0.5))
    return pl.pallas_call(
        functools.partial(_fourier_kernel, s=s, tk=tk),
        grid=(nseq,),
        in_specs=[pl.BlockSpec((s, D), lambda b: (seq_off + b, 0)),
                  pl.BlockSpec((None, None, 6, D), lambda b: (layer, mod_row_fn(b), 0, 0)),
                  pl.BlockSpec(cs.shape, lambda b: (0, 0, 0, 0), pipeline_mode=pl.Buffered(1)),
                  _full((2, GROUP_W, GROUP_W))],
        out_specs=pl.BlockSpec((s, D), lambda b: (b, 0)),
        out_shape=jax.ShapeDtypeStruct((nseq * s, D), BF),
        compiler_params=_cparams(1),
        name="fourier_%d" % s,
    )(x, mod, cs, cc)


def _pos_embed():
    rows = DEC_SEQ // GRID_W
    quarter = D // 4
    omega = 1.0 / (10000.0 ** (np.arange(quarter, dtype=np.float64) / quarter))
    rr, cc = np.meshgrid(np.arange(rows, dtype=np.float64), np.arange(GRID_W, dtype=np.float64), indexing="ij")
    er = rr.reshape(-1, 1) * omega
    ec = cc.reshape(-1, 1) * omega
    return np.concatenate([np.sin(er), np.cos(er), np.sin(ec), np.cos(ec)], axis=-1).astype(np.float32)


def _gate_perm():
    perm = np.zeros(4 * H, dtype=np.int32)
    for hp in range(N_HP):
        for d in range(2):
            for j in range(2):
                for hh in range(2):
                    perm[hp * NG + d * 4 + j * 2 + hh] = d * 2 * H + j * H + 2 * hp + hh
    return perm


def kernel(x_prompt, x_sample, state_C, state_n, state_m, c, c_ctx, w_mod, b_mod, ln_g, ln_b, mlp_w1, mlp_w2,
           ml_w_in, ml_w_gate, ml_b_gate, ml_norm_g, ml_w_out, sc_w_in, sc_conv_w, sc_w_out,
           pl_w, pl_b, pl_scale, ft_w_out, ft_b_out):
    cond = jnp.concatenate([c_ctx[None, :], c, jnp.zeros((NCOND - 1 - DEC_BATCH, D), F32)], axis=0)
    mod = _modulation(cond, w_mod, b_mod)

    x = (x_prompt.reshape(NCTX, D), x_sample.reshape(NLAT, D), jnp.asarray(_pos_embed()))

    new_c = new_n = new_m = None
    for i in range(DEPTH):
        kind, j = i % 4, i // 4
        if kind == 0:
            w_in = ml_w_in[j]
            perm = _gate_perm()
            wg = ml_w_gate[j][:, perm]
            bg = ml_b_gate[j][perm]
            wt = jnp.concatenate([w_in[:, :QK], wg, w_in[:, QK:2 * QK + V]], axis=1).T.astype(BF)
            qt, k, kt, vt, o, g, gt = _ml_proj(
                x, mod, i, wt, w_in[:, QK:2 * QK].astype(BF), w_in[:, 2 * QK + V:].astype(BF), wg.astype(BF),
                bg.reshape(1, -1), bg.reshape(-1, 1))
            scan_args = (qt, k, kt, vt, o, g, gt, ml_norm_g[j])
            uc, cst, nmst = _ml_scan(*scan_args)
            c0 = jnp.concatenate([jnp.swapaxes(state_C[:, j], -1, -2), state_n[:, j][..., None, :],
                                  jnp.zeros((DEC_BATCH, 2, H, DHA - DHV - 1, DQK), F32)], axis=-2)
            m0 = jnp.broadcast_to(state_m[:, j][..., None, None], (DEC_BATCH, 2, H, 1, 128))
            (ul,) = _ml_scan(*scan_args, c0, m0)
            x = _out_proj(uc, ul, ml_w_out[j].astype(BF), jnp.zeros((D,), F32), x, mod, i, ln_g[i, 0], ln_b[i, 0])
            new_c = cst[:, None]
            new_n = nmst[..., 0, :DQK][:, None]
            new_m = nmst[..., 0, DQK][:, None]
        elif kind == 1:
            x = _sconv(x, mod, i, sc_w_in[j].astype(BF), sc_conv_w[j], sc_w_out[j].astype(BF),
                       ln_g[i, 0], ln_b[i, 0])
        elif kind == 2:
            x = _pool(x, mod, i, pl_w[j].astype(BF), pl_b[j], pl_scale[j], ln_g[i, 0], ln_b[i, 0])
        else:
            fc = _fourier(x, mod, i, SEQ, BATCH, 0, lambda b: 0)
            fl = _fourier(x, mod, i, DEC_SEQ, DEC_BATCH, NCTX, lambda b: 1 + b)
            x = _out_proj(fc, fl, ft_w_out[j].astype(BF), ft_b_out[j], x, mod, i, ln_g[i, 0], ln_b[i, 0])
        x = _mlp(x, mod, i, mlp_w1, mlp_w2, ln_g[i, 1], ln_b[i, 1],
                 split_out=(i == DEPTH - 1))

    y_prompt = x[0].reshape(BATCH, SEQ, D)
    y_sample = x[1].reshape(DEC_BATCH, DEC_SEQ, D)
    return y_prompt, y_sample, new_c, new_n, new_m
```

```python
import functools
import math

import numpy as np
import jax
import jax.numpy as jnp
from jax import lax
from jax.experimental import pallas as pl
from jax.experimental.pallas import tpu as pltpu

D = 1024
BATCH, SEQ = 16, 256
DEC_BATCH, DEC_SEQ = 4, 1024
DEPTH = 4
GRID_W = 64
H = 8
DQK = 64
DHV = 128
QK = H * DQK
V = H * DHV
N_GROUPS = 4
GROUP_W = D // N_GROUPS
POOL_WINDOWS = (2, 4, 8, 16)
D_FF = 4 * D
ALPHA = (2.0 * DEPTH) ** 0.25
LN_EPS = 1e-5
F32 = jnp.float32
BF = jnp.bfloat16

NCTX = BATCH * SEQ
NLAT = DEC_BATCH * DEC_SEQ
NTOK = NCTX + NLAT
NCOND = 8
CHUNK = 256
VMEM_LIMIT = 56 * 1024 * 1024


def _cparams(n_axes):
    return pltpu.CompilerParams(dimension_semantics=("arbitrary",) * n_axes,
                                vmem_limit_bytes=VMEM_LIMIT)


def _mod_row(row0):
    return jnp.where(row0 < NCTX, 0, 1 + (row0 - NCTX) // DEC_SEQ)


def _seq_len(row0):
    return jnp.where(row0 < NCTX, SEQ, DEC_SEQ)


def _ln(z, g, b):
    mu = jnp.mean(z, axis=-1, keepdims=True)
    zc = z - mu
    var = jnp.mean(zc * zc, axis=-1, keepdims=True)
    return zc * lax.rsqrt(var + LN_EPS) * g + b


def _dot(a, b):
    return jnp.dot(a, b, preferred_element_type=F32)


def _dot_nt(a, b):
    return lax.dot_general(a, b, (((1,), (1,)), ((), ())), preferred_element_type=F32)


def _split2(x):
    hi = x.astype(BF)
    lo = (x - hi.astype(F32)).astype(BF)
    return hi, lo


def _split3(x):
    hi = x.astype(BF)
    r = x - hi.astype(F32)
    mid = r.astype(BF)
    lo = (r - mid.astype(F32)).astype(BF)
    return hi, mid, lo


def _full(shape):
    n = len(shape)
    return pl.BlockSpec(shape, lambda *_: (0,) * n)


def _x_specs(x, tm):
    if not isinstance(x, tuple):
        return [pl.BlockSpec((tm, D), lambda t: (t, 0))], [x]
    nc = NCTX // tm
    return ([pl.BlockSpec((tm, D), lambda t: (jnp.minimum(t, nc - 1), 0)),
             pl.BlockSpec((tm, D), lambda t: (jnp.maximum(t - nc, 0), 0)),
             pl.BlockSpec((tm, D), lambda t: (t % (DEC_SEQ // tm), 0))], list(x))


def _read_x(x_refs, tm):
    if len(x_refs) == 1:
        return x_refs[0][...]
    xc_ref, xl_ref, pe_ref = x_refs
    return jnp.where(pl.program_id(0) * tm < NCTX, xc_ref[...], xl_ref[...] + pe_ref[...])


def _mod_spec(layer, tm):
    return pl.BlockSpec((None, None, 6, D), lambda t: (layer, _mod_row(t * tm), 0, 0))


def _mod_kernel(c_ref, w_ref, b_ref, o_ref):
    c = c_ref[...]
    s = c * jax.nn.sigmoid(c)
    o_ref[...] = _dot(s.astype(BF), w_ref[...].astype(BF)) + b_ref[...]


def _modulation(cond, w_mod, b_mod):
    tn = 1536
    out = pl.pallas_call(
        _mod_kernel,
        grid=(DEPTH, 6 * D // tn),
        in_specs=[pl.BlockSpec((NCOND, D), lambda i, j: (0, 0)),
                  pl.BlockSpec((None, D, tn), lambda i, j: (i, 0, j)),
                  pl.BlockSpec((None, 1, tn), lambda i, j: (i, 0, j))],
        out_specs=pl.BlockSpec((None, NCOND, tn), lambda i, j: (i, 0, j)),
        out_shape=jax.ShapeDtypeStruct((DEPTH, NCOND, 6 * D), F32),
        compiler_params=_cparams(2),
        name="modulation",
    )(cond, w_mod, b_mod.reshape(DEPTH, 1, 6 * D))
    return out.reshape(DEPTH, NCOND, 6, D)


MLP_TM = 512
MLP_FC = 512
MLP_NF = D_FF // MLP_FC
MLP_NT = NTOK // MLP_TM


def _mlp_tile(s):
    return jnp.maximum(s - (MLP_NF - 1), 0)


def _mlp_kernel(mod_ref, w1_ref, w2_ref, g_ref, b_ref, x1_ref, x_ref, *refs):
    o_refs, (w1s_ref, w2s_ref, acc_ref, h_ref) = refs[:-4], refs[-4:]
    s = pl.program_id(0)
    m = mod_ref[...]

    def pre(x):
        return (x * (1.0 + m[4:5]) + m[3:4]).astype(BF)

    def hidden(h, w1c):
        return jnp.square(jnp.maximum(_dot(h, w1c), 0.0)).astype(BF)

    def result(x, acc):
        return _ln(ALPHA * x + m[5:6] * acc, g_ref[...], b_ref[...])

    @pl.when(s < MLP_NF)
    def _():
        @pl.when(s == 0)
        def _():
            h_ref[0:MLP_TM, :] = pre(x_ref[...])
            h_ref[MLP_TM:, :] = pre(x1_ref[...])

        w1c = w1_ref[...].astype(BF)
        w2c = w2_ref[...].astype(BF)
        part = _dot(hidden(h_ref[...], w1c), w2c)
        w1s_ref[s] = w1c
        w2s_ref[s] = w2c

        @pl.when(s == 0)
        def _():
            acc_ref[...] = part

        @pl.when(s > 0)
        def _():
            acc_ref[...] += part

        @pl.when(s == MLP_NF - 1)
        def _():
            o_refs[0][...] = result(x_ref[...], acc_ref[0:MLP_TM, :])

    @pl.when(s == MLP_NF)
    def _():
        o_refs[0][...] = result(x1_ref[...], acc_ref[MLP_TM:, :])

    @pl.when(s > MLP_NF)
    def _():
        x = x_ref[...]
        h = pre(x)
        acc = jnp.zeros((MLP_TM, D), F32)
        for f in range(MLP_NF):
            acc = acc + _dot(hidden(h, w1s_ref[f]), w2s_ref[f])
        res = result(x, acc)
        if len(o_refs) == 1:
            o_refs[0][...] = res
        else:
            is_ctx = _mlp_tile(s) * MLP_TM < NCTX

            @pl.when(is_ctx)
            def _():
                o_refs[0][...] = res

            @pl.when(jnp.logical_not(is_ctx))
            def _():
                o_refs[1][...] = res


def _mlp(x, mod, layer, w1, w2, ln_g, ln_b, split_out=False):
    tm = MLP_TM
    nc = NCTX // tm
    assert nc >= 2 and MLP_NT >= 2
    if split_out:
        out_specs = [pl.BlockSpec((tm, D), lambda s: (jnp.minimum(_mlp_tile(s), nc - 1), 0)),
                     pl.BlockSpec((tm, D), lambda s: (jnp.maximum(_mlp_tile(s) - nc, 0), 0))]
        out_shape = [jax.ShapeDtypeStruct((NCTX, D), F32), jax.ShapeDtypeStruct((NLAT, D), F32)]
    else:
        out_specs = pl.BlockSpec((tm, D), lambda s: (_mlp_tile(s), 0))
        out_shape = jax.ShapeDtypeStruct((NTOK, D), F32)
    return pl.pallas_call(
        _mlp_kernel,
        grid=(MLP_NF - 1 + MLP_NT,),
        in_specs=[pl.BlockSpec((None, None, 6, D), lambda s: (layer, _mod_row(_mlp_tile(s) * tm), 0, 0)),
                  pl.BlockSpec((None, D, MLP_FC), lambda s: (layer, 0, jnp.minimum(s, MLP_NF - 1))),
                  pl.BlockSpec((None, MLP_FC, D), lambda s: (layer, jnp.minimum(s, MLP_NF - 1), 0)),
                  _full((1, D)), _full((1, D)),
                  pl.BlockSpec((tm, D), lambda s: (1, 0)),
                  pl.BlockSpec((tm, D), lambda s: (_mlp_tile(s), 0))],
        out_specs=out_specs,
        out_shape=out_shape,
        scratch_shapes=[pltpu.VMEM((MLP_NF, D, MLP_FC), BF),
                        pltpu.VMEM((MLP_NF, MLP_FC, D), BF),
                        pltpu.VMEM((2 * tm, D), F32),
                        pltpu.VMEM((2 * tm, D), BF)],
        compiler_params=_cparams(1),
        name="mlp",
    )(mod, w1, w2, ln_g.reshape(1, D), ln_b.reshape(1, D), x, x)


OUT_TM = 1024
OUT_SUB = 256


def _out_kernel(uc_ref, ul_ref, w_ref, bias_ref, mod_ref, g_ref, b_ref, *refs):
    x_refs, o_ref = refs[:-1], refs[-1]
    is_ctx = pl.program_id(0) * OUT_TM < NCTX
    m = mod_ref[...]
    outs = []
    for r in range(0, OUT_TM, OUT_SUB):
        rs = slice(r, r + OUT_SUB)
        u = jnp.where(is_ctx, uc_ref[rs, :], ul_ref[rs, :])
        y = _dot(u, w_ref[...]) + bias_ref[...]
        if len(x_refs) == 1:
            x = x_refs[0][rs, :]
        else:
            x = jnp.where(is_ctx, x_refs[0][rs, :], x_refs[1][rs, :] + x_refs[2][rs, :])
        outs.append(_ln(ALPHA * x + m[2:3] * y, g_ref[...], b_ref[...]))
    for r, out in zip(range(0, OUT_TM, OUT_SUB), outs):
        o_ref[r:r + OUT_SUB, :] = out


def _out_proj(uc, ul, w, bias, x, mod, layer, ln_g, ln_b):
    tm = OUT_TM
    nc = NCTX // tm
    k = w.shape[0]
    x_specs, x_args = _x_specs(x, tm)
    return pl.pallas_call(
        _out_kernel,
        grid=(NTOK // tm,),
        in_specs=[pl.BlockSpec((tm, k), lambda t: (jnp.minimum(t, nc - 1), 0)),
                  pl.BlockSpec((tm, k), lambda t: (jnp.maximum(t - nc, 0), 0)),
                  pl.BlockSpec((k, D), lambda t: (0, 0), pipeline_mode=pl.Buffered(1)),
                  _full((1, D)),
                  _mod_spec(layer, tm),
                  _full((1, D)), _full((1, D))] + x_specs,
        out_specs=pl.BlockSpec((tm, D), lambda t: (t, 0)),
        out_shape=jax.ShapeDtypeStruct((NTOK, D), F32),
        compiler_params=_cparams(1),
        name="out_proj",
    )(uc, ul, w, bias.reshape(1, D), mod, ln_g.reshape(1, D), ln_b.reshape(1, D), *x_args)


ML_TM = 512
N_HP = H // 2
SCAN_PAIRS = 2
NG = 8


def _log_sigmoid(g):
    return jnp.minimum(g, 0.0) - jnp.log1p(jnp.exp(-jnp.abs(g)))


def _ml_proj_kernel(mod_ref, wt_ref, wk_ref, wo_ref, wg_ref, bg_ref, bgt_ref, *refs):
    x_refs = refs[:-7]
    qt_ref, k_ref, kt_ref, vt_ref, o_ref, g_ref, gt_ref = refs[-7:]
    m = mod_ref[...]
    h = (_read_x(x_refs, ML_TM) * (1.0 + m[1:2]) + m[0:1]).astype(BF)
    tr = _dot_nt(wt_ref[...], h)
    r0, r1, r2 = QK, QK + 4 * NG, 2 * QK + 4 * NG
    qt = tr[:r0].astype(BF)
    kt = (tr[r1:r2] * (DQK ** -0.5)).astype(BF)
    vt = tr[r2:].astype(BF)
    k = (_dot(h, wk_ref[...]) * (DQK ** -0.5)).astype(BF)
    o = _dot(h, wo_ref[...])
    g = _dot(h, wg_ref[...]) + bg_ref[...]
    col = lax.broadcasted_iota(jnp.int32, g.shape, 1)
    g = jnp.where((col & 2) != 0, _log_sigmoid(g), g)
    gt = tr[r0:r1] + bgt_ref[...]
    row = lax.broadcasted_iota(jnp.int32, gt.shape, 0)
    gt = jnp.where((row & 2) != 0, _log_sigmoid(gt), gt)
    qt_ref[...] = qt
    k_ref[...] = k
    kt_ref[...] = kt
    vt_ref[...] = vt
    o_ref[...] = o
    for hp in range(N_HP):
        g_ref[hp] = g[:, hp * NG:(hp + 1) * NG]
        gt_ref[hp] = gt[hp * NG:(hp + 1) * NG, :]


def _ml_proj(x, mod, layer, wt, wk, wo, wg, bg, bgt):
    tm = ML_TM
    res = lambda shape: pl.BlockSpec(shape, lambda t: (0,) * len(shape), pipeline_mode=pl.Buffered(1))
    x_specs, x_args = _x_specs(x, tm)
    return pl.pallas_call(
        _ml_proj_kernel,
        grid=(NTOK // tm,),
        in_specs=[_mod_spec(layer, tm),
                  res((2 * QK + 4 * NG + V, D)), res((D, QK)), res((D, V)),
                  res((D, 4 * NG)), res((1, 4 * NG)), res((4 * NG, 1))] + x_specs,
        out_specs=[pl.BlockSpec((QK, tm), lambda t: (0, t)),
                   pl.BlockSpec((tm, QK), lambda t: (t, 0)),
                   pl.BlockSpec((QK, tm), lambda t: (0, t)),
                   pl.BlockSpec((V, tm), lambda t: (0, t)),
                   pl.BlockSpec((tm, V), lambda t: (t, 0)),
                   pl.BlockSpec((N_HP, tm, NG), lambda t: (0, t, 0)),
                   pl.BlockSpec((N_HP, NG, tm), lambda t: (0, 0, t))],
        out_shape=[jax.ShapeDtypeStruct((QK, NTOK), BF),
                   jax.ShapeDtypeStruct((NTOK, QK), BF),
                   jax.ShapeDtypeStruct((QK, NTOK), BF),
                   jax.ShapeDtypeStruct((V, NTOK), BF),
                   jax.ShapeDtypeStruct((NTOK, V), F32),
                   jax.ShapeDtypeStruct((N_HP, NTOK, NG), F32),
                   jax.ShapeDtypeStruct((N_HP, NG, NTOK), F32)],
        compiler_params=_cparams(1),
        name="mlstm_proj",
    )(mod, wt, wk, wo, wg, bg, bgt, *x_args)


DHA = DHV + 8


def _scan_chunk(k_c, kt_c, qt_c, vta, rr_col, rr_row, b_row, m, cta, d, need_state):
    L = k_c.shape[0]
    si = lax.broadcasted_iota(jnp.int32, (L, L), 0)
    ti = lax.broadcasted_iota(jnp.int32, (L, L), 1)
    mask = (si <= ti) if d == 0 else (si >= ti)
    rrm = jnp.where(mask, rr_col, -jnp.inf)
    g = jnp.maximum(jnp.max(rrm, axis=0, keepdims=True), m)
    st = _dot(k_c, qt_c) * jnp.exp(rrm - g)
    num = _dot(vta[:DHV], st.astype(BF))
    den = jnp.sum(st, axis=0, keepdims=True)
    if cta is not None:
        a = jnp.exp(m - g)
        qc = _dot(cta.astype(BF), qt_c)
        num = num + a * qc[:DHV]
        den = den + a * qc[DHV:DHV + 1]
    mt = b_row + g
    ht = num * (1.0 / jnp.maximum(jnp.abs(den), jnp.exp(-mt)))
    if not need_state:
        return ht, None, None
    last = L - 1 if d == 0 else 0
    m_new = mt[:, last:last + 1]
    b_last = b_row[:, last:last + 1]
    w = jnp.exp(b_last + rr_row - m_new)
    kw = (kt_c.astype(F32) * w).astype(BF)
    if cta is None:
        return ht, (_dot_nt(kw, vta[:DHV]), _dot_nt(vta[DHV:], kw)), m_new
    c_new = jnp.exp(b_last + m - m_new) * cta + _dot_nt(vta, kw)
    return ht, c_new, m_new


def _scan_kernel(*refs, is_ctx):
    if is_ctx:
        qt_ref, k_ref, kt_ref, vt_ref, o_ref, g_ref, gt_ref, ng_ref, u_ref, cst_ref, nmst_ref = refs
    else:
        qt_ref, k_ref, kt_ref, vt_ref, o_ref, g_ref, gt_ref, ng_ref, c0_ref, m0_ref, u_ref = refs
    L = CHUNK
    nch = DEC_SEQ // L
    ri = lax.broadcasted_iota(jnp.int32, (L, L), 0)
    ci = lax.broadcasted_iota(jnp.int32, (L, L), 1)
    lower = jnp.where(ci <= ri, 1.0, 0.0).astype(BF)
    upper = jnp.where(ci >= ri, 1.0, 0.0).astype(BF)
    sub = lax.broadcasted_iota(jnp.int32, (DHA - DHV, L), 0)
    ones_rows = jnp.where(sub == 0, 1.0, 0.0).astype(BF)

    rr_cols, rr_rows, b_rows = [], [], []
    for pp in range(SCAN_PAIRS):
        tparts = _split3(jnp.concatenate([gt_ref[pp, :, c * L:(c + 1) * L] for c in range(nch)], axis=0))
        b_row_all = (sum(_dot(p, upper) for p in tparts), sum(_dot(p, lower) for p in tparts))
        rr_cols.append([])
        rr_rows.append([])
        b_rows.append([])
        for c in range(nch):
            gc = g_ref[pp, c * L:(c + 1) * L, :]
            gtc = gt_ref[pp, :, c * L:(c + 1) * L]
            gparts = _split3(gc)
            b_col = (sum(_dot(lower, p) for p in gparts), sum(_dot(upper, p) for p in gparts))
            b_row = tuple(b[c * NG:(c + 1) * NG, :] for b in b_row_all)
            rr_cols[pp].append([gc[:, 0:NG - 2] - b[:, 2:NG] for b in b_col])
            rr_rows[pp].append([gtc[0:NG - 2, :] - b[2:NG, :] for b in b_row])
            b_rows[pp].append(b_row)

    states, us = [], []
    for hl in range(2 * SCAN_PAIRS):
        pp, hh = divmod(hl, 2)
        hsum = [None] * nch
        for d in range(2):
            ig_i, lf_i = d * 4 + hh, d * 4 + 2 + hh
            if is_ctx:
                m, cta = jnp.zeros((1, 1), F32), None
            else:
                m = m0_ref[d, hl][:, 0:1]
                cta = c0_ref[d, hl]
            order = range(nch) if d == 0 else range(nch - 1, -1, -1)
            for n, c in enumerate(order):
                sl = slice(c * L, (c + 1) * L)
                hs = slice(hl * DQK, (hl + 1) * DQK)
                vta = jnp.concatenate([vt_ref[hl * DHV:(hl + 1) * DHV, sl], ones_rows], axis=0)
                need_state = is_ctx or n < nch - 1
                hcur, c_new, m_new = _scan_chunk(
                    k_ref[sl, hs], kt_ref[hs, sl], qt_ref[hs, sl], vta,
                    rr_cols[pp][c][d][:, ig_i:ig_i + 1], rr_rows[pp][c][d][ig_i:ig_i + 1, :],
                    b_rows[pp][c][d][lf_i:lf_i + 1, :], m, cta, d, need_state)
                hsum[c] = hcur if d == 0 else hsum[c] + hcur
                if is_ctx:
                    states.append(((c, d, hl), c_new, m_new))
                else:
                    m, cta = m_new, c_new

        for c in range(nch):
            sl = slice(c * L, (c + 1) * L)
            mu = jnp.mean(hsum[c], axis=0, keepdims=True)
            hc = hsum[c] - mu
            var = jnp.mean(hc * hc, axis=0, keepdims=True)
            hn = hc * lax.rsqrt(var + LN_EPS) * ng_ref[hl * DHV:(hl + 1) * DHV, :]
            og = jax.nn.sigmoid(o_ref[sl, hl * DHV:(hl + 1) * DHV])
            us.append(((sl, slice(hl * DHV, (hl + 1) * DHV)), (hn.T * og).astype(BF)))

    for idx, u in us:
        u_ref[idx] = u
    for idx, (c_fin, n_fin), m_fin in states:
        cst_ref[idx] = c_fin
        nmst_ref[idx] = jnp.concatenate([n_fin, jnp.broadcast_to(m_fin, (8, 128 - DQK))], axis=1)


def _ml_scan(qt, k, kt, vt, o, g, gt, norm_g, c0aug=None, m0b=None):
    is_ctx = c0aug is None
    tt = DEC_SEQ
    toff = 0 if is_ctx else NCTX // tt
    nt = (NCTX if is_ctx else NLAT) // tt
    nh = 2 * SCAN_PAIRS
    in_specs = [pl.BlockSpec((nh * DQK, tt), lambda t, p: (p, toff + t)),
                pl.BlockSpec((tt, nh * DQK), lambda t, p: (toff + t, p)),
                pl.BlockSpec((nh * DQK, tt), lambda t, p: (p, toff + t)),
                pl.BlockSpec((nh * DHV, tt), lambda t, p: (p, toff + t)),
                pl.BlockSpec((tt, nh * DHV), lambda t, p: (toff + t, p)),
                pl.BlockSpec((SCAN_PAIRS, tt, NG), lambda t, p: (p, toff + t, 0)),
                pl.BlockSpec((SCAN_PAIRS, NG, tt), lambda t, p: (p, 0, toff + t)),
                pl.BlockSpec((nh * DHV, 1), lambda t, p: (p, 0))]
    args = [qt, k, kt, vt, o, g, gt, norm_g.reshape(V, 1)]
    u_spec = pl.BlockSpec((tt, nh * DHV), lambda t, p: (t, p))
    u_shape = jax.ShapeDtypeStruct((nt * tt, V), BF)
    if is_ctx:
        nseq = tt // SEQ
        out_specs = [u_spec,
                     pl.BlockSpec((nseq, 2, nh, DQK, DHV), lambda t, p: (t, 0, p, 0, 0)),
                     pl.BlockSpec((nseq, 2, nh, 8, 128), lambda t, p: (t, 0, p, 0, 0))]
        out_shape = [u_shape,
                     jax.ShapeDtypeStruct((BATCH, 2, H, DQK, DHV), F32),
                     jax.ShapeDtypeStruct((BATCH, 2, H, 8, 128), F32)]
    else:
        in_specs += [pl.BlockSpec((None, 2, nh, DHA, DQK), lambda t, p: (t, 0, p, 0, 0)),
                     pl.BlockSpec((None, 2, nh, 1, 128), lambda t, p: (t, 0, p, 0, 0))]
        args += [c0aug, m0b]
        out_specs = [u_spec]
        out_shape = [u_shape]
    return pl.pallas_call(
        functools.partial(_scan_kernel, is_ctx=is_ctx),
        grid=(nt, N_HP // SCAN_PAIRS),
        in_specs=in_specs,
        out_specs=out_specs,
        out_shape=out_shape,
        compiler_params=_cparams(2),
        name="mlstm_scan_ctx" if is_ctx else "mlstm_scan_lat",
    )(*args)


SC_TM = 512


SC_HALO = 8


def _sconv_kernel(x_ref, prev_ref, next_ref, mod_ref, w_in_ref, cw_ref, w_out_ref, g_ref, b_ref, o_ref, e_ref):
    tm = SC_TM
    row0 = pl.program_id(0) * tm
    slen = _seq_len(row0)
    pos = (row0 + lax.broadcasted_iota(jnp.int32, (tm, 1), 0)) & (slen - 1)
    m = mod_ref[...]
    x = x_ref[...]
    xe = jnp.concatenate([prev_ref[...], x, next_ref[...]], axis=0)
    he = (xe * (1.0 + m[1:2]) + m[0:1]).astype(BF)
    cu = _dot(he, w_in_ref[:, D:2 * D]) * _dot(he, w_in_ref[:, 2 * D:3 * D])
    bg = _dot(he[SC_HALO:SC_HALO + tm], w_in_ref[:, 0:D])
    e_ref[...] = cu
    prev = jnp.where(pos == 0, 0.0, e_ref[SC_HALO - 1:SC_HALO - 1 + tm, :])
    nxt = jnp.where(pos == slen - 1, 0.0, e_ref[SC_HALO + 1:SC_HALO + 1 + tm, :])
    conv = cw_ref[0:1, :] * prev + cw_ref[1:2, :] * e_ref[SC_HALO:SC_HALO + tm, :] + cw_ref[2:3, :] * nxt
    y = _dot((bg * conv).astype(BF), w_out_ref[...])
    o_ref[...] = _ln(ALPHA * x + m[2:3] * y, g_ref[...], b_ref[...])


def _sconv(x, mod, layer, w_in, conv_w, w_out, ln_g, ln_b):
    tm = SC_TM
    hb = tm // SC_HALO
    nhb = NTOK // SC_HALO
    return pl.pallas_call(
        _sconv_kernel,
        grid=(NTOK // tm,),
        in_specs=[pl.BlockSpec((tm, D), lambda t: (t, 0)),
                  pl.BlockSpec((SC_HALO, D), lambda t: (jnp.maximum(t * hb - 1, 0), 0)),
                  pl.BlockSpec((SC_HALO, D), lambda t: (jnp.minimum((t + 1) * hb, nhb - 1), 0)),
                  _mod_spec(layer, tm),
                  pl.BlockSpec((D, 3 * D), lambda t: (0, 0), pipeline_mode=pl.Buffered(1)),
                  _full((3, D)),
                  pl.BlockSpec((D, D), lambda t: (0, 0), pipeline_mode=pl.Buffered(1)),
                  _full((1, D)), _full((1, D))],
        out_specs=pl.BlockSpec((tm, D), lambda t: (t, 0)),
        out_shape=jax.ShapeDtypeStruct((NTOK, D), F32),
        scratch_shapes=[pltpu.VMEM((tm + 2 * SC_HALO, D), F32)],
        compiler_params=_cparams(1),
        name="sconv",
    )(x, x, x, mod, w_in, conv_w, w_out, ln_g.reshape(1, D), ln_b.reshape(1, D))


PL_SEG = 256
PL_NSEG = 2
PL_TM = PL_SEG * PL_NSEG
PL_HALO = 16
PL_STRIDE = PL_SEG + 2 * PL_HALO


def _pool_kernel(x_ref, prev_ref, next_ref, mod_ref, w_ref, pb_ref, ps_ref, g_ref, b_ref, o_ref,
                 e_ref, s2_ref, s4_ref, s8_ref):
    tm, seg, hl, gw = PL_TM, PL_SEG, PL_HALO, GROUP_W
    row0 = pl.program_id(0) * tm
    slen = _seq_len(row0)
    m = mod_ref[...]
    x = x_ref[...]
    h = x * (1.0 + m[1:2]) + m[0:1]
    hp = prev_ref[...] * (1.0 + m[1:2]) + m[0:1]
    hn = next_ref[...] * (1.0 + m[1:2]) + m[0:1]
    n0 = PL_NSEG * PL_STRIDE
    for i in range(PL_NSEG):
        base = i * PL_STRIDE
        pos0 = (row0 + i * seg) & (slen - 1)
        before = hp if i == 0 else h[i * seg - hl:i * seg]
        after = hn if i == PL_NSEG - 1 else h[(i + 1) * seg:(i + 1) * seg + hl]
        e_ref[base:base + hl, :] = jnp.where(pos0 != 0, before, 0.0)
        e_ref[base + hl:base + hl + seg, :] = h[i * seg:(i + 1) * seg]
        e_ref[base + hl + seg:base + PL_STRIDE, :] = jnp.where(pos0 + seg != slen, after, 0.0)
    e_ref[n0:, :] = jnp.zeros((8, D), F32)
    n2, n4, n8 = n0, n0 - 8, n0 - 16
    s2_ref[...] = e_ref[0:n2, :] + e_ref[1:n2 + 1, :]
    s4_ref[...] = s2_ref[0:n4, gw:] + s2_ref[2:n4 + 2, gw:]
    s8_ref[...] = s4_ref[0:n8, gw:] + s4_ref[4:n8 + 4, gw:]

    def rows(ref, off, lanes):
        return jnp.concatenate([ref[i * PL_STRIDE + off:i * PL_STRIDE + off + seg, lanes]
                                for i in range(PL_NSEG)], axis=0)

    tots = [rows(s2_ref, hl - 1, slice(0, gw)),
            rows(s4_ref, hl - 2, slice(0, gw)),
            rows(s8_ref, hl - 4, slice(0, gw)),
            rows(s8_ref, hl - 8, slice(gw, 2 * gw)) + rows(s8_ref, hl, slice(gw, 2 * gw))]
    pos = (row0 + lax.broadcasted_iota(jnp.int32, (tm, 1), 0)) & (slen - 1)
    ys = []
    for gi, win in enumerate(POOL_WINDOWS):
        cs = slice(gi * gw, (gi + 1) * gw)
        back = win // 2
        cnt = jnp.minimum(pos + (win - back), slen) - jnp.maximum(pos - back, 0)
        p = tots[gi] / cnt.astype(F32) - h[:, cs]
        ys.append(_dot(p.astype(BF), w_ref[gi]))
    y = (jnp.concatenate(ys, axis=1) + pb_ref[...]) * ps_ref[...]
    o_ref[...] = _ln(ALPHA * x + m[2:3] * y, g_ref[...], b_ref[...])


def _pool(x, mod, layer, w, pb, ps, ln_g, ln_b):
    assert POOL_WINDOWS == (2, 4, 8, 16) and PL_HALO >= POOL_WINDOWS[-1] // 2
    assert SEQ % PL_SEG == 0 and DEC_SEQ % PL_SEG == 0
    tm = PL_TM
    hb = tm // PL_HALO
    nhb = NTOK // PL_HALO
    n0 = PL_NSEG * PL_STRIDE
    return pl.pallas_call(
        _pool_kernel,
        grid=(NTOK // tm,),
        in_specs=[pl.BlockSpec((tm, D), lambda t: (t, 0)),
                  pl.BlockSpec((PL_HALO, D), lambda t: (jnp.maximum(t * hb - 1, 0), 0)),
                  pl.BlockSpec((PL_HALO, D), lambda t: (jnp.minimum((t + 1) * hb, nhb - 1), 0)),
                  _mod_spec(layer, tm),
                  _full((N_GROUPS, GROUP_W, GROUP_W)),
                  _full((1, D)), _full((1, D)), _full((1, D)), _full((1, D))],
        out_specs=pl.BlockSpec((tm, D), lambda t: (t, 0)),
        out_shape=jax.ShapeDtypeStruct((NTOK, D), F32),
        scratch_shapes=[pltpu.VMEM((n0 + 8, D), F32),
                        pltpu.VMEM((n0, D), F32),
                        pltpu.VMEM((n0 - 8, D - GROUP_W), F32),
                        pltpu.VMEM((n0 - 16, D - 2 * GROUP_W), F32)],
        compiler_params=_cparams(1),
        name="pool",
    )(x, x, x, mod, w, pb.reshape(1, D), ps.reshape(1, D), ln_g.reshape(1, D), ln_b.reshape(1, D))


FT_TK = 256


def _dft_mats(n, scale):
    k = np.arange(n, dtype=np.int64)
    ang = 2.0 * np.pi * ((k[:, None] * k[None, :]) % n).astype(np.float64) / n
    return np.stack([np.cos(ang) * scale, np.sin(ang) * scale]).astype(np.float32)


def _dot_x3(a_parts, b_parts):
    (ah, al), (bh, bl) = a_parts, b_parts
    return _dot(ah, bh) + (_dot(ah, bl) + _dot(al, bh))


FT_PAD = 8


def _fourier_kernel(x_ref, mod_ref, cs_ref, cc_ref, o_ref, *, s, tk, nsq):
    m = mod_ref[...]
    ccp = _split2(cc_ref[0])
    scp = _split2(cc_ref[1])
    ri = lax.broadcasted_iota(jnp.int32, (tk, tk), 0)
    ci = lax.broadcasted_iota(jnp.int32, (tk, tk), 1)
    anti = jnp.where(ri + ci == tk - 1, 1.0, 0.0).astype(BF)
    blocks = []
    for sq in range(nsq):
        base = sq * s
        hparts = _split2(x_ref[base:base + s, :] * (1.0 + m[1:2]) + m[0:1])
        for j in range(s // (2 * tk)):
            ac = _dot_x3(_split2(cs_ref[0, j]), hparts)
            as_ = _dot_x3(_split2(cs_ref[1, j]), hparts)
            direct, mirror = [], []
            for gi in range(N_GROUPS):
                cs = slice(gi * GROUP_W, (gi + 1) * GROUP_W)
                p = _dot_x3(_split2(ac[:, cs]), ccp)
                q = _dot_x3(_split2(as_[:, cs]), scp)
                direct.append((p - q)[0:tk])
                mirror.append((p + q)[1:tk + 1])
            blocks.append((base + j * tk, jnp.concatenate(direct, axis=1).astype(BF)))
            mir = jnp.concatenate(mirror, axis=1).astype(BF)
            blocks.append((base + s - (j + 1) * tk, _dot(anti, mir).astype(BF)))
    for r0, blk in blocks:
        o_ref[r0:r0 + tk, :] = blk


def _half_dft_mats(s, tk):
    full = _dft_mats(s, s ** -0.5)
    rows = (np.arange(s // (2 * tk))[:, None] * tk + np.arange(tk + FT_PAD)[None, :]) % s
    return full[:, rows, :]


def _fourier(x, mod, layer, s, nseq, nsq, row_off, mod_row_fn):
    tk = min(FT_TK, s // 2)
    blk = nsq * s
    blk_off = row_off // blk
    cs = jnp.asarray(_half_dft_mats(s, tk))
    cc = jnp.asarray(_dft_mats(GROUP_W, GROUP_W ** -0.5))
    return pl.pallas_call(
        functools.partial(_fourier_kernel, s=s, tk=tk, nsq=nsq),
        grid=(nseq // nsq,),
        in_specs=[pl.BlockSpec((blk, D), lambda b: (blk_off + b, 0)),
                  pl.BlockSpec((None, None, 6, D), lambda b: (layer, mod_row_fn(b), 0, 0)),
                  pl.BlockSpec(cs.shape, lambda b: (0, 0, 0, 0), pipeline_mode=pl.Buffered(1)),
                  _full((2, GROUP_W, GROUP_W))],
        out_specs=pl.BlockSpec((blk, D), lambda b: (b, 0)),
        out_shape=jax.ShapeDtypeStruct((nseq * s, D), BF),
        compiler_params=_cparams(1),
        name="fourier_%d" % s,
    )(x, mod, cs, cc)


def _pos_embed():
    rows = DEC_SEQ // GRID_W
    quarter = D // 4
    omega = 1.0 / (10000.0 ** (np.arange(quarter, dtype=np.float64) / quarter))
    rr, cc = np.meshgrid(np.arange(rows, dtype=np.float64), np.arange(GRID_W, dtype=np.float64), indexing="ij")
    er = rr.reshape(-1, 1) * omega
    ec = cc.reshape(-1, 1) * omega
    return np.concatenate([np.sin(er), np.cos(er), np.sin(ec), np.cos(ec)], axis=-1).astype(np.float32)


def _gate_perm():
    perm = np.zeros(4 * H, dtype=np.int32)
    for hp in range(N_HP):
        for d in range(2):
            for j in range(2):
                for hh in range(2):
                    perm[hp * NG + d * 4 + j * 2 + hh] = d * 2 * H + j * H + 2 * hp + hh
    return perm


def kernel(x_prompt, x_sample, state_C, state_n, state_m, c, c_ctx, w_mod, b_mod, ln_g, ln_b, mlp_w1, mlp_w2,
           ml_w_in, ml_w_gate, ml_b_gate, ml_norm_g, ml_w_out, sc_w_in, sc_conv_w, sc_w_out,
           pl_w, pl_b, pl_scale, ft_w_out, ft_b_out):
    cond = jnp.concatenate([c_ctx[None, :], c, jnp.zeros((NCOND - 1 - DEC_BATCH, D), F32)], axis=0)
    mod = _modulation(cond, w_mod, b_mod)

    x = (x_prompt.reshape(NCTX, D), x_sample.reshape(NLAT, D), jnp.asarray(_pos_embed()))

    new_c = new_n = new_m = None
    for i in range(DEPTH):
        kind, j = i % 4, i // 4
        if kind == 0:
            w_in = ml_w_in[j]
            perm = _gate_perm()
            wg = ml_w_gate[j][:, perm]
            bg = ml_b_gate[j][perm]
            wt = jnp.concatenate([w_in[:, :QK], wg, w_in[:, QK:2 * QK + V]], axis=1).T.astype(BF)
            qt, k, kt, vt, o, g, gt = _ml_proj(
                x, mod, i, wt, w_in[:, QK:2 * QK].astype(BF), w_in[:, 2 * QK + V:].astype(BF), wg.astype(BF),
                bg.reshape(1, -1), bg.reshape(-1, 1))
            scan_args = (qt, k, kt, vt, o, g, gt, ml_norm_g[j])
            uc, cst, nmst = _ml_scan(*scan_args)
            c0 = jnp.concatenate([jnp.swapaxes(state_C[:, j], -1, -2), state_n[:, j][..., None, :],
                                  jnp.zeros((DEC_BATCH, 2, H, DHA - DHV - 1, DQK), F32)], axis=-2)
            m0 = jnp.broadcast_to(state_m[:, j][..., None, None], (DEC_BATCH, 2, H, 1, 128))
            (ul,) = _ml_scan(*scan_args, c0, m0)
            x = _out_proj(uc, ul, ml_w_out[j].astype(BF), jnp.zeros((D,), F32), x, mod, i, ln_g[i, 0], ln_b[i, 0])
            new_c = cst[:, None]
            new_n = nmst[..., 0, :DQK][:, None]
            new_m = nmst[..., 0, DQK][:, None]
        elif kind == 1:
            x = _sconv(x, mod, i, sc_w_in[j].astype(BF), sc_conv_w[j], sc_w_out[j].astype(BF),
                       ln_g[i, 0], ln_b[i, 0])
        elif kind == 2:
            x = _pool(x, mod, i, pl_w[j].astype(BF), pl_b[j], pl_scale[j], ln_g[i, 0], ln_b[i, 0])
        else:
            fc = _fourier(x, mod, i, SEQ, BATCH, 4, 0, lambda b: 0)
            fl = _fourier(x, mod, i, DEC_SEQ, DEC_BATCH, 1, NCTX, lambda b: 1 + b)
            x = _out_proj(fc, fl, ft_w_out[j].astype(BF), ft_b_out[j], x, mod, i, ln_g[i, 0], ln_b[i, 0])
        x = _mlp(x, mod, i, mlp_w1, mlp_w2, ln_g[i, 1], ln_b[i, 1],
                 split_out=(i == DEPTH - 1))

    y_prompt = x[0].reshape(BATCH, SEQ, D)
    y_sample = x[1].reshape(DEC_BATCH, DEC_SEQ, D)
    return y_prompt, y_sample, new_c, new_n, new_m
```

```python
import functools
import math

import numpy as np
import jax
import jax.numpy as jnp
from jax import lax
from jax.experimental import pallas as pl
from jax.experimental.pallas import tpu as pltpu

D = 1024
BATCH, SEQ = 16, 256
DEC_BATCH, DEC_SEQ = 4, 1024
DEPTH = 4
GRID_W = 64
H = 8
DQK = 64
DHV = 128
QK = H * DQK
V = H * DHV
N_GROUPS = 4
GROUP_W = D // N_GROUPS
POOL_WINDOWS = (2, 4, 8, 16)
D_FF = 4 * D
ALPHA = (2.0 * DEPTH) ** 0.25
LN_EPS = 1e-5
F32 = jnp.float32
BF = jnp.bfloat16

NCTX = BATCH * SEQ
NLAT = DEC_BATCH * DEC_SEQ
NTOK = NCTX + NLAT
NCOND = 8
CHUNK = 256
VMEM_LIMIT = 56 * 1024 * 1024


def _cparams(n_axes):
    return pltpu.CompilerParams(dimension_semantics=("arbitrary",) * n_axes,
                                vmem_limit_bytes=VMEM_LIMIT)


def _mod_row(row0):
    return jnp.where(row0 < NCTX, 0, 1 + (row0 - NCTX) // DEC_SEQ)


def _seq_len(row0):
    return jnp.where(row0 < NCTX, SEQ, DEC_SEQ)


def _ln(z, g, b):
    mu = jnp.mean(z, axis=-1, keepdims=True)
    zc = z - mu
    var = jnp.mean(zc * zc, axis=-1, keepdims=True)
    return zc * lax.rsqrt(var + LN_EPS) * g + b


def _dot(a, b):
    return jnp.dot(a, b, preferred_element_type=F32)


def _dot_nt(a, b):
    return lax.dot_general(a, b, (((1,), (1,)), ((), ())), preferred_element_type=F32)


def _split2(x):
    hi = x.astype(BF)
    lo = (x - hi.astype(F32)).astype(BF)
    return hi, lo


def _split3(x):
    hi = x.astype(BF)
    r = x - hi.astype(F32)
    mid = r.astype(BF)
    lo = (r - mid.astype(F32)).astype(BF)
    return hi, mid, lo


def _full(shape):
    n = len(shape)
    return pl.BlockSpec(shape, lambda *_: (0,) * n)


def _x_specs(x, tm):
    if not isinstance(x, tuple):
        return [pl.BlockSpec((tm, D), lambda t: (t, 0))], [x]
    nc = NCTX // tm
    return ([pl.BlockSpec((tm, D), lambda t: (jnp.minimum(t, nc - 1), 0)),
             pl.BlockSpec((tm, D), lambda t: (jnp.maximum(t - nc, 0), 0)),
             pl.BlockSpec((tm, D), lambda t: (t % (DEC_SEQ // tm), 0))], list(x))


def _read_x(x_refs, tm):
    if len(x_refs) == 1:
        return x_refs[0][...]
    xc_ref, xl_ref, pe_ref = x_refs
    return jnp.where(pl.program_id(0) * tm < NCTX, xc_ref[...], xl_ref[...] + pe_ref[...])


def _mod_spec(layer, tm):
    return pl.BlockSpec((None, None, 6, D), lambda t: (layer, _mod_row(t * tm), 0, 0))


def _mod_kernel(c_ref, w_ref, b_ref, o_ref):
    c = c_ref[...]
    s = c * jax.nn.sigmoid(c)
    o_ref[...] = _dot(s.astype(BF), w_ref[...].astype(BF)) + b_ref[...]


def _modulation(cond, w_mod, b_mod):
    tn = 1536
    out = pl.pallas_call(
        _mod_kernel,
        grid=(DEPTH, 6 * D // tn),
        in_specs=[pl.BlockSpec((NCOND, D), lambda i, j: (0, 0)),
                  pl.BlockSpec((None, D, tn), lambda i, j: (i, 0, j)),
                  pl.BlockSpec((None, 1, tn), lambda i, j: (i, 0, j))],
        out_specs=pl.BlockSpec((None, NCOND, tn), lambda i, j: (i, 0, j)),
        out_shape=jax.ShapeDtypeStruct((DEPTH, NCOND, 6 * D), F32),
        compiler_params=_cparams(2),
        name="modulation",
    )(cond, w_mod, b_mod.reshape(DEPTH, 1, 6 * D))
    return out.reshape(DEPTH, NCOND, 6, D)


MLP_TM = 512
MLP_FC = 512
MLP_NF = D_FF // MLP_FC
MLP_NT = NTOK // MLP_TM


def _mlp_tile(s):
    return jnp.maximum(s - (MLP_NF - 1), 0)


def _mlp_kernel(mod_ref, w1_ref, w2_ref, g_ref, b_ref, x1_ref, *refs, nx):
    x_refs, o_refs, (w1s_ref, w2s_ref, acc_ref, h_ref) = refs[:nx], refs[nx:-4], refs[-4:]
    x_ref = x_refs[0]
    s = pl.program_id(0)
    is_ctx = _mlp_tile(s) * MLP_TM < NCTX
    m = mod_ref[...]

    def pre(x):
        return (x * (1.0 + m[4:5]) + m[3:4]).astype(BF)

    def hidden(h, w1c):
        return jnp.square(jnp.maximum(_dot(h, w1c), 0.0)).astype(BF)

    def result(x, acc):
        return _ln(ALPHA * x + m[5:6] * acc, g_ref[...], b_ref[...])

    @pl.when(s < MLP_NF)
    def _():
        @pl.when(s == 0)
        def _():
            h_ref[0:MLP_TM, :] = pre(x_ref[...])
            h_ref[MLP_TM:, :] = pre(x1_ref[...])

        w1c = w1_ref[...].astype(BF)
        w2c = w2_ref[...].astype(BF)
        part = _dot(hidden(h_ref[...], w1c), w2c)
        w1s_ref[s] = w1c
        w2s_ref[s] = w2c

        @pl.when(s == 0)
        def _():
            acc_ref[...] = part

        @pl.when(s > 0)
        def _():
            acc_ref[...] += part

        @pl.when(s == MLP_NF - 1)
        def _():
            o_refs[0][...] = result(x_ref[...], acc_ref[0:MLP_TM, :])

    @pl.when(s == MLP_NF)
    def _():
        o_refs[0][...] = result(x1_ref[...], acc_ref[MLP_TM:, :])

    @pl.when(s > MLP_NF)
    def _():
        x = x_ref[...] if nx == 1 else jnp.where(is_ctx, x_ref[...], x_refs[1][...])
        h = pre(x)
        acc = jnp.zeros((MLP_TM, D), F32)
        for f in range(MLP_NF):
            acc = acc + _dot(hidden(h, w1s_ref[f]), w2s_ref[f])
        res = result(x, acc)
        if len(o_refs) == 1:
            o_refs[0][...] = res
        else:
            @pl.when(is_ctx)
            def _():
                o_refs[0][...] = res

            @pl.when(jnp.logical_not(is_ctx))
            def _():
                o_refs[1][...] = res


def _mlp(x, mod, layer, w1, w2, ln_g, ln_b, split_out=False):
    tm = MLP_TM
    nc = NCTX // tm
    assert nc >= 2 and MLP_NT >= 2
    half_specs = [pl.BlockSpec((tm, D), lambda s: (jnp.minimum(_mlp_tile(s), nc - 1), 0)),
                  pl.BlockSpec((tm, D), lambda s: (jnp.maximum(_mlp_tile(s) - nc, 0), 0))]
    if isinstance(x, tuple):
        x_specs, x_args = half_specs, list(x)
    else:
        x_specs, x_args = [pl.BlockSpec((tm, D), lambda s: (_mlp_tile(s), 0))], [x]
    if split_out:
        out_specs = half_specs
        out_shape = [jax.ShapeDtypeStruct((NCTX, D), F32), jax.ShapeDtypeStruct((NLAT, D), F32)]
    else:
        out_specs = pl.BlockSpec((tm, D), lambda s: (_mlp_tile(s), 0))
        out_shape = jax.ShapeDtypeStruct((NTOK, D), F32)
    return pl.pallas_call(
        functools.partial(_mlp_kernel, nx=len(x_args)),
        grid=(MLP_NF - 1 + MLP_NT,),
        in_specs=[pl.BlockSpec((None, None, 6, D), lambda s: (layer, _mod_row(_mlp_tile(s) * tm), 0, 0)),
                  pl.BlockSpec((None, D, MLP_FC), lambda s: (layer, 0, jnp.minimum(s, MLP_NF - 1))),
                  pl.BlockSpec((None, MLP_FC, D), lambda s: (layer, jnp.minimum(s, MLP_NF - 1), 0)),
                  _full((1, D)), _full((1, D)),
                  pl.BlockSpec((tm, D), lambda s: (1, 0))] + x_specs,
        out_specs=out_specs,
        out_shape=out_shape,
        scratch_shapes=[pltpu.VMEM((MLP_NF, D, MLP_FC), BF),
                        pltpu.VMEM((MLP_NF, MLP_FC, D), BF),
                        pltpu.VMEM((2 * tm, D), F32),
                        pltpu.VMEM((2 * tm, D), BF)],
        compiler_params=_cparams(1),
        name="mlp",
    )(mod, w1, w2, ln_g.reshape(1, D), ln_b.reshape(1, D), x_args[0], *x_args)


OUT_TM = 1024
OUT_SUB = 256


def _out_kernel(uc_ref, ul_ref, w_ref, bias_ref, mod_ref, g_ref, b_ref, *refs):
    x_refs, o_ref = refs[:-1], refs[-1]
    is_ctx = pl.program_id(0) * OUT_TM < NCTX
    m = mod_ref[...]
    outs = []
    for r in range(0, OUT_TM, OUT_SUB):
        rs = slice(r, r + OUT_SUB)
        u = jnp.where(is_ctx, uc_ref[rs, :], ul_ref[rs, :])
        y = _dot(u, w_ref[...]) + bias_ref[...]
        if len(x_refs) == 1:
            x = x_refs[0][rs, :]
        else:
            x = jnp.where(is_ctx, x_refs[0][rs, :], x_refs[1][rs, :] + x_refs[2][rs, :])
        outs.append(_ln(ALPHA * x + m[2:3] * y, g_ref[...], b_ref[...]))
    for r, out in zip(range(0, OUT_TM, OUT_SUB), outs):
        o_ref[r:r + OUT_SUB, :] = out


def _out_proj(uc, ul, w, bias, x, mod, layer, ln_g, ln_b):
    tm = OUT_TM
    nc = NCTX // tm
    k = w.shape[0]
    x_specs, x_args = _x_specs(x, tm)
    return pl.pallas_call(
        _out_kernel,
        grid=(NTOK // tm,),
        in_specs=[pl.BlockSpec((tm, k), lambda t: (jnp.minimum(t, nc - 1), 0)),
                  pl.BlockSpec((tm, k), lambda t: (jnp.maximum(t - nc, 0), 0)),
                  pl.BlockSpec((k, D), lambda t: (0, 0), pipeline_mode=pl.Buffered(1)),
                  _full((1, D)),
                  _mod_spec(layer, tm),
                  _full((1, D)), _full((1, D))] + x_specs,
        out_specs=pl.BlockSpec((tm, D), lambda t: (t, 0)),
        out_shape=jax.ShapeDtypeStruct((NTOK, D), F32),
        compiler_params=_cparams(1),
        name="out_proj",
    )(uc, ul, w, bias.reshape(1, D), mod, ln_g.reshape(1, D), ln_b.reshape(1, D), *x_args)


ML_TM = 512
N_HP = H // 2
SCAN_PAIRS = N_HP
NG = 8


def _log_sigmoid(g):
    return jnp.minimum(g, 0.0) - jnp.log1p(jnp.exp(-jnp.abs(g)))


def _ml_proj_kernel(mod_ref, wt_ref, wk_ref, wo_ref, wg_ref, bg_ref, bgt_ref, *refs):
    x_refs = refs[:-7]
    qt_ref, k_ref, kt_ref, vt_ref, o_ref, g_ref, gt_ref = refs[-7:]
    m = mod_ref[...]
    h = (_read_x(x_refs, ML_TM) * (1.0 + m[1:2]) + m[0:1]).astype(BF)
    tr = _dot_nt(wt_ref[...], h)
    r0, r1, r2 = QK, QK + 4 * NG, 2 * QK + 4 * NG
    qt = tr[:r0].astype(BF)
    kt = (tr[r1:r2] * (DQK ** -0.5)).astype(BF)
    vt = tr[r2:].astype(BF)
    k = (_dot(h, wk_ref[...]) * (DQK ** -0.5)).astype(BF)
    o = _dot(h, wo_ref[...])
    g = _dot(h, wg_ref[...]) + bg_ref[...]
    col = lax.broadcasted_iota(jnp.int32, g.shape, 1)
    g = jnp.where((col & 2) != 0, _log_sigmoid(g), g)
    gt = tr[r0:r1] + bgt_ref[...]
    row = lax.broadcasted_iota(jnp.int32, gt.shape, 0)
    gt = jnp.where((row & 2) != 0, _log_sigmoid(gt), gt)
    qt_ref[...] = qt
    k_ref[...] = k
    kt_ref[...] = kt
    vt_ref[...] = vt
    o_ref[...] = o
    for hp in range(N_HP):
        g_ref[hp] = g[:, hp * NG:(hp + 1) * NG]
        gt_ref[hp] = gt[hp * NG:(hp + 1) * NG, :]


def _ml_proj(x, mod, layer, wt, wk, wo, wg, bg, bgt):
    tm = ML_TM
    res = lambda shape: pl.BlockSpec(shape, lambda t: (0,) * len(shape), pipeline_mode=pl.Buffered(1))
    x_specs, x_args = _x_specs(x, tm)
    return pl.pallas_call(
        _ml_proj_kernel,
        grid=(NTOK // tm,),
        in_specs=[_mod_spec(layer, tm),
                  res((2 * QK + 4 * NG + V, D)), res((D, QK)), res((D, V)),
                  res((D, 4 * NG)), res((1, 4 * NG)), res((4 * NG, 1))] + x_specs,
        out_specs=[pl.BlockSpec((QK, tm), lambda t: (0, t)),
                   pl.BlockSpec((tm, QK), lambda t: (t, 0)),
                   pl.BlockSpec((QK, tm), lambda t: (0, t)),
                   pl.BlockSpec((V, tm), lambda t: (0, t)),
                   pl.BlockSpec((tm, V), lambda t: (t, 0)),
                   pl.BlockSpec((N_HP, tm, NG), lambda t: (0, t, 0)),
                   pl.BlockSpec((N_HP, NG, tm), lambda t: (0, 0, t))],
        out_shape=[jax.ShapeDtypeStruct((QK, NTOK), BF),
                   jax.ShapeDtypeStruct((NTOK, QK), BF),
                   jax.ShapeDtypeStruct((QK, NTOK), BF),
                   jax.ShapeDtypeStruct((V, NTOK), BF),
                   jax.ShapeDtypeStruct((NTOK, V), F32),
                   jax.ShapeDtypeStruct((N_HP, NTOK, NG), F32),
                   jax.ShapeDtypeStruct((N_HP, NG, NTOK), F32)],
        compiler_params=_cparams(1),
        name="mlstm_proj",
    )(mod, wt, wk, wo, wg, bg, bgt, *x_args)


DHA = DHV + 8


def _scan_chunk(k_c, kt_c, qt_c, vta, rr_col, rr_row, b_row, m, cta, d, need_state):
    L = k_c.shape[0]
    si = lax.broadcasted_iota(jnp.int32, (L, L), 0)
    ti = lax.broadcasted_iota(jnp.int32, (L, L), 1)
    mask = (si <= ti) if d == 0 else (si >= ti)
    rrm = jnp.where(mask, rr_col, -jnp.inf)
    g = jnp.maximum(jnp.max(rrm, axis=0, keepdims=True), m)
    st = _dot(k_c, qt_c) * jnp.exp(rrm - g)
    num = _dot(vta[:DHV], st.astype(BF))
    den = jnp.sum(st, axis=0, keepdims=True)
    if cta is not None:
        a = jnp.exp(m - g)
        qc = _dot(cta.astype(BF), qt_c)
        num = num + a * qc[:DHV]
        den = den + a * qc[DHV:DHV + 1]
    mt = b_row + g
    ht = num * (1.0 / jnp.maximum(jnp.abs(den), jnp.exp(-mt)))
    if not need_state:
        return ht, None, None
    last = L - 1 if d == 0 else 0
    m_new = mt[:, last:last + 1]
    b_last = b_row[:, last:last + 1]
    w = jnp.exp(b_last + rr_row - m_new)
    kw = (kt_c.astype(F32) * w).astype(BF)
    if cta is None:
        return ht, (_dot_nt(kw, vta[:DHV]), _dot_nt(vta[DHV:], kw)), m_new
    c_new = jnp.exp(b_last + m - m_new) * cta + _dot_nt(vta, kw)
    return ht, c_new, m_new


def _scan_kernel(qt_ref, k_ref, kt_ref, vt_ref, o_ref, g_ref, gt_ref, ng_ref,
                 wo_ref, mod_ref, lg_ref, lb_ref, x_ref, *refs, is_ctx, has_pe):
    refs = list(refs)
    pe_ref = refs.pop(0) if has_pe else None
    if is_ctx:
        xo_ref, cst_ref, nmst_ref = refs
    else:
        c0_ref, m0_ref, xo_ref = refs
    L = CHUNK
    nch = DEC_SEQ // L
    ri = lax.broadcasted_iota(jnp.int32, (L, L), 0)
    ci = lax.broadcasted_iota(jnp.int32, (L, L), 1)
    lower = jnp.where(ci <= ri, 1.0, 0.0).astype(BF)
    upper = jnp.where(ci >= ri, 1.0, 0.0).astype(BF)
    sub = lax.broadcasted_iota(jnp.int32, (DHA - DHV, L), 0)
    ones_rows = jnp.where(sub == 0, 1.0, 0.0).astype(BF)

    rr_cols, rr_rows, b_rows = [], [], []
    for pp in range(SCAN_PAIRS):
        tparts = _split3(jnp.concatenate([gt_ref[pp, :, c * L:(c + 1) * L] for c in range(nch)], axis=0))
        b_row_all = (sum(_dot(p, upper) for p in tparts), sum(_dot(p, lower) for p in tparts))
        rr_cols.append([])
        rr_rows.append([])
        b_rows.append([])
        for c in range(nch):
            gc = g_ref[pp, c * L:(c + 1) * L, :]
            gtc = gt_ref[pp, :, c * L:(c + 1) * L]
            gparts = _split3(gc)
            b_col = (sum(_dot(lower, p) for p in gparts), sum(_dot(upper, p) for p in gparts))
            b_row = tuple(b[c * NG:(c + 1) * NG, :] for b in b_row_all)
            rr_cols[pp].append([gc[:, 0:NG - 2] - b[:, 2:NG] for b in b_col])
            rr_rows[pp].append([gtc[0:NG - 2, :] - b[2:NG, :] for b in b_row])
            b_rows[pp].append(b_row)

    states, us = [], [[] for _ in range(nch)]
    for hl in range(2 * SCAN_PAIRS):
        pp, hh = divmod(hl, 2)
        hsum = [None] * nch
        for d in range(2):
            ig_i, lf_i = d * 4 + hh, d * 4 + 2 + hh
            if is_ctx:
                m, cta = jnp.zeros((1, 1), F32), None
            else:
                m = m0_ref[d, hl][:, 0:1]
                cta = c0_ref[d, hl]
            order = range(nch) if d == 0 else range(nch - 1, -1, -1)
            for n, c in enumerate(order):
                sl = slice(c * L, (c + 1) * L)
                hs = slice(hl * DQK, (hl + 1) * DQK)
                vta = jnp.concatenate([vt_ref[hl * DHV:(hl + 1) * DHV, sl], ones_rows], axis=0)
                need_state = is_ctx or n < nch - 1
                hcur, c_new, m_new = _scan_chunk(
                    k_ref[sl, hs], kt_ref[hs, sl], qt_ref[hs, sl], vta,
                    rr_cols[pp][c][d][:, ig_i:ig_i + 1], rr_rows[pp][c][d][ig_i:ig_i + 1, :],
                    b_rows[pp][c][d][lf_i:lf_i + 1, :], m, cta, d, need_state)
                hsum[c] = hcur if d == 0 else hsum[c] + hcur
                if is_ctx:
                    states.append(((c, d, hl), c_new, m_new))
                else:
                    m, cta = m_new, c_new

        for c in range(nch):
            sl = slice(c * L, (c + 1) * L)
            mu = jnp.mean(hsum[c], axis=0, keepdims=True)
            hc = hsum[c] - mu
            var = jnp.mean(hc * hc, axis=0, keepdims=True)
            hn = hc * lax.rsqrt(var + LN_EPS) * ng_ref[hl * DHV:(hl + 1) * DHV, :]
            og = jax.nn.sigmoid(o_ref[sl, hl * DHV:(hl + 1) * DHV])
            us[c].append((hn.T * og).astype(BF))

    mod_gate = mod_ref[2:3, :]
    outs = []
    for c in range(nch):
        sl = slice(c * L, (c + 1) * L)
        y = _dot(jnp.concatenate(us[c], axis=1), wo_ref[...])
        x = x_ref[sl, :] if pe_ref is None else x_ref[sl, :] + pe_ref[sl, :]
        outs.append((sl, _ln(ALPHA * x + mod_gate * y, lg_ref[...], lb_ref[...])))

    for sl, out in outs:
        xo_ref[sl, :] = out
    for idx, (c_fin, n_fin), m_fin in states:
        cst_ref[idx] = c_fin
        nmst_ref[idx] = jnp.concatenate([n_fin, jnp.broadcast_to(m_fin, (8, 128 - DQK))], axis=1)


def _ml_scan(qt, k, kt, vt, o, g, gt, norm_g, w_out, mod, layer, ln_g, ln_b, x, c0aug=None, m0b=None):
    assert SCAN_PAIRS == N_HP
    is_ctx = c0aug is None
    tt = DEC_SEQ
    toff = 0 if is_ctx else NCTX // tt
    nt = (NCTX if is_ctx else NLAT) // tt
    nh = 2 * SCAN_PAIRS
    if isinstance(x, tuple):
        x_arr, x_off, pe = (x[0], 0, None) if is_ctx else (x[1], 0, x[2])
    else:
        x_arr, x_off, pe = x, toff, None
    in_specs = [pl.BlockSpec((nh * DQK, tt), lambda t, p: (p, toff + t)),
                pl.BlockSpec((tt, nh * DQK), lambda t, p: (toff + t, p)),
                pl.BlockSpec((nh * DQK, tt), lambda t, p: (p, toff + t)),
                pl.BlockSpec((nh * DHV, tt), lambda t, p: (p, toff + t)),
                pl.BlockSpec((tt, nh * DHV), lambda t, p: (toff + t, p)),
                pl.BlockSpec((SCAN_PAIRS, tt, NG), lambda t, p: (p, toff + t, 0), pipeline_mode=pl.Buffered(1)),
                pl.BlockSpec((SCAN_PAIRS, NG, tt), lambda t, p: (p, 0, toff + t)),
                pl.BlockSpec((nh * DHV, 1), lambda t, p: (p, 0)),
                pl.BlockSpec((V, D), lambda t, p: (0, 0), pipeline_mode=pl.Buffered(1)),
                pl.BlockSpec((None, None, 6, D), lambda t, p: (layer, _mod_row((toff + t) * tt), 0, 0)),
                _full((1, D)), _full((1, D)),
                pl.BlockSpec((tt, D), lambda t, p: (x_off + t, 0))]
    args = [qt, k, kt, vt, o, g, gt, norm_g.reshape(V, 1), w_out, mod,
            ln_g.reshape(1, D), ln_b.reshape(1, D), x_arr]
    if pe is not None:
        in_specs.append(pl.BlockSpec((tt, D), lambda t, p: (0, 0), pipeline_mode=pl.Buffered(1)))
        args.append(pe)
    u_spec = pl.BlockSpec((tt, D), lambda t, p: (t, 0))
    u_shape = jax.ShapeDtypeStruct((nt * tt, D), F32)
    if is_ctx:
        nseq = tt // SEQ
        out_specs = [u_spec,
                     pl.BlockSpec((nseq, 2, nh, DQK, DHV), lambda t, p: (t, 0, p, 0, 0)),
                     pl.BlockSpec((nseq, 2, nh, 8, 128), lambda t, p: (t, 0, p, 0, 0))]
        out_shape = [u_shape,
                     jax.ShapeDtypeStruct((BATCH, 2, H, DQK, DHV), F32),
                     jax.ShapeDtypeStruct((BATCH, 2, H, 8, 128), F32)]
    else:
        in_specs += [pl.BlockSpec((None, 2, nh, DHA, DQK), lambda t, p: (t, 0, p, 0, 0)),
                     pl.BlockSpec((None, 2, nh, 1, 128), lambda t, p: (t, 0, p, 0, 0))]
        args += [c0aug, m0b]
        out_specs = [u_spec]
        out_shape = [u_shape]
    return pl.pallas_call(
        functools.partial(_scan_kernel, is_ctx=is_ctx, has_pe=pe is not None),
        grid=(nt, N_HP // SCAN_PAIRS),
        in_specs=in_specs,
        out_specs=out_specs,
        out_shape=out_shape,
        compiler_params=_cparams(2),
        name="mlstm_scan_ctx" if is_ctx else "mlstm_scan_lat",
    )(*args)


SC_TM = 512


SC_HALO = 8


def _sconv_kernel(x_ref, prev_ref, next_ref, mod_ref, w_in_ref, cw_ref, w_out_ref, g_ref, b_ref, o_ref, e_ref):
    tm = SC_TM
    row0 = pl.program_id(0) * tm
    slen = _seq_len(row0)
    pos = (row0 + lax.broadcasted_iota(jnp.int32, (tm, 1), 0)) & (slen - 1)
    m = mod_ref[...]
    x = x_ref[...]
    xe = jnp.concatenate([prev_ref[...], x, next_ref[...]], axis=0)
    he = (xe * (1.0 + m[1:2]) + m[0:1]).astype(BF)
    cu = _dot(he, w_in_ref[:, D:2 * D]) * _dot(he, w_in_ref[:, 2 * D:3 * D])
    bg = _dot(he[SC_HALO:SC_HALO + tm], w_in_ref[:, 0:D])
    e_ref[...] = cu
    prev = jnp.where(pos == 0, 0.0, e_ref[SC_HALO - 1:SC_HALO - 1 + tm, :])
    nxt = jnp.where(pos == slen - 1, 0.0, e_ref[SC_HALO + 1:SC_HALO + 1 + tm, :])
    conv = cw_ref[0:1, :] * prev + cw_ref[1:2, :] * e_ref[SC_HALO:SC_HALO + tm, :] + cw_ref[2:3, :] * nxt
    y = _dot((bg * conv).astype(BF), w_out_ref[...])
    o_ref[...] = _ln(ALPHA * x + m[2:3] * y, g_ref[...], b_ref[...])


def _sconv(x, mod, layer, w_in, conv_w, w_out, ln_g, ln_b):
    tm = SC_TM
    hb = tm // SC_HALO
    nhb = NTOK // SC_HALO
    return pl.pallas_call(
        _sconv_kernel,
        grid=(NTOK // tm,),
        in_specs=[pl.BlockSpec((tm, D), lambda t: (t, 0)),
                  pl.BlockSpec((SC_HALO, D), lambda t: (jnp.maximum(t * hb - 1, 0), 0)),
                  pl.BlockSpec((SC_HALO, D), lambda t: (jnp.minimum((t + 1) * hb, nhb - 1), 0)),
                  _mod_spec(layer, tm),
                  pl.BlockSpec((D, 3 * D), lambda t: (0, 0), pipeline_mode=pl.Buffered(1)),
                  _full((3, D)),
                  pl.BlockSpec((D, D), lambda t: (0, 0), pipeline_mode=pl.Buffered(1)),
                  _full((1, D)), _full((1, D))],
        out_specs=pl.BlockSpec((tm, D), lambda t: (t, 0)),
        out_shape=jax.ShapeDtypeStruct((NTOK, D), F32),
        scratch_shapes=[pltpu.VMEM((tm + 2 * SC_HALO, D), F32)],
        compiler_params=_cparams(1),
        name="sconv",
    )(x, x, x, mod, w_in, conv_w, w_out, ln_g.reshape(1, D), ln_b.reshape(1, D))


PL_SEG = 256
PL_NSEG = 2
PL_TM = PL_SEG * PL_NSEG
PL_HALO = 16
PL_STRIDE = PL_SEG + 2 * PL_HALO


def _pool_kernel(x_ref, prev_ref, next_ref, mod_ref, w_ref, pb_ref, ps_ref, g_ref, b_ref, o_ref,
                 e_ref, s2_ref, s4_ref, s8_ref):
    tm, seg, hl, gw = PL_TM, PL_SEG, PL_HALO, GROUP_W
    row0 = pl.program_id(0) * tm
    slen = _seq_len(row0)
    m = mod_ref[...]
    x = x_ref[...]
    h = x * (1.0 + m[1:2]) + m[0:1]
    hp = prev_ref[...] * (1.0 + m[1:2]) + m[0:1]
    hn = next_ref[...] * (1.0 + m[1:2]) + m[0:1]
    n0 = PL_NSEG * PL_STRIDE
    for i in range(PL_NSEG):
        base = i * PL_STRIDE
        pos0 = (row0 + i * seg) & (slen - 1)
        before = hp if i == 0 else h[i * seg - hl:i * seg]
        after = hn if i == PL_NSEG - 1 else h[(i + 1) * seg:(i + 1) * seg + hl]
        e_ref[base:base + hl, :] = jnp.where(pos0 != 0, before, 0.0)
        e_ref[base + hl:base + hl + seg, :] = h[i * seg:(i + 1) * seg]
        e_ref[base + hl + seg:base + PL_STRIDE, :] = jnp.where(pos0 + seg != slen, after, 0.0)
    e_ref[n0:, :] = jnp.zeros((8, D), F32)
    n2, n4, n8 = n0, n0 - 8, n0 - 16
    s2_ref[...] = e_ref[0:n2, :] + e_ref[1:n2 + 1, :]
    s4_ref[...] = s2_ref[0:n4, gw:] + s2_ref[2:n4 + 2, gw:]
    s8_ref[...] = s4_ref[0:n8, gw:] + s4_ref[4:n8 + 4, gw:]

    def rows(ref, off, lanes):
        return jnp.concatenate([ref[i * PL_STRIDE + off:i * PL_STRIDE + off + seg, lanes]
                                for i in range(PL_NSEG)], axis=0)

    tots = [rows(s2_ref, hl - 1, slice(0, gw)),
            rows(s4_ref, hl - 2, slice(0, gw)),
            rows(s8_ref, hl - 4, slice(0, gw)),
            rows(s8_ref, hl - 8, slice(gw, 2 * gw)) + rows(s8_ref, hl, slice(gw, 2 * gw))]
    pos = (row0 + lax.broadcasted_iota(jnp.int32, (tm, 1), 0)) & (slen - 1)
    ys = []
    for gi, win in enumerate(POOL_WINDOWS):
        cs = slice(gi * gw, (gi + 1) * gw)
        back = win // 2
        cnt = jnp.minimum(pos + (win - back), slen) - jnp.maximum(pos - back, 0)
        p = tots[gi] / cnt.astype(F32) - h[:, cs]
        ys.append(_dot(p.astype(BF), w_ref[gi]))
    y = (jnp.concatenate(ys, axis=1) + pb_ref[...]) * ps_ref[...]
    o_ref[...] = _ln(ALPHA * x + m[2:3] * y, g_ref[...], b_ref[...])


def _pool(x, mod, layer, w, pb, ps, ln_g, ln_b):
    assert POOL_WINDOWS == (2, 4, 8, 16) and PL_HALO >= POOL_WINDOWS[-1] // 2
    assert SEQ % PL_SEG == 0 and DEC_SEQ % PL_SEG == 0
    tm = PL_TM
    hb = tm // PL_HALO
    nhb = NTOK // PL_HALO
    n0 = PL_NSEG * PL_STRIDE
    return pl.pallas_call(
        _pool_kernel,
        grid=(NTOK // tm,),
        in_specs=[pl.BlockSpec((tm, D), lambda t: (t, 0)),
                  pl.BlockSpec((PL_HALO, D), lambda t: (jnp.maximum(t * hb - 1, 0), 0)),
                  pl.BlockSpec((PL_HALO, D), lambda t: (jnp.minimum((t + 1) * hb, nhb - 1), 0)),
                  _mod_spec(layer, tm),
                  _full((N_GROUPS, GROUP_W, GROUP_W)),
                  _full((1, D)), _full((1, D)), _full((1, D)), _full((1, D))],
        out_specs=pl.BlockSpec((tm, D), lambda t: (t, 0)),
        out_shape=jax.ShapeDtypeStruct((NTOK, D), F32),
        scratch_shapes=[pltpu.VMEM((n0 + 8, D), F32),
                        pltpu.VMEM((n0, D), F32),
                        pltpu.VMEM((n0 - 8, D - GROUP_W), F32),
                        pltpu.VMEM((n0 - 16, D - 2 * GROUP_W), F32)],
        compiler_params=_cparams(1),
        name="pool",
    )(x, x, x, mod, w, pb.reshape(1, D), ps.reshape(1, D), ln_g.reshape(1, D), ln_b.reshape(1, D))


FT_TK = 256


def _dft_mats(n, scale):
    k = np.arange(n, dtype=np.int64)
    ang = 2.0 * np.pi * ((k[:, None] * k[None, :]) % n).astype(np.float64) / n
    return np.stack([np.cos(ang) * scale, np.sin(ang) * scale]).astype(np.float32)


def _dot_x3(a_parts, b_parts):
    (ah, al), (bh, bl) = a_parts, b_parts
    return _dot(ah, bh) + (_dot(ah, bl) + _dot(al, bh))


FT_PAD = 8


def _fourier_kernel(x_ref, mod_ref, cs_ref, cc_ref, o_ref, *, s, tk, nsq):
    m = mod_ref[...]
    ccp = _split2(cc_ref[0])
    scp = _split2(cc_ref[1])
    ri = lax.broadcasted_iota(jnp.int32, (tk, tk), 0)
    ci = lax.broadcasted_iota(jnp.int32, (tk, tk), 1)
    anti = jnp.where(ri + ci == tk - 1, 1.0, 0.0).astype(BF)
    blocks = []
    for sq in range(nsq):
        base = sq * s
        hparts = _split2(x_ref[base:base + s, :] * (1.0 + m[1:2]) + m[0:1])
        for j in range(s // (2 * tk)):
            ac = _dot_x3(_split2(cs_ref[0, j]), hparts)
            as_ = _dot_x3(_split2(cs_ref[1, j]), hparts)
            direct, mirror = [], []
            for gi in range(N_GROUPS):
                cs = slice(gi * GROUP_W, (gi + 1) * GROUP_W)
                p = _dot_x3(_split2(ac[:, cs]), ccp)
                q = _dot_x3(_split2(as_[:, cs]), scp)
                direct.append((p - q)[0:tk])
                mirror.append((p + q)[1:tk + 1])
            blocks.append((base + j * tk, jnp.concatenate(direct, axis=1).astype(BF)))
            mir = jnp.concatenate(mirror, axis=1).astype(BF)
            blocks.append((base + s - (j + 1) * tk, _dot(anti, mir).astype(BF)))
    for r0, blk in blocks:
        o_ref[r0:r0 + tk, :] = blk


def _half_dft_mats(s, tk):
    full = _dft_mats(s, s ** -0.5)
    rows = (np.arange(s // (2 * tk))[:, None] * tk + np.arange(tk + FT_PAD)[None, :]) % s
    return full[:, rows, :]


def _fourier(x, mod, layer, s, nseq, nsq, row_off, mod_row_fn):
    tk = min(FT_TK, s // 2)
    blk = nsq * s
    blk_off = row_off // blk
    cs = jnp.asarray(_half_dft_mats(s, tk))
    cc = jnp.asarray(_dft_mats(GROUP_W, GROUP_W ** -0.5))
    return pl.pallas_call(
        functools.partial(_fourier_kernel, s=s, tk=tk, nsq=nsq),
        grid=(nseq // nsq,),
        in_specs=[pl.BlockSpec((blk, D), lambda b: (blk_off + b, 0)),
                  pl.BlockSpec((None, None, 6, D), lambda b: (layer, mod_row_fn(b), 0, 0)),
                  pl.BlockSpec(cs.shape, lambda b: (0, 0, 0, 0), pipeline_mode=pl.Buffered(1)),
                  _full((2, GROUP_W, GROUP_W))],
        out_specs=pl.BlockSpec((blk, D), lambda b: (b, 0)),
        out_shape=jax.ShapeDtypeStruct((nseq * s, D), BF),
        compiler_params=_cparams(1),
        name="fourier_%d" % s,
    )(x, mod, cs, cc)


def _pos_embed():
    rows = DEC_SEQ // GRID_W
    quarter = D // 4
    omega = 1.0 / (10000.0 ** (np.arange(quarter, dtype=np.float64) / quarter))
    rr, cc = np.meshgrid(np.arange(rows, dtype=np.float64), np.arange(GRID_W, dtype=np.float64), indexing="ij")
    er = rr.reshape(-1, 1) * omega
    ec = cc.reshape(-1, 1) * omega
    return np.concatenate([np.sin(er), np.cos(er), np.sin(ec), np.cos(ec)], axis=-1).astype(np.float32)


def _gate_perm():
    perm = np.zeros(4 * H, dtype=np.int32)
    for hp in range(N_HP):
        for d in range(2):
            for j in range(2):
                for hh in range(2):
                    perm[hp * NG + d * 4 + j * 2 + hh] = d * 2 * H + j * H + 2 * hp + hh
    return perm


def kernel(x_prompt, x_sample, state_C, state_n, state_m, c, c_ctx, w_mod, b_mod, ln_g, ln_b, mlp_w1, mlp_w2,
           ml_w_in, ml_w_gate, ml_b_gate, ml_norm_g, ml_w_out, sc_w_in, sc_conv_w, sc_w_out,
           pl_w, pl_b, pl_scale, ft_w_out, ft_b_out):
    cond = jnp.concatenate([c_ctx[None, :], c, jnp.zeros((NCOND - 1 - DEC_BATCH, D), F32)], axis=0)
    mod = _modulation(cond, w_mod, b_mod)

    x = (x_prompt.reshape(NCTX, D), x_sample.reshape(NLAT, D), jnp.asarray(_pos_embed()))

    new_c = new_n = new_m = None
    for i in range(DEPTH):
        kind, j = i % 4, i // 4
        if kind == 0:
            w_in = ml_w_in[j]
            perm = _gate_perm()
            wg = ml_w_gate[j][:, perm]
            bg = ml_b_gate[j][perm]
            wt = jnp.concatenate([w_in[:, :QK], wg, w_in[:, QK:2 * QK + V]], axis=1).T.astype(BF)
            qt, k, kt, vt, o, g, gt = _ml_proj(
                x, mod, i, wt, w_in[:, QK:2 * QK].astype(BF), w_in[:, 2 * QK + V:].astype(BF), wg.astype(BF),
                bg.reshape(1, -1), bg.reshape(-1, 1))
            scan_args = (qt, k, kt, vt, o, g, gt, ml_norm_g[j], ml_w_out[j].astype(BF), mod, i,
                         ln_g[i, 0], ln_b[i, 0], x)
            xc, cst, nmst = _ml_scan(*scan_args)
            c0 = jnp.concatenate([jnp.swapaxes(state_C[:, j], -1, -2), state_n[:, j][..., None, :],
                                  jnp.zeros((DEC_BATCH, 2, H, DHA - DHV - 1, DQK), F32)], axis=-2)
            m0 = jnp.broadcast_to(state_m[:, j][..., None, None], (DEC_BATCH, 2, H, 1, 128))
            (xl,) = _ml_scan(*scan_args, c0, m0)
            x = (xc, xl)
            new_c = cst[:, None]
            new_n = nmst[..., 0, :DQK][:, None]
            new_m = nmst[..., 0, DQK][:, None]
        elif kind == 1:
            x = _sconv(x, mod, i, sc_w_in[j].astype(BF), sc_conv_w[j], sc_w_out[j].astype(BF),
                       ln_g[i, 0], ln_b[i, 0])
        elif kind == 2:
            x = _pool(x, mod, i, pl_w[j].astype(BF), pl_b[j], pl_scale[j], ln_g[i, 0], ln_b[i, 0])
        else:
            fc = _fourier(x, mod, i, SEQ, BATCH, 4, 0, lambda b: 0)
            fl = _fourier(x, mod, i, DEC_SEQ, DEC_BATCH, 1, NCTX, lambda b: 1 + b)
            x = _out_proj(fc, fl, ft_w_out[j].astype(BF), ft_b_out[j], x, mod, i, ln_g[i, 0], ln_b[i, 0])
        x = _mlp(x, mod, i, mlp_w1, mlp_w2, ln_g[i, 1], ln_b[i, 1],
                 split_out=(i == DEPTH - 1))

    y_prompt = x[0].reshape(BATCH, SEQ, D)
    y_sample = x[1].reshape(DEC_BATCH, DEC_SEQ, D)
    return y_prompt, y_sample, new_c, new_n, new_m
```

```python
import functools

import numpy as np
import jax
import jax.numpy as jnp
from jax import lax
from jax.experimental import pallas as pl
from jax.experimental.pallas import tpu as pltpu

D = 1024
BATCH, SEQ = 16, 256
DEC_BATCH, DEC_SEQ = 4, 1024
DEPTH = 4
GRID_W = 64
H = 8
DQK = 64
DHV = 128
QK = H * DQK
V = H * DHV
N_GROUPS = 4
GROUP_W = D // N_GROUPS
POOL_WINDOWS = (2, 4, 8, 16)
D_FF = 4 * D
ALPHA = (2.0 * DEPTH) ** 0.25
LN_EPS = 1e-5
F32 = jnp.float32
BF = jnp.bfloat16

NCTX = BATCH * SEQ
NLAT = DEC_BATCH * DEC_SEQ
NTOK = NCTX + NLAT
NCOND = 8
CHUNK = 256
SUBLANES, LANES = 8, 128
V7X_VMEM_BYTES = 64 * 1024 * 1024
VMEM_LIMIT = V7X_VMEM_BYTES - 8 * 1024 * 1024


def _cparams(n_axes):
    return pltpu.CompilerParams(dimension_semantics=("arbitrary",) * n_axes,
                                vmem_limit_bytes=VMEM_LIMIT)


def _mod_row(row0):
    return jnp.where(row0 < NCTX, 0, 1 + (row0 - NCTX) // DEC_SEQ)


def _seq_len(row0):
    return jnp.where(row0 < NCTX, SEQ, DEC_SEQ)


def _ln(z, g, b):
    mu = jnp.mean(z, axis=-1, keepdims=True)
    zc = z - mu
    var = jnp.mean(zc * zc, axis=-1, keepdims=True)
    return zc * lax.rsqrt(var + LN_EPS) * g + b


def _dot(a, b):
    return jnp.dot(a, b, preferred_element_type=F32)


def _dot_nt(a, b):
    return lax.dot_general(a, b, (((1,), (1,)), ((), ())), preferred_element_type=F32)


def _split2(x):
    hi = x.astype(BF)
    lo = (x - hi.astype(F32)).astype(BF)
    return hi, lo


def _split3(x):
    hi = x.astype(BF)
    r = x - hi.astype(F32)
    mid = r.astype(BF)
    lo = (r - mid.astype(F32)).astype(BF)
    return hi, mid, lo


def _full(shape):
    n = len(shape)
    return pl.BlockSpec(shape, lambda *_: (0,) * n)


def _x_specs(x, tm):
    if not isinstance(x, tuple):
        return [pl.BlockSpec((tm, D), lambda t: (t, 0))], [x]
    nc = NCTX // tm
    return ([pl.BlockSpec((tm, D), lambda t: (jnp.minimum(t, nc - 1), 0)),
             pl.BlockSpec((tm, D), lambda t: (jnp.maximum(t - nc, 0), 0)),
             pl.BlockSpec((tm, D), lambda t: (t % (DEC_SEQ // tm), 0))], list(x))


def _read_x(x_refs, tm):
    if len(x_refs) == 1:
        return x_refs[0][...]
    xc_ref, xl_ref, pe_ref = x_refs
    return jnp.where(pl.program_id(0) * tm < NCTX, xc_ref[...], xl_ref[...] + pe_ref[...])


def _mod_spec(layer, tm):
    return pl.BlockSpec((None, None, 6, D), lambda t: (layer, _mod_row(t * tm), 0, 0))


def _mod_kernel(c_ref, w_ref, b_ref, o_ref):
    c = c_ref[...]
    s = c * jax.nn.sigmoid(c)
    o_ref[...] = _dot(s.astype(BF), w_ref[...].astype(BF)) + b_ref[...]


def _modulation(cond, w_mod, b_mod):
    tn = 1536
    out = pl.pallas_call(
        _mod_kernel,
        grid=(DEPTH, 6 * D // tn),
        in_specs=[pl.BlockSpec((NCOND, D), lambda i, j: (0, 0)),
                  pl.BlockSpec((None, D, tn), lambda i, j: (i, 0, j)),
                  pl.BlockSpec((None, 1, tn), lambda i, j: (i, 0, j))],
        out_specs=pl.BlockSpec((None, NCOND, tn), lambda i, j: (i, 0, j)),
        out_shape=jax.ShapeDtypeStruct((DEPTH, NCOND, 6 * D), F32),
        compiler_params=_cparams(2),
        name="modulation",
    )(cond, w_mod, b_mod.reshape(DEPTH, 1, 6 * D))
    return out.reshape(DEPTH, NCOND, 6, D)


MLP_TM = 512
MLP_FC = 512
MLP_NF = D_FF // MLP_FC
MLP_NT = NTOK // MLP_TM


def _mlp_tile(s):
    return jnp.maximum(s - (MLP_NF - 1), 0)


def _mlp_kernel(mod_ref, w1_ref, w2_ref, g_ref, b_ref, x1_ref, x_ref, *refs):
    o_refs, (w1s_ref, w2s_ref, acc_ref, h_ref) = refs[:-4], refs[-4:]
    s = pl.program_id(0)
    m = mod_ref[...]

    def pre(x):
        return (x * (1.0 + m[4:5]) + m[3:4]).astype(BF)

    def hidden(h, w1c):
        return jnp.square(jnp.maximum(_dot(h, w1c), 0.0)).astype(BF)

    def result(x, acc):
        return _ln(ALPHA * x + m[5:6] * acc, g_ref[...], b_ref[...])

    @pl.when(s < MLP_NF)
    def _():
        @pl.when(s == 0)
        def _():
            h_ref[0:MLP_TM, :] = pre(x_ref[...])
            h_ref[MLP_TM:, :] = pre(x1_ref[...])

        w1c = w1_ref[...].astype(BF)
        w2c = w2_ref[...].astype(BF)
        part = _dot(hidden(h_ref[...], w1c), w2c)
        w1s_ref[s] = w1c
        w2s_ref[s] = w2c

        @pl.when(s == 0)
        def _():
            acc_ref[...] = part

        @pl.when(s > 0)
        def _():
            acc_ref[...] += part

        @pl.when(s == MLP_NF - 1)
        def _():
            o_refs[0][...] = result(x_ref[...], acc_ref[0:MLP_TM, :])

    @pl.when(s == MLP_NF)
    def _():
        o_refs[0][...] = result(x1_ref[...], acc_ref[MLP_TM:, :])

    @pl.when(s > MLP_NF)
    def _():
        x = x_ref[...]
        h = pre(x)
        acc = jnp.zeros((MLP_TM, D), F32)
        for f in range(MLP_NF):
            acc = acc + _dot(hidden(h, w1s_ref[f]), w2s_ref[f])
        res = result(x, acc)
        if len(o_refs) == 1:
            o_refs[0][...] = res
        else:
            is_ctx = _mlp_tile(s) * MLP_TM < NCTX

            @pl.when(is_ctx)
            def _():
                o_refs[0][...] = res

            @pl.when(jnp.logical_not(is_ctx))
            def _():
                o_refs[1][...] = res


def _mlp(x, mod, layer, w1, w2, ln_g, ln_b, split_out=False):
    tm = MLP_TM
    nc = NCTX // tm
    assert nc >= 2 and MLP_NT >= 2
    if split_out:
        out_specs = [pl.BlockSpec((tm, D), lambda s: (jnp.minimum(_mlp_tile(s), nc - 1), 0)),
                     pl.BlockSpec((tm, D), lambda s: (jnp.maximum(_mlp_tile(s) - nc, 0), 0))]
        out_shape = [jax.ShapeDtypeStruct((NCTX, D), F32), jax.ShapeDtypeStruct((NLAT, D), F32)]
    else:
        out_specs = pl.BlockSpec((tm, D), lambda s: (_mlp_tile(s), 0))
        out_shape = jax.ShapeDtypeStruct((NTOK, D), F32)
    return pl.pallas_call(
        _mlp_kernel,
        grid=(MLP_NF - 1 + MLP_NT,),
        in_specs=[pl.BlockSpec((None, None, 6, D), lambda s: (layer, _mod_row(_mlp_tile(s) * tm), 0, 0)),
                  pl.BlockSpec((None, D, MLP_FC), lambda s: (layer, 0, jnp.minimum(s, MLP_NF - 1))),
                  pl.BlockSpec((None, MLP_FC, D), lambda s: (layer, jnp.minimum(s, MLP_NF - 1), 0)),
                  _full((1, D)), _full((1, D)),
                  pl.BlockSpec((tm, D), lambda s: (1, 0)),
                  pl.BlockSpec((tm, D), lambda s: (_mlp_tile(s), 0))],
        out_specs=out_specs,
        out_shape=out_shape,
        scratch_shapes=[pltpu.VMEM((MLP_NF, D, MLP_FC), BF),
                        pltpu.VMEM((MLP_NF, MLP_FC, D), BF),
                        pltpu.VMEM((2 * tm, D), F32),
                        pltpu.VMEM((2 * tm, D), BF)],
        compiler_params=_cparams(1),
        name="mlp",
    )(mod, w1, w2, ln_g.reshape(1, D), ln_b.reshape(1, D), x, x)


OUT_TM = 1024
OUT_SUB = 256


def _out_kernel(uc_ref, ul_ref, w_f32_ref, bias_ref, mod_ref, g_ref, b_ref, *refs):
    x_refs, o_ref, w_ref = refs[:-2], refs[-2], refs[-1]

    @pl.when(pl.program_id(0) == 0)
    def _():
        w_ref[...] = w_f32_ref[...].astype(BF)

    is_ctx = pl.program_id(0) * OUT_TM < NCTX
    m = mod_ref[...]
    outs = []
    for r in range(0, OUT_TM, OUT_SUB):
        rs = slice(r, r + OUT_SUB)
        u = jnp.where(is_ctx, uc_ref[rs, :], ul_ref[rs, :])
        y = _dot(u, w_ref[...]) + bias_ref[...]
        if len(x_refs) == 1:
            x = x_refs[0][rs, :]
        else:
            x = jnp.where(is_ctx, x_refs[0][rs, :], x_refs[1][rs, :] + x_refs[2][rs, :])
        outs.append(_ln(ALPHA * x + m[2:3] * y, g_ref[...], b_ref[...]))
    for r, out in zip(range(0, OUT_TM, OUT_SUB), outs):
        o_ref[r:r + OUT_SUB, :] = out


def _out_proj(uc, ul, w, bias, x, mod, layer, ln_g, ln_b):
    tm = OUT_TM
    nc = NCTX // tm
    k = w.shape[0]
    x_specs, x_args = _x_specs(x, tm)
    return pl.pallas_call(
        _out_kernel,
        grid=(NTOK // tm,),
        in_specs=[pl.BlockSpec((tm, k), lambda t: (jnp.minimum(t, nc - 1), 0)),
                  pl.BlockSpec((tm, k), lambda t: (jnp.maximum(t - nc, 0), 0)),
                  pl.BlockSpec((k, D), lambda t: (0, 0), pipeline_mode=pl.Buffered(1)),
                  _full((1, D)),
                  _mod_spec(layer, tm),
                  _full((1, D)), _full((1, D))] + x_specs,
        out_specs=pl.BlockSpec((tm, D), lambda t: (t, 0)),
        out_shape=jax.ShapeDtypeStruct((NTOK, D), F32),
        scratch_shapes=[pltpu.VMEM((k, D), BF)],
        compiler_params=_cparams(1),
        name="out_proj",
    )(uc, ul, w, bias.reshape(1, D), mod, ln_g.reshape(1, D), ln_b.reshape(1, D), *x_args)


ML_TM = 512
N_HP = H // 2
SCAN_PAIRS = 2
NG = 8


def _log_sigmoid(g):
    return jnp.minimum(g, 0.0) - jnp.log1p(jnp.exp(-jnp.abs(g)))


def _ml_proj_kernel(mod_ref, w_in_ref, wg_ref, wgt_ref, bg_ref, bgt_ref, *refs):
    x_refs = refs[:-10]
    qt_ref, k_ref, kt_ref, vt_ref, o_ref, g_ref, gt_ref = refs[-10:-3]
    wt_ref, wk_ref, wo_ref = refs[-3:]
    r0, r1, r2 = QK, QK + 4 * NG, 2 * QK + 4 * NG

    @pl.when(pl.program_id(0) == 0)
    def _():
        wt_ref[0:r0, :] = w_in_ref[:, 0:QK].T.astype(BF)
        wt_ref[r0:r1, :] = wgt_ref[...]
        wt_ref[r1:r2, :] = w_in_ref[:, QK:2 * QK].T.astype(BF)
        wt_ref[r2:, :] = w_in_ref[:, 2 * QK:2 * QK + V].T.astype(BF)
        wk_ref[...] = w_in_ref[:, QK:2 * QK].astype(BF)
        wo_ref[...] = w_in_ref[:, 2 * QK + V:].astype(BF)

    m = mod_ref[...]
    h = (_read_x(x_refs, ML_TM) * (1.0 + m[1:2]) + m[0:1]).astype(BF)
    tr = _dot_nt(wt_ref[...], h)
    qt = tr[:r0].astype(BF)
    kt = (tr[r1:r2] * (DQK ** -0.5)).astype(BF)
    vt = tr[r2:].astype(BF)
    k = (_dot(h, wk_ref[...]) * (DQK ** -0.5)).astype(BF)
    o = _dot(h, wo_ref[...])
    g = _dot(h, wg_ref[...]) + bg_ref[...]
    col = lax.broadcasted_iota(jnp.int32, g.shape, 1)
    g = jnp.where((col & 2) != 0, _log_sigmoid(g), g)
    gt = tr[r0:r1] + bgt_ref[...]
    row = lax.broadcasted_iota(jnp.int32, gt.shape, 0)
    gt = jnp.where((row & 2) != 0, _log_sigmoid(gt), gt)
    qt_ref[...] = qt
    k_ref[...] = k
    kt_ref[...] = kt
    vt_ref[...] = vt
    o_ref[...] = o
    for hp in range(N_HP):
        g_ref[hp] = g[:, hp * NG:(hp + 1) * NG]
        gt_ref[hp] = gt[hp * NG:(hp + 1) * NG, :]


def _ml_proj(x, mod, layer, w_in, wg, wgt, bg, bgt):
    tm = ML_TM
    res = lambda shape: pl.BlockSpec(shape, lambda t: (0,) * len(shape), pipeline_mode=pl.Buffered(1))
    x_specs, x_args = _x_specs(x, tm)
    return pl.pallas_call(
        _ml_proj_kernel,
        grid=(NTOK // tm,),
        in_specs=[_mod_spec(layer, tm),
                  res((D, 2 * QK + 2 * V)), res((D, 4 * NG)), res((4 * NG, D)),
                  res((1, 4 * NG)), res((4 * NG, 1))] + x_specs,
        out_specs=[pl.BlockSpec((QK, tm), lambda t: (0, t)),
                   pl.BlockSpec((tm, QK), lambda t: (t, 0)),
                   pl.BlockSpec((QK, tm), lambda t: (0, t)),
                   pl.BlockSpec((V, tm), lambda t: (0, t)),
                   pl.BlockSpec((tm, V), lambda t: (t, 0)),
                   pl.BlockSpec((N_HP, tm, NG), lambda t: (0, t, 0)),
                   pl.BlockSpec((N_HP, NG, tm), lambda t: (0, 0, t))],
        out_shape=[jax.ShapeDtypeStruct((QK, NTOK), BF),
                   jax.ShapeDtypeStruct((NTOK, QK), BF),
                   jax.ShapeDtypeStruct((QK, NTOK), BF),
                   jax.ShapeDtypeStruct((V, NTOK), BF),
                   jax.ShapeDtypeStruct((NTOK, V), F32),
                   jax.ShapeDtypeStruct((N_HP, NTOK, NG), F32),
                   jax.ShapeDtypeStruct((N_HP, NG, NTOK), F32)],
        scratch_shapes=[pltpu.VMEM((2 * QK + 4 * NG + V, D), BF),
                        pltpu.VMEM((D, QK), BF),
                        pltpu.VMEM((D, V), BF)],
        compiler_params=_cparams(1),
        name="mlstm_proj",
    )(mod, w_in, wg, wgt, bg, bgt, *x_args)


DHA = DHV + SUBLANES


def _scan_chunk(k_c, kt_c, qt_c, vta, rr_col, rr_row, b_row, m, cta, d, need_state):
    L = k_c.shape[0]
    si = lax.broadcasted_iota(jnp.int32, (L, L), 0)
    ti = lax.broadcasted_iota(jnp.int32, (L, L), 1)
    mask = (si <= ti) if d == 0 else (si >= ti)
    rrm = jnp.where(mask, rr_col, -jnp.inf)
    g = jnp.maximum(jnp.max(rrm, axis=0, keepdims=True), m)
    st = _dot(k_c, qt_c) * jnp.exp(rrm - g)
    num = _dot(vta[:DHV], st.astype(BF))
    den = jnp.sum(st, axis=0, keepdims=True)
    if cta is not None:
        a = jnp.exp(m - g)
        qc = _dot(cta.astype(BF), qt_c)
        num = num + a * qc[:DHV]
        den = den + a * qc[DHV:DHV + 1]
    mt = b_row + g
    ht = num * (1.0 / jnp.maximum(jnp.abs(den), jnp.exp(-mt)))
    if not need_state:
        return ht, None, None
    last = L - 1 if d == 0 else 0
    m_new = mt[:, last:last + 1]
    b_last = b_row[:, last:last + 1]
    w = jnp.exp(b_last + rr_row - m_new)
    kw = (kt_c.astype(F32) * w).astype(BF)
    if cta is None:
        return ht, (_dot_nt(kw, vta[:DHV]), _dot_nt(vta[DHV:], kw)), m_new
    c_new = jnp.exp(b_last + m - m_new) * cta + _dot_nt(vta, kw)
    return ht, c_new, m_new


def _scan_kernel(*refs, is_ctx):
    if is_ctx:
        qt_ref, k_ref, kt_ref, vt_ref, o_ref, g_ref, gt_ref, ng_ref, u_ref, cst_ref, nmst_ref = refs
    else:
        qt_ref, k_ref, kt_ref, vt_ref, o_ref, g_ref, gt_ref, ng_ref, c0_ref, m0_ref, u_ref = refs
    L = CHUNK
    nch = DEC_SEQ // L
    ri = lax.broadcasted_iota(jnp.int32, (L, L), 0)
    ci = lax.broadcasted_iota(jnp.int32, (L, L), 1)
    lower = jnp.where(ci <= ri, 1.0, 0.0).astype(BF)
    upper = jnp.where(ci >= ri, 1.0, 0.0).astype(BF)
    sub = lax.broadcasted_iota(jnp.int32, (DHA - DHV, L), 0)
    ones_rows = jnp.where(sub == 0, 1.0, 0.0).astype(BF)

    rr_cols, rr_rows, b_rows = [], [], []
    for pp in range(SCAN_PAIRS):
        tparts = _split3(jnp.concatenate([gt_ref[pp, :, c * L:(c + 1) * L] for c in range(nch)], axis=0))
        b_row_all = (sum(_dot(p, upper) for p in tparts), sum(_dot(p, lower) for p in tparts))
        rr_cols.append([])
        rr_rows.append([])
        b_rows.append([])
        for c in range(nch):
            gc = g_ref[pp, c * L:(c + 1) * L, :]
            gtc = gt_ref[pp, :, c * L:(c + 1) * L]
            gparts = _split3(gc)
            b_col = (sum(_dot(lower, p) for p in gparts), sum(_dot(upper, p) for p in gparts))
            b_row = tuple(b[c * NG:(c + 1) * NG, :] for b in b_row_all)
            rr_cols[pp].append([gc[:, 0:NG - 2] - b[:, 2:NG] for b in b_col])
            rr_rows[pp].append([gtc[0:NG - 2, :] - b[2:NG, :] for b in b_row])
            b_rows[pp].append(b_row)

    states, us = [], []
    for hl in range(2 * SCAN_PAIRS):
        pp, hh = divmod(hl, 2)
        hsum = [None] * nch
        for d in range(2):
            ig_i, lf_i = d * 4 + hh, d * 4 + 2 + hh
            if is_ctx:
                m, cta = jnp.zeros((1, 1), F32), None
            else:
                m = m0_ref[d, hl][:, 0:1]
                cta = c0_ref[d, hl]
            order = range(nch) if d == 0 else range(nch - 1, -1, -1)
            for n, c in enumerate(order):
                sl = slice(c * L, (c + 1) * L)
                hs = slice(hl * DQK, (hl + 1) * DQK)
                vta = jnp.concatenate([vt_ref[hl * DHV:(hl + 1) * DHV, sl], ones_rows], axis=0)
                need_state = is_ctx or n < nch - 1
                hcur, c_new, m_new = _scan_chunk(
                    k_ref[sl, hs], kt_ref[hs, sl], qt_ref[hs, sl], vta,
                    rr_cols[pp][c][d][:, ig_i:ig_i + 1], rr_rows[pp][c][d][ig_i:ig_i + 1, :],
                    b_rows[pp][c][d][lf_i:lf_i + 1, :], m, cta, d, need_state)
                hsum[c] = hcur if d == 0 else hsum[c] + hcur
                if is_ctx:
                    states.append(((c, d, hl), c_new, m_new))
                else:
                    m, cta = m_new, c_new

        for c in range(nch):
            sl = slice(c * L, (c + 1) * L)
            mu = jnp.mean(hsum[c], axis=0, keepdims=True)
            hc = hsum[c] - mu
            var = jnp.mean(hc * hc, axis=0, keepdims=True)
            hn = hc * lax.rsqrt(var + LN_EPS) * ng_ref[hl * DHV:(hl + 1) * DHV, :]
            og = jax.nn.sigmoid(o_ref[sl, hl * DHV:(hl + 1) * DHV])
            us.append(((sl, slice(hl * DHV, (hl + 1) * DHV)), (hn.T * og).astype(BF)))

    for idx, u in us:
        u_ref[idx] = u
    for idx, (c_fin, n_fin), m_fin in states:
        cst_ref[idx] = c_fin
        nmst_ref[idx] = jnp.concatenate([n_fin, jnp.broadcast_to(m_fin, (SUBLANES, LANES - DQK))], axis=1)


def _ml_scan(qt, k, kt, vt, o, g, gt, norm_g, c0aug=None, m0b=None):
    is_ctx = c0aug is None
    tt = DEC_SEQ
    toff = 0 if is_ctx else NCTX // tt
    nt = (NCTX if is_ctx else NLAT) // tt
    nh = 2 * SCAN_PAIRS
    in_specs = [pl.BlockSpec((nh * DQK, tt), lambda t, p: (p, toff + t)),
                pl.BlockSpec((tt, nh * DQK), lambda t, p: (toff + t, p)),
                pl.BlockSpec((nh * DQK, tt), lambda t, p: (p, toff + t)),
                pl.BlockSpec((nh * DHV, tt), lambda t, p: (p, toff + t)),
                pl.BlockSpec((tt, nh * DHV), lambda t, p: (toff + t, p)),
                pl.BlockSpec((SCAN_PAIRS, tt, NG), lambda t, p: (p, toff + t, 0)),
                pl.BlockSpec((SCAN_PAIRS, NG, tt), lambda t, p: (p, 0, toff + t)),
                pl.BlockSpec((nh * DHV, 1), lambda t, p: (p, 0))]
    args = [qt, k, kt, vt, o, g, gt, norm_g.reshape(V, 1)]
    u_spec = pl.BlockSpec((tt, nh * DHV), lambda t, p: (t, p))
    u_shape = jax.ShapeDtypeStruct((nt * tt, V), BF)
    if is_ctx:
        nseq = tt // SEQ
        out_specs = [u_spec,
                     pl.BlockSpec((nseq, 2, nh, DQK, DHV), lambda t, p: (t, 0, p, 0, 0)),
                     pl.BlockSpec((nseq, 2, nh, SUBLANES, LANES), lambda t, p: (t, 0, p, 0, 0))]
        out_shape = [u_shape,
                     jax.ShapeDtypeStruct((BATCH, 2, H, DQK, DHV), F32),
                     jax.ShapeDtypeStruct((BATCH, 2, H, SUBLANES, LANES), F32)]
    else:
        in_specs += [pl.BlockSpec((None, 2, nh, DHA, DQK), lambda t, p: (t, 0, p, 0, 0)),
                     pl.BlockSpec((None, 2, nh, 1, LANES), lambda t, p: (t, 0, p, 0, 0))]
        args += [c0aug, m0b]
        out_specs = [u_spec]
        out_shape = [u_shape]
    return pl.pallas_call(
        functools.partial(_scan_kernel, is_ctx=is_ctx),
        grid=(nt, N_HP // SCAN_PAIRS),
        in_specs=in_specs,
        out_specs=out_specs,
        out_shape=out_shape,
        compiler_params=_cparams(2),
        name="mlstm_scan_ctx" if is_ctx else "mlstm_scan_lat",
    )(*args)


SC_TM = 512
SC_HALO = SUBLANES


def _sconv_kernel(x_ref, prev_ref, next_ref, mod_ref, w_in_f32_ref, cw_ref, w_out_f32_ref, g_ref, b_ref, o_ref,
                  e_ref, w_in_ref, w_out_ref):
    tm = SC_TM

    @pl.when(pl.program_id(0) == 0)
    def _():
        w_in_ref[...] = w_in_f32_ref[...].astype(BF)
        w_out_ref[...] = w_out_f32_ref[...].astype(BF)

    row0 = pl.program_id(0) * tm
    slen = _seq_len(row0)
    pos = (row0 + lax.broadcasted_iota(jnp.int32, (tm, 1), 0)) & (slen - 1)
    m = mod_ref[...]
    x = x_ref[...]
    xe = jnp.concatenate([prev_ref[...], x, next_ref[...]], axis=0)
    he = (xe * (1.0 + m[1:2]) + m[0:1]).astype(BF)
    cu = _dot(he, w_in_ref[:, D:2 * D]) * _dot(he, w_in_ref[:, 2 * D:3 * D])
    bg = _dot(he[SC_HALO:SC_HALO + tm], w_in_ref[:, 0:D])
    e_ref[...] = cu
    prev = jnp.where(pos == 0, 0.0, e_ref[SC_HALO - 1:SC_HALO - 1 + tm, :])
    nxt = jnp.where(pos == slen - 1, 0.0, e_ref[SC_HALO + 1:SC_HALO + 1 + tm, :])
    conv = cw_ref[0:1, :] * prev + cw_ref[1:2, :] * e_ref[SC_HALO:SC_HALO + tm, :] + cw_ref[2:3, :] * nxt
    y = _dot((bg * conv).astype(BF), w_out_ref[...])
    o_ref[...] = _ln(ALPHA * x + m[2:3] * y, g_ref[...], b_ref[...])


def _sconv(x, mod, layer, w_in, conv_w, w_out, ln_g, ln_b):
    tm = SC_TM
    hb = tm // SC_HALO
    nhb = NTOK // SC_HALO
    return pl.pallas_call(
        _sconv_kernel,
        grid=(NTOK // tm,),
        in_specs=[pl.BlockSpec((tm, D), lambda t: (t, 0)),
                  pl.BlockSpec((SC_HALO, D), lambda t: (jnp.maximum(t * hb - 1, 0), 0)),
                  pl.BlockSpec((SC_HALO, D), lambda t: (jnp.minimum((t + 1) * hb, nhb - 1), 0)),
                  _mod_spec(layer, tm),
                  pl.BlockSpec((D, 3 * D), lambda t: (0, 0), pipeline_mode=pl.Buffered(1)),
                  _full((3, D)),
                  pl.BlockSpec((D, D), lambda t: (0, 0), pipeline_mode=pl.Buffered(1)),
                  _full((1, D)), _full((1, D))],
        out_specs=pl.BlockSpec((tm, D), lambda t: (t, 0)),
        out_shape=jax.ShapeDtypeStruct((NTOK, D), F32),
        scratch_shapes=[pltpu.VMEM((tm + 2 * SC_HALO, D), F32),
                        pltpu.VMEM((D, 3 * D), BF),
                        pltpu.VMEM((D, D), BF)],
        compiler_params=_cparams(1),
        name="sconv",
    )(x, x, x, mod, w_in, conv_w, w_out, ln_g.reshape(1, D), ln_b.reshape(1, D))


PL_SEG = 256
PL_NSEG = 2
PL_TM = PL_SEG * PL_NSEG
PL_HALO = 16
PL_STRIDE = PL_SEG + 2 * PL_HALO


def _pool_kernel(x_ref, prev_ref, next_ref, mod_ref, w_ref, pb_ref, ps_ref, g_ref, b_ref, o_ref,
                 e_ref, s2_ref, s4_ref, s8_ref):
    tm, seg, hl, gw = PL_TM, PL_SEG, PL_HALO, GROUP_W
    row0 = pl.program_id(0) * tm
    slen = _seq_len(row0)
    m = mod_ref[...]
    x = x_ref[...]
    h = x * (1.0 + m[1:2]) + m[0:1]
    hp = prev_ref[...] * (1.0 + m[1:2]) + m[0:1]
    hn = next_ref[...] * (1.0 + m[1:2]) + m[0:1]
    n0 = PL_NSEG * PL_STRIDE
    for i in range(PL_NSEG):
        base = i * PL_STRIDE
        pos0 = (row0 + i * seg) & (slen - 1)
        before = hp if i == 0 else h[i * seg - hl:i * seg]
        after = hn if i == PL_NSEG - 1 else h[(i + 1) * seg:(i + 1) * seg + hl]
        e_ref[base:base + hl, :] = jnp.where(pos0 != 0, before, 0.0)
        e_ref[base + hl:base + hl + seg, :] = h[i * seg:(i + 1) * seg]
        e_ref[base + hl + seg:base + PL_STRIDE, :] = jnp.where(pos0 + seg != slen, after, 0.0)
    e_ref[n0:, :] = jnp.zeros((SUBLANES, D), F32)
    n2, n4, n8 = n0, n0 - SUBLANES, n0 - 2 * SUBLANES
    s2_ref[...] = e_ref[0:n2, :] + e_ref[1:n2 + 1, :]
    s4_ref[...] = s2_ref[0:n4, gw:] + s2_ref[2:n4 + 2, gw:]
    s8_ref[...] = s4_ref[0:n8, gw:] + s4_ref[4:n8 + 4, gw:]

    def rows(ref, off, lanes):
        return jnp.concatenate([ref[i * PL_STRIDE + off:i * PL_STRIDE + off + seg, lanes]
                                for i in range(PL_NSEG)], axis=0)

    tots = [rows(s2_ref, hl - 1, slice(0, gw)),
            rows(s4_ref, hl - 2, slice(0, gw)),
            rows(s8_ref, hl - 4, slice(0, gw)),
            rows(s8_ref, hl - 8, slice(gw, 2 * gw)) + rows(s8_ref, hl, slice(gw, 2 * gw))]
    pos = (row0 + lax.broadcasted_iota(jnp.int32, (tm, 1), 0)) & (slen - 1)
    ys = []
    for gi, win in enumerate(POOL_WINDOWS):
        cs = slice(gi * gw, (gi + 1) * gw)
        back = win // 2
        cnt = jnp.minimum(pos + (win - back), slen) - jnp.maximum(pos - back, 0)
        p = tots[gi] / cnt.astype(F32) - h[:, cs]
        ys.append(_dot(p.astype(BF), w_ref[gi]))
    y = (jnp.concatenate(ys, axis=1) + pb_ref[...]) * ps_ref[...]
    o_ref[...] = _ln(ALPHA * x + m[2:3] * y, g_ref[...], b_ref[...])


def _pool(x, mod, layer, w, pb, ps, ln_g, ln_b):
    assert POOL_WINDOWS == (2, 4, 8, 16) and PL_HALO >= POOL_WINDOWS[-1] // 2
    assert SEQ % PL_SEG == 0 and DEC_SEQ % PL_SEG == 0
    tm = PL_TM
    hb = tm // PL_HALO
    nhb = NTOK // PL_HALO
    n0 = PL_NSEG * PL_STRIDE
    return pl.pallas_call(
        _pool_kernel,
        grid=(NTOK // tm,),
        in_specs=[pl.BlockSpec((tm, D), lambda t: (t, 0)),
                  pl.BlockSpec((PL_HALO, D), lambda t: (jnp.maximum(t * hb - 1, 0), 0)),
                  pl.BlockSpec((PL_HALO, D), lambda t: (jnp.minimum((t + 1) * hb, nhb - 1), 0)),
                  _mod_spec(layer, tm),
                  _full((N_GROUPS, GROUP_W, GROUP_W)),
                  _full((1, D)), _full((1, D)), _full((1, D)), _full((1, D))],
        out_specs=pl.BlockSpec((tm, D), lambda t: (t, 0)),
        out_shape=jax.ShapeDtypeStruct((NTOK, D), F32),
        scratch_shapes=[pltpu.VMEM((n0 + SUBLANES, D), F32),
                        pltpu.VMEM((n0, D), F32),
                        pltpu.VMEM((n0 - SUBLANES, D - GROUP_W), F32),
                        pltpu.VMEM((n0 - 2 * SUBLANES, D - 2 * GROUP_W), F32)],
        compiler_params=_cparams(1),
        name="pool",
    )(x, x, x, mod, w, pb.reshape(1, D), ps.reshape(1, D), ln_g.reshape(1, D), ln_b.reshape(1, D))


FT_TK = 256
FT_PAD = SUBLANES


def _dft_mats(n, scale):
    k = np.arange(n, dtype=np.int64)
    ang = 2.0 * np.pi * ((k[:, None] * k[None, :]) % n).astype(np.float64) / n
    return np.stack([np.cos(ang) * scale, np.sin(ang) * scale]).astype(np.float32)


def _dot_x3(a_parts, b_parts):
    (ah, al), (bh, bl) = a_parts, b_parts
    return _dot(ah, bh) + (_dot(ah, bl) + _dot(al, bh))


def _fourier_kernel(x_ref, mod_ref, cs_ref, cc_ref, o_ref, *, s, tk, nsq):
    m = mod_ref[...]
    ccp = _split2(cc_ref[0])
    scp = _split2(cc_ref[1])
    ri = lax.broadcasted_iota(jnp.int32, (tk, tk), 0)
    ci = lax.broadcasted_iota(jnp.int32, (tk, tk), 1)
    anti = jnp.where(ri + ci == tk - 1, 1.0, 0.0).astype(BF)
    blocks = []
    for sq in range(nsq):
        base = sq * s
        hparts = _split2(x_ref[base:base + s, :] * (1.0 + m[1:2]) + m[0:1])
        for j in range(s // (2 * tk)):
            ac = _dot_x3(_split2(cs_ref[0, j]), hparts)
            as_ = _dot_x3(_split2(cs_ref[1, j]), hparts)
            direct, mirror = [], []
            for gi in range(N_GROUPS):
                cs = slice(gi * GROUP_W, (gi + 1) * GROUP_W)
                p = _dot_x3(_split2(ac[:, cs]), ccp)
                q = _dot_x3(_split2(as_[:, cs]), scp)
                direct.append((p - q)[0:tk])
                mirror.append((p + q)[1:tk + 1])
            blocks.append((base + j * tk, jnp.concatenate(direct, axis=1).astype(BF)))
            mir = jnp.concatenate(mirror, axis=1).astype(BF)
            blocks.append((base + s - (j + 1) * tk, _dot(anti, mir).astype(BF)))
    for r0, blk in blocks:
        o_ref[r0:r0 + tk, :] = blk


def _half_dft_mats(s, tk):
    full = _dft_mats(s, s ** -0.5)
    rows = (np.arange(s // (2 * tk))[:, None] * tk + np.arange(tk + FT_PAD)[None, :]) % s
    return full[:, rows, :]


def _fourier(x, mod, layer, s, nseq, nsq, row_off, mod_row_fn):
    tk = min(FT_TK, s // 2)
    blk = nsq * s
    blk_off = row_off // blk
    cs = jnp.asarray(_half_dft_mats(s, tk))
    cc = jnp.asarray(_dft_mats(GROUP_W, GROUP_W ** -0.5))
    return pl.pallas_call(
        functools.partial(_fourier_kernel, s=s, tk=tk, nsq=nsq),
        grid=(nseq // nsq,),
        in_specs=[pl.BlockSpec((blk, D), lambda b: (blk_off + b, 0)),
                  pl.BlockSpec((None, None, 6, D), lambda b: (layer, mod_row_fn(b), 0, 0)),
                  pl.BlockSpec(cs.shape, lambda b: (0, 0, 0, 0), pipeline_mode=pl.Buffered(1)),
                  _full((2, GROUP_W, GROUP_W))],
        out_specs=pl.BlockSpec((blk, D), lambda b: (b, 0)),
        out_shape=jax.ShapeDtypeStruct((nseq * s, D), BF),
        compiler_params=_cparams(1),
        name="fourier_%d" % s,
    )(x, mod, cs, cc)


def _pos_embed():
    rows = DEC_SEQ // GRID_W
    quarter = D // 4
    omega = 1.0 / (10000.0 ** (np.arange(quarter, dtype=np.float64) / quarter))
    rr, cc = np.meshgrid(np.arange(rows, dtype=np.float64), np.arange(GRID_W, dtype=np.float64), indexing="ij")
    er = rr.reshape(-1, 1) * omega
    ec = cc.reshape(-1, 1) * omega
    return np.concatenate([np.sin(er), np.cos(er), np.sin(ec), np.cos(ec)], axis=-1).astype(np.float32)


def _gate_perm():
    perm = np.zeros(4 * H, dtype=np.int32)
    for hp in range(N_HP):
        for d in range(2):
            for j in range(2):
                for hh in range(2):
                    perm[hp * NG + d * 4 + j * 2 + hh] = d * 2 * H + j * H + 2 * hp + hh
    return perm


def kernel(x_prompt, x_sample, state_C, state_n, state_m, c, c_ctx, w_mod, b_mod, ln_g, ln_b, mlp_w1, mlp_w2,
           ml_w_in, ml_w_gate, ml_b_gate, ml_norm_g, ml_w_out, sc_w_in, sc_conv_w, sc_w_out,
           pl_w, pl_b, pl_scale, ft_w_out, ft_b_out):
    cond = jnp.concatenate([c_ctx[None, :], c, jnp.zeros((NCOND - 1 - DEC_BATCH, D), F32)], axis=0)
    mod = _modulation(cond, w_mod, b_mod)

    x = (x_prompt.reshape(NCTX, D), x_sample.reshape(NLAT, D), jnp.asarray(_pos_embed()))

    new_c = new_n = new_m = None
    for i in range(DEPTH):
        kind, j = i % 4, i // 4
        if kind == 0:
            perm = _gate_perm()
            wg = ml_w_gate[j][:, perm].astype(BF)
            bg = ml_b_gate[j][perm]
            qt, k, kt, vt, o, g, gt = _ml_proj(
                x, mod, i, ml_w_in[j], wg, wg.T, bg.reshape(1, -1), bg.reshape(-1, 1))
            scan_args = (qt, k, kt, vt, o, g, gt, ml_norm_g[j])
            uc, cst, nmst = _ml_scan(*scan_args)
            c0 = jnp.concatenate([jnp.swapaxes(state_C[:, j], -1, -2), state_n[:, j][..., None, :],
                                  jnp.zeros((DEC_BATCH, 2, H, DHA - DHV - 1, DQK), F32)], axis=-2)
            m0 = jnp.broadcast_to(state_m[:, j][..., None, None], (DEC_BATCH, 2, H, 1, LANES))
            (ul,) = _ml_scan(*scan_args, c0, m0)
            x = _out_proj(uc, ul, ml_w_out[j], jnp.zeros((D,), F32), x, mod, i, ln_g[i, 0], ln_b[i, 0])
            new_c = cst[:, None]
            new_n = nmst[..., 0, :DQK][:, None]
            new_m = nmst[..., 0, DQK][:, None]
        elif kind == 1:
            x = _sconv(x, mod, i, sc_w_in[j], sc_conv_w[j], sc_w_out[j],
                       ln_g[i, 0], ln_b[i, 0])
        elif kind == 2:
            x = _pool(x, mod, i, pl_w[j].astype(BF), pl_b[j], pl_scale[j], ln_g[i, 0], ln_b[i, 0])
        else:
            fc = _fourier(x, mod, i, SEQ, BATCH, 4, 0, lambda b: 0)
            fl = _fourier(x, mod, i, DEC_SEQ, DEC_BATCH, 1, NCTX, lambda b: 1 + b)
            x = _out_proj(fc, fl, ft_w_out[j], ft_b_out[j], x, mod, i, ln_g[i, 0], ln_b[i, 0])
        x = _mlp(x, mod, i, mlp_w1, mlp_w2, ln_g[i, 1], ln_b[i, 1],
                 split_out=(i == DEPTH - 1))

    y_prompt = x[0].reshape(BATCH, SEQ, D)
    y_sample = x[1].reshape(DEC_BATCH, DEC_SEQ, D)
    return y_prompt, y_sample, new_c, new_n, new_m
```

```python
import functools

import numpy as np
import jax
import jax.numpy as jnp
from jax import lax
from jax.experimental import pallas as pl
from jax.experimental.pallas import tpu as pltpu

D = 1024
BATCH, SEQ = 16, 256
DEC_BATCH, DEC_SEQ = 4, 1024
DEPTH = 4
GRID_W = 64
H = 8
DQK = 64
DHV = 128
QK = H * DQK
V = H * DHV
N_GROUPS = 4
GROUP_W = D // N_GROUPS
POOL_WINDOWS = (2, 4, 8, 16)
D_FF = 4 * D
ALPHA = (2.0 * DEPTH) ** 0.25
LN_EPS = 1e-5
F32 = jnp.float32
BF = jnp.bfloat16

NCTX = BATCH * SEQ
NLAT = DEC_BATCH * DEC_SEQ
NTOK = NCTX + NLAT
NCOND = 8
CHUNK = 256
SUBLANES, LANES = 8, 128
V7X_VMEM_BYTES = 64 * 1024 * 1024
VMEM_LIMIT = V7X_VMEM_BYTES - 8 * 1024 * 1024


def _cparams(n_axes):
    return pltpu.CompilerParams(dimension_semantics=("arbitrary",) * n_axes,
                                vmem_limit_bytes=VMEM_LIMIT)


def _mod_row(row0):
    return jnp.where(row0 < NCTX, 0, 1 + (row0 - NCTX) // DEC_SEQ)


def _seq_len(row0):
    return jnp.where(row0 < NCTX, SEQ, DEC_SEQ)


def _ln(z, g, b):
    mu = jnp.mean(z, axis=-1, keepdims=True)
    zc = z - mu
    var = jnp.mean(zc * zc, axis=-1, keepdims=True)
    return zc * lax.rsqrt(var + LN_EPS) * g + b


def _dot(a, b):
    return jnp.dot(a, b, preferred_element_type=F32)


def _dot_nt(a, b):
    return lax.dot_general(a, b, (((1,), (1,)), ((), ())), preferred_element_type=F32)


def _split2(x):
    hi = x.astype(BF)
    lo = (x - hi.astype(F32)).astype(BF)
    return hi, lo


def _split3(x):
    hi = x.astype(BF)
    r = x - hi.astype(F32)
    mid = r.astype(BF)
    lo = (r - mid.astype(F32)).astype(BF)
    return hi, mid, lo


def _full(shape):
    n = len(shape)
    return pl.BlockSpec(shape, lambda *_: (0,) * n)


def _x_specs(x, tm):
    if not isinstance(x, tuple):
        return [pl.BlockSpec((tm, D), lambda t: (t, 0))], [x]
    nc = NCTX // tm
    return ([pl.BlockSpec((tm, D), lambda t: (jnp.minimum(t, nc - 1), 0)),
             pl.BlockSpec((tm, D), lambda t: (jnp.maximum(t - nc, 0), 0)),
             pl.BlockSpec((tm, D), lambda t: (t % (DEC_SEQ // tm), 0))], list(x))


def _read_x(x_refs, tm):
    if len(x_refs) == 1:
        return x_refs[0][...]
    xc_ref, xl_ref, pe_ref = x_refs
    return jnp.where(pl.program_id(0) * tm < NCTX, xc_ref[...], xl_ref[...] + pe_ref[...])


def _mod_spec(layer, tm):
    return pl.BlockSpec((None, None, 6, D), lambda t: (layer, _mod_row(t * tm), 0, 0))


def _mod_kernel(c_ref, w_ref, b_ref, o_ref):
    c = c_ref[...]
    s = c * jax.nn.sigmoid(c)
    o_ref[...] = _dot(s.astype(BF), w_ref[...].astype(BF)) + b_ref[...]


def _modulation(cond, w_mod, b_mod):
    tn = 3 * D
    out = pl.pallas_call(
        _mod_kernel,
        grid=(DEPTH, 6 * D // tn),
        in_specs=[pl.BlockSpec((NCOND, D), lambda i, j: (0, 0)),
                  pl.BlockSpec((None, D, tn), lambda i, j: (i, 0, j)),
                  pl.BlockSpec((None, 1, tn), lambda i, j: (i, 0, j))],
        out_specs=pl.BlockSpec((None, NCOND, tn), lambda i, j: (i, 0, j)),
        out_shape=jax.ShapeDtypeStruct((DEPTH, NCOND, 6 * D), F32),
        compiler_params=_cparams(2),
        name="modulation",
    )(cond, w_mod, b_mod.reshape(DEPTH, 1, 6 * D))
    return out.reshape(DEPTH, NCOND, 6, D)


MLP_TM = 512
MLP_FC = 512
MLP_NF = D_FF // MLP_FC
MLP_NT = NTOK // MLP_TM


def _mlp_tile(s):
    return jnp.maximum(s - (MLP_NF - 1), 0)


def _mlp_kernel(mod_ref, w1_ref, w2_ref, g_ref, b_ref, x1_ref, x_ref, *refs):
    o_refs, (w1s_ref, w2s_ref, acc_ref, h_ref) = refs[:-4], refs[-4:]
    s = pl.program_id(0)
    m = mod_ref[...]

    def pre(x):
        return (x * (1.0 + m[4:5]) + m[3:4]).astype(BF)

    def hidden(h, w1c):
        return jnp.square(jnp.maximum(_dot(h, w1c), 0.0)).astype(BF)

    def result(x, acc):
        return _ln(ALPHA * x + m[5:6] * acc, g_ref[...], b_ref[...])

    @pl.when(s < MLP_NF)
    def _():
        @pl.when(s == 0)
        def _():
            h_ref[0:MLP_TM, :] = pre(x_ref[...])
            h_ref[MLP_TM:, :] = pre(x1_ref[...])

        w1c = w1_ref[...].astype(BF)
        w2c = w2_ref[...].astype(BF)
        part = _dot(hidden(h_ref[...], w1c), w2c)
        w1s_ref[s] = w1c
        w2s_ref[s] = w2c

        @pl.when(s == 0)
        def _():
            acc_ref[...] = part

        @pl.when(s > 0)
        def _():
            acc_ref[...] += part

        @pl.when(s == MLP_NF - 1)
        def _():
            o_refs[0][...] = result(x_ref[...], acc_ref[0:MLP_TM, :])

    @pl.when(s == MLP_NF)
    def _():
        o_refs[0][...] = result(x1_ref[...], acc_ref[MLP_TM:, :])

    @pl.when(s > MLP_NF)
    def _():
        x = x_ref[...]
        h = pre(x)
        acc = jnp.zeros((MLP_TM, D), F32)
        for f in range(MLP_NF):
            acc = acc + _dot(hidden(h, w1s_ref[f]), w2s_ref[f])
        res = result(x, acc)
        if len(o_refs) == 1:
            o_refs[0][...] = res
        else:
            is_ctx = _mlp_tile(s) * MLP_TM < NCTX

            @pl.when(is_ctx)
            def _():
                o_refs[0][...] = res

            @pl.when(jnp.logical_not(is_ctx))
            def _():
                o_refs[1][...] = res


def _mlp(x, mod, layer, w1, w2, ln_g, ln_b, split_out=False):
    tm = MLP_TM
    nc = NCTX // tm
    assert nc >= 2 and MLP_NT >= 2
    if split_out:
        out_specs = [pl.BlockSpec((tm, D), lambda s: (jnp.minimum(_mlp_tile(s), nc - 1), 0)),
                     pl.BlockSpec((tm, D), lambda s: (jnp.maximum(_mlp_tile(s) - nc, 0), 0))]
        out_shape = [jax.ShapeDtypeStruct((NCTX, D), F32), jax.ShapeDtypeStruct((NLAT, D), F32)]
    else:
        out_specs = pl.BlockSpec((tm, D), lambda s: (_mlp_tile(s), 0))
        out_shape = jax.ShapeDtypeStruct((NTOK, D), F32)
    return pl.pallas_call(
        _mlp_kernel,
        grid=(MLP_NF - 1 + MLP_NT,),
        in_specs=[pl.BlockSpec((None, None, 6, D), lambda s: (layer, _mod_row(_mlp_tile(s) * tm), 0, 0)),
                  pl.BlockSpec((None, D, MLP_FC), lambda s: (layer, 0, jnp.minimum(s, MLP_NF - 1))),
                  pl.BlockSpec((None, MLP_FC, D), lambda s: (layer, jnp.minimum(s, MLP_NF - 1), 0)),
                  _full((1, D)), _full((1, D)),
                  pl.BlockSpec((tm, D), lambda s: (1, 0)),
                  pl.BlockSpec((tm, D), lambda s: (_mlp_tile(s), 0))],
        out_specs=out_specs,
        out_shape=out_shape,
        scratch_shapes=[pltpu.VMEM((MLP_NF, D, MLP_FC), BF),
                        pltpu.VMEM((MLP_NF, MLP_FC, D), BF),
                        pltpu.VMEM((2 * tm, D), F32),
                        pltpu.VMEM((2 * tm, D), BF)],
        compiler_params=_cparams(1),
        name="mlp",
    )(mod, w1, w2, ln_g.reshape(1, D), ln_b.reshape(1, D), x, x)


OUT_TM = 1024
OUT_SUB = 256


def _out_kernel(uc_ref, ul_ref, w_f32_ref, bias_ref, mod_ref, g_ref, b_ref, *refs):
    x_refs, o_ref, w_ref = refs[:-2], refs[-2], refs[-1]

    @pl.when(pl.program_id(0) == 0)
    def _():
        w_ref[...] = w_f32_ref[...].astype(BF)

    is_ctx = pl.program_id(0) * OUT_TM < NCTX
    m = mod_ref[...]
    outs = []
    for r in range(0, OUT_TM, OUT_SUB):
        rs = slice(r, r + OUT_SUB)
        u = jnp.where(is_ctx, uc_ref[rs, :], ul_ref[rs, :])
        y = _dot(u, w_ref[...]) + bias_ref[...]
        if len(x_refs) == 1:
            x = x_refs[0][rs, :]
        else:
            x = jnp.where(is_ctx, x_refs[0][rs, :], x_refs[1][rs, :] + x_refs[2][rs, :])
        outs.append(_ln(ALPHA * x + m[2:3] * y, g_ref[...], b_ref[...]))
    for r, out in zip(range(0, OUT_TM, OUT_SUB), outs):
        o_ref[r:r + OUT_SUB, :] = out


def _out_proj(uc, ul, w, bias, x, mod, layer, ln_g, ln_b):
    tm = OUT_TM
    nc = NCTX // tm
    k = w.shape[0]
    x_specs, x_args = _x_specs(x, tm)
    return pl.pallas_call(
        _out_kernel,
        grid=(NTOK // tm,),
        in_specs=[pl.BlockSpec((tm, k), lambda t: (jnp.minimum(t, nc - 1), 0)),
                  pl.BlockSpec((tm, k), lambda t: (jnp.maximum(t - nc, 0), 0)),
                  pl.BlockSpec((k, D), lambda t: (0, 0), pipeline_mode=pl.Buffered(1)),
                  _full((1, D)),
                  _mod_spec(layer, tm),
                  _full((1, D)), _full((1, D))] + x_specs,
        out_specs=pl.BlockSpec((tm, D), lambda t: (t, 0)),
        out_shape=jax.ShapeDtypeStruct((NTOK, D), F32),
        scratch_shapes=[pltpu.VMEM((k, D), BF)],
        compiler_params=_cparams(1),
        name="out_proj",
    )(uc, ul, w, bias.reshape(1, D), mod, ln_g.reshape(1, D), ln_b.reshape(1, D), *x_args)


ML_TM = 512
N_HP = H // 2
SCAN_PAIRS = 2
NG = 8


def _log_sigmoid(g):
    return jnp.minimum(g, 0.0) - jnp.log1p(jnp.exp(-jnp.abs(g)))


def _ml_proj_kernel(mod_ref, w_in_ref, wg_ref, wgt_ref, bg_ref, bgt_ref, *refs):
    x_refs = refs[:-10]
    qt_ref, k_ref, kt_ref, vt_ref, o_ref, g_ref, gt_ref = refs[-10:-3]
    wt_ref, wk_ref, wo_ref = refs[-3:]
    r0, r1, r2 = QK, QK + 4 * NG, 2 * QK + 4 * NG

    @pl.when(pl.program_id(0) == 0)
    def _():
        wt_ref[0:r0, :] = w_in_ref[:, 0:QK].T.astype(BF)
        wt_ref[r0:r1, :] = wgt_ref[...]
        wt_ref[r1:r2, :] = w_in_ref[:, QK:2 * QK].T.astype(BF)
        wt_ref[r2:, :] = w_in_ref[:, 2 * QK:2 * QK + V].T.astype(BF)
        wk_ref[...] = w_in_ref[:, QK:2 * QK].astype(BF)
        wo_ref[...] = w_in_ref[:, 2 * QK + V:].astype(BF)

    m = mod_ref[...]
    h = (_read_x(x_refs, ML_TM) * (1.0 + m[1:2]) + m[0:1]).astype(BF)
    tr = _dot_nt(wt_ref[...], h)
    qt = tr[:r0].astype(BF)
    kt = (tr[r1:r2] * (DQK ** -0.5)).astype(BF)
    vt = tr[r2:].astype(BF)
    k = (_dot(h, wk_ref[...]) * (DQK ** -0.5)).astype(BF)
    o = _dot(h, wo_ref[...])
    g = _dot(h, wg_ref[...]) + bg_ref[...]
    col = lax.broadcasted_iota(jnp.int32, g.shape, 1)
    g = jnp.where((col & 2) != 0, _log_sigmoid(g), g)
    gt = tr[r0:r1] + bgt_ref[...]
    row = lax.broadcasted_iota(jnp.int32, gt.shape, 0)
    gt = jnp.where((row & 2) != 0, _log_sigmoid(gt), gt)
    qt_ref[...] = qt
    k_ref[...] = k
    kt_ref[...] = kt
    vt_ref[...] = vt
    o_ref[...] = o
    for hp in range(N_HP):
        g_ref[hp] = g[:, hp * NG:(hp + 1) * NG]
        gt_ref[hp] = gt[hp * NG:(hp + 1) * NG, :]


def _ml_proj(x, mod, layer, w_in, wg, wgt, bg, bgt):
    tm = ML_TM
    res = lambda shape: pl.BlockSpec(shape, lambda t: (0,) * len(shape), pipeline_mode=pl.Buffered(1))
    x_specs, x_args = _x_specs(x, tm)
    return pl.pallas_call(
        _ml_proj_kernel,
        grid=(NTOK // tm,),
        in_specs=[_mod_spec(layer, tm),
                  res((D, 2 * QK + 2 * V)), res((D, 4 * NG)), res((4 * NG, D)),
                  res((1, 4 * NG)), res((4 * NG, 1))] + x_specs,
        out_specs=[pl.BlockSpec((QK, tm), lambda t: (0, t)),
                   pl.BlockSpec((tm, QK), lambda t: (t, 0)),
                   pl.BlockSpec((QK, tm), lambda t: (0, t)),
                   pl.BlockSpec((V, tm), lambda t: (0, t)),
                   pl.BlockSpec((tm, V), lambda t: (t, 0)),
                   pl.BlockSpec((N_HP, tm, NG), lambda t: (0, t, 0)),
                   pl.BlockSpec((N_HP, NG, tm), lambda t: (0, 0, t))],
        out_shape=[jax.ShapeDtypeStruct((QK, NTOK), BF),
                   jax.ShapeDtypeStruct((NTOK, QK), BF),
                   jax.ShapeDtypeStruct((QK, NTOK), BF),
                   jax.ShapeDtypeStruct((V, NTOK), BF),
                   jax.ShapeDtypeStruct((NTOK, V), F32),
                   jax.ShapeDtypeStruct((N_HP, NTOK, NG), F32),
                   jax.ShapeDtypeStruct((N_HP, NG, NTOK), F32)],
        scratch_shapes=[pltpu.VMEM((2 * QK + 4 * NG + V, D), BF),
                        pltpu.VMEM((D, QK), BF),
                        pltpu.VMEM((D, V), BF)],
        compiler_params=_cparams(1),
        name="mlstm_proj",
    )(mod, w_in, wg, wgt, bg, bgt, *x_args)


DHA = DHV + SUBLANES


def _scan_chunk(k_c, kt_c, qt_c, vta, rr_col, rr_row, b_row, m, cta, d, need_state):
    L = k_c.shape[0]
    si = lax.broadcasted_iota(jnp.int32, (L, L), 0)
    ti = lax.broadcasted_iota(jnp.int32, (L, L), 1)
    mask = (si <= ti) if d == 0 else (si >= ti)
    rrm = jnp.where(mask, rr_col, -jnp.inf)
    g = jnp.maximum(jnp.max(rrm, axis=0, keepdims=True), m)
    st = _dot(k_c, qt_c) * jnp.exp(rrm - g)
    num = _dot(vta[:DHV], st.astype(BF))
    den = jnp.sum(st, axis=0, keepdims=True)
    if cta is not None:
        a = jnp.exp(m - g)
        qc = _dot(cta.astype(BF), qt_c)
        num = num + a * qc[:DHV]
        den = den + a * qc[DHV:DHV + 1]
    mt = b_row + g
    ht = num * (1.0 / jnp.maximum(jnp.abs(den), jnp.exp(-mt)))
    if not need_state:
        return ht, None, None
    last = L - 1 if d == 0 else 0
    m_new = mt[:, last:last + 1]
    b_last = b_row[:, last:last + 1]
    w = jnp.exp(b_last + rr_row - m_new)
    kw = (kt_c.astype(F32) * w).astype(BF)
    if cta is None:
        return ht, (_dot_nt(kw, vta[:DHV]), _dot_nt(vta[DHV:], kw)), m_new
    c_new = jnp.exp(b_last + m - m_new) * cta + _dot_nt(vta, kw)
    return ht, c_new, m_new


def _scan_kernel(*refs, is_ctx):
    if is_ctx:
        qt_ref, k_ref, kt_ref, vt_ref, o_ref, g_ref, gt_ref, ng_ref, u_ref, cst_ref, nmst_ref = refs
    else:
        qt_ref, k_ref, kt_ref, vt_ref, o_ref, g_ref, gt_ref, ng_ref, c0_ref, m0_ref, u_ref = refs
    L = CHUNK
    nch = DEC_SEQ // L
    ri = lax.broadcasted_iota(jnp.int32, (L, L), 0)
    ci = lax.broadcasted_iota(jnp.int32, (L, L), 1)
    lower = jnp.where(ci <= ri, 1.0, 0.0).astype(BF)
    upper = jnp.where(ci >= ri, 1.0, 0.0).astype(BF)
    sub = lax.broadcasted_iota(jnp.int32, (DHA - DHV, L), 0)
    ones_rows = jnp.where(sub == 0, 1.0, 0.0).astype(BF)

    rr_cols, rr_rows, b_rows = [], [], []
    for pp in range(SCAN_PAIRS):
        tparts = _split3(jnp.concatenate([gt_ref[pp, :, c * L:(c + 1) * L] for c in range(nch)], axis=0))
        b_row_all = (sum(_dot(p, upper) for p in tparts), sum(_dot(p, lower) for p in tparts))
        rr_cols.append([])
        rr_rows.append([])
        b_rows.append([])
        for c in range(nch):
            gc = g_ref[pp, c * L:(c + 1) * L, :]
            gtc = gt_ref[pp, :, c * L:(c + 1) * L]
            gparts = _split3(gc)
            b_col = (sum(_dot(lower, p) for p in gparts), sum(_dot(upper, p) for p in gparts))
            b_row = tuple(b[c * NG:(c + 1) * NG, :] for b in b_row_all)
            rr_cols[pp].append([gc[:, 0:NG - 2] - b[:, 2:NG] for b in b_col])
            rr_rows[pp].append([gtc[0:NG - 2, :] - b[2:NG, :] for b in b_row])
            b_rows[pp].append(b_row)

    states, us = [], []
    for hl in range(2 * SCAN_PAIRS):
        pp, hh = divmod(hl, 2)
        hsum = [None] * nch
        for d in range(2):
            ig_i, lf_i = d * 4 + hh, d * 4 + 2 + hh
            if is_ctx:
                m, cta = jnp.zeros((1, 1), F32), None
            else:
                m = m0_ref[d, hl][:, 0:1]
                cta = c0_ref[d, hl]
            order = range(nch) if d == 0 else range(nch - 1, -1, -1)
            for n, c in enumerate(order):
                sl = slice(c * L, (c + 1) * L)
                hs = slice(hl * DQK, (hl + 1) * DQK)
                vta = jnp.concatenate([vt_ref[hl * DHV:(hl + 1) * DHV, sl], ones_rows], axis=0)
                need_state = is_ctx or n < nch - 1
                hcur, c_new, m_new = _scan_chunk(
                    k_ref[sl, hs], kt_ref[hs, sl], qt_ref[hs, sl], vta,
                    rr_cols[pp][c][d][:, ig_i:ig_i + 1], rr_rows[pp][c][d][ig_i:ig_i + 1, :],
                    b_rows[pp][c][d][lf_i:lf_i + 1, :], m, cta, d, need_state)
                hsum[c] = hcur if d == 0 else hsum[c] + hcur
                if is_ctx:
                    states.append(((c, d, hl), c_new, m_new))
                else:
                    m, cta = m_new, c_new

        for c in range(nch):
            sl = slice(c * L, (c + 1) * L)
            mu = jnp.mean(hsum[c], axis=0, keepdims=True)
            hc = hsum[c] - mu
            var = jnp.mean(hc * hc, axis=0, keepdims=True)
            hn = hc * lax.rsqrt(var + LN_EPS) * ng_ref[hl * DHV:(hl + 1) * DHV, :]
            og = jax.nn.sigmoid(o_ref[sl, hl * DHV:(hl + 1) * DHV])
            us.append(((sl, slice(hl * DHV, (hl + 1) * DHV)), (hn.T * og).astype(BF)))

    for idx, u in us:
        u_ref[idx] = u
    for idx, (c_fin, n_fin), m_fin in states:
        cst_ref[idx] = c_fin
        nmst_ref[idx] = jnp.concatenate([n_fin, jnp.broadcast_to(m_fin, (SUBLANES, LANES - DQK))], axis=1)


def _ml_scan(qt, k, kt, vt, o, g, gt, norm_g, c0aug=None, m0b=None):
    is_ctx = c0aug is None
    tt = DEC_SEQ
    toff = 0 if is_ctx else NCTX // tt
    nt = (NCTX if is_ctx else NLAT) // tt
    nh = 2 * SCAN_PAIRS
    in_specs = [pl.BlockSpec((nh * DQK, tt), lambda t, p: (p, toff + t)),
                pl.BlockSpec((tt, nh * DQK), lambda t, p: (toff + t, p)),
                pl.BlockSpec((nh * DQK, tt), lambda t, p: (p, toff + t)),
                pl.BlockSpec((nh * DHV, tt), lambda t, p: (p, toff + t)),
                pl.BlockSpec((tt, nh * DHV), lambda t, p: (toff + t, p)),
                pl.BlockSpec((SCAN_PAIRS, tt, NG), lambda t, p: (p, toff + t, 0)),
                pl.BlockSpec((SCAN_PAIRS, NG, tt), lambda t, p: (p, 0, toff + t)),
                pl.BlockSpec((nh * DHV, 1), lambda t, p: (p, 0))]
    args = [qt, k, kt, vt, o, g, gt, norm_g.reshape(V, 1)]
    u_spec = pl.BlockSpec((tt, nh * DHV), lambda t, p: (t, p))
    u_shape = jax.ShapeDtypeStruct((nt * tt, V), BF)
    if is_ctx:
        nseq = tt // SEQ
        out_specs = [u_spec,
                     pl.BlockSpec((nseq, 2, nh, DQK, DHV), lambda t, p: (t, 0, p, 0, 0)),
                     pl.BlockSpec((nseq, 2, nh, SUBLANES, LANES), lambda t, p: (t, 0, p, 0, 0))]
        out_shape = [u_shape,
                     jax.ShapeDtypeStruct((BATCH, 2, H, DQK, DHV), F32),
                     jax.ShapeDtypeStruct((BATCH, 2, H, SUBLANES, LANES), F32)]
    else:
        in_specs += [pl.BlockSpec((None, 2, nh, DHA, DQK), lambda t, p: (t, 0, p, 0, 0)),
                     pl.BlockSpec((None, 2, nh, 1, LANES), lambda t, p: (t, 0, p, 0, 0))]
        args += [c0aug, m0b]
        out_specs = [u_spec]
        out_shape = [u_shape]
    return pl.pallas_call(
        functools.partial(_scan_kernel, is_ctx=is_ctx),
        grid=(nt, N_HP // SCAN_PAIRS),
        in_specs=in_specs,
        out_specs=out_specs,
        out_shape=out_shape,
        compiler_params=_cparams(2),
        name="mlstm_scan_ctx" if is_ctx else "mlstm_scan_lat",
    )(*args)


SC_TM = 512
SC_HALO = SUBLANES


def _sconv_kernel(x_ref, prev_ref, next_ref, mod_ref, w_in_f32_ref, cw_ref, w_out_f32_ref, g_ref, b_ref, o_ref,
                  e_ref, w_in_ref, w_out_ref):
    tm = SC_TM

    @pl.when(pl.program_id(0) == 0)
    def _():
        w_in_ref[...] = w_in_f32_ref[...].astype(BF)
        w_out_ref[...] = w_out_f32_ref[...].astype(BF)

    row0 = pl.program_id(0) * tm
    slen = _seq_len(row0)
    pos = (row0 + lax.broadcasted_iota(jnp.int32, (tm, 1), 0)) & (slen - 1)
    m = mod_ref[...]
    x = x_ref[...]
    xe = jnp.concatenate([prev_ref[...], x, next_ref[...]], axis=0)
    he = (xe * (1.0 + m[1:2]) + m[0:1]).astype(BF)
    cu = _dot(he, w_in_ref[:, D:2 * D]) * _dot(he, w_in_ref[:, 2 * D:3 * D])
    bg = _dot(he[SC_HALO:SC_HALO + tm], w_in_ref[:, 0:D])
    e_ref[...] = cu
    prev = jnp.where(pos == 0, 0.0, e_ref[SC_HALO - 1:SC_HALO - 1 + tm, :])
    nxt = jnp.where(pos == slen - 1, 0.0, e_ref[SC_HALO + 1:SC_HALO + 1 + tm, :])
    conv = cw_ref[0:1, :] * prev + cw_ref[1:2, :] * e_ref[SC_HALO:SC_HALO + tm, :] + cw_ref[2:3, :] * nxt
    y = _dot((bg * conv).astype(BF), w_out_ref[...])
    o_ref[...] = _ln(ALPHA * x + m[2:3] * y, g_ref[...], b_ref[...])


def _sconv(x, mod, layer, w_in, conv_w, w_out, ln_g, ln_b):
    tm = SC_TM
    hb = tm // SC_HALO
    nhb = NTOK // SC_HALO
    return pl.pallas_call(
        _sconv_kernel,
        grid=(NTOK // tm,),
        in_specs=[pl.BlockSpec((tm, D), lambda t: (t, 0)),
                  pl.BlockSpec((SC_HALO, D), lambda t: (jnp.maximum(t * hb - 1, 0), 0)),
                  pl.BlockSpec((SC_HALO, D), lambda t: (jnp.minimum((t + 1) * hb, nhb - 1), 0)),
                  _mod_spec(layer, tm),
                  pl.BlockSpec((D, 3 * D), lambda t: (0, 0), pipeline_mode=pl.Buffered(1)),
                  _full((3, D)),
                  pl.BlockSpec((D, D), lambda t: (0, 0), pipeline_mode=pl.Buffered(1)),
                  _full((1, D)), _full((1, D))],
        out_specs=pl.BlockSpec((tm, D), lambda t: (t, 0)),
        out_shape=jax.ShapeDtypeStruct((NTOK, D), F32),
        scratch_shapes=[pltpu.VMEM((tm + 2 * SC_HALO, D), F32),
                        pltpu.VMEM((D, 3 * D), BF),
                        pltpu.VMEM((D, D), BF)],
        compiler_params=_cparams(1),
        name="sconv",
    )(x, x, x, mod, w_in, conv_w, w_out, ln_g.reshape(1, D), ln_b.reshape(1, D))


PL_SEG = 256
PL_NSEG = 4
PL_TM = PL_SEG * PL_NSEG
PL_HALO = 16
PL_STRIDE = PL_SEG + 2 * PL_HALO


def _pool_kernel(x_ref, prev_ref, next_ref, mod_ref, w_ref, pb_ref, ps_ref, g_ref, b_ref, o_ref,
                 e_ref, s2_ref, s4_ref, s8_ref):
    tm, seg, hl, gw = PL_TM, PL_SEG, PL_HALO, GROUP_W
    row0 = pl.program_id(0) * tm
    slen = _seq_len(row0)
    m = mod_ref[...]
    x = x_ref[...]
    h = x * (1.0 + m[1:2]) + m[0:1]
    hp = prev_ref[...] * (1.0 + m[1:2]) + m[0:1]
    hn = next_ref[...] * (1.0 + m[1:2]) + m[0:1]
    n0 = PL_NSEG * PL_STRIDE
    for i in range(PL_NSEG):
        base = i * PL_STRIDE
        pos0 = (row0 + i * seg) & (slen - 1)
        before = hp if i == 0 else h[i * seg - hl:i * seg]
        after = hn if i == PL_NSEG - 1 else h[(i + 1) * seg:(i + 1) * seg + hl]
        e_ref[base:base + hl, :] = jnp.where(pos0 != 0, before, 0.0)
        e_ref[base + hl:base + hl + seg, :] = h[i * seg:(i + 1) * seg]
        e_ref[base + hl + seg:base + PL_STRIDE, :] = jnp.where(pos0 + seg != slen, after, 0.0)
    e_ref[n0:, :] = jnp.zeros((SUBLANES, D), F32)
    n2, n4, n8 = n0, n0 - SUBLANES, n0 - 2 * SUBLANES
    s2_ref[...] = e_ref[0:n2, :] + e_ref[1:n2 + 1, :]
    s4_ref[...] = s2_ref[0:n4, gw:] + s2_ref[2:n4 + 2, gw:]
    s8_ref[...] = s4_ref[0:n8, gw:] + s4_ref[4:n8 + 4, gw:]

    def rows(ref, off, lanes):
        return jnp.concatenate([ref[i * PL_STRIDE + off:i * PL_STRIDE + off + seg, lanes]
                                for i in range(PL_NSEG)], axis=0)

    tots = [rows(s2_ref, hl - 1, slice(0, gw)),
            rows(s4_ref, hl - 2, slice(0, gw)),
            rows(s8_ref, hl - 4, slice(0, gw)),
            rows(s8_ref, hl - 8, slice(gw, 2 * gw)) + rows(s8_ref, hl, slice(gw, 2 * gw))]
    pos = (row0 + lax.broadcasted_iota(jnp.int32, (tm, 1), 0)) & (slen - 1)
    ys = []
    for gi, win in enumerate(POOL_WINDOWS):
        cs = slice(gi * gw, (gi + 1) * gw)
        back = win // 2
        cnt = jnp.minimum(pos + (win - back), slen) - jnp.maximum(pos - back, 0)
        p = tots[gi] / cnt.astype(F32) - h[:, cs]
        ys.append(_dot(p.astype(BF), w_ref[gi]))
    y = (jnp.concatenate(ys, axis=1) + pb_ref[...]) * ps_ref[...]
    o_ref[...] = _ln(ALPHA * x + m[2:3] * y, g_ref[...], b_ref[...])


def _pool(x, mod, layer, w, pb, ps, ln_g, ln_b):
    assert POOL_WINDOWS == (2, 4, 8, 16) and PL_HALO >= POOL_WINDOWS[-1] // 2
    assert SEQ % PL_SEG == 0 and DEC_SEQ % PL_SEG == 0
    tm = PL_TM
    hb = tm // PL_HALO
    nhb = NTOK // PL_HALO
    n0 = PL_NSEG * PL_STRIDE
    return pl.pallas_call(
        _pool_kernel,
        grid=(NTOK // tm,),
        in_specs=[pl.BlockSpec((tm, D), lambda t: (t, 0)),
                  pl.BlockSpec((PL_HALO, D), lambda t: (jnp.maximum(t * hb - 1, 0), 0)),
                  pl.BlockSpec((PL_HALO, D), lambda t: (jnp.minimum((t + 1) * hb, nhb - 1), 0)),
                  _mod_spec(layer, tm),
                  _full((N_GROUPS, GROUP_W, GROUP_W)),
                  _full((1, D)), _full((1, D)), _full((1, D)), _full((1, D))],
        out_specs=pl.BlockSpec((tm, D), lambda t: (t, 0)),
        out_shape=jax.ShapeDtypeStruct((NTOK, D), F32),
        scratch_shapes=[pltpu.VMEM((n0 + SUBLANES, D), F32),
                        pltpu.VMEM((n0, D), F32),
                        pltpu.VMEM((n0 - SUBLANES, D - GROUP_W), F32),
                        pltpu.VMEM((n0 - 2 * SUBLANES, D - 2 * GROUP_W), F32)],
        compiler_params=_cparams(1),
        name="pool",
    )(x, x, x, mod, w, pb.reshape(1, D), ps.reshape(1, D), ln_g.reshape(1, D), ln_b.reshape(1, D))


FT_TK = 256
FT_PAD = SUBLANES


def _dft_mats(n, scale):
    k = np.arange(n, dtype=np.int64)
    ang = 2.0 * np.pi * ((k[:, None] * k[None, :]) % n).astype(np.float64) / n
    return np.stack([np.cos(ang) * scale, np.sin(ang) * scale]).astype(np.float32)


def _dot_x3(a_parts, b_parts):
    (ah, al), (bh, bl) = a_parts, b_parts
    return _dot(ah, bh) + (_dot(ah, bl) + _dot(al, bh))


def _fourier_kernel(x_ref, mod_ref, cs_ref, cc_ref, o_ref, *, s, tk, nsq):
    m = mod_ref[...]
    ccp = _split2(cc_ref[0])
    scp = _split2(cc_ref[1])
    ri = lax.broadcasted_iota(jnp.int32, (tk, tk), 0)
    ci = lax.broadcasted_iota(jnp.int32, (tk, tk), 1)
    anti = jnp.where(ri + ci == tk - 1, 1.0, 0.0).astype(BF)
    blocks = []
    for sq in range(nsq):
        base = sq * s
        hparts = _split2(x_ref[base:base + s, :] * (1.0 + m[1:2]) + m[0:1])
        for j in range(s // (2 * tk)):
            ac = _dot_x3(_split2(cs_ref[0, j]), hparts)
            as_ = _dot_x3(_split2(cs_ref[1, j]), hparts)
            direct, mirror = [], []
            for gi in range(N_GROUPS):
                cs = slice(gi * GROUP_W, (gi + 1) * GROUP_W)
                p = _dot_x3(_split2(ac[:, cs]), ccp)
                q = _dot_x3(_split2(as_[:, cs]), scp)
                direct.append((p - q)[0:tk])
                mirror.append((p + q)[1:tk + 1])
            blocks.append((base + j * tk, jnp.concatenate(direct, axis=1).astype(BF)))
            mir = jnp.concatenate(mirror, axis=1).astype(BF)
            blocks.append((base + s - (j + 1) * tk, _dot(anti, mir).astype(BF)))
    for r0, blk in blocks:
        o_ref[r0:r0 + tk, :] = blk


def _half_dft_mats(s, tk):
    full = _dft_mats(s, s ** -0.5)
    rows = (np.arange(s // (2 * tk))[:, None] * tk + np.arange(tk + FT_PAD)[None, :]) % s
    return full[:, rows, :]


def _fourier(x, mod, layer, s, nseq, nsq, row_off, mod_row_fn):
    tk = min(FT_TK, s // 2)
    blk = nsq * s
    blk_off = row_off // blk
    cs = jnp.asarray(_half_dft_mats(s, tk))
    cc = jnp.asarray(_dft_mats(GROUP_W, GROUP_W ** -0.5))
    return pl.pallas_call(
        functools.partial(_fourier_kernel, s=s, tk=tk, nsq=nsq),
        grid=(nseq // nsq,),
        in_specs=[pl.BlockSpec((blk, D), lambda b: (blk_off + b, 0)),
                  pl.BlockSpec((None, None, 6, D), lambda b: (layer, mod_row_fn(b), 0, 0)),
                  pl.BlockSpec(cs.shape, lambda b: (0, 0, 0, 0), pipeline_mode=pl.Buffered(1)),
                  _full((2, GROUP_W, GROUP_W))],
        out_specs=pl.BlockSpec((blk, D), lambda b: (b, 0)),
        out_shape=jax.ShapeDtypeStruct((nseq * s, D), BF),
        compiler_params=_cparams(1),
        name="fourier_%d" % s,
    )(x, mod, cs, cc)


def _pos_embed():
    rows = DEC_SEQ // GRID_W
    quarter = D // 4
    omega = 1.0 / (10000.0 ** (np.arange(quarter, dtype=np.float64) / quarter))
    rr, cc = np.meshgrid(np.arange(rows, dtype=np.float64), np.arange(GRID_W, dtype=np.float64), indexing="ij")
    er = rr.reshape(-1, 1) * omega
    ec = cc.reshape(-1, 1) * omega
    return np.concatenate([np.sin(er), np.cos(er), np.sin(ec), np.cos(ec)], axis=-1).astype(np.float32)


def _gate_perm():
    perm = np.zeros(4 * H, dtype=np.int32)
    for hp in range(N_HP):
        for d in range(2):
            for j in range(2):
                for hh in range(2):
                    perm[hp * NG + d * 4 + j * 2 + hh] = d * 2 * H + j * H + 2 * hp + hh
    return perm


def kernel(x_prompt, x_sample, state_C, state_n, state_m, c, c_ctx, w_mod, b_mod, ln_g, ln_b, mlp_w1, mlp_w2,
           ml_w_in, ml_w_gate, ml_b_gate, ml_norm_g, ml_w_out, sc_w_in, sc_conv_w, sc_w_out,
           pl_w, pl_b, pl_scale, ft_w_out, ft_b_out):
    cond = jnp.concatenate([c_ctx[None, :], c, jnp.zeros((NCOND - 1 - DEC_BATCH, D), F32)], axis=0)
    mod = _modulation(cond, w_mod, b_mod)

    x = (x_prompt.reshape(NCTX, D), x_sample.reshape(NLAT, D), jnp.asarray(_pos_embed()))

    new_c = new_n = new_m = None
    for i in range(DEPTH):
        kind, j = i % 4, i // 4
        if kind == 0:
            perm = _gate_perm()
            wg = ml_w_gate[j][:, perm].astype(BF)
            bg = ml_b_gate[j][perm]
            qt, k, kt, vt, o, g, gt = _ml_proj(
                x, mod, i, ml_w_in[j], wg, wg.T, bg.reshape(1, -1), bg.reshape(-1, 1))
            scan_args = (qt, k, kt, vt, o, g, gt, ml_norm_g[j])
            uc, cst, nmst = _ml_scan(*scan_args)
            c0 = jnp.concatenate([jnp.swapaxes(state_C[:, j], -1, -2), state_n[:, j][..., None, :],
                                  jnp.zeros((DEC_BATCH, 2, H, DHA - DHV - 1, DQK), F32)], axis=-2)
            m0 = jnp.broadcast_to(state_m[:, j][..., None, None], (DEC_BATCH, 2, H, 1, LANES))
            (ul,) = _ml_scan(*scan_args, c0, m0)
            x = _out_proj(uc, ul, ml_w_out[j], jnp.zeros((D,), F32), x, mod, i, ln_g[i, 0], ln_b[i, 0])
            new_c = cst[:, None]
            new_n = nmst[..., 0, :DQK][:, None]
            new_m = nmst[..., 0, DQK][:, None]
        elif kind == 1:
            x = _sconv(x, mod, i, sc_w_in[j], sc_conv_w[j], sc_w_out[j],
                       ln_g[i, 0], ln_b[i, 0])
        elif kind == 2:
            x = _pool(x, mod, i, pl_w[j].astype(BF), pl_b[j], pl_scale[j], ln_g[i, 0], ln_b[i, 0])
        else:
            fc = _fourier(x, mod, i, SEQ, BATCH, 4, 0, lambda b: 0)
            fl = _fourier(x, mod, i, DEC_SEQ, DEC_BATCH, 1, NCTX, lambda b: 1 + b)
            x = _out_proj(fc, fl, ft_w_out[j], ft_b_out[j], x, mod, i, ln_g[i, 0], ln_b[i, 0])
        x = _mlp(x, mod, i, mlp_w1, mlp_w2, ln_g[i, 1], ln_b[i, 1],
                 split_out=(i == DEPTH - 1))

    y_prompt = x[0].reshape(BATCH, SEQ, D)
    y_sample = x[1].reshape(DEC_BATCH, DEC_SEQ, D)
    return y_prompt, y_sample, new_c, new_n, new_m
```

```python
import functools

import numpy as np
import jax
import jax.numpy as jnp
from jax import lax
from jax.experimental import pallas as pl
from jax.experimental.pallas import tpu as pltpu

D = 1024
BATCH, SEQ = 16, 256
DEC_BATCH, DEC_SEQ = 4, 1024
DEPTH = 4
GRID_W = 64
H = 8
DQK = 64
DHV = 128
QK = H * DQK
V = H * DHV
N_GROUPS = 4
GROUP_W = D // N_GROUPS
POOL_WINDOWS = (2, 4, 8, 16)
D_FF = 4 * D
ALPHA = (2.0 * DEPTH) ** 0.25
LN_EPS = 1e-5
F32 = jnp.float32
BF = jnp.bfloat16

NCTX = BATCH * SEQ
NLAT = DEC_BATCH * DEC_SEQ
NTOK = NCTX + NLAT
NCOND = 8
CHUNK = 256
SUBLANES, LANES = 8, 128
V7X_VMEM_BYTES = 64 * 1024 * 1024
VMEM_LIMIT = V7X_VMEM_BYTES - 8 * 1024 * 1024


def _cparams(n_axes):
    return pltpu.CompilerParams(dimension_semantics=("arbitrary",) * n_axes,
                                vmem_limit_bytes=VMEM_LIMIT)


def _mod_row(row0):
    return jnp.where(row0 < NCTX, 0, 1 + (row0 - NCTX) // DEC_SEQ)


def _seq_len(row0):
    return jnp.where(row0 < NCTX, SEQ, DEC_SEQ)


def _ln(z, g, b):
    mu = jnp.mean(z, axis=-1, keepdims=True)
    zc = z - mu
    var = jnp.mean(zc * zc, axis=-1, keepdims=True)
    return zc * lax.rsqrt(var + LN_EPS) * g + b


def _dot(a, b):
    return jnp.dot(a, b, preferred_element_type=F32)


def _dot_nt(a, b):
    return lax.dot_general(a, b, (((1,), (1,)), ((), ())), preferred_element_type=F32)


def _split2(x):
    hi = x.astype(BF)
    lo = (x - hi.astype(F32)).astype(BF)
    return hi, lo


def _split3(x):
    hi = x.astype(BF)
    r = x - hi.astype(F32)
    mid = r.astype(BF)
    lo = (r - mid.astype(F32)).astype(BF)
    return hi, mid, lo


def _full(shape):
    n = len(shape)
    return pl.BlockSpec(shape, lambda *_: (0,) * n)


def _x_specs(x, tm):
    if not isinstance(x, tuple):
        return [pl.BlockSpec((tm, D), lambda t: (t, 0))], [x]
    nc = NCTX // tm
    return ([pl.BlockSpec((tm, D), lambda t: (jnp.minimum(t, nc - 1), 0)),
             pl.BlockSpec((tm, D), lambda t: (jnp.maximum(t - nc, 0), 0)),
             pl.BlockSpec((tm, D), lambda t: (t % (DEC_SEQ // tm), 0))], list(x))


def _read_x(x_refs, tm):
    if len(x_refs) == 1:
        return x_refs[0][...]
    xc_ref, xl_ref, pe_ref = x_refs
    return jnp.where(pl.program_id(0) * tm < NCTX, xc_ref[...], xl_ref[...] + pe_ref[...])


def _mod_spec(layer, tm):
    return pl.BlockSpec((None, None, 6, D), lambda t: (layer, _mod_row(t * tm), 0, 0))


def _mod_kernel(c_ref, w_ref, b_ref, o_ref):
    c = c_ref[...]
    s = c * jax.nn.sigmoid(c)
    o_ref[...] = _dot(s.astype(BF), w_ref[...].astype(BF)) + b_ref[...]


def _modulation(cond, w_mod, b_mod):
    tn = 3 * D
    out = pl.pallas_call(
        _mod_kernel,
        grid=(DEPTH, 6 * D // tn),
        in_specs=[pl.BlockSpec((NCOND, D), lambda i, j: (0, 0)),
                  pl.BlockSpec((None, D, tn), lambda i, j: (i, 0, j)),
                  pl.BlockSpec((None, 1, tn), lambda i, j: (i, 0, j))],
        out_specs=pl.BlockSpec((None, NCOND, tn), lambda i, j: (i, 0, j)),
        out_shape=jax.ShapeDtypeStruct((DEPTH, NCOND, 6 * D), F32),
        compiler_params=_cparams(2),
        name="modulation",
    )(cond, w_mod, b_mod.reshape(DEPTH, 1, 6 * D))
    return out.reshape(DEPTH, NCOND, 6, D)


MLP_TM = 512
MLP_FC = 512
MLP_NF = D_FF // MLP_FC
MLP_NT = NTOK // MLP_TM


def _mlp_tile(s):
    return jnp.maximum(s - (MLP_NF - 1), 0)


def _mlp_kernel(mod_ref, w1_ref, w2_ref, g_ref, b_ref, x1_ref, *refs, nx):
    x_refs, o_refs, (w1s_ref, w2s_ref, acc_ref, h_ref) = refs[:nx], refs[nx:-4], refs[-4:]
    x_ref = x_refs[0]
    s = pl.program_id(0)
    is_ctx = _mlp_tile(s) * MLP_TM < NCTX
    m = mod_ref[...]

    def pre(x):
        return (x * (1.0 + m[4:5]) + m[3:4]).astype(BF)

    def hidden(h, w1c):
        return jnp.square(jnp.maximum(_dot(h, w1c), 0.0)).astype(BF)

    def result(x, acc):
        return _ln(ALPHA * x + m[5:6] * acc, g_ref[...], b_ref[...])

    @pl.when(s < MLP_NF)
    def _():
        @pl.when(s == 0)
        def _():
            h_ref[0:MLP_TM, :] = pre(x_ref[...])
            h_ref[MLP_TM:, :] = pre(x1_ref[...])

        w1c = w1_ref[...].astype(BF)
        w2c = w2_ref[...].astype(BF)
        part = _dot(hidden(h_ref[...], w1c), w2c)
        w1s_ref[s] = w1c
        w2s_ref[s] = w2c

        @pl.when(s == 0)
        def _():
            acc_ref[...] = part

        @pl.when(s > 0)
        def _():
            acc_ref[...] += part

        @pl.when(s == MLP_NF - 1)
        def _():
            o_refs[0][...] = result(x_ref[...], acc_ref[0:MLP_TM, :])

    @pl.when(s == MLP_NF)
    def _():
        o_refs[0][...] = result(x1_ref[...], acc_ref[MLP_TM:, :])

    @pl.when(s > MLP_NF)
    def _():
        x = x_ref[...] if nx == 1 else jnp.where(is_ctx, x_ref[...], x_refs[1][...])
        h = pre(x)
        acc = jnp.zeros((MLP_TM, D), F32)
        for f in range(MLP_NF):
            acc = acc + _dot(hidden(h, w1s_ref[f]), w2s_ref[f])
        res = result(x, acc)
        if len(o_refs) == 1:
            o_refs[0][...] = res
        else:
            @pl.when(is_ctx)
            def _():
                o_refs[0][...] = res

            @pl.when(jnp.logical_not(is_ctx))
            def _():
                o_refs[1][...] = res


def _mlp(x, mod, layer, w1, w2, ln_g, ln_b, split_out=False):
    tm = MLP_TM
    nc = NCTX // tm
    assert nc >= 2 and MLP_NT >= 2
    half_specs = [pl.BlockSpec((tm, D), lambda s: (jnp.minimum(_mlp_tile(s), nc - 1), 0)),
                  pl.BlockSpec((tm, D), lambda s: (jnp.maximum(_mlp_tile(s) - nc, 0), 0))]
    if isinstance(x, tuple):
        x_specs, x_args = half_specs, list(x)
    else:
        x_specs, x_args = [pl.BlockSpec((tm, D), lambda s: (_mlp_tile(s), 0))], [x]
    if split_out:
        out_specs = half_specs
        out_shape = [jax.ShapeDtypeStruct((NCTX, D), F32), jax.ShapeDtypeStruct((NLAT, D), F32)]
    else:
        out_specs = pl.BlockSpec((tm, D), lambda s: (_mlp_tile(s), 0))
        out_shape = jax.ShapeDtypeStruct((NTOK, D), F32)
    return pl.pallas_call(
        functools.partial(_mlp_kernel, nx=len(x_args)),
        grid=(MLP_NF - 1 + MLP_NT,),
        in_specs=[pl.BlockSpec((None, None, 6, D), lambda s: (layer, _mod_row(_mlp_tile(s) * tm), 0, 0)),
                  pl.BlockSpec((None, D, MLP_FC), lambda s: (layer, 0, jnp.minimum(s, MLP_NF - 1))),
                  pl.BlockSpec((None, MLP_FC, D), lambda s: (layer, jnp.minimum(s, MLP_NF - 1), 0)),
                  _full((1, D)), _full((1, D)),
                  pl.BlockSpec((tm, D), lambda s: (1, 0))] + x_specs,
        out_specs=out_specs,
        out_shape=out_shape,
        scratch_shapes=[pltpu.VMEM((MLP_NF, D, MLP_FC), BF),
                        pltpu.VMEM((MLP_NF, MLP_FC, D), BF),
                        pltpu.VMEM((2 * tm, D), F32),
                        pltpu.VMEM((2 * tm, D), BF)],
        compiler_params=_cparams(1),
        name="mlp",
    )(mod, w1, w2, ln_g.reshape(1, D), ln_b.reshape(1, D), x_args[0], *x_args)


OUT_TM = 1024
OUT_SUB = 256


def _out_kernel(uc_ref, ul_ref, w_f32_ref, bias_ref, mod_ref, g_ref, b_ref, *refs):
    x_refs, o_ref, w_ref = refs[:-2], refs[-2], refs[-1]

    @pl.when(pl.program_id(0) == 0)
    def _():
        w_ref[...] = w_f32_ref[...].astype(BF)

    is_ctx = pl.program_id(0) * OUT_TM < NCTX
    m = mod_ref[...]
    outs = []
    for r in range(0, OUT_TM, OUT_SUB):
        rs = slice(r, r + OUT_SUB)
        u = jnp.where(is_ctx, uc_ref[rs, :], ul_ref[rs, :])
        y = _dot(u, w_ref[...]) + bias_ref[...]
        if len(x_refs) == 1:
            x = x_refs[0][rs, :]
        else:
            x = jnp.where(is_ctx, x_refs[0][rs, :], x_refs[1][rs, :] + x_refs[2][rs, :])
        outs.append(_ln(ALPHA * x + m[2:3] * y, g_ref[...], b_ref[...]))
    for r, out in zip(range(0, OUT_TM, OUT_SUB), outs):
        o_ref[r:r + OUT_SUB, :] = out


def _out_proj(uc, ul, w, bias, x, mod, layer, ln_g, ln_b):
    tm = OUT_TM
    nc = NCTX // tm
    k = w.shape[0]
    x_specs, x_args = _x_specs(x, tm)
    return pl.pallas_call(
        _out_kernel,
        grid=(NTOK // tm,),
        in_specs=[pl.BlockSpec((tm, k), lambda t: (jnp.minimum(t, nc - 1), 0)),
                  pl.BlockSpec((tm, k), lambda t: (jnp.maximum(t - nc, 0), 0)),
                  pl.BlockSpec((k, D), lambda t: (0, 0), pipeline_mode=pl.Buffered(1)),
                  _full((1, D)),
                  _mod_spec(layer, tm),
                  _full((1, D)), _full((1, D))] + x_specs,
        out_specs=pl.BlockSpec((tm, D), lambda t: (t, 0)),
        out_shape=jax.ShapeDtypeStruct((NTOK, D), F32),
        scratch_shapes=[pltpu.VMEM((k, D), BF)],
        compiler_params=_cparams(1),
        name="out_proj",
    )(uc, ul, w, bias.reshape(1, D), mod, ln_g.reshape(1, D), ln_b.reshape(1, D), *x_args)


ML_TM = 512
N_HP = H // 2
SCAN_PAIRS = 2
NG = 8


def _log_sigmoid(g):
    return jnp.minimum(g, 0.0) - jnp.log1p(jnp.exp(-jnp.abs(g)))


def _ml_proj_kernel(mod_ref, w_in_ref, wg_ref, wgt_ref, bg_ref, bgt_ref, *refs):
    x_refs = refs[:-10]
    qt_ref, k_ref, kt_ref, vt_ref, o_ref, g_ref, gt_ref = refs[-10:-3]
    wt_ref, wk_ref, wo_ref = refs[-3:]
    r0, r1, r2 = QK, QK + 4 * NG, 2 * QK + 4 * NG

    @pl.when(pl.program_id(0) == 0)
    def _():
        wt_ref[0:r0, :] = w_in_ref[:, 0:QK].T.astype(BF)
        wt_ref[r0:r1, :] = wgt_ref[...]
        wt_ref[r1:r2, :] = w_in_ref[:, QK:2 * QK].T.astype(BF)
        wt_ref[r2:, :] = w_in_ref[:, 2 * QK:2 * QK + V].T.astype(BF)
        wk_ref[...] = w_in_ref[:, QK:2 * QK].astype(BF)
        wo_ref[...] = w_in_ref[:, 2 * QK + V:].astype(BF)

    m = mod_ref[...]
    h = (_read_x(x_refs, ML_TM) * (1.0 + m[1:2]) + m[0:1]).astype(BF)
    tr = _dot_nt(wt_ref[...], h)
    qt = tr[:r0].astype(BF)
    kt = (tr[r1:r2] * (DQK ** -0.5)).astype(BF)
    vt = tr[r2:].astype(BF)
    k = (_dot(h, wk_ref[...]) * (DQK ** -0.5)).astype(BF)
    o = _dot(h, wo_ref[...])
    g = _dot(h, wg_ref[...]) + bg_ref[...]
    col = lax.broadcasted_iota(jnp.int32, g.shape, 1)
    g = jnp.where((col & 2) != 0, _log_sigmoid(g), g)
    gt = tr[r0:r1] + bgt_ref[...]
    row = lax.broadcasted_iota(jnp.int32, gt.shape, 0)
    gt = jnp.where((row & 2) != 0, _log_sigmoid(gt), gt)
    qt_ref[...] = qt
    k_ref[...] = k
    kt_ref[...] = kt
    vt_ref[...] = vt
    o_ref[...] = o
    for hp in range(N_HP):
        g_ref[hp] = g[:, hp * NG:(hp + 1) * NG]
        gt_ref[hp] = gt[hp * NG:(hp + 1) * NG, :]


def _ml_proj(x, mod, layer, w_in, wg, wgt, bg, bgt):
    tm = ML_TM
    res = lambda shape: pl.BlockSpec(shape, lambda t: (0,) * len(shape), pipeline_mode=pl.Buffered(1))
    x_specs, x_args = _x_specs(x, tm)
    return pl.pallas_call(
        _ml_proj_kernel,
        grid=(NTOK // tm,),
        in_specs=[_mod_spec(layer, tm),
                  res((D, 2 * QK + 2 * V)), res((D, 4 * NG)), res((4 * NG, D)),
                  res((1, 4 * NG)), res((4 * NG, 1))] + x_specs,
        out_specs=[pl.BlockSpec((QK, tm), lambda t: (0, t)),
                   pl.BlockSpec((tm, QK), lambda t: (t, 0)),
                   pl.BlockSpec((QK, tm), lambda t: (0, t)),
                   pl.BlockSpec((V, tm), lambda t: (0, t)),
                   pl.BlockSpec((tm, V), lambda t: (t, 0)),
                   pl.BlockSpec((N_HP, tm, NG), lambda t: (0, t, 0)),
                   pl.BlockSpec((N_HP, NG, tm), lambda t: (0, 0, t))],
        out_shape=[jax.ShapeDtypeStruct((QK, NTOK), BF),
                   jax.ShapeDtypeStruct((NTOK, QK), BF),
                   jax.ShapeDtypeStruct((QK, NTOK), BF),
                   jax.ShapeDtypeStruct((V, NTOK), BF),
                   jax.ShapeDtypeStruct((NTOK, V), F32),
                   jax.ShapeDtypeStruct((N_HP, NTOK, NG), F32),
                   jax.ShapeDtypeStruct((N_HP, NG, NTOK), F32)],
        scratch_shapes=[pltpu.VMEM((2 * QK + 4 * NG + V, D), BF),
                        pltpu.VMEM((D, QK), BF),
                        pltpu.VMEM((D, V), BF)],
        compiler_params=_cparams(1),
        name="mlstm_proj",
    )(mod, w_in, wg, wgt, bg, bgt, *x_args)


DHA = DHV + SUBLANES


def _scan_chunk(k_c, kt_c, qt_c, vta, rr_col, rr_row, b_row, m, cta, d, need_state):
    L = k_c.shape[0]
    si = lax.broadcasted_iota(jnp.int32, (L, L), 0)
    ti = lax.broadcasted_iota(jnp.int32, (L, L), 1)
    mask = (si <= ti) if d == 0 else (si >= ti)
    rrm = jnp.where(mask, rr_col, -jnp.inf)
    g = jnp.maximum(jnp.max(rrm, axis=0, keepdims=True), m)
    st = _dot(k_c, qt_c) * jnp.exp(rrm - g)
    num = _dot(vta[:DHV], st.astype(BF))
    den = jnp.sum(st, axis=0, keepdims=True)
    if cta is not None:
        a = jnp.exp(m - g)
        qc = _dot(cta.astype(BF), qt_c)
        num = num + a * qc[:DHV]
        den = den + a * qc[DHV:DHV + 1]
    mt = b_row + g
    ht = num * (1.0 / jnp.maximum(jnp.abs(den), jnp.exp(-mt)))
    if not need_state:
        return ht, None, None
    last = L - 1 if d == 0 else 0
    m_new = mt[:, last:last + 1]
    b_last = b_row[:, last:last + 1]
    w = jnp.exp(b_last + rr_row - m_new)
    kw = (kt_c.astype(F32) * w).astype(BF)
    if cta is None:
        return ht, (_dot_nt(kw, vta[:DHV]), _dot_nt(vta[DHV:], kw)), m_new
    c_new = jnp.exp(b_last + m - m_new) * cta + _dot_nt(vta, kw)
    return ht, c_new, m_new


def _scan_kernel(*refs, is_ctx):
    if is_ctx:
        qt_ref, k_ref, kt_ref, vt_ref, o_ref, g_ref, gt_ref, ng_ref, u_ref, cst_ref, nmst_ref = refs
    else:
        qt_ref, k_ref, kt_ref, vt_ref, o_ref, g_ref, gt_ref, ng_ref, c0_ref, m0_ref, u_ref = refs
    L = CHUNK
    nch = DEC_SEQ // L
    ri = lax.broadcasted_iota(jnp.int32, (L, L), 0)
    ci = lax.broadcasted_iota(jnp.int32, (L, L), 1)
    lower = jnp.where(ci <= ri, 1.0, 0.0).astype(BF)
    upper = jnp.where(ci >= ri, 1.0, 0.0).astype(BF)
    sub = lax.broadcasted_iota(jnp.int32, (DHA - DHV, L), 0)
    ones_rows = jnp.where(sub == 0, 1.0, 0.0).astype(BF)

    rr_cols, rr_rows, b_rows = [], [], []
    for pp in range(SCAN_PAIRS):
        tparts = _split3(jnp.concatenate([gt_ref[pp, :, c * L:(c + 1) * L] for c in range(nch)], axis=0))
        b_row_all = (sum(_dot(p, upper) for p in tparts), sum(_dot(p, lower) for p in tparts))
        rr_cols.append([])
        rr_rows.append([])
        b_rows.append([])
        for c in range(nch):
            gc = g_ref[pp, c * L:(c + 1) * L, :]
            gtc = gt_ref[pp, :, c * L:(c + 1) * L]
            gparts = _split3(gc)
            b_col = (sum(_dot(lower, p) for p in gparts), sum(_dot(upper, p) for p in gparts))
            b_row = tuple(b[c * NG:(c + 1) * NG, :] for b in b_row_all)
            rr_cols[pp].append([gc[:, 0:NG - 2] - b[:, 2:NG] for b in b_col])
            rr_rows[pp].append([gtc[0:NG - 2, :] - b[2:NG, :] for b in b_row])
            b_rows[pp].append(b_row)

    states, us = [], []
    for hl in range(2 * SCAN_PAIRS):
        pp, hh = divmod(hl, 2)
        hsum = [None] * nch
        for d in range(2):
            ig_i, lf_i = d * 4 + hh, d * 4 + 2 + hh
            if is_ctx:
                m, cta = jnp.zeros((1, 1), F32), None
            else:
                m = m0_ref[d, hl][:, 0:1]
                cta = c0_ref[d, hl]
            order = range(nch) if d == 0 else range(nch - 1, -1, -1)
            for n, c in enumerate(order):
                sl = slice(c * L, (c + 1) * L)
                hs = slice(hl * DQK, (hl + 1) * DQK)
                vta = jnp.concatenate([vt_ref[hl * DHV:(hl + 1) * DHV, sl], ones_rows], axis=0)
                need_state = is_ctx or n < nch - 1
                hcur, c_new, m_new = _scan_chunk(
                    k_ref[sl, hs], kt_ref[hs, sl], qt_ref[hs, sl], vta,
                    rr_cols[pp][c][d][:, ig_i:ig_i + 1], rr_rows[pp][c][d][ig_i:ig_i + 1, :],
                    b_rows[pp][c][d][lf_i:lf_i + 1, :], m, cta, d, need_state)
                hsum[c] = hcur if d == 0 else hsum[c] + hcur
                if is_ctx:
                    states.append(((c, d, hl), c_new, m_new))
                else:
                    m, cta = m_new, c_new

        for c in range(nch):
            sl = slice(c * L, (c + 1) * L)
            mu = jnp.mean(hsum[c], axis=0, keepdims=True)
            hc = hsum[c] - mu
            var = jnp.mean(hc * hc, axis=0, keepdims=True)
            hn = hc * lax.rsqrt(var + LN_EPS) * ng_ref[hl * DHV:(hl + 1) * DHV, :]
            og = jax.nn.sigmoid(o_ref[sl, hl * DHV:(hl + 1) * DHV])
            us.append(((sl, slice(hl * DHV, (hl + 1) * DHV)), (hn.T * og).astype(BF)))

    for idx, u in us:
        u_ref[idx] = u
    for idx, (c_fin, n_fin), m_fin in states:
        cst_ref[idx] = c_fin
        nmst_ref[idx] = jnp.concatenate([n_fin, jnp.broadcast_to(m_fin, (SUBLANES, LANES - DQK))], axis=1)


def _ml_scan(qt, k, kt, vt, o, g, gt, norm_g, c0aug=None, m0b=None):
    is_ctx = c0aug is None
    tt = DEC_SEQ
    toff = 0 if is_ctx else NCTX // tt
    nt = (NCTX if is_ctx else NLAT) // tt
    nh = 2 * SCAN_PAIRS
    in_specs = [pl.BlockSpec((nh * DQK, tt), lambda t, p: (p, toff + t)),
                pl.BlockSpec((tt, nh * DQK), lambda t, p: (toff + t, p)),
                pl.BlockSpec((nh * DQK, tt), lambda t, p: (p, toff + t)),
                pl.BlockSpec((nh * DHV, tt), lambda t, p: (p, toff + t)),
                pl.BlockSpec((tt, nh * DHV), lambda t, p: (toff + t, p)),
                pl.BlockSpec((SCAN_PAIRS, tt, NG), lambda t, p: (p, toff + t, 0)),
                pl.BlockSpec((SCAN_PAIRS, NG, tt), lambda t, p: (p, 0, toff + t)),
                pl.BlockSpec((nh * DHV, 1), lambda t, p: (p, 0))]
    args = [qt, k, kt, vt, o, g, gt, norm_g.reshape(V, 1)]
    u_spec = pl.BlockSpec((tt, nh * DHV), lambda t, p: (t, p))
    u_shape = jax.ShapeDtypeStruct((nt * tt, V), BF)
    if is_ctx:
        nseq = tt // SEQ
        out_specs = [u_spec,
                     pl.BlockSpec((nseq, 2, nh, DQK, DHV), lambda t, p: (t, 0, p, 0, 0)),
                     pl.BlockSpec((nseq, 2, nh, SUBLANES, LANES), lambda t, p: (t, 0, p, 0, 0))]
        out_shape = [u_shape,
                     jax.ShapeDtypeStruct((BATCH, 2, H, DQK, DHV), F32),
                     jax.ShapeDtypeStruct((BATCH, 2, H, SUBLANES, LANES), F32)]
    else:
        in_specs += [pl.BlockSpec((None, 2, nh, DHA, DQK), lambda t, p: (t, 0, p, 0, 0)),
                     pl.BlockSpec((None, 2, nh, 1, LANES), lambda t, p: (t, 0, p, 0, 0))]
        args += [c0aug, m0b]
        out_specs = [u_spec]
        out_shape = [u_shape]
    return pl.pallas_call(
        functools.partial(_scan_kernel, is_ctx=is_ctx),
        grid=(nt, N_HP // SCAN_PAIRS),
        in_specs=in_specs,
        out_specs=out_specs,
        out_shape=out_shape,
        compiler_params=_cparams(2),
        name="mlstm_scan_ctx" if is_ctx else "mlstm_scan_lat",
    )(*args)


SC_TM = 512
SC_HALO = SUBLANES


def _sconv_kernel(x_ref, prev_ref, next_ref, mod_ref, w_in_f32_ref, cw_ref, w_out_f32_ref, g_ref, b_ref, o_ref,
                  e_ref, w_in_ref, w_out_ref):
    tm = SC_TM

    @pl.when(pl.program_id(0) == 0)
    def _():
        w_in_ref[...] = w_in_f32_ref[...].astype(BF)
        w_out_ref[...] = w_out_f32_ref[...].astype(BF)

    row0 = pl.program_id(0) * tm
    slen = _seq_len(row0)
    pos = (row0 + lax.broadcasted_iota(jnp.int32, (tm, 1), 0)) & (slen - 1)
    m = mod_ref[...]
    x = x_ref[...]
    xe = jnp.concatenate([prev_ref[...], x, next_ref[...]], axis=0)
    he = (xe * (1.0 + m[1:2]) + m[0:1]).astype(BF)
    cu = _dot(he, w_in_ref[:, D:2 * D]) * _dot(he, w_in_ref[:, 2 * D:3 * D])
    bg = _dot(he[SC_HALO:SC_HALO + tm], w_in_ref[:, 0:D])
    e_ref[...] = cu
    prev = jnp.where(pos == 0, 0.0, e_ref[SC_HALO - 1:SC_HALO - 1 + tm, :])
    nxt = jnp.where(pos == slen - 1, 0.0, e_ref[SC_HALO + 1:SC_HALO + 1 + tm, :])
    conv = cw_ref[0:1, :] * prev + cw_ref[1:2, :] * e_ref[SC_HALO:SC_HALO + tm, :] + cw_ref[2:3, :] * nxt
    y = _dot((bg * conv).astype(BF), w_out_ref[...])
    o_ref[...] = _ln(ALPHA * x + m[2:3] * y, g_ref[...], b_ref[...])


def _sconv(x, mod, layer, w_in, conv_w, w_out, ln_g, ln_b):
    tm = SC_TM
    hb = tm // SC_HALO
    nhb = NTOK // SC_HALO
    return pl.pallas_call(
        _sconv_kernel,
        grid=(NTOK // tm,),
        in_specs=[pl.BlockSpec((tm, D), lambda t: (t, 0)),
                  pl.BlockSpec((SC_HALO, D), lambda t: (jnp.maximum(t * hb - 1, 0), 0)),
                  pl.BlockSpec((SC_HALO, D), lambda t: (jnp.minimum((t + 1) * hb, nhb - 1), 0)),
                  _mod_spec(layer, tm),
                  pl.BlockSpec((D, 3 * D), lambda t: (0, 0), pipeline_mode=pl.Buffered(1)),
                  _full((3, D)),
                  pl.BlockSpec((D, D), lambda t: (0, 0), pipeline_mode=pl.Buffered(1)),
                  _full((1, D)), _full((1, D))],
        out_specs=pl.BlockSpec((tm, D), lambda t: (t, 0)),
        out_shape=jax.ShapeDtypeStruct((NTOK, D), F32),
        scratch_shapes=[pltpu.VMEM((tm + 2 * SC_HALO, D), F32),
                        pltpu.VMEM((D, 3 * D), BF),
                        pltpu.VMEM((D, D), BF)],
        compiler_params=_cparams(1),
        name="sconv",
    )(x, x, x, mod, w_in, conv_w, w_out, ln_g.reshape(1, D), ln_b.reshape(1, D))


PL_SEG = 256
PL_NSEG = 4
PL_TM = PL_SEG * PL_NSEG
PL_HALO = 16
PL_STRIDE = PL_SEG + 2 * PL_HALO


def _pool_kernel(x_ref, prev_ref, next_ref, mod_ref, w_ref, pb_ref, ps_ref, g_ref, b_ref, o_ref,
                 e_ref, s2_ref, s4_ref, s8_ref):
    tm, seg, hl, gw = PL_TM, PL_SEG, PL_HALO, GROUP_W
    row0 = pl.program_id(0) * tm
    slen = _seq_len(row0)
    m = mod_ref[...]
    x = x_ref[...]
    h = x * (1.0 + m[1:2]) + m[0:1]
    hp = prev_ref[...] * (1.0 + m[1:2]) + m[0:1]
    hn = next_ref[...] * (1.0 + m[1:2]) + m[0:1]
    n0 = PL_NSEG * PL_STRIDE
    for i in range(PL_NSEG):
        base = i * PL_STRIDE
        pos0 = (row0 + i * seg) & (slen - 1)
        before = hp if i == 0 else h[i * seg - hl:i * seg]
        after = hn if i == PL_NSEG - 1 else h[(i + 1) * seg:(i + 1) * seg + hl]
        e_ref[base:base + hl, :] = jnp.where(pos0 != 0, before, 0.0)
        e_ref[base + hl:base + hl + seg, :] = h[i * seg:(i + 1) * seg]
        e_ref[base + hl + seg:base + PL_STRIDE, :] = jnp.where(pos0 + seg != slen, after, 0.0)
    e_ref[n0:, :] = jnp.zeros((SUBLANES, D), F32)
    n2, n4, n8 = n0, n0 - SUBLANES, n0 - 2 * SUBLANES
    s2_ref[...] = e_ref[0:n2, :] + e_ref[1:n2 + 1, :]
    s4_ref[...] = s2_ref[0:n4, gw:] + s2_ref[2:n4 + 2, gw:]
    s8_ref[...] = s4_ref[0:n8, gw:] + s4_ref[4:n8 + 4, gw:]

    def rows(ref, off, lanes):
        return jnp.concatenate([ref[i * PL_STRIDE + off:i * PL_STRIDE + off + seg, lanes]
                                for i in range(PL_NSEG)], axis=0)

    tots = [rows(s2_ref, hl - 1, slice(0, gw)),
            rows(s4_ref, hl - 2, slice(0, gw)),
            rows(s8_ref, hl - 4, slice(0, gw)),
            rows(s8_ref, hl - 8, slice(gw, 2 * gw)) + rows(s8_ref, hl, slice(gw, 2 * gw))]
    pos = (row0 + lax.broadcasted_iota(jnp.int32, (tm, 1), 0)) & (slen - 1)
    ys = []
    for gi, win in enumerate(POOL_WINDOWS):
        cs = slice(gi * gw, (gi + 1) * gw)
        back = win // 2
        cnt = jnp.minimum(pos + (win - back), slen) - jnp.maximum(pos - back, 0)
        p = tots[gi] / cnt.astype(F32) - h[:, cs]
        ys.append(_dot(p.astype(BF), w_ref[gi]))
    y = (jnp.concatenate(ys, axis=1) + pb_ref[...]) * ps_ref[...]
    o_ref[...] = _ln(ALPHA * x + m[2:3] * y, g_ref[...], b_ref[...])


def _pool(x, mod, layer, w, pb, ps, ln_g, ln_b):
    assert POOL_WINDOWS == (2, 4, 8, 16) and PL_HALO >= POOL_WINDOWS[-1] // 2
    assert SEQ % PL_SEG == 0 and DEC_SEQ % PL_SEG == 0
    tm = PL_TM
    hb = tm // PL_HALO
    nhb = NTOK // PL_HALO
    n0 = PL_NSEG * PL_STRIDE
    return pl.pallas_call(
        _pool_kernel,
        grid=(NTOK // tm,),
        in_specs=[pl.BlockSpec((tm, D), lambda t: (t, 0)),
                  pl.BlockSpec((PL_HALO, D), lambda t: (jnp.maximum(t * hb - 1, 0), 0)),
                  pl.BlockSpec((PL_HALO, D), lambda t: (jnp.minimum((t + 1) * hb, nhb - 1), 0)),
                  _mod_spec(layer, tm),
                  _full((N_GROUPS, GROUP_W, GROUP_W)),
                  _full((1, D)), _full((1, D)), _full((1, D)), _full((1, D))],
        out_specs=pl.BlockSpec((tm, D), lambda t: (t, 0)),
        out_shape=jax.ShapeDtypeStruct((NTOK, D), F32),
        scratch_shapes=[pltpu.VMEM((n0 + SUBLANES, D), F32),
                        pltpu.VMEM((n0, D), F32),
                        pltpu.VMEM((n0 - SUBLANES, D - GROUP_W), F32),
                        pltpu.VMEM((n0 - 2 * SUBLANES, D - 2 * GROUP_W), F32)],
        compiler_params=_cparams(1),
        name="pool",
    )(x, x, x, mod, w, pb.reshape(1, D), ps.reshape(1, D), ln_g.reshape(1, D), ln_b.reshape(1, D))


FT_TK = 256
FT_PAD = SUBLANES


def _dft_mats(n, scale):
    k = np.arange(n, dtype=np.int64)
    ang = 2.0 * np.pi * ((k[:, None] * k[None, :]) % n).astype(np.float64) / n
    return np.stack([np.cos(ang) * scale, np.sin(ang) * scale]).astype(np.float32)


def _dot_x3(a_parts, b_parts):
    (ah, al), (bh, bl) = a_parts, b_parts
    return _dot(ah, bh) + (_dot(ah, bl) + _dot(al, bh))


def _fourier_kernel(x_ref, mod_ref, cs_ref, cc_ref, w_f32_ref, bias_ref, lg_ref, lb_ref, o_ref, w_ref,
                    *, s, tk, nsq):
    @pl.when(pl.program_id(0) == 0)
    def _():
        w_ref[...] = w_f32_ref[...].astype(BF)

    m = mod_ref[...]
    ccp = _split2(cc_ref[0])
    scp = _split2(cc_ref[1])
    ri = lax.broadcasted_iota(jnp.int32, (tk, tk), 0)
    ci = lax.broadcasted_iota(jnp.int32, (tk, tk), 1)
    anti = jnp.where(ri + ci == tk - 1, 1.0, 0.0).astype(BF)
    blocks = []
    for sq in range(nsq):
        base = sq * s
        hparts = _split2(x_ref[base:base + s, :] * (1.0 + m[1:2]) + m[0:1])
        for j in range(s // (2 * tk)):
            ac = _dot_x3(_split2(cs_ref[0, j]), hparts)
            as_ = _dot_x3(_split2(cs_ref[1, j]), hparts)
            direct, mirror = [], []
            for gi in range(N_GROUPS):
                cs = slice(gi * GROUP_W, (gi + 1) * GROUP_W)
                p = _dot_x3(_split2(ac[:, cs]), ccp)
                q = _dot_x3(_split2(as_[:, cs]), scp)
                direct.append((p - q)[0:tk])
                mirror.append((p + q)[1:tk + 1])
            blocks.append((base + j * tk, jnp.concatenate(direct, axis=1).astype(BF)))
            mir = jnp.concatenate(mirror, axis=1).astype(BF)
            blocks.append((base + s - (j + 1) * tk, _dot(anti, mir).astype(BF)))
    outs = []
    for r0, blk in blocks:
        y = _dot(blk, w_ref[...]) + bias_ref[...]
        outs.append((r0, _ln(ALPHA * x_ref[r0:r0 + tk, :] + m[2:3] * y, lg_ref[...], lb_ref[...])))
    for r0, out in outs:
        o_ref[r0:r0 + tk, :] = out


def _half_dft_mats(s, tk):
    full = _dft_mats(s, s ** -0.5)
    rows = (np.arange(s // (2 * tk))[:, None] * tk + np.arange(tk + FT_PAD)[None, :]) % s
    return full[:, rows, :]


def _fourier(x, mod, layer, s, nseq, nsq, row_off, mod_row_fn, w, bias, ln_g, ln_b):
    tk = min(FT_TK, s // 2)
    blk = nsq * s
    blk_off = row_off // blk
    cs = jnp.asarray(_half_dft_mats(s, tk))
    cc = jnp.asarray(_dft_mats(GROUP_W, GROUP_W ** -0.5))
    return pl.pallas_call(
        functools.partial(_fourier_kernel, s=s, tk=tk, nsq=nsq),
        grid=(nseq // nsq,),
        in_specs=[pl.BlockSpec((blk, D), lambda b: (blk_off + b, 0)),
                  pl.BlockSpec((None, None, 6, D), lambda b: (layer, mod_row_fn(b), 0, 0)),
                  pl.BlockSpec(cs.shape, lambda b: (0, 0, 0, 0), pipeline_mode=pl.Buffered(1)),
                  _full((2, GROUP_W, GROUP_W)),
                  pl.BlockSpec((D, D), lambda b: (0, 0), pipeline_mode=pl.Buffered(1)),
                  _full((1, D)), _full((1, D)), _full((1, D))],
        out_specs=pl.BlockSpec((blk, D), lambda b: (b, 0)),
        out_shape=jax.ShapeDtypeStruct((nseq * s, D), F32),
        scratch_shapes=[pltpu.VMEM((D, D), BF)],
        compiler_params=_cparams(1),
        name="fourier_%d" % s,
    )(x, mod, cs, cc, w, bias.reshape(1, D), ln_g.reshape(1, D), ln_b.reshape(1, D))


def _pos_embed():
    rows = DEC_SEQ // GRID_W
    quarter = D // 4
    omega = 1.0 / (10000.0 ** (np.arange(quarter, dtype=np.float64) / quarter))
    rr, cc = np.meshgrid(np.arange(rows, dtype=np.float64), np.arange(GRID_W, dtype=np.float64), indexing="ij")
    er = rr.reshape(-1, 1) * omega
    ec = cc.reshape(-1, 1) * omega
    return np.concatenate([np.sin(er), np.cos(er), np.sin(ec), np.cos(ec)], axis=-1).astype(np.float32)


def _gate_perm():
    perm = np.zeros(4 * H, dtype=np.int32)
    for hp in range(N_HP):
        for d in range(2):
            for j in range(2):
                for hh in range(2):
                    perm[hp * NG + d * 4 + j * 2 + hh] = d * 2 * H + j * H + 2 * hp + hh
    return perm


def kernel(x_prompt, x_sample, state_C, state_n, state_m, c, c_ctx, w_mod, b_mod, ln_g, ln_b, mlp_w1, mlp_w2,
           ml_w_in, ml_w_gate, ml_b_gate, ml_norm_g, ml_w_out, sc_w_in, sc_conv_w, sc_w_out,
           pl_w, pl_b, pl_scale, ft_w_out, ft_b_out):
    cond = jnp.concatenate([c_ctx[None, :], c, jnp.zeros((NCOND - 1 - DEC_BATCH, D), F32)], axis=0)
    mod = _modulation(cond, w_mod, b_mod)

    x = (x_prompt.reshape(NCTX, D), x_sample.reshape(NLAT, D), jnp.asarray(_pos_embed()))

    new_c = new_n = new_m = None
    for i in range(DEPTH):
        kind, j = i % 4, i // 4
        if kind == 0:
            perm = _gate_perm()
            wg = ml_w_gate[j][:, perm].astype(BF)
            bg = ml_b_gate[j][perm]
            qt, k, kt, vt, o, g, gt = _ml_proj(
                x, mod, i, ml_w_in[j], wg, wg.T, bg.reshape(1, -1), bg.reshape(-1, 1))
            scan_args = (qt, k, kt, vt, o, g, gt, ml_norm_g[j])
            uc, cst, nmst = _ml_scan(*scan_args)
            c0 = jnp.concatenate([jnp.swapaxes(state_C[:, j], -1, -2), state_n[:, j][..., None, :],
                                  jnp.zeros((DEC_BATCH, 2, H, DHA - DHV - 1, DQK), F32)], axis=-2)
            m0 = jnp.broadcast_to(state_m[:, j][..., None, None], (DEC_BATCH, 2, H, 1, LANES))
            (ul,) = _ml_scan(*scan_args, c0, m0)
            x = _out_proj(uc, ul, ml_w_out[j], jnp.zeros((D,), F32), x, mod, i, ln_g[i, 0], ln_b[i, 0])
            new_c = cst[:, None]
            new_n = nmst[..., 0, :DQK][:, None]
            new_m = nmst[..., 0, DQK][:, None]
        elif kind == 1:
            x = _sconv(x, mod, i, sc_w_in[j], sc_conv_w[j], sc_w_out[j],
                       ln_g[i, 0], ln_b[i, 0])
        elif kind == 2:
            x = _pool(x, mod, i, pl_w[j].astype(BF), pl_b[j], pl_scale[j], ln_g[i, 0], ln_b[i, 0])
        else:
            tail = (ft_w_out[j], ft_b_out[j], ln_g[i, 0], ln_b[i, 0])
            x = (_fourier(x, mod, i, SEQ, BATCH, 4, 0, lambda b: 0, *tail),
                 _fourier(x, mod, i, DEC_SEQ, DEC_BATCH, 1, NCTX, lambda b: 1 + b, *tail))
        x = _mlp(x, mod, i, mlp_w1, mlp_w2, ln_g[i, 1], ln_b[i, 1],
                 split_out=(i == DEPTH - 1))

    y_prompt = x[0].reshape(BATCH, SEQ, D)
    y_sample = x[1].reshape(DEC_BATCH, DEC_SEQ, D)
    return y_prompt, y_sample, new_c, new_n, new_m
```

```python
import functools

import numpy as np
import jax
import jax.numpy as jnp
from jax import lax
from jax.experimental import pallas as pl
from jax.experimental.pallas import tpu as pltpu

D = 1024
BATCH, SEQ = 16, 256
DEC_BATCH, DEC_SEQ = 4, 1024
DEPTH = 4
GRID_W = 64
H = 8
DQK = 64
DHV = 128
QK = H * DQK
V = H * DHV
N_GROUPS = 4
GROUP_W = D // N_GROUPS
POOL_WINDOWS = (2, 4, 8, 16)
D_FF = 4 * D
ALPHA = (2.0 * DEPTH) ** 0.25
LN_EPS = 1e-5
F32 = jnp.float32
BF = jnp.bfloat16

NCTX = BATCH * SEQ
NLAT = DEC_BATCH * DEC_SEQ
NTOK = NCTX + NLAT
NCOND = 8
CHUNK = 256
SUBLANES, LANES = 8, 128
V7X_VMEM_BYTES = 64 * 1024 * 1024
VMEM_LIMIT = V7X_VMEM_BYTES - 8 * 1024 * 1024


def _cparams(n_axes):
    return pltpu.CompilerParams(dimension_semantics=("arbitrary",) * n_axes,
                                vmem_limit_bytes=VMEM_LIMIT)


def _mod_row(row0):
    return jnp.where(row0 < NCTX, 0, 1 + (row0 - NCTX) // DEC_SEQ)


def _seq_len(row0):
    return jnp.where(row0 < NCTX, SEQ, DEC_SEQ)


def _ln(z, g, b):
    mu = jnp.mean(z, axis=-1, keepdims=True)
    zc = z - mu
    var = jnp.mean(zc * zc, axis=-1, keepdims=True)
    return zc * lax.rsqrt(var + LN_EPS) * g + b


def _dot(a, b):
    return jnp.dot(a, b, preferred_element_type=F32)


def _dot_nt(a, b):
    return lax.dot_general(a, b, (((1,), (1,)), ((), ())), preferred_element_type=F32)


def _split2(x):
    hi = x.astype(BF)
    lo = (x - hi.astype(F32)).astype(BF)
    return hi, lo


def _split3(x):
    hi = x.astype(BF)
    r = x - hi.astype(F32)
    mid = r.astype(BF)
    lo = (r - mid.astype(F32)).astype(BF)
    return hi, mid, lo


def _full(shape):
    n = len(shape)
    return pl.BlockSpec(shape, lambda *_: (0,) * n)


def _x_specs(x, tm):
    if not isinstance(x, tuple):
        return [pl.BlockSpec((tm, D), lambda t: (t, 0))], [x]
    nc = NCTX // tm
    return ([pl.BlockSpec((tm, D), lambda t: (jnp.minimum(t, nc - 1), 0)),
             pl.BlockSpec((tm, D), lambda t: (jnp.maximum(t - nc, 0), 0)),
             pl.BlockSpec((tm, D), lambda t: (t % (DEC_SEQ // tm), 0))], list(x))


def _read_x(x_refs, tm):
    if len(x_refs) == 1:
        return x_refs[0][...]
    xc_ref, xl_ref, pe_ref = x_refs
    return jnp.where(pl.program_id(0) * tm < NCTX, xc_ref[...], xl_ref[...] + pe_ref[...])


def _mod_spec(layer, tm):
    return pl.BlockSpec((None, None, 6, D), lambda t: (layer, _mod_row(t * tm), 0, 0))


def _mod_kernel(c_ref, w_ref, b_ref, o_ref):
    c = c_ref[...]
    s = c * jax.nn.sigmoid(c)
    o_ref[...] = _dot(s.astype(BF), w_ref[...].astype(BF)) + b_ref[...]


def _modulation(cond, w_mod, b_mod):
    tn = 3 * D
    out = pl.pallas_call(
        _mod_kernel,
        grid=(DEPTH, 6 * D // tn),
        in_specs=[pl.BlockSpec((NCOND, D), lambda i, j: (0, 0)),
                  pl.BlockSpec((None, D, tn), lambda i, j: (i, 0, j)),
                  pl.BlockSpec((None, 1, tn), lambda i, j: (i, 0, j))],
        out_specs=pl.BlockSpec((None, NCOND, tn), lambda i, j: (i, 0, j)),
        out_shape=jax.ShapeDtypeStruct((DEPTH, NCOND, 6 * D), F32),
        compiler_params=_cparams(2),
        name="modulation",
    )(cond, w_mod, b_mod.reshape(DEPTH, 1, 6 * D))
    return out.reshape(DEPTH, NCOND, 6, D)


MLP_TM = 512
MLP_FC = 512
MLP_NF = D_FF // MLP_FC
MLP_NT = NTOK // MLP_TM


def _mlp_tile(s):
    return jnp.maximum(s - (MLP_NF - 1), 0)


def _mlp_kernel(mod_ref, w1_ref, w2_ref, g_ref, b_ref, x1_ref, x_ref, *refs):
    o_refs, (w1s_ref, w2s_ref, acc_ref, h_ref) = refs[:-4], refs[-4:]
    s = pl.program_id(0)
    m = mod_ref[...]

    def pre(x):
        return (x * (1.0 + m[4:5]) + m[3:4]).astype(BF)

    def hidden(h, w1c):
        return jnp.square(jnp.maximum(_dot(h, w1c), 0.0)).astype(BF)

    def result(x, acc):
        return _ln(ALPHA * x + m[5:6] * acc, g_ref[...], b_ref[...])

    @pl.when(s < MLP_NF)
    def _():
        @pl.when(s == 0)
        def _():
            h_ref[0:MLP_TM, :] = pre(x_ref[...])
            h_ref[MLP_TM:, :] = pre(x1_ref[...])

        w1c = w1_ref[...].astype(BF)
        w2c = w2_ref[...].astype(BF)
        part = _dot(hidden(h_ref[...], w1c), w2c)
        w1s_ref[s] = w1c
        w2s_ref[s] = w2c

        @pl.when(s == 0)
        def _():
            acc_ref[...] = part

        @pl.when(s > 0)
        def _():
            acc_ref[...] += part

        @pl.when(s == MLP_NF - 1)
        def _():
            o_refs[0][...] = result(x_ref[...], acc_ref[0:MLP_TM, :])

    @pl.when(s == MLP_NF)
    def _():
        o_refs[0][...] = result(x1_ref[...], acc_ref[MLP_TM:, :])

    @pl.when(s > MLP_NF)
    def _():
        x = x_ref[...]
        h = pre(x)
        acc = jnp.zeros((MLP_TM, D), F32)
        for f in range(MLP_NF):
            acc = acc + _dot(hidden(h, w1s_ref[f]), w2s_ref[f])
        res = result(x, acc)
        if len(o_refs) == 1:
            o_refs[0][...] = res
        else:
            is_ctx = _mlp_tile(s) * MLP_TM < NCTX

            @pl.when(is_ctx)
            def _():
                o_refs[0][...] = res

            @pl.when(jnp.logical_not(is_ctx))
            def _():
                o_refs[1][...] = res


def _mlp(x, mod, layer, w1, w2, ln_g, ln_b, split_out=False):
    tm = MLP_TM
    nc = NCTX // tm
    assert nc >= 2 and MLP_NT >= 2
    if split_out:
        out_specs = [pl.BlockSpec((tm, D), lambda s: (jnp.minimum(_mlp_tile(s), nc - 1), 0)),
                     pl.BlockSpec((tm, D), lambda s: (jnp.maximum(_mlp_tile(s) - nc, 0), 0))]
        out_shape = [jax.ShapeDtypeStruct((NCTX, D), F32), jax.ShapeDtypeStruct((NLAT, D), F32)]
    else:
        out_specs = pl.BlockSpec((tm, D), lambda s: (_mlp_tile(s), 0))
        out_shape = jax.ShapeDtypeStruct((NTOK, D), F32)
    return pl.pallas_call(
        _mlp_kernel,
        grid=(MLP_NF - 1 + MLP_NT,),
        in_specs=[pl.BlockSpec((None, None, 6, D), lambda s: (layer, _mod_row(_mlp_tile(s) * tm), 0, 0)),
                  pl.BlockSpec((None, D, MLP_FC), lambda s: (layer, 0, jnp.minimum(s, MLP_NF - 1))),
                  pl.BlockSpec((None, MLP_FC, D), lambda s: (layer, jnp.minimum(s, MLP_NF - 1), 0)),
                  _full((1, D)), _full((1, D)),
                  pl.BlockSpec((tm, D), lambda s: (1, 0)),
                  pl.BlockSpec((tm, D), lambda s: (_mlp_tile(s), 0))],
        out_specs=out_specs,
        out_shape=out_shape,
        scratch_shapes=[pltpu.VMEM((MLP_NF, D, MLP_FC), BF),
                        pltpu.VMEM((MLP_NF, MLP_FC, D), BF),
                        pltpu.VMEM((2 * tm, D), F32),
                        pltpu.VMEM((2 * tm, D), BF)],
        compiler_params=_cparams(1),
        name="mlp",
    )(mod, w1, w2, ln_g.reshape(1, D), ln_b.reshape(1, D), x, x)


OUT_TM = 1024
OUT_SUB = 256


def _out_kernel(uc_ref, ul_ref, w_f32_ref, bias_ref, mod_ref, g_ref, b_ref, *refs):
    x_refs, o_ref, w_ref = refs[:-2], refs[-2], refs[-1]

    @pl.when(pl.program_id(0) == 0)
    def _():
        w_ref[...] = w_f32_ref[...].astype(BF)

    is_ctx = pl.program_id(0) * OUT_TM < NCTX
    m = mod_ref[...]
    outs = []
    for r in range(0, OUT_TM, OUT_SUB):
        rs = slice(r, r + OUT_SUB)
        u = jnp.where(is_ctx, uc_ref[rs, :], ul_ref[rs, :])
        y = _dot(u, w_ref[...]) + bias_ref[...]
        if len(x_refs) == 1:
            x = x_refs[0][rs, :]
        else:
            x = jnp.where(is_ctx, x_refs[0][rs, :], x_refs[1][rs, :] + x_refs[2][rs, :])
        outs.append(_ln(ALPHA * x + m[2:3] * y, g_ref[...], b_ref[...]))
    for r, out in zip(range(0, OUT_TM, OUT_SUB), outs):
        o_ref[r:r + OUT_SUB, :] = out


def _out_proj(uc, ul, w, bias, x, mod, layer, ln_g, ln_b):
    tm = OUT_TM
    nc = NCTX // tm
    k = w.shape[0]
    x_specs, x_args = _x_specs(x, tm)
    return pl.pallas_call(
        _out_kernel,
        grid=(NTOK // tm,),
        in_specs=[pl.BlockSpec((tm, k), lambda t: (jnp.minimum(t, nc - 1), 0)),
                  pl.BlockSpec((tm, k), lambda t: (jnp.maximum(t - nc, 0), 0)),
                  pl.BlockSpec((k, D), lambda t: (0, 0), pipeline_mode=pl.Buffered(1)),
                  _full((1, D)),
                  _mod_spec(layer, tm),
                  _full((1, D)), _full((1, D))] + x_specs,
        out_specs=pl.BlockSpec((tm, D), lambda t: (t, 0)),
        out_shape=jax.ShapeDtypeStruct((NTOK, D), F32),
        scratch_shapes=[pltpu.VMEM((k, D), BF)],
        compiler_params=_cparams(1),
        name="out_proj",
    )(uc, ul, w, bias.reshape(1, D), mod, ln_g.reshape(1, D), ln_b.reshape(1, D), *x_args)


ML_TM = 512
N_HP = H // 2
SCAN_PAIRS = 2
NG = 8


def _log_sigmoid(g):
    return jnp.minimum(g, 0.0) - jnp.log1p(jnp.exp(-jnp.abs(g)))


def _ml_proj_kernel(mod_ref, w_in_ref, wg_ref, wgt_ref, bg_ref, bgt_ref, *refs):
    x_refs = refs[:-10]
    qt_ref, k_ref, kt_ref, vt_ref, o_ref, g_ref, gt_ref = refs[-10:-3]
    wt_ref, wk_ref, wo_ref = refs[-3:]
    r0, r1, r2 = QK, QK + 4 * NG, 2 * QK + 4 * NG

    @pl.when(pl.program_id(0) == 0)
    def _():
        wt_ref[0:r0, :] = w_in_ref[:, 0:QK].T.astype(BF)
        wt_ref[r0:r1, :] = wgt_ref[...]
        wt_ref[r1:r2, :] = w_in_ref[:, QK:2 * QK].T.astype(BF)
        wt_ref[r2:, :] = w_in_ref[:, 2 * QK:2 * QK + V].T.astype(BF)
        wk_ref[...] = w_in_ref[:, QK:2 * QK].astype(BF)
        wo_ref[...] = w_in_ref[:, 2 * QK + V:].astype(BF)

    m = mod_ref[...]
    h = (_read_x(x_refs, ML_TM) * (1.0 + m[1:2]) + m[0:1]).astype(BF)
    tr = _dot_nt(wt_ref[...], h)
    qt = tr[:r0].astype(BF)
    kt = (tr[r1:r2] * (DQK ** -0.5)).astype(BF)
    vt = tr[r2:].astype(BF)
    k = (_dot(h, wk_ref[...]) * (DQK ** -0.5)).astype(BF)
    o = _dot(h, wo_ref[...])
    g = _dot(h, wg_ref[...]) + bg_ref[...]
    col = lax.broadcasted_iota(jnp.int32, g.shape, 1)
    g = jnp.where((col & 2) != 0, _log_sigmoid(g), g)
    gt = tr[r0:r1] + bgt_ref[...]
    row = lax.broadcasted_iota(jnp.int32, gt.shape, 0)
    gt = jnp.where((row & 2) != 0, _log_sigmoid(gt), gt)
    qt_ref[...] = qt
    k_ref[...] = k
    kt_ref[...] = kt
    vt_ref[...] = vt
    o_ref[...] = o
    for hp in range(N_HP):
        g_ref[hp] = g[:, hp * NG:(hp + 1) * NG]
        gt_ref[hp] = gt[hp * NG:(hp + 1) * NG, :]


def _ml_proj(x, mod, layer, w_in, wg, wgt, bg, bgt):
    tm = ML_TM
    res = lambda shape: pl.BlockSpec(shape, lambda t: (0,) * len(shape), pipeline_mode=pl.Buffered(1))
    x_specs, x_args = _x_specs(x, tm)
    return pl.pallas_call(
        _ml_proj_kernel,
        grid=(NTOK // tm,),
        in_specs=[_mod_spec(layer, tm),
                  res((D, 2 * QK + 2 * V)), res((D, 4 * NG)), res((4 * NG, D)),
                  res((1, 4 * NG)), res((4 * NG, 1))] + x_specs,
        out_specs=[pl.BlockSpec((QK, tm), lambda t: (0, t)),
                   pl.BlockSpec((tm, QK), lambda t: (t, 0)),
                   pl.BlockSpec((QK, tm), lambda t: (0, t)),
                   pl.BlockSpec((V, tm), lambda t: (0, t)),
                   pl.BlockSpec((tm, V), lambda t: (t, 0)),
                   pl.BlockSpec((N_HP, tm, NG), lambda t: (0, t, 0)),
                   pl.BlockSpec((N_HP, NG, tm), lambda t: (0, 0, t))],
        out_shape=[jax.ShapeDtypeStruct((QK, NTOK), BF),
                   jax.ShapeDtypeStruct((NTOK, QK), BF),
                   jax.ShapeDtypeStruct((QK, NTOK), BF),
                   jax.ShapeDtypeStruct((V, NTOK), BF),
                   jax.ShapeDtypeStruct((NTOK, V), F32),
                   jax.ShapeDtypeStruct((N_HP, NTOK, NG), F32),
                   jax.ShapeDtypeStruct((N_HP, NG, NTOK), F32)],
        scratch_shapes=[pltpu.VMEM((2 * QK + 4 * NG + V, D), BF),
                        pltpu.VMEM((D, QK), BF),
                        pltpu.VMEM((D, V), BF)],
        compiler_params=_cparams(1),
        name="mlstm_proj",
    )(mod, w_in, wg, wgt, bg, bgt, *x_args)


DHA = DHV + SUBLANES


def _scan_chunk(k_c, kt_c, qt_c, vta, rr_col, rr_row, b_row, m, cta, d, need_state):
    L = k_c.shape[0]
    si = lax.broadcasted_iota(jnp.int32, (L, L), 0)
    ti = lax.broadcasted_iota(jnp.int32, (L, L), 1)
    mask = (si <= ti) if d == 0 else (si >= ti)
    rrm = jnp.where(mask, rr_col, -jnp.inf)
    g = jnp.maximum(jnp.max(rrm, axis=0, keepdims=True), m)
    st = _dot(k_c, qt_c) * jnp.exp(rrm - g)
    num = _dot(vta[:DHV], st.astype(BF))
    den = jnp.sum(st, axis=0, keepdims=True)
    if cta is not None:
        a = jnp.exp(m - g)
        qc = _dot(cta.astype(BF), qt_c)
        num = num + a * qc[:DHV]
        den = den + a * qc[DHV:DHV + 1]
    mt = b_row + g
    ht = num * (1.0 / jnp.maximum(jnp.abs(den), jnp.exp(-mt)))
    if not need_state:
        return ht, None, None
    last = L - 1 if d == 0 else 0
    m_new = mt[:, last:last + 1]
    b_last = b_row[:, last:last + 1]
    w = jnp.exp(b_last + rr_row - m_new)
    kw = (kt_c.astype(F32) * w).astype(BF)
    if cta is None:
        return ht, (_dot_nt(kw, vta[:DHV]), _dot_nt(vta[DHV:], kw)), m_new
    c_new = jnp.exp(b_last + m - m_new) * cta + _dot_nt(vta, kw)
    return ht, c_new, m_new


def _scan_kernel(*refs, is_ctx):
    if is_ctx:
        qt_ref, k_ref, kt_ref, vt_ref, o_ref, g_ref, gt_ref, ng_ref, u_ref, cst_ref, nmst_ref = refs
    else:
        qt_ref, k_ref, kt_ref, vt_ref, o_ref, g_ref, gt_ref, ng_ref, c0_ref, m0_ref, u_ref = refs
    L = CHUNK
    nch = DEC_SEQ // L
    ri = lax.broadcasted_iota(jnp.int32, (L, L), 0)
    ci = lax.broadcasted_iota(jnp.int32, (L, L), 1)
    lower = jnp.where(ci <= ri, 1.0, 0.0).astype(BF)
    upper = jnp.where(ci >= ri, 1.0, 0.0).astype(BF)
    sub = lax.broadcasted_iota(jnp.int32, (DHA - DHV, L), 0)
    ones_rows = jnp.where(sub == 0, 1.0, 0.0).astype(BF)

    rr_cols, rr_rows, b_rows = [], [], []
    for pp in range(SCAN_PAIRS):
        tparts = _split3(jnp.concatenate([gt_ref[pp, :, c * L:(c + 1) * L] for c in range(nch)], axis=0))
        b_row_all = (sum(_dot(p, upper) for p in tparts), sum(_dot(p, lower) for p in tparts))
        rr_cols.append([])
        rr_rows.append([])
        b_rows.append([])
        for c in range(nch):
            gc = g_ref[pp, c * L:(c + 1) * L, :]
            gtc = gt_ref[pp, :, c * L:(c + 1) * L]
            gparts = _split3(gc)
            b_col = (sum(_dot(lower, p) for p in gparts), sum(_dot(upper, p) for p in gparts))
            b_row = tuple(b[c * NG:(c + 1) * NG, :] for b in b_row_all)
            rr_cols[pp].append([gc[:, 0:NG - 2] - b[:, 2:NG] for b in b_col])
            rr_rows[pp].append([gtc[0:NG - 2, :] - b[2:NG, :] for b in b_row])
            b_rows[pp].append(b_row)

    states, us = [], []
    for hl in range(2 * SCAN_PAIRS):
        pp, hh = divmod(hl, 2)
        hsum = [None] * nch
        for d in range(2):
            ig_i, lf_i = d * 4 + hh, d * 4 + 2 + hh
            if is_ctx:
                m, cta = jnp.zeros((1, 1), F32), None
            else:
                m = m0_ref[d, hl][:, 0:1]
                cta = c0_ref[d, hl]
            order = range(nch) if d == 0 else range(nch - 1, -1, -1)
            for n, c in enumerate(order):
                sl = slice(c * L, (c + 1) * L)
                hs = slice(hl * DQK, (hl + 1) * DQK)
                vta = jnp.concatenate([vt_ref[hl * DHV:(hl + 1) * DHV, sl], ones_rows], axis=0)
                need_state = is_ctx or n < nch - 1
                hcur, c_new, m_new = _scan_chunk(
                    k_ref[sl, hs], kt_ref[hs, sl], qt_ref[hs, sl], vta,
                    rr_cols[pp][c][d][:, ig_i:ig_i + 1], rr_rows[pp][c][d][ig_i:ig_i + 1, :],
                    b_rows[pp][c][d][lf_i:lf_i + 1, :], m, cta, d, need_state)
                hsum[c] = hcur if d == 0 else hsum[c] + hcur
                if is_ctx:
                    states.append(((c, d, hl), c_new, m_new))
                else:
                    m, cta = m_new, c_new

        for c in range(nch):
            sl = slice(c * L, (c + 1) * L)
            mu = jnp.mean(hsum[c], axis=0, keepdims=True)
            hc = hsum[c] - mu
            var = jnp.mean(hc * hc, axis=0, keepdims=True)
            hn = hc * lax.rsqrt(var + LN_EPS) * ng_ref[hl * DHV:(hl + 1) * DHV, :]
            og = jax.nn.sigmoid(o_ref[sl, hl * DHV:(hl + 1) * DHV])
            us.append(((sl, slice(hl * DHV, (hl + 1) * DHV)), (hn.T * og).astype(BF)))

    for idx, u in us:
        u_ref[idx] = u
    for idx, (c_fin, n_fin), m_fin in states:
        cst_ref[idx] = c_fin
        nmst_ref[idx] = jnp.concatenate([n_fin, jnp.broadcast_to(m_fin, (SUBLANES, LANES - DQK))], axis=1)


def _ml_scan(qt, k, kt, vt, o, g, gt, norm_g, c0aug=None, m0b=None):
    is_ctx = c0aug is None
    tt = DEC_SEQ
    toff = 0 if is_ctx else NCTX // tt
    nt = (NCTX if is_ctx else NLAT) // tt
    nh = 2 * SCAN_PAIRS
    in_specs = [pl.BlockSpec((nh * DQK, tt), lambda t, p: (p, toff + t)),
                pl.BlockSpec((tt, nh * DQK), lambda t, p: (toff + t, p)),
                pl.BlockSpec((nh * DQK, tt), lambda t, p: (p, toff + t)),
                pl.BlockSpec((nh * DHV, tt), lambda t, p: (p, toff + t)),
                pl.BlockSpec((tt, nh * DHV), lambda t, p: (toff + t, p)),
                pl.BlockSpec((SCAN_PAIRS, tt, NG), lambda t, p: (p, toff + t, 0)),
                pl.BlockSpec((SCAN_PAIRS, NG, tt), lambda t, p: (p, 0, toff + t)),
                pl.BlockSpec((nh * DHV, 1), lambda t, p: (p, 0))]
    args = [qt, k, kt, vt, o, g, gt, norm_g.reshape(V, 1)]
    u_spec = pl.BlockSpec((tt, nh * DHV), lambda t, p: (t, p))
    u_shape = jax.ShapeDtypeStruct((nt * tt, V), BF)
    if is_ctx:
        nseq = tt // SEQ
        out_specs = [u_spec,
                     pl.BlockSpec((nseq, 2, nh, DQK, DHV), lambda t, p: (t, 0, p, 0, 0)),
                     pl.BlockSpec((nseq, 2, nh, SUBLANES, LANES), lambda t, p: (t, 0, p, 0, 0))]
        out_shape = [u_shape,
                     jax.ShapeDtypeStruct((BATCH, 2, H, DQK, DHV), F32),
                     jax.ShapeDtypeStruct((BATCH, 2, H, SUBLANES, LANES), F32)]
    else:
        in_specs += [pl.BlockSpec((None, 2, nh, DHA, DQK), lambda t, p: (t, 0, p, 0, 0)),
                     pl.BlockSpec((None, 2, nh, 1, LANES), lambda t, p: (t, 0, p, 0, 0))]
        args += [c0aug, m0b]
        out_specs = [u_spec]
        out_shape = [u_shape]
    return pl.pallas_call(
        functools.partial(_scan_kernel, is_ctx=is_ctx),
        grid=(nt, N_HP // SCAN_PAIRS),
        in_specs=in_specs,
        out_specs=out_specs,
        out_shape=out_shape,
        compiler_params=_cparams(2),
        name="mlstm_scan_ctx" if is_ctx else "mlstm_scan_lat",
    )(*args)


SC_TM = 512
SC_HALO = SUBLANES


def _sconv_kernel(x_ref, prev_ref, next_ref, mod_ref, w_in_f32_ref, cw_ref, w_out_f32_ref, g_ref, b_ref, o_ref,
                  e_ref, w_in_ref, w_out_ref):
    tm = SC_TM

    @pl.when(pl.program_id(0) == 0)
    def _():
        w_in_ref[...] = w_in_f32_ref[...].astype(BF)
        w_out_ref[...] = w_out_f32_ref[...].astype(BF)

    row0 = pl.program_id(0) * tm
    slen = _seq_len(row0)
    pos = (row0 + lax.broadcasted_iota(jnp.int32, (tm, 1), 0)) & (slen - 1)
    m = mod_ref[...]
    x = x_ref[...]
    xe = jnp.concatenate([prev_ref[...], x, next_ref[...]], axis=0)
    he = (xe * (1.0 + m[1:2]) + m[0:1]).astype(BF)
    cu = _dot(he, w_in_ref[:, D:2 * D]) * _dot(he, w_in_ref[:, 2 * D:3 * D])
    bg = _dot(he[SC_HALO:SC_HALO + tm], w_in_ref[:, 0:D])
    e_ref[...] = cu
    prev = jnp.where(pos == 0, 0.0, e_ref[SC_HALO - 1:SC_HALO - 1 + tm, :])
    nxt = jnp.where(pos == slen - 1, 0.0, e_ref[SC_HALO + 1:SC_HALO + 1 + tm, :])
    conv = cw_ref[0:1, :] * prev + cw_ref[1:2, :] * e_ref[SC_HALO:SC_HALO + tm, :] + cw_ref[2:3, :] * nxt
    y = _dot((bg * conv).astype(BF), w_out_ref[...])
    o_ref[...] = _ln(ALPHA * x + m[2:3] * y, g_ref[...], b_ref[...])


def _sconv(x, mod, layer, w_in, conv_w, w_out, ln_g, ln_b):
    tm = SC_TM
    hb = tm // SC_HALO
    nhb = NTOK // SC_HALO
    return pl.pallas_call(
        _sconv_kernel,
        grid=(NTOK // tm,),
        in_specs=[pl.BlockSpec((tm, D), lambda t: (t, 0)),
                  pl.BlockSpec((SC_HALO, D), lambda t: (jnp.maximum(t * hb - 1, 0), 0)),
                  pl.BlockSpec((SC_HALO, D), lambda t: (jnp.minimum((t + 1) * hb, nhb - 1), 0)),
                  _mod_spec(layer, tm),
                  pl.BlockSpec((D, 3 * D), lambda t: (0, 0), pipeline_mode=pl.Buffered(1)),
                  _full((3, D)),
                  pl.BlockSpec((D, D), lambda t: (0, 0), pipeline_mode=pl.Buffered(1)),
                  _full((1, D)), _full((1, D))],
        out_specs=pl.BlockSpec((tm, D), lambda t: (t, 0)),
        out_shape=jax.ShapeDtypeStruct((NTOK, D), F32),
        scratch_shapes=[pltpu.VMEM((tm + 2 * SC_HALO, D), F32),
                        pltpu.VMEM((D, 3 * D), BF),
                        pltpu.VMEM((D, D), BF)],
        compiler_params=_cparams(1),
        name="sconv",
    )(x, x, x, mod, w_in, conv_w, w_out, ln_g.reshape(1, D), ln_b.reshape(1, D))


PL_SEG = 256
PL_NSEG = 4
PL_TM = PL_SEG * PL_NSEG
PL_HALO = 16
PL_STRIDE = PL_SEG + 2 * PL_HALO


def _pool_kernel(x_ref, prev_ref, next_ref, mod_ref, w_ref, pb_ref, ps_ref, g_ref, b_ref, o_ref,
                 e_ref, s2_ref, s4_ref, s8_ref):
    tm, seg, hl, gw = PL_TM, PL_SEG, PL_HALO, GROUP_W
    row0 = pl.program_id(0) * tm
    slen = _seq_len(row0)
    m = mod_ref[...]
    x = x_ref[...]
    h = x * (1.0 + m[1:2]) + m[0:1]
    hp = prev_ref[...] * (1.0 + m[1:2]) + m[0:1]
    hn = next_ref[...] * (1.0 + m[1:2]) + m[0:1]
    n0 = PL_NSEG * PL_STRIDE
    for i in range(PL_NSEG):
        base = i * PL_STRIDE
        pos0 = (row0 + i * seg) & (slen - 1)
        before = hp if i == 0 else h[i * seg - hl:i * seg]
        after = hn if i == PL_NSEG - 1 else h[(i + 1) * seg:(i + 1) * seg + hl]
        e_ref[base:base + hl, :] = jnp.where(pos0 != 0, before, 0.0)
        e_ref[base + hl:base + hl + seg, :] = h[i * seg:(i + 1) * seg]
        e_ref[base + hl + seg:base + PL_STRIDE, :] = jnp.where(pos0 + seg != slen, after, 0.0)
    e_ref[n0:, :] = jnp.zeros((SUBLANES, D), F32)
    n2, n4, n8 = n0, n0 - SUBLANES, n0 - 2 * SUBLANES
    s2_ref[...] = e_ref[0:n2, :] + e_ref[1:n2 + 1, :]
    s4_ref[...] = s2_ref[0:n4, gw:] + s2_ref[2:n4 + 2, gw:]
    s8_ref[...] = s4_ref[0:n8, gw:] + s4_ref[4:n8 + 4, gw:]

    def rows(ref, off, lanes):
        return jnp.concatenate([ref[i * PL_STRIDE + off:i * PL_STRIDE + off + seg, lanes]
                                for i in range(PL_NSEG)], axis=0)

    tots = [rows(s2_ref, hl - 1, slice(0, gw)),
            rows(s4_ref, hl - 2, slice(0, gw)),
            rows(s8_ref, hl - 4, slice(0, gw)),
            rows(s8_ref, hl - 8, slice(gw, 2 * gw)) + rows(s8_ref, hl, slice(gw, 2 * gw))]
    pos = (row0 + lax.broadcasted_iota(jnp.int32, (tm, 1), 0)) & (slen - 1)
    ys = []
    for gi, win in enumerate(POOL_WINDOWS):
        cs = slice(gi * gw, (gi + 1) * gw)
        back = win // 2
        cnt = jnp.minimum(pos + (win - back), slen) - jnp.maximum(pos - back, 0)
        p = tots[gi] / cnt.astype(F32) - h[:, cs]
        ys.append(_dot(p.astype(BF), w_ref[gi]))
    y = (jnp.concatenate(ys, axis=1) + pb_ref[...]) * ps_ref[...]
    o_ref[...] = _ln(ALPHA * x + m[2:3] * y, g_ref[...], b_ref[...])


def _pool(x, mod, layer, w, pb, ps, ln_g, ln_b):
    assert POOL_WINDOWS == (2, 4, 8, 16) and PL_HALO >= POOL_WINDOWS[-1] // 2
    assert SEQ % PL_SEG == 0 and DEC_SEQ % PL_SEG == 0
    tm = PL_TM
    hb = tm // PL_HALO
    nhb = NTOK // PL_HALO
    n0 = PL_NSEG * PL_STRIDE
    return pl.pallas_call(
        _pool_kernel,
        grid=(NTOK // tm,),
        in_specs=[pl.BlockSpec((tm, D), lambda t: (t, 0)),
                  pl.BlockSpec((PL_HALO, D), lambda t: (jnp.maximum(t * hb - 1, 0), 0)),
                  pl.BlockSpec((PL_HALO, D), lambda t: (jnp.minimum((t + 1) * hb, nhb - 1), 0)),
                  _mod_spec(layer, tm),
                  _full((N_GROUPS, GROUP_W, GROUP_W)),
                  _full((1, D)), _full((1, D)), _full((1, D)), _full((1, D))],
        out_specs=pl.BlockSpec((tm, D), lambda t: (t, 0)),
        out_shape=jax.ShapeDtypeStruct((NTOK, D), F32),
        scratch_shapes=[pltpu.VMEM((n0 + SUBLANES, D), F32),
                        pltpu.VMEM((n0, D), F32),
                        pltpu.VMEM((n0 - SUBLANES, D - GROUP_W), F32),
                        pltpu.VMEM((n0 - 2 * SUBLANES, D - 2 * GROUP_W), F32)],
        compiler_params=_cparams(1),
        name="pool",
    )(x, x, x, mod, w, pb.reshape(1, D), ps.reshape(1, D), ln_g.reshape(1, D), ln_b.reshape(1, D))


FT_TM = 1024
FT_TK = 256
FT_PAD = SUBLANES


def _dft_mats(n, scale):
    k = np.arange(n, dtype=np.int64)
    ang = 2.0 * np.pi * ((k[:, None] * k[None, :]) % n).astype(np.float64) / n
    return np.stack([np.cos(ang) * scale, np.sin(ang) * scale]).astype(np.float32)


def _dot_x3(a_parts, b_parts):
    (ah, al), (bh, bl) = a_parts, b_parts
    return _dot(ah, bh) + (_dot(ah, bl) + _dot(al, bh))


def _fourier_tile(x_ref, m, cs_ref, ccp, scp, w_ref, bias_ref, lg_ref, lb_ref, o_ref, s, tk):
    nsq = FT_TM // s
    ri = lax.broadcasted_iota(jnp.int32, (tk, tk), 0)
    ci = lax.broadcasted_iota(jnp.int32, (tk, tk), 1)
    anti = jnp.where(ri + ci == tk - 1, 1.0, 0.0).astype(BF)
    blocks = []
    for sq in range(nsq):
        base = sq * s
        hparts = _split2(x_ref[base:base + s, :] * (1.0 + m[1:2]) + m[0:1])
        for j in range(s // (2 * tk)):
            ac = _dot_x3(_split2(cs_ref[0, j]), hparts)
            as_ = _dot_x3(_split2(cs_ref[1, j]), hparts)
            direct, mirror = [], []
            for gi in range(N_GROUPS):
                cs = slice(gi * GROUP_W, (gi + 1) * GROUP_W)
                p = _dot_x3(_split2(ac[:, cs]), ccp)
                q = _dot_x3(_split2(as_[:, cs]), scp)
                direct.append((p - q)[0:tk])
                mirror.append((p + q)[1:tk + 1])
            blocks.append((base + j * tk, jnp.concatenate(direct, axis=1).astype(BF)))
            mir = jnp.concatenate(mirror, axis=1).astype(BF)
            blocks.append((base + s - (j + 1) * tk, _dot(anti, mir).astype(BF)))
    outs = []
    for r0, blk in blocks:
        y = _dot(blk, w_ref[...]) + bias_ref[...]
        outs.append((r0, _ln(ALPHA * x_ref[r0:r0 + tk, :] + m[2:3] * y, lg_ref[...], lb_ref[...])))
    for r0, out in outs:
        o_ref[r0:r0 + tk, :] = out


def _fourier_kernel(x_ref, mod_ref, csc_ref, csl_ref, cc_ref, w_f32_ref, bias_ref, lg_ref, lb_ref, o_ref, w_ref):
    @pl.when(pl.program_id(0) == 0)
    def _():
        w_ref[...] = w_f32_ref[...].astype(BF)

    m = mod_ref[...]
    ccp = _split2(cc_ref[0])
    scp = _split2(cc_ref[1])
    is_ctx = pl.program_id(0) * FT_TM < NCTX

    @pl.when(is_ctx)
    def _():
        _fourier_tile(x_ref, m, csc_ref, ccp, scp, w_ref, bias_ref, lg_ref, lb_ref, o_ref, SEQ, _ft_tk(SEQ))

    @pl.when(jnp.logical_not(is_ctx))
    def _():
        _fourier_tile(x_ref, m, csl_ref, ccp, scp, w_ref, bias_ref, lg_ref, lb_ref, o_ref, DEC_SEQ, _ft_tk(DEC_SEQ))


def _ft_tk(s):
    return min(FT_TK, s // 2)


def _half_dft_mats(s, tk):
    full = _dft_mats(s, s ** -0.5)
    rows = (np.arange(s // (2 * tk))[:, None] * tk + np.arange(tk + FT_PAD)[None, :]) % s
    return full[:, rows, :]


def _fourier(x, mod, layer, w, bias, ln_g, ln_b):
    tm = FT_TM
    assert tm % SEQ == 0 and tm % DEC_SEQ == 0 and NCTX % tm == 0
    csc = jnp.asarray(_half_dft_mats(SEQ, _ft_tk(SEQ)))
    csl = jnp.asarray(_half_dft_mats(DEC_SEQ, _ft_tk(DEC_SEQ)))
    cc = jnp.asarray(_dft_mats(GROUP_W, GROUP_W ** -0.5))
    const = lambda shape: pl.BlockSpec(shape, lambda b: (0,) * len(shape), pipeline_mode=pl.Buffered(1))
    return pl.pallas_call(
        _fourier_kernel,
        grid=(NTOK // tm,),
        in_specs=[pl.BlockSpec((tm, D), lambda b: (b, 0)),
                  _mod_spec(layer, tm),
                  const(csc.shape), const(csl.shape), const(cc.shape), const((D, D)),
                  _full((1, D)), _full((1, D)), _full((1, D))],
        out_specs=pl.BlockSpec((tm, D), lambda b: (b, 0)),
        out_shape=jax.ShapeDtypeStruct((NTOK, D), F32),
        scratch_shapes=[pltpu.VMEM((D, D), BF)],
        compiler_params=_cparams(1),
        name="fourier",
    )(x, mod, csc, csl, cc, w, bias.reshape(1, D), ln_g.reshape(1, D), ln_b.reshape(1, D))


def _pos_embed():
    rows = DEC_SEQ // GRID_W
    quarter = D // 4
    omega = 1.0 / (10000.0 ** (np.arange(quarter, dtype=np.float64) / quarter))
    rr, cc = np.meshgrid(np.arange(rows, dtype=np.float64), np.arange(GRID_W, dtype=np.float64), indexing="ij")
    er = rr.reshape(-1, 1) * omega
    ec = cc.reshape(-1, 1) * omega
    return np.concatenate([np.sin(er), np.cos(er), np.sin(ec), np.cos(ec)], axis=-1).astype(np.float32)


def _gate_perm():
    perm = np.zeros(4 * H, dtype=np.int32)
    for hp in range(N_HP):
        for d in range(2):
            for j in range(2):
                for hh in range(2):
                    perm[hp * NG + d * 4 + j * 2 + hh] = d * 2 * H + j * H + 2 * hp + hh
    return perm


def kernel(x_prompt, x_sample, state_C, state_n, state_m, c, c_ctx, w_mod, b_mod, ln_g, ln_b, mlp_w1, mlp_w2,
           ml_w_in, ml_w_gate, ml_b_gate, ml_norm_g, ml_w_out, sc_w_in, sc_conv_w, sc_w_out,
           pl_w, pl_b, pl_scale, ft_w_out, ft_b_out):
    cond = jnp.concatenate([c_ctx[None, :], c, jnp.zeros((NCOND - 1 - DEC_BATCH, D), F32)], axis=0)
    mod = _modulation(cond, w_mod, b_mod)

    x = (x_prompt.reshape(NCTX, D), x_sample.reshape(NLAT, D), jnp.asarray(_pos_embed()))

    new_c = new_n = new_m = None
    for i in range(DEPTH):
        kind, j = i % 4, i // 4
        if kind == 0:
            perm = _gate_perm()
            wg = ml_w_gate[j][:, perm].astype(BF)
            bg = ml_b_gate[j][perm]
            qt, k, kt, vt, o, g, gt = _ml_proj(
                x, mod, i, ml_w_in[j], wg, wg.T, bg.reshape(1, -1), bg.reshape(-1, 1))
            scan_args = (qt, k, kt, vt, o, g, gt, ml_norm_g[j])
            uc, cst, nmst = _ml_scan(*scan_args)
            c0 = jnp.concatenate([jnp.swapaxes(state_C[:, j], -1, -2), state_n[:, j][..., None, :],
                                  jnp.zeros((DEC_BATCH, 2, H, DHA - DHV - 1, DQK), F32)], axis=-2)
            m0 = jnp.broadcast_to(state_m[:, j][..., None, None], (DEC_BATCH, 2, H, 1, LANES))
            (ul,) = _ml_scan(*scan_args, c0, m0)
            x = _out_proj(uc, ul, ml_w_out[j], jnp.zeros((D,), F32), x, mod, i, ln_g[i, 0], ln_b[i, 0])
            new_c = cst[:, None]
            new_n = nmst[..., 0, :DQK][:, None]
            new_m = nmst[..., 0, DQK][:, None]
        elif kind == 1:
            x = _sconv(x, mod, i, sc_w_in[j], sc_conv_w[j], sc_w_out[j],
                       ln_g[i, 0], ln_b[i, 0])
        elif kind == 2:
            x = _pool(x, mod, i, pl_w[j].astype(BF), pl_b[j], pl_scale[j], ln_g[i, 0], ln_b[i, 0])
        else:
            x = _fourier(x, mod, i, ft_w_out[j], ft_b_out[j], ln_g[i, 0], ln_b[i, 0])
        x = _mlp(x, mod, i, mlp_w1, mlp_w2, ln_g[i, 1], ln_b[i, 1],
                 split_out=(i == DEPTH - 1))

    y_prompt = x[0].reshape(BATCH, SEQ, D)
    y_sample = x[1].reshape(DEC_BATCH, DEC_SEQ, D)
    return y_prompt, y_sample, new_c, new_n, new_m
```

```python
import functools

import numpy as np
import jax
import jax.numpy as jnp
from jax import lax
from jax.experimental import pallas as pl
from jax.experimental.pallas import tpu as pltpu

D = 1024
BATCH, SEQ = 16, 256
DEC_BATCH, DEC_SEQ = 4, 1024
DEPTH = 4
GRID_W = 64
H = 8
DQK = 64
DHV = 128
QK = H * DQK
V = H * DHV
N_GROUPS = 4
GROUP_W = D // N_GROUPS
POOL_WINDOWS = (2, 4, 8, 16)
D_FF = 4 * D
ALPHA = (2.0 * DEPTH) ** 0.25
LN_EPS = 1e-5
F32 = jnp.float32
BF = jnp.bfloat16

NCTX = BATCH * SEQ
NLAT = DEC_BATCH * DEC_SEQ
NTOK = NCTX + NLAT
NCOND = 8
CHUNK = 256
SUBLANES, LANES = 8, 128
V7X_VMEM_BYTES = 64 * 1024 * 1024
VMEM_LIMIT = V7X_VMEM_BYTES - 8 * 1024 * 1024


def _cparams(n_axes):
    return pltpu.CompilerParams(dimension_semantics=("arbitrary",) * n_axes,
                                vmem_limit_bytes=VMEM_LIMIT)


def _mod_row(row0):
    return jnp.where(row0 < NCTX, 0, 1 + (row0 - NCTX) // DEC_SEQ)


def _seq_len(row0):
    return jnp.where(row0 < NCTX, SEQ, DEC_SEQ)


def _ln(z, g, b):
    mu = jnp.mean(z, axis=-1, keepdims=True)
    zc = z - mu
    var = jnp.mean(zc * zc, axis=-1, keepdims=True)
    return zc * lax.rsqrt(var + LN_EPS) * g + b


def _dot(a, b):
    return jnp.dot(a, b, preferred_element_type=F32)


def _dot_nt(a, b):
    return lax.dot_general(a, b, (((1,), (1,)), ((), ())), preferred_element_type=F32)


def _split2(x):
    hi = x.astype(BF)
    lo = (x - hi.astype(F32)).astype(BF)
    return hi, lo


def _split3(x):
    hi = x.astype(BF)
    r = x - hi.astype(F32)
    mid = r.astype(BF)
    lo = (r - mid.astype(F32)).astype(BF)
    return hi, mid, lo


def _full(shape):
    n = len(shape)
    return pl.BlockSpec(shape, lambda *_: (0,) * n)


def _x_specs(x, tm):
    if not isinstance(x, tuple):
        return [pl.BlockSpec((tm, D), lambda t: (t, 0))], [x]
    nc = NCTX // tm
    return ([pl.BlockSpec((tm, D), lambda t: (jnp.minimum(t, nc - 1), 0)),
             pl.BlockSpec((tm, D), lambda t: (jnp.maximum(t - nc, 0), 0)),
             pl.BlockSpec((tm, D), lambda t: (t % (DEC_SEQ // tm), 0))], list(x))


def _read_x(x_refs, tm):
    if len(x_refs) == 1:
        return x_refs[0][...]
    xc_ref, xl_ref, pe_ref = x_refs
    return jnp.where(pl.program_id(0) * tm < NCTX, xc_ref[...], xl_ref[...] + pe_ref[...])


def _mod_spec(layer, tm):
    return pl.BlockSpec((None, None, 6, D), lambda t: (layer, _mod_row(t * tm), 0, 0))


def _mod_kernel(c_ref, w_ref, b_ref, o_ref):
    c = c_ref[...]
    s = c * jax.nn.sigmoid(c)
    o_ref[...] = _dot(s.astype(BF), w_ref[...].astype(BF)) + b_ref[...]


def _modulation(cond, w_mod, b_mod):
    tn = 3 * D
    out = pl.pallas_call(
        _mod_kernel,
        grid=(DEPTH, 6 * D // tn),
        in_specs=[pl.BlockSpec((NCOND, D), lambda i, j: (0, 0)),
                  pl.BlockSpec((None, D, tn), lambda i, j: (i, 0, j)),
                  pl.BlockSpec((None, 1, tn), lambda i, j: (i, 0, j))],
        out_specs=pl.BlockSpec((None, NCOND, tn), lambda i, j: (i, 0, j)),
        out_shape=jax.ShapeDtypeStruct((DEPTH, NCOND, 6 * D), F32),
        compiler_params=_cparams(2),
        name="modulation",
    )(cond, w_mod, b_mod.reshape(DEPTH, 1, 6 * D))
    return out.reshape(DEPTH, NCOND, 6, D)


MLP_TM = 512
MLP_FC = 512
MLP_NF = D_FF // MLP_FC
MLP_NT = NTOK // MLP_TM


def _mlp_tile(s):
    return jnp.maximum(s - (MLP_NF - 1), 0)


def _mlp_kernel(mod_ref, w1_ref, w2_ref, g_ref, b_ref, x1_ref, x_ref, *refs):
    o_refs, (w1s_ref, w2s_ref, acc_ref, h_ref) = refs[:-4], refs[-4:]
    s = pl.program_id(0)
    m = mod_ref[...]

    def pre(x):
        return (x * (1.0 + m[4:5]) + m[3:4]).astype(BF)

    def hidden(h, w1c):
        return jnp.square(jnp.maximum(_dot(h, w1c), 0.0)).astype(BF)

    def result(x, acc):
        return _ln(ALPHA * x + m[5:6] * acc, g_ref[...], b_ref[...])

    @pl.when(s < MLP_NF)
    def _():
        @pl.when(s == 0)
        def _():
            h_ref[0:MLP_TM, :] = pre(x_ref[...])
            h_ref[MLP_TM:, :] = pre(x1_ref[...])

        w1c = w1_ref[...].astype(BF)
        w2c = w2_ref[...].astype(BF)
        part = _dot(hidden(h_ref[...], w1c), w2c)
        w1s_ref[s] = w1c
        w2s_ref[s] = w2c

        @pl.when(s == 0)
        def _():
            acc_ref[...] = part

        @pl.when(s > 0)
        def _():
            acc_ref[...] += part

        @pl.when(s == MLP_NF - 1)
        def _():
            o_refs[0][...] = result(x_ref[...], acc_ref[0:MLP_TM, :])

    @pl.when(s == MLP_NF)
    def _():
        o_refs[0][...] = result(x1_ref[...], acc_ref[MLP_TM:, :])

    @pl.when(s > MLP_NF)
    def _():
        x = x_ref[...]
        h = pre(x)
        acc = jnp.zeros((MLP_TM, D), F32)
        for f in range(MLP_NF):
            acc = acc + _dot(hidden(h, w1s_ref[f]), w2s_ref[f])
        res = result(x, acc)
        if len(o_refs) == 1:
            o_refs[0][...] = res
        else:
            is_ctx = _mlp_tile(s) * MLP_TM < NCTX

            @pl.when(is_ctx)
            def _():
                o_refs[0][...] = res

            @pl.when(jnp.logical_not(is_ctx))
            def _():
                o_refs[1][...] = res


def _mlp(x, mod, layer, w1, w2, ln_g, ln_b, split_out=False):
    tm = MLP_TM
    nc = NCTX // tm
    assert nc >= 2 and MLP_NT >= 2
    if split_out:
        out_specs = [pl.BlockSpec((tm, D), lambda s: (jnp.minimum(_mlp_tile(s), nc - 1), 0)),
                     pl.BlockSpec((tm, D), lambda s: (jnp.maximum(_mlp_tile(s) - nc, 0), 0))]
        out_shape = [jax.ShapeDtypeStruct((NCTX, D), F32), jax.ShapeDtypeStruct((NLAT, D), F32)]
    else:
        out_specs = pl.BlockSpec((tm, D), lambda s: (_mlp_tile(s), 0))
        out_shape = jax.ShapeDtypeStruct((NTOK, D), F32)
    return pl.pallas_call(
        _mlp_kernel,
        grid=(MLP_NF - 1 + MLP_NT,),
        in_specs=[pl.BlockSpec((None, None, 6, D), lambda s: (layer, _mod_row(_mlp_tile(s) * tm), 0, 0)),
                  pl.BlockSpec((None, D, MLP_FC), lambda s: (layer, 0, jnp.minimum(s, MLP_NF - 1))),
                  pl.BlockSpec((None, MLP_FC, D), lambda s: (layer, jnp.minimum(s, MLP_NF - 1), 0)),
                  _full((1, D)), _full((1, D)),
                  pl.BlockSpec((tm, D), lambda s: (1, 0)),
                  pl.BlockSpec((tm, D), lambda s: (_mlp_tile(s), 0))],
        out_specs=out_specs,
        out_shape=out_shape,
        scratch_shapes=[pltpu.VMEM((MLP_NF, D, MLP_FC), BF),
                        pltpu.VMEM((MLP_NF, MLP_FC, D), BF),
                        pltpu.VMEM((2 * tm, D), F32),
                        pltpu.VMEM((2 * tm, D), BF)],
        compiler_params=_cparams(1),
        name="mlp",
    )(mod, w1, w2, ln_g.reshape(1, D), ln_b.reshape(1, D), x, x)


OUT_TM = 1024
OUT_SUB = 256


def _out_kernel(uc_ref, ul_ref, w_f32_ref, bias_ref, mod_ref, g_ref, b_ref, *refs):
    x_refs, o_ref, w_ref = refs[:-2], refs[-2], refs[-1]

    @pl.when(pl.program_id(0) == 0)
    def _():
        w_ref[...] = w_f32_ref[...].astype(BF)

    is_ctx = pl.program_id(0) * OUT_TM < NCTX
    m = mod_ref[...]
    outs = []
    for r in range(0, OUT_TM, OUT_SUB):
        rs = slice(r, r + OUT_SUB)
        u = jnp.where(is_ctx, uc_ref[rs, :], ul_ref[rs, :])
        y = _dot(u, w_ref[...]) + bias_ref[...]
        if len(x_refs) == 1:
            x = x_refs[0][rs, :]
        else:
            x = jnp.where(is_ctx, x_refs[0][rs, :], x_refs[1][rs, :] + x_refs[2][rs, :])
        outs.append(_ln(ALPHA * x + m[2:3] * y, g_ref[...], b_ref[...]))
    for r, out in zip(range(0, OUT_TM, OUT_SUB), outs):
        o_ref[r:r + OUT_SUB, :] = out


def _out_proj(uc, ul, w, bias, x, mod, layer, ln_g, ln_b):
    tm = OUT_TM
    nc = NCTX // tm
    k = w.shape[0]
    x_specs, x_args = _x_specs(x, tm)
    return pl.pallas_call(
        _out_kernel,
        grid=(NTOK // tm,),
        in_specs=[pl.BlockSpec((tm, k), lambda t: (jnp.minimum(t, nc - 1), 0)),
                  pl.BlockSpec((tm, k), lambda t: (jnp.maximum(t - nc, 0), 0)),
                  pl.BlockSpec((k, D), lambda t: (0, 0), pipeline_mode=pl.Buffered(1)),
                  _full((1, D)),
                  _mod_spec(layer, tm),
                  _full((1, D)), _full((1, D))] + x_specs,
        out_specs=pl.BlockSpec((tm, D), lambda t: (t, 0)),
        out_shape=jax.ShapeDtypeStruct((NTOK, D), F32),
        scratch_shapes=[pltpu.VMEM((k, D), BF)],
        compiler_params=_cparams(1),
        name="out_proj",
    )(uc, ul, w, bias.reshape(1, D), mod, ln_g.reshape(1, D), ln_b.reshape(1, D), *x_args)


ML_TM = 512
N_HP = H // 2
SCAN_PAIRS = 2
NG = 8


def _log_sigmoid(g):
    return jnp.minimum(g, 0.0) - jnp.log1p(jnp.exp(-jnp.abs(g)))


def _ml_proj_kernel(mod_ref, w_in_ref, wg_ref, wgt_ref, bg_ref, bgt_ref, *refs):
    x_refs = refs[:-10]
    qt_ref, k_ref, kt_ref, vt_ref, o_ref, g_ref, gt_ref = refs[-10:-3]
    wt_ref, wk_ref, wo_ref = refs[-3:]
    r0, r1, r2 = QK, QK + 4 * NG, 2 * QK + 4 * NG

    @pl.when(pl.program_id(0) == 0)
    def _():
        wt_ref[0:r0, :] = w_in_ref[:, 0:QK].T.astype(BF)
        wt_ref[r0:r1, :] = wgt_ref[...]
        wt_ref[r1:r2, :] = w_in_ref[:, QK:2 * QK].T.astype(BF)
        wt_ref[r2:, :] = w_in_ref[:, 2 * QK:2 * QK + V].T.astype(BF)
        wk_ref[...] = w_in_ref[:, QK:2 * QK].astype(BF)
        wo_ref[...] = w_in_ref[:, 2 * QK + V:].astype(BF)

    m = mod_ref[...]
    h = (_read_x(x_refs, ML_TM) * (1.0 + m[1:2]) + m[0:1]).astype(BF)
    tr = _dot_nt(wt_ref[...], h)
    qt = tr[:r0].astype(BF)
    kt = (tr[r1:r2] * (DQK ** -0.5)).astype(BF)
    vt = tr[r2:].astype(BF)
    k = (_dot(h, wk_ref[...]) * (DQK ** -0.5)).astype(BF)
    o = _dot(h, wo_ref[...])
    g = _dot(h, wg_ref[...]) + bg_ref[...]
    col = lax.broadcasted_iota(jnp.int32, g.shape, 1)
    g = jnp.where((col & 2) != 0, _log_sigmoid(g), g)
    gt = tr[r0:r1] + bgt_ref[...]
    row = lax.broadcasted_iota(jnp.int32, gt.shape, 0)
    gt = jnp.where((row & 2) != 0, _log_sigmoid(gt), gt)
    qt_ref[...] = qt
    k_ref[...] = k
    kt_ref[...] = kt
    vt_ref[...] = vt
    o_ref[...] = o
    for hp in range(N_HP):
        g_ref[hp] = g[:, hp * NG:(hp + 1) * NG]
        gt_ref[hp] = gt[hp * NG:(hp + 1) * NG, :]


def _ml_proj(x, mod, layer, w_in, wg, wgt, bg, bgt):
    tm = ML_TM
    res = lambda shape: pl.BlockSpec(shape, lambda t: (0,) * len(shape), pipeline_mode=pl.Buffered(1))
    x_specs, x_args = _x_specs(x, tm)
    return pl.pallas_call(
        _ml_proj_kernel,
        grid=(NTOK // tm,),
        in_specs=[_mod_spec(layer, tm),
                  res((D, 2 * QK + 2 * V)), res((D, 4 * NG)), res((4 * NG, D)),
                  res((1, 4 * NG)), res((4 * NG, 1))] + x_specs,
        out_specs=[pl.BlockSpec((QK, tm), lambda t: (0, t)),
                   pl.BlockSpec((tm, QK), lambda t: (t, 0)),
                   pl.BlockSpec((QK, tm), lambda t: (0, t)),
                   pl.BlockSpec((V, tm), lambda t: (0, t)),
                   pl.BlockSpec((tm, V), lambda t: (t, 0)),
                   pl.BlockSpec((N_HP, tm, NG), lambda t: (0, t, 0)),
                   pl.BlockSpec((N_HP, NG, tm), lambda t: (0, 0, t))],
        out_shape=[jax.ShapeDtypeStruct((QK, NTOK), BF),
                   jax.ShapeDtypeStruct((NTOK, QK), BF),
                   jax.ShapeDtypeStruct((QK, NTOK), BF),
                   jax.ShapeDtypeStruct((V, NTOK), BF),
                   jax.ShapeDtypeStruct((NTOK, V), F32),
                   jax.ShapeDtypeStruct((N_HP, NTOK, NG), F32),
                   jax.ShapeDtypeStruct((N_HP, NG, NTOK), F32)],
        scratch_shapes=[pltpu.VMEM((2 * QK + 4 * NG + V, D), BF),
                        pltpu.VMEM((D, QK), BF),
                        pltpu.VMEM((D, V), BF)],
        compiler_params=_cparams(1),
        name="mlstm_proj",
    )(mod, w_in, wg, wgt, bg, bgt, *x_args)


DHA = DHV + SUBLANES


def _scan_chunk(k_c, kt_c, qt_c, vta, rr_col, rr_row, b_row, m, cta, d, need_state):
    L = k_c.shape[0]
    si = lax.broadcasted_iota(jnp.int32, (L, L), 0)
    ti = lax.broadcasted_iota(jnp.int32, (L, L), 1)
    mask = (si <= ti) if d == 0 else (si >= ti)
    rrm = jnp.where(mask, rr_col, -jnp.inf)
    g = jnp.maximum(jnp.max(rrm, axis=0, keepdims=True), m)
    st = _dot(k_c, qt_c) * jnp.exp(rrm - g)
    num = _dot(vta[:DHV], st.astype(BF))
    den = jnp.sum(st, axis=0, keepdims=True)
    if cta is not None:
        a = jnp.exp(m - g)
        qc = _dot(cta.astype(BF), qt_c)
        num = num + a * qc[:DHV]
        den = den + a * qc[DHV:DHV + 1]
    mt = b_row + g
    ht = num * (1.0 / jnp.maximum(jnp.abs(den), jnp.exp(-mt)))
    if not need_state:
        return ht, None, None
    last = L - 1 if d == 0 else 0
    m_new = mt[:, last:last + 1]
    b_last = b_row[:, last:last + 1]
    w = jnp.exp(b_last + rr_row - m_new)
    kw = (kt_c.astype(F32) * w).astype(BF)
    if cta is None:
        return ht, (_dot_nt(kw, vta[:DHV]), _dot_nt(vta[DHV:], kw)), m_new
    c_new = jnp.exp(b_last + m - m_new) * cta + _dot_nt(vta, kw)
    return ht, c_new, m_new


def _scan_kernel(*refs, is_ctx):
    if is_ctx:
        qt_ref, k_ref, kt_ref, vt_ref, o_ref, g_ref, gt_ref, ng_ref, u_ref, cst_ref, nmst_ref = refs
    else:
        qt_ref, k_ref, kt_ref, vt_ref, o_ref, g_ref, gt_ref, ng_ref, c0_ref, m0_ref, u_ref = refs
    L = CHUNK
    nch = DEC_SEQ // L
    ri = lax.broadcasted_iota(jnp.int32, (L, L), 0)
    ci = lax.broadcasted_iota(jnp.int32, (L, L), 1)
    lower = jnp.where(ci <= ri, 1.0, 0.0).astype(BF)
    upper = jnp.where(ci >= ri, 1.0, 0.0).astype(BF)
    sub = lax.broadcasted_iota(jnp.int32, (DHA - DHV, L), 0)
    ones_rows = jnp.where(sub == 0, 1.0, 0.0).astype(BF)

    rr_cols, rr_rows, b_rows = [], [], []
    for pp in range(SCAN_PAIRS):
        tparts = _split3(jnp.concatenate([gt_ref[pp, :, c * L:(c + 1) * L] for c in range(nch)], axis=0))
        b_row_all = (sum(_dot(p, upper) for p in tparts), sum(_dot(p, lower) for p in tparts))
        rr_cols.append([])
        rr_rows.append([])
        b_rows.append([])
        for c in range(nch):
            gc = g_ref[pp, c * L:(c + 1) * L, :]
            gtc = gt_ref[pp, :, c * L:(c + 1) * L]
            gparts = _split3(gc)
            b_col = (sum(_dot(lower, p) for p in gparts), sum(_dot(upper, p) for p in gparts))
            b_row = tuple(b[c * NG:(c + 1) * NG, :] for b in b_row_all)
            rr_cols[pp].append([gc[:, 0:NG - 2] - b[:, 2:NG] for b in b_col])
            rr_rows[pp].append([gtc[0:NG - 2, :] - b[2:NG, :] for b in b_row])
            b_rows[pp].append(b_row)

    states, us = [], []
    for hl in range(2 * SCAN_PAIRS):
        pp, hh = divmod(hl, 2)
        hsum = [None] * nch
        for d in range(2):
            ig_i, lf_i = d * 4 + hh, d * 4 + 2 + hh
            if is_ctx:
                m, cta = jnp.zeros((1, 1), F32), None
            else:
                m = m0_ref[d, hl][:, 0:1]
                cta = c0_ref[d, hl]
            order = range(nch) if d == 0 else range(nch - 1, -1, -1)
            for n, c in enumerate(order):
                sl = slice(c * L, (c + 1) * L)
                hs = slice(hl * DQK, (hl + 1) * DQK)
                vta = jnp.concatenate([vt_ref[hl * DHV:(hl + 1) * DHV, sl], ones_rows], axis=0)
                need_state = is_ctx or n < nch - 1
                hcur, c_new, m_new = _scan_chunk(
                    k_ref[sl, hs], kt_ref[hs, sl], qt_ref[hs, sl], vta,
                    rr_cols[pp][c][d][:, ig_i:ig_i + 1], rr_rows[pp][c][d][ig_i:ig_i + 1, :],
                    b_rows[pp][c][d][lf_i:lf_i + 1, :], m, cta, d, need_state)
                hsum[c] = hcur if d == 0 else hsum[c] + hcur
                if is_ctx:
                    states.append(((c, d, hl), c_new, m_new))
                else:
                    m, cta = m_new, c_new

        for c in range(nch):
            sl = slice(c * L, (c + 1) * L)
            mu = jnp.mean(hsum[c], axis=0, keepdims=True)
            hc = hsum[c] - mu
            var = jnp.mean(hc * hc, axis=0, keepdims=True)
            hn = hc * lax.rsqrt(var + LN_EPS) * ng_ref[hl * DHV:(hl + 1) * DHV, :]
            og = jax.nn.sigmoid(o_ref[sl, hl * DHV:(hl + 1) * DHV])
            us.append(((sl, slice(hl * DHV, (hl + 1) * DHV)), (hn.T * og).astype(BF)))

    for idx, u in us:
        u_ref[idx] = u
    for idx, (c_fin, n_fin), m_fin in states:
        cst_ref[idx] = c_fin
        nmst_ref[idx] = jnp.concatenate([n_fin, jnp.broadcast_to(m_fin, (SUBLANES, LANES - DQK))], axis=1)


def _ml_scan(qt, k, kt, vt, o, g, gt, norm_g, c0aug=None, m0b=None):
    is_ctx = c0aug is None
    tt = DEC_SEQ
    toff = 0 if is_ctx else NCTX // tt
    nt = (NCTX if is_ctx else NLAT) // tt
    nh = 2 * SCAN_PAIRS
    in_specs = [pl.BlockSpec((nh * DQK, tt), lambda t, p: (p, toff + t)),
                pl.BlockSpec((tt, nh * DQK), lambda t, p: (toff + t, p)),
                pl.BlockSpec((nh * DQK, tt), lambda t, p: (p, toff + t)),
                pl.BlockSpec((nh * DHV, tt), lambda t, p: (p, toff + t)),
                pl.BlockSpec((tt, nh * DHV), lambda t, p: (toff + t, p)),
                pl.BlockSpec((SCAN_PAIRS, tt, NG), lambda t, p: (p, toff + t, 0)),
                pl.BlockSpec((SCAN_PAIRS, NG, tt), lambda t, p: (p, 0, toff + t)),
                pl.BlockSpec((nh * DHV, 1), lambda t, p: (p, 0))]
    args = [qt, k, kt, vt, o, g, gt, norm_g.reshape(V, 1)]
    u_spec = pl.BlockSpec((tt, nh * DHV), lambda t, p: (t, p))
    u_shape = jax.ShapeDtypeStruct((nt * tt, V), BF)
    if is_ctx:
        nseq = tt // SEQ
        out_specs = [u_spec,
                     pl.BlockSpec((nseq, 2, nh, DQK, DHV), lambda t, p: (t, 0, p, 0, 0)),
                     pl.BlockSpec((nseq, 2, nh, SUBLANES, LANES), lambda t, p: (t, 0, p, 0, 0))]
        out_shape = [u_shape,
                     jax.ShapeDtypeStruct((BATCH, 2, H, DQK, DHV), F32),
                     jax.ShapeDtypeStruct((BATCH, 2, H, SUBLANES, LANES), F32)]
    else:
        in_specs += [pl.BlockSpec((None, 2, nh, DHA, DQK), lambda t, p: (t, 0, p, 0, 0)),
                     pl.BlockSpec((None, 2, nh, 1, LANES), lambda t, p: (t, 0, p, 0, 0))]
        args += [c0aug, m0b]
        out_specs = [u_spec]
        out_shape = [u_shape]
    return pl.pallas_call(
        functools.partial(_scan_kernel, is_ctx=is_ctx),
        grid=(nt, N_HP // SCAN_PAIRS),
        in_specs=in_specs,
        out_specs=out_specs,
        out_shape=out_shape,
        compiler_params=_cparams(2),
        name="mlstm_scan_ctx" if is_ctx else "mlstm_scan_lat",
    )(*args)


SC_TM = 512
SC_HALO = SUBLANES


def _sconv_kernel(x_ref, prev_ref, next_ref, mod_ref, w_in_f32_ref, cw_ref, w_out_f32_ref, g_ref, b_ref, o_ref,
                  e_ref, w_in_ref, w_out_ref):
    tm = SC_TM

    @pl.when(pl.program_id(0) == 0)
    def _():
        w_in_ref[...] = w_in_f32_ref[...].astype(BF)
        w_out_ref[...] = w_out_f32_ref[...].astype(BF)

    row0 = pl.program_id(0) * tm
    slen = _seq_len(row0)
    pos = (row0 + lax.broadcasted_iota(jnp.int32, (tm, 1), 0)) & (slen - 1)
    m = mod_ref[...]
    x = x_ref[...]
    xe = jnp.concatenate([prev_ref[...], x, next_ref[...]], axis=0)
    he = (xe * (1.0 + m[1:2]) + m[0:1]).astype(BF)
    cu = _dot(he, w_in_ref[:, D:2 * D]) * _dot(he, w_in_ref[:, 2 * D:3 * D])
    bg = _dot(he[SC_HALO:SC_HALO + tm], w_in_ref[:, 0:D])
    e_ref[...] = cu
    prev = jnp.where(pos == 0, 0.0, e_ref[SC_HALO - 1:SC_HALO - 1 + tm, :])
    nxt = jnp.where(pos == slen - 1, 0.0, e_ref[SC_HALO + 1:SC_HALO + 1 + tm, :])
    conv = cw_ref[0:1, :] * prev + cw_ref[1:2, :] * e_ref[SC_HALO:SC_HALO + tm, :] + cw_ref[2:3, :] * nxt
    y = _dot((bg * conv).astype(BF), w_out_ref[...])
    o_ref[...] = _ln(ALPHA * x + m[2:3] * y, g_ref[...], b_ref[...])


def _sconv(x, mod, layer, w_in, conv_w, w_out, ln_g, ln_b):
    tm = SC_TM
    hb = tm // SC_HALO
    nhb = NTOK // SC_HALO
    return pl.pallas_call(
        _sconv_kernel,
        grid=(NTOK // tm,),
        in_specs=[pl.BlockSpec((tm, D), lambda t: (t, 0)),
                  pl.BlockSpec((SC_HALO, D), lambda t: (jnp.maximum(t * hb - 1, 0), 0)),
                  pl.BlockSpec((SC_HALO, D), lambda t: (jnp.minimum((t + 1) * hb, nhb - 1), 0)),
                  _mod_spec(layer, tm),
                  pl.BlockSpec((D, 3 * D), lambda t: (0, 0), pipeline_mode=pl.Buffered(1)),
                  _full((3, D)),
                  pl.BlockSpec((D, D), lambda t: (0, 0), pipeline_mode=pl.Buffered(1)),
                  _full((1, D)), _full((1, D))],
        out_specs=pl.BlockSpec((tm, D), lambda t: (t, 0)),
        out_shape=jax.ShapeDtypeStruct((NTOK, D), F32),
        scratch_shapes=[pltpu.VMEM((tm + 2 * SC_HALO, D), F32),
                        pltpu.VMEM((D, 3 * D), BF),
                        pltpu.VMEM((D, D), BF)],
        compiler_params=_cparams(1),
        name="sconv",
    )(x, x, x, mod, w_in, conv_w, w_out, ln_g.reshape(1, D), ln_b.reshape(1, D))


PL_SEG = 256
PL_NSEG = 4
PL_TM = PL_SEG * PL_NSEG
PL_HALO = 16
PL_STRIDE = PL_SEG + 2 * PL_HALO


def _pool_kernel(x_ref, prev_ref, next_ref, mod_ref, w_ref, pb_ref, ps_ref, g_ref, b_ref, o_ref,
                 e_ref, s2_ref, s4_ref, s8_ref):
    tm, seg, hl, gw = PL_TM, PL_SEG, PL_HALO, GROUP_W
    row0 = pl.program_id(0) * tm
    slen = _seq_len(row0)
    m = mod_ref[...]
    x = x_ref[...]
    h = x * (1.0 + m[1:2]) + m[0:1]
    hp = prev_ref[...] * (1.0 + m[1:2]) + m[0:1]
    hn = next_ref[...] * (1.0 + m[1:2]) + m[0:1]
    n0 = PL_NSEG * PL_STRIDE
    for i in range(PL_NSEG):
        base = i * PL_STRIDE
        pos0 = (row0 + i * seg) & (slen - 1)
        before = hp if i == 0 else h[i * seg - hl:i * seg]
        after = hn if i == PL_NSEG - 1 else h[(i + 1) * seg:(i + 1) * seg + hl]
        e_ref[base:base + hl, :] = jnp.where(pos0 != 0, before, 0.0)
        e_ref[base + hl:base + hl + seg, :] = h[i * seg:(i + 1) * seg]
        e_ref[base + hl + seg:base + PL_STRIDE, :] = jnp.where(pos0 + seg != slen, after, 0.0)
    e_ref[n0:, :] = jnp.zeros((SUBLANES, D), F32)
    n2, n4, n8 = n0, n0 - SUBLANES, n0 - 2 * SUBLANES
    s2_ref[...] = e_ref[0:n2, :] + e_ref[1:n2 + 1, :]
    s4_ref[...] = s2_ref[0:n4, gw:] + s2_ref[2:n4 + 2, gw:]
    s8_ref[...] = s4_ref[0:n8, gw:] + s4_ref[4:n8 + 4, gw:]

    def rows(ref, off, lanes):
        return jnp.concatenate([ref[i * PL_STRIDE + off:i * PL_STRIDE + off + seg, lanes]
                                for i in range(PL_NSEG)], axis=0)

    tots = [rows(s2_ref, hl - 1, slice(0, gw)),
            rows(s4_ref, hl - 2, slice(0, gw)),
            rows(s8_ref, hl - 4, slice(0, gw)),
            rows(s8_ref, hl - 8, slice(gw, 2 * gw)) + rows(s8_ref, hl, slice(gw, 2 * gw))]
    pos = (row0 + lax.broadcasted_iota(jnp.int32, (tm, 1), 0)) & (slen - 1)
    ys = []
    for gi, win in enumerate(POOL_WINDOWS):
        cs = slice(gi * gw, (gi + 1) * gw)
        back = win // 2
        cnt = jnp.minimum(pos + (win - back), slen) - jnp.maximum(pos - back, 0)
        p = tots[gi] / cnt.astype(F32) - h[:, cs]
        ys.append(_dot(p.astype(BF), w_ref[gi]))
    y = (jnp.concatenate(ys, axis=1) + pb_ref[...]) * ps_ref[...]
    o_ref[...] = _ln(ALPHA * x + m[2:3] * y, g_ref[...], b_ref[...])


def _pool(x, mod, layer, w, pb, ps, ln_g, ln_b):
    assert POOL_WINDOWS == (2, 4, 8, 16) and PL_HALO >= POOL_WINDOWS[-1] // 2
    assert SEQ % PL_SEG == 0 and DEC_SEQ % PL_SEG == 0
    tm = PL_TM
    hb = tm // PL_HALO
    nhb = NTOK // PL_HALO
    n0 = PL_NSEG * PL_STRIDE
    return pl.pallas_call(
        _pool_kernel,
        grid=(NTOK // tm,),
        in_specs=[pl.BlockSpec((tm, D), lambda t: (t, 0)),
                  pl.BlockSpec((PL_HALO, D), lambda t: (jnp.maximum(t * hb - 1, 0), 0)),
                  pl.BlockSpec((PL_HALO, D), lambda t: (jnp.minimum((t + 1) * hb, nhb - 1), 0)),
                  _mod_spec(layer, tm),
                  _full((N_GROUPS, GROUP_W, GROUP_W)),
                  _full((1, D)), _full((1, D)), _full((1, D)), _full((1, D))],
        out_specs=pl.BlockSpec((tm, D), lambda t: (t, 0)),
        out_shape=jax.ShapeDtypeStruct((NTOK, D), F32),
        scratch_shapes=[pltpu.VMEM((n0 + SUBLANES, D), F32),
                        pltpu.VMEM((n0, D), F32),
                        pltpu.VMEM((n0 - SUBLANES, D - GROUP_W), F32),
                        pltpu.VMEM((n0 - 2 * SUBLANES, D - 2 * GROUP_W), F32)],
        compiler_params=_cparams(1),
        name="pool",
    )(x, x, x, mod, w, pb.reshape(1, D), ps.reshape(1, D), ln_g.reshape(1, D), ln_b.reshape(1, D))


FT_TM = 1024
FT_TK = 256
FT_FOLD_MIN_SEQ = 512
FT_PAD = SUBLANES


def _dft_mats(n, scale):
    k = np.arange(n, dtype=np.int64)
    ang = 2.0 * np.pi * ((k[:, None] * k[None, :]) % n).astype(np.float64) / n
    return np.stack([np.cos(ang) * scale, np.sin(ang) * scale]).astype(np.float32)


def _dot_x3(a_parts, b_parts):
    (ah, al), (bh, bl) = a_parts, b_parts
    return _dot(ah, bh) + (_dot(ah, bl) + _dot(al, bh))


def _rev_rows(x):
    r = lax.broadcasted_iota(jnp.int32, (SUBLANES, x.shape[1]), 0)
    out = []
    for i in range(x.shape[0] // SUBLANES - 1, -1, -1):
        p = x[i * SUBLANES:(i + 1) * SUBLANES]
        for sh in (1, 2, 4):
            p = jnp.where((r & sh) == 0, pltpu.roll(p, SUBLANES - sh, 0), pltpu.roll(p, sh, 0))
        out.append(p)
    return jnp.concatenate(out, axis=0)


def _fourier_tile(x_ref, m, cs_ref, ccp, scp, w_ref, bias_ref, lg_ref, lb_ref, o_ref, s, tk):
    nsq = FT_TM // s
    ri = lax.broadcasted_iota(jnp.int32, (tk, tk), 0)
    ci = lax.broadcasted_iota(jnp.int32, (tk, tk), 1)
    anti = jnp.where(ri + ci == tk - 1, 1.0, 0.0).astype(BF)
    blocks = []
    half = s // 2
    row = lax.broadcasted_iota(jnp.int32, (half, 1), 0)
    for sq in range(nsq):
        base = sq * s
        h = x_ref[base:base + s, :] * (1.0 + m[1:2]) + m[0:1]
        if s >= FT_FOLD_MIN_SEQ:
            hr = jnp.where(row == 0, 0.0, pltpu.roll(_rev_rows(h[half:]), 1, 0))
            hp = _split2(h[:half] + hr)
            hm = _split2(h[:half] - hr)
            h_mid = h[half:half + 1]
        else:
            hparts = _split2(h)
        for j in range(s // (2 * tk)):
            if s >= FT_FOLD_MIN_SEQ:
                ac = _dot_x3(_split2(cs_ref[0, j, :, 0:half]), hp) + cs_ref[0, j, :, half:half + 1] * h_mid
                as_ = _dot_x3(_split2(cs_ref[1, j, :, 0:half]), hm)
            else:
                ac = _dot_x3(_split2(cs_ref[0, j]), hparts)
                as_ = _dot_x3(_split2(cs_ref[1, j]), hparts)
            direct, mirror = [], []
            for gi in range(N_GROUPS):
                cs = slice(gi * GROUP_W, (gi + 1) * GROUP_W)
                p = _dot_x3(_split2(ac[:, cs]), ccp)
                q = _dot_x3(_split2(as_[:, cs]), scp)
                direct.append((p - q)[0:tk])
                mirror.append((p + q)[1:tk + 1])
            blocks.append((base + j * tk, jnp.concatenate(direct, axis=1).astype(BF)))
            mir = jnp.concatenate(mirror, axis=1).astype(BF)
            blocks.append((base + s - (j + 1) * tk, _dot(anti, mir).astype(BF)))
    outs = []
    for r0, blk in blocks:
        y = _dot(blk, w_ref[...]) + bias_ref[...]
        outs.append((r0, _ln(ALPHA * x_ref[r0:r0 + tk, :] + m[2:3] * y, lg_ref[...], lb_ref[...])))
    for r0, out in outs:
        o_ref[r0:r0 + tk, :] = out


def _fourier_kernel(x_ref, mod_ref, csc_ref, csl_ref, cc_ref, w_f32_ref, bias_ref, lg_ref, lb_ref, o_ref, w_ref):
    @pl.when(pl.program_id(0) == 0)
    def _():
        w_ref[...] = w_f32_ref[...].astype(BF)

    m = mod_ref[...]
    ccp = _split2(cc_ref[0])
    scp = _split2(cc_ref[1])
    is_ctx = pl.program_id(0) * FT_TM < NCTX

    @pl.when(is_ctx)
    def _():
        _fourier_tile(x_ref, m, csc_ref, ccp, scp, w_ref, bias_ref, lg_ref, lb_ref, o_ref, SEQ, _ft_tk(SEQ))

    @pl.when(jnp.logical_not(is_ctx))
    def _():
        _fourier_tile(x_ref, m, csl_ref, ccp, scp, w_ref, bias_ref, lg_ref, lb_ref, o_ref, DEC_SEQ, _ft_tk(DEC_SEQ))


def _ft_tk(s):
    return min(FT_TK, s // 2)


def _half_dft_mats(s, tk):
    full = _dft_mats(s, s ** -0.5)
    rows = (np.arange(s // (2 * tk))[:, None] * tk + np.arange(tk + FT_PAD)[None, :]) % s
    return full[:, rows, :]


def _fourier(x, mod, layer, w, bias, ln_g, ln_b):
    tm = FT_TM
    assert tm % SEQ == 0 and tm % DEC_SEQ == 0 and NCTX % tm == 0
    csc = jnp.asarray(_half_dft_mats(SEQ, _ft_tk(SEQ)))
    csl = jnp.asarray(_half_dft_mats(DEC_SEQ, _ft_tk(DEC_SEQ)))
    cc = jnp.asarray(_dft_mats(GROUP_W, GROUP_W ** -0.5))
    const = lambda shape: pl.BlockSpec(shape, lambda b: (0,) * len(shape), pipeline_mode=pl.Buffered(1))
    return pl.pallas_call(
        _fourier_kernel,
        grid=(NTOK // tm,),
        in_specs=[pl.BlockSpec((tm, D), lambda b: (b, 0)),
                  _mod_spec(layer, tm),
                  const(csc.shape), const(csl.shape), const(cc.shape), const((D, D)),
                  _full((1, D)), _full((1, D)), _full((1, D))],
        out_specs=pl.BlockSpec((tm, D), lambda b: (b, 0)),
        out_shape=jax.ShapeDtypeStruct((NTOK, D), F32),
        scratch_shapes=[pltpu.VMEM((D, D), BF)],
        compiler_params=_cparams(1),
        name="fourier",
    )(x, mod, csc, csl, cc, w, bias.reshape(1, D), ln_g.reshape(1, D), ln_b.reshape(1, D))


def _pos_embed():
    rows = DEC_SEQ // GRID_W
    quarter = D // 4
    omega = 1.0 / (10000.0 ** (np.arange(quarter, dtype=np.float64) / quarter))
    rr, cc = np.meshgrid(np.arange(rows, dtype=np.float64), np.arange(GRID_W, dtype=np.float64), indexing="ij")
    er = rr.reshape(-1, 1) * omega
    ec = cc.reshape(-1, 1) * omega
    return np.concatenate([np.sin(er), np.cos(er), np.sin(ec), np.cos(ec)], axis=-1).astype(np.float32)


def _gate_perm():
    perm = np.zeros(4 * H, dtype=np.int32)
    for hp in range(N_HP):
        for d in range(2):
            for j in range(2):
                for hh in range(2):
                    perm[hp * NG + d * 4 + j * 2 + hh] = d * 2 * H + j * H + 2 * hp + hh
    return perm


def kernel(x_prompt, x_sample, state_C, state_n, state_m, c, c_ctx, w_mod, b_mod, ln_g, ln_b, mlp_w1, mlp_w2,
           ml_w_in, ml_w_gate, ml_b_gate, ml_norm_g, ml_w_out, sc_w_in, sc_conv_w, sc_w_out,
           pl_w, pl_b, pl_scale, ft_w_out, ft_b_out):
    cond = jnp.concatenate([c_ctx[None, :], c, jnp.zeros((NCOND - 1 - DEC_BATCH, D), F32)], axis=0)
    mod = _modulation(cond, w_mod, b_mod)

    x = (x_prompt.reshape(NCTX, D), x_sample.reshape(NLAT, D), jnp.asarray(_pos_embed()))

    new_c = new_n = new_m = None
    for i in range(DEPTH):
        kind, j = i % 4, i // 4
        if kind == 0:
            perm = _gate_perm()
            wg = ml_w_gate[j][:, perm].astype(BF)
            bg = ml_b_gate[j][perm]
            qt, k, kt, vt, o, g, gt = _ml_proj(
                x, mod, i, ml_w_in[j], wg, wg.T, bg.reshape(1, -1), bg.reshape(-1, 1))
            scan_args = (qt, k, kt, vt, o, g, gt, ml_norm_g[j])
            uc, cst, nmst = _ml_scan(*scan_args)
            c0 = jnp.concatenate([jnp.swapaxes(state_C[:, j], -1, -2), state_n[:, j][..., None, :],
                                  jnp.zeros((DEC_BATCH, 2, H, DHA - DHV - 1, DQK), F32)], axis=-2)
            m0 = jnp.broadcast_to(state_m[:, j][..., None, None], (DEC_BATCH, 2, H, 1, LANES))
            (ul,) = _ml_scan(*scan_args, c0, m0)
            x = _out_proj(uc, ul, ml_w_out[j], jnp.zeros((D,), F32), x, mod, i, ln_g[i, 0], ln_b[i, 0])
            new_c = cst[:, None]
            new_n = nmst[..., 0, :DQK][:, None]
            new_m = nmst[..., 0, DQK][:, None]
        elif kind == 1:
            x = _sconv(x, mod, i, sc_w_in[j], sc_conv_w[j], sc_w_out[j],
                       ln_g[i, 0], ln_b[i, 0])
        elif kind == 2:
            x = _pool(x, mod, i, pl_w[j].astype(BF), pl_b[j], pl_scale[j], ln_g[i, 0], ln_b[i, 0])
        else:
            x = _fourier(x, mod, i, ft_w_out[j], ft_b_out[j], ln_g[i, 0], ln_b[i, 0])
        x = _mlp(x, mod, i, mlp_w1, mlp_w2, ln_g[i, 1], ln_b[i, 1],
                 split_out=(i == DEPTH - 1))

    y_prompt = x[0].reshape(BATCH, SEQ, D)
    y_sample = x[1].reshape(DEC_BATCH, DEC_SEQ, D)
    return y_prompt, y_sample, new_c, new_n, new_m
```

```python
import functools

import numpy as np
import jax
import jax.numpy as jnp
from jax import lax
from jax.experimental import pallas as pl
from jax.experimental.pallas import tpu as pltpu

D = 1024
BATCH, SEQ = 16, 256
DEC_BATCH, DEC_SEQ = 4, 1024
DEPTH = 4
GRID_W = 64
H = 8
DQK = 64
DHV = 128
QK = H * DQK
V = H * DHV
N_GROUPS = 4
GROUP_W = D // N_GROUPS
POOL_WINDOWS = (2, 4, 8, 16)
D_FF = 4 * D
ALPHA = (2.0 * DEPTH) ** 0.25
LN_EPS = 1e-5
F32 = jnp.float32
BF = jnp.bfloat16

NCTX = BATCH * SEQ
NLAT = DEC_BATCH * DEC_SEQ
NTOK = NCTX + NLAT
NCOND = 8
CHUNK = 256
SUBLANES, LANES = 8, 128
V7X_VMEM_BYTES = 64 * 1024 * 1024
VMEM_LIMIT = V7X_VMEM_BYTES - 8 * 1024 * 1024


def _cparams(n_axes):
    return pltpu.CompilerParams(dimension_semantics=("arbitrary",) * n_axes,
                                vmem_limit_bytes=VMEM_LIMIT)


def _mod_row(row0):
    return jnp.where(row0 < NCTX, 0, 1 + (row0 - NCTX) // DEC_SEQ)


def _seq_len(row0):
    return jnp.where(row0 < NCTX, SEQ, DEC_SEQ)


def _ln(z, g, b):
    mu = jnp.mean(z, axis=-1, keepdims=True)
    zc = z - mu
    var = jnp.mean(zc * zc, axis=-1, keepdims=True)
    return zc * lax.rsqrt(var + LN_EPS) * g + b


def _dot(a, b):
    return jnp.dot(a, b, preferred_element_type=F32)


def _dot_nt(a, b):
    return lax.dot_general(a, b, (((1,), (1,)), ((), ())), preferred_element_type=F32)


def _split2(x):
    hi = x.astype(BF)
    lo = (x - hi.astype(F32)).astype(BF)
    return hi, lo


def _split3(x):
    hi = x.astype(BF)
    r = x - hi.astype(F32)
    mid = r.astype(BF)
    lo = (r - mid.astype(F32)).astype(BF)
    return hi, mid, lo


def _full(shape):
    n = len(shape)
    return pl.BlockSpec(shape, lambda *_: (0,) * n)


def _x_specs(x, tm):
    if not isinstance(x, tuple):
        return [pl.BlockSpec((tm, D), lambda t: (t, 0))], [x]
    nc = NCTX // tm
    return ([pl.BlockSpec((tm, D), lambda t: (jnp.minimum(t, nc - 1), 0)),
             pl.BlockSpec((tm, D), lambda t: (jnp.maximum(t - nc, 0), 0)),
             pl.BlockSpec((tm, D), lambda t: (t % (DEC_SEQ // tm), 0))], list(x))


def _read_x(x_refs, tm):
    if len(x_refs) == 1:
        return x_refs[0][...]
    xc_ref, xl_ref, pe_ref = x_refs
    return jnp.where(pl.program_id(0) * tm < NCTX, xc_ref[...], xl_ref[...] + pe_ref[...])


def _mod_spec(layer, tm):
    return pl.BlockSpec((None, None, 6, D), lambda t: (layer, _mod_row(t * tm), 0, 0))


def _mod_kernel(c_ref, w_ref, b_ref, o_ref):
    c = c_ref[...]
    s = c * jax.nn.sigmoid(c)
    o_ref[...] = _dot(s.astype(BF), w_ref[...].astype(BF)) + b_ref[...]


def _modulation(cond, w_mod, b_mod):
    tn = 3 * D
    out = pl.pallas_call(
        _mod_kernel,
        grid=(DEPTH, 6 * D // tn),
        in_specs=[pl.BlockSpec((NCOND, D), lambda i, j: (0, 0)),
                  pl.BlockSpec((None, D, tn), lambda i, j: (i, 0, j)),
                  pl.BlockSpec((None, 1, tn), lambda i, j: (i, 0, j))],
        out_specs=pl.BlockSpec((None, NCOND, tn), lambda i, j: (i, 0, j)),
        out_shape=jax.ShapeDtypeStruct((DEPTH, NCOND, 6 * D), F32),
        compiler_params=_cparams(2),
        name="modulation",
    )(cond, w_mod, b_mod.reshape(DEPTH, 1, 6 * D))
    return out.reshape(DEPTH, NCOND, 6, D)


MLP_TM = 512
MLP_FC = 512
MLP_NF = D_FF // MLP_FC
MLP_NT = NTOK // MLP_TM


def _mlp_tile(s):
    return jnp.maximum(s - (MLP_NF - 1), 0)


def _mlp_kernel(mod_ref, w1_ref, w2_ref, g_ref, b_ref, x1_ref, x_ref, *refs):
    o_refs, (w1s_ref, w2s_ref, acc_ref, h_ref) = refs[:-4], refs[-4:]
    s = pl.program_id(0)
    m = mod_ref[...]

    def pre(x):
        return (x * (1.0 + m[4:5]) + m[3:4]).astype(BF)

    def hidden(h, w1c):
        return jnp.square(jnp.maximum(_dot(h, w1c), 0.0)).astype(BF)

    def result(x, acc):
        return _ln(ALPHA * x + m[5:6] * acc, g_ref[...], b_ref[...])

    @pl.when(s < MLP_NF)
    def _():
        @pl.when(s == 0)
        def _():
            h_ref[0:MLP_TM, :] = pre(x_ref[...])
            h_ref[MLP_TM:, :] = pre(x1_ref[...])

        w1c = w1_ref[...].astype(BF)
        w2c = w2_ref[...].astype(BF)
        part = _dot(hidden(h_ref[...], w1c), w2c)
        w1s_ref[s] = w1c
        w2s_ref[s] = w2c

        @pl.when(s == 0)
        def _():
            acc_ref[...] = part

        @pl.when(s > 0)
        def _():
            acc_ref[...] += part

        @pl.when(s == MLP_NF - 1)
        def _():
            o_refs[0][...] = result(x_ref[...], acc_ref[0:MLP_TM, :])

    @pl.when(s == MLP_NF)
    def _():
        o_refs[0][...] = result(x1_ref[...], acc_ref[MLP_TM:, :])

    @pl.when(s > MLP_NF)
    def _():
        x = x_ref[...]
        h = pre(x)
        acc = jnp.zeros((MLP_TM, D), F32)
        for f in range(MLP_NF):
            acc = acc + _dot(hidden(h, w1s_ref[f]), w2s_ref[f])
        res = result(x, acc)
        if len(o_refs) == 1:
            o_refs[0][...] = res
        else:
            is_ctx = _mlp_tile(s) * MLP_TM < NCTX

            @pl.when(is_ctx)
            def _():
                o_refs[0][...] = res

            @pl.when(jnp.logical_not(is_ctx))
            def _():
                o_refs[1][...] = res


def _mlp(x, mod, layer, w1, w2, ln_g, ln_b, split_out=False):
    tm = MLP_TM
    nc = NCTX // tm
    assert nc >= 2 and MLP_NT >= 2
    if split_out:
        out_specs = [pl.BlockSpec((tm, D), lambda s: (jnp.minimum(_mlp_tile(s), nc - 1), 0)),
                     pl.BlockSpec((tm, D), lambda s: (jnp.maximum(_mlp_tile(s) - nc, 0), 0))]
        out_shape = [jax.ShapeDtypeStruct((NCTX, D), F32), jax.ShapeDtypeStruct((NLAT, D), F32)]
    else:
        out_specs = pl.BlockSpec((tm, D), lambda s: (_mlp_tile(s), 0))
        out_shape = jax.ShapeDtypeStruct((NTOK, D), F32)
    return pl.pallas_call(
        _mlp_kernel,
        grid=(MLP_NF - 1 + MLP_NT,),
        in_specs=[pl.BlockSpec((None, None, 6, D), lambda s: (layer, _mod_row(_mlp_tile(s) * tm), 0, 0)),
                  pl.BlockSpec((None, D, MLP_FC), lambda s: (layer, 0, jnp.minimum(s, MLP_NF - 1))),
                  pl.BlockSpec((None, MLP_FC, D), lambda s: (layer, jnp.minimum(s, MLP_NF - 1), 0)),
                  _full((1, D)), _full((1, D)),
                  pl.BlockSpec((tm, D), lambda s: (1, 0)),
                  pl.BlockSpec((tm, D), lambda s: (_mlp_tile(s), 0))],
        out_specs=out_specs,
        out_shape=out_shape,
        scratch_shapes=[pltpu.VMEM((MLP_NF, D, MLP_FC), BF),
                        pltpu.VMEM((MLP_NF, MLP_FC, D), BF),
                        pltpu.VMEM((2 * tm, D), F32),
                        pltpu.VMEM((2 * tm, D), BF)],
        compiler_params=_cparams(1),
        name="mlp",
    )(mod, w1, w2, ln_g.reshape(1, D), ln_b.reshape(1, D), x, x)


OUT_TM = 1024
OUT_SUB = 256


def _out_kernel(uc_ref, ul_ref, w_f32_ref, bias_ref, mod_ref, g_ref, b_ref, *refs):
    x_refs, o_ref, w_ref = refs[:-2], refs[-2], refs[-1]

    @pl.when(pl.program_id(0) == 0)
    def _():
        w_ref[...] = w_f32_ref[...].astype(BF)

    is_ctx = pl.program_id(0) * OUT_TM < NCTX
    m = mod_ref[...]
    outs = []
    for r in range(0, OUT_TM, OUT_SUB):
        rs = slice(r, r + OUT_SUB)
        u = jnp.where(is_ctx, uc_ref[rs, :], ul_ref[rs, :])
        y = _dot(u, w_ref[...]) + bias_ref[...]
        if len(x_refs) == 1:
            x = x_refs[0][rs, :]
        else:
            x = jnp.where(is_ctx, x_refs[0][rs, :], x_refs[1][rs, :] + x_refs[2][rs, :])
        outs.append(_ln(ALPHA * x + m[2:3] * y, g_ref[...], b_ref[...]))
    for r, out in zip(range(0, OUT_TM, OUT_SUB), outs):
        o_ref[r:r + OUT_SUB, :] = out


def _out_proj(uc, ul, w, bias, x, mod, layer, ln_g, ln_b):
    tm = OUT_TM
    nc = NCTX // tm
    k = w.shape[0]
    x_specs, x_args = _x_specs(x, tm)
    return pl.pallas_call(
        _out_kernel,
        grid=(NTOK // tm,),
        in_specs=[pl.BlockSpec((tm, k), lambda t: (jnp.minimum(t, nc - 1), 0)),
                  pl.BlockSpec((tm, k), lambda t: (jnp.maximum(t - nc, 0), 0)),
                  pl.BlockSpec((k, D), lambda t: (0, 0), pipeline_mode=pl.Buffered(1)),
                  _full((1, D)),
                  _mod_spec(layer, tm),
                  _full((1, D)), _full((1, D))] + x_specs,
        out_specs=pl.BlockSpec((tm, D), lambda t: (t, 0)),
        out_shape=jax.ShapeDtypeStruct((NTOK, D), F32),
        scratch_shapes=[pltpu.VMEM((k, D), BF)],
        compiler_params=_cparams(1),
        name="out_proj",
    )(uc, ul, w, bias.reshape(1, D), mod, ln_g.reshape(1, D), ln_b.reshape(1, D), *x_args)


ML_TM = 512
N_HP = H // 2
SCAN_PAIRS = 2
NG = 8


def _log_sigmoid(g):
    return jnp.minimum(g, 0.0) - jnp.log1p(jnp.exp(-jnp.abs(g)))


def _ml_proj_kernel(mod_ref, w_in_ref, wg_ref, wgt_ref, bg_ref, bgt_ref, *refs):
    x_refs = refs[:-9]
    qt_ref, k_ref, kt_ref, vt_ref, o_ref, g_ref, gt_ref = refs[-9:-2]
    wt_ref, wo_ref = refs[-2:]
    r0, r1, r2 = QK, QK + 4 * NG, 2 * QK + 4 * NG

    @pl.when(pl.program_id(0) == 0)
    def _():
        wt_ref[0:r0, :] = w_in_ref[:, 0:QK].T.astype(BF)
        wt_ref[r0:r1, :] = wgt_ref[...]
        wt_ref[r1:r2, :] = w_in_ref[:, QK:2 * QK].T.astype(BF)
        wt_ref[r2:, :] = w_in_ref[:, 2 * QK:2 * QK + V].T.astype(BF)
        wo_ref[...] = w_in_ref[:, 2 * QK + V:].astype(BF)

    m = mod_ref[...]
    h = (_read_x(x_refs, ML_TM) * (1.0 + m[1:2]) + m[0:1]).astype(BF)
    tr = _dot_nt(wt_ref[...], h)
    qt = tr[:r0].astype(BF)
    ktf = tr[r1:r2] * (DQK ** -0.5)
    kt = ktf.astype(BF)
    k = ktf.T.astype(BF)
    vt = tr[r2:].astype(BF)
    o = _dot(h, wo_ref[...])
    g = _dot(h, wg_ref[...]) + bg_ref[...]
    col = lax.broadcasted_iota(jnp.int32, g.shape, 1)
    g = jnp.where((col & 2) != 0, _log_sigmoid(g), g)
    gt = tr[r0:r1] + bgt_ref[...]
    row = lax.broadcasted_iota(jnp.int32, gt.shape, 0)
    gt = jnp.where((row & 2) != 0, _log_sigmoid(gt), gt)
    qt_ref[...] = qt
    k_ref[...] = k
    kt_ref[...] = kt
    vt_ref[...] = vt
    o_ref[...] = o
    for hp in range(N_HP):
        g_ref[hp] = g[:, hp * NG:(hp + 1) * NG]
        gt_ref[hp] = gt[hp * NG:(hp + 1) * NG, :]


def _ml_proj(x, mod, layer, w_in, wg, wgt, bg, bgt):
    tm = ML_TM
    res = lambda shape: pl.BlockSpec(shape, lambda t: (0,) * len(shape), pipeline_mode=pl.Buffered(1))
    x_specs, x_args = _x_specs(x, tm)
    return pl.pallas_call(
        _ml_proj_kernel,
        grid=(NTOK // tm,),
        in_specs=[_mod_spec(layer, tm),
                  res((D, 2 * QK + 2 * V)), res((D, 4 * NG)), res((4 * NG, D)),
                  res((1, 4 * NG)), res((4 * NG, 1))] + x_specs,
        out_specs=[pl.BlockSpec((QK, tm), lambda t: (0, t)),
                   pl.BlockSpec((tm, QK), lambda t: (t, 0)),
                   pl.BlockSpec((QK, tm), lambda t: (0, t)),
                   pl.BlockSpec((V, tm), lambda t: (0, t)),
                   pl.BlockSpec((tm, V), lambda t: (t, 0)),
                   pl.BlockSpec((N_HP, tm, NG), lambda t: (0, t, 0)),
                   pl.BlockSpec((N_HP, NG, tm), lambda t: (0, 0, t))],
        out_shape=[jax.ShapeDtypeStruct((QK, NTOK), BF),
                   jax.ShapeDtypeStruct((NTOK, QK), BF),
                   jax.ShapeDtypeStruct((QK, NTOK), BF),
                   jax.ShapeDtypeStruct((V, NTOK), BF),
                   jax.ShapeDtypeStruct((NTOK, V), F32),
                   jax.ShapeDtypeStruct((N_HP, NTOK, NG), F32),
                   jax.ShapeDtypeStruct((N_HP, NG, NTOK), F32)],
        scratch_shapes=[pltpu.VMEM((2 * QK + 4 * NG + V, D), BF),
                        pltpu.VMEM((D, V), BF)],
        compiler_params=_cparams(1),
        name="mlstm_proj",
    )(mod, w_in, wg, wgt, bg, bgt, *x_args)


DHA = DHV + SUBLANES


def _scan_chunk(k_c, kt_c, qt_c, vta, rr_col, rr_row, b_row, m, cta, d, need_state):
    L = k_c.shape[0]
    si = lax.broadcasted_iota(jnp.int32, (L, L), 0)
    ti = lax.broadcasted_iota(jnp.int32, (L, L), 1)
    mask = (si <= ti) if d == 0 else (si >= ti)
    rrm = jnp.where(mask, rr_col, -jnp.inf)
    g = jnp.maximum(jnp.max(rrm, axis=0, keepdims=True), m)
    st = _dot(k_c, qt_c) * jnp.exp(rrm - g)
    num = _dot(vta[:DHV], st.astype(BF))
    den = jnp.sum(st, axis=0, keepdims=True)
    if cta is not None:
        a = jnp.exp(m - g)
        qc = _dot(cta.astype(BF), qt_c)
        num = num + a * qc[:DHV]
        den = den + a * qc[DHV:DHV + 1]
    mt = b_row + g
    ht = num * (1.0 / jnp.maximum(jnp.abs(den), jnp.exp(-mt)))
    if not need_state:
        return ht, None, None
    last = L - 1 if d == 0 else 0
    m_new = mt[:, last:last + 1]
    b_last = b_row[:, last:last + 1]
    w = jnp.exp(b_last + rr_row - m_new)
    kw = (kt_c.astype(F32) * w).astype(BF)
    if cta is None:
        return ht, (_dot_nt(kw, vta[:DHV]), _dot_nt(vta[DHV:], kw)), m_new
    c_new = jnp.exp(b_last + m - m_new) * cta + _dot_nt(vta, kw)
    return ht, c_new, m_new


def _scan_kernel(*refs, is_ctx):
    if is_ctx:
        qt_ref, k_ref, kt_ref, vt_ref, o_ref, g_ref, gt_ref, ng_ref, u_ref, cst_ref, nmst_ref = refs
    else:
        qt_ref, k_ref, kt_ref, vt_ref, o_ref, g_ref, gt_ref, ng_ref, c0_ref, m0_ref, u_ref = refs
    L = CHUNK
    nch = DEC_SEQ // L
    ri = lax.broadcasted_iota(jnp.int32, (L, L), 0)
    ci = lax.broadcasted_iota(jnp.int32, (L, L), 1)
    lower = jnp.where(ci <= ri, 1.0, 0.0).astype(BF)
    upper = jnp.where(ci >= ri, 1.0, 0.0).astype(BF)
    sub = lax.broadcasted_iota(jnp.int32, (DHA - DHV, L), 0)
    ones_rows = jnp.where(sub == 0, 1.0, 0.0).astype(BF)

    rr_cols, rr_rows, b_rows = [], [], []
    for pp in range(SCAN_PAIRS):
        tparts = _split3(jnp.concatenate([gt_ref[pp, :, c * L:(c + 1) * L] for c in range(nch)], axis=0))
        b_row_all = (sum(_dot(p, upper) for p in tparts), sum(_dot(p, lower) for p in tparts))
        rr_cols.append([])
        rr_rows.append([])
        b_rows.append([])
        for c in range(nch):
            gc = g_ref[pp, c * L:(c + 1) * L, :]
            gtc = gt_ref[pp, :, c * L:(c + 1) * L]
            gparts = _split3(gc)
            b_col = (sum(_dot(lower, p) for p in gparts), sum(_dot(upper, p) for p in gparts))
            b_row = tuple(b[c * NG:(c + 1) * NG, :] for b in b_row_all)
            rr_cols[pp].append([gc[:, 0:NG - 2] - b[:, 2:NG] for b in b_col])
            rr_rows[pp].append([gtc[0:NG - 2, :] - b[2:NG, :] for b in b_row])
            b_rows[pp].append(b_row)

    states, us = [], []
    for hl in range(2 * SCAN_PAIRS):
        pp, hh = divmod(hl, 2)
        hsum = [None] * nch
        for d in range(2):
            ig_i, lf_i = d * 4 + hh, d * 4 + 2 + hh
            if is_ctx:
                m, cta = jnp.zeros((1, 1), F32), None
            else:
                m = m0_ref[d, hl][:, 0:1]
                cta = c0_ref[d, hl]
            order = range(nch) if d == 0 else range(nch - 1, -1, -1)
            for n, c in enumerate(order):
                sl = slice(c * L, (c + 1) * L)
                hs = slice(hl * DQK, (hl + 1) * DQK)
                vta = jnp.concatenate([vt_ref[hl * DHV:(hl + 1) * DHV, sl], ones_rows], axis=0)
                need_state = is_ctx or n < nch - 1
                hcur, c_new, m_new = _scan_chunk(
                    k_ref[sl, hs], kt_ref[hs, sl], qt_ref[hs, sl], vta,
                    rr_cols[pp][c][d][:, ig_i:ig_i + 1], rr_rows[pp][c][d][ig_i:ig_i + 1, :],
                    b_rows[pp][c][d][lf_i:lf_i + 1, :], m, cta, d, need_state)
                hsum[c] = hcur if d == 0 else hsum[c] + hcur
                if is_ctx:
                    states.append(((c, d, hl), c_new, m_new))
                else:
                    m, cta = m_new, c_new

        for c in range(nch):
            sl = slice(c * L, (c + 1) * L)
            mu = jnp.mean(hsum[c], axis=0, keepdims=True)
            hc = hsum[c] - mu
            var = jnp.mean(hc * hc, axis=0, keepdims=True)
            hn = hc * lax.rsqrt(var + LN_EPS) * ng_ref[hl * DHV:(hl + 1) * DHV, :]
            og = jax.nn.sigmoid(o_ref[sl, hl * DHV:(hl + 1) * DHV])
            us.append(((sl, slice(hl * DHV, (hl + 1) * DHV)), (hn.T * og).astype(BF)))

    for idx, u in us:
        u_ref[idx] = u
    for idx, (c_fin, n_fin), m_fin in states:
        cst_ref[idx] = c_fin
        nmst_ref[idx] = jnp.concatenate([n_fin, jnp.broadcast_to(m_fin, (SUBLANES, LANES - DQK))], axis=1)


def _ml_scan(qt, k, kt, vt, o, g, gt, norm_g, c0aug=None, m0b=None):
    is_ctx = c0aug is None
    tt = DEC_SEQ
    toff = 0 if is_ctx else NCTX // tt
    nt = (NCTX if is_ctx else NLAT) // tt
    nh = 2 * SCAN_PAIRS
    in_specs = [pl.BlockSpec((nh * DQK, tt), lambda t, p: (p, toff + t)),
                pl.BlockSpec((tt, nh * DQK), lambda t, p: (toff + t, p)),
                pl.BlockSpec((nh * DQK, tt), lambda t, p: (p, toff + t)),
                pl.BlockSpec((nh * DHV, tt), lambda t, p: (p, toff + t)),
                pl.BlockSpec((tt, nh * DHV), lambda t, p: (toff + t, p)),
                pl.BlockSpec((SCAN_PAIRS, tt, NG), lambda t, p: (p, toff + t, 0)),
                pl.BlockSpec((SCAN_PAIRS, NG, tt), lambda t, p: (p, 0, toff + t)),
                pl.BlockSpec((nh * DHV, 1), lambda t, p: (p, 0))]
    args = [qt, k, kt, vt, o, g, gt, norm_g.reshape(V, 1)]
    u_spec = pl.BlockSpec((tt, nh * DHV), lambda t, p: (t, p))
    u_shape = jax.ShapeDtypeStruct((nt * tt, V), BF)
    if is_ctx:
        nseq = tt // SEQ
        out_specs = [u_spec,
                     pl.BlockSpec((nseq, 2, nh, DQK, DHV), lambda t, p: (t, 0, p, 0, 0)),
                     pl.BlockSpec((nseq, 2, nh, SUBLANES, LANES), lambda t, p: (t, 0, p, 0, 0))]
        out_shape = [u_shape,
                     jax.ShapeDtypeStruct((BATCH, 2, H, DQK, DHV), F32),
                     jax.ShapeDtypeStruct((BATCH, 2, H, SUBLANES, LANES), F32)]
    else:
        in_specs += [pl.BlockSpec((None, 2, nh, DHA, DQK), lambda t, p: (t, 0, p, 0, 0)),
                     pl.BlockSpec((None, 2, nh, 1, LANES), lambda t, p: (t, 0, p, 0, 0))]
        args += [c0aug, m0b]
        out_specs = [u_spec]
        out_shape = [u_shape]
    return pl.pallas_call(
        functools.partial(_scan_kernel, is_ctx=is_ctx),
        grid=(nt, N_HP // SCAN_PAIRS),
        in_specs=in_specs,
        out_specs=out_specs,
        out_shape=out_shape,
        compiler_params=_cparams(2),
        name="mlstm_scan_ctx" if is_ctx else "mlstm_scan_lat",
    )(*args)


SC_TM = 512
SC_HALO = SUBLANES


def _sconv_kernel(x_ref, prev_ref, next_ref, mod_ref, w_in_f32_ref, cw_ref, w_out_f32_ref, g_ref, b_ref, o_ref,
                  e_ref, w_in_ref, w_out_ref):
    tm = SC_TM

    @pl.when(pl.program_id(0) == 0)
    def _():
        w_in_ref[...] = w_in_f32_ref[...].astype(BF)
        w_out_ref[...] = w_out_f32_ref[...].astype(BF)

    row0 = pl.program_id(0) * tm
    slen = _seq_len(row0)
    pos = (row0 + lax.broadcasted_iota(jnp.int32, (tm, 1), 0)) & (slen - 1)
    m = mod_ref[...]
    x = x_ref[...]
    xe = jnp.concatenate([prev_ref[...], x, next_ref[...]], axis=0)
    he = (xe * (1.0 + m[1:2]) + m[0:1]).astype(BF)
    cu = _dot(he, w_in_ref[:, D:2 * D]) * _dot(he, w_in_ref[:, 2 * D:3 * D])
    bg = _dot(he[SC_HALO:SC_HALO + tm], w_in_ref[:, 0:D])
    e_ref[...] = cu
    prev = jnp.where(pos == 0, 0.0, e_ref[SC_HALO - 1:SC_HALO - 1 + tm, :])
    nxt = jnp.where(pos == slen - 1, 0.0, e_ref[SC_HALO + 1:SC_HALO + 1 + tm, :])
    conv = cw_ref[0:1, :] * prev + cw_ref[1:2, :] * e_ref[SC_HALO:SC_HALO + tm, :] + cw_ref[2:3, :] * nxt
    y = _dot((bg * conv).astype(BF), w_out_ref[...])
    o_ref[...] = _ln(ALPHA * x + m[2:3] * y, g_ref[...], b_ref[...])


def _sconv(x, mod, layer, w_in, conv_w, w_out, ln_g, ln_b):
    tm = SC_TM
    hb = tm // SC_HALO
    nhb = NTOK // SC_HALO
    return pl.pallas_call(
        _sconv_kernel,
        grid=(NTOK // tm,),
        in_specs=[pl.BlockSpec((tm, D), lambda t: (t, 0)),
                  pl.BlockSpec((SC_HALO, D), lambda t: (jnp.maximum(t * hb - 1, 0), 0)),
                  pl.BlockSpec((SC_HALO, D), lambda t: (jnp.minimum((t + 1) * hb, nhb - 1), 0)),
                  _mod_spec(layer, tm),
                  pl.BlockSpec((D, 3 * D), lambda t: (0, 0), pipeline_mode=pl.Buffered(1)),
                  _full((3, D)),
                  pl.BlockSpec((D, D), lambda t: (0, 0), pipeline_mode=pl.Buffered(1)),
                  _full((1, D)), _full((1, D))],
        out_specs=pl.BlockSpec((tm, D), lambda t: (t, 0)),
        out_shape=jax.ShapeDtypeStruct((NTOK, D), F32),
        scratch_shapes=[pltpu.VMEM((tm + 2 * SC_HALO, D), F32),
                        pltpu.VMEM((D, 3 * D), BF),
                        pltpu.VMEM((D, D), BF)],
        compiler_params=_cparams(1),
        name="sconv",
    )(x, x, x, mod, w_in, conv_w, w_out, ln_g.reshape(1, D), ln_b.reshape(1, D))


PL_SEG = 256
PL_NSEG = 4
PL_TM = PL_SEG * PL_NSEG
PL_HALO = 16
PL_STRIDE = PL_SEG + 2 * PL_HALO


def _pool_kernel(x_ref, prev_ref, next_ref, mod_ref, w_ref, pb_ref, ps_ref, g_ref, b_ref, o_ref,
                 e_ref, s2_ref, s4_ref, s8_ref):
    tm, seg, hl, gw = PL_TM, PL_SEG, PL_HALO, GROUP_W
    row0 = pl.program_id(0) * tm
    slen = _seq_len(row0)
    m = mod_ref[...]
    x = x_ref[...]
    h = x * (1.0 + m[1:2]) + m[0:1]
    hp = prev_ref[...] * (1.0 + m[1:2]) + m[0:1]
    hn = next_ref[...] * (1.0 + m[1:2]) + m[0:1]
    n0 = PL_NSEG * PL_STRIDE
    for i in range(PL_NSEG):
        base = i * PL_STRIDE
        pos0 = (row0 + i * seg) & (slen - 1)
        before = hp if i == 0 else h[i * seg - hl:i * seg]
        after = hn if i == PL_NSEG - 1 else h[(i + 1) * seg:(i + 1) * seg + hl]
        e_ref[base:base + hl, :] = jnp.where(pos0 != 0, before, 0.0)
        e_ref[base + hl:base + hl + seg, :] = h[i * seg:(i + 1) * seg]
        e_ref[base + hl + seg:base + PL_STRIDE, :] = jnp.where(pos0 + seg != slen, after, 0.0)
    e_ref[n0:, :] = jnp.zeros((SUBLANES, D), F32)
    n2, n4, n8 = n0, n0 - SUBLANES, n0 - 2 * SUBLANES
    s2_ref[...] = e_ref[0:n2, :] + e_ref[1:n2 + 1, :]
    s4_ref[...] = s2_ref[0:n4, gw:] + s2_ref[2:n4 + 2, gw:]
    s8_ref[...] = s4_ref[0:n8, gw:] + s4_ref[4:n8 + 4, gw:]

    def rows(ref, off, lanes):
        return jnp.concatenate([ref[i * PL_STRIDE + off:i * PL_STRIDE + off + seg, lanes]
                                for i in range(PL_NSEG)], axis=0)

    tots = [rows(s2_ref, hl - 1, slice(0, gw)),
            rows(s4_ref, hl - 2, slice(0, gw)),
            rows(s8_ref, hl - 4, slice(0, gw)),
            rows(s8_ref, hl - 8, slice(gw, 2 * gw)) + rows(s8_ref, hl, slice(gw, 2 * gw))]
    pos = (row0 + lax.broadcasted_iota(jnp.int32, (tm, 1), 0)) & (slen - 1)
    ys = []
    for gi, win in enumerate(POOL_WINDOWS):
        cs = slice(gi * gw, (gi + 1) * gw)
        back = win // 2
        cnt = jnp.minimum(pos + (win - back), slen) - jnp.maximum(pos - back, 0)
        p = tots[gi] / cnt.astype(F32) - h[:, cs]
        ys.append(_dot(p.astype(BF), w_ref[gi]))
    y = (jnp.concatenate(ys, axis=1) + pb_ref[...]) * ps_ref[...]
    o_ref[...] = _ln(ALPHA * x + m[2:3] * y, g_ref[...], b_ref[...])


def _pool(x, mod, layer, w, pb, ps, ln_g, ln_b):
    assert POOL_WINDOWS == (2, 4, 8, 16) and PL_HALO >= POOL_WINDOWS[-1] // 2
    assert SEQ % PL_SEG == 0 and DEC_SEQ % PL_SEG == 0
    tm = PL_TM
    hb = tm // PL_HALO
    nhb = NTOK // PL_HALO
    n0 = PL_NSEG * PL_STRIDE
    return pl.pallas_call(
        _pool_kernel,
        grid=(NTOK // tm,),
        in_specs=[pl.BlockSpec((tm, D), lambda t: (t, 0)),
                  pl.BlockSpec((PL_HALO, D), lambda t: (jnp.maximum(t * hb - 1, 0), 0)),
                  pl.BlockSpec((PL_HALO, D), lambda t: (jnp.minimum((t + 1) * hb, nhb - 1), 0)),
                  _mod_spec(layer, tm),
                  _full((N_GROUPS, GROUP_W, GROUP_W)),
                  _full((1, D)), _full((1, D)), _full((1, D)), _full((1, D))],
        out_specs=pl.BlockSpec((tm, D), lambda t: (t, 0)),
        out_shape=jax.ShapeDtypeStruct((NTOK, D), F32),
        scratch_shapes=[pltpu.VMEM((n0 + SUBLANES, D), F32),
                        pltpu.VMEM((n0, D), F32),
                        pltpu.VMEM((n0 - SUBLANES, D - GROUP_W), F32),
                        pltpu.VMEM((n0 - 2 * SUBLANES, D - 2 * GROUP_W), F32)],
        compiler_params=_cparams(1),
        name="pool",
    )(x, x, x, mod, w, pb.reshape(1, D), ps.reshape(1, D), ln_g.reshape(1, D), ln_b.reshape(1, D))


FT_TM = 1024
FT_TK = 256
FT_FOLD_MIN_SEQ = 512
FT_PAD = SUBLANES


def _dft_mats(n, scale):
    k = np.arange(n, dtype=np.int64)
    ang = 2.0 * np.pi * ((k[:, None] * k[None, :]) % n).astype(np.float64) / n
    return np.stack([np.cos(ang) * scale, np.sin(ang) * scale]).astype(np.float32)


def _dot_x3(a_parts, b_parts):
    (ah, al), (bh, bl) = a_parts, b_parts
    return _dot(ah, bh) + (_dot(ah, bl) + _dot(al, bh))


def _rev_rows(x):
    r = lax.broadcasted_iota(jnp.int32, (SUBLANES, x.shape[1]), 0)
    out = []
    for i in range(x.shape[0] // SUBLANES - 1, -1, -1):
        p = x[i * SUBLANES:(i + 1) * SUBLANES]
        for sh in (1, 2, 4):
            p = jnp.where((r & sh) == 0, pltpu.roll(p, SUBLANES - sh, 0), pltpu.roll(p, sh, 0))
        out.append(p)
    return jnp.concatenate(out, axis=0)


def _fourier_tile(x_ref, m, cs_ref, ccp, scp, w_ref, bias_ref, lg_ref, lb_ref, o_ref, s, tk):
    nsq = FT_TM // s
    ri = lax.broadcasted_iota(jnp.int32, (tk, tk), 0)
    ci = lax.broadcasted_iota(jnp.int32, (tk, tk), 1)
    anti = jnp.where(ri + ci == tk - 1, 1.0, 0.0).astype(BF)
    blocks = []
    half = s // 2
    row = lax.broadcasted_iota(jnp.int32, (half, 1), 0)
    for sq in range(nsq):
        base = sq * s
        h = x_ref[base:base + s, :] * (1.0 + m[1:2]) + m[0:1]
        if s >= FT_FOLD_MIN_SEQ:
            hr = jnp.where(row == 0, 0.0, pltpu.roll(_rev_rows(h[half:]), 1, 0))
            hp = _split2(h[:half] + hr)
            hm = _split2(h[:half] - hr)
            h_mid = h[half:half + 1]
        else:
            hparts = _split2(h)
        for j in range(s // (2 * tk)):
            if s >= FT_FOLD_MIN_SEQ:
                ac = _dot_x3(_split2(cs_ref[0, j, :, 0:half]), hp) + cs_ref[0, j, :, half:half + 1] * h_mid
                as_ = _dot_x3(_split2(cs_ref[1, j, :, 0:half]), hm)
            else:
                ac = _dot_x3(_split2(cs_ref[0, j]), hparts)
                as_ = _dot_x3(_split2(cs_ref[1, j]), hparts)
            direct, mirror = [], []
            for gi in range(N_GROUPS):
                cs = slice(gi * GROUP_W, (gi + 1) * GROUP_W)
                p = _dot_x3(_split2(ac[:, cs]), ccp)
                q = _dot_x3(_split2(as_[:, cs]), scp)
                direct.append((p - q)[0:tk])
                mirror.append((p + q)[1:tk + 1])
            blocks.append((base + j * tk, jnp.concatenate(direct, axis=1).astype(BF)))
            mir = jnp.concatenate(mirror, axis=1).astype(BF)
            blocks.append((base + s - (j + 1) * tk, _dot(anti, mir).astype(BF)))
    outs = []
    for r0, blk in blocks:
        y = _dot(blk, w_ref[...]) + bias_ref[...]
        outs.append((r0, _ln(ALPHA * x_ref[r0:r0 + tk, :] + m[2:3] * y, lg_ref[...], lb_ref[...])))
    for r0, out in outs:
        o_ref[r0:r0 + tk, :] = out


def _fourier_kernel(x_ref, mod_ref, csc_ref, csl_ref, cc_ref, w_f32_ref, bias_ref, lg_ref, lb_ref, o_ref, w_ref):
    @pl.when(pl.program_id(0) == 0)
    def _():
        w_ref[...] = w_f32_ref[...].astype(BF)

    m = mod_ref[...]
    ccp = _split2(cc_ref[0])
    scp = _split2(cc_ref[1])
    is_ctx = pl.program_id(0) * FT_TM < NCTX

    @pl.when(is_ctx)
    def _():
        _fourier_tile(x_ref, m, csc_ref, ccp, scp, w_ref, bias_ref, lg_ref, lb_ref, o_ref, SEQ, _ft_tk(SEQ))

    @pl.when(jnp.logical_not(is_ctx))
    def _():
        _fourier_tile(x_ref, m, csl_ref, ccp, scp, w_ref, bias_ref, lg_ref, lb_ref, o_ref, DEC_SEQ, _ft_tk(DEC_SEQ))


def _ft_tk(s):
    return min(FT_TK, s // 2)


def _half_dft_mats(s, tk):
    full = _dft_mats(s, s ** -0.5)
    rows = (np.arange(s // (2 * tk))[:, None] * tk + np.arange(tk + FT_PAD)[None, :]) % s
    return full[:, rows, :]


def _fourier(x, mod, layer, w, bias, ln_g, ln_b):
    tm = FT_TM
    assert tm % SEQ == 0 and tm % DEC_SEQ == 0 and NCTX % tm == 0
    csc = jnp.asarray(_half_dft_mats(SEQ, _ft_tk(SEQ)))
    csl = jnp.asarray(_half_dft_mats(DEC_SEQ, _ft_tk(DEC_SEQ)))
    cc = jnp.asarray(_dft_mats(GROUP_W, GROUP_W ** -0.5))
    const = lambda shape: pl.BlockSpec(shape, lambda b: (0,) * len(shape), pipeline_mode=pl.Buffered(1))
    return pl.pallas_call(
        _fourier_kernel,
        grid=(NTOK // tm,),
        in_specs=[pl.BlockSpec((tm, D), lambda b: (b, 0)),
                  _mod_spec(layer, tm),
                  const(csc.shape), const(csl.shape), const(cc.shape), const((D, D)),
                  _full((1, D)), _full((1, D)), _full((1, D))],
        out_specs=pl.BlockSpec((tm, D), lambda b: (b, 0)),
        out_shape=jax.ShapeDtypeStruct((NTOK, D), F32),
        scratch_shapes=[pltpu.VMEM((D, D), BF)],
        compiler_params=_cparams(1),
        name="fourier",
    )(x, mod, csc, csl, cc, w, bias.reshape(1, D), ln_g.reshape(1, D), ln_b.reshape(1, D))


def _pos_embed():
    rows = DEC_SEQ // GRID_W
    quarter = D // 4
    omega = 1.0 / (10000.0 ** (np.arange(quarter, dtype=np.float64) / quarter))
    rr, cc = np.meshgrid(np.arange(rows, dtype=np.float64), np.arange(GRID_W, dtype=np.float64), indexing="ij")
    er = rr.reshape(-1, 1) * omega
    ec = cc.reshape(-1, 1) * omega
    return np.concatenate([np.sin(er), np.cos(er), np.sin(ec), np.cos(ec)], axis=-1).astype(np.float32)


def _gate_perm():
    perm = np.zeros(4 * H, dtype=np.int32)
    for hp in range(N_HP):
        for d in range(2):
            for j in range(2):
                for hh in range(2):
                    perm[hp * NG + d * 4 + j * 2 + hh] = d * 2 * H + j * H + 2 * hp + hh
    return perm


def kernel(x_prompt, x_sample, state_C, state_n, state_m, c, c_ctx, w_mod, b_mod, ln_g, ln_b, mlp_w1, mlp_w2,
           ml_w_in, ml_w_gate, ml_b_gate, ml_norm_g, ml_w_out, sc_w_in, sc_conv_w, sc_w_out,
           pl_w, pl_b, pl_scale, ft_w_out, ft_b_out):
    cond = jnp.concatenate([c_ctx[None, :], c, jnp.zeros((NCOND - 1 - DEC_BATCH, D), F32)], axis=0)
    mod = _modulation(cond, w_mod, b_mod)

    x = (x_prompt.reshape(NCTX, D), x_sample.reshape(NLAT, D), jnp.asarray(_pos_embed()))

    new_c = new_n = new_m = None
    for i in range(DEPTH):
        kind, j = i % 4, i // 4
        if kind == 0:
            perm = _gate_perm()
            wg = ml_w_gate[j][:, perm].astype(BF)
            bg = ml_b_gate[j][perm]
            qt, k, kt, vt, o, g, gt = _ml_proj(
                x, mod, i, ml_w_in[j], wg, wg.T, bg.reshape(1, -1), bg.reshape(-1, 1))
            scan_args = (qt, k, kt, vt, o, g, gt, ml_norm_g[j])
            uc, cst, nmst = _ml_scan(*scan_args)
            c0 = jnp.concatenate([jnp.swapaxes(state_C[:, j], -1, -2), state_n[:, j][..., None, :],
                                  jnp.zeros((DEC_BATCH, 2, H, DHA - DHV - 1, DQK), F32)], axis=-2)
            m0 = jnp.broadcast_to(state_m[:, j][..., None, None], (DEC_BATCH, 2, H, 1, LANES))
            (ul,) = _ml_scan(*scan_args, c0, m0)
            x = _out_proj(uc, ul, ml_w_out[j], jnp.zeros((D,), F32), x, mod, i, ln_g[i, 0], ln_b[i, 0])
            new_c = cst[:, None]
            new_n = nmst[..., 0, :DQK][:, None]
            new_m = nmst[..., 0, DQK][:, None]
        elif kind == 1:
            x = _sconv(x, mod, i, sc_w_in[j], sc_conv_w[j], sc_w_out[j],
                       ln_g[i, 0], ln_b[i, 0])
        elif kind == 2:
            x = _pool(x, mod, i, pl_w[j].astype(BF), pl_b[j], pl_scale[j], ln_g[i, 0], ln_b[i, 0])
        else:
            x = _fourier(x, mod, i, ft_w_out[j], ft_b_out[j], ln_g[i, 0], ln_b[i, 0])
        x = _mlp(x, mod, i, mlp_w1, mlp_w2, ln_g[i, 1], ln_b[i, 1],
                 split_out=(i == DEPTH - 1))

    y_prompt = x[0].reshape(BATCH, SEQ, D)
    y_sample = x[1].reshape(DEC_BATCH, DEC_SEQ, D)
    return y_prompt, y_sample, new_c, new_n, new_m
```

```python
import functools

import numpy as np
import jax
import jax.numpy as jnp
from jax import lax
from jax.experimental import pallas as pl
from jax.experimental.pallas import tpu as pltpu

D = 1024
BATCH, SEQ = 16, 256
DEC_BATCH, DEC_SEQ = 4, 1024
DEPTH = 4
GRID_W = 64
H = 8
DQK = 64
DHV = 128
QK = H * DQK
V = H * DHV
N_GROUPS = 4
GROUP_W = D // N_GROUPS
POOL_WINDOWS = (2, 4, 8, 16)
D_FF = 4 * D
ALPHA = (2.0 * DEPTH) ** 0.25
LN_EPS = 1e-5
F32 = jnp.float32
BF = jnp.bfloat16

NCTX = BATCH * SEQ
NLAT = DEC_BATCH * DEC_SEQ
NTOK = NCTX + NLAT
NCOND = 8
CHUNK = 256
SUBLANES, LANES = 8, 128
V7X_VMEM_BYTES = 64 * 1024 * 1024
VMEM_LIMIT = V7X_VMEM_BYTES - 8 * 1024 * 1024


def _cparams(n_axes):
    return pltpu.CompilerParams(dimension_semantics=("arbitrary",) * n_axes,
                                vmem_limit_bytes=VMEM_LIMIT)


def _mod_row(row0):
    return jnp.where(row0 < NCTX, 0, 1 + (row0 - NCTX) // DEC_SEQ)


def _seq_len(row0):
    return jnp.where(row0 < NCTX, SEQ, DEC_SEQ)


def _ln(z, g, b):
    mu = jnp.mean(z, axis=-1, keepdims=True)
    zc = z - mu
    var = jnp.mean(zc * zc, axis=-1, keepdims=True)
    return zc * lax.rsqrt(var + LN_EPS) * g + b


def _dot(a, b):
    return jnp.dot(a, b, preferred_element_type=F32)


def _dot_nt(a, b):
    return lax.dot_general(a, b, (((1,), (1,)), ((), ())), preferred_element_type=F32)


def _split2(x):
    hi = x.astype(BF)
    lo = (x - hi.astype(F32)).astype(BF)
    return hi, lo


def _split3(x):
    hi = x.astype(BF)
    r = x - hi.astype(F32)
    mid = r.astype(BF)
    lo = (r - mid.astype(F32)).astype(BF)
    return hi, mid, lo


def _full(shape):
    n = len(shape)
    return pl.BlockSpec(shape, lambda *_: (0,) * n)


def _x_specs(x, tm):
    if not isinstance(x, tuple):
        return [pl.BlockSpec((tm, D), lambda t: (t, 0))], [x]
    nc = NCTX // tm
    return ([pl.BlockSpec((tm, D), lambda t: (jnp.minimum(t, nc - 1), 0)),
             pl.BlockSpec((tm, D), lambda t: (jnp.maximum(t - nc, 0), 0)),
             pl.BlockSpec((tm, D), lambda t: (t % (DEC_SEQ // tm), 0))], list(x))


def _read_x(x_refs, tm):
    if len(x_refs) == 1:
        return x_refs[0][...]
    xc_ref, xl_ref, pe_ref = x_refs
    return jnp.where(pl.program_id(0) * tm < NCTX, xc_ref[...], xl_ref[...] + pe_ref[...])


def _mod_spec(layer, tm):
    return pl.BlockSpec((None, None, 6, D), lambda t: (layer, _mod_row(t * tm), 0, 0))


def _mod_kernel(c_ref, w_ref, b_ref, o_ref):
    c = c_ref[...]
    s = c * jax.nn.sigmoid(c)
    o_ref[...] = _dot(s.astype(BF), w_ref[...].astype(BF)) + b_ref[...]


def _modulation(cond, w_mod, b_mod):
    tn = 3 * D
    out = pl.pallas_call(
        _mod_kernel,
        grid=(DEPTH, 6 * D // tn),
        in_specs=[pl.BlockSpec((NCOND, D), lambda i, j: (0, 0)),
                  pl.BlockSpec((None, D, tn), lambda i, j: (i, 0, j)),
                  pl.BlockSpec((None, 1, tn), lambda i, j: (i, 0, j))],
        out_specs=pl.BlockSpec((None, NCOND, tn), lambda i, j: (i, 0, j)),
        out_shape=jax.ShapeDtypeStruct((DEPTH, NCOND, 6 * D), F32),
        compiler_params=_cparams(2),
        name="modulation",
    )(cond, w_mod, b_mod.reshape(DEPTH, 1, 6 * D))
    return out.reshape(DEPTH, NCOND, 6, D)


MLP_TM = 512
MLP_FC = 512
MLP_NF = D_FF // MLP_FC
MLP_NT = NTOK // MLP_TM


def _mlp_tile(s):
    return jnp.maximum(s - (MLP_NF - 1), 0)


def _mlp_kernel(mod_ref, w1_ref, w2_ref, g_ref, b_ref, x1_ref, x_ref, *refs):
    o_refs, (w1s_ref, w2s_ref, acc_ref, h_ref) = refs[:-4], refs[-4:]
    s = pl.program_id(0)
    m = mod_ref[...]

    def pre(x):
        return (x * (1.0 + m[4:5]) + m[3:4]).astype(BF)

    def hidden(h, w1c):
        return jnp.square(jnp.maximum(_dot(h, w1c), 0.0)).astype(BF)

    def result(x, acc):
        return _ln(ALPHA * x + m[5:6] * acc, g_ref[...], b_ref[...])

    @pl.when(s < MLP_NF)
    def _():
        @pl.when(s == 0)
        def _():
            h_ref[0:MLP_TM, :] = pre(x_ref[...])
            h_ref[MLP_TM:, :] = pre(x1_ref[...])

        w1c = w1_ref[...].astype(BF)
        w2c = w2_ref[...].astype(BF)
        part = _dot(hidden(h_ref[...], w1c), w2c)
        w1s_ref[s] = w1c
        w2s_ref[s] = w2c

        @pl.when(s == 0)
        def _():
            acc_ref[...] = part

        @pl.when(s > 0)
        def _():
            acc_ref[...] += part

        @pl.when(s == MLP_NF - 1)
        def _():
            o_refs[0][...] = result(x_ref[...], acc_ref[0:MLP_TM, :])

    @pl.when(s == MLP_NF)
    def _():
        o_refs[0][...] = result(x1_ref[...], acc_ref[MLP_TM:, :])

    @pl.when(s > MLP_NF)
    def _():
        x = x_ref[...]
        h = pre(x)
        acc = jnp.zeros((MLP_TM, D), F32)
        for f in range(MLP_NF):
            acc = acc + _dot(hidden(h, w1s_ref[f]), w2s_ref[f])
        res = result(x, acc)
        if len(o_refs) == 1:
            o_refs[0][...] = res
        else:
            is_ctx = _mlp_tile(s) * MLP_TM < NCTX

            @pl.when(is_ctx)
            def _():
                o_refs[0][...] = res

            @pl.when(jnp.logical_not(is_ctx))
            def _():
                o_refs[1][...] = res


def _mlp(x, mod, layer, w1, w2, ln_g, ln_b, split_out=False):
    tm = MLP_TM
    nc = NCTX // tm
    assert nc >= 2 and MLP_NT >= 2
    if split_out:
        out_specs = [pl.BlockSpec((tm, D), lambda s: (jnp.minimum(_mlp_tile(s), nc - 1), 0)),
                     pl.BlockSpec((tm, D), lambda s: (jnp.maximum(_mlp_tile(s) - nc, 0), 0))]
        out_shape = [jax.ShapeDtypeStruct((NCTX, D), F32), jax.ShapeDtypeStruct((NLAT, D), F32)]
    else:
        out_specs = pl.BlockSpec((tm, D), lambda s: (_mlp_tile(s), 0))
        out_shape = jax.ShapeDtypeStruct((NTOK, D), F32)
    return pl.pallas_call(
        _mlp_kernel,
        grid=(MLP_NF - 1 + MLP_NT,),
        in_specs=[pl.BlockSpec((None, None, 6, D), lambda s: (layer, _mod_row(_mlp_tile(s) * tm), 0, 0)),
                  pl.BlockSpec((None, D, MLP_FC), lambda s: (layer, 0, jnp.minimum(s, MLP_NF - 1))),
                  pl.BlockSpec((None, MLP_FC, D), lambda s: (layer, jnp.minimum(s, MLP_NF - 1), 0)),
                  _full((1, D)), _full((1, D)),
                  pl.BlockSpec((tm, D), lambda s: (1, 0)),
                  pl.BlockSpec((tm, D), lambda s: (_mlp_tile(s), 0))],
        out_specs=out_specs,
        out_shape=out_shape,
        scratch_shapes=[pltpu.VMEM((MLP_NF, D, MLP_FC), BF),
                        pltpu.VMEM((MLP_NF, MLP_FC, D), BF),
                        pltpu.VMEM((2 * tm, D), F32),
                        pltpu.VMEM((2 * tm, D), BF)],
        compiler_params=_cparams(1),
        name="mlp",
    )(mod, w1, w2, ln_g.reshape(1, D), ln_b.reshape(1, D), x, x)


OUT_TM = 1024
OUT_SUB = 256


def _out_kernel(uc_ref, ul_ref, w_f32_ref, bias_ref, mod_ref, g_ref, b_ref, *refs):
    x_refs, o_ref, w_ref = refs[:-2], refs[-2], refs[-1]

    @pl.when(pl.program_id(0) == 0)
    def _():
        w_ref[...] = w_f32_ref[...].astype(BF)

    is_ctx = pl.program_id(0) * OUT_TM < NCTX
    m = mod_ref[...]
    outs = []
    for r in range(0, OUT_TM, OUT_SUB):
        rs = slice(r, r + OUT_SUB)
        u = jnp.where(is_ctx, uc_ref[rs, :], ul_ref[rs, :])
        y = _dot(u, w_ref[...]) + bias_ref[...]
        if len(x_refs) == 1:
            x = x_refs[0][rs, :]
        else:
            x = jnp.where(is_ctx, x_refs[0][rs, :], x_refs[1][rs, :] + x_refs[2][rs, :])
        outs.append(_ln(ALPHA * x + m[2:3] * y, g_ref[...], b_ref[...]))
    for r, out in zip(range(0, OUT_TM, OUT_SUB), outs):
        o_ref[r:r + OUT_SUB, :] = out


def _out_proj(uc, ul, w, bias, x, mod, layer, ln_g, ln_b):
    tm = OUT_TM
    nc = NCTX // tm
    k = w.shape[0]
    x_specs, x_args = _x_specs(x, tm)
    return pl.pallas_call(
        _out_kernel,
        grid=(NTOK // tm,),
        in_specs=[pl.BlockSpec((tm, k), lambda t: (jnp.minimum(t, nc - 1), 0)),
                  pl.BlockSpec((tm, k), lambda t: (jnp.maximum(t - nc, 0), 0)),
                  pl.BlockSpec((k, D), lambda t: (0, 0), pipeline_mode=pl.Buffered(1)),
                  _full((1, D)),
                  _mod_spec(layer, tm),
                  _full((1, D)), _full((1, D))] + x_specs,
        out_specs=pl.BlockSpec((tm, D), lambda t: (t, 0)),
        out_shape=jax.ShapeDtypeStruct((NTOK, D), F32),
        scratch_shapes=[pltpu.VMEM((k, D), BF)],
        compiler_params=_cparams(1),
        name="out_proj",
    )(uc, ul, w, bias.reshape(1, D), mod, ln_g.reshape(1, D), ln_b.reshape(1, D), *x_args)


ML_TM = 512
N_HP = H // 2
SCAN_PAIRS = 2
NG = 8


def _log_sigmoid(g):
    return jnp.minimum(g, 0.0) - jnp.log1p(jnp.exp(-jnp.abs(g)))


def _ml_proj_kernel(mod_ref, w_in_ref, wgt_ref, bgt_ref, *refs):
    x_refs = refs[:-9]
    qt_ref, k_ref, kt_ref, vt_ref, o_ref, g_ref, gt_ref = refs[-9:-2]
    wt_ref, wo_ref = refs[-2:]
    r0, r1, r2 = QK, QK + 4 * NG, 2 * QK + 4 * NG

    @pl.when(pl.program_id(0) == 0)
    def _():
        wt_ref[0:r0, :] = w_in_ref[:, 0:QK].T.astype(BF)
        wt_ref[r0:r1, :] = wgt_ref[...]
        wt_ref[r1:r2, :] = w_in_ref[:, QK:2 * QK].T.astype(BF)
        wt_ref[r2:, :] = w_in_ref[:, 2 * QK:2 * QK + V].T.astype(BF)
        wo_ref[...] = w_in_ref[:, 2 * QK + V:].astype(BF)

    m = mod_ref[...]
    h = (_read_x(x_refs, ML_TM) * (1.0 + m[1:2]) + m[0:1]).astype(BF)
    tr = _dot_nt(wt_ref[...], h)
    qt = tr[:r0].astype(BF)
    ktf = tr[r1:r2] * (DQK ** -0.5)
    kt = ktf.astype(BF)
    k = ktf.T.astype(BF)
    vt = tr[r2:].astype(BF)
    o = _dot(h, wo_ref[...])
    gt = tr[r0:r1] + bgt_ref[...]
    row = lax.broadcasted_iota(jnp.int32, gt.shape, 0)
    gt = jnp.where((row & 2) != 0, _log_sigmoid(gt), gt)
    g = jnp.concatenate([gt, jnp.zeros((LANES - 4 * NG, ML_TM), F32)], axis=0).T[:, 0:4 * NG]
    qt_ref[...] = qt
    k_ref[...] = k
    kt_ref[...] = kt
    vt_ref[...] = vt
    o_ref[...] = o
    for hp in range(N_HP):
        g_ref[hp] = g[:, hp * NG:(hp + 1) * NG]
        gt_ref[hp] = gt[hp * NG:(hp + 1) * NG, :]


def _ml_proj(x, mod, layer, w_in, wgt, bgt):
    tm = ML_TM
    res = lambda shape: pl.BlockSpec(shape, lambda t: (0,) * len(shape), pipeline_mode=pl.Buffered(1))
    x_specs, x_args = _x_specs(x, tm)
    return pl.pallas_call(
        _ml_proj_kernel,
        grid=(NTOK // tm,),
        in_specs=[_mod_spec(layer, tm),
                  res((D, 2 * QK + 2 * V)), res((4 * NG, D)), res((4 * NG, 1))] + x_specs,
        out_specs=[pl.BlockSpec((QK, tm), lambda t: (0, t)),
                   pl.BlockSpec((tm, QK), lambda t: (t, 0)),
                   pl.BlockSpec((QK, tm), lambda t: (0, t)),
                   pl.BlockSpec((V, tm), lambda t: (0, t)),
                   pl.BlockSpec((tm, V), lambda t: (t, 0)),
                   pl.BlockSpec((N_HP, tm, NG), lambda t: (0, t, 0)),
                   pl.BlockSpec((N_HP, NG, tm), lambda t: (0, 0, t))],
        out_shape=[jax.ShapeDtypeStruct((QK, NTOK), BF),
                   jax.ShapeDtypeStruct((NTOK, QK), BF),
                   jax.ShapeDtypeStruct((QK, NTOK), BF),
                   jax.ShapeDtypeStruct((V, NTOK), BF),
                   jax.ShapeDtypeStruct((NTOK, V), F32),
                   jax.ShapeDtypeStruct((N_HP, NTOK, NG), F32),
                   jax.ShapeDtypeStruct((N_HP, NG, NTOK), F32)],
        scratch_shapes=[pltpu.VMEM((2 * QK + 4 * NG + V, D), BF),
                        pltpu.VMEM((D, V), BF)],
        compiler_params=_cparams(1),
        name="mlstm_proj",
    )(mod, w_in, wgt, bgt, *x_args)


DHA = DHV + SUBLANES


def _scan_chunk(k_c, kt_c, qt_c, vta, rr_col, rr_row, b_row, m, cta, d, need_state):
    L = k_c.shape[0]
    si = lax.broadcasted_iota(jnp.int32, (L, L), 0)
    ti = lax.broadcasted_iota(jnp.int32, (L, L), 1)
    mask = (si <= ti) if d == 0 else (si >= ti)
    rrm = jnp.where(mask, rr_col, -jnp.inf)
    g = jnp.maximum(jnp.max(rrm, axis=0, keepdims=True), m)
    st = _dot(k_c, qt_c) * jnp.exp(rrm - g)
    num = _dot(vta[:DHV], st.astype(BF))
    den = jnp.sum(st, axis=0, keepdims=True)
    if cta is not None:
        a = jnp.exp(m - g)
        qc = _dot(cta.astype(BF), qt_c)
        num = num + a * qc[:DHV]
        den = den + a * qc[DHV:DHV + 1]
    mt = b_row + g
    ht = num * (1.0 / jnp.maximum(jnp.abs(den), jnp.exp(-mt)))
    if not need_state:
        return ht, None, None
    last = L - 1 if d == 0 else 0
    m_new = mt[:, last:last + 1]
    b_last = b_row[:, last:last + 1]
    w = jnp.exp(b_last + rr_row - m_new)
    kw = (kt_c.astype(F32) * w).astype(BF)
    if cta is None:
        return ht, (_dot_nt(kw, vta[:DHV]), _dot_nt(vta[DHV:], kw)), m_new
    c_new = jnp.exp(b_last + m - m_new) * cta + _dot_nt(vta, kw)
    return ht, c_new, m_new


def _scan_kernel(*refs, is_ctx):
    if is_ctx:
        qt_ref, k_ref, kt_ref, vt_ref, o_ref, g_ref, gt_ref, ng_ref, u_ref, cst_ref, nmst_ref = refs
    else:
        qt_ref, k_ref, kt_ref, vt_ref, o_ref, g_ref, gt_ref, ng_ref, c0_ref, m0_ref, u_ref = refs
    L = CHUNK
    nch = DEC_SEQ // L
    ri = lax.broadcasted_iota(jnp.int32, (L, L), 0)
    ci = lax.broadcasted_iota(jnp.int32, (L, L), 1)
    lower = jnp.where(ci <= ri, 1.0, 0.0).astype(BF)
    upper = jnp.where(ci >= ri, 1.0, 0.0).astype(BF)
    sub = lax.broadcasted_iota(jnp.int32, (DHA - DHV, L), 0)
    ones_rows = jnp.where(sub == 0, 1.0, 0.0).astype(BF)

    rr_cols, rr_rows, b_rows = [], [], []
    for pp in range(SCAN_PAIRS):
        tparts = _split3(jnp.concatenate([gt_ref[pp, :, c * L:(c + 1) * L] for c in range(nch)], axis=0))
        b_row_all = (sum(_dot(p, upper) for p in tparts), sum(_dot(p, lower) for p in tparts))
        rr_cols.append([])
        rr_rows.append([])
        b_rows.append([])
        for c in range(nch):
            gc = g_ref[pp, c * L:(c + 1) * L, :]
            gtc = gt_ref[pp, :, c * L:(c + 1) * L]
            gparts = _split3(gc)
            b_col = (sum(_dot(lower, p) for p in gparts), sum(_dot(upper, p) for p in gparts))
            b_row = tuple(b[c * NG:(c + 1) * NG, :] for b in b_row_all)
            rr_cols[pp].append([gc[:, 0:NG - 2] - b[:, 2:NG] for b in b_col])
            rr_rows[pp].append([gtc[0:NG - 2, :] - b[2:NG, :] for b in b_row])
            b_rows[pp].append(b_row)

    states, us = [], []
    for hl in range(2 * SCAN_PAIRS):
        pp, hh = divmod(hl, 2)
        hsum = [None] * nch
        for d in range(2):
            ig_i, lf_i = d * 4 + hh, d * 4 + 2 + hh
            if is_ctx:
                m, cta = jnp.zeros((1, 1), F32), None
            else:
                m = m0_ref[d, hl][:, 0:1]
                cta = c0_ref[d, hl]
            order = range(nch) if d == 0 else range(nch - 1, -1, -1)
            for n, c in enumerate(order):
                sl = slice(c * L, (c + 1) * L)
                hs = slice(hl * DQK, (hl + 1) * DQK)
                vta = jnp.concatenate([vt_ref[hl * DHV:(hl + 1) * DHV, sl], ones_rows], axis=0)
                need_state = is_ctx or n < nch - 1
                hcur, c_new, m_new = _scan_chunk(
                    k_ref[sl, hs], kt_ref[hs, sl], qt_ref[hs, sl], vta,
                    rr_cols[pp][c][d][:, ig_i:ig_i + 1], rr_rows[pp][c][d][ig_i:ig_i + 1, :],
                    b_rows[pp][c][d][lf_i:lf_i + 1, :], m, cta, d, need_state)
                hsum[c] = hcur if d == 0 else hsum[c] + hcur
                if is_ctx:
                    states.append(((c, d, hl), c_new, m_new))
                else:
                    m, cta = m_new, c_new

        for c in range(nch):
            sl = slice(c * L, (c + 1) * L)
            mu = jnp.mean(hsum[c], axis=0, keepdims=True)
            hc = hsum[c] - mu
            var = jnp.mean(hc * hc, axis=0, keepdims=True)
            hn = hc * lax.rsqrt(var + LN_EPS) * ng_ref[hl * DHV:(hl + 1) * DHV, :]
            og = jax.nn.sigmoid(o_ref[sl, hl * DHV:(hl + 1) * DHV])
            us.append(((sl, slice(hl * DHV, (hl + 1) * DHV)), (hn.T * og).astype(BF)))

    for idx, u in us:
        u_ref[idx] = u
    for idx, (c_fin, n_fin), m_fin in states:
        cst_ref[idx] = c_fin
        nmst_ref[idx] = jnp.concatenate([n_fin, jnp.broadcast_to(m_fin, (SUBLANES, LANES - DQK))], axis=1)


def _ml_scan(qt, k, kt, vt, o, g, gt, norm_g, c0aug=None, m0b=None):
    is_ctx = c0aug is None
    tt = DEC_SEQ
    toff = 0 if is_ctx else NCTX // tt
    nt = (NCTX if is_ctx else NLAT) // tt
    nh = 2 * SCAN_PAIRS
    in_specs = [pl.BlockSpec((nh * DQK, tt), lambda t, p: (p, toff + t)),
                pl.BlockSpec((tt, nh * DQK), lambda t, p: (toff + t, p)),
                pl.BlockSpec((nh * DQK, tt), lambda t, p: (p, toff + t)),
                pl.BlockSpec((nh * DHV, tt), lambda t, p: (p, toff + t)),
                pl.BlockSpec((tt, nh * DHV), lambda t, p: (toff + t, p)),
                pl.BlockSpec((SCAN_PAIRS, tt, NG), lambda t, p: (p, toff + t, 0)),
                pl.BlockSpec((SCAN_PAIRS, NG, tt), lambda t, p: (p, 0, toff + t)),
                pl.BlockSpec((nh * DHV, 1), lambda t, p: (p, 0))]
    args = [qt, k, kt, vt, o, g, gt, norm_g.reshape(V, 1)]
    u_spec = pl.BlockSpec((tt, nh * DHV), lambda t, p: (t, p))
    u_shape = jax.ShapeDtypeStruct((nt * tt, V), BF)
    if is_ctx:
        nseq = tt // SEQ
        out_specs = [u_spec,
                     pl.BlockSpec((nseq, 2, nh, DQK, DHV), lambda t, p: (t, 0, p, 0, 0)),
                     pl.BlockSpec((nseq, 2, nh, SUBLANES, LANES), lambda t, p: (t, 0, p, 0, 0))]
        out_shape = [u_shape,
                     jax.ShapeDtypeStruct((BATCH, 2, H, DQK, DHV), F32),
                     jax.ShapeDtypeStruct((BATCH, 2, H, SUBLANES, LANES), F32)]
    else:
        in_specs += [pl.BlockSpec((None, 2, nh, DHA, DQK), lambda t, p: (t, 0, p, 0, 0)),
                     pl.BlockSpec((None, 2, nh, 1, LANES), lambda t, p: (t, 0, p, 0, 0))]
        args += [c0aug, m0b]
        out_specs = [u_spec]
        out_shape = [u_shape]
    return pl.pallas_call(
        functools.partial(_scan_kernel, is_ctx=is_ctx),
        grid=(nt, N_HP // SCAN_PAIRS),
        in_specs=in_specs,
        out_specs=out_specs,
        out_shape=out_shape,
        compiler_params=_cparams(2),
        name="mlstm_scan_ctx" if is_ctx else "mlstm_scan_lat",
    )(*args)


SC_TM = 512
SC_HALO = SUBLANES


def _sconv_kernel(x_ref, prev_ref, next_ref, mod_ref, w_in_f32_ref, cw_ref, w_out_f32_ref, g_ref, b_ref, o_ref,
                  e_ref, w_in_ref, w_out_ref):
    tm = SC_TM

    @pl.when(pl.program_id(0) == 0)
    def _():
        w_in_ref[...] = w_in_f32_ref[...].astype(BF)
        w_out_ref[...] = w_out_f32_ref[...].astype(BF)

    row0 = pl.program_id(0) * tm
    slen = _seq_len(row0)
    pos = (row0 + lax.broadcasted_iota(jnp.int32, (tm, 1), 0)) & (slen - 1)
    m = mod_ref[...]
    x = x_ref[...]
    xe = jnp.concatenate([prev_ref[...], x, next_ref[...]], axis=0)
    he = (xe * (1.0 + m[1:2]) + m[0:1]).astype(BF)
    cu = _dot(he, w_in_ref[:, D:2 * D]) * _dot(he, w_in_ref[:, 2 * D:3 * D])
    bg = _dot(he[SC_HALO:SC_HALO + tm], w_in_ref[:, 0:D])
    e_ref[...] = cu
    prev = jnp.where(pos == 0, 0.0, e_ref[SC_HALO - 1:SC_HALO - 1 + tm, :])
    nxt = jnp.where(pos == slen - 1, 0.0, e_ref[SC_HALO + 1:SC_HALO + 1 + tm, :])
    conv = cw_ref[0:1, :] * prev + cw_ref[1:2, :] * e_ref[SC_HALO:SC_HALO + tm, :] + cw_ref[2:3, :] * nxt
    y = _dot((bg * conv).astype(BF), w_out_ref[...])
    o_ref[...] = _ln(ALPHA * x + m[2:3] * y, g_ref[...], b_ref[...])


def _sconv(x, mod, layer, w_in, conv_w, w_out, ln_g, ln_b):
    tm = SC_TM
    hb = tm // SC_HALO
    nhb = NTOK // SC_HALO
    return pl.pallas_call(
        _sconv_kernel,
        grid=(NTOK // tm,),
        in_specs=[pl.BlockSpec((tm, D), lambda t: (t, 0)),
                  pl.BlockSpec((SC_HALO, D), lambda t: (jnp.maximum(t * hb - 1, 0), 0)),
                  pl.BlockSpec((SC_HALO, D), lambda t: (jnp.minimum((t + 1) * hb, nhb - 1), 0)),
                  _mod_spec(layer, tm),
                  pl.BlockSpec((D, 3 * D), lambda t: (0, 0), pipeline_mode=pl.Buffered(1)),
                  _full((3, D)),
                  pl.BlockSpec((D, D), lambda t: (0, 0), pipeline_mode=pl.Buffered(1)),
                  _full((1, D)), _full((1, D))],
        out_specs=pl.BlockSpec((tm, D), lambda t: (t, 0)),
        out_shape=jax.ShapeDtypeStruct((NTOK, D), F32),
        scratch_shapes=[pltpu.VMEM((tm + 2 * SC_HALO, D), F32),
                        pltpu.VMEM((D, 3 * D), BF),
                        pltpu.VMEM((D, D), BF)],
        compiler_params=_cparams(1),
        name="sconv",
    )(x, x, x, mod, w_in, conv_w, w_out, ln_g.reshape(1, D), ln_b.reshape(1, D))


PL_SEG = 256
PL_NSEG = 4
PL_TM = PL_SEG * PL_NSEG
PL_HALO = 16
PL_STRIDE = PL_SEG + 2 * PL_HALO


def _pool_kernel(x_ref, prev_ref, next_ref, mod_ref, w_ref, pb_ref, ps_ref, g_ref, b_ref, o_ref,
                 e_ref, s2_ref, s4_ref, s8_ref):
    tm, seg, hl, gw = PL_TM, PL_SEG, PL_HALO, GROUP_W
    row0 = pl.program_id(0) * tm
    slen = _seq_len(row0)
    m = mod_ref[...]
    x = x_ref[...]
    h = x * (1.0 + m[1:2]) + m[0:1]
    hp = prev_ref[...] * (1.0 + m[1:2]) + m[0:1]
    hn = next_ref[...] * (1.0 + m[1:2]) + m[0:1]
    n0 = PL_NSEG * PL_STRIDE
    for i in range(PL_NSEG):
        base = i * PL_STRIDE
        pos0 = (row0 + i * seg) & (slen - 1)
        before = hp if i == 0 else h[i * seg - hl:i * seg]
        after = hn if i == PL_NSEG - 1 else h[(i + 1) * seg:(i + 1) * seg + hl]
        e_ref[base:base + hl, :] = jnp.where(pos0 != 0, before, 0.0)
        e_ref[base + hl:base + hl + seg, :] = h[i * seg:(i + 1) * seg]
        e_ref[base + hl + seg:base + PL_STRIDE, :] = jnp.where(pos0 + seg != slen, after, 0.0)
    e_ref[n0:, :] = jnp.zeros((SUBLANES, D), F32)
    n2, n4, n8 = n0, n0 - SUBLANES, n0 - 2 * SUBLANES
    s2_ref[...] = e_ref[0:n2, :] + e_ref[1:n2 + 1, :]
    s4_ref[...] = s2_ref[0:n4, gw:] + s2_ref[2:n4 + 2, gw:]
    s8_ref[...] = s4_ref[0:n8, gw:] + s4_ref[4:n8 + 4, gw:]

    def rows(ref, off, lanes):
        return jnp.concatenate([ref[i * PL_STRIDE + off:i * PL_STRIDE + off + seg, lanes]
                                for i in range(PL_NSEG)], axis=0)

    tots = [rows(s2_ref, hl - 1, slice(0, gw)),
            rows(s4_ref, hl - 2, slice(0, gw)),
            rows(s8_ref, hl - 4, slice(0, gw)),
            rows(s8_ref, hl - 8, slice(gw, 2 * gw)) + rows(s8_ref, hl, slice(gw, 2 * gw))]
    pos = (row0 + lax.broadcasted_iota(jnp.int32, (tm, 1), 0)) & (slen - 1)
    ys = []
    for gi, win in enumerate(POOL_WINDOWS):
        cs = slice(gi * gw, (gi + 1) * gw)
        back = win // 2
        cnt = jnp.minimum(pos + (win - back), slen) - jnp.maximum(pos - back, 0)
        p = tots[gi] / cnt.astype(F32) - h[:, cs]
        ys.append(_dot(p.astype(BF), w_ref[gi]))
    y = (jnp.concatenate(ys, axis=1) + pb_ref[...]) * ps_ref[...]
    o_ref[...] = _ln(ALPHA * x + m[2:3] * y, g_ref[...], b_ref[...])


def _pool(x, mod, layer, w, pb, ps, ln_g, ln_b):
    assert POOL_WINDOWS == (2, 4, 8, 16) and PL_HALO >= POOL_WINDOWS[-1] // 2
    assert SEQ % PL_SEG == 0 and DEC_SEQ % PL_SEG == 0
    tm = PL_TM
    hb = tm // PL_HALO
    nhb = NTOK // PL_HALO
    n0 = PL_NSEG * PL_STRIDE
    return pl.pallas_call(
        _pool_kernel,
        grid=(NTOK // tm,),
        in_specs=[pl.BlockSpec((tm, D), lambda t: (t, 0)),
                  pl.BlockSpec((PL_HALO, D), lambda t: (jnp.maximum(t * hb - 1, 0), 0)),
                  pl.BlockSpec((PL_HALO, D), lambda t: (jnp.minimum((t + 1) * hb, nhb - 1), 0)),
                  _mod_spec(layer, tm),
                  _full((N_GROUPS, GROUP_W, GROUP_W)),
                  _full((1, D)), _full((1, D)), _full((1, D)), _full((1, D))],
        out_specs=pl.BlockSpec((tm, D), lambda t: (t, 0)),
        out_shape=jax.ShapeDtypeStruct((NTOK, D), F32),
        scratch_shapes=[pltpu.VMEM((n0 + SUBLANES, D), F32),
                        pltpu.VMEM((n0, D), F32),
                        pltpu.VMEM((n0 - SUBLANES, D - GROUP_W), F32),
                        pltpu.VMEM((n0 - 2 * SUBLANES, D - 2 * GROUP_W), F32)],
        compiler_params=_cparams(1),
        name="pool",
    )(x, x, x, mod, w, pb.reshape(1, D), ps.reshape(1, D), ln_g.reshape(1, D), ln_b.reshape(1, D))


FT_TM = 1024
FT_TK = 256
FT_FOLD_MIN_SEQ = 512
FT_PAD = SUBLANES


def _dft_mats(n, scale):
    k = np.arange(n, dtype=np.int64)
    ang = 2.0 * np.pi * ((k[:, None] * k[None, :]) % n).astype(np.float64) / n
    return np.stack([np.cos(ang) * scale, np.sin(ang) * scale]).astype(np.float32)


def _dot_x3(a_parts, b_parts):
    (ah, al), (bh, bl) = a_parts, b_parts
    return _dot(ah, bh) + (_dot(ah, bl) + _dot(al, bh))


def _rev_rows(x):
    r = lax.broadcasted_iota(jnp.int32, (SUBLANES, x.shape[1]), 0)
    out = []
    for i in range(x.shape[0] // SUBLANES - 1, -1, -1):
        p = x[i * SUBLANES:(i + 1) * SUBLANES]
        for sh in (1, 2, 4):
            p = jnp.where((r & sh) == 0, pltpu.roll(p, SUBLANES - sh, 0), pltpu.roll(p, sh, 0))
        out.append(p)
    return jnp.concatenate(out, axis=0)


def _fourier_tile(x_ref, m, cs_ref, ccp, scp, w_ref, bias_ref, lg_ref, lb_ref, o_ref, s, tk):
    nsq = FT_TM // s
    ri = lax.broadcasted_iota(jnp.int32, (tk, tk), 0)
    ci = lax.broadcasted_iota(jnp.int32, (tk, tk), 1)
    anti = jnp.where(ri + ci == tk - 1, 1.0, 0.0).astype(BF)
    blocks = []
    half = s // 2
    row = lax.broadcasted_iota(jnp.int32, (half, 1), 0)
    for sq in range(nsq):
        base = sq * s
        h = x_ref[base:base + s, :] * (1.0 + m[1:2]) + m[0:1]
        if s >= FT_FOLD_MIN_SEQ:
            hr = jnp.where(row == 0, 0.0, pltpu.roll(_rev_rows(h[half:]), 1, 0))
            hp = _split2(h[:half] + hr)
            hm = _split2(h[:half] - hr)
            h_mid = h[half:half + 1]
        else:
            hparts = _split2(h)
        for j in range(s // (2 * tk)):
            if s >= FT_FOLD_MIN_SEQ:
                ac = _dot_x3(_split2(cs_ref[0, j, :, 0:half]), hp) + cs_ref[0, j, :, half:half + 1] * h_mid
                as_ = _dot_x3(_split2(cs_ref[1, j, :, 0:half]), hm)
            else:
                ac = _dot_x3(_split2(cs_ref[0, j]), hparts)
                as_ = _dot_x3(_split2(cs_ref[1, j]), hparts)
            direct, mirror = [], []
            for gi in range(N_GROUPS):
                cs = slice(gi * GROUP_W, (gi + 1) * GROUP_W)
                p = _dot_x3(_split2(ac[:, cs]), ccp)
                q = _dot_x3(_split2(as_[:, cs]), scp)
                direct.append((p - q)[0:tk])
                mirror.append((p + q)[1:tk + 1])
            blocks.append((base + j * tk, jnp.concatenate(direct, axis=1).astype(BF)))
            mir = jnp.concatenate(mirror, axis=1).astype(BF)
            blocks.append((base + s - (j + 1) * tk, _dot(anti, mir).astype(BF)))
    outs = []
    for r0, blk in blocks:
        y = _dot(blk, w_ref[...]) + bias_ref[...]
        outs.append((r0, _ln(ALPHA * x_ref[r0:r0 + tk, :] + m[2:3] * y, lg_ref[...], lb_ref[...])))
    for r0, out in outs:
        o_ref[r0:r0 + tk, :] = out


def _fourier_kernel(x_ref, mod_ref, csc_ref, csl_ref, cc_ref, w_f32_ref, bias_ref, lg_ref, lb_ref, o_ref, w_ref):
    @pl.when(pl.program_id(0) == 0)
    def _():
        w_ref[...] = w_f32_ref[...].astype(BF)

    m = mod_ref[...]
    ccp = _split2(cc_ref[0])
    scp = _split2(cc_ref[1])
    is_ctx = pl.program_id(0) * FT_TM < NCTX

    @pl.when(is_ctx)
    def _():
        _fourier_tile(x_ref, m, csc_ref, ccp, scp, w_ref, bias_ref, lg_ref, lb_ref, o_ref, SEQ, _ft_tk(SEQ))

    @pl.when(jnp.logical_not(is_ctx))
    def _():
        _fourier_tile(x_ref, m, csl_ref, ccp, scp, w_ref, bias_ref, lg_ref, lb_ref, o_ref, DEC_SEQ, _ft_tk(DEC_SEQ))


def _ft_tk(s):
    return min(FT_TK, s // 2)


def _half_dft_mats(s, tk):
    full = _dft_mats(s, s ** -0.5)
    rows = (np.arange(s // (2 * tk))[:, None] * tk + np.arange(tk + FT_PAD)[None, :]) % s
    return full[:, rows, :]


def _fourier(x, mod, layer, w, bias, ln_g, ln_b):
    tm = FT_TM
    assert tm % SEQ == 0 and tm % DEC_SEQ == 0 and NCTX % tm == 0
    csc = jnp.asarray(_half_dft_mats(SEQ, _ft_tk(SEQ)))
    csl = jnp.asarray(_half_dft_mats(DEC_SEQ, _ft_tk(DEC_SEQ)))
    cc = jnp.asarray(_dft_mats(GROUP_W, GROUP_W ** -0.5))
    const = lambda shape: pl.BlockSpec(shape, lambda b: (0,) * len(shape), pipeline_mode=pl.Buffered(1))
    return pl.pallas_call(
        _fourier_kernel,
        grid=(NTOK // tm,),
        in_specs=[pl.BlockSpec((tm, D), lambda b: (b, 0)),
                  _mod_spec(layer, tm),
                  const(csc.shape), const(csl.shape), const(cc.shape), const((D, D)),
                  _full((1, D)), _full((1, D)), _full((1, D))],
        out_specs=pl.BlockSpec((tm, D), lambda b: (b, 0)),
        out_shape=jax.ShapeDtypeStruct((NTOK, D), F32),
        scratch_shapes=[pltpu.VMEM((D, D), BF)],
        compiler_params=_cparams(1),
        name="fourier",
    )(x, mod, csc, csl, cc, w, bias.reshape(1, D), ln_g.reshape(1, D), ln_b.reshape(1, D))


def _pos_embed():
    rows = DEC_SEQ // GRID_W
    quarter = D // 4
    omega = 1.0 / (10000.0 ** (np.arange(quarter, dtype=np.float64) / quarter))
    rr, cc = np.meshgrid(np.arange(rows, dtype=np.float64), np.arange(GRID_W, dtype=np.float64), indexing="ij")
    er = rr.reshape(-1, 1) * omega
    ec = cc.reshape(-1, 1) * omega
    return np.concatenate([np.sin(er), np.cos(er), np.sin(ec), np.cos(ec)], axis=-1).astype(np.float32)


def _gate_perm():
    perm = np.zeros(4 * H, dtype=np.int32)
    for hp in range(N_HP):
        for d in range(2):
            for j in range(2):
                for hh in range(2):
                    perm[hp * NG + d * 4 + j * 2 + hh] = d * 2 * H + j * H + 2 * hp + hh
    return perm


def kernel(x_prompt, x_sample, state_C, state_n, state_m, c, c_ctx, w_mod, b_mod, ln_g, ln_b, mlp_w1, mlp_w2,
           ml_w_in, ml_w_gate, ml_b_gate, ml_norm_g, ml_w_out, sc_w_in, sc_conv_w, sc_w_out,
           pl_w, pl_b, pl_scale, ft_w_out, ft_b_out):
    cond = jnp.concatenate([c_ctx[None, :], c, jnp.zeros((NCOND - 1 - DEC_BATCH, D), F32)], axis=0)
    mod = _modulation(cond, w_mod, b_mod)

    x = (x_prompt.reshape(NCTX, D), x_sample.reshape(NLAT, D), jnp.asarray(_pos_embed()))

    new_c = new_n = new_m = None
    for i in range(DEPTH):
        kind, j = i % 4, i // 4
        if kind == 0:
            perm = _gate_perm()
            wgt = ml_w_gate[j][:, perm].astype(BF).T
            qt, k, kt, vt, o, g, gt = _ml_proj(x, mod, i, ml_w_in[j], wgt, ml_b_gate[j][perm].reshape(-1, 1))
            scan_args = (qt, k, kt, vt, o, g, gt, ml_norm_g[j])
            uc, cst, nmst = _ml_scan(*scan_args)
            c0 = jnp.concatenate([jnp.swapaxes(state_C[:, j], -1, -2), state_n[:, j][..., None, :],
                                  jnp.zeros((DEC_BATCH, 2, H, DHA - DHV - 1, DQK), F32)], axis=-2)
            m0 = jnp.broadcast_to(state_m[:, j][..., None, None], (DEC_BATCH, 2, H, 1, LANES))
            (ul,) = _ml_scan(*scan_args, c0, m0)
            x = _out_proj(uc, ul, ml_w_out[j], jnp.zeros((D,), F32), x, mod, i, ln_g[i, 0], ln_b[i, 0])
            new_c = cst[:, None]
            new_n = nmst[..., 0, :DQK][:, None]
            new_m = nmst[..., 0, DQK][:, None]
        elif kind == 1:
            x = _sconv(x, mod, i, sc_w_in[j], sc_conv_w[j], sc_w_out[j],
                       ln_g[i, 0], ln_b[i, 0])
        elif kind == 2:
            x = _pool(x, mod, i, pl_w[j].astype(BF), pl_b[j], pl_scale[j], ln_g[i, 0], ln_b[i, 0])
        else:
            x = _fourier(x, mod, i, ft_w_out[j], ft_b_out[j], ln_g[i, 0], ln_b[i, 0])
        x = _mlp(x, mod, i, mlp_w1, mlp_w2, ln_g[i, 1], ln_b[i, 1],
                 split_out=(i == DEPTH - 1))

    y_prompt = x[0].reshape(BATCH, SEQ, D)
    y_sample = x[1].reshape(DEC_BATCH, DEC_SEQ, D)
    return y_prompt, y_sample, new_c, new_n, new_m
```

```python
import functools

import numpy as np
import jax
import jax.numpy as jnp
from jax import lax
from jax.experimental import pallas as pl
from jax.experimental.pallas import tpu as pltpu

D = 1024
BATCH, SEQ = 16, 256
DEC_BATCH, DEC_SEQ = 4, 1024
DEPTH = 4
GRID_W = 64
H = 8
DQK = 64
DHV = 128
QK = H * DQK
V = H * DHV
N_GROUPS = 4
GROUP_W = D // N_GROUPS
POOL_WINDOWS = (2, 4, 8, 16)
D_FF = 4 * D
ALPHA = (2.0 * DEPTH) ** 0.25
LN_EPS = 1e-5
F32 = jnp.float32
BF = jnp.bfloat16

NCTX = BATCH * SEQ
NLAT = DEC_BATCH * DEC_SEQ
NTOK = NCTX + NLAT
NCOND = 8
CHUNK = 256
SUBLANES, LANES = 8, 128
V7X_VMEM_BYTES = 64 * 1024 * 1024
VMEM_LIMIT = V7X_VMEM_BYTES - 8 * 1024 * 1024


def _cparams(n_axes):
    return pltpu.CompilerParams(dimension_semantics=("arbitrary",) * n_axes,
                                vmem_limit_bytes=VMEM_LIMIT)


def _mod_row(row0):
    return jnp.where(row0 < NCTX, 0, 1 + (row0 - NCTX) // DEC_SEQ)


def _seq_len(row0):
    return jnp.where(row0 < NCTX, SEQ, DEC_SEQ)


def _ln(z, g, b):
    mu = jnp.mean(z, axis=-1, keepdims=True)
    zc = z - mu
    var = jnp.mean(zc * zc, axis=-1, keepdims=True)
    return zc * lax.rsqrt(var + LN_EPS) * g + b


def _dot(a, b):
    return jnp.dot(a, b, preferred_element_type=F32)


def _dot_nt(a, b):
    return lax.dot_general(a, b, (((1,), (1,)), ((), ())), preferred_element_type=F32)


def _split2(x):
    hi = x.astype(BF)
    lo = (x - hi.astype(F32)).astype(BF)
    return hi, lo


def _split3(x):
    hi = x.astype(BF)
    r = x - hi.astype(F32)
    mid = r.astype(BF)
    lo = (r - mid.astype(F32)).astype(BF)
    return hi, mid, lo


def _full(shape):
    n = len(shape)
    return pl.BlockSpec(shape, lambda *_: (0,) * n)


def _x_specs(x, tm):
    if not isinstance(x, tuple):
        return [pl.BlockSpec((tm, D), lambda t: (t, 0))], [x]
    nc = NCTX // tm
    return ([pl.BlockSpec((tm, D), lambda t: (jnp.minimum(t, nc - 1), 0)),
             pl.BlockSpec((tm, D), lambda t: (jnp.maximum(t - nc, 0), 0)),
             pl.BlockSpec((tm, D), lambda t: (t % (DEC_SEQ // tm), 0))], list(x))


def _read_x(x_refs, tm):
    if len(x_refs) == 1:
        return x_refs[0][...]
    xc_ref, xl_ref, pe_ref = x_refs
    return jnp.where(pl.program_id(0) * tm < NCTX, xc_ref[...], xl_ref[...] + pe_ref[...])


def _mod_spec(layer, tm):
    return pl.BlockSpec((None, None, 6, D), lambda t: (layer, _mod_row(t * tm), 0, 0))


def _mod_kernel(c_ref, w_ref, b_ref, o_ref):
    c = c_ref[...]
    s = c * jax.nn.sigmoid(c)
    o_ref[...] = _dot(s.astype(BF), w_ref[...].astype(BF)) + b_ref[...]


def _modulation(cond, w_mod, b_mod):
    tn = 3 * D
    out = pl.pallas_call(
        _mod_kernel,
        grid=(DEPTH, 6 * D // tn),
        in_specs=[pl.BlockSpec((NCOND, D), lambda i, j: (0, 0)),
                  pl.BlockSpec((None, D, tn), lambda i, j: (i, 0, j)),
                  pl.BlockSpec((None, 1, tn), lambda i, j: (i, 0, j))],
        out_specs=pl.BlockSpec((None, NCOND, tn), lambda i, j: (i, 0, j)),
        out_shape=jax.ShapeDtypeStruct((DEPTH, NCOND, 6 * D), F32),
        compiler_params=_cparams(2),
        name="modulation",
    )(cond, w_mod, b_mod.reshape(DEPTH, 1, 6 * D))
    return out.reshape(DEPTH, NCOND, 6, D)


MLP_TM = 512
MLP_FC = 512
MLP_NF = D_FF // MLP_FC
MLP_NT = NTOK // MLP_TM


def _mlp_tile(s):
    return jnp.maximum(s - (MLP_NF - 1), 0)


def _mlp_kernel(mod_ref, w1_ref, w2_ref, g_ref, b_ref, x1_ref, x_ref, *refs):
    o_refs, (w1s_ref, w2s_ref, acc_ref, h_ref) = refs[:-4], refs[-4:]
    s = pl.program_id(0)
    m = mod_ref[...]

    def pre(x):
        return (x * (1.0 + m[4:5]) + m[3:4]).astype(BF)

    def hidden(h, w1c):
        return jnp.square(jnp.maximum(_dot(h, w1c), 0.0)).astype(BF)

    def result(x, acc):
        return _ln(ALPHA * x + m[5:6] * acc, g_ref[...], b_ref[...])

    @pl.when(s < MLP_NF)
    def _():
        @pl.when(s == 0)
        def _():
            h_ref[0:MLP_TM, :] = pre(x_ref[...])
            h_ref[MLP_TM:, :] = pre(x1_ref[...])

        w1c = w1_ref[...].astype(BF)
        w2c = w2_ref[...].astype(BF)
        part = _dot(hidden(h_ref[...], w1c), w2c)
        w1s_ref[s] = w1c
        w2s_ref[s] = w2c

        @pl.when(s == 0)
        def _():
            acc_ref[...] = part

        @pl.when(s > 0)
        def _():
            acc_ref[...] += part

        @pl.when(s == MLP_NF - 1)
        def _():
            o_refs[0][...] = result(x_ref[...], acc_ref[0:MLP_TM, :])

    @pl.when(s == MLP_NF)
    def _():
        o_refs[0][...] = result(x1_ref[...], acc_ref[MLP_TM:, :])

    @pl.when(s > MLP_NF)
    def _():
        x = x_ref[...]
        h = pre(x)
        acc = jnp.zeros((MLP_TM, D), F32)
        for f in range(MLP_NF):
            acc = acc + _dot(hidden(h, w1s_ref[f]), w2s_ref[f])
        res = result(x, acc)
        if len(o_refs) == 1:
            o_refs[0][...] = res
        else:
            is_ctx = _mlp_tile(s) * MLP_TM < NCTX

            @pl.when(is_ctx)
            def _():
                o_refs[0][...] = res

            @pl.when(jnp.logical_not(is_ctx))
            def _():
                o_refs[1][...] = res


def _mlp(x, mod, layer, w1, w2, ln_g, ln_b, split_out=False):
    tm = MLP_TM
    nc = NCTX // tm
    assert nc >= 2 and MLP_NT >= 2
    if split_out:
        out_specs = [pl.BlockSpec((tm, D), lambda s: (jnp.minimum(_mlp_tile(s), nc - 1), 0)),
                     pl.BlockSpec((tm, D), lambda s: (jnp.maximum(_mlp_tile(s) - nc, 0), 0))]
        out_shape = [jax.ShapeDtypeStruct((NCTX, D), F32), jax.ShapeDtypeStruct((NLAT, D), F32)]
    else:
        out_specs = pl.BlockSpec((tm, D), lambda s: (_mlp_tile(s), 0))
        out_shape = jax.ShapeDtypeStruct((NTOK, D), F32)
    return pl.pallas_call(
        _mlp_kernel,
        grid=(MLP_NF - 1 + MLP_NT,),
        in_specs=[pl.BlockSpec((None, None, 6, D), lambda s: (layer, _mod_row(_mlp_tile(s) * tm), 0, 0)),
                  pl.BlockSpec((None, D, MLP_FC), lambda s: (layer, 0, jnp.minimum(s, MLP_NF - 1))),
                  pl.BlockSpec((None, MLP_FC, D), lambda s: (layer, jnp.minimum(s, MLP_NF - 1), 0)),
                  _full((1, D)), _full((1, D)),
                  pl.BlockSpec((tm, D), lambda s: (1, 0)),
                  pl.BlockSpec((tm, D), lambda s: (_mlp_tile(s), 0))],
        out_specs=out_specs,
        out_shape=out_shape,
        scratch_shapes=[pltpu.VMEM((MLP_NF, D, MLP_FC), BF),
                        pltpu.VMEM((MLP_NF, MLP_FC, D), BF),
                        pltpu.VMEM((2 * tm, D), F32),
                        pltpu.VMEM((2 * tm, D), BF)],
        compiler_params=_cparams(1),
        name="mlp",
    )(mod, w1, w2, ln_g.reshape(1, D), ln_b.reshape(1, D), x, x)


OUT_TM = 1024
OUT_SUB = 256


def _out_kernel(uc_ref, ul_ref, w_f32_ref, bias_ref, mod_ref, g_ref, b_ref, *refs):
    x_refs, o_ref, w_ref = refs[:-2], refs[-2], refs[-1]

    @pl.when(pl.program_id(0) == 0)
    def _():
        w_ref[...] = w_f32_ref[...].astype(BF)

    is_ctx = pl.program_id(0) * OUT_TM < NCTX
    m = mod_ref[...]
    outs = []
    for r in range(0, OUT_TM, OUT_SUB):
        rs = slice(r, r + OUT_SUB)
        u = jnp.where(is_ctx, uc_ref[rs, :], ul_ref[rs, :])
        y = _dot(u, w_ref[...]) + bias_ref[...]
        if len(x_refs) == 1:
            x = x_refs[0][rs, :]
        else:
            x = jnp.where(is_ctx, x_refs[0][rs, :], x_refs[1][rs, :] + x_refs[2][rs, :])
        outs.append(_ln(ALPHA * x + m[2:3] * y, g_ref[...], b_ref[...]))
    for r, out in zip(range(0, OUT_TM, OUT_SUB), outs):
        o_ref[r:r + OUT_SUB, :] = out


def _out_proj(uc, ul, w, bias, x, mod, layer, ln_g, ln_b):
    tm = OUT_TM
    nc = NCTX // tm
    k = w.shape[0]
    x_specs, x_args = _x_specs(x, tm)
    return pl.pallas_call(
        _out_kernel,
        grid=(NTOK // tm,),
        in_specs=[pl.BlockSpec((tm, k), lambda t: (jnp.minimum(t, nc - 1), 0)),
                  pl.BlockSpec((tm, k), lambda t: (jnp.maximum(t - nc, 0), 0)),
                  pl.BlockSpec((k, D), lambda t: (0, 0), pipeline_mode=pl.Buffered(1)),
                  _full((1, D)),
                  _mod_spec(layer, tm),
                  _full((1, D)), _full((1, D))] + x_specs,
        out_specs=pl.BlockSpec((tm, D), lambda t: (t, 0)),
        out_shape=jax.ShapeDtypeStruct((NTOK, D), F32),
        scratch_shapes=[pltpu.VMEM((k, D), BF)],
        compiler_params=_cparams(1),
        name="out_proj",
    )(uc, ul, w, bias.reshape(1, D), mod, ln_g.reshape(1, D), ln_b.reshape(1, D), *x_args)


ML_TM = 512
N_HP = H // 2
SCAN_PAIRS = 2
NG = 8


def _log_sigmoid(g):
    return jnp.minimum(g, 0.0) - jnp.log1p(jnp.exp(-jnp.abs(g)))


def _ml_proj_kernel(mod_ref, w_in_ref, wgt_ref, bgt_ref, *refs):
    x_refs = refs[:-8]
    qt_ref, k_ref, kt_ref, vt_ref, o_ref, gt_ref = refs[-8:-2]
    wt_ref, wo_ref = refs[-2:]
    r0, r1, r2 = QK, QK + 4 * NG, 2 * QK + 4 * NG

    @pl.when(pl.program_id(0) == 0)
    def _():
        wt_ref[0:r0, :] = w_in_ref[:, 0:QK].T.astype(BF)
        wt_ref[r0:r1, :] = wgt_ref[...]
        wt_ref[r1:r2, :] = w_in_ref[:, QK:2 * QK].T.astype(BF)
        wt_ref[r2:, :] = w_in_ref[:, 2 * QK:2 * QK + V].T.astype(BF)
        wo_ref[...] = w_in_ref[:, 2 * QK + V:].astype(BF)

    m = mod_ref[...]
    h = (_read_x(x_refs, ML_TM) * (1.0 + m[1:2]) + m[0:1]).astype(BF)
    tr = _dot_nt(wt_ref[...], h)
    qt = tr[:r0].astype(BF)
    ktf = tr[r1:r2] * (DQK ** -0.5)
    kt = ktf.astype(BF)
    k = ktf.T.astype(BF)
    vt = tr[r2:].astype(BF)
    o = _dot(h, wo_ref[...])
    gt = tr[r0:r1] + bgt_ref[...]
    row = lax.broadcasted_iota(jnp.int32, gt.shape, 0)
    gt = jnp.where((row & 2) != 0, _log_sigmoid(gt), gt)
    qt_ref[...] = qt
    k_ref[...] = k
    kt_ref[...] = kt
    vt_ref[...] = vt
    o_ref[...] = o
    for hp in range(N_HP):
        gt_ref[hp] = gt[hp * NG:(hp + 1) * NG, :]


def _ml_proj(x, mod, layer, w_in, wgt, bgt):
    tm = ML_TM
    res = lambda shape: pl.BlockSpec(shape, lambda t: (0,) * len(shape), pipeline_mode=pl.Buffered(1))
    x_specs, x_args = _x_specs(x, tm)
    return pl.pallas_call(
        _ml_proj_kernel,
        grid=(NTOK // tm,),
        in_specs=[_mod_spec(layer, tm),
                  res((D, 2 * QK + 2 * V)), res((4 * NG, D)), res((4 * NG, 1))] + x_specs,
        out_specs=[pl.BlockSpec((QK, tm), lambda t: (0, t)),
                   pl.BlockSpec((tm, QK), lambda t: (t, 0)),
                   pl.BlockSpec((QK, tm), lambda t: (0, t)),
                   pl.BlockSpec((V, tm), lambda t: (0, t)),
                   pl.BlockSpec((tm, V), lambda t: (t, 0)),
                   pl.BlockSpec((N_HP, NG, tm), lambda t: (0, 0, t))],
        out_shape=[jax.ShapeDtypeStruct((QK, NTOK), BF),
                   jax.ShapeDtypeStruct((NTOK, QK), BF),
                   jax.ShapeDtypeStruct((QK, NTOK), BF),
                   jax.ShapeDtypeStruct((V, NTOK), BF),
                   jax.ShapeDtypeStruct((NTOK, V), F32),
                   jax.ShapeDtypeStruct((N_HP, NG, NTOK), F32)],
        scratch_shapes=[pltpu.VMEM((2 * QK + 4 * NG + V, D), BF),
                        pltpu.VMEM((D, V), BF)],
        compiler_params=_cparams(1),
        name="mlstm_proj",
    )(mod, w_in, wgt, bgt, *x_args)


DHA = DHV + SUBLANES


def _scan_chunk(k_c, kt_c, qt_c, vta, rr_col, rr_row, b_row, m, cta, d, need_state):
    L = k_c.shape[0]
    si = lax.broadcasted_iota(jnp.int32, (L, L), 0)
    ti = lax.broadcasted_iota(jnp.int32, (L, L), 1)
    mask = (si <= ti) if d == 0 else (si >= ti)
    rrm = jnp.where(mask, rr_col, -jnp.inf)
    g = jnp.maximum(jnp.max(rrm, axis=0, keepdims=True), m)
    st = _dot(k_c, qt_c) * jnp.exp(rrm - g)
    num = _dot(vta[:DHV], st.astype(BF))
    den = jnp.sum(st, axis=0, keepdims=True)
    if cta is not None:
        a = jnp.exp(m - g)
        qc = _dot(cta.astype(BF), qt_c)
        num = num + a * qc[:DHV]
        den = den + a * qc[DHV:DHV + 1]
    mt = b_row + g
    ht = num * (1.0 / jnp.maximum(jnp.abs(den), jnp.exp(-mt)))
    if not need_state:
        return ht, None, None
    last = L - 1 if d == 0 else 0
    m_new = mt[:, last:last + 1]
    b_last = b_row[:, last:last + 1]
    w = jnp.exp(b_last + rr_row - m_new)
    kw = (kt_c.astype(F32) * w).astype(BF)
    if cta is None:
        return ht, (_dot_nt(kw, vta[:DHV]), _dot_nt(vta[DHV:], kw)), m_new
    c_new = jnp.exp(b_last + m - m_new) * cta + _dot_nt(vta, kw)
    return ht, c_new, m_new


def _scan_kernel(*refs, is_ctx):
    if is_ctx:
        qt_ref, k_ref, kt_ref, vt_ref, o_ref, gt_ref, ng_ref, u_ref, cst_ref, nmst_ref = refs
    else:
        qt_ref, k_ref, kt_ref, vt_ref, o_ref, gt_ref, ng_ref, c0_ref, m0_ref, u_ref = refs
    L = CHUNK
    nch = DEC_SEQ // L
    ri = lax.broadcasted_iota(jnp.int32, (L, L), 0)
    ci = lax.broadcasted_iota(jnp.int32, (L, L), 1)
    lower = jnp.where(ci <= ri, 1.0, 0.0).astype(BF)
    upper = jnp.where(ci >= ri, 1.0, 0.0).astype(BF)
    sub = lax.broadcasted_iota(jnp.int32, (DHA - DHV, L), 0)
    ones_rows = jnp.where(sub == 0, 1.0, 0.0).astype(BF)

    rr_cols, rr_rows, b_rows = [], [], []
    nr = nch * NG
    pad = jnp.zeros((LANES - 2 * nr, L), F32)
    for pp in range(SCAN_PAIRS):
        gt_all = jnp.concatenate([gt_ref[pp, :, c * L:(c + 1) * L] for c in range(nch)], axis=0)
        tparts = _split3(gt_all)
        b_row_all = (sum(_dot(p, upper) for p in tparts), sum(_dot(p, lower) for p in tparts))
        rr_row_all = [gt_all - pltpu.roll(b, nr - 2, 0) for b in b_row_all]
        rr_col_all = jnp.concatenate(rr_row_all + [pad], axis=0).T
        rr_cols.append([[rr_col_all[:, d * nr + c * NG:d * nr + (c + 1) * NG] for d in range(2)]
                        for c in range(nch)])
        rr_rows.append([[r[c * NG:(c + 1) * NG, :] for r in rr_row_all] for c in range(nch)])
        b_rows.append([[b[c * NG:(c + 1) * NG, :] for b in b_row_all] for c in range(nch)])

    states, us = [], []
    for hl in range(2 * SCAN_PAIRS):
        pp, hh = divmod(hl, 2)
        hsum = [None] * nch
        for d in range(2):
            ig_i, lf_i = d * 4 + hh, d * 4 + 2 + hh
            if is_ctx:
                m, cta = jnp.zeros((1, 1), F32), None
            else:
                m = m0_ref[d, hl][:, 0:1]
                cta = c0_ref[d, hl]
            order = range(nch) if d == 0 else range(nch - 1, -1, -1)
            for n, c in enumerate(order):
                sl = slice(c * L, (c + 1) * L)
                hs = slice(hl * DQK, (hl + 1) * DQK)
                vta = jnp.concatenate([vt_ref[hl * DHV:(hl + 1) * DHV, sl], ones_rows], axis=0)
                need_state = is_ctx or n < nch - 1
                hcur, c_new, m_new = _scan_chunk(
                    k_ref[sl, hs], kt_ref[hs, sl], qt_ref[hs, sl], vta,
                    rr_cols[pp][c][d][:, ig_i:ig_i + 1], rr_rows[pp][c][d][ig_i:ig_i + 1, :],
                    b_rows[pp][c][d][lf_i:lf_i + 1, :], m, cta, d, need_state)
                hsum[c] = hcur if d == 0 else hsum[c] + hcur
                if is_ctx:
                    states.append(((c, d, hl), c_new, m_new))
                else:
                    m, cta = m_new, c_new

        for c in range(nch):
            sl = slice(c * L, (c + 1) * L)
            mu = jnp.mean(hsum[c], axis=0, keepdims=True)
            hc = hsum[c] - mu
            var = jnp.mean(hc * hc, axis=0, keepdims=True)
            hn = hc * lax.rsqrt(var + LN_EPS) * ng_ref[hl * DHV:(hl + 1) * DHV, :]
            og = jax.nn.sigmoid(o_ref[sl, hl * DHV:(hl + 1) * DHV])
            us.append(((sl, slice(hl * DHV, (hl + 1) * DHV)), (hn.T * og).astype(BF)))

    for idx, u in us:
        u_ref[idx] = u
    for idx, (c_fin, n_fin), m_fin in states:
        cst_ref[idx] = c_fin
        nmst_ref[idx] = jnp.concatenate([n_fin, jnp.broadcast_to(m_fin, (SUBLANES, LANES - DQK))], axis=1)


def _ml_scan(qt, k, kt, vt, o, gt, norm_g, c0aug=None, m0b=None):
    is_ctx = c0aug is None
    tt = DEC_SEQ
    toff = 0 if is_ctx else NCTX // tt
    nt = (NCTX if is_ctx else NLAT) // tt
    nh = 2 * SCAN_PAIRS
    in_specs = [pl.BlockSpec((nh * DQK, tt), lambda t, p: (p, toff + t)),
                pl.BlockSpec((tt, nh * DQK), lambda t, p: (toff + t, p)),
                pl.BlockSpec((nh * DQK, tt), lambda t, p: (p, toff + t)),
                pl.BlockSpec((nh * DHV, tt), lambda t, p: (p, toff + t)),
                pl.BlockSpec((tt, nh * DHV), lambda t, p: (toff + t, p)),
                pl.BlockSpec((SCAN_PAIRS, NG, tt), lambda t, p: (p, 0, toff + t)),
                pl.BlockSpec((nh * DHV, 1), lambda t, p: (p, 0))]
    args = [qt, k, kt, vt, o, gt, norm_g.reshape(V, 1)]
    u_spec = pl.BlockSpec((tt, nh * DHV), lambda t, p: (t, p))
    u_shape = jax.ShapeDtypeStruct((nt * tt, V), BF)
    if is_ctx:
        nseq = tt // SEQ
        out_specs = [u_spec,
                     pl.BlockSpec((nseq, 2, nh, DQK, DHV), lambda t, p: (t, 0, p, 0, 0)),
                     pl.BlockSpec((nseq, 2, nh, SUBLANES, LANES), lambda t, p: (t, 0, p, 0, 0))]
        out_shape = [u_shape,
                     jax.ShapeDtypeStruct((BATCH, 2, H, DQK, DHV), F32),
                     jax.ShapeDtypeStruct((BATCH, 2, H, SUBLANES, LANES), F32)]
    else:
        in_specs += [pl.BlockSpec((None, 2, nh, DHA, DQK), lambda t, p: (t, 0, p, 0, 0)),
                     pl.BlockSpec((None, 2, nh, 1, LANES), lambda t, p: (t, 0, p, 0, 0))]
        args += [c0aug, m0b]
        out_specs = [u_spec]
        out_shape = [u_shape]
    return pl.pallas_call(
        functools.partial(_scan_kernel, is_ctx=is_ctx),
        grid=(nt, N_HP // SCAN_PAIRS),
        in_specs=in_specs,
        out_specs=out_specs,
        out_shape=out_shape,
        compiler_params=_cparams(2),
        name="mlstm_scan_ctx" if is_ctx else "mlstm_scan_lat",
    )(*args)


SC_TM = 512
SC_HALO = SUBLANES


def _sconv_kernel(x_ref, prev_ref, next_ref, mod_ref, w_in_f32_ref, cw_ref, w_out_f32_ref, g_ref, b_ref, o_ref,
                  e_ref, w_in_ref, w_out_ref):
    tm = SC_TM

    @pl.when(pl.program_id(0) == 0)
    def _():
        w_in_ref[...] = w_in_f32_ref[...].astype(BF)
        w_out_ref[...] = w_out_f32_ref[...].astype(BF)

    row0 = pl.program_id(0) * tm
    slen = _seq_len(row0)
    pos = (row0 + lax.broadcasted_iota(jnp.int32, (tm, 1), 0)) & (slen - 1)
    m = mod_ref[...]
    x = x_ref[...]
    xe = jnp.concatenate([prev_ref[...], x, next_ref[...]], axis=0)
    he = (xe * (1.0 + m[1:2]) + m[0:1]).astype(BF)
    cu = _dot(he, w_in_ref[:, D:2 * D]) * _dot(he, w_in_ref[:, 2 * D:3 * D])
    bg = _dot(he[SC_HALO:SC_HALO + tm], w_in_ref[:, 0:D])
    e_ref[...] = cu
    prev = jnp.where(pos == 0, 0.0, e_ref[SC_HALO - 1:SC_HALO - 1 + tm, :])
    nxt = jnp.where(pos == slen - 1, 0.0, e_ref[SC_HALO + 1:SC_HALO + 1 + tm, :])
    conv = cw_ref[0:1, :] * prev + cw_ref[1:2, :] * e_ref[SC_HALO:SC_HALO + tm, :] + cw_ref[2:3, :] * nxt
    y = _dot((bg * conv).astype(BF), w_out_ref[...])
    o_ref[...] = _ln(ALPHA * x + m[2:3] * y, g_ref[...], b_ref[...])


def _sconv(x, mod, layer, w_in, conv_w, w_out, ln_g, ln_b):
    tm = SC_TM
    hb = tm // SC_HALO
    nhb = NTOK // SC_HALO
    return pl.pallas_call(
        _sconv_kernel,
        grid=(NTOK // tm,),
        in_specs=[pl.BlockSpec((tm, D), lambda t: (t, 0)),
                  pl.BlockSpec((SC_HALO, D), lambda t: (jnp.maximum(t * hb - 1, 0), 0)),
                  pl.BlockSpec((SC_HALO, D), lambda t: (jnp.minimum((t + 1) * hb, nhb - 1), 0)),
                  _mod_spec(layer, tm),
                  pl.BlockSpec((D, 3 * D), lambda t: (0, 0), pipeline_mode=pl.Buffered(1)),
                  _full((3, D)),
                  pl.BlockSpec((D, D), lambda t: (0, 0), pipeline_mode=pl.Buffered(1)),
                  _full((1, D)), _full((1, D))],
        out_specs=pl.BlockSpec((tm, D), lambda t: (t, 0)),
        out_shape=jax.ShapeDtypeStruct((NTOK, D), F32),
        scratch_shapes=[pltpu.VMEM((tm + 2 * SC_HALO, D), F32),
                        pltpu.VMEM((D, 3 * D), BF),
                        pltpu.VMEM((D, D), BF)],
        compiler_params=_cparams(1),
        name="sconv",
    )(x, x, x, mod, w_in, conv_w, w_out, ln_g.reshape(1, D), ln_b.reshape(1, D))


PL_SEG = 256
PL_NSEG = 4
PL_TM = PL_SEG * PL_NSEG
PL_HALO = 16
PL_STRIDE = PL_SEG + 2 * PL_HALO


def _pool_kernel(x_ref, prev_ref, next_ref, mod_ref, w_ref, pb_ref, ps_ref, g_ref, b_ref, o_ref,
                 e_ref, s2_ref, s4_ref, s8_ref):
    tm, seg, hl, gw = PL_TM, PL_SEG, PL_HALO, GROUP_W
    row0 = pl.program_id(0) * tm
    slen = _seq_len(row0)
    m = mod_ref[...]
    x = x_ref[...]
    h = x * (1.0 + m[1:2]) + m[0:1]
    hp = prev_ref[...] * (1.0 + m[1:2]) + m[0:1]
    hn = next_ref[...] * (1.0 + m[1:2]) + m[0:1]
    n0 = PL_NSEG * PL_STRIDE
    for i in range(PL_NSEG):
        base = i * PL_STRIDE
        pos0 = (row0 + i * seg) & (slen - 1)
        before = hp if i == 0 else h[i * seg - hl:i * seg]
        after = hn if i == PL_NSEG - 1 else h[(i + 1) * seg:(i + 1) * seg + hl]
        e_ref[base:base + hl, :] = jnp.where(pos0 != 0, before, 0.0)
        e_ref[base + hl:base + hl + seg, :] = h[i * seg:(i + 1) * seg]
        e_ref[base + hl + seg:base + PL_STRIDE, :] = jnp.where(pos0 + seg != slen, after, 0.0)
    e_ref[n0:, :] = jnp.zeros((SUBLANES, D), F32)
    n2, n4, n8 = n0, n0 - SUBLANES, n0 - 2 * SUBLANES
    s2_ref[...] = e_ref[0:n2, :] + e_ref[1:n2 + 1, :]
    s4_ref[...] = s2_ref[0:n4, gw:] + s2_ref[2:n4 + 2, gw:]
    s8_ref[...] = s4_ref[0:n8, gw:] + s4_ref[4:n8 + 4, gw:]

    def rows(ref, off, lanes):
        return jnp.concatenate([ref[i * PL_STRIDE + off:i * PL_STRIDE + off + seg, lanes]
                                for i in range(PL_NSEG)], axis=0)

    tots = [rows(s2_ref, hl - 1, slice(0, gw)),
            rows(s4_ref, hl - 2, slice(0, gw)),
            rows(s8_ref, hl - 4, slice(0, gw)),
            rows(s8_ref, hl - 8, slice(gw, 2 * gw)) + rows(s8_ref, hl, slice(gw, 2 * gw))]
    pos = (row0 + lax.broadcasted_iota(jnp.int32, (tm, 1), 0)) & (slen - 1)
    ys = []
    for gi, win in enumerate(POOL_WINDOWS):
        cs = slice(gi * gw, (gi + 1) * gw)
        back = win // 2
        cnt = jnp.minimum(pos + (win - back), slen) - jnp.maximum(pos - back, 0)
        p = tots[gi] / cnt.astype(F32) - h[:, cs]
        ys.append(_dot(p.astype(BF), w_ref[gi]))
    y = (jnp.concatenate(ys, axis=1) + pb_ref[...]) * ps_ref[...]
    o_ref[...] = _ln(ALPHA * x + m[2:3] * y, g_ref[...], b_ref[...])


def _pool(x, mod, layer, w, pb, ps, ln_g, ln_b):
    assert POOL_WINDOWS == (2, 4, 8, 16) and PL_HALO >= POOL_WINDOWS[-1] // 2
    assert SEQ % PL_SEG == 0 and DEC_SEQ % PL_SEG == 0
    tm = PL_TM
    hb = tm // PL_HALO
    nhb = NTOK // PL_HALO
    n0 = PL_NSEG * PL_STRIDE
    return pl.pallas_call(
        _pool_kernel,
        grid=(NTOK // tm,),
        in_specs=[pl.BlockSpec((tm, D), lambda t: (t, 0)),
                  pl.BlockSpec((PL_HALO, D), lambda t: (jnp.maximum(t * hb - 1, 0), 0)),
                  pl.BlockSpec((PL_HALO, D), lambda t: (jnp.minimum((t + 1) * hb, nhb - 1), 0)),
                  _mod_spec(layer, tm),
                  _full((N_GROUPS, GROUP_W, GROUP_W)),
                  _full((1, D)), _full((1, D)), _full((1, D)), _full((1, D))],
        out_specs=pl.BlockSpec((tm, D), lambda t: (t, 0)),
        out_shape=jax.ShapeDtypeStruct((NTOK, D), F32),
        scratch_shapes=[pltpu.VMEM((n0 + SUBLANES, D), F32),
                        pltpu.VMEM((n0, D), F32),
                        pltpu.VMEM((n0 - SUBLANES, D - GROUP_W), F32),
                        pltpu.VMEM((n0 - 2 * SUBLANES, D - 2 * GROUP_W), F32)],
        compiler_params=_cparams(1),
        name="pool",
    )(x, x, x, mod, w, pb.reshape(1, D), ps.reshape(1, D), ln_g.reshape(1, D), ln_b.reshape(1, D))


FT_TM = 1024
FT_TK = 256
FT_FOLD_MIN_SEQ = 512
FT_PAD = SUBLANES


def _dft_mats(n, scale):
    k = np.arange(n, dtype=np.int64)
    ang = 2.0 * np.pi * ((k[:, None] * k[None, :]) % n).astype(np.float64) / n
    return np.stack([np.cos(ang) * scale, np.sin(ang) * scale]).astype(np.float32)


def _dot_x3(a_parts, b_parts):
    (ah, al), (bh, bl) = a_parts, b_parts
    return _dot(ah, bh) + (_dot(ah, bl) + _dot(al, bh))


def _rev_rows(x):
    r = lax.broadcasted_iota(jnp.int32, (SUBLANES, x.shape[1]), 0)
    out = []
    for i in range(x.shape[0] // SUBLANES - 1, -1, -1):
        p = x[i * SUBLANES:(i + 1) * SUBLANES]
        for sh in (1, 2, 4):
            p = jnp.where((r & sh) == 0, pltpu.roll(p, SUBLANES - sh, 0), pltpu.roll(p, sh, 0))
        out.append(p)
    return jnp.concatenate(out, axis=0)


def _fourier_tile(x_ref, m, cs_ref, ccp, scp, w_ref, bias_ref, lg_ref, lb_ref, o_ref, s, tk):
    nsq = FT_TM // s
    ri = lax.broadcasted_iota(jnp.int32, (tk, tk), 0)
    ci = lax.broadcasted_iota(jnp.int32, (tk, tk), 1)
    anti = jnp.where(ri + ci == tk - 1, 1.0, 0.0).astype(BF)
    blocks = []
    half = s // 2
    row = lax.broadcasted_iota(jnp.int32, (half, 1), 0)
    for sq in range(nsq):
        base = sq * s
        h = x_ref[base:base + s, :] * (1.0 + m[1:2]) + m[0:1]
        if s >= FT_FOLD_MIN_SEQ:
            hr = jnp.where(row == 0, 0.0, pltpu.roll(_rev_rows(h[half:]), 1, 0))
            hp = _split2(h[:half] + hr)
            hm = _split2(h[:half] - hr)
            h_mid = h[half:half + 1]
        else:
            hparts = _split2(h)
        for j in range(s // (2 * tk)):
            if s >= FT_FOLD_MIN_SEQ:
                ac = _dot_x3(_split2(cs_ref[0, j, :, 0:half]), hp) + cs_ref[0, j, :, half:half + 1] * h_mid
                as_ = _dot_x3(_split2(cs_ref[1, j, :, 0:half]), hm)
            else:
                ac = _dot_x3(_split2(cs_ref[0, j]), hparts)
                as_ = _dot_x3(_split2(cs_ref[1, j]), hparts)
            direct, mirror = [], []
            for gi in range(N_GROUPS):
                cs = slice(gi * GROUP_W, (gi + 1) * GROUP_W)
                p = _dot_x3(_split2(ac[:, cs]), ccp)
                q = _dot_x3(_split2(as_[:, cs]), scp)
                direct.append((p - q)[0:tk])
                mirror.append((p + q)[1:tk + 1])
            blocks.append((base + j * tk, jnp.concatenate(direct, axis=1).astype(BF)))
            mir = jnp.concatenate(mirror, axis=1).astype(BF)
            blocks.append((base + s - (j + 1) * tk, _dot(anti, mir).astype(BF)))
    outs = []
    for r0, blk in blocks:
        y = _dot(blk, w_ref[...]) + bias_ref[...]
        outs.append((r0, _ln(ALPHA * x_ref[r0:r0 + tk, :] + m[2:3] * y, lg_ref[...], lb_ref[...])))
    for r0, out in outs:
        o_ref[r0:r0 + tk, :] = out


def _fourier_kernel(x_ref, mod_ref, csc_ref, csl_ref, cc_ref, w_f32_ref, bias_ref, lg_ref, lb_ref, o_ref, w_ref):
    @pl.when(pl.program_id(0) == 0)
    def _():
        w_ref[...] = w_f32_ref[...].astype(BF)

    m = mod_ref[...]
    ccp = _split2(cc_ref[0])
    scp = _split2(cc_ref[1])
    is_ctx = pl.program_id(0) * FT_TM < NCTX

    @pl.when(is_ctx)
    def _():
        _fourier_tile(x_ref, m, csc_ref, ccp, scp, w_ref, bias_ref, lg_ref, lb_ref, o_ref, SEQ, _ft_tk(SEQ))

    @pl.when(jnp.logical_not(is_ctx))
    def _():
        _fourier_tile(x_ref, m, csl_ref, ccp, scp, w_ref, bias_ref, lg_ref, lb_ref, o_ref, DEC_SEQ, _ft_tk(DEC_SEQ))


def _ft_tk(s):
    return min(FT_TK, s // 2)


def _half_dft_mats(s, tk):
    full = _dft_mats(s, s ** -0.5)
    rows = (np.arange(s // (2 * tk))[:, None] * tk + np.arange(tk + FT_PAD)[None, :]) % s
    return full[:, rows, :]


def _fourier(x, mod, layer, w, bias, ln_g, ln_b):
    tm = FT_TM
    assert tm % SEQ == 0 and tm % DEC_SEQ == 0 and NCTX % tm == 0
    csc = jnp.asarray(_half_dft_mats(SEQ, _ft_tk(SEQ)))
    csl = jnp.asarray(_half_dft_mats(DEC_SEQ, _ft_tk(DEC_SEQ)))
    cc = jnp.asarray(_dft_mats(GROUP_W, GROUP_W ** -0.5))
    const = lambda shape: pl.BlockSpec(shape, lambda b: (0,) * len(shape), pipeline_mode=pl.Buffered(1))
    return pl.pallas_call(
        _fourier_kernel,
        grid=(NTOK // tm,),
        in_specs=[pl.BlockSpec((tm, D), lambda b: (b, 0)),
                  _mod_spec(layer, tm),
                  const(csc.shape), const(csl.shape), const(cc.shape), const((D, D)),
                  _full((1, D)), _full((1, D)), _full((1, D))],
        out_specs=pl.BlockSpec((tm, D), lambda b: (b, 0)),
        out_shape=jax.ShapeDtypeStruct((NTOK, D), F32),
        scratch_shapes=[pltpu.VMEM((D, D), BF)],
        compiler_params=_cparams(1),
        name="fourier",
    )(x, mod, csc, csl, cc, w, bias.reshape(1, D), ln_g.reshape(1, D), ln_b.reshape(1, D))


def _pos_embed():
    rows = DEC_SEQ // GRID_W
    quarter = D // 4
    omega = 1.0 / (10000.0 ** (np.arange(quarter, dtype=np.float64) / quarter))
    rr, cc = np.meshgrid(np.arange(rows, dtype=np.float64), np.arange(GRID_W, dtype=np.float64), indexing="ij")
    er = rr.reshape(-1, 1) * omega
    ec = cc.reshape(-1, 1) * omega
    return np.concatenate([np.sin(er), np.cos(er), np.sin(ec), np.cos(ec)], axis=-1).astype(np.float32)


def _gate_perm():
    perm = np.zeros(4 * H, dtype=np.int32)
    for hp in range(N_HP):
        for d in range(2):
            for j in range(2):
                for hh in range(2):
                    perm[hp * NG + d * 4 + j * 2 + hh] = d * 2 * H + j * H + 2 * hp + hh
    return perm


def kernel(x_prompt, x_sample, state_C, state_n, state_m, c, c_ctx, w_mod, b_mod, ln_g, ln_b, mlp_w1, mlp_w2,
           ml_w_in, ml_w_gate, ml_b_gate, ml_norm_g, ml_w_out, sc_w_in, sc_conv_w, sc_w_out,
           pl_w, pl_b, pl_scale, ft_w_out, ft_b_out):
    cond = jnp.concatenate([c_ctx[None, :], c, jnp.zeros((NCOND - 1 - DEC_BATCH, D), F32)], axis=0)
    mod = _modulation(cond, w_mod, b_mod)

    x = (x_prompt.reshape(NCTX, D), x_sample.reshape(NLAT, D), jnp.asarray(_pos_embed()))

    new_c = new_n = new_m = None
    for i in range(DEPTH):
        kind, j = i % 4, i // 4
        if kind == 0:
            perm = _gate_perm()
            wgt = ml_w_gate[j][:, perm].astype(BF).T
            qt, k, kt, vt, o, gt = _ml_proj(x, mod, i, ml_w_in[j], wgt, ml_b_gate[j][perm].reshape(-1, 1))
            scan_args = (qt, k, kt, vt, o, gt, ml_norm_g[j])
            uc, cst, nmst = _ml_scan(*scan_args)
            c0 = jnp.concatenate([jnp.swapaxes(state_C[:, j], -1, -2), state_n[:, j][..., None, :],
                                  jnp.zeros((DEC_BATCH, 2, H, DHA - DHV - 1, DQK), F32)], axis=-2)
            m0 = jnp.broadcast_to(state_m[:, j][..., None, None], (DEC_BATCH, 2, H, 1, LANES))
            (ul,) = _ml_scan(*scan_args, c0, m0)
            x = _out_proj(uc, ul, ml_w_out[j], jnp.zeros((D,), F32), x, mod, i, ln_g[i, 0], ln_b[i, 0])
            new_c = cst[:, None]
            new_n = nmst[..., 0, :DQK][:, None]
            new_m = nmst[..., 0, DQK][:, None]
        elif kind == 1:
            x = _sconv(x, mod, i, sc_w_in[j], sc_conv_w[j], sc_w_out[j],
                       ln_g[i, 0], ln_b[i, 0])
        elif kind == 2:
            x = _pool(x, mod, i, pl_w[j].astype(BF), pl_b[j], pl_scale[j], ln_g[i, 0], ln_b[i, 0])
        else:
            x = _fourier(x, mod, i, ft_w_out[j], ft_b_out[j], ln_g[i, 0], ln_b[i, 0])
        x = _mlp(x, mod, i, mlp_w1, mlp_w2, ln_g[i, 1], ln_b[i, 1],
                 split_out=(i == DEPTH - 1))

    y_prompt = x[0].reshape(BATCH, SEQ, D)
    y_sample = x[1].reshape(DEC_BATCH, DEC_SEQ, D)
    return y_prompt, y_sample, new_c, new_n, new_m
```

```python
import functools

import numpy as np
import jax
import jax.numpy as jnp
from jax import lax
from jax.experimental import pallas as pl
from jax.experimental.pallas import tpu as pltpu

D = 1024
BATCH, SEQ = 16, 256
DEC_BATCH, DEC_SEQ = 4, 1024
DEPTH = 4
GRID_W = 64
H = 8
DQK = 64
DHV = 128
QK = H * DQK
V = H * DHV
N_GROUPS = 4
GROUP_W = D // N_GROUPS
POOL_WINDOWS = (2, 4, 8, 16)
D_FF = 4 * D
ALPHA = (2.0 * DEPTH) ** 0.25
LN_EPS = 1e-5
F32 = jnp.float32
BF = jnp.bfloat16

NCTX = BATCH * SEQ
NLAT = DEC_BATCH * DEC_SEQ
NTOK = NCTX + NLAT
NCOND = 8
CHUNK = 256
SUBLANES, LANES = 8, 128
V7X_VMEM_BYTES = 64 * 1024 * 1024
VMEM_LIMIT = V7X_VMEM_BYTES - 8 * 1024 * 1024


def _cparams(n_axes):
    return pltpu.CompilerParams(dimension_semantics=("arbitrary",) * n_axes,
                                vmem_limit_bytes=VMEM_LIMIT)


def _mod_row(row0):
    return jnp.where(row0 < NCTX, 0, 1 + (row0 - NCTX) // DEC_SEQ)


def _seq_len(row0):
    return jnp.where(row0 < NCTX, SEQ, DEC_SEQ)


def _ln(z, g, b):
    mu = jnp.mean(z, axis=-1, keepdims=True)
    zc = z - mu
    var = jnp.mean(zc * zc, axis=-1, keepdims=True)
    return zc * lax.rsqrt(var + LN_EPS) * g + b


def _dot(a, b):
    return jnp.dot(a, b, preferred_element_type=F32)


def _dot_nt(a, b):
    return lax.dot_general(a, b, (((1,), (1,)), ((), ())), preferred_element_type=F32)


def _split2(x):
    hi = x.astype(BF)
    lo = (x - hi.astype(F32)).astype(BF)
    return hi, lo


def _split3(x):
    hi = x.astype(BF)
    r = x - hi.astype(F32)
    mid = r.astype(BF)
    lo = (r - mid.astype(F32)).astype(BF)
    return hi, mid, lo


def _full(shape):
    n = len(shape)
    return pl.BlockSpec(shape, lambda *_: (0,) * n)


def _x_specs(x, tm):
    if not isinstance(x, tuple):
        return [pl.BlockSpec((tm, D), lambda t: (t, 0))], [x]
    nc = NCTX // tm
    return ([pl.BlockSpec((tm, D), lambda t: (jnp.minimum(t, nc - 1), 0)),
             pl.BlockSpec((tm, D), lambda t: (jnp.maximum(t - nc, 0), 0)),
             pl.BlockSpec((tm, D), lambda t: (t % (DEC_SEQ // tm), 0))], list(x))


def _read_x(x_refs, tm):
    if len(x_refs) == 1:
        return x_refs[0][...]
    xc_ref, xl_ref, pe_ref = x_refs
    return jnp.where(pl.program_id(0) * tm < NCTX, xc_ref[...], xl_ref[...] + pe_ref[...])


def _mod_spec(layer, tm):
    return pl.BlockSpec((None, None, 6, D), lambda t: (layer, _mod_row(t * tm), 0, 0))


def _mod_kernel(c_ref, w_ref, b_ref, o_ref):
    c = c_ref[...]
    s = c * jax.nn.sigmoid(c)
    o_ref[...] = _dot(s.astype(BF), w_ref[...].astype(BF)) + b_ref[...]


def _modulation(cond, w_mod, b_mod):
    tn = 3 * D
    out = pl.pallas_call(
        _mod_kernel,
        grid=(DEPTH, 6 * D // tn),
        in_specs=[pl.BlockSpec((NCOND, D), lambda i, j: (0, 0)),
                  pl.BlockSpec((None, D, tn), lambda i, j: (i, 0, j)),
                  pl.BlockSpec((None, 1, tn), lambda i, j: (i, 0, j))],
        out_specs=pl.BlockSpec((None, NCOND, tn), lambda i, j: (i, 0, j)),
        out_shape=jax.ShapeDtypeStruct((DEPTH, NCOND, 6 * D), F32),
        compiler_params=_cparams(2),
        name="modulation",
    )(cond, w_mod, b_mod.reshape(DEPTH, 1, 6 * D))
    return out.reshape(DEPTH, NCOND, 6, D)


MLP_TM = 512
MLP_FC = 512
MLP_NF = D_FF // MLP_FC
MLP_NT = NTOK // MLP_TM


def _mlp_tile(s):
    return jnp.maximum(s - (MLP_NF - 1), 0)


def _mlp_kernel(mod_ref, w1_ref, w2_ref, g_ref, b_ref, x1_ref, x_ref, *refs):
    o_refs, (w1s_ref, w2s_ref, acc_ref, h_ref) = refs[:-4], refs[-4:]
    s = pl.program_id(0)
    m = mod_ref[...]

    def pre(x):
        return (x * (1.0 + m[4:5]) + m[3:4]).astype(BF)

    def hidden(h, w1c):
        return jnp.square(jnp.maximum(_dot(h, w1c), 0.0)).astype(BF)

    def result(x, acc):
        return _ln(ALPHA * x + m[5:6] * acc, g_ref[...], b_ref[...])

    @pl.when(s < MLP_NF)
    def _():
        @pl.when(s == 0)
        def _():
            h_ref[0:MLP_TM, :] = pre(x_ref[...])
            h_ref[MLP_TM:, :] = pre(x1_ref[...])

        w1c = w1_ref[...].astype(BF)
        w2c = w2_ref[...].astype(BF)
        part = _dot(hidden(h_ref[...], w1c), w2c)
        w1s_ref[s] = w1c
        w2s_ref[s] = w2c

        @pl.when(s == 0)
        def _():
            acc_ref[...] = part

        @pl.when(s > 0)
        def _():
            acc_ref[...] += part

        @pl.when(s == MLP_NF - 1)
        def _():
            o_refs[0][...] = result(x_ref[...], acc_ref[0:MLP_TM, :])

    @pl.when(s == MLP_NF)
    def _():
        o_refs[0][...] = result(x1_ref[...], acc_ref[MLP_TM:, :])

    @pl.when(s > MLP_NF)
    def _():
        x = x_ref[...]
        h = pre(x)
        acc = jnp.zeros((MLP_TM, D), F32)
        for f in range(MLP_NF):
            acc = acc + _dot(hidden(h, w1s_ref[f]), w2s_ref[f])
        res = result(x, acc)
        if len(o_refs) == 1:
            o_refs[0][...] = res
        else:
            is_ctx = _mlp_tile(s) * MLP_TM < NCTX

            @pl.when(is_ctx)
            def _():
                o_refs[0][...] = res

            @pl.when(jnp.logical_not(is_ctx))
            def _():
                o_refs[1][...] = res


def _mlp(x, mod, layer, w1, w2, ln_g, ln_b, split_out=False):
    tm = MLP_TM
    nc = NCTX // tm
    assert nc >= 2 and MLP_NT >= 2
    if split_out:
        out_specs = [pl.BlockSpec((tm, D), lambda s: (jnp.minimum(_mlp_tile(s), nc - 1), 0)),
                     pl.BlockSpec((tm, D), lambda s: (jnp.maximum(_mlp_tile(s) - nc, 0), 0))]
        out_shape = [jax.ShapeDtypeStruct((NCTX, D), F32), jax.ShapeDtypeStruct((NLAT, D), F32)]
    else:
        out_specs = pl.BlockSpec((tm, D), lambda s: (_mlp_tile(s), 0))
        out_shape = jax.ShapeDtypeStruct((NTOK, D), F32)
    return pl.pallas_call(
        _mlp_kernel,
        grid=(MLP_NF - 1 + MLP_NT,),
        in_specs=[pl.BlockSpec((None, None, 6, D), lambda s: (layer, _mod_row(_mlp_tile(s) * tm), 0, 0)),
                  pl.BlockSpec((None, D, MLP_FC), lambda s: (layer, 0, jnp.minimum(s, MLP_NF - 1))),
                  pl.BlockSpec((None, MLP_FC, D), lambda s: (layer, jnp.minimum(s, MLP_NF - 1), 0)),
                  _full((1, D)), _full((1, D)),
                  pl.BlockSpec((tm, D), lambda s: (1, 0)),
                  pl.BlockSpec((tm, D), lambda s: (_mlp_tile(s), 0))],
        out_specs=out_specs,
        out_shape=out_shape,
        scratch_shapes=[pltpu.VMEM((MLP_NF, D, MLP_FC), BF),
                        pltpu.VMEM((MLP_NF, MLP_FC, D), BF),
                        pltpu.VMEM((2 * tm, D), F32),
                        pltpu.VMEM((2 * tm, D), BF)],
        compiler_params=_cparams(1),
        name="mlp",
    )(mod, w1, w2, ln_g.reshape(1, D), ln_b.reshape(1, D), x, x)


OUT_TM = 1024
OUT_SUB = 256


def _out_kernel(uc_ref, ul_ref, w_f32_ref, bias_ref, mod_ref, g_ref, b_ref, *refs):
    x_refs, o_ref, w_ref = refs[:-2], refs[-2], refs[-1]

    @pl.when(pl.program_id(0) == 0)
    def _():
        w_ref[...] = w_f32_ref[...].astype(BF)

    is_ctx = pl.program_id(0) * OUT_TM < NCTX
    m = mod_ref[...]
    outs = []
    for r in range(0, OUT_TM, OUT_SUB):
        rs = slice(r, r + OUT_SUB)
        u = jnp.where(is_ctx, uc_ref[rs, :], ul_ref[rs, :])
        y = _dot(u, w_ref[...]) + bias_ref[...]
        if len(x_refs) == 1:
            x = x_refs[0][rs, :]
        else:
            x = jnp.where(is_ctx, x_refs[0][rs, :], x_refs[1][rs, :] + x_refs[2][rs, :])
        outs.append(_ln(ALPHA * x + m[2:3] * y, g_ref[...], b_ref[...]))
    for r, out in zip(range(0, OUT_TM, OUT_SUB), outs):
        o_ref[r:r + OUT_SUB, :] = out


def _out_proj(uc, ul, w, bias, x, mod, layer, ln_g, ln_b):
    tm = OUT_TM
    nc = NCTX // tm
    k = w.shape[0]
    x_specs, x_args = _x_specs(x, tm)
    return pl.pallas_call(
        _out_kernel,
        grid=(NTOK // tm,),
        in_specs=[pl.BlockSpec((tm, k), lambda t: (jnp.minimum(t, nc - 1), 0)),
                  pl.BlockSpec((tm, k), lambda t: (jnp.maximum(t - nc, 0), 0)),
                  pl.BlockSpec((k, D), lambda t: (0, 0), pipeline_mode=pl.Buffered(1)),
                  _full((1, D)),
                  _mod_spec(layer, tm),
                  _full((1, D)), _full((1, D))] + x_specs,
        out_specs=pl.BlockSpec((tm, D), lambda t: (t, 0)),
        out_shape=jax.ShapeDtypeStruct((NTOK, D), F32),
        scratch_shapes=[pltpu.VMEM((k, D), BF)],
        compiler_params=_cparams(1),
        name="out_proj",
    )(uc, ul, w, bias.reshape(1, D), mod, ln_g.reshape(1, D), ln_b.reshape(1, D), *x_args)


ML_TM = 512
N_HP = H // 2
SCAN_PAIRS = 4
NG = 8


def _log_sigmoid(g):
    return jnp.minimum(g, 0.0) - jnp.log1p(jnp.exp(-jnp.abs(g)))


def _ml_proj_kernel(mod_ref, w_in_ref, wgt_ref, bgt_ref, *refs):
    x_refs = refs[:-8]
    qt_ref, k_ref, kt_ref, vt_ref, o_ref, gt_ref = refs[-8:-2]
    wt_ref, wo_ref = refs[-2:]
    r0, r1, r2 = QK, QK + 4 * NG, 2 * QK + 4 * NG

    @pl.when(pl.program_id(0) == 0)
    def _():
        wt_ref[0:r0, :] = w_in_ref[:, 0:QK].T.astype(BF)
        wt_ref[r0:r1, :] = wgt_ref[...]
        wt_ref[r1:r2, :] = w_in_ref[:, QK:2 * QK].T.astype(BF)
        wt_ref[r2:, :] = w_in_ref[:, 2 * QK:2 * QK + V].T.astype(BF)
        wo_ref[...] = w_in_ref[:, 2 * QK + V:].astype(BF)

    m = mod_ref[...]
    h = (_read_x(x_refs, ML_TM) * (1.0 + m[1:2]) + m[0:1]).astype(BF)
    tr = _dot_nt(wt_ref[...], h)
    qt = tr[:r0].astype(BF)
    ktf = tr[r1:r2] * (DQK ** -0.5)
    kt = ktf.astype(BF)
    k = ktf.T.astype(BF)
    vt = tr[r2:].astype(BF)
    o = _dot(h, wo_ref[...])
    gt = tr[r0:r1] + bgt_ref[...]
    row = lax.broadcasted_iota(jnp.int32, gt.shape, 0)
    gt = jnp.where((row & 2) != 0, _log_sigmoid(gt), gt)
    qt_ref[...] = qt
    k_ref[...] = k
    kt_ref[...] = kt
    vt_ref[...] = vt
    o_ref[...] = o
    for hp in range(N_HP):
        gt_ref[hp] = gt[hp * NG:(hp + 1) * NG, :]


def _ml_proj(x, mod, layer, w_in, wgt, bgt):
    tm = ML_TM
    res = lambda shape: pl.BlockSpec(shape, lambda t: (0,) * len(shape), pipeline_mode=pl.Buffered(1))
    x_specs, x_args = _x_specs(x, tm)
    return pl.pallas_call(
        _ml_proj_kernel,
        grid=(NTOK // tm,),
        in_specs=[_mod_spec(layer, tm),
                  res((D, 2 * QK + 2 * V)), res((4 * NG, D)), res((4 * NG, 1))] + x_specs,
        out_specs=[pl.BlockSpec((QK, tm), lambda t: (0, t)),
                   pl.BlockSpec((tm, QK), lambda t: (t, 0)),
                   pl.BlockSpec((QK, tm), lambda t: (0, t)),
                   pl.BlockSpec((V, tm), lambda t: (0, t)),
                   pl.BlockSpec((tm, V), lambda t: (t, 0)),
                   pl.BlockSpec((N_HP, NG, tm), lambda t: (0, 0, t))],
        out_shape=[jax.ShapeDtypeStruct((QK, NTOK), BF),
                   jax.ShapeDtypeStruct((NTOK, QK), BF),
                   jax.ShapeDtypeStruct((QK, NTOK), BF),
                   jax.ShapeDtypeStruct((V, NTOK), BF),
                   jax.ShapeDtypeStruct((NTOK, V), F32),
                   jax.ShapeDtypeStruct((N_HP, NG, NTOK), F32)],
        scratch_shapes=[pltpu.VMEM((2 * QK + 4 * NG + V, D), BF),
                        pltpu.VMEM((D, V), BF)],
        compiler_params=_cparams(1),
        name="mlstm_proj",
    )(mod, w_in, wgt, bgt, *x_args)


DHA = DHV + SUBLANES


def _scan_chunk(k_c, kt_c, qt_c, vta, rr_col, rr_row, b_row, m, cta, d, need_state):
    L = k_c.shape[0]
    si = lax.broadcasted_iota(jnp.int32, (L, L), 0)
    ti = lax.broadcasted_iota(jnp.int32, (L, L), 1)
    mask = (si <= ti) if d == 0 else (si >= ti)
    rrm = jnp.where(mask, rr_col, -jnp.inf)
    g = jnp.maximum(jnp.max(rrm, axis=0, keepdims=True), m)
    st = _dot(k_c, qt_c) * jnp.exp(rrm - g)
    num = _dot(vta[:DHV], st.astype(BF))
    den = jnp.sum(st, axis=0, keepdims=True)
    if cta is not None:
        a = jnp.exp(m - g)
        qc = _dot(cta.astype(BF), qt_c)
        num = num + a * qc[:DHV]
        den = den + a * qc[DHV:DHV + 1]
    mt = b_row + g
    ht = num * (1.0 / jnp.maximum(jnp.abs(den), jnp.exp(-mt)))
    if not need_state:
        return ht, None, None
    last = L - 1 if d == 0 else 0
    m_new = mt[:, last:last + 1]
    b_last = b_row[:, last:last + 1]
    w = jnp.exp(b_last + rr_row - m_new)
    kw = (kt_c.astype(F32) * w).astype(BF)
    if cta is None:
        return ht, (_dot_nt(kw, vta[:DHV]), _dot_nt(vta[DHV:], kw)), m_new
    c_new = jnp.exp(b_last + m - m_new) * cta + _dot_nt(vta, kw)
    return ht, c_new, m_new


def _scan_kernel(*refs, is_ctx):
    if is_ctx:
        qt_ref, k_ref, kt_ref, vt_ref, o_ref, gt_ref, ng_ref, u_ref, cst_ref, nmst_ref = refs
    else:
        qt_ref, k_ref, kt_ref, vt_ref, o_ref, gt_ref, ng_ref, c0_ref, m0_ref, u_ref = refs
    L = CHUNK
    nch = DEC_SEQ // L
    ri = lax.broadcasted_iota(jnp.int32, (L, L), 0)
    ci = lax.broadcasted_iota(jnp.int32, (L, L), 1)
    lower = jnp.where(ci <= ri, 1.0, 0.0).astype(BF)
    upper = jnp.where(ci >= ri, 1.0, 0.0).astype(BF)
    sub = lax.broadcasted_iota(jnp.int32, (DHA - DHV, L), 0)
    ones_rows = jnp.where(sub == 0, 1.0, 0.0).astype(BF)

    rr_cols, rr_rows, b_rows = [], [], []
    nr = nch * NG
    pad = jnp.zeros((LANES - 2 * nr, L), F32)
    for pp in range(SCAN_PAIRS):
        gt_all = jnp.concatenate([gt_ref[pp, :, c * L:(c + 1) * L] for c in range(nch)], axis=0)
        tparts = _split3(gt_all)
        b_row_all = (sum(_dot(p, upper) for p in tparts), sum(_dot(p, lower) for p in tparts))
        rr_row_all = [gt_all - pltpu.roll(b, nr - 2, 0) for b in b_row_all]
        rr_col_all = jnp.concatenate(rr_row_all + [pad], axis=0).T
        rr_cols.append([[rr_col_all[:, d * nr + c * NG:d * nr + (c + 1) * NG] for d in range(2)]
                        for c in range(nch)])
        rr_rows.append([[r[c * NG:(c + 1) * NG, :] for r in rr_row_all] for c in range(nch)])
        b_rows.append([[b[c * NG:(c + 1) * NG, :] for b in b_row_all] for c in range(nch)])

    states, us = [], []
    for hl in range(2 * SCAN_PAIRS):
        pp, hh = divmod(hl, 2)
        hsum = [None] * nch
        for d in range(2):
            ig_i, lf_i = d * 4 + hh, d * 4 + 2 + hh
            if is_ctx:
                m, cta = jnp.zeros((1, 1), F32), None
            else:
                m = m0_ref[d, hl][:, 0:1]
                cta = c0_ref[d, hl]
            order = range(nch) if d == 0 else range(nch - 1, -1, -1)
            for n, c in enumerate(order):
                sl = slice(c * L, (c + 1) * L)
                hs = slice(hl * DQK, (hl + 1) * DQK)
                vta = jnp.concatenate([vt_ref[hl * DHV:(hl + 1) * DHV, sl], ones_rows], axis=0)
                need_state = is_ctx or n < nch - 1
                hcur, c_new, m_new = _scan_chunk(
                    k_ref[sl, hs], kt_ref[hs, sl], qt_ref[hs, sl], vta,
                    rr_cols[pp][c][d][:, ig_i:ig_i + 1], rr_rows[pp][c][d][ig_i:ig_i + 1, :],
                    b_rows[pp][c][d][lf_i:lf_i + 1, :], m, cta, d, need_state)
                hsum[c] = hcur if d == 0 else hsum[c] + hcur
                if is_ctx:
                    states.append(((c, d, hl), c_new, m_new))
                else:
                    m, cta = m_new, c_new

        for c in range(nch):
            sl = slice(c * L, (c + 1) * L)
            mu = jnp.mean(hsum[c], axis=0, keepdims=True)
            hc = hsum[c] - mu
            var = jnp.mean(hc * hc, axis=0, keepdims=True)
            hn = hc * lax.rsqrt(var + LN_EPS) * ng_ref[hl * DHV:(hl + 1) * DHV, :]
            og = jax.nn.sigmoid(o_ref[sl, hl * DHV:(hl + 1) * DHV])
            us.append(((sl, slice(hl * DHV, (hl + 1) * DHV)), (hn.T * og).astype(BF)))

    for idx, u in us:
        u_ref[idx] = u
    for idx, (c_fin, n_fin), m_fin in states:
        cst_ref[idx] = c_fin
        nmst_ref[idx] = jnp.concatenate([n_fin, jnp.broadcast_to(m_fin, (SUBLANES, LANES - DQK))], axis=1)


def _ml_scan(qt, k, kt, vt, o, gt, norm_g, c0aug=None, m0b=None):
    is_ctx = c0aug is None
    tt = DEC_SEQ
    toff = 0 if is_ctx else NCTX // tt
    nt = (NCTX if is_ctx else NLAT) // tt
    nh = 2 * SCAN_PAIRS
    in_specs = [pl.BlockSpec((nh * DQK, tt), lambda t, p: (p, toff + t)),
                pl.BlockSpec((tt, nh * DQK), lambda t, p: (toff + t, p)),
                pl.BlockSpec((nh * DQK, tt), lambda t, p: (p, toff + t)),
                pl.BlockSpec((nh * DHV, tt), lambda t, p: (p, toff + t)),
                pl.BlockSpec((tt, nh * DHV), lambda t, p: (toff + t, p)),
                pl.BlockSpec((SCAN_PAIRS, NG, tt), lambda t, p: (p, 0, toff + t)),
                pl.BlockSpec((nh * DHV, 1), lambda t, p: (p, 0))]
    args = [qt, k, kt, vt, o, gt, norm_g.reshape(V, 1)]
    u_spec = pl.BlockSpec((tt, nh * DHV), lambda t, p: (t, p))
    u_shape = jax.ShapeDtypeStruct((nt * tt, V), BF)
    if is_ctx:
        nseq = tt // SEQ
        out_specs = [u_spec,
                     pl.BlockSpec((nseq, 2, nh, DQK, DHV), lambda t, p: (t, 0, p, 0, 0)),
                     pl.BlockSpec((nseq, 2, nh, SUBLANES, LANES), lambda t, p: (t, 0, p, 0, 0))]
        out_shape = [u_shape,
                     jax.ShapeDtypeStruct((BATCH, 2, H, DQK, DHV), F32),
                     jax.ShapeDtypeStruct((BATCH, 2, H, SUBLANES, LANES), F32)]
    else:
        in_specs += [pl.BlockSpec((None, 2, nh, DHA, DQK), lambda t, p: (t, 0, p, 0, 0)),
                     pl.BlockSpec((None, 2, nh, 1, LANES), lambda t, p: (t, 0, p, 0, 0))]
        args += [c0aug, m0b]
        out_specs = [u_spec]
        out_shape = [u_shape]
    return pl.pallas_call(
        functools.partial(_scan_kernel, is_ctx=is_ctx),
        grid=(nt, N_HP // SCAN_PAIRS),
        in_specs=in_specs,
        out_specs=out_specs,
        out_shape=out_shape,
        compiler_params=_cparams(2),
        name="mlstm_scan_ctx" if is_ctx else "mlstm_scan_lat",
    )(*args)


SC_TM = 512
SC_HALO = SUBLANES


def _sconv_kernel(x_ref, prev_ref, next_ref, mod_ref, w_in_f32_ref, cw_ref, w_out_f32_ref, g_ref, b_ref, o_ref,
                  e_ref, w_in_ref, w_out_ref):
    tm = SC_TM

    @pl.when(pl.program_id(0) == 0)
    def _():
        w_in_ref[...] = w_in_f32_ref[...].astype(BF)
        w_out_ref[...] = w_out_f32_ref[...].astype(BF)

    row0 = pl.program_id(0) * tm
    slen = _seq_len(row0)
    pos = (row0 + lax.broadcasted_iota(jnp.int32, (tm, 1), 0)) & (slen - 1)
    m = mod_ref[...]
    x = x_ref[...]
    xe = jnp.concatenate([prev_ref[...], x, next_ref[...]], axis=0)
    he = (xe * (1.0 + m[1:2]) + m[0:1]).astype(BF)
    cu = _dot(he, w_in_ref[:, D:2 * D]) * _dot(he, w_in_ref[:, 2 * D:3 * D])
    bg = _dot(he[SC_HALO:SC_HALO + tm], w_in_ref[:, 0:D])
    e_ref[...] = cu
    prev = jnp.where(pos == 0, 0.0, e_ref[SC_HALO - 1:SC_HALO - 1 + tm, :])
    nxt = jnp.where(pos == slen - 1, 0.0, e_ref[SC_HALO + 1:SC_HALO + 1 + tm, :])
    conv = cw_ref[0:1, :] * prev + cw_ref[1:2, :] * e_ref[SC_HALO:SC_HALO + tm, :] + cw_ref[2:3, :] * nxt
    y = _dot((bg * conv).astype(BF), w_out_ref[...])
    o_ref[...] = _ln(ALPHA * x + m[2:3] * y, g_ref[...], b_ref[...])


def _sconv(x, mod, layer, w_in, conv_w, w_out, ln_g, ln_b):
    tm = SC_TM
    hb = tm // SC_HALO
    nhb = NTOK // SC_HALO
    return pl.pallas_call(
        _sconv_kernel,
        grid=(NTOK // tm,),
        in_specs=[pl.BlockSpec((tm, D), lambda t: (t, 0)),
                  pl.BlockSpec((SC_HALO, D), lambda t: (jnp.maximum(t * hb - 1, 0), 0)),
                  pl.BlockSpec((SC_HALO, D), lambda t: (jnp.minimum((t + 1) * hb, nhb - 1), 0)),
                  _mod_spec(layer, tm),
                  pl.BlockSpec((D, 3 * D), lambda t: (0, 0), pipeline_mode=pl.Buffered(1)),
                  _full((3, D)),
                  pl.BlockSpec((D, D), lambda t: (0, 0), pipeline_mode=pl.Buffered(1)),
                  _full((1, D)), _full((1, D))],
        out_specs=pl.BlockSpec((tm, D), lambda t: (t, 0)),
        out_shape=jax.ShapeDtypeStruct((NTOK, D), F32),
        scratch_shapes=[pltpu.VMEM((tm + 2 * SC_HALO, D), F32),
                        pltpu.VMEM((D, 3 * D), BF),
                        pltpu.VMEM((D, D), BF)],
        compiler_params=_cparams(1),
        name="sconv",
    )(x, x, x, mod, w_in, conv_w, w_out, ln_g.reshape(1, D), ln_b.reshape(1, D))


PL_SEG = 256
PL_NSEG = 4
PL_TM = PL_SEG * PL_NSEG
PL_HALO = 16
PL_STRIDE = PL_SEG + 2 * PL_HALO


def _pool_kernel(x_ref, prev_ref, next_ref, mod_ref, w_ref, pb_ref, ps_ref, g_ref, b_ref, o_ref,
                 e_ref, s2_ref, s4_ref, s8_ref):
    tm, seg, hl, gw = PL_TM, PL_SEG, PL_HALO, GROUP_W
    row0 = pl.program_id(0) * tm
    slen = _seq_len(row0)
    m = mod_ref[...]
    x = x_ref[...]
    h = x * (1.0 + m[1:2]) + m[0:1]
    hp = prev_ref[...] * (1.0 + m[1:2]) + m[0:1]
    hn = next_ref[...] * (1.0 + m[1:2]) + m[0:1]
    n0 = PL_NSEG * PL_STRIDE
    for i in range(PL_NSEG):
        base = i * PL_STRIDE
        pos0 = (row0 + i * seg) & (slen - 1)
        before = hp if i == 0 else h[i * seg - hl:i * seg]
        after = hn if i == PL_NSEG - 1 else h[(i + 1) * seg:(i + 1) * seg + hl]
        e_ref[base:base + hl, :] = jnp.where(pos0 != 0, before, 0.0)
        e_ref[base + hl:base + hl + seg, :] = h[i * seg:(i + 1) * seg]
        e_ref[base + hl + seg:base + PL_STRIDE, :] = jnp.where(pos0 + seg != slen, after, 0.0)
    e_ref[n0:, :] = jnp.zeros((SUBLANES, D), F32)
    n2, n4, n8 = n0, n0 - SUBLANES, n0 - 2 * SUBLANES
    s2_ref[...] = e_ref[0:n2, :] + e_ref[1:n2 + 1, :]
    s4_ref[...] = s2_ref[0:n4, gw:] + s2_ref[2:n4 + 2, gw:]
    s8_ref[...] = s4_ref[0:n8, gw:] + s4_ref[4:n8 + 4, gw:]

    def rows(ref, off, lanes):
        return jnp.concatenate([ref[i * PL_STRIDE + off:i * PL_STRIDE + off + seg, lanes]
                                for i in range(PL_NSEG)], axis=0)

    tots = [rows(s2_ref, hl - 1, slice(0, gw)),
            rows(s4_ref, hl - 2, slice(0, gw)),
            rows(s8_ref, hl - 4, slice(0, gw)),
            rows(s8_ref, hl - 8, slice(gw, 2 * gw)) + rows(s8_ref, hl, slice(gw, 2 * gw))]
    pos = (row0 + lax.broadcasted_iota(jnp.int32, (tm, 1), 0)) & (slen - 1)
    ys = []
    for gi, win in enumerate(POOL_WINDOWS):
        cs = slice(gi * gw, (gi + 1) * gw)
        back = win // 2
        cnt = jnp.minimum(pos + (win - back), slen) - jnp.maximum(pos - back, 0)
        p = tots[gi] / cnt.astype(F32) - h[:, cs]
        ys.append(_dot(p.astype(BF), w_ref[gi]))
    y = (jnp.concatenate(ys, axis=1) + pb_ref[...]) * ps_ref[...]
    o_ref[...] = _ln(ALPHA * x + m[2:3] * y, g_ref[...], b_ref[...])


def _pool(x, mod, layer, w, pb, ps, ln_g, ln_b):
    assert POOL_WINDOWS == (2, 4, 8, 16) and PL_HALO >= POOL_WINDOWS[-1] // 2
    assert SEQ % PL_SEG == 0 and DEC_SEQ % PL_SEG == 0
    tm = PL_TM
    hb = tm // PL_HALO
    nhb = NTOK // PL_HALO
    n0 = PL_NSEG * PL_STRIDE
    return pl.pallas_call(
        _pool_kernel,
        grid=(NTOK // tm,),
        in_specs=[pl.BlockSpec((tm, D), lambda t: (t, 0)),
                  pl.BlockSpec((PL_HALO, D), lambda t: (jnp.maximum(t * hb - 1, 0), 0)),
                  pl.BlockSpec((PL_HALO, D), lambda t: (jnp.minimum((t + 1) * hb, nhb - 1), 0)),
                  _mod_spec(layer, tm),
                  _full((N_GROUPS, GROUP_W, GROUP_W)),
                  _full((1, D)), _full((1, D)), _full((1, D)), _full((1, D))],
        out_specs=pl.BlockSpec((tm, D), lambda t: (t, 0)),
        out_shape=jax.ShapeDtypeStruct((NTOK, D), F32),
        scratch_shapes=[pltpu.VMEM((n0 + SUBLANES, D), F32),
                        pltpu.VMEM((n0, D), F32),
                        pltpu.VMEM((n0 - SUBLANES, D - GROUP_W), F32),
                        pltpu.VMEM((n0 - 2 * SUBLANES, D - 2 * GROUP_W), F32)],
        compiler_params=_cparams(1),
        name="pool",
    )(x, x, x, mod, w, pb.reshape(1, D), ps.reshape(1, D), ln_g.reshape(1, D), ln_b.reshape(1, D))


FT_TM = 1024
FT_TK = 256
FT_FOLD_MIN_SEQ = 512
FT_PAD = SUBLANES


def _dft_mats(n, scale):
    k = np.arange(n, dtype=np.int64)
    ang = 2.0 * np.pi * ((k[:, None] * k[None, :]) % n).astype(np.float64) / n
    return np.stack([np.cos(ang) * scale, np.sin(ang) * scale]).astype(np.float32)


def _dot_x3(a_parts, b_parts):
    (ah, al), (bh, bl) = a_parts, b_parts
    return _dot(ah, bh) + (_dot(ah, bl) + _dot(al, bh))


def _rev_rows(x):
    r = lax.broadcasted_iota(jnp.int32, (SUBLANES, x.shape[1]), 0)
    out = []
    for i in range(x.shape[0] // SUBLANES - 1, -1, -1):
        p = x[i * SUBLANES:(i + 1) * SUBLANES]
        for sh in (1, 2, 4):
            p = jnp.where((r & sh) == 0, pltpu.roll(p, SUBLANES - sh, 0), pltpu.roll(p, sh, 0))
        out.append(p)
    return jnp.concatenate(out, axis=0)


def _fourier_tile(x_ref, m, cs_ref, ccp, scp, w_ref, bias_ref, lg_ref, lb_ref, o_ref, s, tk):
    nsq = FT_TM // s
    ri = lax.broadcasted_iota(jnp.int32, (tk, tk), 0)
    ci = lax.broadcasted_iota(jnp.int32, (tk, tk), 1)
    anti = jnp.where(ri + ci == tk - 1, 1.0, 0.0).astype(BF)
    blocks = []
    half = s // 2
    row = lax.broadcasted_iota(jnp.int32, (half, 1), 0)
    for sq in range(nsq):
        base = sq * s
        h = x_ref[base:base + s, :] * (1.0 + m[1:2]) + m[0:1]
        if s >= FT_FOLD_MIN_SEQ:
            hr = jnp.where(row == 0, 0.0, pltpu.roll(_rev_rows(h[half:]), 1, 0))
            hp = _split2(h[:half] + hr)
            hm = _split2(h[:half] - hr)
            h_mid = h[half:half + 1]
        else:
            hparts = _split2(h)
        for j in range(s // (2 * tk)):
            if s >= FT_FOLD_MIN_SEQ:
                ac = _dot_x3(_split2(cs_ref[0, j, :, 0:half]), hp) + cs_ref[0, j, :, half:half + 1] * h_mid
                as_ = _dot_x3(_split2(cs_ref[1, j, :, 0:half]), hm)
            else:
                ac = _dot_x3(_split2(cs_ref[0, j]), hparts)
                as_ = _dot_x3(_split2(cs_ref[1, j]), hparts)
            direct, mirror = [], []
            for gi in range(N_GROUPS):
                cs = slice(gi * GROUP_W, (gi + 1) * GROUP_W)
                p = _dot_x3(_split2(ac[:, cs]), ccp)
                q = _dot_x3(_split2(as_[:, cs]), scp)
                direct.append((p - q)[0:tk])
                mirror.append((p + q)[1:tk + 1])
            blocks.append((base + j * tk, jnp.concatenate(direct, axis=1).astype(BF)))
            mir = jnp.concatenate(mirror, axis=1).astype(BF)
            blocks.append((base + s - (j + 1) * tk, _dot(anti, mir).astype(BF)))
    outs = []
    for r0, blk in blocks:
        y = _dot(blk, w_ref[...]) + bias_ref[...]
        outs.append((r0, _ln(ALPHA * x_ref[r0:r0 + tk, :] + m[2:3] * y, lg_ref[...], lb_ref[...])))
    for r0, out in outs:
        o_ref[r0:r0 + tk, :] = out


def _fourier_kernel(x_ref, mod_ref, csc_ref, csl_ref, cc_ref, w_f32_ref, bias_ref, lg_ref, lb_ref, o_ref, w_ref):
    @pl.when(pl.program_id(0) == 0)
    def _():
        w_ref[...] = w_f32_ref[...].astype(BF)

    m = mod_ref[...]
    ccp = _split2(cc_ref[0])
    scp = _split2(cc_ref[1])
    is_ctx = pl.program_id(0) * FT_TM < NCTX

    @pl.when(is_ctx)
    def _():
        _fourier_tile(x_ref, m, csc_ref, ccp, scp, w_ref, bias_ref, lg_ref, lb_ref, o_ref, SEQ, _ft_tk(SEQ))

    @pl.when(jnp.logical_not(is_ctx))
    def _():
        _fourier_tile(x_ref, m, csl_ref, ccp, scp, w_ref, bias_ref, lg_ref, lb_ref, o_ref, DEC_SEQ, _ft_tk(DEC_SEQ))


def _ft_tk(s):
    return min(FT_TK, s // 2)


def _half_dft_mats(s, tk):
    full = _dft_mats(s, s ** -0.5)
    rows = (np.arange(s // (2 * tk))[:, None] * tk + np.arange(tk + FT_PAD)[None, :]) % s
    return full[:, rows, :]


def _fourier(x, mod, layer, w, bias, ln_g, ln_b):
    tm = FT_TM
    assert tm % SEQ == 0 and tm % DEC_SEQ == 0 and NCTX % tm == 0
    csc = jnp.asarray(_half_dft_mats(SEQ, _ft_tk(SEQ)))
    csl = jnp.asarray(_half_dft_mats(DEC_SEQ, _ft_tk(DEC_SEQ)))
    cc = jnp.asarray(_dft_mats(GROUP_W, GROUP_W ** -0.5))
    const = lambda shape: pl.BlockSpec(shape, lambda b: (0,) * len(shape), pipeline_mode=pl.Buffered(1))
    return pl.pallas_call(
        _fourier_kernel,
        grid=(NTOK // tm,),
        in_specs=[pl.BlockSpec((tm, D), lambda b: (b, 0)),
                  _mod_spec(layer, tm),
                  const(csc.shape), const(csl.shape), const(cc.shape), const((D, D)),
                  _full((1, D)), _full((1, D)), _full((1, D))],
        out_specs=pl.BlockSpec((tm, D), lambda b: (b, 0)),
        out_shape=jax.ShapeDtypeStruct((NTOK, D), F32),
        scratch_shapes=[pltpu.VMEM((D, D), BF)],
        compiler_params=_cparams(1),
        name="fourier",
    )(x, mod, csc, csl, cc, w, bias.reshape(1, D), ln_g.reshape(1, D), ln_b.reshape(1, D))


def _pos_embed():
    rows = DEC_SEQ // GRID_W
    quarter = D // 4
    omega = 1.0 / (10000.0 ** (np.arange(quarter, dtype=np.float64) / quarter))
    rr, cc = np.meshgrid(np.arange(rows, dtype=np.float64), np.arange(GRID_W, dtype=np.float64), indexing="ij")
    er = rr.reshape(-1, 1) * omega
    ec = cc.reshape(-1, 1) * omega
    return np.concatenate([np.sin(er), np.cos(er), np.sin(ec), np.cos(ec)], axis=-1).astype(np.float32)


def _gate_perm():
    perm = np.zeros(4 * H, dtype=np.int32)
    for hp in range(N_HP):
        for d in range(2):
            for j in range(2):
                for hh in range(2):
                    perm[hp * NG + d * 4 + j * 2 + hh] = d * 2 * H + j * H + 2 * hp + hh
    return perm


def kernel(x_prompt, x_sample, state_C, state_n, state_m, c, c_ctx, w_mod, b_mod, ln_g, ln_b, mlp_w1, mlp_w2,
           ml_w_in, ml_w_gate, ml_b_gate, ml_norm_g, ml_w_out, sc_w_in, sc_conv_w, sc_w_out,
           pl_w, pl_b, pl_scale, ft_w_out, ft_b_out):
    cond = jnp.concatenate([c_ctx[None, :], c, jnp.zeros((NCOND - 1 - DEC_BATCH, D), F32)], axis=0)
    mod = _modulation(cond, w_mod, b_mod)

    x = (x_prompt.reshape(NCTX, D), x_sample.reshape(NLAT, D), jnp.asarray(_pos_embed()))

    new_c = new_n = new_m = None
    for i in range(DEPTH):
        kind, j = i % 4, i // 4
        if kind == 0:
            perm = _gate_perm()
            wgt = ml_w_gate[j][:, perm].astype(BF).T
            qt, k, kt, vt, o, gt = _ml_proj(x, mod, i, ml_w_in[j], wgt, ml_b_gate[j][perm].reshape(-1, 1))
            scan_args = (qt, k, kt, vt, o, gt, ml_norm_g[j])
            uc, cst, nmst = _ml_scan(*scan_args)
            c0 = jnp.concatenate([jnp.swapaxes(state_C[:, j], -1, -2), state_n[:, j][..., None, :],
                                  jnp.zeros((DEC_BATCH, 2, H, DHA - DHV - 1, DQK), F32)], axis=-2)
            m0 = jnp.broadcast_to(state_m[:, j][..., None, None], (DEC_BATCH, 2, H, 1, LANES))
            (ul,) = _ml_scan(*scan_args, c0, m0)
            x = _out_proj(uc, ul, ml_w_out[j], jnp.zeros((D,), F32), x, mod, i, ln_g[i, 0], ln_b[i, 0])
            new_c = cst[:, None]
            new_n = nmst[..., 0, :DQK][:, None]
            new_m = nmst[..., 0, DQK][:, None]
        elif kind == 1:
            x = _sconv(x, mod, i, sc_w_in[j], sc_conv_w[j], sc_w_out[j],
                       ln_g[i, 0], ln_b[i, 0])
        elif kind == 2:
            x = _pool(x, mod, i, pl_w[j].astype(BF), pl_b[j], pl_scale[j], ln_g[i, 0], ln_b[i, 0])
        else:
            x = _fourier(x, mod, i, ft_w_out[j], ft_b_out[j], ln_g[i, 0], ln_b[i, 0])
        x = _mlp(x, mod, i, mlp_w1, mlp_w2, ln_g[i, 1], ln_b[i, 1],
                 split_out=(i == DEPTH - 1))

    y_prompt = x[0].reshape(BATCH, SEQ, D)
    y_sample = x[1].reshape(DEC_BATCH, DEC_SEQ, D)
    return y_prompt, y_sample, new_c, new_n, new_m
```

```python
import functools

import numpy as np
import jax
import jax.numpy as jnp
from jax import lax
from jax.experimental import pallas as pl
from jax.experimental.pallas import tpu as pltpu

D = 1024
BATCH, SEQ = 16, 256
DEC_BATCH, DEC_SEQ = 4, 1024
DEPTH = 4
GRID_W = 64
H = 8
DQK = 64
DHV = 128
QK = H * DQK
V = H * DHV
N_GROUPS = 4
GROUP_W = D // N_GROUPS
POOL_WINDOWS = (2, 4, 8, 16)
D_FF = 4 * D
ALPHA = (2.0 * DEPTH) ** 0.25
LN_EPS = 1e-5
F32 = jnp.float32
BF = jnp.bfloat16

NCTX = BATCH * SEQ
NLAT = DEC_BATCH * DEC_SEQ
NTOK = NCTX + NLAT
NCOND = 8
CHUNK = 256
SUBLANES, LANES = 8, 128
V7X_VMEM_BYTES = 64 * 1024 * 1024
VMEM_LIMIT = V7X_VMEM_BYTES - 8 * 1024 * 1024


def _cparams(n_axes):
    return pltpu.CompilerParams(dimension_semantics=("arbitrary",) * n_axes,
                                vmem_limit_bytes=VMEM_LIMIT)


def _mod_row(row0):
    return jnp.where(row0 < NCTX, 0, 1 + (row0 - NCTX) // DEC_SEQ)


def _seq_len(row0):
    return jnp.where(row0 < NCTX, SEQ, DEC_SEQ)


def _ln(z, g, b):
    mu = jnp.mean(z, axis=-1, keepdims=True)
    zc = z - mu
    var = jnp.mean(zc * zc, axis=-1, keepdims=True)
    return zc * lax.rsqrt(var + LN_EPS) * g + b


def _dot(a, b):
    return jnp.dot(a, b, preferred_element_type=F32)


def _dot_nt(a, b):
    return lax.dot_general(a, b, (((1,), (1,)), ((), ())), preferred_element_type=F32)


def _split2(x):
    hi = x.astype(BF)
    lo = (x - hi.astype(F32)).astype(BF)
    return hi, lo


def _split3(x):
    hi = x.astype(BF)
    r = x - hi.astype(F32)
    mid = r.astype(BF)
    lo = (r - mid.astype(F32)).astype(BF)
    return hi, mid, lo


def _full(shape):
    n = len(shape)
    return pl.BlockSpec(shape, lambda *_: (0,) * n)


def _x_specs(x, tm):
    if not isinstance(x, tuple):
        return [pl.BlockSpec((tm, D), lambda t: (t, 0))], [x]
    nc = NCTX // tm
    return ([pl.BlockSpec((tm, D), lambda t: (jnp.minimum(t, nc - 1), 0)),
             pl.BlockSpec((tm, D), lambda t: (jnp.maximum(t - nc, 0), 0)),
             pl.BlockSpec((tm, D), lambda t: (t % (DEC_SEQ // tm), 0))], list(x))


def _read_x(x_refs, tm):
    if len(x_refs) == 1:
        return x_refs[0][...]
    xc_ref, xl_ref, pe_ref = x_refs
    return jnp.where(pl.program_id(0) * tm < NCTX, xc_ref[...], xl_ref[...] + pe_ref[...])


def _mod_spec(layer, tm):
    return pl.BlockSpec((None, None, 6, D), lambda t: (layer, _mod_row(t * tm), 0, 0))


def _mod_kernel(c_ref, w_ref, b_ref, o_ref):
    c = c_ref[...]
    s = c * jax.nn.sigmoid(c)
    o_ref[...] = _dot(s.astype(BF), w_ref[...].astype(BF)) + b_ref[...]


def _modulation(cond, w_mod, b_mod):
    tn = 3 * D
    out = pl.pallas_call(
        _mod_kernel,
        grid=(DEPTH, 6 * D // tn),
        in_specs=[pl.BlockSpec((NCOND, D), lambda i, j: (0, 0)),
                  pl.BlockSpec((None, D, tn), lambda i, j: (i, 0, j)),
                  pl.BlockSpec((None, 1, tn), lambda i, j: (i, 0, j))],
        out_specs=pl.BlockSpec((None, NCOND, tn), lambda i, j: (i, 0, j)),
        out_shape=jax.ShapeDtypeStruct((DEPTH, NCOND, 6 * D), F32),
        compiler_params=_cparams(2),
        name="modulation",
    )(cond, w_mod, b_mod.reshape(DEPTH, 1, 6 * D))
    return out.reshape(DEPTH, NCOND, 6, D)


MLP_TM = 512
MLP_FC = 512
MLP_NF = D_FF // MLP_FC
MLP_NT = NTOK // MLP_TM


def _mlp_tile(s):
    return jnp.maximum(s - (MLP_NF - 1), 0)


def _mlp_kernel(mod_ref, w1_ref, w2_ref, g_ref, b_ref, x1_ref, x_ref, *refs):
    o_refs, (w1s_ref, w2s_ref, acc_ref, h_ref) = refs[:-4], refs[-4:]
    s = pl.program_id(0)
    m = mod_ref[...]

    def pre(x):
        return (x * (1.0 + m[4:5]) + m[3:4]).astype(BF)

    def hidden(h, w1c):
        return jnp.square(jnp.maximum(_dot(h, w1c), 0.0)).astype(BF)

    def result(x, acc):
        return _ln(ALPHA * x + m[5:6] * acc, g_ref[...], b_ref[...])

    @pl.when(s < MLP_NF)
    def _():
        @pl.when(s == 0)
        def _():
            h_ref[0:MLP_TM, :] = pre(x_ref[...])
            h_ref[MLP_TM:, :] = pre(x1_ref[...])

        w1c = w1_ref[...].astype(BF)
        w2c = w2_ref[...].astype(BF)
        part = _dot(hidden(h_ref[...], w1c), w2c)
        w1s_ref[s] = w1c
        w2s_ref[s] = w2c

        @pl.when(s == 0)
        def _():
            acc_ref[...] = part

        @pl.when(s > 0)
        def _():
            acc_ref[...] += part

        @pl.when(s == MLP_NF - 1)
        def _():
            o_refs[0][...] = result(x_ref[...], acc_ref[0:MLP_TM, :])

    @pl.when(s == MLP_NF)
    def _():
        o_refs[0][...] = result(x1_ref[...], acc_ref[MLP_TM:, :])

    @pl.when(s > MLP_NF)
    def _():
        x = x_ref[...]
        h = pre(x)
        acc = jnp.zeros((MLP_TM, D), F32)
        for f in range(MLP_NF):
            acc = acc + _dot(hidden(h, w1s_ref[f]), w2s_ref[f])
        res = result(x, acc)
        if len(o_refs) == 1:
            o_refs[0][...] = res
        else:
            is_ctx = _mlp_tile(s) * MLP_TM < NCTX

            @pl.when(is_ctx)
            def _():
                o_refs[0][...] = res

            @pl.when(jnp.logical_not(is_ctx))
            def _():
                o_refs[1][...] = res


def _mlp(x, mod, layer, w1, w2, ln_g, ln_b, split_out=False):
    tm = MLP_TM
    nc = NCTX // tm
    assert nc >= 2 and MLP_NT >= 2
    if split_out:
        out_specs = [pl.BlockSpec((tm, D), lambda s: (jnp.minimum(_mlp_tile(s), nc - 1), 0)),
                     pl.BlockSpec((tm, D), lambda s: (jnp.maximum(_mlp_tile(s) - nc, 0), 0))]
        out_shape = [jax.ShapeDtypeStruct((NCTX, D), F32), jax.ShapeDtypeStruct((NLAT, D), F32)]
    else:
        out_specs = pl.BlockSpec((tm, D), lambda s: (_mlp_tile(s), 0))
        out_shape = jax.ShapeDtypeStruct((NTOK, D), F32)
    return pl.pallas_call(
        _mlp_kernel,
        grid=(MLP_NF - 1 + MLP_NT,),
        in_specs=[pl.BlockSpec((None, None, 6, D), lambda s: (layer, _mod_row(_mlp_tile(s) * tm), 0, 0)),
                  pl.BlockSpec((None, D, MLP_FC), lambda s: (layer, 0, jnp.minimum(s, MLP_NF - 1))),
                  pl.BlockSpec((None, MLP_FC, D), lambda s: (layer, jnp.minimum(s, MLP_NF - 1), 0)),
                  _full((1, D)), _full((1, D)),
                  pl.BlockSpec((tm, D), lambda s: (1, 0)),
                  pl.BlockSpec((tm, D), lambda s: (_mlp_tile(s), 0))],
        out_specs=out_specs,
        out_shape=out_shape,
        scratch_shapes=[pltpu.VMEM((MLP_NF, D, MLP_FC), BF),
                        pltpu.VMEM((MLP_NF, MLP_FC, D), BF),
                        pltpu.VMEM((2 * tm, D), F32),
                        pltpu.VMEM((2 * tm, D), BF)],
        compiler_params=_cparams(1),
        name="mlp",
    )(mod, w1, w2, ln_g.reshape(1, D), ln_b.reshape(1, D), x, x)


OUT_TM = 1024
OUT_SUB = 256


def _out_kernel(uc_ref, ul_ref, w_f32_ref, bias_ref, mod_ref, g_ref, b_ref, *refs):
    x_refs, o_ref, w_ref = refs[:-2], refs[-2], refs[-1]

    @pl.when(pl.program_id(0) == 0)
    def _():
        w_ref[...] = w_f32_ref[...].astype(BF)

    is_ctx = pl.program_id(0) * OUT_TM < NCTX
    m = mod_ref[...]
    outs = []
    for r in range(0, OUT_TM, OUT_SUB):
        rs = slice(r, r + OUT_SUB)
        u = jnp.where(is_ctx, uc_ref[rs, :], ul_ref[rs, :])
        y = _dot(u, w_ref[...]) + bias_ref[...]
        if len(x_refs) == 1:
            x = x_refs[0][rs, :]
        else:
            x = jnp.where(is_ctx, x_refs[0][rs, :], x_refs[1][rs, :] + x_refs[2][rs, :])
        outs.append(_ln(ALPHA * x + m[2:3] * y, g_ref[...], b_ref[...]))
    for r, out in zip(range(0, OUT_TM, OUT_SUB), outs):
        o_ref[r:r + OUT_SUB, :] = out


def _out_proj(uc, ul, w, bias, x, mod, layer, ln_g, ln_b):
    tm = OUT_TM
    nc = NCTX // tm
    k = w.shape[0]
    x_specs, x_args = _x_specs(x, tm)
    return pl.pallas_call(
        _out_kernel,
        grid=(NTOK // tm,),
        in_specs=[pl.BlockSpec((tm, k), lambda t: (jnp.minimum(t, nc - 1), 0)),
                  pl.BlockSpec((tm, k), lambda t: (jnp.maximum(t - nc, 0), 0)),
                  pl.BlockSpec((k, D), lambda t: (0, 0), pipeline_mode=pl.Buffered(1)),
                  _full((1, D)),
                  _mod_spec(layer, tm),
                  _full((1, D)), _full((1, D))] + x_specs,
        out_specs=pl.BlockSpec((tm, D), lambda t: (t, 0)),
        out_shape=jax.ShapeDtypeStruct((NTOK, D), F32),
        scratch_shapes=[pltpu.VMEM((k, D), BF)],
        compiler_params=_cparams(1),
        name="out_proj",
    )(uc, ul, w, bias.reshape(1, D), mod, ln_g.reshape(1, D), ln_b.reshape(1, D), *x_args)


ML_TM = 512
N_HP = H // 2
SCAN_PAIRS = 4
NG = 8


def _log_sigmoid(g):
    return jnp.minimum(g, 0.0) - jnp.log1p(jnp.exp(-jnp.abs(g)))


def _ml_proj_kernel(mod_ref, w_in_ref, wgt_ref, bgt_ref, *refs):
    x_refs = refs[:-8]
    qt_ref, k_ref, kt_ref, vt_ref, o_ref, gt_ref = refs[-8:-2]
    wt_ref, wo_ref = refs[-2:]
    r0, r1, r2 = QK, QK + 4 * NG, 2 * QK + 4 * NG

    @pl.when(pl.program_id(0) == 0)
    def _():
        wt_ref[0:r0, :] = w_in_ref[:, 0:QK].T.astype(BF)
        wt_ref[r0:r1, :] = wgt_ref[...]
        wt_ref[r1:r2, :] = w_in_ref[:, QK:2 * QK].T.astype(BF)
        wt_ref[r2:, :] = w_in_ref[:, 2 * QK:2 * QK + V].T.astype(BF)
        wo_ref[...] = w_in_ref[:, 2 * QK + V:].astype(BF)

    m = mod_ref[...]
    h = (_read_x(x_refs, ML_TM) * (1.0 + m[1:2]) + m[0:1]).astype(BF)
    tr = _dot_nt(wt_ref[...], h)
    qt = tr[:r0].astype(BF)
    ktf = tr[r1:r2] * (DQK ** -0.5)
    kt = ktf.astype(BF)
    k = ktf.T.astype(BF)
    vt = tr[r2:].astype(BF)
    o = _dot(h, wo_ref[...])
    gt = tr[r0:r1] + bgt_ref[...]
    row = lax.broadcasted_iota(jnp.int32, gt.shape, 0)
    gt = jnp.where((row & 2) != 0, _log_sigmoid(gt), gt)
    qt_ref[...] = qt
    k_ref[...] = k
    kt_ref[...] = kt
    vt_ref[...] = vt
    o_ref[...] = o
    for hp in range(N_HP):
        gt_ref[hp] = gt[hp * NG:(hp + 1) * NG, :]


def _ml_proj(x, mod, layer, w_in, wgt, bgt):
    tm = ML_TM
    res = lambda shape: pl.BlockSpec(shape, lambda t: (0,) * len(shape), pipeline_mode=pl.Buffered(1))
    x_specs, x_args = _x_specs(x, tm)
    return pl.pallas_call(
        _ml_proj_kernel,
        grid=(NTOK // tm,),
        in_specs=[_mod_spec(layer, tm),
                  res((D, 2 * QK + 2 * V)), res((4 * NG, D)), res((4 * NG, 1))] + x_specs,
        out_specs=[pl.BlockSpec((QK, tm), lambda t: (0, t)),
                   pl.BlockSpec((tm, QK), lambda t: (t, 0)),
                   pl.BlockSpec((QK, tm), lambda t: (0, t)),
                   pl.BlockSpec((V, tm), lambda t: (0, t)),
                   pl.BlockSpec((tm, V), lambda t: (t, 0)),
                   pl.BlockSpec((N_HP, NG, tm), lambda t: (0, 0, t))],
        out_shape=[jax.ShapeDtypeStruct((QK, NTOK), BF),
                   jax.ShapeDtypeStruct((NTOK, QK), BF),
                   jax.ShapeDtypeStruct((QK, NTOK), BF),
                   jax.ShapeDtypeStruct((V, NTOK), BF),
                   jax.ShapeDtypeStruct((NTOK, V), F32),
                   jax.ShapeDtypeStruct((N_HP, NG, NTOK), F32)],
        scratch_shapes=[pltpu.VMEM((2 * QK + 4 * NG + V, D), BF),
                        pltpu.VMEM((D, V), BF)],
        compiler_params=_cparams(1),
        name="mlstm_proj",
    )(mod, w_in, wgt, bgt, *x_args)


DHA = DHV + SUBLANES


def _scan_chunk(k_c, kt_c, qt_c, vta, rr_col, rr_row, b_row, m, cta, d, need_state):
    L = k_c.shape[0]
    si = lax.broadcasted_iota(jnp.int32, (L, L), 0)
    ti = lax.broadcasted_iota(jnp.int32, (L, L), 1)
    mask = (si <= ti) if d == 0 else (si >= ti)
    rrm = jnp.where(mask, rr_col, -jnp.inf)
    g = jnp.maximum(jnp.max(rrm, axis=0, keepdims=True), m)
    st = _dot(k_c, qt_c) * jnp.exp(rrm - g)
    num = _dot(vta[:DHV], st.astype(BF))
    den = jnp.sum(st, axis=0, keepdims=True)
    if cta is not None:
        a = jnp.exp(m - g)
        qc = _dot(cta.astype(BF), qt_c)
        num = num + a * qc[:DHV]
        den = den + a * qc[DHV:DHV + 1]
    mt = b_row + g
    ht = num * (1.0 / jnp.maximum(jnp.abs(den), jnp.exp(-mt)))
    if not need_state:
        return ht, None, None
    last = L - 1 if d == 0 else 0
    m_new = mt[:, last:last + 1]
    b_last = b_row[:, last:last + 1]
    w = jnp.exp(b_last + rr_row - m_new)
    kw = (kt_c.astype(F32) * w).astype(BF)
    if cta is None:
        return ht, (_dot_nt(kw, vta[:DHV]), _dot_nt(vta[DHV:], kw)), m_new
    c_new = jnp.exp(b_last + m - m_new) * cta + _dot_nt(vta, kw)
    return ht, c_new, m_new


def _scan_kernel(*refs, is_ctx):
    if is_ctx:
        qt_ref, k_ref, kt_ref, vt_ref, o_ref, gt_ref, ng_ref, u_ref, cst_ref, nmst_ref = refs
    else:
        qt_ref, k_ref, kt_ref, vt_ref, o_ref, gt_ref, ng_ref, c0_ref, m0_ref, u_ref = refs
    L = CHUNK
    nch = DEC_SEQ // L
    ri = lax.broadcasted_iota(jnp.int32, (L, L), 0)
    ci = lax.broadcasted_iota(jnp.int32, (L, L), 1)
    lower = jnp.where(ci <= ri, 1.0, 0.0).astype(BF)
    upper = jnp.where(ci >= ri, 1.0, 0.0).astype(BF)
    sub = lax.broadcasted_iota(jnp.int32, (DHA - DHV, L), 0)
    ones_rows = jnp.where(sub == 0, 1.0, 0.0).astype(BF)

    rr_cols, rr_rows, b_rows = [], [], []
    nr = nch * NG
    pad = jnp.zeros((LANES - 2 * nr, L), F32)
    for pp in range(SCAN_PAIRS):
        gt_all = jnp.concatenate([gt_ref[pp, :, c * L:(c + 1) * L] for c in range(nch)], axis=0)
        tparts = _split3(gt_all)
        b_row_all = (sum(_dot(p, upper) for p in tparts), sum(_dot(p, lower) for p in tparts))
        rr_row_all = [gt_all - pltpu.roll(b, nr - 2, 0) for b in b_row_all]
        rr_col_all = jnp.concatenate(rr_row_all + [pad], axis=0).T
        rr_cols.append([[rr_col_all[:, d * nr + c * NG:d * nr + (c + 1) * NG] for d in range(2)]
                        for c in range(nch)])
        rr_rows.append([[r[c * NG:(c + 1) * NG, :] for r in rr_row_all] for c in range(nch)])
        b_rows.append([[b[c * NG:(c + 1) * NG, :] for b in b_row_all] for c in range(nch)])

    states, us = [], []
    for hl in range(2 * SCAN_PAIRS):
        pp, hh = divmod(hl, 2)
        hsum = [None] * nch
        for d in range(2):
            ig_i, lf_i = d * 4 + hh, d * 4 + 2 + hh
            if is_ctx:
                m, cta = jnp.zeros((1, 1), F32), None
            else:
                m = m0_ref[d, hl][:, 0:1]
                cta = c0_ref[d, hl]
            order = range(nch) if d == 0 else range(nch - 1, -1, -1)
            for n, c in enumerate(order):
                sl = slice(c * L, (c + 1) * L)
                hs = slice(hl * DQK, (hl + 1) * DQK)
                vta = jnp.concatenate([vt_ref[hl * DHV:(hl + 1) * DHV, sl], ones_rows], axis=0)
                need_state = is_ctx or n < nch - 1
                hcur, c_new, m_new = _scan_chunk(
                    k_ref[sl, hs], kt_ref[hs, sl], qt_ref[hs, sl], vta,
                    rr_cols[pp][c][d][:, ig_i:ig_i + 1], rr_rows[pp][c][d][ig_i:ig_i + 1, :],
                    b_rows[pp][c][d][lf_i:lf_i + 1, :], m, cta, d, need_state)
                hsum[c] = hcur if d == 0 else hsum[c] + hcur
                if is_ctx:
                    states.append(((c, d, hl), c_new, m_new))
                else:
                    m, cta = m_new, c_new

        for c in range(nch):
            sl = slice(c * L, (c + 1) * L)
            mu = jnp.mean(hsum[c], axis=0, keepdims=True)
            hc = hsum[c] - mu
            var = jnp.mean(hc * hc, axis=0, keepdims=True)
            hn = hc * lax.rsqrt(var + LN_EPS) * ng_ref[hl * DHV:(hl + 1) * DHV, :]
            og = jax.nn.sigmoid(o_ref[sl, hl * DHV:(hl + 1) * DHV])
            us.append(((sl, slice(hl * DHV, (hl + 1) * DHV)), (hn.T * og).astype(BF)))

    for idx, u in us:
        u_ref[idx] = u
    for idx, (c_fin, n_fin), m_fin in states:
        cst_ref[idx] = c_fin
        nmst_ref[idx] = jnp.concatenate([n_fin, jnp.broadcast_to(m_fin, (SUBLANES, LANES - DQK))], axis=1)


def _ml_scan(qt, k, kt, vt, o, gt, norm_g, c0aug=None, m0b=None):
    is_ctx = c0aug is None
    tt = DEC_SEQ
    toff = 0 if is_ctx else NCTX // tt
    nt = (NCTX if is_ctx else NLAT) // tt
    nh = 2 * SCAN_PAIRS
    in_specs = [pl.BlockSpec((nh * DQK, tt), lambda t, p: (p, toff + t)),
                pl.BlockSpec((tt, nh * DQK), lambda t, p: (toff + t, p)),
                pl.BlockSpec((nh * DQK, tt), lambda t, p: (p, toff + t)),
                pl.BlockSpec((nh * DHV, tt), lambda t, p: (p, toff + t)),
                pl.BlockSpec((tt, nh * DHV), lambda t, p: (toff + t, p)),
                pl.BlockSpec((SCAN_PAIRS, NG, tt), lambda t, p: (p, 0, toff + t)),
                pl.BlockSpec((nh * DHV, 1), lambda t, p: (p, 0))]
    args = [qt, k, kt, vt, o, gt, norm_g.reshape(V, 1)]
    u_spec = pl.BlockSpec((tt, nh * DHV), lambda t, p: (t, p))
    u_shape = jax.ShapeDtypeStruct((nt * tt, V), BF)
    if is_ctx:
        nseq = tt // SEQ
        out_specs = [u_spec,
                     pl.BlockSpec((nseq, 2, nh, DQK, DHV), lambda t, p: (t, 0, p, 0, 0)),
                     pl.BlockSpec((nseq, 2, nh, SUBLANES, LANES), lambda t, p: (t, 0, p, 0, 0))]
        out_shape = [u_shape,
                     jax.ShapeDtypeStruct((BATCH, 2, H, DQK, DHV), F32),
                     jax.ShapeDtypeStruct((BATCH, 2, H, SUBLANES, LANES), F32)]
    else:
        in_specs += [pl.BlockSpec((None, 2, nh, DHA, DQK), lambda t, p: (t, 0, p, 0, 0)),
                     pl.BlockSpec((None, 2, nh, 1, LANES), lambda t, p: (t, 0, p, 0, 0))]
        args += [c0aug, m0b]
        out_specs = [u_spec]
        out_shape = [u_shape]
    return pl.pallas_call(
        functools.partial(_scan_kernel, is_ctx=is_ctx),
        grid=(nt, N_HP // SCAN_PAIRS),
        in_specs=in_specs,
        out_specs=out_specs,
        out_shape=out_shape,
        compiler_params=_cparams(2),
        name="mlstm_scan_ctx" if is_ctx else "mlstm_scan_lat",
    )(*args)


SC_TM = 1024
SC_HALO = SUBLANES


def _sconv_kernel(x_ref, prev_ref, next_ref, mod_ref, w_in_f32_ref, cw_ref, w_out_f32_ref, g_ref, b_ref, o_ref,
                  e_ref, w_in_ref, w_out_ref):
    tm = SC_TM

    @pl.when(pl.program_id(0) == 0)
    def _():
        w_in_ref[...] = w_in_f32_ref[...].astype(BF)
        w_out_ref[...] = w_out_f32_ref[...].astype(BF)

    row0 = pl.program_id(0) * tm
    slen = _seq_len(row0)
    pos = (row0 + lax.broadcasted_iota(jnp.int32, (tm, 1), 0)) & (slen - 1)
    m = mod_ref[...]
    x = x_ref[...]
    xe = jnp.concatenate([prev_ref[...], x, next_ref[...]], axis=0)
    he = (xe * (1.0 + m[1:2]) + m[0:1]).astype(BF)
    cu = _dot(he, w_in_ref[:, D:2 * D]) * _dot(he, w_in_ref[:, 2 * D:3 * D])
    bg = _dot(he[SC_HALO:SC_HALO + tm], w_in_ref[:, 0:D])
    e_ref[...] = cu
    prev = jnp.where(pos == 0, 0.0, e_ref[SC_HALO - 1:SC_HALO - 1 + tm, :])
    nxt = jnp.where(pos == slen - 1, 0.0, e_ref[SC_HALO + 1:SC_HALO + 1 + tm, :])
    conv = cw_ref[0:1, :] * prev + cw_ref[1:2, :] * e_ref[SC_HALO:SC_HALO + tm, :] + cw_ref[2:3, :] * nxt
    y = _dot((bg * conv).astype(BF), w_out_ref[...])
    o_ref[...] = _ln(ALPHA * x + m[2:3] * y, g_ref[...], b_ref[...])


def _sconv(x, mod, layer, w_in, conv_w, w_out, ln_g, ln_b):
    tm = SC_TM
    hb = tm // SC_HALO
    nhb = NTOK // SC_HALO
    return pl.pallas_call(
        _sconv_kernel,
        grid=(NTOK // tm,),
        in_specs=[pl.BlockSpec((tm, D), lambda t: (t, 0)),
                  pl.BlockSpec((SC_HALO, D), lambda t: (jnp.maximum(t * hb - 1, 0), 0)),
                  pl.BlockSpec((SC_HALO, D), lambda t: (jnp.minimum((t + 1) * hb, nhb - 1), 0)),
                  _mod_spec(layer, tm),
                  pl.BlockSpec((D, 3 * D), lambda t: (0, 0), pipeline_mode=pl.Buffered(1)),
                  _full((3, D)),
                  pl.BlockSpec((D, D), lambda t: (0, 0), pipeline_mode=pl.Buffered(1)),
                  _full((1, D)), _full((1, D))],
        out_specs=pl.BlockSpec((tm, D), lambda t: (t, 0)),
        out_shape=jax.ShapeDtypeStruct((NTOK, D), F32),
        scratch_shapes=[pltpu.VMEM((tm + 2 * SC_HALO, D), F32),
                        pltpu.VMEM((D, 3 * D), BF),
                        pltpu.VMEM((D, D), BF)],
        compiler_params=_cparams(1),
        name="sconv",
    )(x, x, x, mod, w_in, conv_w, w_out, ln_g.reshape(1, D), ln_b.reshape(1, D))


PL_SEG = 256
PL_NSEG = 4
PL_TM = PL_SEG * PL_NSEG
PL_HALO = 16
PL_STRIDE = PL_SEG + 2 * PL_HALO


def _pool_kernel(x_ref, prev_ref, next_ref, mod_ref, w_ref, pb_ref, ps_ref, g_ref, b_ref, o_ref,
                 e_ref, s2_ref, s4_ref, s8_ref):
    tm, seg, hl, gw = PL_TM, PL_SEG, PL_HALO, GROUP_W
    row0 = pl.program_id(0) * tm
    slen = _seq_len(row0)
    m = mod_ref[...]
    x = x_ref[...]
    h = x * (1.0 + m[1:2]) + m[0:1]
    hp = prev_ref[...] * (1.0 + m[1:2]) + m[0:1]
    hn = next_ref[...] * (1.0 + m[1:2]) + m[0:1]
    n0 = PL_NSEG * PL_STRIDE
    for i in range(PL_NSEG):
        base = i * PL_STRIDE
        pos0 = (row0 + i * seg) & (slen - 1)
        before = hp if i == 0 else h[i * seg - hl:i * seg]
        after = hn if i == PL_NSEG - 1 else h[(i + 1) * seg:(i + 1) * seg + hl]
        e_ref[base:base + hl, :] = jnp.where(pos0 != 0, before, 0.0)
        e_ref[base + hl:base + hl + seg, :] = h[i * seg:(i + 1) * seg]
        e_ref[base + hl + seg:base + PL_STRIDE, :] = jnp.where(pos0 + seg != slen, after, 0.0)
    e_ref[n0:, :] = jnp.zeros((SUBLANES, D), F32)
    n2, n4, n8 = n0, n0 - SUBLANES, n0 - 2 * SUBLANES
    s2_ref[...] = e_ref[0:n2, :] + e_ref[1:n2 + 1, :]
    s4_ref[...] = s2_ref[0:n4, gw:] + s2_ref[2:n4 + 2, gw:]
    s8_ref[...] = s4_ref[0:n8, gw:] + s4_ref[4:n8 + 4, gw:]

    def rows(ref, off, lanes):
        return jnp.concatenate([ref[i * PL_STRIDE + off:i * PL_STRIDE + off + seg, lanes]
                                for i in range(PL_NSEG)], axis=0)

    tots = [rows(s2_ref, hl - 1, slice(0, gw)),
            rows(s4_ref, hl - 2, slice(0, gw)),
            rows(s8_ref, hl - 4, slice(0, gw)),
            rows(s8_ref, hl - 8, slice(gw, 2 * gw)) + rows(s8_ref, hl, slice(gw, 2 * gw))]
    pos = (row0 + lax.broadcasted_iota(jnp.int32, (tm, 1), 0)) & (slen - 1)
    ys = []
    for gi, win in enumerate(POOL_WINDOWS):
        cs = slice(gi * gw, (gi + 1) * gw)
        back = win // 2
        cnt = jnp.minimum(pos + (win - back), slen) - jnp.maximum(pos - back, 0)
        p = tots[gi] / cnt.astype(F32) - h[:, cs]
        ys.append(_dot(p.astype(BF), w_ref[gi]))
    y = (jnp.concatenate(ys, axis=1) + pb_ref[...]) * ps_ref[...]
    o_ref[...] = _ln(ALPHA * x + m[2:3] * y, g_ref[...], b_ref[...])


def _pool(x, mod, layer, w, pb, ps, ln_g, ln_b):
    assert POOL_WINDOWS == (2, 4, 8, 16) and PL_HALO >= POOL_WINDOWS[-1] // 2
    assert SEQ % PL_SEG == 0 and DEC_SEQ % PL_SEG == 0
    tm = PL_TM
    hb = tm // PL_HALO
    nhb = NTOK // PL_HALO
    n0 = PL_NSEG * PL_STRIDE
    return pl.pallas_call(
        _pool_kernel,
        grid=(NTOK // tm,),
        in_specs=[pl.BlockSpec((tm, D), lambda t: (t, 0)),
                  pl.BlockSpec((PL_HALO, D), lambda t: (jnp.maximum(t * hb - 1, 0), 0)),
                  pl.BlockSpec((PL_HALO, D), lambda t: (jnp.minimum((t + 1) * hb, nhb - 1), 0)),
                  _mod_spec(layer, tm),
                  _full((N_GROUPS, GROUP_W, GROUP_W)),
                  _full((1, D)), _full((1, D)), _full((1, D)), _full((1, D))],
        out_specs=pl.BlockSpec((tm, D), lambda t: (t, 0)),
        out_shape=jax.ShapeDtypeStruct((NTOK, D), F32),
        scratch_shapes=[pltpu.VMEM((n0 + SUBLANES, D), F32),
                        pltpu.VMEM((n0, D), F32),
                        pltpu.VMEM((n0 - SUBLANES, D - GROUP_W), F32),
                        pltpu.VMEM((n0 - 2 * SUBLANES, D - 2 * GROUP_W), F32)],
        compiler_params=_cparams(1),
        name="pool",
    )(x, x, x, mod, w, pb.reshape(1, D), ps.reshape(1, D), ln_g.reshape(1, D), ln_b.reshape(1, D))


FT_TM = 1024
FT_TK = 256
FT_FOLD_MIN_SEQ = 512
FT_PAD = SUBLANES


def _dft_mats(n, scale):
    k = np.arange(n, dtype=np.int64)
    ang = 2.0 * np.pi * ((k[:, None] * k[None, :]) % n).astype(np.float64) / n
    return np.stack([np.cos(ang) * scale, np.sin(ang) * scale]).astype(np.float32)


def _dot_x3(a_parts, b_parts):
    (ah, al), (bh, bl) = a_parts, b_parts
    return _dot(ah, bh) + (_dot(ah, bl) + _dot(al, bh))


def _rev_rows(x):
    r = lax.broadcasted_iota(jnp.int32, (SUBLANES, x.shape[1]), 0)
    out = []
    for i in range(x.shape[0] // SUBLANES - 1, -1, -1):
        p = x[i * SUBLANES:(i + 1) * SUBLANES]
        for sh in (1, 2, 4):
            p = jnp.where((r & sh) == 0, pltpu.roll(p, SUBLANES - sh, 0), pltpu.roll(p, sh, 0))
        out.append(p)
    return jnp.concatenate(out, axis=0)


def _fourier_tile(x_ref, m, cs_ref, ccp, scp, w_ref, bias_ref, lg_ref, lb_ref, o_ref, s, tk):
    nsq = FT_TM // s
    ri = lax.broadcasted_iota(jnp.int32, (tk, tk), 0)
    ci = lax.broadcasted_iota(jnp.int32, (tk, tk), 1)
    anti = jnp.where(ri + ci == tk - 1, 1.0, 0.0).astype(BF)
    blocks = []
    half = s // 2
    row = lax.broadcasted_iota(jnp.int32, (half, 1), 0)
    for sq in range(nsq):
        base = sq * s
        h = x_ref[base:base + s, :] * (1.0 + m[1:2]) + m[0:1]
        if s >= FT_FOLD_MIN_SEQ:
            hr = jnp.where(row == 0, 0.0, pltpu.roll(_rev_rows(h[half:]), 1, 0))
            hp = _split2(h[:half] + hr)
            hm = _split2(h[:half] - hr)
            h_mid = h[half:half + 1]
        else:
            hparts = _split2(h)
        for j in range(s // (2 * tk)):
            if s >= FT_FOLD_MIN_SEQ:
                ac = _dot_x3(_split2(cs_ref[0, j, :, 0:half]), hp) + cs_ref[0, j, :, half:half + 1] * h_mid
                as_ = _dot_x3(_split2(cs_ref[1, j, :, 0:half]), hm)
            else:
                ac = _dot_x3(_split2(cs_ref[0, j]), hparts)
                as_ = _dot_x3(_split2(cs_ref[1, j]), hparts)
            direct, mirror = [], []
            for gi in range(N_GROUPS):
                cs = slice(gi * GROUP_W, (gi + 1) * GROUP_W)
                p = _dot_x3(_split2(ac[:, cs]), ccp)
                q = _dot_x3(_split2(as_[:, cs]), scp)
                direct.append((p - q)[0:tk])
                mirror.append((p + q)[1:tk + 1])
            blocks.append((base + j * tk, jnp.concatenate(direct, axis=1).astype(BF)))
            mir = jnp.concatenate(mirror, axis=1).astype(BF)
            blocks.append((base + s - (j + 1) * tk, _dot(anti, mir).astype(BF)))
    outs = []
    for r0, blk in blocks:
        y = _dot(blk, w_ref[...]) + bias_ref[...]
        outs.append((r0, _ln(ALPHA * x_ref[r0:r0 + tk, :] + m[2:3] * y, lg_ref[...], lb_ref[...])))
    for r0, out in outs:
        o_ref[r0:r0 + tk, :] = out


def _fourier_kernel(x_ref, mod_ref, csc_ref, csl_ref, cc_ref, w_f32_ref, bias_ref, lg_ref, lb_ref, o_ref, w_ref):
    @pl.when(pl.program_id(0) == 0)
    def _():
        w_ref[...] = w_f32_ref[...].astype(BF)

    m = mod_ref[...]
    ccp = _split2(cc_ref[0])
    scp = _split2(cc_ref[1])
    is_ctx = pl.program_id(0) * FT_TM < NCTX

    @pl.when(is_ctx)
    def _():
        _fourier_tile(x_ref, m, csc_ref, ccp, scp, w_ref, bias_ref, lg_ref, lb_ref, o_ref, SEQ, _ft_tk(SEQ))

    @pl.when(jnp.logical_not(is_ctx))
    def _():
        _fourier_tile(x_ref, m, csl_ref, ccp, scp, w_ref, bias_ref, lg_ref, lb_ref, o_ref, DEC_SEQ, _ft_tk(DEC_SEQ))


def _ft_tk(s):
    return min(FT_TK, s // 2)


def _half_dft_mats(s, tk):
    full = _dft_mats(s, s ** -0.5)
    rows = (np.arange(s // (2 * tk))[:, None] * tk + np.arange(tk + FT_PAD)[None, :]) % s
    return full[:, rows, :]


def _fourier(x, mod, layer, w, bias, ln_g, ln_b):
    tm = FT_TM
    assert tm % SEQ == 0 and tm % DEC_SEQ == 0 and NCTX % tm == 0
    csc = jnp.asarray(_half_dft_mats(SEQ, _ft_tk(SEQ)))
    csl = jnp.asarray(_half_dft_mats(DEC_SEQ, _ft_tk(DEC_SEQ)))
    cc = jnp.asarray(_dft_mats(GROUP_W, GROUP_W ** -0.5))
    const = lambda shape: pl.BlockSpec(shape, lambda b: (0,) * len(shape), pipeline_mode=pl.Buffered(1))
    return pl.pallas_call(
        _fourier_kernel,
        grid=(NTOK // tm,),
        in_specs=[pl.BlockSpec((tm, D), lambda b: (b, 0)),
                  _mod_spec(layer, tm),
                  const(csc.shape), const(csl.shape), const(cc.shape), const((D, D)),
                  _full((1, D)), _full((1, D)), _full((1, D))],
        out_specs=pl.BlockSpec((tm, D), lambda b: (b, 0)),
        out_shape=jax.ShapeDtypeStruct((NTOK, D), F32),
        scratch_shapes=[pltpu.VMEM((D, D), BF)],
        compiler_params=_cparams(1),
        name="fourier",
    )(x, mod, csc, csl, cc, w, bias.reshape(1, D), ln_g.reshape(1, D), ln_b.reshape(1, D))


def _pos_embed():
    rows = DEC_SEQ // GRID_W
    quarter = D // 4
    omega = 1.0 / (10000.0 ** (np.arange(quarter, dtype=np.float64) / quarter))
    rr, cc = np.meshgrid(np.arange(rows, dtype=np.float64), np.arange(GRID_W, dtype=np.float64), indexing="ij")
    er = rr.reshape(-1, 1) * omega
    ec = cc.reshape(-1, 1) * omega
    return np.concatenate([np.sin(er), np.cos(er), np.sin(ec), np.cos(ec)], axis=-1).astype(np.float32)


def _gate_perm():
    perm = np.zeros(4 * H, dtype=np.int32)
    for hp in range(N_HP):
        for d in range(2):
            for j in range(2):
                for hh in range(2):
                    perm[hp * NG + d * 4 + j * 2 + hh] = d * 2 * H + j * H + 2 * hp + hh
    return perm


def kernel(x_prompt, x_sample, state_C, state_n, state_m, c, c_ctx, w_mod, b_mod, ln_g, ln_b, mlp_w1, mlp_w2,
           ml_w_in, ml_w_gate, ml_b_gate, ml_norm_g, ml_w_out, sc_w_in, sc_conv_w, sc_w_out,
           pl_w, pl_b, pl_scale, ft_w_out, ft_b_out):
    cond = jnp.concatenate([c_ctx[None, :], c, jnp.zeros((NCOND - 1 - DEC_BATCH, D), F32)], axis=0)
    mod = _modulation(cond, w_mod, b_mod)

    x = (x_prompt.reshape(NCTX, D), x_sample.reshape(NLAT, D), jnp.asarray(_pos_embed()))

    new_c = new_n = new_m = None
    for i in range(DEPTH):
        kind, j = i % 4, i // 4
        if kind == 0:
            perm = _gate_perm()
            wgt = ml_w_gate[j][:, perm].astype(BF).T
            qt, k, kt, vt, o, gt = _ml_proj(x, mod, i, ml_w_in[j], wgt, ml_b_gate[j][perm].reshape(-1, 1))
            scan_args = (qt, k, kt, vt, o, gt, ml_norm_g[j])
            uc, cst, nmst = _ml_scan(*scan_args)
            c0 = jnp.concatenate([jnp.swapaxes(state_C[:, j], -1, -2), state_n[:, j][..., None, :],
                                  jnp.zeros((DEC_BATCH, 2, H, DHA - DHV - 1, DQK), F32)], axis=-2)
            m0 = jnp.broadcast_to(state_m[:, j][..., None, None], (DEC_BATCH, 2, H, 1, LANES))
            (ul,) = _ml_scan(*scan_args, c0, m0)
            x = _out_proj(uc, ul, ml_w_out[j], jnp.zeros((D,), F32), x, mod, i, ln_g[i, 0], ln_b[i, 0])
            new_c = cst[:, None]
            new_n = nmst[..., 0, :DQK][:, None]
            new_m = nmst[..., 0, DQK][:, None]
        elif kind == 1:
            x = _sconv(x, mod, i, sc_w_in[j], sc_conv_w[j], sc_w_out[j],
                       ln_g[i, 0], ln_b[i, 0])
        elif kind == 2:
            x = _pool(x, mod, i, pl_w[j].astype(BF), pl_b[j], pl_scale[j], ln_g[i, 0], ln_b[i, 0])
        else:
            x = _fourier(x, mod, i, ft_w_out[j], ft_b_out[j], ln_g[i, 0], ln_b[i, 0])
        x = _mlp(x, mod, i, mlp_w1, mlp_w2, ln_g[i, 1], ln_b[i, 1],
                 split_out=(i == DEPTH - 1))

    y_prompt = x[0].reshape(BATCH, SEQ, D)
    y_sample = x[1].reshape(DEC_BATCH, DEC_SEQ, D)
    return y_prompt, y_sample, new_c, new_n, new_m
```
